```python
import jax
import jax.numpy as jnp
from jax import lax
import numpy as np

D_MODEL = 1024
BATCH = 8
SEQ = 2048
DEPTH = 1

RET_HEADS = 4
RET_DK = 128
RET_DV = 128
RET_CHUNK = 128
ROPE_BASE = 10000.0
MOBA_HEADS = 8
MOBA_DH = 64
MOBA_BLOCK = 256
MOBA_TOPK = 3
MOBA_QCHUNK = 32
RET_QK_WIDTH = RET_HEADS * RET_DK
RET_V_WIDTH = RET_HEADS * RET_DV
MOBA_WIDTH = MOBA_HEADS * MOBA_DH
MIX_WIDTH = RET_V_WIDTH + MOBA_WIDTH
IN_SPLIT_SIZES = (RET_QK_WIDTH, RET_QK_WIDTH, RET_V_WIDTH, RET_V_WIDTH, MOBA_WIDTH, MOBA_WIDTH, MOBA_WIDTH)
IN_COLS = sum(IN_SPLIT_SIZES)
N_EXPERTS = 256
TOP_K = 8
N_GROUPS = 8
TOPK_GROUPS = 4
EXPERT_FF = 256
SHARED_FF = 256
ROUTED_SCALE = 2.5
MOE_BLOCK = 128
N_MOD = 6
EPS = 1e-6

kernel_name = 'hymba_retnet_moba_moe_adaln'


def rms_norm(x, g=None):
    xf = x.astype(jnp.float32)
    y = xf * lax.rsqrt(jnp.mean(xf * xf, axis=-1, keepdims=True) + EPS)
    if g is not None:
        y = y * g.astype(jnp.float32)
    return y.astype(x.dtype)


def rotary(x, pos):
    half = x.shape[-1] // 2
    inv = ROPE_BASE ** (-jnp.arange(half, dtype=jnp.float32) / half)
    ang = pos[:, None] * inv[None, :]
    cos = jnp.cos(ang)[None, :, None, :].astype(x.dtype)
    sin = jnp.sin(ang)[None, :, None, :].astype(x.dtype)
    x1, x2 = x[..., :half], x[..., half:]
    return jnp.concatenate([x1 * cos - x2 * sin, x1 * sin + x2 * cos], axis=-1)


def swiglu(x, wg, wu, wd):
    return (jax.nn.silu(x @ wg) * (x @ wu)) @ wd


def retention(q, k, v, g):
    B_, S_, H, dk = q.shape
    dv = v.shape[-1]
    C = RET_CHUNK
    N = S_ // C
    dt = q.dtype
    pos = jnp.arange(S_, dtype=jnp.float32)
    q = rotary(q, pos)
    k = rotary(k, pos) * (dk ** -0.5)
    log_g = jnp.log1p(-jnp.exp2(-5.0 - jnp.arange(H, dtype=jnp.float32)))
    idx = jnp.arange(C, dtype=jnp.float32)
    diff = idx[:, None] - idx[None, :]
    decay_in = jnp.where(diff >= 0, jnp.exp(log_g[:, None, None] * jnp.maximum(diff, 0.0)), 0.0)
    q_decay = jnp.exp(log_g[:, None] * (idx + 1.0))
    k_decay = jnp.exp(log_g[:, None] * (C - 1.0 - idx))
    chunk_decay = jnp.exp(log_g * C)
    qc = q.reshape(B_, N, C, H, dk).transpose(0, 3, 1, 2, 4)
    kc = k.reshape(B_, N, C, H, dk).transpose(0, 3, 1, 2, 4)
    vc = v.reshape(B_, N, C, H, dv).transpose(0, 3, 1, 2, 4)
    scores = jnp.einsum('bhncd,bhnmd->bhncm', qc, kc) * decay_in[None, :, None].astype(dt)
    inner = jnp.einsum('bhncm,bhnme->bhnce', scores, vc)
    kv = jnp.einsum('bhnmd,bhnme->nbhde', kc * k_decay[None, :, None, :, None].astype(dt), vc)
    cd = chunk_decay[None, :, None, None].astype(dt)

    def step(state, kv_n):
        return state * cd + kv_n, state

    _, prev = lax.scan(step, jnp.zeros((B_, H, dk, dv), dt), kv)
    cross = jnp.einsum('bhncd,nbhde->bhnce', qc * q_decay[None, :, None, :, None].astype(dt), prev)
    o = (inner + cross).transpose(0, 2, 3, 1, 4).reshape(B_, S_, H, dv)
    o = rms_norm(o)
    return (jax.nn.silu(g) * o).reshape(B_, S_, H * dv)


def moba_attention(q, k, v, q_gain, k_gain):
    B_, S_, H, dh = q.shape
    L = MOBA_BLOCK
    NB = -(-S_ // L)
    Sp = NB * L
    pad = Sp - S_
    q = rms_norm(q, q_gain)
    k = rms_norm(k, k_gain)
    padw = ((0, 0), (0, 0), (0, pad), (0, 0))
    q = jnp.pad(q.transpose(0, 2, 1, 3), padw)
    k = jnp.pad(k.transpose(0, 2, 1, 3), padw)
    v = jnp.pad(v.transpose(0, 2, 1, 3), padw)
    kb = k.reshape(B_, H, NB, L, dh)
    vb = v.reshape(B_, H, NB, L, dh)
    k_mean = jnp.mean(kb.astype(jnp.float32), axis=3).astype(k.dtype)
    gate = jnp.einsum('bhsd,bhnd->bhsn', q, k_mean).astype(jnp.float32)
    q_blk = jnp.arange(Sp) // L
    past = jnp.arange(NB)[None, :] < q_blk[:, None]
    gate = jnp.where(past[None, None], gate, -jnp.inf)
    topk = min(MOBA_TOPK, NB)
    _, sel = lax.top_k(gate, topk)
    sel_valid = jnp.arange(topk)[None, :] < jnp.minimum(q_blk, topk)[:, None]
    scale = dh ** -0.5
    QC = MOBA_QCHUNK
    bi = jnp.arange(B_)[:, None, None, None]
    hi = jnp.arange(H)[None, :, None, None]

    def chunk(ci):
        s0 = ci * QC
        qc = lax.dynamic_slice_in_dim(q, s0, QC, axis=2)
        selc = lax.dynamic_slice_in_dim(sel, s0, QC, axis=2)
        validc = lax.dynamic_slice_in_dim(sel_valid, s0, QC, axis=0)
        own = s0 // L
        k_own = lax.dynamic_index_in_dim(kb, own, axis=2, keepdims=False)
        v_own = lax.dynamic_index_in_dim(vb, own, axis=2, keepdims=False)
        kg = kb[bi, hi, selc]
        vg = vb[bi, hi, selc]
        s_sel = jnp.einsum('bhqd,bhqkld->bhqkl', qc, kg).astype(jnp.float32) * scale
        s_sel = jnp.where(validc[None, None, :, :, None], s_sel, -jnp.inf)
        s_own = jnp.einsum('bhqd,bhld->bhql', qc, k_own).astype(jnp.float32) * scale
        qpos = s0 + jnp.arange(QC)
        kpos = own * L + jnp.arange(L)
        s_own = jnp.where((kpos[None, :] <= qpos[:, None])[None, None], s_own, -jnp.inf)
        logits = jnp.concatenate([s_sel.reshape(B_, H, QC, topk * L), s_own], axis=-1)
        p = jax.nn.softmax(logits, axis=-1).astype(v.dtype)
        p_sel = p[..., :topk * L].reshape(B_, H, QC, topk, L)
        p_own = p[..., topk * L:]
        return (jnp.einsum('bhqkl,bhqkld->bhqd', p_sel, vg)
                + jnp.einsum('bhql,bhld->bhqd', p_own, v_own))

    o = lax.map(chunk, jnp.arange(Sp // QC))
    o = o.transpose(1, 2, 0, 3, 4).reshape(B_, H, Sp, dh)[:, :, :S_]
    return o.transpose(0, 2, 1, 3).reshape(B_, S_, H * dh)


def moe_ffn(h, w_router, router_bias, w_gate, w_up, w_down, ws_gate, ws_up, ws_down):
    B_, S_, D = h.shape
    xf = h.reshape(-1, D)
    N = xf.shape[0]
    scores = jax.nn.sigmoid(jnp.einsum('nd,de->ne', xf, w_router).astype(jnp.float32))
    biased = scores + router_bias.astype(jnp.float32)
    grp = biased.reshape(N, N_GROUPS, N_EXPERTS // N_GROUPS)
    grp_score = jnp.sum(lax.top_k(grp, 2)[0], axis=-1)
    _, top_g = lax.top_k(grp_score, TOPK_GROUPS)
    gmask = jnp.any(top_g[:, :, None] == jnp.arange(N_GROUPS)[None, None, :], axis=1)
    emask = jnp.repeat(gmask, N_EXPERTS // N_GROUPS, axis=1)
    choice = jnp.where(emask, biased, -jnp.inf)
    _, top_e = lax.top_k(choice, TOP_K)
    w = jnp.take_along_axis(scores, top_e, axis=-1)
    w = w / jnp.sum(w, axis=-1, keepdims=True) * ROUTED_SCALE
    A = N * TOP_K
    flat_e = top_e.reshape(A)
    flat_t = jnp.repeat(jnp.arange(N, dtype=jnp.int32), TOP_K)
    flat_w = w.reshape(A)
    order = jnp.argsort(flat_e)
    se = flat_e[order]
    counts = jnp.bincount(flat_e, length=N_EXPERTS)
    padded = (counts + MOE_BLOCK - 1) // MOE_BLOCK * MOE_BLOCK
    pad_end = jnp.cumsum(padded)
    pad_start = pad_end - padded
    start = jnp.cumsum(counts) - counts
    dest = pad_start[se] + jnp.arange(A) - start[se]
    n_blocks = -(-A // MOE_BLOCK) + N_EXPERTS
    P = n_blocks * MOE_BLOCK
    buf_t = jnp.full((P,), N, jnp.int32).at[dest].set(flat_t[order])
    buf_w = jnp.zeros((P,), jnp.float32).at[dest].set(flat_w[order])
    blk_e = jnp.minimum(jnp.searchsorted(pad_end, jnp.arange(n_blocks) * MOE_BLOCK, side='right'), N_EXPERTS - 1)
    x_pad = jnp.concatenate([xf, jnp.zeros((1, D), xf.dtype)], axis=0)

    def expert_block(args):
        tok, wt, e = args
        xb = x_pad[tok]
        hid = jax.nn.silu(xb @ w_gate[e]) * (xb @ w_up[e])
        return (hid @ w_down[e]) * wt[:, None].astype(xb.dtype)

    out = lax.map(expert_block, (buf_t.reshape(n_blocks, MOE_BLOCK), buf_w.reshape(n_blocks, MOE_BLOCK), blk_e))
    routed = jax.ops.segment_sum(out.reshape(P, D), buf_t, num_segments=N + 1)[:N]
    shared = swiglu(xf, ws_gate, ws_up, ws_down)
    return (routed + shared).reshape(B_, S_, D)


def setup_inputs(seed: int = 0) -> dict:
    key = jax.random.key(seed)
    ks = jax.random.split(key, 18)
    D = D_MODEL

    def nrm(k, shape, scale):
        return jax.random.normal(k, shape, jnp.float32) * scale

    return {
        'x': nrm(ks[0], (BATCH, SEQ, D), 1.0),
        'c': nrm(ks[1], (BATCH, D), 1.0),
        'w_ada': nrm(ks[2], (DEPTH, D, N_MOD * D), 0.3 * D ** -0.5),
        'b_ada': nrm(ks[3], (DEPTH, N_MOD * D), 0.02),
        'g_mix': 1.0 + nrm(ks[4], (DEPTH, D), 0.02),
        'w_in': nrm(ks[5], (DEPTH, D, IN_COLS), D ** -0.5),
        'q_gain': 1.0 + nrm(ks[6], (DEPTH, MOBA_DH), 0.02),
        'k_gain': 1.0 + nrm(ks[7], (DEPTH, MOBA_DH), 0.02),
        'w_out': nrm(ks[8], (DEPTH, MIX_WIDTH, D), MIX_WIDTH ** -0.5),
        'g_ffn': 1.0 + nrm(ks[9], (DEPTH, D), 0.02),
        'w_router': nrm(ks[10], (DEPTH, D, N_EXPERTS), D ** -0.5),
        'router_bias': nrm(ks[11], (DEPTH, N_EXPERTS), 0.01),
        'w_gate': nrm(ks[12], (DEPTH, N_EXPERTS, D, EXPERT_FF), D ** -0.5),
        'w_up': nrm(ks[13], (DEPTH, N_EXPERTS, D, EXPERT_FF), D ** -0.5),
        'w_down': nrm(ks[14], (DEPTH, N_EXPERTS, EXPERT_FF, D), EXPERT_FF ** -0.5),
        'ws_gate': nrm(ks[15], (DEPTH, D, SHARED_FF), D ** -0.5),
        'ws_up': nrm(ks[16], (DEPTH, D, SHARED_FF), D ** -0.5),
        'ws_down': nrm(ks[17], (DEPTH, SHARED_FF, D), SHARED_FF ** -0.5),
    }


def reference(x, c, w_ada, b_ada, g_mix, w_in, q_gain, k_gain, w_out, g_ffn, w_router, router_bias,
              w_gate, w_up, w_down, ws_gate, ws_up, ws_down):
    B_, S_, D = x.shape
    split_at = np.cumsum(IN_SPLIT_SIZES)[:-1].tolist()
    for l in range(DEPTH):
        mod = jnp.einsum('bd,de->be', jax.nn.silu(c), w_ada[l]) + b_ada[l]
        sh_a, sc_a, gt_a, sh_f, sc_f, gt_f = [m[:, None, :] for m in jnp.split(mod, N_MOD, axis=-1)]
        h = rms_norm(x, g_mix[l]) * (1.0 + sc_a) + sh_a
        proj = jnp.einsum('bsd,dc->bsc', h, w_in[l])
        rq, rk, rv, rg, mq, mk, mv = jnp.split(proj, split_at, axis=-1)
        ret_out = retention(rq.reshape(B_, S_, RET_HEADS, RET_DK), rk.reshape(B_, S_, RET_HEADS, RET_DK),
                            rv.reshape(B_, S_, RET_HEADS, RET_DV), rg.reshape(B_, S_, RET_HEADS, RET_DV))
        moba_out = moba_attention(mq.reshape(B_, S_, MOBA_HEADS, MOBA_DH), mk.reshape(B_, S_, MOBA_HEADS, MOBA_DH),
                                  mv.reshape(B_, S_, MOBA_HEADS, MOBA_DH), q_gain[l], k_gain[l])
        mixed = jnp.einsum('bsc,cd->bsd', jnp.concatenate([ret_out, moba_out], axis=-1), w_out[l])
        x = x + gt_a * mixed
        h = rms_norm(x, g_ffn[l]) * (1.0 + sc_f) + sh_f
        x = x + gt_f * moe_ffn(h, w_router[l], router_bias[l], w_gate[l], w_up[l], w_down[l],
                               ws_gate[l], ws_up[l], ws_down[l])
    return x
```

```python
import functools

import numpy as np
import jax
import jax.numpy as jnp
from jax import lax
from jax.experimental import pallas as pl
from jax.experimental.pallas import tpu as pltpu

F32 = jnp.float32
BF16 = jnp.bfloat16
I32 = jnp.int32

D_MODEL = 1024
RET_HEADS = 4
RET_DK = 128
MOBA_HEADS = 8
MOBA_DH = 64
MOBA_BLOCK = 256
MOBA_TOPK = 3
ROPE_BASE = 10000.0
N_EXPERTS = 256
TOP_K = 8
N_GROUPS = 8
TOPK_GROUPS = 4
GROUP_SIZE = N_EXPERTS // N_GROUPS
EXPERT_FF = 256
ROUTED_SCALE = 2.5
N_MOD = 6
EPS = 1e-6
IN_COLS = 3584

LANES = 128
RET_CHUNK = 256
TM_PROJ = 512
TM_DISPATCH = 256
TM_COMBINE = 128
FFN_BLOCK = 256
VMEM_LIMIT = 56 * 1024 * 1024

NEG_INF = float("-inf")


def _silu(x):
    return x * jax.nn.sigmoid(x)


def _nt_dot(a, b):
    return lax.dot_general(a, b, (((1,), (1,)), ((), ())), preferred_element_type=F32)


def _tn_dot(a, b):
    return lax.dot_general(a, b, (((0,), (0,)), ((), ())), preferred_element_type=F32)


def _dot(a, b):
    return jnp.dot(a, b, preferred_element_type=F32)


def _adaln_kernel(c_ref, w_ref, b_ref, o_ref):
    s = _silu(c_ref[...])
    o_ref[...] = _dot(s.astype(BF16), w_ref[...].astype(BF16)) + b_ref[...]


def _adaln(c, w_ada, b_ada):
    bsz, d = c.shape
    ncol = w_ada.shape[1]
    tn = 1024
    return pl.pallas_call(
        _adaln_kernel,
        out_shape=jax.ShapeDtypeStruct((bsz, ncol), F32),
        grid=(ncol // tn,),
        in_specs=[
            pl.BlockSpec((bsz, d), lambda j: (0, 0)),
            pl.BlockSpec((d, tn), lambda j: (0, j)),
            pl.BlockSpec((1, tn), lambda j: (0, j)),
        ],
        out_specs=pl.BlockSpec((bsz, tn), lambda j: (0, j)),
        compiler_params=pltpu.CompilerParams(vmem_limit_bytes=VMEM_LIMIT),
        name="adaln",
    )(c, w_ada, b_ada.reshape(1, ncol))


def _inproj_kernel(x_ref, mod_ref, g_ref, w_ref, cos_ref, sin_ref, o_ref):
    x = x_ref[...]
    ms = jnp.mean(x * x, axis=-1, keepdims=True)
    m = mod_ref[0]
    h = (x * lax.rsqrt(ms + EPS) * g_ref[...]) * (1.0 + m[1:2]) + m[0:1]
    hb = h.astype(BF16)
    cosf = cos_ref[...]
    sinf = sin_ref[...]
    k_scale = RET_DK ** -0.5
    width = RET_HEADS * RET_DK
    for ci in range(IN_COLS // width):
        acc = _dot(hb, w_ref[:, ci * width:(ci + 1) * width])
        if ci < 2:
            for hh in range(RET_HEADS):
                xh = acc[:, hh * RET_DK:(hh + 1) * RET_DK]
                r = xh * cosf + pltpu.roll(xh, RET_DK // 2, axis=1) * sinf
                if ci == 1:
                    r = r * k_scale
                o_ref[:, ci * width + hh * RET_DK:ci * width + (hh + 1) * RET_DK] = r.astype(BF16)
        else:
            o_ref[:, ci * width:(ci + 1) * width] = acc.astype(BF16)


def _inproj(x2, mod3, g_mix, w_in_bf, cos_full, sin_signed, seq):
    n, d = x2.shape
    tm = TM_PROJ
    tiles_per_seq = seq // tm
    return pl.pallas_call(
        _inproj_kernel,
        out_shape=jax.ShapeDtypeStruct((n, IN_COLS), BF16),
        grid=(n // tm,),
        in_specs=[
            pl.BlockSpec((tm, d), lambda i: (i, 0)),
            pl.BlockSpec((1, N_MOD, d), lambda i: (i // tiles_per_seq, 0, 0)),
            pl.BlockSpec((1, d), lambda i: (0, 0)),
            pl.BlockSpec((d, IN_COLS), lambda i: (0, 0)),
            pl.BlockSpec((tm, LANES), lambda i: (i % tiles_per_seq, 0)),
            pl.BlockSpec((tm, LANES), lambda i: (i % tiles_per_seq, 0)),
        ],
        out_specs=pl.BlockSpec((tm, IN_COLS), lambda i: (i, 0)),
        compiler_params=pltpu.CompilerParams(vmem_limit_bytes=VMEM_LIMIT),
        name="inproj",
    )(x2, mod3, g_mix, w_in_bf, cos_full, sin_signed)


def _ret_kernel(lg_ref, q_ref, k_ref, v_ref, g_ref, o_ref):
    seq = q_ref.shape[1]
    c = RET_CHUNK
    lg = lg_ref[pl.program_id(1)]
    row = lax.broadcasted_iota(I32, (c, c), 0)
    col = lax.broadcasted_iota(I32, (c, c), 1)
    diff = (row - col).astype(F32)
    dmask = jnp.where(diff >= 0, jnp.exp(lg * jnp.maximum(diff, 0.0)), 0.0)
    idx = lax.broadcasted_iota(I32, (c, 1), 0).astype(F32)
    q_decay = jnp.exp(lg * (idx + 1.0))
    k_decay = jnp.exp(lg * (c - 1.0 - idx))
    chunk_decay = jnp.exp(jnp.full((1, 1), lg * c, F32))
    state = jnp.zeros((RET_DK, RET_DK), F32)
    for n in range(seq // c):
        rows = slice(n * c, (n + 1) * c)
        qn = q_ref[0, rows, :]
        kn = k_ref[0, rows, :]
        vn = v_ref[0, rows, :]
        scores = _nt_dot(qn, kn) * dmask
        inner = _dot(scores.astype(BF16), vn)
        qs = (qn.astype(F32) * q_decay).astype(BF16)
        cross = _dot(qs, state.astype(BF16))
        o = inner + cross
        o = o * lax.rsqrt(jnp.mean(o * o, axis=-1, keepdims=True) + EPS)
        gn = g_ref[0, rows, :].astype(F32)
        o_ref[0, rows, :] = (_silu(gn) * o).astype(BF16)
        ks = (kn.astype(F32) * k_decay).astype(BF16)
        state = state * chunk_decay + _tn_dot(ks, vn)


def _retention(log_g, proj3):
    bsz, seq, _ = proj3.shape
    blk = (1, seq, RET_DK)
    return pl.pallas_call(
        _ret_kernel,
        out_shape=jax.ShapeDtypeStruct((bsz, seq, RET_HEADS * RET_DK), BF16),
        grid_spec=pltpu.PrefetchScalarGridSpec(
            num_scalar_prefetch=1,
            grid=(bsz, RET_HEADS),
            in_specs=[
                pl.BlockSpec(blk, lambda b, h, lg: (b, 0, h)),
                pl.BlockSpec(blk, lambda b, h, lg: (b, 0, RET_HEADS + h)),
                pl.BlockSpec(blk, lambda b, h, lg: (b, 0, 2 * RET_HEADS + h)),
                pl.BlockSpec(blk, lambda b, h, lg: (b, 0, 3 * RET_HEADS + h)),
            ],
            out_specs=pl.BlockSpec(blk, lambda b, h, lg: (b, 0, h)),
        ),
        compiler_params=pltpu.CompilerParams(vmem_limit_bytes=VMEM_LIMIT),
        name="retention",
    )(log_g, proj3, proj3, proj3, proj3)


def _moba_kernel(q_ref, k_ref, v_ref, qg_ref, kg_ref, o_ref, qn_s, kn_s):
    seq = q_ref.shape[1]
    lb = MOBA_BLOCK
    nb = seq // lb
    lane = lax.broadcasted_iota(I32, (1, LANES), 1)
    is_a = lane < MOBA_DH
    halves = (is_a, jnp.logical_not(is_a))

    def head_norm(xf, gain):
        sq = xf * xf
        s_a = jnp.sum(jnp.where(is_a, sq, 0.0), axis=-1, keepdims=True)
        s_b = jnp.sum(jnp.where(is_a, 0.0, sq), axis=-1, keepdims=True)
        inv = jnp.where(is_a, lax.rsqrt(s_a / MOBA_DH + EPS), lax.rsqrt(s_b / MOBA_DH + EPS))
        return xf * inv * gain

    qg = qg_ref[...]
    kg = kg_ref[...]
    k_means = []
    for j in range(nb):
        rows = slice(j * lb, (j + 1) * lb)
        kf = head_norm(k_ref[0, rows, :].astype(F32), kg)
        kn_s[rows, :] = kf.astype(BF16)
        k_means.append(jnp.mean(kf, axis=0, keepdims=True))
        qf = head_norm(q_ref[0, rows, :].astype(F32), qg)
        qn_s[rows, :] = (qf * (MOBA_DH ** -0.5)).astype(BF16)
    k_mean = jnp.concatenate(k_means, axis=0)

    r_loc = lax.broadcasted_iota(I32, (lb, lb), 0)
    c_loc = lax.broadcasted_iota(I32, (lb, lb), 1)
    causal = c_loc <= r_loc

    for i in range(nb):
        rows = slice(i * lb, (i + 1) * lb)
        qi = qn_s[rows, :]
        outs = []
        for hx in range(2):
            q_h = jnp.where(halves[hx], qi, jnp.zeros_like(qi))
            sel = [None] * i
            if i > MOBA_TOPK:
                km_h = jnp.where(halves[hx], k_mean, 0.0).astype(BF16)
                gate = _nt_dot(q_h, km_h)
                cols = [gate[:, j:j + 1] for j in range(i)]
                for j in range(i):
                    rank = jnp.zeros((lb, 1), F32)
                    for j2 in range(i):
                        if j2 == j:
                            continue
                        beats = (cols[j2] >= cols[j]) if j2 < j else (cols[j2] > cols[j])
                        rank = rank + jnp.where(beats, 1.0, 0.0)
                    sel[j] = rank < float(MOBA_TOPK)
            pieces = []
            for j in range(i + 1):
                s = _nt_dot(q_h, kn_s[j * lb:(j + 1) * lb, :])
                if j == i:
                    s = jnp.where(causal, s, NEG_INF)
                elif sel[j] is not None:
                    s = jnp.where(sel[j], s, NEG_INF)
                pieces.append(s)
            mx = jnp.max(pieces[0], axis=-1, keepdims=True)
            for s in pieces[1:]:
                mx = jnp.maximum(mx, jnp.max(s, axis=-1, keepdims=True))
            den = jnp.zeros((lb, 1), F32)
            acc = jnp.zeros((lb, LANES), F32)
            for j, s in enumerate(pieces):
                p = jnp.exp(s - mx)
                den = den + jnp.sum(p, axis=-1, keepdims=True)
                acc = acc + _dot(p.astype(BF16), v_ref[0, j * lb:(j + 1) * lb, :])
            outs.append(acc / den)
        o_ref[0, rows, :] = jnp.where(is_a, outs[0], outs[1]).astype(BF16)


def _moba(proj3, qg2, kg2):
    bsz, seq, _ = proj3.shape
    pairs = MOBA_HEADS // 2
    blk = (1, seq, LANES)
    base = 4 * RET_HEADS
    return pl.pallas_call(
        _moba_kernel,
        out_shape=jax.ShapeDtypeStruct((bsz, seq, MOBA_HEADS * MOBA_DH), BF16),
        grid=(bsz, pairs),
        in_specs=[
            pl.BlockSpec(blk, lambda b, p: (b, 0, base + p)),
            pl.BlockSpec(blk, lambda b, p: (b, 0, base + pairs + p)),
            pl.BlockSpec(blk, lambda b, p: (b, 0, base + 2 * pairs + p)),
            pl.BlockSpec((1, LANES), lambda b, p: (0, 0)),
            pl.BlockSpec((1, LANES), lambda b, p: (0, 0)),
        ],
        out_specs=pl.BlockSpec(blk, lambda b, p: (b, 0, p)),
        scratch_shapes=[pltpu.VMEM((seq, LANES), BF16), pltpu.VMEM((seq, LANES), BF16)],
        compiler_params=pltpu.CompilerParams(vmem_limit_bytes=VMEM_LIMIT),
        name="moba",
    )(proj3, proj3, proj3, qg2, kg2)


def _mid_kernel(ret_ref, moba_ref, x_ref, mod_ref, wo1_ref, wo2_ref, g_ref, wr_ref, rb_ref,
                wsg_ref, wsu_ref, wsd_ref,
                xb_ref, h2_ref, e_ref, w_ref, rk_ref, cnt_ref, carry_s):
    i = pl.program_id(0)
    tm = x_ref.shape[0]

    @pl.when(i == 0)
    def _():
        carry_s[...] = jnp.zeros_like(carry_s)

    m = mod_ref[0]
    mixed = _dot(ret_ref[...], wo1_ref[...]) + _dot(moba_ref[...], wo2_ref[...])
    x1 = x_ref[...] + m[2:3] * mixed
    ms = jnp.mean(x1 * x1, axis=-1, keepdims=True)
    h2 = (x1 * lax.rsqrt(ms + EPS) * g_ref[...]) * (1.0 + m[4:5]) + m[3:4]
    h2_ref[...] = h2
    h2b = h2.astype(BF16)

    hid = _silu(_dot(h2b, wsg_ref[...])) * _dot(h2b, wsu_ref[...])
    xb_ref[...] = x1 + m[5:6] * _dot(hid.astype(BF16), wsd_ref[...])

    scores = jax.nn.sigmoid(_nt_dot(wr_ref[...], h2b))
    biased = scores + rb_ref[...]
    grp = biased.reshape(N_GROUPS, GROUP_SIZE, tm)
    gi = lax.broadcasted_iota(I32, (N_GROUPS, GROUP_SIZE, tm), 1).astype(F32)
    top1 = jnp.max(grp, axis=1, keepdims=True)
    first = jnp.min(jnp.where(grp == top1, gi, float(GROUP_SIZE)), axis=1, keepdims=True)
    top2 = jnp.max(jnp.where(gi == first, NEG_INF, grp), axis=1, keepdims=True)
    gscore = (top1 + top2).reshape(N_GROUPS, tm)
    gidx = lax.broadcasted_iota(I32, (N_GROUPS, tm), 0)
    grank = jnp.zeros((N_GROUPS, tm), F32)
    for g2 in range(N_GROUPS):
        rowv = gscore[g2:g2 + 1, :]
        beats = (rowv > gscore) | ((rowv == gscore) & (g2 < gidx))
        grank = grank + jnp.where(beats, 1.0, 0.0)
    gsel = jnp.where(grank < float(TOPK_GROUPS), 1.0, 0.0)
    emask = jnp.broadcast_to(gsel.reshape(N_GROUPS, 1, tm), (N_GROUPS, GROUP_SIZE, tm)).reshape(N_EXPERTS, tm)
    choice = jnp.where(emask > 0.5, biased, NEG_INF)

    eidx = lax.broadcasted_iota(I32, (N_EXPERTS, tm), 0).astype(F32)
    selmask = jnp.zeros((N_EXPERTS, tm), F32)
    e_rows = []
    w_rows = []
    for _k in range(TOP_K):
        mx = jnp.max(choice, axis=0, keepdims=True)
        idx = jnp.min(jnp.where(choice == mx, eidx, float(N_EXPERTS)), axis=0, keepdims=True)
        onehot = eidx == idx
        e_rows.append(idx)
        w_rows.append(jnp.sum(jnp.where(onehot, scores, 0.0), axis=0, keepdims=True))
        selmask = selmask + jnp.where(onehot, 1.0, 0.0)
        choice = jnp.where(onehot, NEG_INF, choice)
    wsum = w_rows[0]
    for wk in w_rows[1:]:
        wsum = wsum + wk

    tr = lax.broadcasted_iota(I32, (tm, tm), 0)
    tc = lax.broadcasted_iota(I32, (tm, tm), 1)
    upper = jnp.where(tr < tc, 1.0, 0.0).astype(BF16)
    prefix = _dot(selmask.astype(BF16), upper) + carry_s[...]
    r_rows = [jnp.sum(jnp.where(eidx == ek, prefix, 0.0), axis=0, keepdims=True) for ek in e_rows]
    carry_s[...] = carry_s[...] + jnp.sum(selmask, axis=1, keepdims=True)

    e_ref[...] = jnp.concatenate(e_rows, axis=0).astype(I32)
    w_ref[...] = jnp.concatenate([wk / wsum * ROUTED_SCALE for wk in w_rows], axis=0)
    rk_ref[...] = jnp.concatenate(r_rows, axis=0).astype(I32)
    cnt_ref[...] = carry_s[...].astype(I32)


def _mid(ret2, moba2, x2, mod3, wo1, wo2, g_ffn, wr_t, rbias, wsg, wsu, wsd, seq):
    n, d = x2.shape
    tm = TM_PROJ
    tiles_per_seq = seq // tm
    half = ret2.shape[1]
    ff = wsg.shape[1]
    const = lambda i: (0, 0)
    row = lambda i: (i, 0)
    colt = lambda i: (0, i)
    return pl.pallas_call(
        _mid_kernel,
        out_shape=(
            jax.ShapeDtypeStruct((n, d), F32),
            jax.ShapeDtypeStruct((n, d), F32),
            jax.ShapeDtypeStruct((TOP_K, n), I32),
            jax.ShapeDtypeStruct((TOP_K, n), F32),
            jax.ShapeDtypeStruct((TOP_K, n), I32),
            jax.ShapeDtypeStruct((N_EXPERTS, 1), I32),
        ),
        grid=(n // tm,),
        in_specs=[
            pl.BlockSpec((tm, half), row),
            pl.BlockSpec((tm, half), row),
            pl.BlockSpec((tm, d), row),
            pl.BlockSpec((1, N_MOD, d), lambda i: (i // tiles_per_seq, 0, 0)),
            pl.BlockSpec((half, d), const),
            pl.BlockSpec((half, d), const),
            pl.BlockSpec((1, d), const),
            pl.BlockSpec((N_EXPERTS, d), const),
            pl.BlockSpec((N_EXPERTS, 1), const),
            pl.BlockSpec((d, ff), const),
            pl.BlockSpec((d, ff), const),
            pl.BlockSpec((ff, d), const),
        ],
        out_specs=(
            pl.BlockSpec((tm, d), row),
            pl.BlockSpec((tm, d), row),
            pl.BlockSpec((TOP_K, tm), colt),
            pl.BlockSpec((TOP_K, tm), colt),
            pl.BlockSpec((TOP_K, tm), colt),
            pl.BlockSpec((N_EXPERTS, 1), const),
        ),
        scratch_shapes=[pltpu.VMEM((N_EXPERTS, 1), F32)],
        compiler_params=pltpu.CompilerParams(
            dimension_semantics=("arbitrary",), vmem_limit_bytes=VMEM_LIMIT),
        name="mid",
    )(ret2, moba2, x2, mod3, wo1, wo2, g_ffn, wr_t, rbias, wsg, wsu, wsd)


def _row_copy(src_ref, src_row, dst_ref, dst_row, sem):
    return pltpu.make_async_copy(src_ref.at[pl.ds(src_row, 1)], dst_ref.at[pl.ds(dst_row, 1)], sem)


def _dispatch_kernel(start_ref, e_ref, rk_ref, h_ref, xs_ref, e_s, rk_s, sem_i, sem_r):
    tm = h_ref.shape[0]
    ce = pltpu.make_async_copy(e_ref, e_s, sem_i.at[0])
    cr = pltpu.make_async_copy(rk_ref, rk_s, sem_i.at[1])
    ce.start()
    cr.start()
    ce.wait()
    cr.wait()

    def issue(t, carry):
        for k in range(TOP_K):
            dst = start_ref[e_s[k, t]] + rk_s[k, t]
            _row_copy(h_ref, t, xs_ref, dst, sem_r).start()
        return carry

    lax.fori_loop(0, tm, issue, 0)

    def drain(t, carry):
        for _k in range(TOP_K):
            _row_copy(h_ref, 0, xs_ref, 0, sem_r).wait()
        return carry

    lax.fori_loop(0, tm, drain, 0)


def _dispatch(start, e_idx, rank, h2):
    n, d = h2.shape
    tm = TM_DISPATCH
    return pl.pallas_call(
        _dispatch_kernel,
        out_shape=jax.ShapeDtypeStruct((n * TOP_K, d), h2.dtype),
        grid_spec=pltpu.PrefetchScalarGridSpec(
            num_scalar_prefetch=1,
            grid=(n // tm,),
            in_specs=[
                pl.BlockSpec((TOP_K, tm), lambda i, s: (0, i)),
                pl.BlockSpec((TOP_K, tm), lambda i, s: (0, i)),
                pl.BlockSpec((tm, d), lambda i, s: (i, 0)),
            ],
            out_specs=pl.BlockSpec(memory_space=pl.ANY),
            scratch_shapes=[
                pltpu.SMEM((TOP_K, tm), I32),
                pltpu.SMEM((TOP_K, tm), I32),
                pltpu.SemaphoreType.DMA((2,)),
                pltpu.SemaphoreType.DMA,
            ],
        ),
        compiler_params=pltpu.CompilerParams(
            dimension_semantics=("arbitrary",), vmem_limit_bytes=VMEM_LIMIT, has_side_effects=True),
        name="dispatch",
    )(start, e_idx, rank, h2)


def _ffn_kernel(blk_ref, exp_ref, lo_ref, hi_ref, x_ref, wg_ref, wu_ref, wd_ref, y_ref):
    i = pl.program_id(0)
    prev = jnp.maximum(i - 1, 0)
    first = jnp.logical_or(i == 0, blk_ref[i] != blk_ref[prev])

    @pl.when(first)
    def _():
        y_ref[...] = jnp.zeros_like(y_ref)

    lo = lo_ref[i]
    hi = hi_ref[i]

    @pl.when(hi > lo)
    def _():
        xb = x_ref[...].astype(BF16)
        hid = _silu(_dot(xb, wg_ref[0].astype(BF16))) * _dot(xb, wu_ref[0].astype(BF16))
        y = _dot(hid.astype(BF16), wd_ref[0].astype(BF16))
        r = lax.broadcasted_iota(I32, (x_ref.shape[0], 1), 0)
        mine = (r >= lo) & (r < hi)
        y_ref[...] = jnp.where(mine, y, y_ref[...])


def _ffn(blk, exp, lo, hi, xs, w_gate, w_up, w_down):
    a, d = xs.shape
    ff = w_gate.shape[2]
    steps = blk.shape[0]
    return pl.pallas_call(
        _ffn_kernel,
        out_shape=jax.ShapeDtypeStruct((a, d), F32),
        grid_spec=pltpu.PrefetchScalarGridSpec(
            num_scalar_prefetch=4,
            grid=(steps,),
            in_specs=[
                pl.BlockSpec((FFN_BLOCK, d), lambda i, b, e, l, h: (b[i], 0)),
                pl.BlockSpec((1, d, ff), lambda i, b, e, l, h: (e[i], 0, 0)),
                pl.BlockSpec((1, d, ff), lambda i, b, e, l, h: (e[i], 0, 0)),
                pl.BlockSpec((1, ff, d), lambda i, b, e, l, h: (e[i], 0, 0)),
            ],
            out_specs=pl.BlockSpec((FFN_BLOCK, d), lambda i, b, e, l, h: (b[i], 0)),
        ),
        compiler_params=pltpu.CompilerParams(
            dimension_semantics=("arbitrary",), vmem_limit_bytes=VMEM_LIMIT),
        name="ffn",
    )(blk, exp, lo, hi, xs, w_gate, w_up, w_down)


def _combine_kernel(start_ref, e_ref, rk_ref, wt_ref, xb_ref, mod_ref, y_ref, o_ref,
                    e_s, rk_s, buf, sem_i, sem_r):
    tm = xb_ref.shape[0]
    ce = pltpu.make_async_copy(e_ref, e_s, sem_i.at[0])
    cr = pltpu.make_async_copy(rk_ref, rk_s, sem_i.at[1])
    ce.start()
    cr.start()
    ce.wait()
    cr.wait()

    def issue(t, carry):
        for k in range(TOP_K):
            src = start_ref[e_s[k, t]] + rk_s[k, t]
            _row_copy(y_ref, src, buf.at[k], t, sem_r).start()
        return carry

    lax.fori_loop(0, tm, issue, 0)

    def drain(t, carry):
        for k in range(TOP_K):
            _row_copy(y_ref, 0, buf.at[k], 0, sem_r).wait()
        return carry

    lax.fori_loop(0, tm, drain, 0)

    wt = wt_ref[...]
    routed = buf[0] * wt[:, 0:1]
    for k in range(1, TOP_K):
        routed = routed + buf[k] * wt[:, k:k + 1]
    o_ref[...] = xb_ref[...] + mod_ref[0][5:6] * routed


def _combine(start, e_idx, rank, w_t, xb, mod3, y, seq):
    n, d = xb.shape
    tm = TM_COMBINE
    tiles_per_seq = seq // tm
    return pl.pallas_call(
        _combine_kernel,
        out_shape=jax.ShapeDtypeStruct((n, d), F32),
        grid_spec=pltpu.PrefetchScalarGridSpec(
            num_scalar_prefetch=1,
            grid=(n // tm,),
            in_specs=[
                pl.BlockSpec((TOP_K, tm), lambda i, s: (0, i)),
                pl.BlockSpec((TOP_K, tm), lambda i, s: (0, i)),
                pl.BlockSpec((tm, TOP_K), lambda i, s: (i, 0)),
                pl.BlockSpec((tm, d), lambda i, s: (i, 0)),
                pl.BlockSpec((1, N_MOD, d), lambda i, s: (i // tiles_per_seq, 0, 0)),
                pl.BlockSpec(memory_space=pl.ANY),
            ],
            out_specs=pl.BlockSpec((tm, d), lambda i, s: (i, 0)),
            scratch_shapes=[
                pltpu.SMEM((TOP_K, tm), I32),
                pltpu.SMEM((TOP_K, tm), I32),
                pltpu.VMEM((TOP_K, tm, d), F32),
                pltpu.SemaphoreType.DMA((2,)),
                pltpu.SemaphoreType.DMA,
            ],
        ),
        compiler_params=pltpu.CompilerParams(
            dimension_semantics=("arbitrary",), vmem_limit_bytes=VMEM_LIMIT),
        name="combine",
    )(start, e_idx, rank, w_t, xb, mod3, y)


def _rotary_tables(seq):
    half = RET_DK // 2
    pos = jnp.arange(seq, dtype=F32)
    inv = ROPE_BASE ** (-jnp.arange(half, dtype=F32) / half)
    ang = pos[:, None] * inv[None, :]
    cos = jnp.cos(ang)
    sin = jnp.sin(ang)
    return jnp.concatenate([cos, cos], axis=-1), jnp.concatenate([-sin, sin], axis=-1)


def _ffn_schedule(counts, total_rows):
    nblk = total_rows // FFN_BLOCK
    ends = jnp.cumsum(counts)
    start = ends - counts
    bp = jnp.sort(jnp.concatenate([jnp.arange(nblk, dtype=I32) * FFN_BLOCK, start]))
    lo = bp
    hi = jnp.concatenate([bp[1:], jnp.full((1,), total_rows, I32)])
    blk = jnp.minimum(lo // FFN_BLOCK, nblk - 1)
    exp = jnp.minimum(jnp.searchsorted(ends, lo, side="right").astype(I32), N_EXPERTS - 1)
    return start, blk, exp, lo - blk * FFN_BLOCK, hi - blk * FFN_BLOCK


def kernel(x, c, w_ada, b_ada, g_mix, w_in, q_gain, k_gain, w_out, g_ffn, w_router, router_bias,
           w_gate, w_up, w_down, ws_gate, ws_up, ws_down):
    bsz, seq, d = x.shape
    n = bsz * seq
    depth = w_ada.shape[0]
    cos_full, sin_signed = _rotary_tables(seq)
    log_g = jnp.log1p(-jnp.exp2(-5.0 - jnp.arange(RET_HEADS, dtype=F32)))
    ret_w = RET_HEADS * RET_DK
    x2 = x.reshape(n, d)
    for l in range(depth):
        mod3 = _adaln(c, w_ada[l], b_ada[l]).reshape(bsz, N_MOD, d)
        proj = _inproj(x2, mod3, g_mix[l].reshape(1, d), w_in[l].astype(BF16), cos_full, sin_signed, seq)
        proj3 = proj.reshape(bsz, seq, IN_COLS)
        ret = _retention(log_g, proj3)
        qg2 = jnp.tile(q_gain[l].reshape(1, MOBA_DH), (1, 2))
        kg2 = jnp.tile(k_gain[l].reshape(1, MOBA_DH), (1, 2))
        moba = _moba(proj3, qg2, kg2)
        wo = w_out[l].astype(BF16)
        xb, h2, e_idx, w_k, rank, cnt = _mid(
            ret.reshape(n, ret_w), moba.reshape(n, MOBA_HEADS * MOBA_DH), x2, mod3,
            wo[:ret_w], wo[ret_w:], g_ffn[l].reshape(1, d),
            w_router[l].T.astype(BF16), router_bias[l].reshape(N_EXPERTS, 1),
            ws_gate[l].astype(BF16), ws_up[l].astype(BF16), ws_down[l].astype(BF16), seq)
        start, blk, exp, lo, hi = _ffn_schedule(cnt[:, 0], n * TOP_K)
        xs = _dispatch(start, e_idx, rank, h2)
        y = _ffn(blk, exp, lo, hi, xs, w_gate[l], w_up[l], w_down[l])
        x2 = _combine(start, e_idx, rank, w_k.T, xb, mod3, y, seq)
    return x2.reshape(bsz, seq, d)
```

```python
import functools

import numpy as np
import jax
import jax.numpy as jnp
from jax import lax
from jax.experimental import pallas as pl
from jax.experimental.pallas import tpu as pltpu
from jax.experimental.pallas import tpu_sc as plsc

F32 = jnp.float32
BF16 = jnp.bfloat16
I32 = jnp.int32

D_MODEL = 1024
RET_HEADS = 4
RET_DK = 128
MOBA_HEADS = 8
MOBA_DH = 64
MOBA_BLOCK = 256
MOBA_TOPK = 3
ROPE_BASE = 10000.0
N_EXPERTS = 256
TOP_K = 8
N_GROUPS = 8
TOPK_GROUPS = 4
GROUP_SIZE = N_EXPERTS // N_GROUPS
EXPERT_FF = 256
ROUTED_SCALE = 2.5
N_MOD = 6
EPS = 1e-6
IN_COLS = 3584

LANES = 128
RET_CHUNK = 256
TM_PROJ = 512
TM_COMBINE = 256
SC_CORES = 2
SC_SUBCORES = 16
SC_CHUNK = 128
SC_GATHER_CHUNK = 64
FFN_BLOCK = 256
VMEM_LIMIT = 56 * 1024 * 1024

NEG_INF = float("-inf")


def _silu(x):
    return x * jax.nn.sigmoid(x)


def _nt_dot(a, b):
    return lax.dot_general(a, b, (((1,), (1,)), ((), ())), preferred_element_type=F32)


def _tn_dot(a, b):
    return lax.dot_general(a, b, (((0,), (0,)), ((), ())), preferred_element_type=F32)


def _dot(a, b):
    return jnp.dot(a, b, preferred_element_type=F32)


HI_MASK = -65536


def _pack_halves(v):
    w = v.shape[1] // 2
    lo = lax.bitcast_convert_type(v[:, :w].astype(BF16).astype(F32), I32)
    hi = lax.bitcast_convert_type(v[:, w:].astype(BF16).astype(F32), I32)
    return lax.shift_right_logical(lo, 16) | (hi & HI_MASK)


def _unpack_halves(u):
    lo = lax.bitcast_convert_type(lax.shift_left(u, 16), F32)
    hi = lax.bitcast_convert_type(u & HI_MASK, F32)
    return lo, hi


def _adaln_kernel(c_ref, w_ref, b_ref, o_ref):
    s = _silu(c_ref[...])
    o_ref[...] = _dot(s.astype(BF16), w_ref[...].astype(BF16)) + b_ref[...]


def _adaln(c, w_ada, b_ada):
    bsz, d = c.shape
    ncol = w_ada.shape[1]
    tn = 1024
    return pl.pallas_call(
        _adaln_kernel,
        out_shape=jax.ShapeDtypeStruct((bsz, ncol), F32),
        grid=(ncol // tn,),
        in_specs=[
            pl.BlockSpec((bsz, d), lambda j: (0, 0)),
            pl.BlockSpec((d, tn), lambda j: (0, j)),
            pl.BlockSpec((1, tn), lambda j: (0, j)),
        ],
        out_specs=pl.BlockSpec((bsz, tn), lambda j: (0, j)),
        compiler_params=pltpu.CompilerParams(vmem_limit_bytes=VMEM_LIMIT),
        name="adaln",
    )(c, w_ada, b_ada.reshape(1, ncol))


def _inproj_kernel(x_ref, mod_ref, g_ref, w_ref, cos_ref, sin_ref, o_ref):
    x = x_ref[...]
    ms = jnp.mean(x * x, axis=-1, keepdims=True)
    m = mod_ref[0]
    h = (x * lax.rsqrt(ms + EPS) * g_ref[...]) * (1.0 + m[1:2]) + m[0:1]
    hb = h.astype(BF16)
    cosf = cos_ref[...]
    sinf = sin_ref[...]
    k_scale = RET_DK ** -0.5
    width = RET_HEADS * RET_DK
    for ci in range(IN_COLS // width):
        acc = _dot(hb, w_ref[:, ci * width:(ci + 1) * width])
        if ci < 2:
            for hh in range(RET_HEADS):
                xh = acc[:, hh * RET_DK:(hh + 1) * RET_DK]
                r = xh * cosf + pltpu.roll(xh, RET_DK // 2, axis=1) * sinf
                if ci == 1:
                    r = r * k_scale
                o_ref[:, ci * width + hh * RET_DK:ci * width + (hh + 1) * RET_DK] = r.astype(BF16)
        else:
            o_ref[:, ci * width:(ci + 1) * width] = acc.astype(BF16)


def _inproj(x2, mod3, g_mix, w_in_bf, cos_full, sin_signed, seq):
    n, d = x2.shape
    tm = TM_PROJ
    tiles_per_seq = seq // tm
    return pl.pallas_call(
        _inproj_kernel,
        out_shape=jax.ShapeDtypeStruct((n, IN_COLS), BF16),
        grid=(n // tm,),
        in_specs=[
            pl.BlockSpec((tm, d), lambda i: (i, 0)),
            pl.BlockSpec((1, N_MOD, d), lambda i: (i // tiles_per_seq, 0, 0)),
            pl.BlockSpec((1, d), lambda i: (0, 0)),
            pl.BlockSpec((d, IN_COLS), lambda i: (0, 0)),
            pl.BlockSpec((tm, LANES), lambda i: (i % tiles_per_seq, 0)),
            pl.BlockSpec((tm, LANES), lambda i: (i % tiles_per_seq, 0)),
        ],
        out_specs=pl.BlockSpec((tm, IN_COLS), lambda i: (i, 0)),
        compiler_params=pltpu.CompilerParams(vmem_limit_bytes=VMEM_LIMIT),
        name="inproj",
    )(x2, mod3, g_mix, w_in_bf, cos_full, sin_signed)


def _ret_kernel(lg_ref, q_ref, k_ref, v_ref, g_ref, o_ref):
    seq = q_ref.shape[1]
    c = RET_CHUNK
    lg = lg_ref[pl.program_id(1)]
    row = lax.broadcasted_iota(I32, (c, c), 0)
    col = lax.broadcasted_iota(I32, (c, c), 1)
    diff = (row - col).astype(F32)
    dmask = jnp.where(diff >= 0, jnp.exp(lg * jnp.maximum(diff, 0.0)), 0.0)
    idx = lax.broadcasted_iota(I32, (c, 1), 0).astype(F32)
    q_decay = jnp.exp(lg * (idx + 1.0))
    k_decay = jnp.exp(lg * (c - 1.0 - idx))
    chunk_decay = jnp.exp(jnp.full((1, 1), lg * c, F32))
    state = jnp.zeros((RET_DK, RET_DK), F32)
    for n in range(seq // c):
        rows = slice(n * c, (n + 1) * c)
        qn = q_ref[0, rows, :]
        kn = k_ref[0, rows, :]
        vn = v_ref[0, rows, :]
        scores = _nt_dot(qn, kn) * dmask
        inner = _dot(scores.astype(BF16), vn)
        qs = (qn.astype(F32) * q_decay).astype(BF16)
        cross = _dot(qs, state.astype(BF16))
        o = inner + cross
        o = o * lax.rsqrt(jnp.mean(o * o, axis=-1, keepdims=True) + EPS)
        gn = g_ref[0, rows, :].astype(F32)
        o_ref[0, rows, :] = (_silu(gn) * o).astype(BF16)
        ks = (kn.astype(F32) * k_decay).astype(BF16)
        state = state * chunk_decay + _tn_dot(ks, vn)


def _retention(log_g, proj3):
    bsz, seq, _ = proj3.shape
    blk = (1, seq, RET_DK)
    return pl.pallas_call(
        _ret_kernel,
        out_shape=jax.ShapeDtypeStruct((bsz, seq, RET_HEADS * RET_DK), BF16),
        grid_spec=pltpu.PrefetchScalarGridSpec(
            num_scalar_prefetch=1,
            grid=(bsz, RET_HEADS),
            in_specs=[
                pl.BlockSpec(blk, lambda b, h, lg: (b, 0, h)),
                pl.BlockSpec(blk, lambda b, h, lg: (b, 0, RET_HEADS + h)),
                pl.BlockSpec(blk, lambda b, h, lg: (b, 0, 2 * RET_HEADS + h)),
                pl.BlockSpec(blk, lambda b, h, lg: (b, 0, 3 * RET_HEADS + h)),
            ],
            out_specs=pl.BlockSpec(blk, lambda b, h, lg: (b, 0, h)),
        ),
        compiler_params=pltpu.CompilerParams(vmem_limit_bytes=VMEM_LIMIT),
        name="retention",
    )(log_g, proj3, proj3, proj3, proj3)


def _moba_kernel(q_ref, k_ref, v_ref, qg_ref, kg_ref, o_ref, qn_s, kn_s):
    seq = q_ref.shape[1]
    lb = MOBA_BLOCK
    nb = seq // lb
    lane = lax.broadcasted_iota(I32, (1, LANES), 1)
    is_a = lane < MOBA_DH
    halves = (is_a, jnp.logical_not(is_a))

    def head_norm(xf, gain):
        sq = xf * xf
        s_a = jnp.sum(jnp.where(is_a, sq, 0.0), axis=-1, keepdims=True)
        s_b = jnp.sum(jnp.where(is_a, 0.0, sq), axis=-1, keepdims=True)
        inv = jnp.where(is_a, lax.rsqrt(s_a / MOBA_DH + EPS), lax.rsqrt(s_b / MOBA_DH + EPS))
        return xf * inv * gain

    qg = qg_ref[...]
    kg = kg_ref[...]
    k_means = []
    for j in range(nb):
        rows = slice(j * lb, (j + 1) * lb)
        kf = head_norm(k_ref[0, rows, :].astype(F32), kg)
        kn_s[rows, :] = kf.astype(BF16)
        k_means.append(jnp.mean(kf, axis=0, keepdims=True))
        qf = head_norm(q_ref[0, rows, :].astype(F32), qg)
        qn_s[rows, :] = (qf * (MOBA_DH ** -0.5)).astype(BF16)
    k_mean = jnp.concatenate(k_means, axis=0)

    r_loc = lax.broadcasted_iota(I32, (lb, lb), 0)
    c_loc = lax.broadcasted_iota(I32, (lb, lb), 1)
    causal = c_loc <= r_loc

    for i in range(nb):
        rows = slice(i * lb, (i + 1) * lb)
        qi = qn_s[rows, :]
        outs = []
        for hx in range(2):
            q_h = jnp.where(halves[hx], qi, jnp.zeros_like(qi))
            sel = [None] * i
            if i > MOBA_TOPK:
                km_h = jnp.where(halves[hx], k_mean, 0.0).astype(BF16)
                gate = _nt_dot(q_h, km_h)
                cols = [gate[:, j:j + 1] for j in range(i)]
                for j in range(i):
                    rank = jnp.zeros((lb, 1), F32)
                    for j2 in range(i):
                        if j2 == j:
                            continue
                        beats = (cols[j2] >= cols[j]) if j2 < j else (cols[j2] > cols[j])
                        rank = rank + jnp.where(beats, 1.0, 0.0)
                    sel[j] = rank < float(MOBA_TOPK)
            pieces = []
            for j in range(i + 1):
                s = _nt_dot(q_h, kn_s[j * lb:(j + 1) * lb, :])
                if j == i:
                    s = jnp.where(causal, s, NEG_INF)
                elif sel[j] is not None:
                    s = jnp.where(sel[j], s, NEG_INF)
                pieces.append(s)
            mx = jnp.max(pieces[0], axis=-1, keepdims=True)
            for s in pieces[1:]:
                mx = jnp.maximum(mx, jnp.max(s, axis=-1, keepdims=True))
            den = jnp.zeros((lb, 1), F32)
            acc = jnp.zeros((lb, LANES), F32)
            for j, s in enumerate(pieces):
                p = jnp.exp(s - mx)
                den = den + jnp.sum(p, axis=-1, keepdims=True)
                acc = acc + _dot(p.astype(BF16), v_ref[0, j * lb:(j + 1) * lb, :])
            outs.append(acc / den)
        o_ref[0, rows, :] = jnp.where(is_a, outs[0], outs[1]).astype(BF16)


def _moba(proj3, qg2, kg2):
    bsz, seq, _ = proj3.shape
    pairs = MOBA_HEADS // 2
    blk = (1, seq, LANES)
    base = 4 * RET_HEADS
    return pl.pallas_call(
        _moba_kernel,
        out_shape=jax.ShapeDtypeStruct((bsz, seq, MOBA_HEADS * MOBA_DH), BF16),
        grid=(bsz, pairs),
        in_specs=[
            pl.BlockSpec(blk, lambda b, p: (b, 0, base + p)),
            pl.BlockSpec(blk, lambda b, p: (b, 0, base + pairs + p)),
            pl.BlockSpec(blk, lambda b, p: (b, 0, base + 2 * pairs + p)),
            pl.BlockSpec((1, LANES), lambda b, p: (0, 0)),
            pl.BlockSpec((1, LANES), lambda b, p: (0, 0)),
        ],
        out_specs=pl.BlockSpec(blk, lambda b, p: (b, 0, p)),
        scratch_shapes=[pltpu.VMEM((seq, LANES), BF16), pltpu.VMEM((seq, LANES), BF16)],
        compiler_params=pltpu.CompilerParams(vmem_limit_bytes=VMEM_LIMIT),
        name="moba",
    )(proj3, proj3, proj3, qg2, kg2)


def _mid_kernel(ret_ref, moba_ref, x_ref, mod_ref, wo1_ref, wo2_ref, g_ref, wr_ref, rb_ref,
                wsg_ref, wsu_ref, wsd_ref,
                xb_ref, h2_ref, e_ref, w_ref, rk_ref, cnt_ref, carry_s):
    i = pl.program_id(0)
    tm = x_ref.shape[0]

    @pl.when(i == 0)
    def _():
        carry_s[...] = jnp.zeros_like(carry_s)

    m = mod_ref[0]
    mixed = _dot(ret_ref[...], wo1_ref[...]) + _dot(moba_ref[...], wo2_ref[...])
    x1 = x_ref[...] + m[2:3] * mixed
    ms = jnp.mean(x1 * x1, axis=-1, keepdims=True)
    h2 = (x1 * lax.rsqrt(ms + EPS) * g_ref[...]) * (1.0 + m[4:5]) + m[3:4]
    h2_ref[...] = _pack_halves(h2)
    h2b = h2.astype(BF16)

    hid = _silu(_dot(h2b, wsg_ref[...])) * _dot(h2b, wsu_ref[...])
    xb_ref[...] = x1 + m[5:6] * _dot(hid.astype(BF16), wsd_ref[...])

    scores = jax.nn.sigmoid(_nt_dot(wr_ref[...], h2b))
    biased = scores + rb_ref[...]
    grp = biased.reshape(N_GROUPS, GROUP_SIZE, tm)
    gi = lax.broadcasted_iota(I32, (N_GROUPS, GROUP_SIZE, tm), 1).astype(F32)
    top1 = jnp.max(grp, axis=1, keepdims=True)
    first = jnp.min(jnp.where(grp == top1, gi, float(GROUP_SIZE)), axis=1, keepdims=True)
    top2 = jnp.max(jnp.where(gi == first, NEG_INF, grp), axis=1, keepdims=True)
    gscore = (top1 + top2).reshape(N_GROUPS, tm)
    gidx = lax.broadcasted_iota(I32, (N_GROUPS, tm), 0)
    grank = jnp.zeros((N_GROUPS, tm), F32)
    for g2 in range(N_GROUPS):
        rowv = gscore[g2:g2 + 1, :]
        beats = (rowv > gscore) | ((rowv == gscore) & (g2 < gidx))
        grank = grank + jnp.where(beats, 1.0, 0.0)
    gsel = jnp.where(grank < float(TOPK_GROUPS), 1.0, 0.0)
    emask = jnp.broadcast_to(gsel.reshape(N_GROUPS, 1, tm), (N_GROUPS, GROUP_SIZE, tm)).reshape(N_EXPERTS, tm)
    choice = jnp.where(emask > 0.5, biased, NEG_INF)

    eidx = lax.broadcasted_iota(I32, (N_EXPERTS, tm), 0).astype(F32)
    selmask = jnp.zeros((N_EXPERTS, tm), F32)
    e_rows = []
    w_rows = []
    for _k in range(TOP_K):
        mx = jnp.max(choice, axis=0, keepdims=True)
        idx = jnp.min(jnp.where(choice == mx, eidx, float(N_EXPERTS)), axis=0, keepdims=True)
        onehot = eidx == idx
        e_rows.append(idx)
        w_rows.append(jnp.sum(jnp.where(onehot, scores, 0.0), axis=0, keepdims=True))
        selmask = selmask + jnp.where(onehot, 1.0, 0.0)
        choice = jnp.where(onehot, NEG_INF, choice)
    wsum = w_rows[0]
    for wk in w_rows[1:]:
        wsum = wsum + wk

    tr = lax.broadcasted_iota(I32, (tm, tm), 0)
    tc = lax.broadcasted_iota(I32, (tm, tm), 1)
    upper = jnp.where(tr < tc, 1.0, 0.0).astype(BF16)
    prefix = _dot(selmask.astype(BF16), upper) + carry_s[...]
    r_rows = [jnp.sum(jnp.where(eidx == ek, prefix, 0.0), axis=0, keepdims=True) for ek in e_rows]
    carry_s[...] = carry_s[...] + jnp.sum(selmask, axis=1, keepdims=True)

    e_ref[...] = jnp.concatenate(e_rows, axis=0).astype(I32)
    w_ref[...] = jnp.concatenate([wk / wsum * ROUTED_SCALE for wk in w_rows], axis=0)
    rk_ref[...] = jnp.concatenate(r_rows, axis=0).astype(I32)
    cnt_ref[...] = carry_s[...].astype(I32)


def _mid(ret2, moba2, x2, mod3, wo1, wo2, g_ffn, wr_t, rbias, wsg, wsu, wsd, seq):
    n, d = x2.shape
    tm = TM_PROJ
    tiles_per_seq = seq // tm
    half = ret2.shape[1]
    ff = wsg.shape[1]
    const = lambda i: (0, 0)
    row = lambda i: (i, 0)
    colt = lambda i: (0, i)
    return pl.pallas_call(
        _mid_kernel,
        out_shape=(
            jax.ShapeDtypeStruct((n, d), F32),
            jax.ShapeDtypeStruct((n, d // 2), I32),
            jax.ShapeDtypeStruct((TOP_K, n), I32),
            jax.ShapeDtypeStruct((TOP_K, n), F32),
            jax.ShapeDtypeStruct((TOP_K, n), I32),
            jax.ShapeDtypeStruct((N_EXPERTS, 1), I32),
        ),
        grid=(n // tm,),
        in_specs=[
            pl.BlockSpec((tm, half), row),
            pl.BlockSpec((tm, half), row),
            pl.BlockSpec((tm, d), row),
            pl.BlockSpec((1, N_MOD, d), lambda i: (i // tiles_per_seq, 0, 0)),
            pl.BlockSpec((half, d), const),
            pl.BlockSpec((half, d), const),
            pl.BlockSpec((1, d), const),
            pl.BlockSpec((N_EXPERTS, d), const),
            pl.BlockSpec((N_EXPERTS, 1), const),
            pl.BlockSpec((d, ff), const),
            pl.BlockSpec((d, ff), const),
            pl.BlockSpec((ff, d), const),
        ],
        out_specs=(
            pl.BlockSpec((tm, d), row),
            pl.BlockSpec((tm, d // 2), row),
            pl.BlockSpec((TOP_K, tm), colt),
            pl.BlockSpec((TOP_K, tm), colt),
            pl.BlockSpec((TOP_K, tm), colt),
            pl.BlockSpec((N_EXPERTS, 1), const),
        ),
        scratch_shapes=[pltpu.VMEM((N_EXPERTS, 1), F32)],
        compiler_params=pltpu.CompilerParams(
            dimension_semantics=("arbitrary",), vmem_limit_bytes=VMEM_LIMIT),
        name="mid",
    )(ret2, moba2, x2, mod3, wo1, wo2, g_ffn, wr_t, rbias, wsg, wsu, wsd)


def _sc_dispatch(h2p, dest3):
    n, words = h2p.shape
    nchunks = n // SC_CHUNK
    per_worker = nchunks // (SC_CORES * SC_SUBCORES)
    mesh = plsc.VectorSubcoreMesh(core_axis_name="c", subcore_axis_name="s",
                                  num_cores=SC_CORES, num_subcores=SC_SUBCORES)

    @functools.partial(
        pl.kernel, mesh=mesh,
        out_type=jax.ShapeDtypeStruct((n * TOP_K, words), I32),
        scratch_types=[
            pltpu.VMEM((TOP_K, SC_CHUNK), I32),
            pltpu.VMEM((SC_CHUNK, words), I32),
            pltpu.SemaphoreType.DMA,
        ],
        name="sc_dispatch",
    )
    def run(h_hbm, d_hbm, xs_hbm, idx_v, rows_v, sem):
        wid = lax.axis_index("s") * SC_CORES + lax.axis_index("c")

        @pl.loop(0, per_worker)
        def _(j):
            ch = wid * per_worker + j
            pltpu.sync_copy(d_hbm.at[ch], idx_v)
            pltpu.sync_copy(h_hbm.at[pl.ds(ch * SC_CHUNK, SC_CHUNK)], rows_v)
            copies = [pltpu.async_copy(rows_v, xs_hbm.at[idx_v.at[k]], sem) for k in range(TOP_K)]
            for cp in copies:
                cp.wait()

    return run(h2p, dest3)


def _ffn_kernel(blk_ref, exp_ref, lo_ref, hi_ref, x_ref, wg_ref, wu_ref, wd_ref, y_ref):
    i = pl.program_id(0)
    prev = jnp.maximum(i - 1, 0)
    first = jnp.logical_or(i == 0, blk_ref[i] != blk_ref[prev])

    @pl.when(first)
    def _():
        y_ref[...] = jnp.zeros_like(y_ref)

    lo = lo_ref[i]
    hi = hi_ref[i]

    @pl.when(hi > lo)
    def _():
        half = x_ref.shape[1]
        x_lo, x_hi = _unpack_halves(x_ref[...])
        x_lo = x_lo.astype(BF16)
        x_hi = x_hi.astype(BF16)
        wg = wg_ref[0].astype(BF16)
        wu = wu_ref[0].astype(BF16)
        hg = _dot(x_lo, wg[:half]) + _dot(x_hi, wg[half:])
        hu = _dot(x_lo, wu[:half]) + _dot(x_hi, wu[half:])
        y = _dot((_silu(hg) * hu).astype(BF16), wd_ref[0].astype(BF16))
        r = lax.broadcasted_iota(I32, (x_ref.shape[0], 1), 0)
        mine = (r >= lo) & (r < hi)
        y_ref[...] = jnp.where(mine, _pack_halves(y), y_ref[...])


def _ffn(blk, exp, lo, hi, xs, w_gate, w_up, w_down):
    a, half = xs.shape
    d = 2 * half
    ff = w_gate.shape[2]
    steps = blk.shape[0]
    return pl.pallas_call(
        _ffn_kernel,
        out_shape=jax.ShapeDtypeStruct((a, half), I32),
        grid_spec=pltpu.PrefetchScalarGridSpec(
            num_scalar_prefetch=4,
            grid=(steps,),
            in_specs=[
                pl.BlockSpec((FFN_BLOCK, half), lambda i, b, e, l, h: (b[i], 0)),
                pl.BlockSpec((1, d, ff), lambda i, b, e, l, h: (e[i], 0, 0)),
                pl.BlockSpec((1, d, ff), lambda i, b, e, l, h: (e[i], 0, 0)),
                pl.BlockSpec((1, ff, d), lambda i, b, e, l, h: (e[i], 0, 0)),
            ],
            out_specs=pl.BlockSpec((FFN_BLOCK, half), lambda i, b, e, l, h: (b[i], 0)),
        ),
        compiler_params=pltpu.CompilerParams(
            dimension_semantics=("arbitrary",), vmem_limit_bytes=VMEM_LIMIT),
        name="ffn",
    )(blk, exp, lo, hi, xs, w_gate, w_up, w_down)


def _sc_gather(y, dest3):
    a, words = y.shape
    nchunks, _, chunk = dest3.shape
    n = nchunks * chunk
    per_worker = nchunks // (SC_CORES * SC_SUBCORES)
    mesh = plsc.VectorSubcoreMesh(core_axis_name="c", subcore_axis_name="s",
                                  num_cores=SC_CORES, num_subcores=SC_SUBCORES)

    @functools.partial(
        pl.kernel, mesh=mesh,
        out_type=jax.ShapeDtypeStruct((TOP_K, n, words), I32),
        scratch_types=[
            pltpu.VMEM((TOP_K, chunk), I32),
            pltpu.VMEM((chunk, words), I32),
            pltpu.VMEM((chunk, words), I32),
            pltpu.SemaphoreType.DMA,
            pltpu.SemaphoreType.DMA((2,)),
        ],
        name="sc_gather",
    )
    def run(y_hbm, d_hbm, yt_hbm, idx_v, buf0, buf1, sem_g, sem_w):
        wid = lax.axis_index("s") * SC_CORES + lax.axis_index("c")
        bufs = (buf0, buf1)

        @pl.loop(0, per_worker)
        def _(j):
            ch = wid * per_worker + j
            pltpu.sync_copy(d_hbm.at[ch], idx_v)
            rows = pl.ds(ch * chunk, chunk)
            gather = pltpu.async_copy(y_hbm.at[idx_v.at[0]], bufs[0], sem_g)
            writes = []
            for k in range(TOP_K):
                gather.wait()
                writes.append(pltpu.async_copy(bufs[k % 2], yt_hbm.at[k, rows], sem_w.at[k % 2]))
                if k + 1 < TOP_K:
                    if k >= 1:
                        writes[k - 1].wait()
                    gather = pltpu.async_copy(y_hbm.at[idx_v.at[k + 1]], bufs[(k + 1) % 2], sem_g)
            writes[TOP_K - 2].wait()
            writes[TOP_K - 1].wait()

    return run(y, dest3)


def _combine_kernel(yt_ref, wt_ref, xb_ref, mod_ref, o_ref):
    half = yt_ref.shape[2]
    wt = wt_ref[...]
    lo, hi = _unpack_halves(yt_ref[0])
    r_lo = lo * wt[:, 0:1]
    r_hi = hi * wt[:, 0:1]
    for k in range(1, TOP_K):
        lo, hi = _unpack_halves(yt_ref[k])
        r_lo = r_lo + lo * wt[:, k:k + 1]
        r_hi = r_hi + hi * wt[:, k:k + 1]
    gate = mod_ref[0][5:6]
    o_ref[:, :half] = xb_ref[:, :half] + gate[:, :half] * r_lo
    o_ref[:, half:] = xb_ref[:, half:] + gate[:, half:] * r_hi


def _combine(yt, w_t, xb, mod3, seq):
    n, d = xb.shape
    tm = TM_COMBINE
    tiles_per_seq = seq // tm
    return pl.pallas_call(
        _combine_kernel,
        out_shape=jax.ShapeDtypeStruct((n, d), F32),
        grid=(n // tm,),
        in_specs=[
            pl.BlockSpec((TOP_K, tm, d // 2), lambda i: (0, i, 0)),
            pl.BlockSpec((tm, TOP_K), lambda i: (i, 0)),
            pl.BlockSpec((tm, d), lambda i: (i, 0)),
            pl.BlockSpec((1, N_MOD, d), lambda i: (i // tiles_per_seq, 0, 0)),
        ],
        out_specs=pl.BlockSpec((tm, d), lambda i: (i, 0)),
        compiler_params=pltpu.CompilerParams(vmem_limit_bytes=VMEM_LIMIT),
        name="combine",
    )(yt, w_t, xb, mod3)


def _rotary_tables(seq):
    half = RET_DK // 2
    pos = jnp.arange(seq, dtype=F32)
    inv = ROPE_BASE ** (-jnp.arange(half, dtype=F32) / half)
    ang = pos[:, None] * inv[None, :]
    cos = jnp.cos(ang)
    sin = jnp.sin(ang)
    return jnp.concatenate([cos, cos], axis=-1), jnp.concatenate([-sin, sin], axis=-1)


def _ffn_schedule(counts, total_rows):
    nblk = total_rows // FFN_BLOCK
    ends = jnp.cumsum(counts)
    start = ends - counts
    bp = jnp.sort(jnp.concatenate([jnp.arange(nblk, dtype=I32) * FFN_BLOCK, start]))
    lo = bp
    hi = jnp.concatenate([bp[1:], jnp.full((1,), total_rows, I32)])
    blk = jnp.minimum(lo // FFN_BLOCK, nblk - 1)
    exp = jnp.minimum(jnp.sum((ends[None, :] <= lo[:, None]).astype(I32), axis=1), N_EXPERTS - 1)
    return start, blk, exp, lo - blk * FFN_BLOCK, hi - blk * FFN_BLOCK


def _chunked(dest, chunk):
    k, n = dest.shape
    return dest.reshape(k, n // chunk, chunk).transpose(1, 0, 2)


def kernel(x, c, w_ada, b_ada, g_mix, w_in, q_gain, k_gain, w_out, g_ffn, w_router, router_bias,
           w_gate, w_up, w_down, ws_gate, ws_up, ws_down):
    bsz, seq, d = x.shape
    n = bsz * seq
    depth = w_ada.shape[0]
    cos_full, sin_signed = _rotary_tables(seq)
    log_g = jnp.log1p(-jnp.exp2(-5.0 - jnp.arange(RET_HEADS, dtype=F32)))
    ret_w = RET_HEADS * RET_DK
    x2 = x.reshape(n, d)
    for l in range(depth):
        mod3 = _adaln(c, w_ada[l], b_ada[l]).reshape(bsz, N_MOD, d)
        proj = _inproj(x2, mod3, g_mix[l].reshape(1, d), w_in[l].astype(BF16), cos_full, sin_signed, seq)
        proj3 = proj.reshape(bsz, seq, IN_COLS)
        ret = _retention(log_g, proj3)
        qg2 = jnp.tile(q_gain[l].reshape(1, MOBA_DH), (1, 2))
        kg2 = jnp.tile(k_gain[l].reshape(1, MOBA_DH), (1, 2))
        moba = _moba(proj3, qg2, kg2)
        wo = w_out[l].astype(BF16)
        xb, h2, e_idx, w_k, rank, cnt = _mid(
            ret.reshape(n, ret_w), moba.reshape(n, MOBA_HEADS * MOBA_DH), x2, mod3,
            wo[:ret_w], wo[ret_w:], g_ffn[l].reshape(1, d),
            w_router[l].T.astype(BF16), router_bias[l].reshape(N_EXPERTS, 1),
            ws_gate[l].astype(BF16), ws_up[l].astype(BF16), ws_down[l].astype(BF16), seq)
        start, blk, exp, lo, hi = _ffn_schedule(cnt[:, 0], n * TOP_K)
        dest = jnp.take(start, e_idx) + rank
        xs = _sc_dispatch(h2, _chunked(dest, SC_CHUNK))
        y = _ffn(blk, exp, lo, hi, xs, w_gate[l], w_up[l], w_down[l])
        yt = _sc_gather(y, _chunked(dest, SC_GATHER_CHUNK))
        x2 = _combine(yt, w_k.T, xb, mod3, seq)
    return x2.reshape(bsz, seq, d)
```

```python
import functools

import numpy as np
import jax
import jax.numpy as jnp
from jax import lax
from jax.experimental import pallas as pl
from jax.experimental.pallas import tpu as pltpu
from jax.experimental.pallas import tpu_sc as plsc

F32 = jnp.float32
BF16 = jnp.bfloat16
I32 = jnp.int32

D_MODEL = 1024
RET_HEADS = 4
RET_DK = 128
MOBA_HEADS = 8
MOBA_DH = 64
MOBA_BLOCK = 256
MOBA_TOPK = 3
ROPE_BASE = 10000.0
N_EXPERTS = 256
TOP_K = 8
N_GROUPS = 8
TOPK_GROUPS = 4
GROUP_SIZE = N_EXPERTS // N_GROUPS
EXPERT_FF = 256
ROUTED_SCALE = 2.5
N_MOD = 6
EPS = 1e-6
IN_COLS = 3584

LANES = 128
RET_CHUNK = 256
TM_PROJ = 512
TM_COMBINE = 256
SC_CORES = 2
SC_SUBCORES = 16
SC_CHUNK = 128
SC_GATHER_CHUNK = 64
FFN_BLOCK = 256
VMEM_LIMIT = 56 * 1024 * 1024

NEG_INF = float("-inf")


def _silu(x):
    return x * jax.nn.sigmoid(x)


def _nt_dot(a, b):
    return lax.dot_general(a, b, (((1,), (1,)), ((), ())), preferred_element_type=F32)


def _tn_dot(a, b):
    return lax.dot_general(a, b, (((0,), (0,)), ((), ())), preferred_element_type=F32)


def _dot(a, b):
    return jnp.dot(a, b, preferred_element_type=F32)


HI_MASK = -65536


def _pack_halves(v):
    w = v.shape[1] // 2
    lo = lax.bitcast_convert_type(v[:, :w].astype(BF16).astype(F32), I32)
    hi = lax.bitcast_convert_type(v[:, w:].astype(BF16).astype(F32), I32)
    return lax.shift_right_logical(lo, 16) | (hi & HI_MASK)


def _unpack_halves(u):
    lo = lax.bitcast_convert_type(lax.shift_left(u, 16), F32)
    hi = lax.bitcast_convert_type(u & HI_MASK, F32)
    return lo, hi


def _adaln_kernel(c_ref, w_ref, b_ref, o_ref):
    s = _silu(c_ref[...])
    o_ref[...] = _dot(s.astype(BF16), w_ref[...].astype(BF16)) + b_ref[...]


def _adaln(c, w_ada, b_ada):
    bsz, d = c.shape
    ncol = w_ada.shape[1]
    tn = 1024
    return pl.pallas_call(
        _adaln_kernel,
        out_shape=jax.ShapeDtypeStruct((bsz, ncol), F32),
        grid=(ncol // tn,),
        in_specs=[
            pl.BlockSpec((bsz, d), lambda j: (0, 0)),
            pl.BlockSpec((d, tn), lambda j: (0, j)),
            pl.BlockSpec((1, tn), lambda j: (0, j)),
        ],
        out_specs=pl.BlockSpec((bsz, tn), lambda j: (0, j)),
        compiler_params=pltpu.CompilerParams(vmem_limit_bytes=VMEM_LIMIT),
        name="adaln",
    )(c, w_ada, b_ada.reshape(1, ncol))


def _inproj_kernel(x_ref, mod_ref, g_ref, w_ref, cos_ref, sin_ref, o_ref):
    x = x_ref[...]
    ms = jnp.mean(x * x, axis=-1, keepdims=True)
    m = mod_ref[0]
    h = (x * lax.rsqrt(ms + EPS) * g_ref[...]) * (1.0 + m[1:2]) + m[0:1]
    hb = h.astype(BF16)
    cosf = cos_ref[...]
    sinf = sin_ref[...]
    k_scale = RET_DK ** -0.5
    width = RET_HEADS * RET_DK
    for ci in range(IN_COLS // width):
        acc = _dot(hb, w_ref[:, ci * width:(ci + 1) * width])
        if ci < 2:
            for hh in range(RET_HEADS):
                xh = acc[:, hh * RET_DK:(hh + 1) * RET_DK]
                r = xh * cosf + pltpu.roll(xh, RET_DK // 2, axis=1) * sinf
                if ci == 1:
                    r = r * k_scale
                o_ref[:, ci * width + hh * RET_DK:ci * width + (hh + 1) * RET_DK] = r.astype(BF16)
        else:
            o_ref[:, ci * width:(ci + 1) * width] = acc.astype(BF16)


def _inproj(x2, mod3, g_mix, w_in_bf, cos_full, sin_signed, seq):
    n, d = x2.shape
    tm = TM_PROJ
    tiles_per_seq = seq // tm
    return pl.pallas_call(
        _inproj_kernel,
        out_shape=jax.ShapeDtypeStruct((n, IN_COLS), BF16),
        grid=(n // tm,),
        in_specs=[
            pl.BlockSpec((tm, d), lambda i: (i, 0)),
            pl.BlockSpec((1, N_MOD, d), lambda i: (i // tiles_per_seq, 0, 0)),
            pl.BlockSpec((1, d), lambda i: (0, 0)),
            pl.BlockSpec((d, IN_COLS), lambda i: (0, 0)),
            pl.BlockSpec((tm, LANES), lambda i: (i % tiles_per_seq, 0)),
            pl.BlockSpec((tm, LANES), lambda i: (i % tiles_per_seq, 0)),
        ],
        out_specs=pl.BlockSpec((tm, IN_COLS), lambda i: (i, 0)),
        compiler_params=pltpu.CompilerParams(vmem_limit_bytes=VMEM_LIMIT),
        name="inproj",
    )(x2, mod3, g_mix, w_in_bf, cos_full, sin_signed)


def _ret_kernel(lg_ref, q_ref, k_ref, v_ref, g_ref, o_ref):
    seq = q_ref.shape[1]
    c = RET_CHUNK
    lg = lg_ref[pl.program_id(1)]
    row = lax.broadcasted_iota(I32, (c, c), 0)
    col = lax.broadcasted_iota(I32, (c, c), 1)
    diff = (row - col).astype(F32)
    dmask = jnp.where(diff >= 0, jnp.exp(lg * jnp.maximum(diff, 0.0)), 0.0)
    idx = lax.broadcasted_iota(I32, (c, 1), 0).astype(F32)
    q_decay = jnp.exp(lg * (idx + 1.0))
    k_decay = jnp.exp(lg * (c - 1.0 - idx))
    chunk_decay = jnp.exp(jnp.full((1, 1), lg * c, F32))
    state = jnp.zeros((RET_DK, RET_DK), F32)
    for n in range(seq // c):
        rows = slice(n * c, (n + 1) * c)
        qn = q_ref[0, rows, :]
        kn = k_ref[0, rows, :]
        vn = v_ref[0, rows, :]
        scores = _nt_dot(qn, kn) * dmask
        inner = _dot(scores.astype(BF16), vn)
        qs = (qn.astype(F32) * q_decay).astype(BF16)
        cross = _dot(qs, state.astype(BF16))
        o = inner + cross
        o = o * lax.rsqrt(jnp.mean(o * o, axis=-1, keepdims=True) + EPS)
        gn = g_ref[0, rows, :].astype(F32)
        o_ref[0, rows, :] = (_silu(gn) * o).astype(BF16)
        ks = (kn.astype(F32) * k_decay).astype(BF16)
        state = state * chunk_decay + _tn_dot(ks, vn)


def _retention(log_g, proj3):
    bsz, seq, _ = proj3.shape
    blk = (1, seq, RET_DK)
    return pl.pallas_call(
        _ret_kernel,
        out_shape=jax.ShapeDtypeStruct((bsz, seq, RET_HEADS * RET_DK), BF16),
        grid_spec=pltpu.PrefetchScalarGridSpec(
            num_scalar_prefetch=1,
            grid=(bsz, RET_HEADS),
            in_specs=[
                pl.BlockSpec(blk, lambda b, h, lg: (b, 0, h)),
                pl.BlockSpec(blk, lambda b, h, lg: (b, 0, RET_HEADS + h)),
                pl.BlockSpec(blk, lambda b, h, lg: (b, 0, 2 * RET_HEADS + h)),
                pl.BlockSpec(blk, lambda b, h, lg: (b, 0, 3 * RET_HEADS + h)),
            ],
            out_specs=pl.BlockSpec(blk, lambda b, h, lg: (b, 0, h)),
        ),
        compiler_params=pltpu.CompilerParams(vmem_limit_bytes=VMEM_LIMIT),
        name="retention",
    )(log_g, proj3, proj3, proj3, proj3)


def _moba_kernel(q_ref, k_ref, v_ref, qg_ref, kg_ref, o_ref, qn_s, kn_s):
    seq = q_ref.shape[1]
    lb = MOBA_BLOCK
    nb = seq // lb
    lane = lax.broadcasted_iota(I32, (1, LANES), 1)
    is_a = lane < MOBA_DH
    halves = (is_a, jnp.logical_not(is_a))

    def head_norm(xf, gain):
        sq = xf * xf
        s_a = jnp.sum(jnp.where(is_a, sq, 0.0), axis=-1, keepdims=True)
        s_b = jnp.sum(jnp.where(is_a, 0.0, sq), axis=-1, keepdims=True)
        inv = jnp.where(is_a, lax.rsqrt(s_a / MOBA_DH + EPS), lax.rsqrt(s_b / MOBA_DH + EPS))
        return xf * inv * gain

    qg = qg_ref[...]
    kg = kg_ref[...]
    k_means = []
    for j in range(nb):
        rows = slice(j * lb, (j + 1) * lb)
        kf = head_norm(k_ref[0, rows, :].astype(F32), kg)
        kn_s[rows, :] = kf.astype(BF16)
        k_means.append(jnp.mean(kf, axis=0, keepdims=True))
        qf = head_norm(q_ref[0, rows, :].astype(F32), qg)
        qn_s[rows, :] = (qf * (MOBA_DH ** -0.5)).astype(BF16)
    k_mean = jnp.concatenate(k_means, axis=0)

    r_loc = lax.broadcasted_iota(I32, (lb, lb), 0)
    c_loc = lax.broadcasted_iota(I32, (lb, lb), 1)
    causal = c_loc <= r_loc

    for i in range(nb):
        rows = slice(i * lb, (i + 1) * lb)
        qi = qn_s[rows, :]
        outs = []
        for hx in range(2):
            q_h = jnp.where(halves[hx], qi, jnp.zeros_like(qi))
            sel = [None] * i
            if i > MOBA_TOPK:
                km_h = jnp.where(halves[hx], k_mean, 0.0).astype(BF16)
                gate = _nt_dot(q_h, km_h)
                cols = [gate[:, j:j + 1] for j in range(i)]
                for j in range(i):
                    rank = jnp.zeros((lb, 1), F32)
                    for j2 in range(i):
                        if j2 == j:
                            continue
                        beats = (cols[j2] >= cols[j]) if j2 < j else (cols[j2] > cols[j])
                        rank = rank + jnp.where(beats, 1.0, 0.0)
                    sel[j] = rank < float(MOBA_TOPK)
            pieces = []
            for j in range(i + 1):
                s = _nt_dot(q_h, kn_s[j * lb:(j + 1) * lb, :])
                if j == i:
                    s = jnp.where(causal, s, NEG_INF)
                elif sel[j] is not None:
                    s = jnp.where(sel[j], s, NEG_INF)
                pieces.append(s)
            mx = jnp.max(pieces[0], axis=-1, keepdims=True)
            for s in pieces[1:]:
                mx = jnp.maximum(mx, jnp.max(s, axis=-1, keepdims=True))
            den = jnp.zeros((lb, 1), F32)
            acc = jnp.zeros((lb, LANES), F32)
            for j, s in enumerate(pieces):
                p = jnp.exp(s - mx)
                den = den + jnp.sum(p, axis=-1, keepdims=True)
                acc = acc + _dot(p.astype(BF16), v_ref[0, j * lb:(j + 1) * lb, :])
            outs.append(acc / den)
        o_ref[0, rows, :] = jnp.where(is_a, outs[0], outs[1]).astype(BF16)


def _moba(proj3, qg2, kg2):
    bsz, seq, _ = proj3.shape
    pairs = MOBA_HEADS // 2
    blk = (1, seq, LANES)
    base = 4 * RET_HEADS
    return pl.pallas_call(
        _moba_kernel,
        out_shape=jax.ShapeDtypeStruct((bsz, seq, MOBA_HEADS * MOBA_DH), BF16),
        grid=(bsz, pairs),
        in_specs=[
            pl.BlockSpec(blk, lambda b, p: (b, 0, base + p)),
            pl.BlockSpec(blk, lambda b, p: (b, 0, base + pairs + p)),
            pl.BlockSpec(blk, lambda b, p: (b, 0, base + 2 * pairs + p)),
            pl.BlockSpec((1, LANES), lambda b, p: (0, 0)),
            pl.BlockSpec((1, LANES), lambda b, p: (0, 0)),
        ],
        out_specs=pl.BlockSpec(blk, lambda b, p: (b, 0, p)),
        scratch_shapes=[pltpu.VMEM((seq, LANES), BF16), pltpu.VMEM((seq, LANES), BF16)],
        compiler_params=pltpu.CompilerParams(vmem_limit_bytes=VMEM_LIMIT),
        name="moba",
    )(proj3, proj3, proj3, qg2, kg2)


def _mid_kernel(ret_ref, moba_ref, x_ref, mod_ref, wo1_ref, wo2_ref, g_ref, wr_ref, rb_ref,
                wsg_ref, wsu_ref, wsd_ref,
                xb_ref, h2_ref, e_ref, w_ref, rk_ref, cnt_ref, carry_s):
    i = pl.program_id(0)
    tm = x_ref.shape[0]

    @pl.when(i == 0)
    def _():
        carry_s[...] = jnp.zeros_like(carry_s)

    m = mod_ref[0]
    mixed = _dot(ret_ref[...], wo1_ref[...]) + _dot(moba_ref[...], wo2_ref[...])
    x1 = x_ref[...] + m[2:3] * mixed
    ms = jnp.mean(x1 * x1, axis=-1, keepdims=True)
    h2 = (x1 * lax.rsqrt(ms + EPS) * g_ref[...]) * (1.0 + m[4:5]) + m[3:4]
    h2_ref[...] = _pack_halves(h2)
    h2b = h2.astype(BF16)

    hid = _silu(_dot(h2b, wsg_ref[...])) * _dot(h2b, wsu_ref[...])
    xb_ref[...] = x1 + m[5:6] * _dot(hid.astype(BF16), wsd_ref[...])

    scores = jax.nn.sigmoid(_nt_dot(wr_ref[...], h2b))
    biased = scores + rb_ref[...]
    grp = biased.reshape(N_GROUPS, GROUP_SIZE, tm)
    gi = lax.broadcasted_iota(I32, (N_GROUPS, GROUP_SIZE, tm), 1).astype(F32)
    top1 = jnp.max(grp, axis=1, keepdims=True)
    first = jnp.min(jnp.where(grp == top1, gi, float(GROUP_SIZE)), axis=1, keepdims=True)
    top2 = jnp.max(jnp.where(gi == first, NEG_INF, grp), axis=1, keepdims=True)
    gscore = (top1 + top2).reshape(N_GROUPS, tm)
    gidx = lax.broadcasted_iota(I32, (N_GROUPS, tm), 0)
    grank = jnp.zeros((N_GROUPS, tm), F32)
    for g2 in range(N_GROUPS):
        rowv = gscore[g2:g2 + 1, :]
        beats = (rowv > gscore) | ((rowv == gscore) & (g2 < gidx))
        grank = grank + jnp.where(beats, 1.0, 0.0)
    gsel = jnp.where(grank < float(TOPK_GROUPS), 1.0, 0.0)
    emask = jnp.broadcast_to(gsel.reshape(N_GROUPS, 1, tm), (N_GROUPS, GROUP_SIZE, tm)).reshape(N_EXPERTS, tm)
    choice = jnp.where(emask > 0.5, biased, NEG_INF)

    eidx = lax.broadcasted_iota(I32, (N_EXPERTS, tm), 0).astype(F32)
    selmask = jnp.zeros((N_EXPERTS, tm), F32)
    e_rows = []
    w_rows = []
    for _k in range(TOP_K):
        mx = jnp.max(choice, axis=0, keepdims=True)
        idx = jnp.min(jnp.where(choice == mx, eidx, float(N_EXPERTS)), axis=0, keepdims=True)
        onehot = eidx == idx
        e_rows.append(idx)
        w_rows.append(jnp.sum(jnp.where(onehot, scores, 0.0), axis=0, keepdims=True))
        selmask = selmask + jnp.where(onehot, 1.0, 0.0)
        choice = jnp.where(onehot, NEG_INF, choice)
    wsum = w_rows[0]
    for wk in w_rows[1:]:
        wsum = wsum + wk

    tr = lax.broadcasted_iota(I32, (tm, tm), 0)
    tc = lax.broadcasted_iota(I32, (tm, tm), 1)
    upper = jnp.where(tr < tc, 1.0, 0.0).astype(BF16)
    prefix = _dot(selmask.astype(BF16), upper) + carry_s[...]
    r_rows = [jnp.sum(jnp.where(eidx == ek, prefix, 0.0), axis=0, keepdims=True) for ek in e_rows]
    carry_s[...] = carry_s[...] + jnp.sum(selmask, axis=1, keepdims=True)

    e_ref[...] = jnp.concatenate(e_rows, axis=0).astype(I32)
    w_ref[...] = jnp.concatenate([wk / wsum * ROUTED_SCALE for wk in w_rows], axis=0)
    rk_ref[...] = jnp.concatenate(r_rows, axis=0).astype(I32)
    cnt_ref[...] = carry_s[...].astype(I32)


def _mid(ret2, moba2, x2, mod3, wo1, wo2, g_ffn, wr_t, rbias, wsg, wsu, wsd, seq):
    n, d = x2.shape
    tm = TM_PROJ
    tiles_per_seq = seq // tm
    half = ret2.shape[1]
    ff = wsg.shape[1]
    const = lambda i: (0, 0)
    row = lambda i: (i, 0)
    colt = lambda i: (0, i)
    return pl.pallas_call(
        _mid_kernel,
        out_shape=(
            jax.ShapeDtypeStruct((n, d), F32),
            jax.ShapeDtypeStruct((n, d // 2), I32),
            jax.ShapeDtypeStruct((TOP_K, n), I32),
            jax.ShapeDtypeStruct((TOP_K, n), F32),
            jax.ShapeDtypeStruct((TOP_K, n), I32),
            jax.ShapeDtypeStruct((N_EXPERTS, 1), I32),
        ),
        grid=(n // tm,),
        in_specs=[
            pl.BlockSpec((tm, half), row),
            pl.BlockSpec((tm, half), row),
            pl.BlockSpec((tm, d), row),
            pl.BlockSpec((1, N_MOD, d), lambda i: (i // tiles_per_seq, 0, 0)),
            pl.BlockSpec((half, d), const),
            pl.BlockSpec((half, d), const),
            pl.BlockSpec((1, d), const),
            pl.BlockSpec((N_EXPERTS, d), const),
            pl.BlockSpec((N_EXPERTS, 1), const),
            pl.BlockSpec((d, ff), const),
            pl.BlockSpec((d, ff), const),
            pl.BlockSpec((ff, d), const),
        ],
        out_specs=(
            pl.BlockSpec((tm, d), row),
            pl.BlockSpec((tm, d // 2), row),
            pl.BlockSpec((TOP_K, tm), colt),
            pl.BlockSpec((TOP_K, tm), colt),
            pl.BlockSpec((TOP_K, tm), colt),
            pl.BlockSpec((N_EXPERTS, 1), const),
        ),
        scratch_shapes=[pltpu.VMEM((N_EXPERTS, 1), F32)],
        compiler_params=pltpu.CompilerParams(
            dimension_semantics=("arbitrary",), vmem_limit_bytes=VMEM_LIMIT),
        name="mid",
    )(ret2, moba2, x2, mod3, wo1, wo2, g_ffn, wr_t, rbias, wsg, wsu, wsd)


def _dest_kernel(start_ref, e_ref, rk_ref, o_ref):
    e = e_ref[...]

    def body(ex, acc):
        return acc + jnp.where(e == ex, start_ref[ex], 0)

    o_ref[...] = lax.fori_loop(0, N_EXPERTS, body, rk_ref[...], unroll=8)


def _dest(start, e_idx, rank):
    k, n = e_idx.shape
    tn = 2048
    return pl.pallas_call(
        _dest_kernel,
        out_shape=jax.ShapeDtypeStruct((k, n), I32),
        grid_spec=pltpu.PrefetchScalarGridSpec(
            num_scalar_prefetch=1,
            grid=(n // tn,),
            in_specs=[
                pl.BlockSpec((k, tn), lambda i, s: (0, i)),
                pl.BlockSpec((k, tn), lambda i, s: (0, i)),
            ],
            out_specs=pl.BlockSpec((k, tn), lambda i, s: (0, i)),
        ),
        compiler_params=pltpu.CompilerParams(vmem_limit_bytes=VMEM_LIMIT),
        name="dest",
    )(start, e_idx, rank)


def _sc_dispatch(h2p, dest3):
    n, words = h2p.shape
    nchunks = n // SC_CHUNK
    per_worker = nchunks // (SC_CORES * SC_SUBCORES)
    mesh = plsc.VectorSubcoreMesh(core_axis_name="c", subcore_axis_name="s",
                                  num_cores=SC_CORES, num_subcores=SC_SUBCORES)

    @functools.partial(
        pl.kernel, mesh=mesh,
        out_type=jax.ShapeDtypeStruct((n * TOP_K, words), I32),
        scratch_types=[
            pltpu.VMEM((TOP_K, SC_CHUNK), I32),
            pltpu.VMEM((SC_CHUNK, words), I32),
            pltpu.SemaphoreType.DMA,
        ],
        name="sc_dispatch",
    )
    def run(h_hbm, d_hbm, xs_hbm, idx_v, rows_v, sem):
        wid = lax.axis_index("s") * SC_CORES + lax.axis_index("c")

        @pl.loop(0, per_worker)
        def _(j):
            ch = wid * per_worker + j
            pltpu.sync_copy(d_hbm.at[ch], idx_v)
            pltpu.sync_copy(h_hbm.at[pl.ds(ch * SC_CHUNK, SC_CHUNK)], rows_v)
            copies = [pltpu.async_copy(rows_v, xs_hbm.at[idx_v.at[k]], sem) for k in range(TOP_K)]
            for cp in copies:
                cp.wait()

    return run(h2p, dest3)


def _ffn_kernel(blk_ref, exp_ref, lo_ref, hi_ref, x_ref, wg_ref, wu_ref, wd_ref, y_ref):
    i = pl.program_id(0)
    prev = jnp.maximum(i - 1, 0)
    first = jnp.logical_or(i == 0, blk_ref[i] != blk_ref[prev])

    @pl.when(first)
    def _():
        y_ref[...] = jnp.zeros_like(y_ref)

    lo = lo_ref[i]
    hi = hi_ref[i]

    @pl.when(hi > lo)
    def _():
        half = x_ref.shape[1]
        x_lo, x_hi = _unpack_halves(x_ref[...])
        x_lo = x_lo.astype(BF16)
        x_hi = x_hi.astype(BF16)
        wg = wg_ref[0].astype(BF16)
        wu = wu_ref[0].astype(BF16)
        hg = _dot(x_lo, wg[:half]) + _dot(x_hi, wg[half:])
        hu = _dot(x_lo, wu[:half]) + _dot(x_hi, wu[half:])
        y = _dot((_silu(hg) * hu).astype(BF16), wd_ref[0].astype(BF16))
        r = lax.broadcasted_iota(I32, (x_ref.shape[0], 1), 0)
        mine = (r >= lo) & (r < hi)
        y_ref[...] = jnp.where(mine, _pack_halves(y), y_ref[...])


def _ffn(blk, exp, lo, hi, xs, w_gate, w_up, w_down):
    a, half = xs.shape
    d = 2 * half
    ff = w_gate.shape[2]
    steps = blk.shape[0]
    return pl.pallas_call(
        _ffn_kernel,
        out_shape=jax.ShapeDtypeStruct((a, half), I32),
        grid_spec=pltpu.PrefetchScalarGridSpec(
            num_scalar_prefetch=4,
            grid=(steps,),
            in_specs=[
                pl.BlockSpec((FFN_BLOCK, half), lambda i, b, e, l, h: (b[i], 0)),
                pl.BlockSpec((1, d, ff), lambda i, b, e, l, h: (e[i], 0, 0)),
                pl.BlockSpec((1, d, ff), lambda i, b, e, l, h: (e[i], 0, 0)),
                pl.BlockSpec((1, ff, d), lambda i, b, e, l, h: (e[i], 0, 0)),
            ],
            out_specs=pl.BlockSpec((FFN_BLOCK, half), lambda i, b, e, l, h: (b[i], 0)),
        ),
        compiler_params=pltpu.CompilerParams(
            dimension_semantics=("arbitrary",), vmem_limit_bytes=VMEM_LIMIT),
        name="ffn",
    )(blk, exp, lo, hi, xs, w_gate, w_up, w_down)


def _sc_gather(y, dest3):
    a, words = y.shape
    nchunks, _, chunk = dest3.shape
    n = nchunks * chunk
    per_worker = nchunks // (SC_CORES * SC_SUBCORES)
    mesh = plsc.VectorSubcoreMesh(core_axis_name="c", subcore_axis_name="s",
                                  num_cores=SC_CORES, num_subcores=SC_SUBCORES)

    @functools.partial(
        pl.kernel, mesh=mesh,
        out_type=jax.ShapeDtypeStruct((TOP_K, n, words), I32),
        scratch_types=[
            pltpu.VMEM((TOP_K, chunk), I32),
            pltpu.VMEM((chunk, words), I32),
            pltpu.VMEM((chunk, words), I32),
            pltpu.SemaphoreType.DMA,
            pltpu.SemaphoreType.DMA((2,)),
        ],
        name="sc_gather",
    )
    def run(y_hbm, d_hbm, yt_hbm, idx_v, buf0, buf1, sem_g, sem_w):
        wid = lax.axis_index("s") * SC_CORES + lax.axis_index("c")
        bufs = (buf0, buf1)

        @pl.loop(0, per_worker)
        def _(j):
            ch = wid * per_worker + j
            pltpu.sync_copy(d_hbm.at[ch], idx_v)
            rows = pl.ds(ch * chunk, chunk)
            gather = pltpu.async_copy(y_hbm.at[idx_v.at[0]], bufs[0], sem_g)
            writes = []
            for k in range(TOP_K):
                gather.wait()
                writes.append(pltpu.async_copy(bufs[k % 2], yt_hbm.at[k, rows], sem_w.at[k % 2]))
                if k + 1 < TOP_K:
                    if k >= 1:
                        writes[k - 1].wait()
                    gather = pltpu.async_copy(y_hbm.at[idx_v.at[k + 1]], bufs[(k + 1) % 2], sem_g)
            writes[TOP_K - 2].wait()
            writes[TOP_K - 1].wait()

    return run(y, dest3)


def _combine_kernel(yt_ref, wt_ref, xb_ref, mod_ref, o_ref):
    half = yt_ref.shape[2]
    wt = wt_ref[...]
    lo, hi = _unpack_halves(yt_ref[0])
    r_lo = lo * wt[:, 0:1]
    r_hi = hi * wt[:, 0:1]
    for k in range(1, TOP_K):
        lo, hi = _unpack_halves(yt_ref[k])
        r_lo = r_lo + lo * wt[:, k:k + 1]
        r_hi = r_hi + hi * wt[:, k:k + 1]
    gate = mod_ref[0][5:6]
    o_ref[:, :half] = xb_ref[:, :half] + gate[:, :half] * r_lo
    o_ref[:, half:] = xb_ref[:, half:] + gate[:, half:] * r_hi


def _combine(yt, w_t, xb, mod3, seq):
    n, d = xb.shape
    tm = TM_COMBINE
    tiles_per_seq = seq // tm
    return pl.pallas_call(
        _combine_kernel,
        out_shape=jax.ShapeDtypeStruct((n, d), F32),
        grid=(n // tm,),
        in_specs=[
            pl.BlockSpec((TOP_K, tm, d // 2), lambda i: (0, i, 0)),
            pl.BlockSpec((tm, TOP_K), lambda i: (i, 0)),
            pl.BlockSpec((tm, d), lambda i: (i, 0)),
            pl.BlockSpec((1, N_MOD, d), lambda i: (i // tiles_per_seq, 0, 0)),
        ],
        out_specs=pl.BlockSpec((tm, d), lambda i: (i, 0)),
        compiler_params=pltpu.CompilerParams(vmem_limit_bytes=VMEM_LIMIT),
        name="combine",
    )(yt, w_t, xb, mod3)


def _rotary_tables(seq):
    half = RET_DK // 2
    pos = jnp.arange(seq, dtype=F32)
    inv = ROPE_BASE ** (-jnp.arange(half, dtype=F32) / half)
    ang = pos[:, None] * inv[None, :]
    cos = jnp.cos(ang)
    sin = jnp.sin(ang)
    return jnp.concatenate([cos, cos], axis=-1), jnp.concatenate([-sin, sin], axis=-1)


def _ffn_schedule(counts, total_rows):
    nblk = total_rows // FFN_BLOCK
    ends = jnp.cumsum(counts)
    start = ends - counts
    bp = jnp.sort(jnp.concatenate([jnp.arange(nblk, dtype=I32) * FFN_BLOCK, start]))
    lo = bp
    hi = jnp.concatenate([bp[1:], jnp.full((1,), total_rows, I32)])
    blk = jnp.minimum(lo // FFN_BLOCK, nblk - 1)
    exp = jnp.minimum(jnp.sum((ends[None, :] <= lo[:, None]).astype(I32), axis=1), N_EXPERTS - 1)
    return start, blk, exp, lo - blk * FFN_BLOCK, hi - blk * FFN_BLOCK


def _chunked(dest, chunk):
    k, n = dest.shape
    return dest.reshape(k, n // chunk, chunk).transpose(1, 0, 2)


def kernel(x, c, w_ada, b_ada, g_mix, w_in, q_gain, k_gain, w_out, g_ffn, w_router, router_bias,
           w_gate, w_up, w_down, ws_gate, ws_up, ws_down):
    bsz, seq, d = x.shape
    n = bsz * seq
    depth = w_ada.shape[0]
    cos_full, sin_signed = _rotary_tables(seq)
    log_g = jnp.log1p(-jnp.exp2(-5.0 - jnp.arange(RET_HEADS, dtype=F32)))
    ret_w = RET_HEADS * RET_DK
    x2 = x.reshape(n, d)
    for l in range(depth):
        mod3 = _adaln(c, w_ada[l], b_ada[l]).reshape(bsz, N_MOD, d)
        proj = _inproj(x2, mod3, g_mix[l].reshape(1, d), w_in[l].astype(BF16), cos_full, sin_signed, seq)
        proj3 = proj.reshape(bsz, seq, IN_COLS)
        ret = _retention(log_g, proj3)
        qg2 = jnp.tile(q_gain[l].reshape(1, MOBA_DH), (1, 2))
        kg2 = jnp.tile(k_gain[l].reshape(1, MOBA_DH), (1, 2))
        moba = _moba(proj3, qg2, kg2)
        wo = w_out[l].astype(BF16)
        xb, h2, e_idx, w_k, rank, cnt = _mid(
            ret.reshape(n, ret_w), moba.reshape(n, MOBA_HEADS * MOBA_DH), x2, mod3,
            wo[:ret_w], wo[ret_w:], g_ffn[l].reshape(1, d),
            w_router[l].T.astype(BF16), router_bias[l].reshape(N_EXPERTS, 1),
            ws_gate[l].astype(BF16), ws_up[l].astype(BF16), ws_down[l].astype(BF16), seq)
        start, blk, exp, lo, hi = _ffn_schedule(cnt[:, 0], n * TOP_K)
        dest = _dest(start, e_idx, rank)
        xs = _sc_dispatch(h2, _chunked(dest, SC_CHUNK))
        y = _ffn(blk, exp, lo, hi, xs, w_gate[l], w_up[l], w_down[l])
        yt = _sc_gather(y, _chunked(dest, SC_GATHER_CHUNK))
        x2 = _combine(yt, w_k.T, xb, mod3, seq)
    return x2.reshape(bsz, seq, d)
```

```python
import functools

import numpy as np
import jax
import jax.numpy as jnp
from jax import lax
from jax.experimental import pallas as pl
from jax.experimental.pallas import tpu as pltpu
from jax.experimental.pallas import tpu_sc as plsc

F32 = jnp.float32
BF16 = jnp.bfloat16
I32 = jnp.int32

D_MODEL = 1024
RET_HEADS = 4
RET_DK = 128
MOBA_HEADS = 8
MOBA_DH = 64
MOBA_BLOCK = 256
MOBA_TOPK = 3
ROPE_BASE = 10000.0
N_EXPERTS = 256
TOP_K = 8
N_GROUPS = 8
TOPK_GROUPS = 4
GROUP_SIZE = N_EXPERTS // N_GROUPS
EXPERT_FF = 256
ROUTED_SCALE = 2.5
N_MOD = 6
EPS = 1e-6
IN_COLS = 3584

LANES = 128
RET_CHUNK = 256
TM_PROJ = 512
TM_COMBINE = 256
SC_CORES = 2
SC_SUBCORES = 16
SC_CHUNK = 128
SC_GATHER_CHUNK = 64
FFN_BLOCK = 256
VMEM_LIMIT = 56 * 1024 * 1024

NEG_INF = float("-inf")


def _silu(x):
    return x * jax.nn.sigmoid(x)


def _nt_dot(a, b):
    return lax.dot_general(a, b, (((1,), (1,)), ((), ())), preferred_element_type=F32)


def _tn_dot(a, b):
    return lax.dot_general(a, b, (((0,), (0,)), ((), ())), preferred_element_type=F32)


def _dot(a, b):
    return jnp.dot(a, b, preferred_element_type=F32)


HI_MASK = -65536


def _pack_halves(v):
    w = v.shape[1] // 2
    lo = lax.bitcast_convert_type(v[:, :w].astype(BF16).astype(F32), I32)
    hi = lax.bitcast_convert_type(v[:, w:].astype(BF16).astype(F32), I32)
    return lax.shift_right_logical(lo, 16) | (hi & HI_MASK)


def _unpack_halves(u):
    lo = lax.bitcast_convert_type(lax.shift_left(u, 16), F32)
    hi = lax.bitcast_convert_type(u & HI_MASK, F32)
    return lo, hi


def _adaln_kernel(c_ref, w_ref, b_ref, o_ref):
    s = _silu(c_ref[...])
    o_ref[...] = _dot(s.astype(BF16), w_ref[...].astype(BF16)) + b_ref[...]


def _adaln(c, w_ada, b_ada):
    bsz, d = c.shape
    ncol = w_ada.shape[1]
    tn = 1024
    return pl.pallas_call(
        _adaln_kernel,
        out_shape=jax.ShapeDtypeStruct((bsz, ncol), F32),
        grid=(ncol // tn,),
        in_specs=[
            pl.BlockSpec((bsz, d), lambda j: (0, 0)),
            pl.BlockSpec((d, tn), lambda j: (0, j)),
            pl.BlockSpec((1, tn), lambda j: (0, j)),
        ],
        out_specs=pl.BlockSpec((bsz, tn), lambda j: (0, j)),
        compiler_params=pltpu.CompilerParams(vmem_limit_bytes=VMEM_LIMIT),
        name="adaln",
    )(c, w_ada, b_ada.reshape(1, ncol))


def _inproj_kernel(x_ref, mod_ref, g_ref, w_ref, cos_ref, sin_ref, o_ref):
    x = x_ref[...]
    ms = jnp.mean(x * x, axis=-1, keepdims=True)
    m = mod_ref[0]
    h = (x * lax.rsqrt(ms + EPS) * g_ref[...]) * (1.0 + m[1:2]) + m[0:1]
    hb = h.astype(BF16)
    cosf = cos_ref[...]
    sinf = sin_ref[...]
    k_scale = RET_DK ** -0.5
    width = RET_HEADS * RET_DK
    for ci in range(IN_COLS // width):
        acc = _dot(hb, w_ref[:, ci * width:(ci + 1) * width])
        if ci < 2:
            for hh in range(RET_HEADS):
                xh = acc[:, hh * RET_DK:(hh + 1) * RET_DK]
                r = xh * cosf + pltpu.roll(xh, RET_DK // 2, axis=1) * sinf
                if ci == 1:
                    r = r * k_scale
                o_ref[:, ci * width + hh * RET_DK:ci * width + (hh + 1) * RET_DK] = r.astype(BF16)
        else:
            o_ref[:, ci * width:(ci + 1) * width] = acc.astype(BF16)


def _inproj(x2, mod3, g_mix, w_in_bf, cos_full, sin_signed, seq):
    n, d = x2.shape
    tm = TM_PROJ
    tiles_per_seq = seq // tm
    return pl.pallas_call(
        _inproj_kernel,
        out_shape=jax.ShapeDtypeStruct((n, IN_COLS), BF16),
        grid=(n // tm,),
        in_specs=[
            pl.BlockSpec((tm, d), lambda i: (i, 0)),
            pl.BlockSpec((1, N_MOD, d), lambda i: (i // tiles_per_seq, 0, 0)),
            pl.BlockSpec((1, d), lambda i: (0, 0)),
            pl.BlockSpec((d, IN_COLS), lambda i: (0, 0)),
            pl.BlockSpec((tm, LANES), lambda i: (i % tiles_per_seq, 0)),
            pl.BlockSpec((tm, LANES), lambda i: (i % tiles_per_seq, 0)),
        ],
        out_specs=pl.BlockSpec((tm, IN_COLS), lambda i: (i, 0)),
        compiler_params=pltpu.CompilerParams(vmem_limit_bytes=VMEM_LIMIT),
        name="inproj",
    )(x2, mod3, g_mix, w_in_bf, cos_full, sin_signed)


def _ret_kernel(lg_ref, q_ref, k_ref, v_ref, g_ref, o_ref):
    seq = q_ref.shape[1]
    c = RET_CHUNK
    lg = lg_ref[pl.program_id(1)]
    row = lax.broadcasted_iota(I32, (c, c), 0)
    col = lax.broadcasted_iota(I32, (c, c), 1)
    diff = (row - col).astype(F32)
    dmask = jnp.where(diff >= 0, jnp.exp(lg * jnp.maximum(diff, 0.0)), 0.0)
    idx = lax.broadcasted_iota(I32, (c, 1), 0).astype(F32)
    q_decay = jnp.exp(lg * (idx + 1.0))
    k_decay = jnp.exp(lg * (c - 1.0 - idx))
    chunk_decay = jnp.exp(jnp.full((1, 1), lg * c, F32))
    state = jnp.zeros((RET_DK, RET_DK), F32)
    for n in range(seq // c):
        rows = slice(n * c, (n + 1) * c)
        qn = q_ref[0, rows, :]
        kn = k_ref[0, rows, :]
        vn = v_ref[0, rows, :]
        scores = _nt_dot(qn, kn) * dmask
        inner = _dot(scores.astype(BF16), vn)
        qs = (qn.astype(F32) * q_decay).astype(BF16)
        cross = _dot(qs, state.astype(BF16))
        o = inner + cross
        o = o * lax.rsqrt(jnp.mean(o * o, axis=-1, keepdims=True) + EPS)
        gn = g_ref[0, rows, :].astype(F32)
        o_ref[0, rows, :] = (_silu(gn) * o).astype(BF16)
        ks = (kn.astype(F32) * k_decay).astype(BF16)
        state = state * chunk_decay + _tn_dot(ks, vn)


def _retention(log_g, proj3):
    bsz, seq, _ = proj3.shape
    blk = (1, seq, RET_DK)
    return pl.pallas_call(
        _ret_kernel,
        out_shape=jax.ShapeDtypeStruct((bsz, seq, RET_HEADS * RET_DK), BF16),
        grid_spec=pltpu.PrefetchScalarGridSpec(
            num_scalar_prefetch=1,
            grid=(bsz, RET_HEADS),
            in_specs=[
                pl.BlockSpec(blk, lambda b, h, lg: (b, 0, h)),
                pl.BlockSpec(blk, lambda b, h, lg: (b, 0, RET_HEADS + h)),
                pl.BlockSpec(blk, lambda b, h, lg: (b, 0, 2 * RET_HEADS + h)),
                pl.BlockSpec(blk, lambda b, h, lg: (b, 0, 3 * RET_HEADS + h)),
            ],
            out_specs=pl.BlockSpec(blk, lambda b, h, lg: (b, 0, h)),
        ),
        compiler_params=pltpu.CompilerParams(vmem_limit_bytes=VMEM_LIMIT),
        name="retention",
    )(log_g, proj3, proj3, proj3, proj3)


def _moba_kernel(q_ref, k_ref, v_ref, qg_ref, kg_ref, o_ref, qn_s, kn_s):
    seq = q_ref.shape[1]
    lb = MOBA_BLOCK
    nb = seq // lb
    lane = lax.broadcasted_iota(I32, (1, LANES), 1)
    is_a = lane < MOBA_DH
    halves = (is_a, jnp.logical_not(is_a))

    def head_norm(xf, gain):
        sq = xf * xf
        s_a = jnp.sum(jnp.where(is_a, sq, 0.0), axis=-1, keepdims=True)
        s_b = jnp.sum(jnp.where(is_a, 0.0, sq), axis=-1, keepdims=True)
        inv = jnp.where(is_a, lax.rsqrt(s_a / MOBA_DH + EPS), lax.rsqrt(s_b / MOBA_DH + EPS))
        return xf * inv * gain

    qg = qg_ref[...]
    kg = kg_ref[...]
    k_means = []
    for j in range(nb):
        rows = slice(j * lb, (j + 1) * lb)
        kf = head_norm(k_ref[0, rows, :].astype(F32), kg)
        kn_s[rows, :] = kf.astype(BF16)
        k_means.append(jnp.mean(kf, axis=0, keepdims=True))
        qf = head_norm(q_ref[0, rows, :].astype(F32), qg)
        qn_s[rows, :] = (qf * (MOBA_DH ** -0.5)).astype(BF16)
    k_mean = jnp.concatenate(k_means, axis=0)

    r_loc = lax.broadcasted_iota(I32, (lb, lb), 0)
    c_loc = lax.broadcasted_iota(I32, (lb, lb), 1)
    causal = c_loc <= r_loc

    for i in range(nb):
        rows = slice(i * lb, (i + 1) * lb)
        qi = qn_s[rows, :]
        outs = []
        for hx in range(2):
            q_h = jnp.where(halves[hx], qi, jnp.zeros_like(qi))
            sel = [None] * i
            if i > MOBA_TOPK:
                km_h = jnp.where(halves[hx], k_mean, 0.0).astype(BF16)
                gate = _nt_dot(q_h, km_h)
                cols = [gate[:, j:j + 1] for j in range(i)]
                for j in range(i):
                    rank = jnp.zeros((lb, 1), F32)
                    for j2 in range(i):
                        if j2 == j:
                            continue
                        beats = (cols[j2] >= cols[j]) if j2 < j else (cols[j2] > cols[j])
                        rank = rank + jnp.where(beats, 1.0, 0.0)
                    sel[j] = rank < float(MOBA_TOPK)
            pieces = []
            for j in range(i + 1):
                s = _nt_dot(q_h, kn_s[j * lb:(j + 1) * lb, :])
                if j == i:
                    s = jnp.where(causal, s, NEG_INF)
                elif sel[j] is not None:
                    s = jnp.where(sel[j], s, NEG_INF)
                pieces.append(s)
            mx = jnp.max(pieces[0], axis=-1, keepdims=True)
            for s in pieces[1:]:
                mx = jnp.maximum(mx, jnp.max(s, axis=-1, keepdims=True))
            den = jnp.zeros((lb, 1), F32)
            acc = jnp.zeros((lb, LANES), F32)
            for j, s in enumerate(pieces):
                p = jnp.exp(s - mx)
                den = den + jnp.sum(p, axis=-1, keepdims=True)
                acc = acc + _dot(p.astype(BF16), v_ref[0, j * lb:(j + 1) * lb, :])
            outs.append(acc / den)
        o_ref[0, rows, :] = jnp.where(is_a, outs[0], outs[1]).astype(BF16)


def _moba(proj3, qg2, kg2):
    bsz, seq, _ = proj3.shape
    pairs = MOBA_HEADS // 2
    blk = (1, seq, LANES)
    base = 4 * RET_HEADS
    return pl.pallas_call(
        _moba_kernel,
        out_shape=jax.ShapeDtypeStruct((bsz, seq, MOBA_HEADS * MOBA_DH), BF16),
        grid=(bsz, pairs),
        in_specs=[
            pl.BlockSpec(blk, lambda b, p: (b, 0, base + p)),
            pl.BlockSpec(blk, lambda b, p: (b, 0, base + pairs + p)),
            pl.BlockSpec(blk, lambda b, p: (b, 0, base + 2 * pairs + p)),
            pl.BlockSpec((1, LANES), lambda b, p: (0, 0)),
            pl.BlockSpec((1, LANES), lambda b, p: (0, 0)),
        ],
        out_specs=pl.BlockSpec(blk, lambda b, p: (b, 0, p)),
        scratch_shapes=[pltpu.VMEM((seq, LANES), BF16), pltpu.VMEM((seq, LANES), BF16)],
        compiler_params=pltpu.CompilerParams(vmem_limit_bytes=VMEM_LIMIT),
        name="moba",
    )(proj3, proj3, proj3, qg2, kg2)


def _mid_kernel(ret_ref, moba_ref, x_ref, mod_ref, wo1_ref, wo2_ref, g_ref, wr_ref, rb_ref,
                wsg_ref, wsu_ref, wsd_ref,
                xb_ref, h2_ref, e_ref, w_ref, rk_ref, cnt_ref, carry_s):
    i = pl.program_id(0)
    tm = x_ref.shape[0]

    @pl.when(i == 0)
    def _():
        carry_s[...] = jnp.zeros_like(carry_s)

    m = mod_ref[0]
    mixed = _dot(ret_ref[...], wo1_ref[...]) + _dot(moba_ref[...], wo2_ref[...])
    x1 = x_ref[...] + m[2:3] * mixed
    ms = jnp.mean(x1 * x1, axis=-1, keepdims=True)
    h2 = (x1 * lax.rsqrt(ms + EPS) * g_ref[...]) * (1.0 + m[4:5]) + m[3:4]
    h2_ref[...] = _pack_halves(h2)
    h2b = h2.astype(BF16)

    hid = _silu(_dot(h2b, wsg_ref[...])) * _dot(h2b, wsu_ref[...])
    xb_ref[...] = x1 + m[5:6] * _dot(hid.astype(BF16), wsd_ref[...])

    scores = jax.nn.sigmoid(_nt_dot(wr_ref[...], h2b))
    biased = scores + rb_ref[...]
    grp = biased.reshape(N_GROUPS, GROUP_SIZE, tm)
    gi = lax.broadcasted_iota(I32, (N_GROUPS, GROUP_SIZE, tm), 1).astype(F32)
    top1 = jnp.max(grp, axis=1, keepdims=True)
    first = jnp.min(jnp.where(grp == top1, gi, float(GROUP_SIZE)), axis=1, keepdims=True)
    top2 = jnp.max(jnp.where(gi == first, NEG_INF, grp), axis=1, keepdims=True)
    gscore = (top1 + top2).reshape(N_GROUPS, tm)
    gidx = lax.broadcasted_iota(I32, (N_GROUPS, tm), 0)
    grank = jnp.zeros((N_GROUPS, tm), F32)
    for g2 in range(N_GROUPS):
        rowv = gscore[g2:g2 + 1, :]
        beats = (rowv > gscore) | ((rowv == gscore) & (g2 < gidx))
        grank = grank + jnp.where(beats, 1.0, 0.0)
    gsel = jnp.where(grank < float(TOPK_GROUPS), 1.0, 0.0)
    emask = jnp.broadcast_to(gsel.reshape(N_GROUPS, 1, tm), (N_GROUPS, GROUP_SIZE, tm)).reshape(N_EXPERTS, tm)
    choice = jnp.where(emask > 0.5, biased, NEG_INF)

    eidx = lax.broadcasted_iota(I32, (N_EXPERTS, tm), 0).astype(F32)
    selmask = jnp.zeros((N_EXPERTS, tm), F32)
    e_rows = []
    w_rows = []
    for _k in range(TOP_K):
        mx = jnp.max(choice, axis=0, keepdims=True)
        idx = jnp.min(jnp.where(choice == mx, eidx, float(N_EXPERTS)), axis=0, keepdims=True)
        onehot = eidx == idx
        e_rows.append(idx)
        w_rows.append(jnp.sum(jnp.where(onehot, scores, 0.0), axis=0, keepdims=True))
        selmask = selmask + jnp.where(onehot, 1.0, 0.0)
        choice = jnp.where(onehot, NEG_INF, choice)
    wsum = w_rows[0]
    for wk in w_rows[1:]:
        wsum = wsum + wk

    tr = lax.broadcasted_iota(I32, (tm, tm), 0)
    tc = lax.broadcasted_iota(I32, (tm, tm), 1)
    upper = jnp.where(tr < tc, 1.0, 0.0).astype(BF16)
    prefix = _dot(selmask.astype(BF16), upper) + carry_s[...]
    r_rows = [jnp.sum(jnp.where(eidx == ek, prefix, 0.0), axis=0, keepdims=True) for ek in e_rows]
    carry_s[...] = carry_s[...] + jnp.sum(selmask, axis=1, keepdims=True)

    e_ref[...] = jnp.concatenate(e_rows, axis=0).astype(I32)
    w_ref[...] = jnp.concatenate([wk / wsum * ROUTED_SCALE for wk in w_rows], axis=0)
    rk_ref[...] = jnp.concatenate(r_rows, axis=0).astype(I32)
    cnt_ref[...] = carry_s[...].astype(I32)


def _mid(ret2, moba2, x2, mod3, wo1, wo2, g_ffn, wr_t, rbias, wsg, wsu, wsd, seq):
    n, d = x2.shape
    tm = TM_PROJ
    tiles_per_seq = seq // tm
    half = ret2.shape[1]
    ff = wsg.shape[1]
    const = lambda i: (0, 0)
    row = lambda i: (i, 0)
    colt = lambda i: (0, i)
    return pl.pallas_call(
        _mid_kernel,
        out_shape=(
            jax.ShapeDtypeStruct((n, d), F32),
            jax.ShapeDtypeStruct((n, d // 2), I32),
            jax.ShapeDtypeStruct((TOP_K, n), I32),
            jax.ShapeDtypeStruct((TOP_K, n), F32),
            jax.ShapeDtypeStruct((TOP_K, n), I32),
            jax.ShapeDtypeStruct((N_EXPERTS, 1), I32),
        ),
        grid=(n // tm,),
        in_specs=[
            pl.BlockSpec((tm, half), row),
            pl.BlockSpec((tm, half), row),
            pl.BlockSpec((tm, d), row),
            pl.BlockSpec((1, N_MOD, d), lambda i: (i // tiles_per_seq, 0, 0)),
            pl.BlockSpec((half, d), const),
            pl.BlockSpec((half, d), const),
            pl.BlockSpec((1, d), const),
            pl.BlockSpec((N_EXPERTS, d), const),
            pl.BlockSpec((N_EXPERTS, 1), const),
            pl.BlockSpec((d, ff), const),
            pl.BlockSpec((d, ff), const),
            pl.BlockSpec((ff, d), const),
        ],
        out_specs=(
            pl.BlockSpec((tm, d), row),
            pl.BlockSpec((tm, d // 2), row),
            pl.BlockSpec((TOP_K, tm), colt),
            pl.BlockSpec((TOP_K, tm), colt),
            pl.BlockSpec((TOP_K, tm), colt),
            pl.BlockSpec((N_EXPERTS, 1), const),
        ),
        scratch_shapes=[pltpu.VMEM((N_EXPERTS, 1), F32)],
        compiler_params=pltpu.CompilerParams(
            dimension_semantics=("arbitrary",), vmem_limit_bytes=VMEM_LIMIT),
        name="mid",
    )(ret2, moba2, x2, mod3, wo1, wo2, g_ffn, wr_t, rbias, wsg, wsu, wsd)


def _dest_kernel(start_ref, e_ref, rk_ref, o_ref):
    e = e_ref[...]

    def body(ex, acc):
        return acc + jnp.where(e == ex, start_ref[ex], 0)

    o_ref[...] = lax.fori_loop(0, N_EXPERTS, body, rk_ref[...], unroll=8)


def _dest(start, e_idx, rank):
    k, n = e_idx.shape
    tn = 2048
    return pl.pallas_call(
        _dest_kernel,
        out_shape=jax.ShapeDtypeStruct((k, n), I32),
        grid_spec=pltpu.PrefetchScalarGridSpec(
            num_scalar_prefetch=1,
            grid=(n // tn,),
            in_specs=[
                pl.BlockSpec((k, tn), lambda i, s: (0, i)),
                pl.BlockSpec((k, tn), lambda i, s: (0, i)),
            ],
            out_specs=pl.BlockSpec((k, tn), lambda i, s: (0, i)),
        ),
        compiler_params=pltpu.CompilerParams(vmem_limit_bytes=VMEM_LIMIT),
        name="dest",
    )(start, e_idx, rank)


def _sc_dispatch(h2p, dest3, total_rows):
    n, words = h2p.shape
    nchunks = n // SC_CHUNK
    per_worker = nchunks // (SC_CORES * SC_SUBCORES)
    mesh = plsc.VectorSubcoreMesh(core_axis_name="c", subcore_axis_name="s",
                                  num_cores=SC_CORES, num_subcores=SC_SUBCORES)

    @functools.partial(
        pl.kernel, mesh=mesh,
        out_type=jax.ShapeDtypeStruct((total_rows, words), I32),
        scratch_types=[
            pltpu.VMEM((TOP_K, SC_CHUNK), I32),
            pltpu.VMEM((SC_CHUNK, words), I32),
            pltpu.SemaphoreType.DMA,
        ],
        name="sc_dispatch",
    )
    def run(h_hbm, d_hbm, xs_hbm, idx_v, rows_v, sem):
        wid = lax.axis_index("s") * SC_CORES + lax.axis_index("c")

        @pl.loop(0, per_worker)
        def _(j):
            ch = wid * per_worker + j
            pltpu.sync_copy(d_hbm.at[ch], idx_v)
            pltpu.sync_copy(h_hbm.at[pl.ds(ch * SC_CHUNK, SC_CHUNK)], rows_v)
            copies = [pltpu.async_copy(rows_v, xs_hbm.at[idx_v.at[k]], sem) for k in range(TOP_K)]
            for cp in copies:
                cp.wait()

    return run(h2p, dest3)


def _ffn_kernel(nreal_ref, blk_ref, exp_ref, valid_ref, x_ref, wg_ref, wu_ref, wd_ref, y_ref):
    i = pl.program_id(0)

    @pl.when(i < nreal_ref[0])
    def _():
        half = x_ref.shape[1]
        r = lax.broadcasted_iota(I32, (x_ref.shape[0], 1), 0)
        x_lo, x_hi = _unpack_halves(jnp.where(r < valid_ref[i], x_ref[...], 0))
        wg = wg_ref[0]
        wu = wu_ref[0]
        hg = _dot(x_lo, wg[:half]) + _dot(x_hi, wg[half:])
        hu = _dot(x_lo, wu[:half]) + _dot(x_hi, wu[half:])
        y_ref[...] = _pack_halves(_dot(_silu(hg) * hu, wd_ref[0]))


def _ffn(nreal, blk, exp, valid, xs, w_gate, w_up, w_down):
    p, half = xs.shape
    d = 2 * half
    ff = w_gate.shape[2]
    steps = blk.shape[0]
    return pl.pallas_call(
        _ffn_kernel,
        out_shape=jax.ShapeDtypeStruct((p, half), I32),
        grid_spec=pltpu.PrefetchScalarGridSpec(
            num_scalar_prefetch=4,
            grid=(steps,),
            in_specs=[
                pl.BlockSpec((FFN_BLOCK, half), lambda i, n, b, e, v: (b[i], 0)),
                pl.BlockSpec((1, d, ff), lambda i, n, b, e, v: (e[i], 0, 0)),
                pl.BlockSpec((1, d, ff), lambda i, n, b, e, v: (e[i], 0, 0)),
                pl.BlockSpec((1, ff, d), lambda i, n, b, e, v: (e[i], 0, 0)),
            ],
            out_specs=pl.BlockSpec((FFN_BLOCK, half), lambda i, n, b, e, v: (b[i], 0)),
        ),
        compiler_params=pltpu.CompilerParams(
            dimension_semantics=("arbitrary",), vmem_limit_bytes=VMEM_LIMIT),
        name="ffn",
    )(nreal, blk, exp, valid, xs, w_gate, w_up, w_down)


def _sc_gather(y, dest3):
    a, words = y.shape
    nchunks, _, chunk = dest3.shape
    n = nchunks * chunk
    per_worker = nchunks // (SC_CORES * SC_SUBCORES)
    mesh = plsc.VectorSubcoreMesh(core_axis_name="c", subcore_axis_name="s",
                                  num_cores=SC_CORES, num_subcores=SC_SUBCORES)

    @functools.partial(
        pl.kernel, mesh=mesh,
        out_type=jax.ShapeDtypeStruct((TOP_K, n, words), I32),
        scratch_types=[
            pltpu.VMEM((TOP_K, chunk), I32),
            pltpu.VMEM((chunk, words), I32),
            pltpu.VMEM((chunk, words), I32),
            pltpu.SemaphoreType.DMA,
            pltpu.SemaphoreType.DMA((2,)),
        ],
        name="sc_gather",
    )
    def run(y_hbm, d_hbm, yt_hbm, idx_v, buf0, buf1, sem_g, sem_w):
        wid = lax.axis_index("s") * SC_CORES + lax.axis_index("c")
        bufs = (buf0, buf1)

        @pl.loop(0, per_worker)
        def _(j):
            ch = wid * per_worker + j
            pltpu.sync_copy(d_hbm.at[ch], idx_v)
            rows = pl.ds(ch * chunk, chunk)
            gather = pltpu.async_copy(y_hbm.at[idx_v.at[0]], bufs[0], sem_g)
            writes = []
            for k in range(TOP_K):
                gather.wait()
                writes.append(pltpu.async_copy(bufs[k % 2], yt_hbm.at[k, rows], sem_w.at[k % 2]))
                if k + 1 < TOP_K:
                    if k >= 1:
                        writes[k - 1].wait()
                    gather = pltpu.async_copy(y_hbm.at[idx_v.at[k + 1]], bufs[(k + 1) % 2], sem_g)
            writes[TOP_K - 2].wait()
            writes[TOP_K - 1].wait()

    return run(y, dest3)


def _combine_kernel(yt_ref, wt_ref, xb_ref, mod_ref, o_ref):
    half = yt_ref.shape[2]
    wt = wt_ref[...]
    lo, hi = _unpack_halves(yt_ref[0])
    r_lo = lo * wt[:, 0:1]
    r_hi = hi * wt[:, 0:1]
    for k in range(1, TOP_K):
        lo, hi = _unpack_halves(yt_ref[k])
        r_lo = r_lo + lo * wt[:, k:k + 1]
        r_hi = r_hi + hi * wt[:, k:k + 1]
    gate = mod_ref[0][5:6]
    o_ref[:, :half] = xb_ref[:, :half] + gate[:, :half] * r_lo
    o_ref[:, half:] = xb_ref[:, half:] + gate[:, half:] * r_hi


def _combine(yt, w_t, xb, mod3, seq):
    n, d = xb.shape
    tm = TM_COMBINE
    tiles_per_seq = seq // tm
    return pl.pallas_call(
        _combine_kernel,
        out_shape=jax.ShapeDtypeStruct((n, d), F32),
        grid=(n // tm,),
        in_specs=[
            pl.BlockSpec((TOP_K, tm, d // 2), lambda i: (0, i, 0)),
            pl.BlockSpec((tm, TOP_K), lambda i: (i, 0)),
            pl.BlockSpec((tm, d), lambda i: (i, 0)),
            pl.BlockSpec((1, N_MOD, d), lambda i: (i // tiles_per_seq, 0, 0)),
        ],
        out_specs=pl.BlockSpec((tm, d), lambda i: (i, 0)),
        compiler_params=pltpu.CompilerParams(vmem_limit_bytes=VMEM_LIMIT),
        name="combine",
    )(yt, w_t, xb, mod3)


def _rotary_tables(seq):
    half = RET_DK // 2
    pos = jnp.arange(seq, dtype=F32)
    inv = ROPE_BASE ** (-jnp.arange(half, dtype=F32) / half)
    ang = pos[:, None] * inv[None, :]
    cos = jnp.cos(ang)
    sin = jnp.sin(ang)
    return jnp.concatenate([cos, cos], axis=-1), jnp.concatenate([-sin, sin], axis=-1)


def _ffn_schedule(counts, nblk):
    padded = (counts + FFN_BLOCK - 1) // FFN_BLOCK * FFN_BLOCK
    ends = jnp.cumsum(padded)
    start = ends - padded
    nreal = ends[-1:] // FFN_BLOCK
    blk = jnp.minimum(jnp.arange(nblk, dtype=I32), nreal - 1)
    row0 = blk * FFN_BLOCK
    exp = jnp.sum((ends[None, :] <= row0[:, None]).astype(I32), axis=1)
    mine = exp[:, None] == jnp.arange(N_EXPERTS, dtype=I32)[None, :]
    left = jnp.sum(jnp.where(mine, (counts + start)[None, :], 0), axis=1) - row0
    return start, nreal, blk, exp, jnp.clip(left, 0, FFN_BLOCK)


def _chunked(dest, chunk):
    k, n = dest.shape
    return dest.reshape(k, n // chunk, chunk).transpose(1, 0, 2)


def kernel(x, c, w_ada, b_ada, g_mix, w_in, q_gain, k_gain, w_out, g_ffn, w_router, router_bias,
           w_gate, w_up, w_down, ws_gate, ws_up, ws_down):
    bsz, seq, d = x.shape
    n = bsz * seq
    depth = w_ada.shape[0]
    cos_full, sin_signed = _rotary_tables(seq)
    log_g = jnp.log1p(-jnp.exp2(-5.0 - jnp.arange(RET_HEADS, dtype=F32)))
    ret_w = RET_HEADS * RET_DK
    x2 = x.reshape(n, d)
    for l in range(depth):
        mod3 = _adaln(c, w_ada[l], b_ada[l]).reshape(bsz, N_MOD, d)
        proj = _inproj(x2, mod3, g_mix[l].reshape(1, d), w_in[l].astype(BF16), cos_full, sin_signed, seq)
        proj3 = proj.reshape(bsz, seq, IN_COLS)
        ret = _retention(log_g, proj3)
        qg2 = jnp.tile(q_gain[l].reshape(1, MOBA_DH), (1, 2))
        kg2 = jnp.tile(k_gain[l].reshape(1, MOBA_DH), (1, 2))
        moba = _moba(proj3, qg2, kg2)
        wo = w_out[l].astype(BF16)
        xb, h2, e_idx, w_k, rank, cnt = _mid(
            ret.reshape(n, ret_w), moba.reshape(n, MOBA_HEADS * MOBA_DH), x2, mod3,
            wo[:ret_w], wo[ret_w:], g_ffn[l].reshape(1, d),
            w_router[l].T.astype(BF16), router_bias[l].reshape(N_EXPERTS, 1),
            ws_gate[l].astype(BF16), ws_up[l].astype(BF16), ws_down[l].astype(BF16), seq)
        nblk = n * TOP_K // FFN_BLOCK + N_EXPERTS
        start, nreal, blk, exp, valid = _ffn_schedule(cnt[:, 0], nblk)
        dest = _dest(start, e_idx, rank)
        xs = _sc_dispatch(h2, _chunked(dest, SC_CHUNK), nblk * FFN_BLOCK)
        y = _ffn(nreal, blk, exp, valid, xs, w_gate[l], w_up[l], w_down[l])
        yt = _sc_gather(y, _chunked(dest, SC_GATHER_CHUNK))
        x2 = _combine(yt, w_k.T, xb, mod3, seq)
    return x2.reshape(bsz, seq, d)
```

```python
import functools

import numpy as np
import jax
import jax.numpy as jnp
from jax import lax
from jax.experimental import pallas as pl
from jax.experimental.pallas import tpu as pltpu
from jax.experimental.pallas import tpu_sc as plsc

F32 = jnp.float32
BF16 = jnp.bfloat16
I32 = jnp.int32

D_MODEL = 1024
RET_HEADS = 4
RET_DK = 128
MOBA_HEADS = 8
MOBA_DH = 64
MOBA_BLOCK = 256
MOBA_TOPK = 3
ROPE_BASE = 10000.0
N_EXPERTS = 256
TOP_K = 8
N_GROUPS = 8
TOPK_GROUPS = 4
GROUP_SIZE = N_EXPERTS // N_GROUPS
EXPERT_FF = 256
ROUTED_SCALE = 2.5
N_MOD = 6
EPS = 1e-6
IN_COLS = 3584

LANES = 128
RET_CHUNK = 256
TM_PROJ = 512
TM_COMBINE = 256
SC_CORES = 2
SC_SUBCORES = 16
SC_CHUNK = 128
SC_GATHER_CHUNK = 64
FFN_BLOCK = 256
VMEM_LIMIT = 56 * 1024 * 1024

NEG_INF = float("-inf")


def _silu(x):
    return x * jax.nn.sigmoid(x)


def _nt_dot(a, b):
    return lax.dot_general(a, b, (((1,), (1,)), ((), ())), preferred_element_type=F32)


def _tn_dot(a, b):
    return lax.dot_general(a, b, (((0,), (0,)), ((), ())), preferred_element_type=F32)


def _dot(a, b):
    return jnp.dot(a, b, preferred_element_type=F32)


HI_MASK = -65536


def _pack_halves(v):
    w = v.shape[1] // 2
    lo = lax.bitcast_convert_type(v[:, :w].astype(BF16).astype(F32), I32)
    hi = lax.bitcast_convert_type(v[:, w:].astype(BF16).astype(F32), I32)
    return lax.shift_right_logical(lo, 16) | (hi & HI_MASK)


def _unpack_halves(u):
    lo = lax.bitcast_convert_type(lax.shift_left(u, 16), F32)
    hi = lax.bitcast_convert_type(u & HI_MASK, F32)
    return lo, hi


def _adaln_kernel(c_ref, w_ref, b_ref, o_ref):
    s = _silu(c_ref[...])
    o_ref[...] = _dot(s.astype(BF16), w_ref[...].astype(BF16)) + b_ref[...]


def _adaln(c, w_ada, b_ada):
    bsz, d = c.shape
    ncol = w_ada.shape[1]
    tn = 1024
    return pl.pallas_call(
        _adaln_kernel,
        out_shape=jax.ShapeDtypeStruct((bsz, ncol), F32),
        grid=(ncol // tn,),
        in_specs=[
            pl.BlockSpec((bsz, d), lambda j: (0, 0)),
            pl.BlockSpec((d, tn), lambda j: (0, j)),
            pl.BlockSpec((1, tn), lambda j: (0, j)),
        ],
        out_specs=pl.BlockSpec((bsz, tn), lambda j: (0, j)),
        compiler_params=pltpu.CompilerParams(vmem_limit_bytes=VMEM_LIMIT),
        name="adaln",
    )(c, w_ada, b_ada.reshape(1, ncol))


def _inproj_kernel(x_ref, mod_ref, g_ref, w_ref, cos_ref, sin_ref, o_ref):
    x = x_ref[...]
    ms = jnp.mean(x * x, axis=-1, keepdims=True)
    m = mod_ref[0]
    h = (x * lax.rsqrt(ms + EPS) * g_ref[...]) * (1.0 + m[1:2]) + m[0:1]
    hb = h.astype(BF16)
    cosf = cos_ref[...]
    sinf = sin_ref[...]
    k_scale = RET_DK ** -0.5
    width = RET_HEADS * RET_DK
    for ci in range(IN_COLS // width):
        acc = _dot(hb, w_ref[:, ci * width:(ci + 1) * width])
        if ci < 2:
            for hh in range(RET_HEADS):
                xh = acc[:, hh * RET_DK:(hh + 1) * RET_DK]
                r = xh * cosf + pltpu.roll(xh, RET_DK // 2, axis=1) * sinf
                if ci == 1:
                    r = r * k_scale
                o_ref[:, ci * width + hh * RET_DK:ci * width + (hh + 1) * RET_DK] = r.astype(BF16)
        else:
            o_ref[:, ci * width:(ci + 1) * width] = acc.astype(BF16)


def _inproj(x2, mod3, g_mix, w_in_bf, cos_full, sin_signed, seq):
    n, d = x2.shape
    tm = TM_PROJ
    tiles_per_seq = seq // tm
    return pl.pallas_call(
        _inproj_kernel,
        out_shape=jax.ShapeDtypeStruct((n, IN_COLS), BF16),
        grid=(n // tm,),
        in_specs=[
            pl.BlockSpec((tm, d), lambda i: (i, 0)),
            pl.BlockSpec((1, N_MOD, d), lambda i: (i // tiles_per_seq, 0, 0)),
            pl.BlockSpec((1, d), lambda i: (0, 0)),
            pl.BlockSpec((d, IN_COLS), lambda i: (0, 0)),
            pl.BlockSpec((tm, LANES), lambda i: (i % tiles_per_seq, 0)),
            pl.BlockSpec((tm, LANES), lambda i: (i % tiles_per_seq, 0)),
        ],
        out_specs=pl.BlockSpec((tm, IN_COLS), lambda i: (i, 0)),
        compiler_params=pltpu.CompilerParams(vmem_limit_bytes=VMEM_LIMIT),
        name="inproj",
    )(x2, mod3, g_mix, w_in_bf, cos_full, sin_signed)


def _ret_kernel(lg_ref, q_ref, k_ref, v_ref, g_ref, o_ref):
    seq = q_ref.shape[1]
    c = RET_CHUNK
    lg = lg_ref[pl.program_id(1)]
    row = lax.broadcasted_iota(I32, (c, c), 0)
    col = lax.broadcasted_iota(I32, (c, c), 1)
    diff = (row - col).astype(F32)
    dmask = jnp.where(diff >= 0, jnp.exp(lg * jnp.maximum(diff, 0.0)), 0.0)
    idx = lax.broadcasted_iota(I32, (c, 1), 0).astype(F32)
    q_decay = jnp.exp(lg * (idx + 1.0))
    k_decay = jnp.exp(lg * (c - 1.0 - idx))
    chunk_decay = jnp.exp(jnp.full((1, 1), lg * c, F32))
    state = jnp.zeros((RET_DK, RET_DK), F32)
    for n in range(seq // c):
        rows = slice(n * c, (n + 1) * c)
        qn = q_ref[0, rows, :]
        kn = k_ref[0, rows, :]
        vn = v_ref[0, rows, :]
        scores = _nt_dot(qn, kn) * dmask
        inner = _dot(scores.astype(BF16), vn)
        qs = (qn.astype(F32) * q_decay).astype(BF16)
        cross = _dot(qs, state.astype(BF16))
        o = inner + cross
        o = o * lax.rsqrt(jnp.mean(o * o, axis=-1, keepdims=True) + EPS)
        gn = g_ref[0, rows, :].astype(F32)
        o_ref[0, rows, :] = (_silu(gn) * o).astype(BF16)
        ks = (kn.astype(F32) * k_decay).astype(BF16)
        state = state * chunk_decay + _tn_dot(ks, vn)


def _retention(log_g, proj3):
    bsz, seq, _ = proj3.shape
    blk = (1, seq, RET_DK)
    return pl.pallas_call(
        _ret_kernel,
        out_shape=jax.ShapeDtypeStruct((bsz, seq, RET_HEADS * RET_DK), BF16),
        grid_spec=pltpu.PrefetchScalarGridSpec(
            num_scalar_prefetch=1,
            grid=(bsz, RET_HEADS),
            in_specs=[
                pl.BlockSpec(blk, lambda b, h, lg: (b, 0, h)),
                pl.BlockSpec(blk, lambda b, h, lg: (b, 0, RET_HEADS + h)),
                pl.BlockSpec(blk, lambda b, h, lg: (b, 0, 2 * RET_HEADS + h)),
                pl.BlockSpec(blk, lambda b, h, lg: (b, 0, 3 * RET_HEADS + h)),
            ],
            out_specs=pl.BlockSpec(blk, lambda b, h, lg: (b, 0, h)),
        ),
        compiler_params=pltpu.CompilerParams(vmem_limit_bytes=VMEM_LIMIT),
        name="retention",
    )(log_g, proj3, proj3, proj3, proj3)


def _moba_kernel(q_ref, k_ref, v_ref, qg_ref, kg_ref, o_ref, qn_s, kn_s):
    seq = q_ref.shape[1]
    lb = MOBA_BLOCK
    nb = seq // lb
    lane = lax.broadcasted_iota(I32, (1, LANES), 1)
    is_a = lane < MOBA_DH
    halves = (is_a, jnp.logical_not(is_a))

    def head_norm(xf, gain):
        sq = xf * xf
        s_a = jnp.sum(jnp.where(is_a, sq, 0.0), axis=-1, keepdims=True)
        s_b = jnp.sum(jnp.where(is_a, 0.0, sq), axis=-1, keepdims=True)
        inv = jnp.where(is_a, lax.rsqrt(s_a / MOBA_DH + EPS), lax.rsqrt(s_b / MOBA_DH + EPS))
        return xf * inv * gain

    qg = qg_ref[...]
    kg = kg_ref[...]
    k_means = []
    for j in range(nb):
        rows = slice(j * lb, (j + 1) * lb)
        kf = head_norm(k_ref[0, rows, :].astype(F32), kg)
        kn_s[rows, :] = kf.astype(BF16)
        k_means.append(jnp.mean(kf, axis=0, keepdims=True))
        qf = head_norm(q_ref[0, rows, :].astype(F32), qg)
        qn_s[rows, :] = (qf * (MOBA_DH ** -0.5)).astype(BF16)
    k_mean = jnp.concatenate(k_means, axis=0)

    r_loc = lax.broadcasted_iota(I32, (lb, lb), 0)
    c_loc = lax.broadcasted_iota(I32, (lb, lb), 1)
    causal = c_loc <= r_loc

    for i in range(nb):
        rows = slice(i * lb, (i + 1) * lb)
        qi = qn_s[rows, :]
        outs = []
        for hx in range(2):
            q_h = jnp.where(halves[hx], qi, jnp.zeros_like(qi))
            sel = [None] * i
            if i > MOBA_TOPK:
                km_h = jnp.where(halves[hx], k_mean, 0.0).astype(BF16)
                gate = _nt_dot(q_h, km_h)
                cols = [gate[:, j:j + 1] for j in range(i)]
                for j in range(i):
                    rank = jnp.zeros((lb, 1), F32)
                    for j2 in range(i):
                        if j2 == j:
                            continue
                        beats = (cols[j2] >= cols[j]) if j2 < j else (cols[j2] > cols[j])
                        rank = rank + jnp.where(beats, 1.0, 0.0)
                    sel[j] = rank < float(MOBA_TOPK)
            pieces = []
            for j in range(i + 1):
                s = _nt_dot(q_h, kn_s[j * lb:(j + 1) * lb, :])
                if j == i:
                    s = jnp.where(causal, s, NEG_INF)
                elif sel[j] is not None:
                    s = jnp.where(sel[j], s, NEG_INF)
                pieces.append(s)
            mx = jnp.max(pieces[0], axis=-1, keepdims=True)
            for s in pieces[1:]:
                mx = jnp.maximum(mx, jnp.max(s, axis=-1, keepdims=True))
            den = jnp.zeros((lb, 1), F32)
            acc = jnp.zeros((lb, LANES), F32)
            for j, s in enumerate(pieces):
                p = jnp.exp(s - mx)
                den = den + jnp.sum(p, axis=-1, keepdims=True)
                acc = acc + _dot(p.astype(BF16), v_ref[0, j * lb:(j + 1) * lb, :])
            outs.append(acc / den)
        o_ref[0, rows, :] = jnp.where(is_a, outs[0], outs[1]).astype(BF16)


def _moba(proj3, qg2, kg2):
    bsz, seq, _ = proj3.shape
    pairs = MOBA_HEADS // 2
    blk = (1, seq, LANES)
    base = 4 * RET_HEADS
    return pl.pallas_call(
        _moba_kernel,
        out_shape=jax.ShapeDtypeStruct((bsz, seq, MOBA_HEADS * MOBA_DH), BF16),
        grid=(bsz, pairs),
        in_specs=[
            pl.BlockSpec(blk, lambda b, p: (b, 0, base + p)),
            pl.BlockSpec(blk, lambda b, p: (b, 0, base + pairs + p)),
            pl.BlockSpec(blk, lambda b, p: (b, 0, base + 2 * pairs + p)),
            pl.BlockSpec((1, LANES), lambda b, p: (0, 0)),
            pl.BlockSpec((1, LANES), lambda b, p: (0, 0)),
        ],
        out_specs=pl.BlockSpec(blk, lambda b, p: (b, 0, p)),
        scratch_shapes=[pltpu.VMEM((seq, LANES), BF16), pltpu.VMEM((seq, LANES), BF16)],
        compiler_params=pltpu.CompilerParams(vmem_limit_bytes=VMEM_LIMIT),
        name="moba",
    )(proj3, proj3, proj3, qg2, kg2)


def _mid_kernel(ret_ref, moba_ref, x_ref, mod_ref, wo1_ref, wo2_ref, g_ref, wr_ref, rb_ref,
                wsg_ref, wsu_ref, wsd_ref,
                xb_ref, h2_ref, e_ref, w_ref, rk_ref, cnt_ref, carry_s):
    i = pl.program_id(0)
    tm = x_ref.shape[0]

    @pl.when(i == 0)
    def _():
        carry_s[...] = jnp.zeros_like(carry_s)

    m = mod_ref[0]
    mixed = _dot(ret_ref[...], wo1_ref[...]) + _dot(moba_ref[...], wo2_ref[...])
    x1 = x_ref[...] + m[2:3] * mixed
    ms = jnp.mean(x1 * x1, axis=-1, keepdims=True)
    h2 = (x1 * lax.rsqrt(ms + EPS) * g_ref[...]) * (1.0 + m[4:5]) + m[3:4]
    h2_ref[...] = _pack_halves(h2)
    h2b = h2.astype(BF16)

    hid = _silu(_dot(h2b, wsg_ref[...])) * _dot(h2b, wsu_ref[...])
    xb_ref[...] = x1 + m[5:6] * _dot(hid.astype(BF16), wsd_ref[...])

    scores = jax.nn.sigmoid(_nt_dot(wr_ref[...], h2b))
    biased = scores + rb_ref[...]
    grp = biased.reshape(N_GROUPS, GROUP_SIZE, tm)
    gi = lax.broadcasted_iota(I32, (N_GROUPS, GROUP_SIZE, tm), 1).astype(F32)
    top1 = jnp.max(grp, axis=1, keepdims=True)
    first = jnp.min(jnp.where(grp == top1, gi, float(GROUP_SIZE)), axis=1, keepdims=True)
    top2 = jnp.max(jnp.where(gi == first, NEG_INF, grp), axis=1, keepdims=True)
    gscore = (top1 + top2).reshape(N_GROUPS, tm)
    gidx = lax.broadcasted_iota(I32, (N_GROUPS, tm), 0)
    grank = jnp.zeros((N_GROUPS, tm), F32)
    for g2 in range(N_GROUPS):
        rowv = gscore[g2:g2 + 1, :]
        beats = (rowv > gscore) | ((rowv == gscore) & (g2 < gidx))
        grank = grank + jnp.where(beats, 1.0, 0.0)
    gsel = jnp.where(grank < float(TOPK_GROUPS), 1.0, 0.0)
    emask = jnp.broadcast_to(gsel.reshape(N_GROUPS, 1, tm), (N_GROUPS, GROUP_SIZE, tm)).reshape(N_EXPERTS, tm)
    choice = jnp.where(emask > 0.5, biased, NEG_INF)

    eidx = lax.broadcasted_iota(I32, (N_EXPERTS, tm), 0).astype(F32)
    selmask = jnp.zeros((N_EXPERTS, tm), F32)
    e_rows = []
    w_rows = []
    for _k in range(TOP_K):
        mx = jnp.max(choice, axis=0, keepdims=True)
        idx = jnp.min(jnp.where(choice == mx, eidx, float(N_EXPERTS)), axis=0, keepdims=True)
        onehot = eidx == idx
        e_rows.append(idx)
        w_rows.append(jnp.sum(jnp.where(onehot, scores, 0.0), axis=0, keepdims=True))
        selmask = selmask + jnp.where(onehot, 1.0, 0.0)
        choice = jnp.where(onehot, NEG_INF, choice)
    wsum = w_rows[0]
    for wk in w_rows[1:]:
        wsum = wsum + wk

    tr = lax.broadcasted_iota(I32, (tm, tm), 0)
    tc = lax.broadcasted_iota(I32, (tm, tm), 1)
    upper = jnp.where(tr < tc, 1.0, 0.0).astype(BF16)
    prefix = _dot(selmask.astype(BF16), upper) + carry_s[...]
    r_rows = [jnp.sum(jnp.where(eidx == ek, prefix, 0.0), axis=0, keepdims=True) for ek in e_rows]
    carry_s[...] = carry_s[...] + jnp.sum(selmask, axis=1, keepdims=True)

    e_ref[...] = jnp.concatenate(e_rows, axis=0).astype(I32)
    w_ref[...] = jnp.concatenate([wk / wsum * ROUTED_SCALE for wk in w_rows], axis=0)
    rk_ref[...] = jnp.concatenate(r_rows, axis=0).astype(I32)
    cnt_ref[...] = carry_s[...].astype(I32)


def _mid(ret2, moba2, x2, mod3, wo1, wo2, g_ffn, wr_t, rbias, wsg, wsu, wsd, seq):
    n, d = x2.shape
    tm = TM_PROJ
    tiles_per_seq = seq // tm
    half = ret2.shape[1]
    ff = wsg.shape[1]
    const = lambda i: (0, 0)
    row = lambda i: (i, 0)
    colt = lambda i: (0, i)
    return pl.pallas_call(
        _mid_kernel,
        out_shape=(
            jax.ShapeDtypeStruct((n, d), F32),
            jax.ShapeDtypeStruct((n, d // 2), I32),
            jax.ShapeDtypeStruct((TOP_K, n), I32),
            jax.ShapeDtypeStruct((TOP_K, n), F32),
            jax.ShapeDtypeStruct((TOP_K, n), I32),
            jax.ShapeDtypeStruct((N_EXPERTS, 1), I32),
        ),
        grid=(n // tm,),
        in_specs=[
            pl.BlockSpec((tm, half), row),
            pl.BlockSpec((tm, half), row),
            pl.BlockSpec((tm, d), row),
            pl.BlockSpec((1, N_MOD, d), lambda i: (i // tiles_per_seq, 0, 0)),
            pl.BlockSpec((half, d), const),
            pl.BlockSpec((half, d), const),
            pl.BlockSpec((1, d), const),
            pl.BlockSpec((N_EXPERTS, d), const),
            pl.BlockSpec((N_EXPERTS, 1), const),
            pl.BlockSpec((d, ff), const),
            pl.BlockSpec((d, ff), const),
            pl.BlockSpec((ff, d), const),
        ],
        out_specs=(
            pl.BlockSpec((tm, d), row),
            pl.BlockSpec((tm, d // 2), row),
            pl.BlockSpec((TOP_K, tm), colt),
            pl.BlockSpec((TOP_K, tm), colt),
            pl.BlockSpec((TOP_K, tm), colt),
            pl.BlockSpec((N_EXPERTS, 1), const),
        ),
        scratch_shapes=[pltpu.VMEM((N_EXPERTS, 1), F32)],
        compiler_params=pltpu.CompilerParams(
            dimension_semantics=("arbitrary",), vmem_limit_bytes=VMEM_LIMIT),
        name="mid",
    )(ret2, moba2, x2, mod3, wo1, wo2, g_ffn, wr_t, rbias, wsg, wsu, wsd)


def _dest_kernel(start_ref, e_ref, rk_ref, o_ref):
    e = e_ref[...]

    def body(ex, acc):
        return acc + jnp.where(e == ex, start_ref[ex], 0)

    o_ref[...] = lax.fori_loop(0, N_EXPERTS, body, rk_ref[...], unroll=8)


def _dest(start, e_idx, rank):
    k, n = e_idx.shape
    tn = 2048
    return pl.pallas_call(
        _dest_kernel,
        out_shape=jax.ShapeDtypeStruct((k, n), I32),
        grid_spec=pltpu.PrefetchScalarGridSpec(
            num_scalar_prefetch=1,
            grid=(n // tn,),
            in_specs=[
                pl.BlockSpec((k, tn), lambda i, s: (0, i)),
                pl.BlockSpec((k, tn), lambda i, s: (0, i)),
            ],
            out_specs=pl.BlockSpec((k, tn), lambda i, s: (0, i)),
        ),
        compiler_params=pltpu.CompilerParams(vmem_limit_bytes=VMEM_LIMIT),
        name="dest",
    )(start, e_idx, rank)


def _sc_dispatch(h2p, dest3, total_rows):
    n, words = h2p.shape
    nchunks = n // SC_CHUNK
    per_worker = nchunks // (SC_CORES * SC_SUBCORES)
    mesh = plsc.VectorSubcoreMesh(core_axis_name="c", subcore_axis_name="s",
                                  num_cores=SC_CORES, num_subcores=SC_SUBCORES)

    @functools.partial(
        pl.kernel, mesh=mesh,
        out_type=jax.ShapeDtypeStruct((total_rows, words), I32),
        scratch_types=[
            pltpu.VMEM((TOP_K, SC_CHUNK), I32),
            pltpu.VMEM((SC_CHUNK, words), I32),
            pltpu.SemaphoreType.DMA,
        ],
        name="sc_dispatch",
    )
    def run(h_hbm, d_hbm, xs_hbm, idx_v, rows_v, sem):
        wid = lax.axis_index("s") * SC_CORES + lax.axis_index("c")

        @pl.loop(0, per_worker)
        def _(j):
            ch = wid * per_worker + j
            pltpu.sync_copy(d_hbm.at[ch], idx_v)
            pltpu.sync_copy(h_hbm.at[pl.ds(ch * SC_CHUNK, SC_CHUNK)], rows_v)
            copies = [pltpu.async_copy(rows_v, xs_hbm.at[idx_v.at[k]], sem) for k in range(TOP_K)]
            for cp in copies:
                cp.wait()

    return run(h2p, dest3)


def _ffn_kernel(nreal_ref, blk_ref, exp_ref, valid_ref, first_ref, next_ref, slot_ref,
                x_ref, wg_hbm, wu_hbm, wd_hbm, y_ref, wg_s, wu_s, wd_s, sem):
    i = pl.program_id(0)

    def weight_copies(e, s):
        return (pltpu.make_async_copy(wg_hbm.at[e], wg_s.at[s], sem.at[s, 0]),
                pltpu.make_async_copy(wu_hbm.at[e], wu_s.at[s], sem.at[s, 1]),
                pltpu.make_async_copy(wd_hbm.at[e], wd_s.at[s], sem.at[s, 2]))

    @pl.when(i == 0)
    def _():
        for cp in weight_copies(exp_ref[0], 0):
            cp.start()

    @pl.when(i < nreal_ref[0])
    def _():
        s = slot_ref[i]

        @pl.when(first_ref[i] == 1)
        def _():
            for cp in weight_copies(exp_ref[i], s):
                cp.wait()

            @pl.when(next_ref[i] >= 0)
            def _():
                for cp in weight_copies(next_ref[i], 1 - s):
                    cp.start()

        half = x_ref.shape[1]
        r = lax.broadcasted_iota(I32, (x_ref.shape[0], 1), 0)
        x_lo, x_hi = _unpack_halves(jnp.where(r < valid_ref[i], x_ref[...], 0))
        hg = _dot(x_lo, wg_s[s, :half, :]) + _dot(x_hi, wg_s[s, half:, :])
        hu = _dot(x_lo, wu_s[s, :half, :]) + _dot(x_hi, wu_s[s, half:, :])
        y_ref[...] = _pack_halves(_dot(_silu(hg) * hu, wd_s[s]))


def _ffn(sched, xs, w_gate, w_up, w_down):
    p, half = xs.shape
    d = 2 * half
    ff = w_gate.shape[2]
    steps = sched[1].shape[0]
    row = lambda i, n, b, *_: (b[i], 0)
    return pl.pallas_call(
        _ffn_kernel,
        out_shape=jax.ShapeDtypeStruct((p, half), I32),
        grid_spec=pltpu.PrefetchScalarGridSpec(
            num_scalar_prefetch=len(sched),
            grid=(steps,),
            in_specs=[
                pl.BlockSpec((FFN_BLOCK, half), row),
                pl.BlockSpec(memory_space=pl.ANY),
                pl.BlockSpec(memory_space=pl.ANY),
                pl.BlockSpec(memory_space=pl.ANY),
            ],
            out_specs=pl.BlockSpec((FFN_BLOCK, half), row),
            scratch_shapes=[
                pltpu.VMEM((2, d, ff), F32),
                pltpu.VMEM((2, d, ff), F32),
                pltpu.VMEM((2, ff, d), F32),
                pltpu.SemaphoreType.DMA((2, 3)),
            ],
        ),
        compiler_params=pltpu.CompilerParams(
            dimension_semantics=("arbitrary",), vmem_limit_bytes=VMEM_LIMIT),
        name="ffn",
    )(*sched, xs, w_gate, w_up, w_down)


def _sc_gather(y, dest3):
    a, words = y.shape
    nchunks, _, chunk = dest3.shape
    n = nchunks * chunk
    per_worker = nchunks // (SC_CORES * SC_SUBCORES)
    mesh = plsc.VectorSubcoreMesh(core_axis_name="c", subcore_axis_name="s",
                                  num_cores=SC_CORES, num_subcores=SC_SUBCORES)

    @functools.partial(
        pl.kernel, mesh=mesh,
        out_type=jax.ShapeDtypeStruct((TOP_K, n, words), I32),
        scratch_types=[
            pltpu.VMEM((TOP_K, chunk), I32),
            pltpu.VMEM((chunk, words), I32),
            pltpu.VMEM((chunk, words), I32),
            pltpu.SemaphoreType.DMA,
            pltpu.SemaphoreType.DMA((2,)),
        ],
        name="sc_gather",
    )
    def run(y_hbm, d_hbm, yt_hbm, idx_v, buf0, buf1, sem_g, sem_w):
        wid = lax.axis_index("s") * SC_CORES + lax.axis_index("c")
        bufs = (buf0, buf1)

        @pl.loop(0, per_worker)
        def _(j):
            ch = wid * per_worker + j
            pltpu.sync_copy(d_hbm.at[ch], idx_v)
            rows = pl.ds(ch * chunk, chunk)
            gather = pltpu.async_copy(y_hbm.at[idx_v.at[0]], bufs[0], sem_g)
            writes = []
            for k in range(TOP_K):
                gather.wait()
                writes.append(pltpu.async_copy(bufs[k % 2], yt_hbm.at[k, rows], sem_w.at[k % 2]))
                if k + 1 < TOP_K:
                    if k >= 1:
                        writes[k - 1].wait()
                    gather = pltpu.async_copy(y_hbm.at[idx_v.at[k + 1]], bufs[(k + 1) % 2], sem_g)
            writes[TOP_K - 2].wait()
            writes[TOP_K - 1].wait()

    return run(y, dest3)


def _combine_kernel(yt_ref, wt_ref, xb_ref, mod_ref, o_ref):
    half = yt_ref.shape[2]
    wt = wt_ref[...]
    lo, hi = _unpack_halves(yt_ref[0])
    r_lo = lo * wt[:, 0:1]
    r_hi = hi * wt[:, 0:1]
    for k in range(1, TOP_K):
        lo, hi = _unpack_halves(yt_ref[k])
        r_lo = r_lo + lo * wt[:, k:k + 1]
        r_hi = r_hi + hi * wt[:, k:k + 1]
    gate = mod_ref[0][5:6]
    o_ref[:, :half] = xb_ref[:, :half] + gate[:, :half] * r_lo
    o_ref[:, half:] = xb_ref[:, half:] + gate[:, half:] * r_hi


def _combine(yt, w_t, xb, mod3, seq):
    n, d = xb.shape
    tm = TM_COMBINE
    tiles_per_seq = seq // tm
    return pl.pallas_call(
        _combine_kernel,
        out_shape=jax.ShapeDtypeStruct((n, d), F32),
        grid=(n // tm,),
        in_specs=[
            pl.BlockSpec((TOP_K, tm, d // 2), lambda i: (0, i, 0)),
            pl.BlockSpec((tm, TOP_K), lambda i: (i, 0)),
            pl.BlockSpec((tm, d), lambda i: (i, 0)),
            pl.BlockSpec((1, N_MOD, d), lambda i: (i // tiles_per_seq, 0, 0)),
        ],
        out_specs=pl.BlockSpec((tm, d), lambda i: (i, 0)),
        compiler_params=pltpu.CompilerParams(vmem_limit_bytes=VMEM_LIMIT),
        name="combine",
    )(yt, w_t, xb, mod3)


def _rotary_tables(seq):
    half = RET_DK // 2
    pos = jnp.arange(seq, dtype=F32)
    inv = ROPE_BASE ** (-jnp.arange(half, dtype=F32) / half)
    ang = pos[:, None] * inv[None, :]
    cos = jnp.cos(ang)
    sin = jnp.sin(ang)
    return jnp.concatenate([cos, cos], axis=-1), jnp.concatenate([-sin, sin], axis=-1)


def _ffn_schedule(counts, nblk):
    padded = (counts + FFN_BLOCK - 1) // FFN_BLOCK * FFN_BLOCK
    ends = jnp.cumsum(padded)
    start = ends - padded
    nreal = ends[-1:] // FFN_BLOCK
    blk = jnp.minimum(jnp.arange(nblk, dtype=I32), nreal - 1)
    row0 = blk * FFN_BLOCK
    exp = jnp.sum((ends[None, :] <= row0[:, None]).astype(I32), axis=1)
    eids = jnp.arange(N_EXPERTS, dtype=I32)
    mine = exp[:, None] == eids[None, :]

    def per_block(table):
        return jnp.sum(jnp.where(mine, table[None, :], 0), axis=1)

    valid = jnp.clip(per_block(counts + start) - row0, 0, FFN_BLOCK)
    steps = jnp.arange(nblk, dtype=I32)
    first = ((steps < nreal) & ((steps == 0) | (exp != jnp.roll(exp, 1)))).astype(I32)
    used = counts > 0
    at_or_after = jnp.flip(lax.cummin(jnp.flip(jnp.where(used, eids, N_EXPERTS))))
    nxt = jnp.concatenate([at_or_after[1:], jnp.full((1,), N_EXPERTS, I32)])
    nxt = jnp.where(nxt >= N_EXPERTS, -1, nxt)
    parity = (jnp.cumsum(used.astype(I32)) - 1) % 2
    return start, (nreal, blk, exp, valid, first, per_block(nxt), per_block(parity))


def _chunked(dest, chunk):
    k, n = dest.shape
    return dest.reshape(k, n // chunk, chunk).transpose(1, 0, 2)


def kernel(x, c, w_ada, b_ada, g_mix, w_in, q_gain, k_gain, w_out, g_ffn, w_router, router_bias,
           w_gate, w_up, w_down, ws_gate, ws_up, ws_down):
    bsz, seq, d = x.shape
    n = bsz * seq
    depth = w_ada.shape[0]
    cos_full, sin_signed = _rotary_tables(seq)
    log_g = jnp.log1p(-jnp.exp2(-5.0 - jnp.arange(RET_HEADS, dtype=F32)))
    ret_w = RET_HEADS * RET_DK
    x2 = x.reshape(n, d)
    for l in range(depth):
        mod3 = _adaln(c, w_ada[l], b_ada[l]).reshape(bsz, N_MOD, d)
        proj = _inproj(x2, mod3, g_mix[l].reshape(1, d), w_in[l].astype(BF16), cos_full, sin_signed, seq)
        proj3 = proj.reshape(bsz, seq, IN_COLS)
        ret = _retention(log_g, proj3)
        qg2 = jnp.tile(q_gain[l].reshape(1, MOBA_DH), (1, 2))
        kg2 = jnp.tile(k_gain[l].reshape(1, MOBA_DH), (1, 2))
        moba = _moba(proj3, qg2, kg2)
        wo = w_out[l].astype(BF16)
        xb, h2, e_idx, w_k, rank, cnt = _mid(
            ret.reshape(n, ret_w), moba.reshape(n, MOBA_HEADS * MOBA_DH), x2, mod3,
            wo[:ret_w], wo[ret_w:], g_ffn[l].reshape(1, d),
            w_router[l].T.astype(BF16), router_bias[l].reshape(N_EXPERTS, 1),
            ws_gate[l].astype(BF16), ws_up[l].astype(BF16), ws_down[l].astype(BF16), seq)
        nblk = n * TOP_K // FFN_BLOCK + N_EXPERTS
        start, sched = _ffn_schedule(cnt[:, 0], nblk)
        dest = _dest(start, e_idx, rank)
        xs = _sc_dispatch(h2, _chunked(dest, SC_CHUNK), nblk * FFN_BLOCK)
        y = _ffn(sched, xs, w_gate[l], w_up[l], w_down[l])
        yt = _sc_gather(y, _chunked(dest, SC_GATHER_CHUNK))
        x2 = _combine(yt, w_k.T, xb, mod3, seq)
    return x2.reshape(bsz, seq, d)
```

```python
import functools

import numpy as np
import jax
import jax.numpy as jnp
from jax import lax
from jax.experimental import pallas as pl
from jax.experimental.pallas import tpu as pltpu
from jax.experimental.pallas import tpu_sc as plsc

F32 = jnp.float32
BF16 = jnp.bfloat16
I32 = jnp.int32

D_MODEL = 1024
RET_HEADS = 4
RET_DK = 128
MOBA_HEADS = 8
MOBA_DH = 64
MOBA_BLOCK = 256
MOBA_TOPK = 3
ROPE_BASE = 10000.0
N_EXPERTS = 256
TOP_K = 8
N_GROUPS = 8
TOPK_GROUPS = 4
GROUP_SIZE = N_EXPERTS // N_GROUPS
EXPERT_FF = 256
ROUTED_SCALE = 2.5
N_MOD = 6
EPS = 1e-6
IN_COLS = 3584

LANES = 128
RET_CHUNK = 256
TM_PROJ = 512
TM_COMBINE = 256
SC_CORES = 2
SC_SUBCORES = 16
SC_CHUNK = 128
SC_GATHER_CHUNK = 64
FFN_BLOCK = 256
VMEM_LIMIT = 56 * 1024 * 1024

NEG_INF = float("-inf")


def _silu(x):
    return x * jax.nn.sigmoid(x)


def _nt_dot(a, b):
    return lax.dot_general(a, b, (((1,), (1,)), ((), ())), preferred_element_type=F32)


def _tn_dot(a, b):
    return lax.dot_general(a, b, (((0,), (0,)), ((), ())), preferred_element_type=F32)


def _dot(a, b):
    return jnp.dot(a, b, preferred_element_type=F32)


HI_MASK = -65536


def _pack_halves(v):
    w = v.shape[1] // 2
    lo = lax.bitcast_convert_type(v[:, :w].astype(BF16).astype(F32), I32)
    hi = lax.bitcast_convert_type(v[:, w:].astype(BF16).astype(F32), I32)
    return lax.shift_right_logical(lo, 16) | (hi & HI_MASK)


def _unpack_halves(u):
    lo = lax.bitcast_convert_type(lax.shift_left(u, 16), F32)
    hi = lax.bitcast_convert_type(u & HI_MASK, F32)
    return lo, hi


def _adaln_kernel(c_ref, w_ref, b_ref, o_ref):
    s = _silu(c_ref[...])
    o_ref[...] = _dot(s.astype(BF16), w_ref[...].astype(BF16)) + b_ref[...]


def _adaln(c, w_ada, b_ada):
    bsz, d = c.shape
    ncol = w_ada.shape[1]
    tn = 1024
    return pl.pallas_call(
        _adaln_kernel,
        out_shape=jax.ShapeDtypeStruct((bsz, ncol), F32),
        grid=(ncol // tn,),
        in_specs=[
            pl.BlockSpec((bsz, d), lambda j: (0, 0)),
            pl.BlockSpec((d, tn), lambda j: (0, j)),
            pl.BlockSpec((1, tn), lambda j: (0, j)),
        ],
        out_specs=pl.BlockSpec((bsz, tn), lambda j: (0, j)),
        compiler_params=pltpu.CompilerParams(vmem_limit_bytes=VMEM_LIMIT),
        name="adaln",
    )(c, w_ada, b_ada.reshape(1, ncol))


def _inproj_kernel(x_ref, mod_ref, g_ref, w_ref, cos_ref, sin_ref, o_ref):
    x = x_ref[...]
    ms = jnp.mean(x * x, axis=-1, keepdims=True)
    m = mod_ref[0]
    h = (x * lax.rsqrt(ms + EPS) * g_ref[...]) * (1.0 + m[1:2]) + m[0:1]
    hb = h.astype(BF16)
    cosf = cos_ref[...]
    sinf = sin_ref[...]
    k_scale = RET_DK ** -0.5
    width = RET_HEADS * RET_DK
    for ci in range(IN_COLS // width):
        acc = _dot(hb, w_ref[:, ci * width:(ci + 1) * width])
        if ci < 2:
            for hh in range(RET_HEADS):
                xh = acc[:, hh * RET_DK:(hh + 1) * RET_DK]
                r = xh * cosf + pltpu.roll(xh, RET_DK // 2, axis=1) * sinf
                if ci == 1:
                    r = r * k_scale
                o_ref[:, ci * width + hh * RET_DK:ci * width + (hh + 1) * RET_DK] = r.astype(BF16)
        else:
            o_ref[:, ci * width:(ci + 1) * width] = acc.astype(BF16)


def _inproj(x2, mod3, g_mix, w_in_bf, cos_full, sin_signed, seq):
    n, d = x2.shape
    tm = TM_PROJ
    tiles_per_seq = seq // tm
    return pl.pallas_call(
        _inproj_kernel,
        out_shape=jax.ShapeDtypeStruct((n, IN_COLS), BF16),
        grid=(n // tm,),
        in_specs=[
            pl.BlockSpec((tm, d), lambda i: (i, 0)),
            pl.BlockSpec((1, N_MOD, d), lambda i: (i // tiles_per_seq, 0, 0)),
            pl.BlockSpec((1, d), lambda i: (0, 0)),
            pl.BlockSpec((d, IN_COLS), lambda i: (0, 0)),
            pl.BlockSpec((tm, LANES), lambda i: (i % tiles_per_seq, 0)),
            pl.BlockSpec((tm, LANES), lambda i: (i % tiles_per_seq, 0)),
        ],
        out_specs=pl.BlockSpec((tm, IN_COLS), lambda i: (i, 0)),
        compiler_params=pltpu.CompilerParams(vmem_limit_bytes=VMEM_LIMIT),
        name="inproj",
    )(x2, mod3, g_mix, w_in_bf, cos_full, sin_signed)


def _ret_kernel(lg_ref, q_ref, k_ref, v_ref, g_ref, o_ref):
    seq = q_ref.shape[1]
    c = RET_CHUNK
    lg = lg_ref[pl.program_id(1)]
    row = lax.broadcasted_iota(I32, (c, c), 0)
    col = lax.broadcasted_iota(I32, (c, c), 1)
    diff = (row - col).astype(F32)
    dmask = jnp.where(diff >= 0, jnp.exp(lg * jnp.maximum(diff, 0.0)), 0.0)
    idx = lax.broadcasted_iota(I32, (c, 1), 0).astype(F32)
    q_decay = jnp.exp(lg * (idx + 1.0))
    k_decay = jnp.exp(lg * (c - 1.0 - idx))
    chunk_decay = jnp.exp(jnp.full((1, 1), lg * c, F32))
    state = jnp.zeros((RET_DK, RET_DK), F32)
    for n in range(seq // c):
        rows = slice(n * c, (n + 1) * c)
        qn = q_ref[0, rows, :]
        kn = k_ref[0, rows, :]
        vn = v_ref[0, rows, :]
        scores = _nt_dot(qn, kn) * dmask
        inner = _dot(scores.astype(BF16), vn)
        qs = (qn.astype(F32) * q_decay).astype(BF16)
        cross = _dot(qs, state.astype(BF16))
        o = inner + cross
        o = o * lax.rsqrt(jnp.mean(o * o, axis=-1, keepdims=True) + EPS)
        gn = g_ref[0, rows, :].astype(F32)
        o_ref[0, rows, :] = (_silu(gn) * o).astype(BF16)
        ks = (kn.astype(F32) * k_decay).astype(BF16)
        state = state * chunk_decay + _tn_dot(ks, vn)


def _retention(log_g, proj3):
    bsz, seq, _ = proj3.shape
    blk = (1, seq, RET_DK)
    return pl.pallas_call(
        _ret_kernel,
        out_shape=jax.ShapeDtypeStruct((bsz, seq, RET_HEADS * RET_DK), BF16),
        grid_spec=pltpu.PrefetchScalarGridSpec(
            num_scalar_prefetch=1,
            grid=(bsz, RET_HEADS),
            in_specs=[
                pl.BlockSpec(blk, lambda b, h, lg: (b, 0, h)),
                pl.BlockSpec(blk, lambda b, h, lg: (b, 0, RET_HEADS + h)),
                pl.BlockSpec(blk, lambda b, h, lg: (b, 0, 2 * RET_HEADS + h)),
                pl.BlockSpec(blk, lambda b, h, lg: (b, 0, 3 * RET_HEADS + h)),
            ],
            out_specs=pl.BlockSpec(blk, lambda b, h, lg: (b, 0, h)),
        ),
        compiler_params=pltpu.CompilerParams(vmem_limit_bytes=VMEM_LIMIT),
        name="retention",
    )(log_g, proj3, proj3, proj3, proj3)


def _moba_kernel(q_ref, k_ref, v_ref, qg_ref, kg_ref, o_ref, qt_s, ka_s, kb_s, vta_s, vtb_s):
    seq = q_ref.shape[1]
    lb = MOBA_BLOCK
    nb = seq // lb
    lane = lax.broadcasted_iota(I32, (1, LANES), 1)
    is_a = lane < MOBA_DH
    sub = lax.broadcasted_iota(I32, (LANES, 1), 0)
    top = sub < MOBA_DH

    def head_norm(xf, gain):
        sq = xf * xf
        s_a = jnp.sum(jnp.where(is_a, sq, 0.0), axis=-1, keepdims=True)
        s_b = jnp.sum(jnp.where(is_a, 0.0, sq), axis=-1, keepdims=True)
        inv = jnp.where(is_a, lax.rsqrt(s_a / MOBA_DH + EPS), lax.rsqrt(s_b / MOBA_DH + EPS))
        return xf * inv * gain

    qg = qg_ref[...]
    kg = kg_ref[...]
    k_means = []
    for j in range(nb):
        rows = slice(j * lb, (j + 1) * lb)
        kf = head_norm(k_ref[0, rows, :].astype(F32), kg)
        ka_s[rows, :] = jnp.where(is_a, kf, 0.0).astype(BF16)
        kb_s[rows, :] = jnp.where(is_a, 0.0, kf).astype(BF16)
        k_means.append(jnp.mean(kf, axis=0, keepdims=True))
        qf = head_norm(q_ref[0, rows, :].astype(F32), qg)
        qt_s[:, rows] = (qf * (MOBA_DH ** -0.5)).T.astype(BF16)
        vt = v_ref[0, rows, :].astype(F32).T
        vta_s[:, rows] = jnp.where(top, vt, 1.0).astype(BF16)
        vtb_s[:, rows] = jnp.where(top, 1.0, vt).astype(BF16)
    k_mean = jnp.concatenate(k_means + [jnp.zeros((16 - nb, LANES), F32)], axis=0)
    k_mean_h = (jnp.where(is_a, k_mean, 0.0).astype(BF16), jnp.where(is_a, 0.0, k_mean).astype(BF16))
    k_s = (ka_s, kb_s)
    vt_s = (vta_s, vtb_s)

    r_loc = lax.broadcasted_iota(I32, (lb, lb), 0)
    c_loc = lax.broadcasted_iota(I32, (lb, lb), 1)
    causal = r_loc <= c_loc

    for i in range(nb):
        cols = slice(i * lb, (i + 1) * lb)
        qt = qt_s[:, cols]
        outs = []
        for hx in range(2):
            bias = [None] * i
            if i > MOBA_TOPK:
                gate = _dot(k_mean_h[hx], qt)
                g = [gate[j:j + 1, :] for j in range(i)]
                for j in range(i):
                    rank = jnp.zeros((1, lb), F32)
                    for j2 in range(i):
                        if j2 == j:
                            continue
                        beats = (g[j2] >= g[j]) if j2 < j else (g[j2] > g[j])
                        rank = rank + jnp.where(beats, 1.0, 0.0)
                    bias[j] = jnp.where(rank < float(MOBA_TOPK), 0.0, NEG_INF)
            pieces = []
            for j in range(i + 1):
                s = _dot(k_s[hx][j * lb:(j + 1) * lb, :], qt)
                if j == i:
                    s = jnp.where(causal, s, NEG_INF)
                elif bias[j] is not None:
                    s = s + bias[j]
                pieces.append(s)
            mx = jnp.max(pieces[0], axis=0, keepdims=True)
            for s in pieces[1:]:
                mx = jnp.maximum(mx, jnp.max(s, axis=0, keepdims=True))
            acc = jnp.zeros((LANES, lb), F32)
            for j, s in enumerate(pieces):
                p = jnp.exp(s - mx).astype(BF16)
                acc = acc + _dot(vt_s[hx][:, j * lb:(j + 1) * lb], p)
            den = acc[MOBA_DH:MOBA_DH + 1, :] if hx == 0 else acc[0:1, :]
            outs.append(acc / den)
        o_ref[0, cols, :] = jnp.where(top, outs[0], outs[1]).T.astype(BF16)


def _moba(proj3, qg2, kg2):
    bsz, seq, _ = proj3.shape
    pairs = MOBA_HEADS // 2
    blk = (1, seq, LANES)
    base = 4 * RET_HEADS
    return pl.pallas_call(
        _moba_kernel,
        out_shape=jax.ShapeDtypeStruct((bsz, seq, MOBA_HEADS * MOBA_DH), BF16),
        grid=(bsz, pairs),
        in_specs=[
            pl.BlockSpec(blk, lambda b, p: (b, 0, base + p)),
            pl.BlockSpec(blk, lambda b, p: (b, 0, base + pairs + p)),
            pl.BlockSpec(blk, lambda b, p: (b, 0, base + 2 * pairs + p)),
            pl.BlockSpec((1, LANES), lambda b, p: (0, 0)),
            pl.BlockSpec((1, LANES), lambda b, p: (0, 0)),
        ],
        out_specs=pl.BlockSpec(blk, lambda b, p: (b, 0, p)),
        scratch_shapes=[
            pltpu.VMEM((LANES, seq), BF16),
            pltpu.VMEM((seq, LANES), BF16),
            pltpu.VMEM((seq, LANES), BF16),
            pltpu.VMEM((LANES, seq), BF16),
            pltpu.VMEM((LANES, seq), BF16),
        ],
        compiler_params=pltpu.CompilerParams(vmem_limit_bytes=VMEM_LIMIT),
        name="moba",
    )(proj3, proj3, proj3, qg2, kg2)


def _mid_kernel(ret_ref, moba_ref, x_ref, mod_ref, wo1_ref, wo2_ref, g_ref, wr_ref, rb_ref,
                wsg_ref, wsu_ref, wsd_ref,
                xb_ref, h2_ref, e_ref, w_ref, rk_ref, cnt_ref, carry_s):
    i = pl.program_id(0)
    tm = x_ref.shape[0]

    @pl.when(i == 0)
    def _():
        carry_s[...] = jnp.zeros_like(carry_s)

    m = mod_ref[0]
    mixed = _dot(ret_ref[...], wo1_ref[...]) + _dot(moba_ref[...], wo2_ref[...])
    x1 = x_ref[...] + m[2:3] * mixed
    ms = jnp.mean(x1 * x1, axis=-1, keepdims=True)
    h2 = (x1 * lax.rsqrt(ms + EPS) * g_ref[...]) * (1.0 + m[4:5]) + m[3:4]
    h2_ref[...] = _pack_halves(h2)
    h2b = h2.astype(BF16)

    hid = _silu(_dot(h2b, wsg_ref[...])) * _dot(h2b, wsu_ref[...])
    xb_ref[...] = x1 + m[5:6] * _dot(hid.astype(BF16), wsd_ref[...])

    scores = jax.nn.sigmoid(_nt_dot(wr_ref[...], h2b))
    biased = scores + rb_ref[...]
    grp = biased.reshape(N_GROUPS, GROUP_SIZE, tm)
    gi = lax.broadcasted_iota(I32, (N_GROUPS, GROUP_SIZE, tm), 1).astype(F32)
    top1 = jnp.max(grp, axis=1, keepdims=True)
    first = jnp.min(jnp.where(grp == top1, gi, float(GROUP_SIZE)), axis=1, keepdims=True)
    top2 = jnp.max(jnp.where(gi == first, NEG_INF, grp), axis=1, keepdims=True)
    gscore = (top1 + top2).reshape(N_GROUPS, tm)
    gidx = lax.broadcasted_iota(I32, (N_GROUPS, tm), 0)
    grank = jnp.zeros((N_GROUPS, tm), F32)
    for g2 in range(N_GROUPS):
        rowv = gscore[g2:g2 + 1, :]
        beats = (rowv > gscore) | ((rowv == gscore) & (g2 < gidx))
        grank = grank + jnp.where(beats, 1.0, 0.0)
    gsel = jnp.where(grank < float(TOPK_GROUPS), 1.0, 0.0)
    emask = jnp.broadcast_to(gsel.reshape(N_GROUPS, 1, tm), (N_GROUPS, GROUP_SIZE, tm)).reshape(N_EXPERTS, tm)
    choice = jnp.where(emask > 0.5, biased, NEG_INF)

    eidx = lax.broadcasted_iota(I32, (N_EXPERTS, tm), 0).astype(F32)
    selmask = jnp.zeros((N_EXPERTS, tm), F32)
    e_rows = []
    w_rows = []
    for _k in range(TOP_K):
        mx = jnp.max(choice, axis=0, keepdims=True)
        idx = jnp.min(jnp.where(choice == mx, eidx, float(N_EXPERTS)), axis=0, keepdims=True)
        onehot = eidx == idx
        e_rows.append(idx)
        w_rows.append(jnp.sum(jnp.where(onehot, scores, 0.0), axis=0, keepdims=True))
        selmask = selmask + jnp.where(onehot, 1.0, 0.0)
        choice = jnp.where(onehot, NEG_INF, choice)
    wsum = w_rows[0]
    for wk in w_rows[1:]:
        wsum = wsum + wk

    tr = lax.broadcasted_iota(I32, (tm, tm), 0)
    tc = lax.broadcasted_iota(I32, (tm, tm), 1)
    upper = jnp.where(tr < tc, 1.0, 0.0).astype(BF16)
    prefix = _dot(selmask.astype(BF16), upper) + carry_s[...]
    r_rows = [jnp.sum(jnp.where(eidx == ek, prefix, 0.0), axis=0, keepdims=True) for ek in e_rows]
    carry_s[...] = carry_s[...] + jnp.sum(selmask, axis=1, keepdims=True)

    e_ref[...] = jnp.concatenate(e_rows, axis=0).astype(I32)
    w_ref[...] = jnp.concatenate([wk / wsum * ROUTED_SCALE for wk in w_rows], axis=0)
    rk_ref[...] = jnp.concatenate(r_rows, axis=0).astype(I32)
    cnt_ref[...] = carry_s[...].astype(I32)


def _mid(ret2, moba2, x2, mod3, wo1, wo2, g_ffn, wr_t, rbias, wsg, wsu, wsd, seq):
    n, d = x2.shape
    tm = TM_PROJ
    tiles_per_seq = seq // tm
    half = ret2.shape[1]
    ff = wsg.shape[1]
    const = lambda i: (0, 0)
    row = lambda i: (i, 0)
    colt = lambda i: (0, i)
    return pl.pallas_call(
        _mid_kernel,
        out_shape=(
            jax.ShapeDtypeStruct((n, d), F32),
            jax.ShapeDtypeStruct((n, d // 2), I32),
            jax.ShapeDtypeStruct((TOP_K, n), I32),
            jax.ShapeDtypeStruct((TOP_K, n), F32),
            jax.ShapeDtypeStruct((TOP_K, n), I32),
            jax.ShapeDtypeStruct((N_EXPERTS, 1), I32),
        ),
        grid=(n // tm,),
        in_specs=[
            pl.BlockSpec((tm, half), row),
            pl.BlockSpec((tm, half), row),
            pl.BlockSpec((tm, d), row),
            pl.BlockSpec((1, N_MOD, d), lambda i: (i // tiles_per_seq, 0, 0)),
            pl.BlockSpec((half, d), const),
            pl.BlockSpec((half, d), const),
            pl.BlockSpec((1, d), const),
            pl.BlockSpec((N_EXPERTS, d), const),
            pl.BlockSpec((N_EXPERTS, 1), const),
            pl.BlockSpec((d, ff), const),
            pl.BlockSpec((d, ff), const),
            pl.BlockSpec((ff, d), const),
        ],
        out_specs=(
            pl.BlockSpec((tm, d), row),
            pl.BlockSpec((tm, d // 2), row),
            pl.BlockSpec((TOP_K, tm), colt),
            pl.BlockSpec((TOP_K, tm), colt),
            pl.BlockSpec((TOP_K, tm), colt),
            pl.BlockSpec((N_EXPERTS, 1), const),
        ),
        scratch_shapes=[pltpu.VMEM((N_EXPERTS, 1), F32)],
        compiler_params=pltpu.CompilerParams(
            dimension_semantics=("arbitrary",), vmem_limit_bytes=VMEM_LIMIT),
        name="mid",
    )(ret2, moba2, x2, mod3, wo1, wo2, g_ffn, wr_t, rbias, wsg, wsu, wsd)


def _dest_kernel(start_ref, e_ref, rk_ref, o_ref):
    e = e_ref[...]

    def body(ex, acc):
        return acc + jnp.where(e == ex, start_ref[ex], 0)

    o_ref[...] = lax.fori_loop(0, N_EXPERTS, body, rk_ref[...], unroll=8)


def _dest(start, e_idx, rank):
    k, n = e_idx.shape
    tn = 2048
    return pl.pallas_call(
        _dest_kernel,
        out_shape=jax.ShapeDtypeStruct((k, n), I32),
        grid_spec=pltpu.PrefetchScalarGridSpec(
            num_scalar_prefetch=1,
            grid=(n // tn,),
            in_specs=[
                pl.BlockSpec((k, tn), lambda i, s: (0, i)),
                pl.BlockSpec((k, tn), lambda i, s: (0, i)),
            ],
            out_specs=pl.BlockSpec((k, tn), lambda i, s: (0, i)),
        ),
        compiler_params=pltpu.CompilerParams(vmem_limit_bytes=VMEM_LIMIT),
        name="dest",
    )(start, e_idx, rank)


def _sc_dispatch(h2p, dest3, total_rows):
    n, words = h2p.shape
    nchunks = n // SC_CHUNK
    per_worker = nchunks // (SC_CORES * SC_SUBCORES)
    mesh = plsc.VectorSubcoreMesh(core_axis_name="c", subcore_axis_name="s",
                                  num_cores=SC_CORES, num_subcores=SC_SUBCORES)

    @functools.partial(
        pl.kernel, mesh=mesh,
        out_type=jax.ShapeDtypeStruct((total_rows, words), I32),
        scratch_types=[
            pltpu.VMEM((TOP_K, SC_CHUNK), I32),
            pltpu.VMEM((SC_CHUNK, words), I32),
            pltpu.SemaphoreType.DMA,
        ],
        name="sc_dispatch",
    )
    def run(h_hbm, d_hbm, xs_hbm, idx_v, rows_v, sem):
        wid = lax.axis_index("s") * SC_CORES + lax.axis_index("c")

        @pl.loop(0, per_worker)
        def _(j):
            ch = wid * per_worker + j
            pltpu.sync_copy(d_hbm.at[ch], idx_v)
            pltpu.sync_copy(h_hbm.at[pl.ds(ch * SC_CHUNK, SC_CHUNK)], rows_v)
            copies = [pltpu.async_copy(rows_v, xs_hbm.at[idx_v.at[k]], sem) for k in range(TOP_K)]
            for cp in copies:
                cp.wait()

    return run(h2p, dest3)


def _ffn_kernel(nreal_ref, blk_ref, exp_ref, valid_ref, first_ref, next_ref, slot_ref,
                x_ref, wg_hbm, wu_hbm, wd_hbm, y_ref, wg_s, wu_s, wd_s, sem):
    i = pl.program_id(0)

    def weight_copies(e, s):
        return (pltpu.make_async_copy(wg_hbm.at[e], wg_s.at[s], sem.at[s, 0]),
                pltpu.make_async_copy(wu_hbm.at[e], wu_s.at[s], sem.at[s, 1]),
                pltpu.make_async_copy(wd_hbm.at[e], wd_s.at[s], sem.at[s, 2]))

    @pl.when(i == 0)
    def _():
        for cp in weight_copies(exp_ref[0], 0):
            cp.start()

    @pl.when(i < nreal_ref[0])
    def _():
        s = slot_ref[i]

        @pl.when(first_ref[i] == 1)
        def _():
            for cp in weight_copies(exp_ref[i], s):
                cp.wait()

            @pl.when(next_ref[i] >= 0)
            def _():
                for cp in weight_copies(next_ref[i], 1 - s):
                    cp.start()

        half = x_ref.shape[1]
        r = lax.broadcasted_iota(I32, (x_ref.shape[0], 1), 0)
        x_lo, x_hi = _unpack_halves(jnp.where(r < valid_ref[i], x_ref[...], 0))
        hg = _dot(x_lo, wg_s[s, :half, :]) + _dot(x_hi, wg_s[s, half:, :])
        hu = _dot(x_lo, wu_s[s, :half, :]) + _dot(x_hi, wu_s[s, half:, :])
        y_ref[...] = _pack_halves(_dot(_silu(hg) * hu, wd_s[s]))


def _ffn(sched, xs, w_gate, w_up, w_down):
    p, half = xs.shape
    d = 2 * half
    ff = w_gate.shape[2]
    steps = sched[1].shape[0]
    row = lambda i, n, b, *_: (b[i], 0)
    return pl.pallas_call(
        _ffn_kernel,
        out_shape=jax.ShapeDtypeStruct((p, half), I32),
        grid_spec=pltpu.PrefetchScalarGridSpec(
            num_scalar_prefetch=len(sched),
            grid=(steps,),
            in_specs=[
                pl.BlockSpec((FFN_BLOCK, half), row),
                pl.BlockSpec(memory_space=pl.ANY),
                pl.BlockSpec(memory_space=pl.ANY),
                pl.BlockSpec(memory_space=pl.ANY),
            ],
            out_specs=pl.BlockSpec((FFN_BLOCK, half), row),
            scratch_shapes=[
                pltpu.VMEM((2, d, ff), F32),
                pltpu.VMEM((2, d, ff), F32),
                pltpu.VMEM((2, ff, d), F32),
                pltpu.SemaphoreType.DMA((2, 3)),
            ],
        ),
        compiler_params=pltpu.CompilerParams(
            dimension_semantics=("arbitrary",), vmem_limit_bytes=VMEM_LIMIT),
        name="ffn",
    )(*sched, xs, w_gate, w_up, w_down)


def _sc_gather(y, dest3):
    a, words = y.shape
    nchunks, _, chunk = dest3.shape
    n = nchunks * chunk
    per_worker = nchunks // (SC_CORES * SC_SUBCORES)
    mesh = plsc.VectorSubcoreMesh(core_axis_name="c", subcore_axis_name="s",
                                  num_cores=SC_CORES, num_subcores=SC_SUBCORES)

    @functools.partial(
        pl.kernel, mesh=mesh,
        out_type=jax.ShapeDtypeStruct((TOP_K, n, words), I32),
        scratch_types=[
            pltpu.VMEM((TOP_K, chunk), I32),
            pltpu.VMEM((chunk, words), I32),
            pltpu.VMEM((chunk, words), I32),
            pltpu.SemaphoreType.DMA,
            pltpu.SemaphoreType.DMA((2,)),
        ],
        name="sc_gather",
    )
    def run(y_hbm, d_hbm, yt_hbm, idx_v, buf0, buf1, sem_g, sem_w):
        wid = lax.axis_index("s") * SC_CORES + lax.axis_index("c")
        bufs = (buf0, buf1)

        @pl.loop(0, per_worker)
        def _(j):
            ch = wid * per_worker + j
            pltpu.sync_copy(d_hbm.at[ch], idx_v)
            rows = pl.ds(ch * chunk, chunk)
            gather = pltpu.async_copy(y_hbm.at[idx_v.at[0]], bufs[0], sem_g)
            writes = []
            for k in range(TOP_K):
                gather.wait()
                writes.append(pltpu.async_copy(bufs[k % 2], yt_hbm.at[k, rows], sem_w.at[k % 2]))
                if k + 1 < TOP_K:
                    if k >= 1:
                        writes[k - 1].wait()
                    gather = pltpu.async_copy(y_hbm.at[idx_v.at[k + 1]], bufs[(k + 1) % 2], sem_g)
            writes[TOP_K - 2].wait()
            writes[TOP_K - 1].wait()

    return run(y, dest3)


def _combine_kernel(yt_ref, wt_ref, xb_ref, mod_ref, o_ref):
    half = yt_ref.shape[2]
    wt = wt_ref[...]
    lo, hi = _unpack_halves(yt_ref[0])
    r_lo = lo * wt[:, 0:1]
    r_hi = hi * wt[:, 0:1]
    for k in range(1, TOP_K):
        lo, hi = _unpack_halves(yt_ref[k])
        r_lo = r_lo + lo * wt[:, k:k + 1]
        r_hi = r_hi + hi * wt[:, k:k + 1]
    gate = mod_ref[0][5:6]
    o_ref[:, :half] = xb_ref[:, :half] + gate[:, :half] * r_lo
    o_ref[:, half:] = xb_ref[:, half:] + gate[:, half:] * r_hi


def _combine(yt, w_t, xb, mod3, seq):
    n, d = xb.shape
    tm = TM_COMBINE
    tiles_per_seq = seq // tm
    return pl.pallas_call(
        _combine_kernel,
        out_shape=jax.ShapeDtypeStruct((n, d), F32),
        grid=(n // tm,),
        in_specs=[
            pl.BlockSpec((TOP_K, tm, d // 2), lambda i: (0, i, 0)),
            pl.BlockSpec((tm, TOP_K), lambda i: (i, 0)),
            pl.BlockSpec((tm, d), lambda i: (i, 0)),
            pl.BlockSpec((1, N_MOD, d), lambda i: (i // tiles_per_seq, 0, 0)),
        ],
        out_specs=pl.BlockSpec((tm, d), lambda i: (i, 0)),
        compiler_params=pltpu.CompilerParams(vmem_limit_bytes=VMEM_LIMIT),
        name="combine",
    )(yt, w_t, xb, mod3)


def _rotary_tables(seq):
    half = RET_DK // 2
    pos = jnp.arange(seq, dtype=F32)
    inv = ROPE_BASE ** (-jnp.arange(half, dtype=F32) / half)
    ang = pos[:, None] * inv[None, :]
    cos = jnp.cos(ang)
    sin = jnp.sin(ang)
    return jnp.concatenate([cos, cos], axis=-1), jnp.concatenate([-sin, sin], axis=-1)


def _ffn_schedule(counts, nblk):
    padded = (counts + FFN_BLOCK - 1) // FFN_BLOCK * FFN_BLOCK
    ends = jnp.cumsum(padded)
    start = ends - padded
    nreal = ends[-1:] // FFN_BLOCK
    blk = jnp.minimum(jnp.arange(nblk, dtype=I32), nreal - 1)
    row0 = blk * FFN_BLOCK
    exp = jnp.sum((ends[None, :] <= row0[:, None]).astype(I32), axis=1)
    eids = jnp.arange(N_EXPERTS, dtype=I32)
    mine = exp[:, None] == eids[None, :]

    def per_block(table):
        return jnp.sum(jnp.where(mine, table[None, :], 0), axis=1)

    valid = jnp.clip(per_block(counts + start) - row0, 0, FFN_BLOCK)
    steps = jnp.arange(nblk, dtype=I32)
    first = ((steps < nreal) & ((steps == 0) | (exp != jnp.roll(exp, 1)))).astype(I32)
    used = counts > 0
    at_or_after = jnp.flip(lax.cummin(jnp.flip(jnp.where(used, eids, N_EXPERTS))))
    nxt = jnp.concatenate([at_or_after[1:], jnp.full((1,), N_EXPERTS, I32)])
    nxt = jnp.where(nxt >= N_EXPERTS, -1, nxt)
    parity = (jnp.cumsum(used.astype(I32)) - 1) % 2
    return start, (nreal, blk, exp, valid, first, per_block(nxt), per_block(parity))


def _chunked(dest, chunk):
    k, n = dest.shape
    return dest.reshape(k, n // chunk, chunk).transpose(1, 0, 2)


def kernel(x, c, w_ada, b_ada, g_mix, w_in, q_gain, k_gain, w_out, g_ffn, w_router, router_bias,
           w_gate, w_up, w_down, ws_gate, ws_up, ws_down):
    bsz, seq, d = x.shape
    n = bsz * seq
    depth = w_ada.shape[0]
    cos_full, sin_signed = _rotary_tables(seq)
    log_g = jnp.log1p(-jnp.exp2(-5.0 - jnp.arange(RET_HEADS, dtype=F32)))
    ret_w = RET_HEADS * RET_DK
    x2 = x.reshape(n, d)
    for l in range(depth):
        mod3 = _adaln(c, w_ada[l], b_ada[l]).reshape(bsz, N_MOD, d)
        proj = _inproj(x2, mod3, g_mix[l].reshape(1, d), w_in[l].astype(BF16), cos_full, sin_signed, seq)
        proj3 = proj.reshape(bsz, seq, IN_COLS)
        ret = _retention(log_g, proj3)
        qg2 = jnp.tile(q_gain[l].reshape(1, MOBA_DH), (1, 2))
        kg2 = jnp.tile(k_gain[l].reshape(1, MOBA_DH), (1, 2))
        moba = _moba(proj3, qg2, kg2)
        wo = w_out[l].astype(BF16)
        xb, h2, e_idx, w_k, rank, cnt = _mid(
            ret.reshape(n, ret_w), moba.reshape(n, MOBA_HEADS * MOBA_DH), x2, mod3,
            wo[:ret_w], wo[ret_w:], g_ffn[l].reshape(1, d),
            w_router[l].T.astype(BF16), router_bias[l].reshape(N_EXPERTS, 1),
            ws_gate[l].astype(BF16), ws_up[l].astype(BF16), ws_down[l].astype(BF16), seq)
        nblk = n * TOP_K // FFN_BLOCK + N_EXPERTS
        start, sched = _ffn_schedule(cnt[:, 0], nblk)
        dest = _dest(start, e_idx, rank)
        xs = _sc_dispatch(h2, _chunked(dest, SC_CHUNK), nblk * FFN_BLOCK)
        y = _ffn(sched, xs, w_gate[l], w_up[l], w_down[l])
        yt = _sc_gather(y, _chunked(dest, SC_GATHER_CHUNK))
        x2 = _combine(yt, w_k.T, xb, mod3, seq)
    return x2.reshape(bsz, seq, d)
```

```python
import functools

import numpy as np
import jax
import jax.numpy as jnp
from jax import lax
from jax.experimental import pallas as pl
from jax.experimental.pallas import tpu as pltpu
from jax.experimental.pallas import tpu_sc as plsc

F32 = jnp.float32
BF16 = jnp.bfloat16
I32 = jnp.int32

D_MODEL = 1024
RET_HEADS = 4
RET_DK = 128
MOBA_HEADS = 8
MOBA_DH = 64
MOBA_BLOCK = 256
MOBA_TOPK = 3
ROPE_BASE = 10000.0
N_EXPERTS = 256
TOP_K = 8
N_GROUPS = 8
TOPK_GROUPS = 4
GROUP_SIZE = N_EXPERTS // N_GROUPS
EXPERT_FF = 256
ROUTED_SCALE = 2.5
N_MOD = 6
EPS = 1e-6
IN_COLS = 3584

LANES = 128
RET_CHUNK = 256
TM_PROJ = 512
TM_COMBINE = 256
SC_CORES = 2
SC_SUBCORES = 16
SC_CHUNK = 128
SC_GATHER_CHUNK = 64
FFN_BLOCK = 256
FFN_LOOKAHEAD = 3
VMEM_LIMIT = 56 * 1024 * 1024

NEG_INF = float("-inf")


def _silu(x):
    return x * jax.nn.sigmoid(x)


def _nt_dot(a, b):
    return lax.dot_general(a, b, (((1,), (1,)), ((), ())), preferred_element_type=F32)


def _tn_dot(a, b):
    return lax.dot_general(a, b, (((0,), (0,)), ((), ())), preferred_element_type=F32)


def _dot(a, b):
    return jnp.dot(a, b, preferred_element_type=F32)


HI_MASK = -65536


def _pack_halves(v):
    w = v.shape[1] // 2
    lo = lax.bitcast_convert_type(v[:, :w].astype(BF16).astype(F32), I32)
    hi = lax.bitcast_convert_type(v[:, w:].astype(BF16).astype(F32), I32)
    return lax.shift_right_logical(lo, 16) | (hi & HI_MASK)


def _unpack_halves(u):
    lo = lax.bitcast_convert_type(lax.shift_left(u, 16), F32)
    hi = lax.bitcast_convert_type(u & HI_MASK, F32)
    return lo, hi


def _adaln_kernel(c_ref, w_ref, b_ref, o_ref):
    s = _silu(c_ref[...])
    o_ref[...] = _dot(s.astype(BF16), w_ref[...].astype(BF16)) + b_ref[...]


def _adaln(c, w_ada, b_ada):
    bsz, d = c.shape
    ncol = w_ada.shape[1]
    tn = 1024
    return pl.pallas_call(
        _adaln_kernel,
        out_shape=jax.ShapeDtypeStruct((bsz, ncol), F32),
        grid=(ncol // tn,),
        in_specs=[
            pl.BlockSpec((bsz, d), lambda j: (0, 0)),
            pl.BlockSpec((d, tn), lambda j: (0, j)),
            pl.BlockSpec((1, tn), lambda j: (0, j)),
        ],
        out_specs=pl.BlockSpec((bsz, tn), lambda j: (0, j)),
        compiler_params=pltpu.CompilerParams(vmem_limit_bytes=VMEM_LIMIT),
        name="adaln",
    )(c, w_ada, b_ada.reshape(1, ncol))


def _inproj_kernel(x_ref, mod_ref, g_ref, w_ref, cos_ref, sin_ref, o_ref):
    x = x_ref[...]
    ms = jnp.mean(x * x, axis=-1, keepdims=True)
    m = mod_ref[0]
    h = (x * lax.rsqrt(ms + EPS) * g_ref[...]) * (1.0 + m[1:2]) + m[0:1]
    hb = h.astype(BF16)
    cosf = cos_ref[...]
    sinf = sin_ref[...]
    k_scale = RET_DK ** -0.5
    width = RET_HEADS * RET_DK
    for ci in range(IN_COLS // width):
        acc = _dot(hb, w_ref[:, ci * width:(ci + 1) * width])
        if ci < 2:
            for hh in range(RET_HEADS):
                xh = acc[:, hh * RET_DK:(hh + 1) * RET_DK]
                r = xh * cosf + pltpu.roll(xh, RET_DK // 2, axis=1) * sinf
                if ci == 1:
                    r = r * k_scale
                o_ref[:, ci * width + hh * RET_DK:ci * width + (hh + 1) * RET_DK] = r.astype(BF16)
        else:
            o_ref[:, ci * width:(ci + 1) * width] = acc.astype(BF16)


def _inproj(x2, mod3, g_mix, w_in_bf, cos_full, sin_signed, seq):
    n, d = x2.shape
    tm = TM_PROJ
    tiles_per_seq = seq // tm
    return pl.pallas_call(
        _inproj_kernel,
        out_shape=jax.ShapeDtypeStruct((n, IN_COLS), BF16),
        grid=(n // tm,),
        in_specs=[
            pl.BlockSpec((tm, d), lambda i: (i, 0)),
            pl.BlockSpec((1, N_MOD, d), lambda i: (i // tiles_per_seq, 0, 0)),
            pl.BlockSpec((1, d), lambda i: (0, 0)),
            pl.BlockSpec((d, IN_COLS), lambda i: (0, 0)),
            pl.BlockSpec((tm, LANES), lambda i: (i % tiles_per_seq, 0)),
            pl.BlockSpec((tm, LANES), lambda i: (i % tiles_per_seq, 0)),
        ],
        out_specs=pl.BlockSpec((tm, IN_COLS), lambda i: (i, 0)),
        compiler_params=pltpu.CompilerParams(vmem_limit_bytes=VMEM_LIMIT),
        name="inproj",
    )(x2, mod3, g_mix, w_in_bf, cos_full, sin_signed)


def _ret_kernel(lg_ref, q_ref, k_ref, v_ref, g_ref, o_ref):
    seq = q_ref.shape[1]
    c = RET_CHUNK
    lg = lg_ref[pl.program_id(1)]
    row = lax.broadcasted_iota(I32, (c, c), 0)
    col = lax.broadcasted_iota(I32, (c, c), 1)
    diff = (row - col).astype(F32)
    dmask = jnp.where(diff >= 0, jnp.exp(lg * jnp.maximum(diff, 0.0)), 0.0)
    idx = lax.broadcasted_iota(I32, (c, 1), 0).astype(F32)
    q_decay = jnp.exp(lg * (idx + 1.0))
    k_decay = jnp.exp(lg * (c - 1.0 - idx))
    chunk_decay = jnp.exp(jnp.full((1, 1), lg * c, F32))
    state = jnp.zeros((RET_DK, RET_DK), F32)
    for n in range(seq // c):
        rows = slice(n * c, (n + 1) * c)
        qn = q_ref[0, rows, :]
        kn = k_ref[0, rows, :]
        vn = v_ref[0, rows, :]
        scores = _nt_dot(qn, kn) * dmask
        inner = _dot(scores.astype(BF16), vn)
        qs = (qn.astype(F32) * q_decay).astype(BF16)
        cross = _dot(qs, state.astype(BF16))
        o = inner + cross
        o = o * lax.rsqrt(jnp.mean(o * o, axis=-1, keepdims=True) + EPS)
        gn = g_ref[0, rows, :].astype(F32)
        o_ref[0, rows, :] = (_silu(gn) * o).astype(BF16)
        ks = (kn.astype(F32) * k_decay).astype(BF16)
        state = state * chunk_decay + _tn_dot(ks, vn)


def _retention(log_g, proj3):
    bsz, seq, _ = proj3.shape
    blk = (1, seq, RET_DK)
    return pl.pallas_call(
        _ret_kernel,
        out_shape=jax.ShapeDtypeStruct((bsz, seq, RET_HEADS * RET_DK), BF16),
        grid_spec=pltpu.PrefetchScalarGridSpec(
            num_scalar_prefetch=1,
            grid=(bsz, RET_HEADS),
            in_specs=[
                pl.BlockSpec(blk, lambda b, h, lg: (b, 0, h)),
                pl.BlockSpec(blk, lambda b, h, lg: (b, 0, RET_HEADS + h)),
                pl.BlockSpec(blk, lambda b, h, lg: (b, 0, 2 * RET_HEADS + h)),
                pl.BlockSpec(blk, lambda b, h, lg: (b, 0, 3 * RET_HEADS + h)),
            ],
            out_specs=pl.BlockSpec(blk, lambda b, h, lg: (b, 0, h)),
        ),
        compiler_params=pltpu.CompilerParams(vmem_limit_bytes=VMEM_LIMIT),
        name="retention",
    )(log_g, proj3, proj3, proj3, proj3)


def _moba_kernel(q_ref, k_ref, v_ref, qg_ref, kg_ref, o_ref, qt_s, ka_s, kb_s, vta_s, vtb_s):
    seq = q_ref.shape[1]
    lb = MOBA_BLOCK
    nb = seq // lb
    lane = lax.broadcasted_iota(I32, (1, LANES), 1)
    is_a = lane < MOBA_DH
    sub = lax.broadcasted_iota(I32, (LANES, 1), 0)
    top = sub < MOBA_DH

    def head_norm(xf, gain):
        sq = xf * xf
        s_a = jnp.sum(jnp.where(is_a, sq, 0.0), axis=-1, keepdims=True)
        s_b = jnp.sum(jnp.where(is_a, 0.0, sq), axis=-1, keepdims=True)
        inv = jnp.where(is_a, lax.rsqrt(s_a / MOBA_DH + EPS), lax.rsqrt(s_b / MOBA_DH + EPS))
        return xf * inv * gain

    qg = qg_ref[...]
    kg = kg_ref[...]
    k_means = []
    for j in range(nb):
        rows = slice(j * lb, (j + 1) * lb)
        kf = head_norm(k_ref[0, rows, :].astype(F32), kg)
        ka_s[rows, :] = jnp.where(is_a, kf, 0.0).astype(BF16)
        kb_s[rows, :] = jnp.where(is_a, 0.0, kf).astype(BF16)
        k_means.append(jnp.mean(kf, axis=0, keepdims=True))
        qf = head_norm(q_ref[0, rows, :].astype(F32), qg)
        qt_s[:, rows] = (qf * (MOBA_DH ** -0.5)).T.astype(BF16)
        vt = v_ref[0, rows, :].astype(F32).T
        vta_s[:, rows] = jnp.where(top, vt, 1.0).astype(BF16)
        vtb_s[:, rows] = jnp.where(top, 1.0, vt).astype(BF16)
    k_mean = jnp.concatenate(k_means + [jnp.zeros((16 - nb, LANES), F32)], axis=0)
    k_mean_h = (jnp.where(is_a, k_mean, 0.0).astype(BF16), jnp.where(is_a, 0.0, k_mean).astype(BF16))
    k_s = (ka_s, kb_s)
    vt_s = (vta_s, vtb_s)

    r_loc = lax.broadcasted_iota(I32, (lb, lb), 0)
    c_loc = lax.broadcasted_iota(I32, (lb, lb), 1)
    causal = r_loc <= c_loc

    for i in range(nb):
        cols = slice(i * lb, (i + 1) * lb)
        qt = qt_s[:, cols]
        outs = []
        for hx in range(2):
            bias = [None] * i
            if i > MOBA_TOPK:
                gate = _dot(k_mean_h[hx], qt)
                g = [gate[j:j + 1, :] for j in range(i)]
                for j in range(i):
                    rank = jnp.zeros((1, lb), F32)
                    for j2 in range(i):
                        if j2 == j:
                            continue
                        beats = (g[j2] >= g[j]) if j2 < j else (g[j2] > g[j])
                        rank = rank + jnp.where(beats, 1.0, 0.0)
                    bias[j] = jnp.where(rank < float(MOBA_TOPK), 0.0, NEG_INF)
            pieces = []
            for j in range(i + 1):
                s = _dot(k_s[hx][j * lb:(j + 1) * lb, :], qt)
                if j == i:
                    s = jnp.where(causal, s, NEG_INF)
                elif bias[j] is not None:
                    s = s + bias[j]
                pieces.append(s)
            mx = jnp.max(pieces[0], axis=0, keepdims=True)
            for s in pieces[1:]:
                mx = jnp.maximum(mx, jnp.max(s, axis=0, keepdims=True))
            acc = jnp.zeros((LANES, lb), F32)
            for j, s in enumerate(pieces):
                p = jnp.exp(s - mx).astype(BF16)
                acc = acc + _dot(vt_s[hx][:, j * lb:(j + 1) * lb], p)
            den = acc[MOBA_DH:MOBA_DH + 1, :] if hx == 0 else acc[0:1, :]
            outs.append(acc / den)
        o_ref[0, cols, :] = jnp.where(top, outs[0], outs[1]).T.astype(BF16)


def _moba(proj3, qg2, kg2):
    bsz, seq, _ = proj3.shape
    pairs = MOBA_HEADS // 2
    blk = (1, seq, LANES)
    base = 4 * RET_HEADS
    return pl.pallas_call(
        _moba_kernel,
        out_shape=jax.ShapeDtypeStruct((bsz, seq, MOBA_HEADS * MOBA_DH), BF16),
        grid=(bsz, pairs),
        in_specs=[
            pl.BlockSpec(blk, lambda b, p: (b, 0, base + p)),
            pl.BlockSpec(blk, lambda b, p: (b, 0, base + pairs + p)),
            pl.BlockSpec(blk, lambda b, p: (b, 0, base + 2 * pairs + p)),
            pl.BlockSpec((1, LANES), lambda b, p: (0, 0)),
            pl.BlockSpec((1, LANES), lambda b, p: (0, 0)),
        ],
        out_specs=pl.BlockSpec(blk, lambda b, p: (b, 0, p)),
        scratch_shapes=[
            pltpu.VMEM((LANES, seq), BF16),
            pltpu.VMEM((seq, LANES), BF16),
            pltpu.VMEM((seq, LANES), BF16),
            pltpu.VMEM((LANES, seq), BF16),
            pltpu.VMEM((LANES, seq), BF16),
        ],
        compiler_params=pltpu.CompilerParams(vmem_limit_bytes=VMEM_LIMIT),
        name="moba",
    )(proj3, proj3, proj3, qg2, kg2)


def _mid_kernel(ret_ref, moba_ref, x_ref, mod_ref, wo1_ref, wo2_ref, g_ref, wr_ref, rb_ref,
                wsg_ref, wsu_ref, wsd_ref,
                xb_ref, h2_ref, e_ref, w_ref, rk_ref, cnt_ref, carry_s):
    i = pl.program_id(0)
    tm = x_ref.shape[0]

    @pl.when(i == 0)
    def _():
        carry_s[...] = jnp.zeros_like(carry_s)

    m = mod_ref[0]
    mixed = _dot(ret_ref[...], wo1_ref[...]) + _dot(moba_ref[...], wo2_ref[...])
    x1 = x_ref[...] + m[2:3] * mixed
    ms = jnp.mean(x1 * x1, axis=-1, keepdims=True)
    h2 = (x1 * lax.rsqrt(ms + EPS) * g_ref[...]) * (1.0 + m[4:5]) + m[3:4]
    h2_ref[...] = _pack_halves(h2)
    h2b = h2.astype(BF16)

    hid = _silu(_dot(h2b, wsg_ref[...])) * _dot(h2b, wsu_ref[...])
    xb_ref[...] = x1 + m[5:6] * _dot(hid.astype(BF16), wsd_ref[...])

    scores = jax.nn.sigmoid(_nt_dot(wr_ref[...], h2b))
    biased = scores + rb_ref[...]
    grp = biased.reshape(N_GROUPS, GROUP_SIZE, tm)
    gi = lax.broadcasted_iota(I32, (N_GROUPS, GROUP_SIZE, tm), 1).astype(F32)
    top1 = jnp.max(grp, axis=1, keepdims=True)
    first = jnp.min(jnp.where(grp == top1, gi, float(GROUP_SIZE)), axis=1, keepdims=True)
    top2 = jnp.max(jnp.where(gi == first, NEG_INF, grp), axis=1, keepdims=True)
    gscore = (top1 + top2).reshape(N_GROUPS, tm)
    gidx = lax.broadcasted_iota(I32, (N_GROUPS, tm), 0)
    grank = jnp.zeros((N_GROUPS, tm), F32)
    for g2 in range(N_GROUPS):
        rowv = gscore[g2:g2 + 1, :]
        beats = (rowv > gscore) | ((rowv == gscore) & (g2 < gidx))
        grank = grank + jnp.where(beats, 1.0, 0.0)
    gsel = jnp.where(grank < float(TOPK_GROUPS), 1.0, 0.0)
    emask = jnp.broadcast_to(gsel.reshape(N_GROUPS, 1, tm), (N_GROUPS, GROUP_SIZE, tm)).reshape(N_EXPERTS, tm)
    choice = jnp.where(emask > 0.5, biased, NEG_INF)

    eidx = lax.broadcasted_iota(I32, (N_EXPERTS, tm), 0).astype(F32)
    selmask = jnp.zeros((N_EXPERTS, tm), F32)
    e_rows = []
    w_rows = []
    for _k in range(TOP_K):
        mx = jnp.max(choice, axis=0, keepdims=True)
        idx = jnp.min(jnp.where(choice == mx, eidx, float(N_EXPERTS)), axis=0, keepdims=True)
        onehot = eidx == idx
        e_rows.append(idx)
        w_rows.append(jnp.sum(jnp.where(onehot, scores, 0.0), axis=0, keepdims=True))
        selmask = selmask + jnp.where(onehot, 1.0, 0.0)
        choice = jnp.where(onehot, NEG_INF, choice)
    wsum = w_rows[0]
    for wk in w_rows[1:]:
        wsum = wsum + wk

    tr = lax.broadcasted_iota(I32, (tm, tm), 0)
    tc = lax.broadcasted_iota(I32, (tm, tm), 1)
    upper = jnp.where(tr < tc, 1.0, 0.0).astype(BF16)
    prefix = _dot(selmask.astype(BF16), upper) + carry_s[...]
    r_rows = [jnp.sum(jnp.where(eidx == ek, prefix, 0.0), axis=0, keepdims=True) for ek in e_rows]
    carry_s[...] = carry_s[...] + jnp.sum(selmask, axis=1, keepdims=True)

    e_ref[...] = jnp.concatenate(e_rows, axis=0).astype(I32)
    w_ref[...] = jnp.concatenate([wk / wsum * ROUTED_SCALE for wk in w_rows], axis=0)
    rk_ref[...] = jnp.concatenate(r_rows, axis=0).astype(I32)
    cnt_ref[...] = carry_s[...].astype(I32)


def _mid(ret2, moba2, x2, mod3, wo1, wo2, g_ffn, wr_t, rbias, wsg, wsu, wsd, seq):
    n, d = x2.shape
    tm = TM_PROJ
    tiles_per_seq = seq // tm
    half = ret2.shape[1]
    ff = wsg.shape[1]
    const = lambda i: (0, 0)
    row = lambda i: (i, 0)
    colt = lambda i: (0, i)
    return pl.pallas_call(
        _mid_kernel,
        out_shape=(
            jax.ShapeDtypeStruct((n, d), F32),
            jax.ShapeDtypeStruct((n, d // 2), I32),
            jax.ShapeDtypeStruct((TOP_K, n), I32),
            jax.ShapeDtypeStruct((TOP_K, n), F32),
            jax.ShapeDtypeStruct((TOP_K, n), I32),
            jax.ShapeDtypeStruct((N_EXPERTS, 1), I32),
        ),
        grid=(n // tm,),
        in_specs=[
            pl.BlockSpec((tm, half), row),
            pl.BlockSpec((tm, half), row),
            pl.BlockSpec((tm, d), row),
            pl.BlockSpec((1, N_MOD, d), lambda i: (i // tiles_per_seq, 0, 0)),
            pl.BlockSpec((half, d), const),
            pl.BlockSpec((half, d), const),
            pl.BlockSpec((1, d), const),
            pl.BlockSpec((N_EXPERTS, d), const),
            pl.BlockSpec((N_EXPERTS, 1), const),
            pl.BlockSpec((d, ff), const),
            pl.BlockSpec((d, ff), const),
            pl.BlockSpec((ff, d), const),
        ],
        out_specs=(
            pl.BlockSpec((tm, d), row),
            pl.BlockSpec((tm, d // 2), row),
            pl.BlockSpec((TOP_K, tm), colt),
            pl.BlockSpec((TOP_K, tm), colt),
            pl.BlockSpec((TOP_K, tm), colt),
            pl.BlockSpec((N_EXPERTS, 1), const),
        ),
        scratch_shapes=[pltpu.VMEM((N_EXPERTS, 1), F32)],
        compiler_params=pltpu.CompilerParams(
            dimension_semantics=("arbitrary",), vmem_limit_bytes=VMEM_LIMIT),
        name="mid",
    )(ret2, moba2, x2, mod3, wo1, wo2, g_ffn, wr_t, rbias, wsg, wsu, wsd)


def _dest_kernel(start_ref, e_ref, rk_ref, o_ref):
    e = e_ref[...]

    def body(ex, acc):
        return acc + jnp.where(e == ex, start_ref[ex], 0)

    o_ref[...] = lax.fori_loop(0, N_EXPERTS, body, rk_ref[...], unroll=8)


def _dest(start, e_idx, rank):
    k, n = e_idx.shape
    tn = 2048
    return pl.pallas_call(
        _dest_kernel,
        out_shape=jax.ShapeDtypeStruct((k, n), I32),
        grid_spec=pltpu.PrefetchScalarGridSpec(
            num_scalar_prefetch=1,
            grid=(n // tn,),
            in_specs=[
                pl.BlockSpec((k, tn), lambda i, s: (0, i)),
                pl.BlockSpec((k, tn), lambda i, s: (0, i)),
            ],
            out_specs=pl.BlockSpec((k, tn), lambda i, s: (0, i)),
        ),
        compiler_params=pltpu.CompilerParams(vmem_limit_bytes=VMEM_LIMIT),
        name="dest",
    )(start, e_idx, rank)


def _sc_dispatch(h2p, dest3, total_rows):
    n, words = h2p.shape
    nchunks = n // SC_CHUNK
    per_worker = nchunks // (SC_CORES * SC_SUBCORES)
    mesh = plsc.VectorSubcoreMesh(core_axis_name="c", subcore_axis_name="s",
                                  num_cores=SC_CORES, num_subcores=SC_SUBCORES)

    @functools.partial(
        pl.kernel, mesh=mesh,
        out_type=jax.ShapeDtypeStruct((total_rows, words), I32),
        scratch_types=[
            pltpu.VMEM((TOP_K, SC_CHUNK), I32),
            pltpu.VMEM((SC_CHUNK, words), I32),
            pltpu.SemaphoreType.DMA,
        ],
        name="sc_dispatch",
    )
    def run(h_hbm, d_hbm, xs_hbm, idx_v, rows_v, sem):
        wid = lax.axis_index("s") * SC_CORES + lax.axis_index("c")

        @pl.loop(0, per_worker)
        def _(j):
            ch = wid * per_worker + j
            pltpu.sync_copy(d_hbm.at[ch], idx_v)
            pltpu.sync_copy(h_hbm.at[pl.ds(ch * SC_CHUNK, SC_CHUNK)], rows_v)
            copies = [pltpu.async_copy(rows_v, xs_hbm.at[idx_v.at[k]], sem) for k in range(TOP_K)]
            for cp in copies:
                cp.wait()

    return run(h2p, dest3)


def _ffn_kernel(nreal_ref, blk_ref, exp_ref, valid_ref, first_ref, next_ref, slot_ref,
                x_hbm, wg_hbm, wu_hbm, wd_hbm, y_ref, x_s, wg_s, wu_s, wd_s, sem_x, sem):
    i = pl.program_id(0)
    nreal = nreal_ref[0]
    rows_per = x_s.shape[1]

    def row_copy(g):
        slot = lax.rem(g, FFN_LOOKAHEAD + 1)
        return pltpu.make_async_copy(x_hbm.at[pl.ds(g * rows_per, rows_per)], x_s.at[slot], sem_x.at[slot])

    def weight_copies(e, s):
        return (pltpu.make_async_copy(wg_hbm.at[e], wg_s.at[s], sem.at[s, 0]),
                pltpu.make_async_copy(wu_hbm.at[e], wu_s.at[s], sem.at[s, 1]),
                pltpu.make_async_copy(wd_hbm.at[e], wd_s.at[s], sem.at[s, 2]))

    @pl.when(i == 0)
    def _():
        for cp in weight_copies(exp_ref[0], 0):
            cp.start()
        for g in range(FFN_LOOKAHEAD):
            @pl.when(g < nreal)
            def _():
                row_copy(g).start()

    @pl.when(i < nreal)
    def _():
        s = slot_ref[i]

        @pl.when(i + FFN_LOOKAHEAD < nreal)
        def _():
            row_copy(i + FFN_LOOKAHEAD).start()

        row_copy(i).wait()

        @pl.when(first_ref[i] == 1)
        def _():
            for cp in weight_copies(exp_ref[i], s):
                cp.wait()

            @pl.when(next_ref[i] >= 0)
            def _():
                for cp in weight_copies(next_ref[i], 1 - s):
                    cp.start()

        half = x_s.shape[2]
        r = lax.broadcasted_iota(I32, (rows_per, 1), 0)
        x_blk = x_s[lax.rem(i, FFN_LOOKAHEAD + 1)]
        x_lo, x_hi = _unpack_halves(jnp.where(r < valid_ref[i], x_blk, 0))
        hg = _dot(x_lo, wg_s[s, :half, :]) + _dot(x_hi, wg_s[s, half:, :])
        hu = _dot(x_lo, wu_s[s, :half, :]) + _dot(x_hi, wu_s[s, half:, :])
        y_ref[...] = _pack_halves(_dot(_silu(hg) * hu, wd_s[s]))


def _ffn(sched, xs, w_gate, w_up, w_down):
    p, half = xs.shape
    d = 2 * half
    ff = w_gate.shape[2]
    steps = sched[1].shape[0]
    row = lambda i, n, b, *_: (b[i], 0)
    return pl.pallas_call(
        _ffn_kernel,
        out_shape=jax.ShapeDtypeStruct((p, half), I32),
        grid_spec=pltpu.PrefetchScalarGridSpec(
            num_scalar_prefetch=len(sched),
            grid=(steps,),
            in_specs=[
                pl.BlockSpec(memory_space=pl.ANY),
                pl.BlockSpec(memory_space=pl.ANY),
                pl.BlockSpec(memory_space=pl.ANY),
                pl.BlockSpec(memory_space=pl.ANY),
            ],
            out_specs=pl.BlockSpec((FFN_BLOCK, half), row),
            scratch_shapes=[
                pltpu.VMEM((FFN_LOOKAHEAD + 1, FFN_BLOCK, half), I32),
                pltpu.VMEM((2, d, ff), F32),
                pltpu.VMEM((2, d, ff), F32),
                pltpu.VMEM((2, ff, d), F32),
                pltpu.SemaphoreType.DMA((FFN_LOOKAHEAD + 1,)),
                pltpu.SemaphoreType.DMA((2, 3)),
            ],
        ),
        compiler_params=pltpu.CompilerParams(
            dimension_semantics=("arbitrary",), vmem_limit_bytes=VMEM_LIMIT),
        name="ffn",
    )(*sched, xs, w_gate, w_up, w_down)


def _sc_gather(y, dest3):
    a, words = y.shape
    nchunks, _, chunk = dest3.shape
    n = nchunks * chunk
    per_worker = nchunks // (SC_CORES * SC_SUBCORES)
    mesh = plsc.VectorSubcoreMesh(core_axis_name="c", subcore_axis_name="s",
                                  num_cores=SC_CORES, num_subcores=SC_SUBCORES)

    @functools.partial(
        pl.kernel, mesh=mesh,
        out_type=jax.ShapeDtypeStruct((TOP_K, n, words), I32),
        scratch_types=[
            pltpu.VMEM((TOP_K, chunk), I32),
            pltpu.VMEM((chunk, words), I32),
            pltpu.VMEM((chunk, words), I32),
            pltpu.SemaphoreType.DMA,
            pltpu.SemaphoreType.DMA((2,)),
        ],
        name="sc_gather",
    )
    def run(y_hbm, d_hbm, yt_hbm, idx_v, buf0, buf1, sem_g, sem_w):
        wid = lax.axis_index("s") * SC_CORES + lax.axis_index("c")
        bufs = (buf0, buf1)

        @pl.loop(0, per_worker)
        def _(j):
            ch = wid * per_worker + j
            pltpu.sync_copy(d_hbm.at[ch], idx_v)
            rows = pl.ds(ch * chunk, chunk)
            gather = pltpu.async_copy(y_hbm.at[idx_v.at[0]], bufs[0], sem_g)
            writes = []
            for k in range(TOP_K):
                gather.wait()
                writes.append(pltpu.async_copy(bufs[k % 2], yt_hbm.at[k, rows], sem_w.at[k % 2]))
                if k + 1 < TOP_K:
                    if k >= 1:
                        writes[k - 1].wait()
                    gather = pltpu.async_copy(y_hbm.at[idx_v.at[k + 1]], bufs[(k + 1) % 2], sem_g)
            writes[TOP_K - 2].wait()
            writes[TOP_K - 1].wait()

    return run(y, dest3)


def _combine_kernel(yt_ref, wt_ref, xb_ref, mod_ref, o_ref):
    half = yt_ref.shape[2]
    wt = wt_ref[...]
    lo, hi = _unpack_halves(yt_ref[0])
    r_lo = lo * wt[:, 0:1]
    r_hi = hi * wt[:, 0:1]
    for k in range(1, TOP_K):
        lo, hi = _unpack_halves(yt_ref[k])
        r_lo = r_lo + lo * wt[:, k:k + 1]
        r_hi = r_hi + hi * wt[:, k:k + 1]
    gate = mod_ref[0][5:6]
    o_ref[:, :half] = xb_ref[:, :half] + gate[:, :half] * r_lo
    o_ref[:, half:] = xb_ref[:, half:] + gate[:, half:] * r_hi


def _combine(yt, w_t, xb, mod3, seq):
    n, d = xb.shape
    tm = TM_COMBINE
    tiles_per_seq = seq // tm
    return pl.pallas_call(
        _combine_kernel,
        out_shape=jax.ShapeDtypeStruct((n, d), F32),
        grid=(n // tm,),
        in_specs=[
            pl.BlockSpec((TOP_K, tm, d // 2), lambda i: (0, i, 0)),
            pl.BlockSpec((tm, TOP_K), lambda i: (i, 0)),
            pl.BlockSpec((tm, d), lambda i: (i, 0)),
            pl.BlockSpec((1, N_MOD, d), lambda i: (i // tiles_per_seq, 0, 0)),
        ],
        out_specs=pl.BlockSpec((tm, d), lambda i: (i, 0)),
        compiler_params=pltpu.CompilerParams(vmem_limit_bytes=VMEM_LIMIT),
        name="combine",
    )(yt, w_t, xb, mod3)


def _rotary_tables(seq):
    half = RET_DK // 2
    pos = jnp.arange(seq, dtype=F32)
    inv = ROPE_BASE ** (-jnp.arange(half, dtype=F32) / half)
    ang = pos[:, None] * inv[None, :]
    cos = jnp.cos(ang)
    sin = jnp.sin(ang)
    return jnp.concatenate([cos, cos], axis=-1), jnp.concatenate([-sin, sin], axis=-1)


def _ffn_schedule(counts, nblk):
    padded = (counts + FFN_BLOCK - 1) // FFN_BLOCK * FFN_BLOCK
    ends = jnp.cumsum(padded)
    start = ends - padded
    nreal = ends[-1:] // FFN_BLOCK
    blk = jnp.minimum(jnp.arange(nblk, dtype=I32), nreal - 1)
    row0 = blk * FFN_BLOCK
    exp = jnp.sum((ends[None, :] <= row0[:, None]).astype(I32), axis=1)
    eids = jnp.arange(N_EXPERTS, dtype=I32)
    mine = exp[:, None] == eids[None, :]

    def per_block(table):
        return jnp.sum(jnp.where(mine, table[None, :], 0), axis=1)

    valid = jnp.clip(per_block(counts + start) - row0, 0, FFN_BLOCK)
    steps = jnp.arange(nblk, dtype=I32)
    first = ((steps < nreal) & ((steps == 0) | (exp != jnp.roll(exp, 1)))).astype(I32)
    used = counts > 0
    at_or_after = jnp.flip(lax.cummin(jnp.flip(jnp.where(used, eids, N_EXPERTS))))
    nxt = jnp.concatenate([at_or_after[1:], jnp.full((1,), N_EXPERTS, I32)])
    nxt = jnp.where(nxt >= N_EXPERTS, -1, nxt)
    parity = (jnp.cumsum(used.astype(I32)) - 1) % 2
    return start, (nreal, blk, exp, valid, first, per_block(nxt), per_block(parity))


def _chunked(dest, chunk):
    k, n = dest.shape
    return dest.reshape(k, n // chunk, chunk).transpose(1, 0, 2)


def kernel(x, c, w_ada, b_ada, g_mix, w_in, q_gain, k_gain, w_out, g_ffn, w_router, router_bias,
           w_gate, w_up, w_down, ws_gate, ws_up, ws_down):
    bsz, seq, d = x.shape
    n = bsz * seq
    depth = w_ada.shape[0]
    cos_full, sin_signed = _rotary_tables(seq)
    log_g = jnp.log1p(-jnp.exp2(-5.0 - jnp.arange(RET_HEADS, dtype=F32)))
    ret_w = RET_HEADS * RET_DK
    x2 = x.reshape(n, d)
    for l in range(depth):
        mod3 = _adaln(c, w_ada[l], b_ada[l]).reshape(bsz, N_MOD, d)
        proj = _inproj(x2, mod3, g_mix[l].reshape(1, d), w_in[l].astype(BF16), cos_full, sin_signed, seq)
        proj3 = proj.reshape(bsz, seq, IN_COLS)
        ret = _retention(log_g, proj3)
        qg2 = jnp.tile(q_gain[l].reshape(1, MOBA_DH), (1, 2))
        kg2 = jnp.tile(k_gain[l].reshape(1, MOBA_DH), (1, 2))
        moba = _moba(proj3, qg2, kg2)
        wo = w_out[l].astype(BF16)
        xb, h2, e_idx, w_k, rank, cnt = _mid(
            ret.reshape(n, ret_w), moba.reshape(n, MOBA_HEADS * MOBA_DH), x2, mod3,
            wo[:ret_w], wo[ret_w:], g_ffn[l].reshape(1, d),
            w_router[l].T.astype(BF16), router_bias[l].reshape(N_EXPERTS, 1),
            ws_gate[l].astype(BF16), ws_up[l].astype(BF16), ws_down[l].astype(BF16), seq)
        nblk = n * TOP_K // FFN_BLOCK + N_EXPERTS
        start, sched = _ffn_schedule(cnt[:, 0], nblk)
        dest = _dest(start, e_idx, rank)
        xs = _sc_dispatch(h2, _chunked(dest, SC_CHUNK), nblk * FFN_BLOCK)
        y = _ffn(sched, xs, w_gate[l], w_up[l], w_down[l])
        yt = _sc_gather(y, _chunked(dest, SC_GATHER_CHUNK))
        x2 = _combine(yt, w_k.T, xb, mod3, seq)
    return x2.reshape(bsz, seq, d)
```

```python
import functools

import numpy as np
import jax
import jax.numpy as jnp
from jax import lax
from jax.experimental import pallas as pl
from jax.experimental.pallas import tpu as pltpu
from jax.experimental.pallas import tpu_sc as plsc

F32 = jnp.float32
BF16 = jnp.bfloat16
I32 = jnp.int32

D_MODEL = 1024
RET_HEADS = 4
RET_DK = 128
MOBA_HEADS = 8
MOBA_DH = 64
MOBA_BLOCK = 256
MOBA_TOPK = 3
ROPE_BASE = 10000.0
N_EXPERTS = 256
TOP_K = 8
N_GROUPS = 8
TOPK_GROUPS = 4
GROUP_SIZE = N_EXPERTS // N_GROUPS
EXPERT_FF = 256
ROUTED_SCALE = 2.5
N_MOD = 6
EPS = 1e-6
IN_COLS = 3584

LANES = 128
RET_CHUNK = 256
TM_PROJ = 512
TM_COMBINE = 256
SC_CORES = 2
SC_SUBCORES = 16
SC_CHUNK = 128
SC_GATHER_CHUNK = 64
FFN_BLOCK = 256
FFN_LOOKAHEAD = 5
FFN_WEIGHT_SLOTS = 3
VMEM_LIMIT = 56 * 1024 * 1024

NEG_INF = float("-inf")
LOG2_E = 1.4426950408889634


def _silu(x):
    return x * jax.nn.sigmoid(x)


def _nt_dot(a, b):
    return lax.dot_general(a, b, (((1,), (1,)), ((), ())), preferred_element_type=F32)


def _tn_dot(a, b):
    return lax.dot_general(a, b, (((0,), (0,)), ((), ())), preferred_element_type=F32)


def _dot(a, b):
    return jnp.dot(a, b, preferred_element_type=F32)


HI_MASK = -65536


def _pack_halves(v):
    w = v.shape[1] // 2
    lo = lax.bitcast_convert_type(v[:, :w].astype(BF16).astype(F32), I32)
    hi = lax.bitcast_convert_type(v[:, w:].astype(BF16).astype(F32), I32)
    return lax.shift_right_logical(lo, 16) | (hi & HI_MASK)


def _unpack_halves(u):
    lo = lax.bitcast_convert_type(lax.shift_left(u, 16), F32)
    hi = lax.bitcast_convert_type(u & HI_MASK, F32)
    return lo, hi


def _adaln_kernel(c_ref, w_ref, b_ref, o_ref):
    s = _silu(c_ref[...])
    o_ref[...] = _dot(s.astype(BF16), w_ref[...].astype(BF16)) + b_ref[...]


def _adaln(c, w_ada, b_ada):
    bsz, d = c.shape
    ncol = w_ada.shape[1]
    tn = 1024
    return pl.pallas_call(
        _adaln_kernel,
        out_shape=jax.ShapeDtypeStruct((bsz, ncol), F32),
        grid=(ncol // tn,),
        in_specs=[
            pl.BlockSpec((bsz, d), lambda j: (0, 0)),
            pl.BlockSpec((d, tn), lambda j: (0, j)),
            pl.BlockSpec((1, tn), lambda j: (0, j)),
        ],
        out_specs=pl.BlockSpec((bsz, tn), lambda j: (0, j)),
        compiler_params=pltpu.CompilerParams(vmem_limit_bytes=VMEM_LIMIT),
        name="adaln",
    )(c, w_ada, b_ada.reshape(1, ncol))


def _inproj_kernel(x_ref, mod_ref, g_ref, w_ref, cos_ref, sin_ref, o_ref):
    x = x_ref[...]
    ms = jnp.mean(x * x, axis=-1, keepdims=True)
    m = mod_ref[0]
    h = (x * lax.rsqrt(ms + EPS) * g_ref[...]) * (1.0 + m[1:2]) + m[0:1]
    hb = h.astype(BF16)
    cosf = cos_ref[...]
    sinf = sin_ref[...]
    k_scale = RET_DK ** -0.5
    width = RET_HEADS * RET_DK
    for ci in range(IN_COLS // width):
        acc = _dot(hb, w_ref[:, ci * width:(ci + 1) * width])
        if ci < 2:
            for hh in range(RET_HEADS):
                xh = acc[:, hh * RET_DK:(hh + 1) * RET_DK]
                r = xh * cosf + pltpu.roll(xh, RET_DK // 2, axis=1) * sinf
                if ci == 1:
                    r = r * k_scale
                o_ref[:, ci * width + hh * RET_DK:ci * width + (hh + 1) * RET_DK] = r.astype(BF16)
        else:
            o_ref[:, ci * width:(ci + 1) * width] = acc.astype(BF16)


def _inproj(x2, mod3, g_mix, w_in_bf, cos_full, sin_signed, seq):
    n, d = x2.shape
    tm = TM_PROJ
    tiles_per_seq = seq // tm
    return pl.pallas_call(
        _inproj_kernel,
        out_shape=jax.ShapeDtypeStruct((n, IN_COLS), BF16),
        grid=(n // tm,),
        in_specs=[
            pl.BlockSpec((tm, d), lambda i: (i, 0)),
            pl.BlockSpec((1, N_MOD, d), lambda i: (i // tiles_per_seq, 0, 0)),
            pl.BlockSpec((1, d), lambda i: (0, 0)),
            pl.BlockSpec((d, IN_COLS), lambda i: (0, 0)),
            pl.BlockSpec((tm, LANES), lambda i: (i % tiles_per_seq, 0)),
            pl.BlockSpec((tm, LANES), lambda i: (i % tiles_per_seq, 0)),
        ],
        out_specs=pl.BlockSpec((tm, IN_COLS), lambda i: (i, 0)),
        compiler_params=pltpu.CompilerParams(vmem_limit_bytes=VMEM_LIMIT),
        name="inproj",
    )(x2, mod3, g_mix, w_in_bf, cos_full, sin_signed)


def _ret_kernel(lg_ref, q_ref, k_ref, v_ref, g_ref, o_ref):
    seq = q_ref.shape[1]
    c = RET_CHUNK
    lg = lg_ref[pl.program_id(1)]
    row = lax.broadcasted_iota(I32, (c, c), 0)
    col = lax.broadcasted_iota(I32, (c, c), 1)
    diff = (row - col).astype(F32)
    dmask = jnp.where(diff >= 0, jnp.exp(lg * jnp.maximum(diff, 0.0)), 0.0)
    idx = lax.broadcasted_iota(I32, (c, 1), 0).astype(F32)
    q_decay = jnp.exp(lg * (idx + 1.0))
    k_decay = jnp.exp(lg * (c - 1.0 - idx))
    chunk_decay = jnp.exp(jnp.full((1, 1), lg * c, F32))
    state = jnp.zeros((RET_DK, RET_DK), F32)
    for n in range(seq // c):
        rows = slice(n * c, (n + 1) * c)
        qn = q_ref[0, rows, :]
        kn = k_ref[0, rows, :]
        vn = v_ref[0, rows, :]
        scores = _nt_dot(qn, kn) * dmask
        inner = _dot(scores.astype(BF16), vn)
        qs = (qn.astype(F32) * q_decay).astype(BF16)
        cross = _dot(qs, state.astype(BF16))
        o = inner + cross
        o = o * lax.rsqrt(jnp.mean(o * o, axis=-1, keepdims=True) + EPS)
        gn = g_ref[0, rows, :].astype(F32)
        o_ref[0, rows, :] = (_silu(gn) * o).astype(BF16)
        ks = (kn.astype(F32) * k_decay).astype(BF16)
        state = state * chunk_decay + _tn_dot(ks, vn)


def _retention(log_g, proj3):
    bsz, seq, _ = proj3.shape
    blk = (1, seq, RET_DK)
    return pl.pallas_call(
        _ret_kernel,
        out_shape=jax.ShapeDtypeStruct((bsz, seq, RET_HEADS * RET_DK), BF16),
        grid_spec=pltpu.PrefetchScalarGridSpec(
            num_scalar_prefetch=1,
            grid=(bsz, RET_HEADS),
            in_specs=[
                pl.BlockSpec(blk, lambda b, h, lg: (b, 0, h)),
                pl.BlockSpec(blk, lambda b, h, lg: (b, 0, RET_HEADS + h)),
                pl.BlockSpec(blk, lambda b, h, lg: (b, 0, 2 * RET_HEADS + h)),
                pl.BlockSpec(blk, lambda b, h, lg: (b, 0, 3 * RET_HEADS + h)),
            ],
            out_specs=pl.BlockSpec(blk, lambda b, h, lg: (b, 0, h)),
        ),
        compiler_params=pltpu.CompilerParams(vmem_limit_bytes=VMEM_LIMIT),
        name="retention",
    )(log_g, proj3, proj3, proj3, proj3)


def _moba_kernel(q_ref, k_ref, v_ref, qg_ref, kg_ref, o_ref, qt_s, ka_s, kb_s, vta_s, vtb_s):
    seq = q_ref.shape[1]
    lb = MOBA_BLOCK
    nb = seq // lb
    lane = lax.broadcasted_iota(I32, (1, LANES), 1)
    is_a = lane < MOBA_DH
    sub = lax.broadcasted_iota(I32, (LANES, 1), 0)
    top = sub < MOBA_DH

    def head_norm(xf, gain):
        sq = xf * xf
        s_a = jnp.sum(jnp.where(is_a, sq, 0.0), axis=-1, keepdims=True)
        s_b = jnp.sum(jnp.where(is_a, 0.0, sq), axis=-1, keepdims=True)
        inv = jnp.where(is_a, lax.rsqrt(s_a / MOBA_DH + EPS), lax.rsqrt(s_b / MOBA_DH + EPS))
        return xf * inv * gain

    qg = qg_ref[...]
    kg = kg_ref[...]
    k_means = []
    for j in range(nb):
        rows = slice(j * lb, (j + 1) * lb)
        kf = head_norm(k_ref[0, rows, :].astype(F32), kg)
        ka_s[rows, :] = jnp.where(is_a, kf, 0.0).astype(BF16)
        kb_s[rows, :] = jnp.where(is_a, 0.0, kf).astype(BF16)
        k_means.append(jnp.mean(kf, axis=0, keepdims=True))
        qf = head_norm(q_ref[0, rows, :].astype(F32), qg)
        qt_s[:, rows] = (qf * (MOBA_DH ** -0.5 * LOG2_E)).T.astype(BF16)
        vt = v_ref[0, rows, :].astype(F32).T
        vta_s[:, rows] = jnp.where(top, vt, 1.0).astype(BF16)
        vtb_s[:, rows] = jnp.where(top, 1.0, vt).astype(BF16)
    k_mean = jnp.concatenate(k_means + [jnp.zeros((16 - nb, LANES), F32)], axis=0)
    k_mean_h = (jnp.where(is_a, k_mean, 0.0).astype(BF16), jnp.where(is_a, 0.0, k_mean).astype(BF16))
    k_s = (ka_s, kb_s)
    vt_s = (vta_s, vtb_s)

    r_loc = lax.broadcasted_iota(I32, (lb, lb), 0)
    c_loc = lax.broadcasted_iota(I32, (lb, lb), 1)
    causal = r_loc <= c_loc

    for i in range(nb):
        cols = slice(i * lb, (i + 1) * lb)
        qt = qt_s[:, cols]
        outs = []
        for hx in range(2):
            bias = [None] * i
            if i > MOBA_TOPK:
                gate = _dot(k_mean_h[hx], qt)
                g = [gate[j:j + 1, :] for j in range(i)]
                for j in range(i):
                    rank = jnp.zeros((1, lb), F32)
                    for j2 in range(i):
                        if j2 == j:
                            continue
                        beats = (g[j2] >= g[j]) if j2 < j else (g[j2] > g[j])
                        rank = rank + jnp.where(beats, 1.0, 0.0)
                    bias[j] = jnp.where(rank < float(MOBA_TOPK), 0.0, NEG_INF)
            pieces = []
            for j in range(i + 1):
                s = _dot(k_s[hx][j * lb:(j + 1) * lb, :], qt)
                if j == i:
                    s = jnp.where(causal, s, NEG_INF)
                elif bias[j] is not None:
                    s = s + bias[j]
                pieces.append(s)
            mx = jnp.max(pieces[0], axis=0, keepdims=True)
            for s in pieces[1:]:
                mx = jnp.maximum(mx, jnp.max(s, axis=0, keepdims=True))
            acc = jnp.zeros((LANES, lb), F32)
            for j, s in enumerate(pieces):
                p = jnp.exp2(s - mx).astype(BF16)
                acc = acc + _dot(vt_s[hx][:, j * lb:(j + 1) * lb], p)
            den = acc[MOBA_DH:MOBA_DH + 1, :] if hx == 0 else acc[0:1, :]
            outs.append(acc / den)
        o_ref[0, cols, :] = jnp.where(top, outs[0], outs[1]).T.astype(BF16)


def _moba(proj3, qg2, kg2):
    bsz, seq, _ = proj3.shape
    pairs = MOBA_HEADS // 2
    blk = (1, seq, LANES)
    base = 4 * RET_HEADS
    return pl.pallas_call(
        _moba_kernel,
        out_shape=jax.ShapeDtypeStruct((bsz, seq, MOBA_HEADS * MOBA_DH), BF16),
        grid=(bsz, pairs),
        in_specs=[
            pl.BlockSpec(blk, lambda b, p: (b, 0, base + p)),
            pl.BlockSpec(blk, lambda b, p: (b, 0, base + pairs + p)),
            pl.BlockSpec(blk, lambda b, p: (b, 0, base + 2 * pairs + p)),
            pl.BlockSpec((1, LANES), lambda b, p: (0, 0)),
            pl.BlockSpec((1, LANES), lambda b, p: (0, 0)),
        ],
        out_specs=pl.BlockSpec(blk, lambda b, p: (b, 0, p)),
        scratch_shapes=[
            pltpu.VMEM((LANES, seq), BF16),
            pltpu.VMEM((seq, LANES), BF16),
            pltpu.VMEM((seq, LANES), BF16),
            pltpu.VMEM((LANES, seq), BF16),
            pltpu.VMEM((LANES, seq), BF16),
        ],
        compiler_params=pltpu.CompilerParams(vmem_limit_bytes=VMEM_LIMIT),
        name="moba",
    )(proj3, proj3, proj3, qg2, kg2)


def _mid_kernel(ret_ref, moba_ref, x_ref, mod_ref, wo1_ref, wo2_ref, g_ref, wr_ref, rb_ref,
                wsg_ref, wsu_ref, wsd_ref,
                xb_ref, h2_ref, e_ref, w_ref, rk_ref, cnt_ref, carry_s):
    i = pl.program_id(0)
    tm = x_ref.shape[0]

    @pl.when(i == 0)
    def _():
        carry_s[...] = jnp.zeros_like(carry_s)

    m = mod_ref[0]
    mixed = _dot(ret_ref[...], wo1_ref[...]) + _dot(moba_ref[...], wo2_ref[...])
    x1 = x_ref[...] + m[2:3] * mixed
    ms = jnp.mean(x1 * x1, axis=-1, keepdims=True)
    h2 = (x1 * lax.rsqrt(ms + EPS) * g_ref[...]) * (1.0 + m[4:5]) + m[3:4]
    h2_ref[...] = _pack_halves(h2)
    h2b = h2.astype(BF16)

    hid = _silu(_dot(h2b, wsg_ref[...])) * _dot(h2b, wsu_ref[...])
    xb_ref[...] = x1 + m[5:6] * _dot(hid.astype(BF16), wsd_ref[...])

    scores = jax.nn.sigmoid(_nt_dot(wr_ref[...], h2b))
    biased = scores + rb_ref[...]
    grp = biased.reshape(N_GROUPS, GROUP_SIZE, tm)
    gi = lax.broadcasted_iota(I32, (N_GROUPS, GROUP_SIZE, tm), 1).astype(F32)
    top1 = jnp.max(grp, axis=1, keepdims=True)
    first = jnp.min(jnp.where(grp == top1, gi, float(GROUP_SIZE)), axis=1, keepdims=True)
    top2 = jnp.max(jnp.where(gi == first, NEG_INF, grp), axis=1, keepdims=True)
    gscore = (top1 + top2).reshape(N_GROUPS, tm)
    gidx = lax.broadcasted_iota(I32, (N_GROUPS, tm), 0)
    grank = jnp.zeros((N_GROUPS, tm), F32)
    for g2 in range(N_GROUPS):
        rowv = gscore[g2:g2 + 1, :]
        beats = (rowv > gscore) | ((rowv == gscore) & (g2 < gidx))
        grank = grank + jnp.where(beats, 1.0, 0.0)
    gsel = jnp.where(grank < float(TOPK_GROUPS), 1.0, 0.0)
    emask = jnp.broadcast_to(gsel.reshape(N_GROUPS, 1, tm), (N_GROUPS, GROUP_SIZE, tm)).reshape(N_EXPERTS, tm)
    choice = jnp.where(emask > 0.5, biased, NEG_INF)

    eidx = lax.broadcasted_iota(I32, (N_EXPERTS, tm), 0).astype(F32)
    selmask = jnp.zeros((N_EXPERTS, tm), F32)
    e_rows = []
    w_rows = []
    for _k in range(TOP_K):
        mx = jnp.max(choice, axis=0, keepdims=True)
        idx = jnp.min(jnp.where(choice == mx, eidx, float(N_EXPERTS)), axis=0, keepdims=True)
        onehot = eidx == idx
        e_rows.append(idx)
        w_rows.append(jnp.sum(jnp.where(onehot, scores, 0.0), axis=0, keepdims=True))
        selmask = selmask + jnp.where(onehot, 1.0, 0.0)
        choice = jnp.where(onehot, NEG_INF, choice)
    wsum = w_rows[0]
    for wk in w_rows[1:]:
        wsum = wsum + wk

    tr = lax.broadcasted_iota(I32, (tm, tm), 0)
    tc = lax.broadcasted_iota(I32, (tm, tm), 1)
    upper = jnp.where(tr < tc, 1.0, 0.0).astype(BF16)
    prefix = _dot(selmask.astype(BF16), upper) + carry_s[...]
    r_rows = [jnp.sum(jnp.where(eidx == ek, prefix, 0.0), axis=0, keepdims=True) for ek in e_rows]
    carry_s[...] = carry_s[...] + jnp.sum(selmask, axis=1, keepdims=True)

    e_ref[...] = jnp.concatenate(e_rows, axis=0).astype(I32)
    w_ref[...] = jnp.concatenate([wk / wsum * ROUTED_SCALE for wk in w_rows], axis=0)
    rk_ref[...] = jnp.concatenate(r_rows, axis=0).astype(I32)
    cnt_ref[...] = carry_s[...].astype(I32)


def _mid(ret2, moba2, x2, mod3, wo1, wo2, g_ffn, wr_t, rbias, wsg, wsu, wsd, seq):
    n, d = x2.shape
    tm = TM_PROJ
    tiles_per_seq = seq // tm
    half = ret2.shape[1]
    ff = wsg.shape[1]
    const = lambda i: (0, 0)
    row = lambda i: (i, 0)
    colt = lambda i: (0, i)
    return pl.pallas_call(
        _mid_kernel,
        out_shape=(
            jax.ShapeDtypeStruct((n, d), F32),
            jax.ShapeDtypeStruct((n, d // 2), I32),
            jax.ShapeDtypeStruct((TOP_K, n), I32),
            jax.ShapeDtypeStruct((TOP_K, n), F32),
            jax.ShapeDtypeStruct((TOP_K, n), I32),
            jax.ShapeDtypeStruct((N_EXPERTS, 1), I32),
        ),
        grid=(n // tm,),
        in_specs=[
            pl.BlockSpec((tm, half), row),
            pl.BlockSpec((tm, half), row),
            pl.BlockSpec((tm, d), row),
            pl.BlockSpec((1, N_MOD, d), lambda i: (i // tiles_per_seq, 0, 0)),
            pl.BlockSpec((half, d), const),
            pl.BlockSpec((half, d), const),
            pl.BlockSpec((1, d), const),
            pl.BlockSpec((N_EXPERTS, d), const),
            pl.BlockSpec((N_EXPERTS, 1), const),
            pl.BlockSpec((d, ff), const),
            pl.BlockSpec((d, ff), const),
            pl.BlockSpec((ff, d), const),
        ],
        out_specs=(
            pl.BlockSpec((tm, d), row),
            pl.BlockSpec((tm, d // 2), row),
            pl.BlockSpec((TOP_K, tm), colt),
            pl.BlockSpec((TOP_K, tm), colt),
            pl.BlockSpec((TOP_K, tm), colt),
            pl.BlockSpec((N_EXPERTS, 1), const),
        ),
        scratch_shapes=[pltpu.VMEM((N_EXPERTS, 1), F32)],
        compiler_params=pltpu.CompilerParams(
            dimension_semantics=("arbitrary",), vmem_limit_bytes=VMEM_LIMIT),
        name="mid",
    )(ret2, moba2, x2, mod3, wo1, wo2, g_ffn, wr_t, rbias, wsg, wsu, wsd)


def _dest_kernel(start_ref, e_ref, rk_ref, o_ref):
    e = e_ref[...]

    def body(ex, acc):
        return acc + jnp.where(e == ex, start_ref[ex], 0)

    o_ref[...] = lax.fori_loop(0, N_EXPERTS, body, rk_ref[...], unroll=8)


def _dest(start, e_idx, rank):
    k, n = e_idx.shape
    tn = 2048
    return pl.pallas_call(
        _dest_kernel,
        out_shape=jax.ShapeDtypeStruct((k, n), I32),
        grid_spec=pltpu.PrefetchScalarGridSpec(
            num_scalar_prefetch=1,
            grid=(n // tn,),
            in_specs=[
                pl.BlockSpec((k, tn), lambda i, s: (0, i)),
                pl.BlockSpec((k, tn), lambda i, s: (0, i)),
            ],
            out_specs=pl.BlockSpec((k, tn), lambda i, s: (0, i)),
        ),
        compiler_params=pltpu.CompilerParams(vmem_limit_bytes=VMEM_LIMIT),
        name="dest",
    )(start, e_idx, rank)


def _sc_dispatch(h2p, dest3, total_rows):
    n, words = h2p.shape
    nchunks = n // SC_CHUNK
    per_worker = nchunks // (SC_CORES * SC_SUBCORES)
    mesh = plsc.VectorSubcoreMesh(core_axis_name="c", subcore_axis_name="s",
                                  num_cores=SC_CORES, num_subcores=SC_SUBCORES)

    @functools.partial(
        pl.kernel, mesh=mesh,
        out_type=jax.ShapeDtypeStruct((total_rows, words), I32),
        scratch_types=[
            pltpu.VMEM((TOP_K, SC_CHUNK), I32),
            pltpu.VMEM((SC_CHUNK, words), I32),
            pltpu.SemaphoreType.DMA,
        ],
        name="sc_dispatch",
    )
    def run(h_hbm, d_hbm, xs_hbm, idx_v, rows_v, sem):
        wid = lax.axis_index("s") * SC_CORES + lax.axis_index("c")

        @pl.loop(0, per_worker)
        def _(j):
            ch = wid * per_worker + j
            pltpu.sync_copy(d_hbm.at[ch], idx_v)
            pltpu.sync_copy(h_hbm.at[pl.ds(ch * SC_CHUNK, SC_CHUNK)], rows_v)
            copies = [pltpu.async_copy(rows_v, xs_hbm.at[idx_v.at[k]], sem) for k in range(TOP_K)]
            for cp in copies:
                cp.wait()

    return run(h2p, dest3)


def _ffn_kernel(head_ref, blk_ref, exp_ref, valid_ref, first_ref, ahead_ref, slot_ref,
                x_hbm, wg_hbm, wu_hbm, wd_hbm, y_ref, x_s, wg_s, wu_s, wd_s, sem_x, sem):
    i = pl.program_id(0)
    nreal = head_ref[0]
    rows_per = x_s.shape[1]

    def row_copy(g):
        slot = lax.rem(g, FFN_LOOKAHEAD + 1)
        return pltpu.make_async_copy(x_hbm.at[pl.ds(g * rows_per, rows_per)], x_s.at[slot], sem_x.at[slot])

    def weight_copies(e, s):
        return (pltpu.make_async_copy(wg_hbm.at[e], wg_s.at[s], sem.at[s, 0]),
                pltpu.make_async_copy(wu_hbm.at[e], wu_s.at[s], sem.at[s, 1]),
                pltpu.make_async_copy(wd_hbm.at[e], wd_s.at[s], sem.at[s, 2]))

    @pl.when(i == 0)
    def _():
        for j in range(FFN_WEIGHT_SLOTS - 1):
            @pl.when(head_ref[1 + j] >= 0)
            def _():
                for cp in weight_copies(head_ref[1 + j], j):
                    cp.start()
        for g in range(FFN_LOOKAHEAD):
            @pl.when(g < nreal)
            def _():
                row_copy(g).start()

    @pl.when(i < nreal)
    def _():
        s = slot_ref[i]

        @pl.when(i + FFN_LOOKAHEAD < nreal)
        def _():
            row_copy(i + FFN_LOOKAHEAD).start()

        row_copy(i).wait()

        @pl.when(first_ref[i] == 1)
        def _():
            for cp in weight_copies(exp_ref[i], s):
                cp.wait()

            @pl.when(ahead_ref[i] >= 0)
            def _():
                for cp in weight_copies(ahead_ref[i], lax.rem(s + FFN_WEIGHT_SLOTS - 1, FFN_WEIGHT_SLOTS)):
                    cp.start()

        half = x_s.shape[2]
        r = lax.broadcasted_iota(I32, (rows_per, 1), 0)
        x_blk = x_s[lax.rem(i, FFN_LOOKAHEAD + 1)]
        x_lo, x_hi = _unpack_halves(jnp.where(r < valid_ref[i], x_blk, 0))
        hg = _dot(x_lo, wg_s[s, :half, :]) + _dot(x_hi, wg_s[s, half:, :])
        hu = _dot(x_lo, wu_s[s, :half, :]) + _dot(x_hi, wu_s[s, half:, :])
        y_ref[...] = _pack_halves(_dot(_silu(hg) * hu, wd_s[s]))


def _ffn(sched, xs, w_gate, w_up, w_down):
    p, half = xs.shape
    d = 2 * half
    ff = w_gate.shape[2]
    steps = sched[1].shape[0]
    row = lambda i, n, b, *_: (b[i], 0)
    return pl.pallas_call(
        _ffn_kernel,
        out_shape=jax.ShapeDtypeStruct((p, half), I32),
        grid_spec=pltpu.PrefetchScalarGridSpec(
            num_scalar_prefetch=len(sched),
            grid=(steps,),
            in_specs=[
                pl.BlockSpec(memory_space=pl.ANY),
                pl.BlockSpec(memory_space=pl.ANY),
                pl.BlockSpec(memory_space=pl.ANY),
                pl.BlockSpec(memory_space=pl.ANY),
            ],
            out_specs=pl.BlockSpec((FFN_BLOCK, half), row),
            scratch_shapes=[
                pltpu.VMEM((FFN_LOOKAHEAD + 1, FFN_BLOCK, half), I32),
                pltpu.VMEM((FFN_WEIGHT_SLOTS, d, ff), F32),
                pltpu.VMEM((FFN_WEIGHT_SLOTS, d, ff), F32),
                pltpu.VMEM((FFN_WEIGHT_SLOTS, ff, d), F32),
                pltpu.SemaphoreType.DMA((FFN_LOOKAHEAD + 1,)),
                pltpu.SemaphoreType.DMA((FFN_WEIGHT_SLOTS, 3)),
            ],
        ),
        compiler_params=pltpu.CompilerParams(
            dimension_semantics=("arbitrary",), vmem_limit_bytes=VMEM_LIMIT),
        name="ffn",
    )(*sched, xs, w_gate, w_up, w_down)


def _sc_gather(y, dest3):
    a, words = y.shape
    nchunks, _, chunk = dest3.shape
    n = nchunks * chunk
    per_worker = nchunks // (SC_CORES * SC_SUBCORES)
    mesh = plsc.VectorSubcoreMesh(core_axis_name="c", subcore_axis_name="s",
                                  num_cores=SC_CORES, num_subcores=SC_SUBCORES)

    @functools.partial(
        pl.kernel, mesh=mesh,
        out_type=jax.ShapeDtypeStruct((TOP_K, n, words), I32),
        scratch_types=[
            pltpu.VMEM((TOP_K, chunk), I32),
            pltpu.VMEM((chunk, words), I32),
            pltpu.VMEM((chunk, words), I32),
            pltpu.SemaphoreType.DMA,
            pltpu.SemaphoreType.DMA((2,)),
        ],
        name="sc_gather",
    )
    def run(y_hbm, d_hbm, yt_hbm, idx_v, buf0, buf1, sem_g, sem_w):
        wid = lax.axis_index("s") * SC_CORES + lax.axis_index("c")
        bufs = (buf0, buf1)

        @pl.loop(0, per_worker)
        def _(j):
            ch = wid * per_worker + j
            pltpu.sync_copy(d_hbm.at[ch], idx_v)
            rows = pl.ds(ch * chunk, chunk)
            gather = pltpu.async_copy(y_hbm.at[idx_v.at[0]], bufs[0], sem_g)
            writes = []
            for k in range(TOP_K):
                gather.wait()
                writes.append(pltpu.async_copy(bufs[k % 2], yt_hbm.at[k, rows], sem_w.at[k % 2]))
                if k + 1 < TOP_K:
                    if k >= 1:
                        writes[k - 1].wait()
                    gather = pltpu.async_copy(y_hbm.at[idx_v.at[k + 1]], bufs[(k + 1) % 2], sem_g)
            writes[TOP_K - 2].wait()
            writes[TOP_K - 1].wait()

    return run(y, dest3)


def _combine_kernel(yt_ref, wt_ref, xb_ref, mod_ref, o_ref):
    half = yt_ref.shape[2]
    wt = wt_ref[...]
    lo, hi = _unpack_halves(yt_ref[0])
    r_lo = lo * wt[:, 0:1]
    r_hi = hi * wt[:, 0:1]
    for k in range(1, TOP_K):
        lo, hi = _unpack_halves(yt_ref[k])
        r_lo = r_lo + lo * wt[:, k:k + 1]
        r_hi = r_hi + hi * wt[:, k:k + 1]
    gate = mod_ref[0][5:6]
    o_ref[:, :half] = xb_ref[:, :half] + gate[:, :half] * r_lo
    o_ref[:, half:] = xb_ref[:, half:] + gate[:, half:] * r_hi


def _combine(yt, w_t, xb, mod3, seq):
    n, d = xb.shape
    tm = TM_COMBINE
    tiles_per_seq = seq // tm
    return pl.pallas_call(
        _combine_kernel,
        out_shape=jax.ShapeDtypeStruct((n, d), F32),
        grid=(n // tm,),
        in_specs=[
            pl.BlockSpec((TOP_K, tm, d // 2), lambda i: (0, i, 0)),
            pl.BlockSpec((tm, TOP_K), lambda i: (i, 0)),
            pl.BlockSpec((tm, d), lambda i: (i, 0)),
            pl.BlockSpec((1, N_MOD, d), lambda i: (i // tiles_per_seq, 0, 0)),
        ],
        out_specs=pl.BlockSpec((tm, d), lambda i: (i, 0)),
        compiler_params=pltpu.CompilerParams(vmem_limit_bytes=VMEM_LIMIT),
        name="combine",
    )(yt, w_t, xb, mod3)


def _rotary_tables(seq):
    half = RET_DK // 2
    pos = jnp.arange(seq, dtype=F32)
    inv = ROPE_BASE ** (-jnp.arange(half, dtype=F32) / half)
    ang = pos[:, None] * inv[None, :]
    cos = jnp.cos(ang)
    sin = jnp.sin(ang)
    return jnp.concatenate([cos, cos], axis=-1), jnp.concatenate([-sin, sin], axis=-1)


def _ffn_schedule(counts, nblk):
    padded = (counts + FFN_BLOCK - 1) // FFN_BLOCK * FFN_BLOCK
    ends = jnp.cumsum(padded)
    start = ends - padded
    nreal = ends[-1:] // FFN_BLOCK
    blk = jnp.minimum(jnp.arange(nblk, dtype=I32), nreal - 1)
    row0 = blk * FFN_BLOCK
    exp = jnp.sum((ends[None, :] <= row0[:, None]).astype(I32), axis=1)
    eids = jnp.arange(N_EXPERTS, dtype=I32)
    mine = exp[:, None] == eids[None, :]

    def per_block(table):
        return jnp.sum(jnp.where(mine, table[None, :], 0), axis=1)

    valid = jnp.clip(per_block(counts + start) - row0, 0, FFN_BLOCK)
    steps = jnp.arange(nblk, dtype=I32)
    first = ((steps < nreal) & ((steps == 0) | (exp != jnp.roll(exp, 1)))).astype(I32)
    used = counts > 0
    ordinal = jnp.cumsum(used.astype(I32)) - 1

    def used_at(pos):
        hit = used[None, :] & (ordinal[None, :] == pos[:, None])
        return jnp.sum(jnp.where(hit, eids[None, :] + 1, 0), axis=1) - 1

    ahead = used_at(ordinal + (FFN_WEIGHT_SLOTS - 1))
    head = jnp.concatenate([nreal, used_at(jnp.arange(FFN_WEIGHT_SLOTS - 1, dtype=I32))])
    return start, (head, blk, exp, valid, first, per_block(ahead), per_block(ordinal % FFN_WEIGHT_SLOTS))


def _chunked(dest, chunk):
    k, n = dest.shape
    return dest.reshape(k, n // chunk, chunk).transpose(1, 0, 2)


def kernel(x, c, w_ada, b_ada, g_mix, w_in, q_gain, k_gain, w_out, g_ffn, w_router, router_bias,
           w_gate, w_up, w_down, ws_gate, ws_up, ws_down):
    bsz, seq, d = x.shape
    n = bsz * seq
    depth = w_ada.shape[0]
    cos_full, sin_signed = _rotary_tables(seq)
    log_g = jnp.log1p(-jnp.exp2(-5.0 - jnp.arange(RET_HEADS, dtype=F32)))
    ret_w = RET_HEADS * RET_DK
    x2 = x.reshape(n, d)
    for l in range(depth):
        mod3 = _adaln(c, w_ada[l], b_ada[l]).reshape(bsz, N_MOD, d)
        proj = _inproj(x2, mod3, g_mix[l].reshape(1, d), w_in[l].astype(BF16), cos_full, sin_signed, seq)
        proj3 = proj.reshape(bsz, seq, IN_COLS)
        ret = _retention(log_g, proj3)
        qg2 = jnp.tile(q_gain[l].reshape(1, MOBA_DH), (1, 2))
        kg2 = jnp.tile(k_gain[l].reshape(1, MOBA_DH), (1, 2))
        moba = _moba(proj3, qg2, kg2)
        wo = w_out[l].astype(BF16)
        xb, h2, e_idx, w_k, rank, cnt = _mid(
            ret.reshape(n, ret_w), moba.reshape(n, MOBA_HEADS * MOBA_DH), x2, mod3,
            wo[:ret_w], wo[ret_w:], g_ffn[l].reshape(1, d),
            w_router[l].T.astype(BF16), router_bias[l].reshape(N_EXPERTS, 1),
            ws_gate[l].astype(BF16), ws_up[l].astype(BF16), ws_down[l].astype(BF16), seq)
        nblk = n * TOP_K // FFN_BLOCK + N_EXPERTS
        start, sched = _ffn_schedule(cnt[:, 0], nblk)
        dest = _dest(start, e_idx, rank)
        xs = _sc_dispatch(h2, _chunked(dest, SC_CHUNK), nblk * FFN_BLOCK)
        y = _ffn(sched, xs, w_gate[l], w_up[l], w_down[l])
        yt = _sc_gather(y, _chunked(dest, SC_GATHER_CHUNK))
        x2 = _combine(yt, w_k.T, xb, mod3, seq)
    return x2.reshape(bsz, seq, d)
```

```python
import functools

import numpy as np
import jax
import jax.numpy as jnp
from jax import lax
from jax.experimental import pallas as pl
from jax.experimental.pallas import tpu as pltpu
from jax.experimental.pallas import tpu_sc as plsc

F32 = jnp.float32
BF16 = jnp.bfloat16
I32 = jnp.int32

D_MODEL = 1024
RET_HEADS = 4
RET_DK = 128
MOBA_HEADS = 8
MOBA_DH = 64
MOBA_BLOCK = 256
MOBA_TOPK = 3
ROPE_BASE = 10000.0
N_EXPERTS = 256
TOP_K = 8
N_GROUPS = 8
TOPK_GROUPS = 4
GROUP_SIZE = N_EXPERTS // N_GROUPS
EXPERT_FF = 256
ROUTED_SCALE = 2.5
N_MOD = 6
EPS = 1e-6
IN_COLS = 3584

LANES = 128
RET_CHUNK = 256
TM_PROJ = 512
TM_COMBINE = 256
SC_CORES = 2
SC_SUBCORES = 16
SC_CHUNK = 128
SC_GATHER_CHUNK = 64
FFN_BLOCK = 256
FFN_LOOKAHEAD = 5
FFN_WEIGHT_SLOTS = 3
VMEM_LIMIT = 56 * 1024 * 1024

NEG_INF = float("-inf")
LOG2_E = 1.4426950408889634


def _silu(x):
    return x * jax.nn.sigmoid(x)


def _nt_dot(a, b):
    return lax.dot_general(a, b, (((1,), (1,)), ((), ())), preferred_element_type=F32)


def _tn_dot(a, b):
    return lax.dot_general(a, b, (((0,), (0,)), ((), ())), preferred_element_type=F32)


def _dot(a, b):
    return jnp.dot(a, b, preferred_element_type=F32)


HI_MASK = -65536


def _pack_halves(v):
    w = v.shape[1] // 2
    lo = lax.bitcast_convert_type(v[:, :w].astype(BF16).astype(F32), I32)
    hi = lax.bitcast_convert_type(v[:, w:].astype(BF16).astype(F32), I32)
    return lax.shift_right_logical(lo, 16) | (hi & HI_MASK)


def _unpack_halves(u):
    lo = lax.bitcast_convert_type(lax.shift_left(u, 16), F32)
    hi = lax.bitcast_convert_type(u & HI_MASK, F32)
    return lo, hi


def _adaln_kernel(c_ref, w_ref, b_ref, o_ref):
    s = _silu(c_ref[...])
    o_ref[...] = _dot(s.astype(BF16), w_ref[...].astype(BF16)) + b_ref[...]


def _adaln(c, w_ada, b_ada):
    bsz, d = c.shape
    ncol = w_ada.shape[1]
    tn = 1024
    return pl.pallas_call(
        _adaln_kernel,
        out_shape=jax.ShapeDtypeStruct((bsz, ncol), F32),
        grid=(ncol // tn,),
        in_specs=[
            pl.BlockSpec((bsz, d), lambda j: (0, 0)),
            pl.BlockSpec((d, tn), lambda j: (0, j)),
            pl.BlockSpec((1, tn), lambda j: (0, j)),
        ],
        out_specs=pl.BlockSpec((bsz, tn), lambda j: (0, j)),
        compiler_params=pltpu.CompilerParams(vmem_limit_bytes=VMEM_LIMIT),
        name="adaln",
    )(c, w_ada, b_ada.reshape(1, ncol))


def _inproj_kernel(x_ref, mod_ref, g_ref, w_ref, cos_ref, sin_ref, o_ref):
    x = x_ref[...]
    ms = jnp.mean(x * x, axis=-1, keepdims=True)
    m = mod_ref[0]
    h = (x * lax.rsqrt(ms + EPS) * g_ref[...]) * (1.0 + m[1:2]) + m[0:1]
    hb = h.astype(BF16)
    cosf = cos_ref[...]
    sinf = sin_ref[...]
    k_scale = RET_DK ** -0.5
    width = RET_HEADS * RET_DK
    for ci in range(IN_COLS // width):
        acc = _dot(hb, w_ref[:, ci * width:(ci + 1) * width])
        if ci < 2:
            for hh in range(RET_HEADS):
                xh = acc[:, hh * RET_DK:(hh + 1) * RET_DK]
                r = xh * cosf + pltpu.roll(xh, RET_DK // 2, axis=1) * sinf
                if ci == 1:
                    r = r * k_scale
                o_ref[:, ci * width + hh * RET_DK:ci * width + (hh + 1) * RET_DK] = r.astype(BF16)
        else:
            o_ref[:, ci * width:(ci + 1) * width] = acc.astype(BF16)


def _inproj(x2, mod3, g_mix, w_in_bf, cos_full, sin_signed, seq):
    n, d = x2.shape
    tm = TM_PROJ
    tiles_per_seq = seq // tm
    return pl.pallas_call(
        _inproj_kernel,
        out_shape=jax.ShapeDtypeStruct((n, IN_COLS), BF16),
        grid=(n // tm,),
        in_specs=[
            pl.BlockSpec((tm, d), lambda i: (i, 0)),
            pl.BlockSpec((1, N_MOD, d), lambda i: (i // tiles_per_seq, 0, 0)),
            pl.BlockSpec((1, d), lambda i: (0, 0)),
            pl.BlockSpec((d, IN_COLS), lambda i: (0, 0)),
            pl.BlockSpec((tm, LANES), lambda i: (i % tiles_per_seq, 0)),
            pl.BlockSpec((tm, LANES), lambda i: (i % tiles_per_seq, 0)),
        ],
        out_specs=pl.BlockSpec((tm, IN_COLS), lambda i: (i, 0)),
        compiler_params=pltpu.CompilerParams(vmem_limit_bytes=VMEM_LIMIT),
        name="inproj",
    )(x2, mod3, g_mix, w_in_bf, cos_full, sin_signed)


def _ret_kernel(lg_ref, q_ref, k_ref, v_ref, g_ref, o_ref):
    seq = q_ref.shape[1]
    c = RET_CHUNK
    lg = lg_ref[pl.program_id(1)]
    row = lax.broadcasted_iota(I32, (c, c), 0)
    col = lax.broadcasted_iota(I32, (c, c), 1)
    diff = (row - col).astype(F32)
    dmask = jnp.where(diff >= 0, jnp.exp(lg * jnp.maximum(diff, 0.0)), 0.0)
    idx = lax.broadcasted_iota(I32, (c, 1), 0).astype(F32)
    q_decay = jnp.exp(lg * (idx + 1.0))
    k_decay = jnp.exp(lg * (c - 1.0 - idx))
    chunk_decay = jnp.exp(jnp.full((1, 1), lg * c, F32))
    state = jnp.zeros((RET_DK, RET_DK), F32)
    for n in range(seq // c):
        rows = slice(n * c, (n + 1) * c)
        qn = q_ref[0, rows, :]
        kn = k_ref[0, rows, :]
        vn = v_ref[0, rows, :]
        scores = _nt_dot(qn, kn) * dmask
        inner = _dot(scores.astype(BF16), vn)
        qs = (qn.astype(F32) * q_decay).astype(BF16)
        cross = _dot(qs, state.astype(BF16))
        o = inner + cross
        o = o * lax.rsqrt(jnp.mean(o * o, axis=-1, keepdims=True) + EPS)
        gn = g_ref[0, rows, :].astype(F32)
        o_ref[0, rows, :] = (_silu(gn) * o).astype(BF16)
        ks = (kn.astype(F32) * k_decay).astype(BF16)
        state = state * chunk_decay + _tn_dot(ks, vn)


def _retention(log_g, proj3):
    bsz, seq, _ = proj3.shape
    blk = (1, seq, RET_DK)
    return pl.pallas_call(
        _ret_kernel,
        out_shape=jax.ShapeDtypeStruct((bsz, seq, RET_HEADS * RET_DK), BF16),
        grid_spec=pltpu.PrefetchScalarGridSpec(
            num_scalar_prefetch=1,
            grid=(bsz, RET_HEADS),
            in_specs=[
                pl.BlockSpec(blk, lambda b, h, lg: (b, 0, h)),
                pl.BlockSpec(blk, lambda b, h, lg: (b, 0, RET_HEADS + h)),
                pl.BlockSpec(blk, lambda b, h, lg: (b, 0, 2 * RET_HEADS + h)),
                pl.BlockSpec(blk, lambda b, h, lg: (b, 0, 3 * RET_HEADS + h)),
            ],
            out_specs=pl.BlockSpec(blk, lambda b, h, lg: (b, 0, h)),
        ),
        compiler_params=pltpu.CompilerParams(vmem_limit_bytes=VMEM_LIMIT),
        name="retention",
    )(log_g, proj3, proj3, proj3, proj3)


def _moba_kernel(q_ref, k_ref, v_ref, qg_ref, kg_ref, o_ref, qt_s, ka_s, kb_s, vta_s, vtb_s):
    seq = q_ref.shape[1]
    lb = MOBA_BLOCK
    nb = seq // lb
    lane = lax.broadcasted_iota(I32, (1, LANES), 1)
    is_a = lane < MOBA_DH
    sub = lax.broadcasted_iota(I32, (LANES, 1), 0)
    top = sub < MOBA_DH

    def head_norm(xf, gain):
        sq = xf * xf
        s_a = jnp.sum(jnp.where(is_a, sq, 0.0), axis=-1, keepdims=True)
        s_b = jnp.sum(jnp.where(is_a, 0.0, sq), axis=-1, keepdims=True)
        inv = jnp.where(is_a, lax.rsqrt(s_a / MOBA_DH + EPS), lax.rsqrt(s_b / MOBA_DH + EPS))
        return xf * inv * gain

    qg = qg_ref[...]
    kg = kg_ref[...]
    k_means = []
    for j in range(nb):
        rows = slice(j * lb, (j + 1) * lb)
        kf = head_norm(k_ref[0, rows, :].astype(F32), kg)
        ka_s[rows, :] = jnp.where(is_a, kf, 0.0).astype(BF16)
        kb_s[rows, :] = jnp.where(is_a, 0.0, kf).astype(BF16)
        k_means.append(jnp.mean(kf, axis=0, keepdims=True))
        qf = head_norm(q_ref[0, rows, :].astype(F32), qg)
        qt_s[:, rows] = (qf * (MOBA_DH ** -0.5 * LOG2_E)).T.astype(BF16)
        vt = v_ref[0, rows, :].astype(F32).T
        vta_s[:, rows] = jnp.where(top, vt, 1.0).astype(BF16)
        vtb_s[:, rows] = jnp.where(top, 1.0, vt).astype(BF16)
    k_mean = jnp.concatenate(k_means + [jnp.zeros((16 - nb, LANES), F32)], axis=0)
    k_mean_h = (jnp.where(is_a, k_mean, 0.0).astype(BF16), jnp.where(is_a, 0.0, k_mean).astype(BF16))
    k_s = (ka_s, kb_s)
    vt_s = (vta_s, vtb_s)

    r_loc = lax.broadcasted_iota(I32, (lb, lb), 0)
    c_loc = lax.broadcasted_iota(I32, (lb, lb), 1)
    causal = r_loc <= c_loc

    for i in range(nb):
        cols = slice(i * lb, (i + 1) * lb)
        qt = qt_s[:, cols]
        outs = []
        for hx in range(2):
            bias = [None] * i
            if i > MOBA_TOPK:
                gate = _dot(k_mean_h[hx], qt)
                g = [gate[j:j + 1, :] for j in range(i)]
                for j in range(i):
                    rank = jnp.zeros((1, lb), F32)
                    for j2 in range(i):
                        if j2 == j:
                            continue
                        beats = (g[j2] >= g[j]) if j2 < j else (g[j2] > g[j])
                        rank = rank + jnp.where(beats, 1.0, 0.0)
                    bias[j] = jnp.where(rank < float(MOBA_TOPK), 0.0, NEG_INF)
            pieces = []
            for j in range(i + 1):
                s = _dot(k_s[hx][j * lb:(j + 1) * lb, :], qt)
                if j == i:
                    s = jnp.where(causal, s, NEG_INF)
                elif bias[j] is not None:
                    s = s + bias[j]
                pieces.append(s)
            mx = jnp.max(pieces[0], axis=0, keepdims=True)
            for s in pieces[1:]:
                mx = jnp.maximum(mx, jnp.max(s, axis=0, keepdims=True))
            acc = jnp.zeros((LANES, lb), F32)
            for j, s in enumerate(pieces):
                p = jnp.exp2(s - mx).astype(BF16)
                acc = acc + _dot(vt_s[hx][:, j * lb:(j + 1) * lb], p)
            den = acc[MOBA_DH:MOBA_DH + 1, :] if hx == 0 else acc[0:1, :]
            outs.append(acc / den)
        o_ref[0, cols, :] = jnp.where(top, outs[0], outs[1]).T.astype(BF16)


def _moba(proj3, qg2, kg2):
    bsz, seq, _ = proj3.shape
    pairs = MOBA_HEADS // 2
    blk = (1, seq, LANES)
    base = 4 * RET_HEADS
    return pl.pallas_call(
        _moba_kernel,
        out_shape=jax.ShapeDtypeStruct((bsz, seq, MOBA_HEADS * MOBA_DH), BF16),
        grid=(bsz, pairs),
        in_specs=[
            pl.BlockSpec(blk, lambda b, p: (b, 0, base + p)),
            pl.BlockSpec(blk, lambda b, p: (b, 0, base + pairs + p)),
            pl.BlockSpec(blk, lambda b, p: (b, 0, base + 2 * pairs + p)),
            pl.BlockSpec((1, LANES), lambda b, p: (0, 0)),
            pl.BlockSpec((1, LANES), lambda b, p: (0, 0)),
        ],
        out_specs=pl.BlockSpec(blk, lambda b, p: (b, 0, p)),
        scratch_shapes=[
            pltpu.VMEM((LANES, seq), BF16),
            pltpu.VMEM((seq, LANES), BF16),
            pltpu.VMEM((seq, LANES), BF16),
            pltpu.VMEM((LANES, seq), BF16),
            pltpu.VMEM((LANES, seq), BF16),
        ],
        compiler_params=pltpu.CompilerParams(vmem_limit_bytes=VMEM_LIMIT),
        name="moba",
    )(proj3, proj3, proj3, qg2, kg2)


def _mid_kernel(ret_ref, moba_ref, x_ref, mod_ref, wo1_ref, wo2_ref, g_ref, wr_ref, rb_ref,
                wsg_ref, wsu_ref, wsd_ref,
                xb_ref, h2_ref, e_ref, w_ref, rk_ref, cnt_ref, carry_s):
    i = pl.program_id(0)
    tm = x_ref.shape[0]

    @pl.when(i == 0)
    def _():
        carry_s[...] = jnp.zeros_like(carry_s)

    m = mod_ref[0]
    mixed = _dot(ret_ref[...], wo1_ref[...]) + _dot(moba_ref[...], wo2_ref[...])
    x1 = x_ref[...] + m[2:3] * mixed
    ms = jnp.mean(x1 * x1, axis=-1, keepdims=True)
    h2 = (x1 * lax.rsqrt(ms + EPS) * g_ref[...]) * (1.0 + m[4:5]) + m[3:4]
    h2_ref[...] = _pack_halves(h2)
    h2b = h2.astype(BF16)

    hid = _silu(_dot(h2b, wsg_ref[...])) * _dot(h2b, wsu_ref[...])
    xb_ref[...] = x1 + m[5:6] * _dot(hid.astype(BF16), wsd_ref[...])

    scores = jax.nn.sigmoid(_nt_dot(wr_ref[...], h2b))
    biased = scores + rb_ref[...]
    grp = biased.reshape(N_GROUPS, GROUP_SIZE, tm)
    gi = lax.broadcasted_iota(I32, (N_GROUPS, GROUP_SIZE, tm), 1).astype(F32)
    top1 = jnp.max(grp, axis=1, keepdims=True)
    first = jnp.min(jnp.where(grp == top1, gi, float(GROUP_SIZE)), axis=1, keepdims=True)
    top2 = jnp.max(jnp.where(gi == first, NEG_INF, grp), axis=1, keepdims=True)
    gscore = (top1 + top2).reshape(N_GROUPS, tm)
    gidx = lax.broadcasted_iota(I32, (N_GROUPS, tm), 0)
    grank = jnp.zeros((N_GROUPS, tm), F32)
    for g2 in range(N_GROUPS):
        rowv = gscore[g2:g2 + 1, :]
        beats = (rowv > gscore) | ((rowv == gscore) & (g2 < gidx))
        grank = grank + jnp.where(beats, 1.0, 0.0)
    gsel = jnp.where(grank < float(TOPK_GROUPS), 1.0, 0.0)
    emask = jnp.broadcast_to(gsel.reshape(N_GROUPS, 1, tm), (N_GROUPS, GROUP_SIZE, tm)).reshape(N_EXPERTS, tm)
    choice = jnp.where(emask > 0.5, biased, NEG_INF)

    eidx = lax.broadcasted_iota(I32, (N_EXPERTS, tm), 0).astype(F32)
    remaining = choice
    e_rows = []
    for _k in range(TOP_K):
        mx = jnp.max(remaining, axis=0, keepdims=True)
        idx = jnp.min(jnp.where(remaining == mx, eidx, float(N_EXPERTS)), axis=0, keepdims=True)
        e_rows.append(idx)
        remaining = jnp.where(eidx == idx, NEG_INF, remaining)
    selmask = jnp.where(remaining != choice, 1.0, 0.0)

    tr = lax.broadcasted_iota(I32, (tm, tm), 0)
    tc = lax.broadcasted_iota(I32, (tm, tm), 1)
    upper = jnp.where(tr < tc, 1.0, 0.0).astype(BF16)
    prefix = _dot(selmask.astype(BF16), upper) + carry_s[...]
    w_rows = []
    r_rows = []
    for ek in e_rows:
        onehot = eidx == ek
        w_rows.append(jnp.sum(jnp.where(onehot, scores, 0.0), axis=0, keepdims=True))
        r_rows.append(jnp.sum(jnp.where(onehot, prefix, 0.0), axis=0, keepdims=True))
    wsum = w_rows[0]
    for wk in w_rows[1:]:
        wsum = wsum + wk
    carry_s[...] = carry_s[...] + jnp.sum(selmask, axis=1, keepdims=True)

    e_ref[...] = jnp.concatenate(e_rows, axis=0).astype(I32)
    w_ref[...] = jnp.concatenate([wk / wsum * ROUTED_SCALE for wk in w_rows], axis=0)
    rk_ref[...] = jnp.concatenate(r_rows, axis=0).astype(I32)
    cnt_ref[...] = carry_s[...].astype(I32)


def _mid(ret2, moba2, x2, mod3, wo1, wo2, g_ffn, wr_t, rbias, wsg, wsu, wsd, seq):
    n, d = x2.shape
    tm = TM_PROJ
    tiles_per_seq = seq // tm
    half = ret2.shape[1]
    ff = wsg.shape[1]
    const = lambda i: (0, 0)
    row = lambda i: (i, 0)
    colt = lambda i: (0, i)
    return pl.pallas_call(
        _mid_kernel,
        out_shape=(
            jax.ShapeDtypeStruct((n, d), F32),
            jax.ShapeDtypeStruct((n, d // 2), I32),
            jax.ShapeDtypeStruct((TOP_K, n), I32),
            jax.ShapeDtypeStruct((TOP_K, n), F32),
            jax.ShapeDtypeStruct((TOP_K, n), I32),
            jax.ShapeDtypeStruct((N_EXPERTS, 1), I32),
        ),
        grid=(n // tm,),
        in_specs=[
            pl.BlockSpec((tm, half), row),
            pl.BlockSpec((tm, half), row),
            pl.BlockSpec((tm, d), row),
            pl.BlockSpec((1, N_MOD, d), lambda i: (i // tiles_per_seq, 0, 0)),
            pl.BlockSpec((half, d), const),
            pl.BlockSpec((half, d), const),
            pl.BlockSpec((1, d), const),
            pl.BlockSpec((N_EXPERTS, d), const),
            pl.BlockSpec((N_EXPERTS, 1), const),
            pl.BlockSpec((d, ff), const),
            pl.BlockSpec((d, ff), const),
            pl.BlockSpec((ff, d), const),
        ],
        out_specs=(
            pl.BlockSpec((tm, d), row),
            pl.BlockSpec((tm, d // 2), row),
            pl.BlockSpec((TOP_K, tm), colt),
            pl.BlockSpec((TOP_K, tm), colt),
            pl.BlockSpec((TOP_K, tm), colt),
            pl.BlockSpec((N_EXPERTS, 1), const),
        ),
        scratch_shapes=[pltpu.VMEM((N_EXPERTS, 1), F32)],
        compiler_params=pltpu.CompilerParams(
            dimension_semantics=("arbitrary",), vmem_limit_bytes=VMEM_LIMIT),
        name="mid",
    )(ret2, moba2, x2, mod3, wo1, wo2, g_ffn, wr_t, rbias, wsg, wsu, wsd)


def _dest_kernel(start_ref, e_ref, rk_ref, o_ref):
    e = e_ref[...]

    def body(ex, acc):
        return acc + jnp.where(e == ex, start_ref[ex], 0)

    o_ref[...] = lax.fori_loop(0, N_EXPERTS, body, rk_ref[...], unroll=8)


def _dest(start, e_idx, rank):
    k, n = e_idx.shape
    tn = 2048
    return pl.pallas_call(
        _dest_kernel,
        out_shape=jax.ShapeDtypeStruct((k, n), I32),
        grid_spec=pltpu.PrefetchScalarGridSpec(
            num_scalar_prefetch=1,
            grid=(n // tn,),
            in_specs=[
                pl.BlockSpec((k, tn), lambda i, s: (0, i)),
                pl.BlockSpec((k, tn), lambda i, s: (0, i)),
            ],
            out_specs=pl.BlockSpec((k, tn), lambda i, s: (0, i)),
        ),
        compiler_params=pltpu.CompilerParams(vmem_limit_bytes=VMEM_LIMIT),
        name="dest",
    )(start, e_idx, rank)


def _sc_dispatch(h2p, dest3, total_rows):
    n, words = h2p.shape
    nchunks = n // SC_CHUNK
    per_worker = nchunks // (SC_CORES * SC_SUBCORES)
    mesh = plsc.VectorSubcoreMesh(core_axis_name="c", subcore_axis_name="s",
                                  num_cores=SC_CORES, num_subcores=SC_SUBCORES)

    @functools.partial(
        pl.kernel, mesh=mesh,
        out_type=jax.ShapeDtypeStruct((total_rows, words), I32),
        scratch_types=[
            pltpu.VMEM((TOP_K, SC_CHUNK), I32),
            pltpu.VMEM((SC_CHUNK, words), I32),
            pltpu.SemaphoreType.DMA,
        ],
        name="sc_dispatch",
    )
    def run(h_hbm, d_hbm, xs_hbm, idx_v, rows_v, sem):
        wid = lax.axis_index("s") * SC_CORES + lax.axis_index("c")

        @pl.loop(0, per_worker)
        def _(j):
            ch = wid * per_worker + j
            pltpu.sync_copy(d_hbm.at[ch], idx_v)
            pltpu.sync_copy(h_hbm.at[pl.ds(ch * SC_CHUNK, SC_CHUNK)], rows_v)
            copies = [pltpu.async_copy(rows_v, xs_hbm.at[idx_v.at[k]], sem) for k in range(TOP_K)]
            for cp in copies:
                cp.wait()

    return run(h2p, dest3)


def _ffn_kernel(head_ref, blk_ref, exp_ref, valid_ref, first_ref, ahead_ref, slot_ref,
                x_hbm, wg_hbm, wu_hbm, wd_hbm, y_ref, x_s, wg_s, wu_s, wd_s, sem_x, sem):
    i = pl.program_id(0)
    nreal = head_ref[0]
    rows_per = x_s.shape[1]

    def row_copy(g):
        slot = lax.rem(g, FFN_LOOKAHEAD + 1)
        return pltpu.make_async_copy(x_hbm.at[pl.ds(g * rows_per, rows_per)], x_s.at[slot], sem_x.at[slot])

    def weight_copies(e, s):
        return (pltpu.make_async_copy(wg_hbm.at[e], wg_s.at[s], sem.at[s, 0]),
                pltpu.make_async_copy(wu_hbm.at[e], wu_s.at[s], sem.at[s, 1]),
                pltpu.make_async_copy(wd_hbm.at[e], wd_s.at[s], sem.at[s, 2]))

    @pl.when(i == 0)
    def _():
        for j in range(FFN_WEIGHT_SLOTS - 1):
            @pl.when(head_ref[1 + j] >= 0)
            def _():
                for cp in weight_copies(head_ref[1 + j], j):
                    cp.start()
        for g in range(FFN_LOOKAHEAD):
            @pl.when(g < nreal)
            def _():
                row_copy(g).start()

    @pl.when(i < nreal)
    def _():
        s = slot_ref[i]

        @pl.when(i + FFN_LOOKAHEAD < nreal)
        def _():
            row_copy(i + FFN_LOOKAHEAD).start()

        row_copy(i).wait()

        @pl.when(first_ref[i] == 1)
        def _():
            for cp in weight_copies(exp_ref[i], s):
                cp.wait()

            @pl.when(ahead_ref[i] >= 0)
            def _():
                for cp in weight_copies(ahead_ref[i], lax.rem(s + FFN_WEIGHT_SLOTS - 1, FFN_WEIGHT_SLOTS)):
                    cp.start()

        half = x_s.shape[2]
        valid = valid_ref[i]
        x_slot = lax.rem(i, FFN_LOOKAHEAD + 1)

        def expert_rows(nrows):
            r = lax.broadcasted_iota(I32, (nrows, 1), 0)
            x_lo, x_hi = _unpack_halves(jnp.where(r < valid, x_s[x_slot, :nrows, :], 0))
            hg = _dot(x_lo, wg_s[s, :half, :]) + _dot(x_hi, wg_s[s, half:, :])
            hu = _dot(x_lo, wu_s[s, :half, :]) + _dot(x_hi, wu_s[s, half:, :])
            y_ref[:nrows, :] = _pack_halves(_dot(_silu(hg) * hu, wd_s[s]))

        @pl.when(valid > rows_per // 2)
        def _():
            expert_rows(rows_per)

        @pl.when(valid <= rows_per // 2)
        def _():
            expert_rows(rows_per // 2)
            y_ref[rows_per // 2:, :] = jnp.zeros((rows_per - rows_per // 2, half), I32)


def _ffn(sched, xs, w_gate, w_up, w_down):
    p, half = xs.shape
    d = 2 * half
    ff = w_gate.shape[2]
    steps = sched[1].shape[0]
    row = lambda i, n, b, *_: (b[i], 0)
    return pl.pallas_call(
        _ffn_kernel,
        out_shape=jax.ShapeDtypeStruct((p, half), I32),
        grid_spec=pltpu.PrefetchScalarGridSpec(
            num_scalar_prefetch=len(sched),
            grid=(steps,),
            in_specs=[
                pl.BlockSpec(memory_space=pl.ANY),
                pl.BlockSpec(memory_space=pl.ANY),
                pl.BlockSpec(memory_space=pl.ANY),
                pl.BlockSpec(memory_space=pl.ANY),
            ],
            out_specs=pl.BlockSpec((FFN_BLOCK, half), row),
            scratch_shapes=[
                pltpu.VMEM((FFN_LOOKAHEAD + 1, FFN_BLOCK, half), I32),
                pltpu.VMEM((FFN_WEIGHT_SLOTS, d, ff), F32),
                pltpu.VMEM((FFN_WEIGHT_SLOTS, d, ff), F32),
                pltpu.VMEM((FFN_WEIGHT_SLOTS, ff, d), F32),
                pltpu.SemaphoreType.DMA((FFN_LOOKAHEAD + 1,)),
                pltpu.SemaphoreType.DMA((FFN_WEIGHT_SLOTS, 3)),
            ],
        ),
        compiler_params=pltpu.CompilerParams(
            dimension_semantics=("arbitrary",), vmem_limit_bytes=VMEM_LIMIT),
        name="ffn",
    )(*sched, xs, w_gate, w_up, w_down)


def _sc_gather(y, dest3):
    a, words = y.shape
    nchunks, _, chunk = dest3.shape
    n = nchunks * chunk
    per_worker = nchunks // (SC_CORES * SC_SUBCORES)
    mesh = plsc.VectorSubcoreMesh(core_axis_name="c", subcore_axis_name="s",
                                  num_cores=SC_CORES, num_subcores=SC_SUBCORES)

    @functools.partial(
        pl.kernel, mesh=mesh,
        out_type=jax.ShapeDtypeStruct((TOP_K, n, words), I32),
        scratch_types=[
            pltpu.VMEM((TOP_K, chunk), I32),
            pltpu.VMEM((chunk, words), I32),
            pltpu.VMEM((chunk, words), I32),
            pltpu.SemaphoreType.DMA,
            pltpu.SemaphoreType.DMA((2,)),
        ],
        name="sc_gather",
    )
    def run(y_hbm, d_hbm, yt_hbm, idx_v, buf0, buf1, sem_g, sem_w):
        wid = lax.axis_index("s") * SC_CORES + lax.axis_index("c")
        bufs = (buf0, buf1)

        @pl.loop(0, per_worker)
        def _(j):
            ch = wid * per_worker + j
            pltpu.sync_copy(d_hbm.at[ch], idx_v)
            rows = pl.ds(ch * chunk, chunk)
            gather = pltpu.async_copy(y_hbm.at[idx_v.at[0]], bufs[0], sem_g)
            writes = []
            for k in range(TOP_K):
                gather.wait()
                writes.append(pltpu.async_copy(bufs[k % 2], yt_hbm.at[k, rows], sem_w.at[k % 2]))
                if k + 1 < TOP_K:
                    if k >= 1:
                        writes[k - 1].wait()
                    gather = pltpu.async_copy(y_hbm.at[idx_v.at[k + 1]], bufs[(k + 1) % 2], sem_g)
            writes[TOP_K - 2].wait()
            writes[TOP_K - 1].wait()

    return run(y, dest3)


def _combine_kernel(yt_ref, wt_ref, xb_ref, mod_ref, o_ref):
    half = yt_ref.shape[2]
    wt = wt_ref[...]
    lo, hi = _unpack_halves(yt_ref[0])
    r_lo = lo * wt[:, 0:1]
    r_hi = hi * wt[:, 0:1]
    for k in range(1, TOP_K):
        lo, hi = _unpack_halves(yt_ref[k])
        r_lo = r_lo + lo * wt[:, k:k + 1]
        r_hi = r_hi + hi * wt[:, k:k + 1]
    gate = mod_ref[0][5:6]
    o_ref[:, :half] = xb_ref[:, :half] + gate[:, :half] * r_lo
    o_ref[:, half:] = xb_ref[:, half:] + gate[:, half:] * r_hi


def _combine(yt, w_t, xb, mod3, seq):
    n, d = xb.shape
    tm = TM_COMBINE
    tiles_per_seq = seq // tm
    return pl.pallas_call(
        _combine_kernel,
        out_shape=jax.ShapeDtypeStruct((n, d), F32),
        grid=(n // tm,),
        in_specs=[
            pl.BlockSpec((TOP_K, tm, d // 2), lambda i: (0, i, 0)),
            pl.BlockSpec((tm, TOP_K), lambda i: (i, 0)),
            pl.BlockSpec((tm, d), lambda i: (i, 0)),
            pl.BlockSpec((1, N_MOD, d), lambda i: (i // tiles_per_seq, 0, 0)),
        ],
        out_specs=pl.BlockSpec((tm, d), lambda i: (i, 0)),
        compiler_params=pltpu.CompilerParams(vmem_limit_bytes=VMEM_LIMIT),
        name="combine",
    )(yt, w_t, xb, mod3)


def _rotary_tables(seq):
    half = RET_DK // 2
    pos = jnp.arange(seq, dtype=F32)
    inv = ROPE_BASE ** (-jnp.arange(half, dtype=F32) / half)
    ang = pos[:, None] * inv[None, :]
    cos = jnp.cos(ang)
    sin = jnp.sin(ang)
    return jnp.concatenate([cos, cos], axis=-1), jnp.concatenate([-sin, sin], axis=-1)


def _ffn_schedule(counts, nblk):
    padded = (counts + FFN_BLOCK - 1) // FFN_BLOCK * FFN_BLOCK
    ends = jnp.cumsum(padded)
    start = ends - padded
    nreal = ends[-1:] // FFN_BLOCK
    blk = jnp.minimum(jnp.arange(nblk, dtype=I32), nreal - 1)
    row0 = blk * FFN_BLOCK
    exp = jnp.sum((ends[None, :] <= row0[:, None]).astype(I32), axis=1)
    eids = jnp.arange(N_EXPERTS, dtype=I32)
    mine = exp[:, None] == eids[None, :]

    def per_block(table):
        return jnp.sum(jnp.where(mine, table[None, :], 0), axis=1)

    valid = jnp.clip(per_block(counts + start) - row0, 0, FFN_BLOCK)
    steps = jnp.arange(nblk, dtype=I32)
    first = ((steps < nreal) & ((steps == 0) | (exp != jnp.roll(exp, 1)))).astype(I32)
    used = counts > 0
    ordinal = jnp.cumsum(used.astype(I32)) - 1

    def used_at(pos):
        hit = used[None, :] & (ordinal[None, :] == pos[:, None])
        return jnp.sum(jnp.where(hit, eids[None, :] + 1, 0), axis=1) - 1

    ahead = used_at(ordinal + (FFN_WEIGHT_SLOTS - 1))
    head = jnp.concatenate([nreal, used_at(jnp.arange(FFN_WEIGHT_SLOTS - 1, dtype=I32))])
    return start, (head, blk, exp, valid, first, per_block(ahead), per_block(ordinal % FFN_WEIGHT_SLOTS))


def _chunked(dest, chunk):
    k, n = dest.shape
    return dest.reshape(k, n // chunk, chunk).transpose(1, 0, 2)


def kernel(x, c, w_ada, b_ada, g_mix, w_in, q_gain, k_gain, w_out, g_ffn, w_router, router_bias,
           w_gate, w_up, w_down, ws_gate, ws_up, ws_down):
    bsz, seq, d = x.shape
    n = bsz * seq
    depth = w_ada.shape[0]
    cos_full, sin_signed = _rotary_tables(seq)
    log_g = jnp.log1p(-jnp.exp2(-5.0 - jnp.arange(RET_HEADS, dtype=F32)))
    ret_w = RET_HEADS * RET_DK
    x2 = x.reshape(n, d)
    for l in range(depth):
        mod3 = _adaln(c, w_ada[l], b_ada[l]).reshape(bsz, N_MOD, d)
        proj = _inproj(x2, mod3, g_mix[l].reshape(1, d), w_in[l].astype(BF16), cos_full, sin_signed, seq)
        proj3 = proj.reshape(bsz, seq, IN_COLS)
        ret = _retention(log_g, proj3)
        qg2 = jnp.tile(q_gain[l].reshape(1, MOBA_DH), (1, 2))
        kg2 = jnp.tile(k_gain[l].reshape(1, MOBA_DH), (1, 2))
        moba = _moba(proj3, qg2, kg2)
        wo = w_out[l].astype(BF16)
        xb, h2, e_idx, w_k, rank, cnt = _mid(
            ret.reshape(n, ret_w), moba.reshape(n, MOBA_HEADS * MOBA_DH), x2, mod3,
            wo[:ret_w], wo[ret_w:], g_ffn[l].reshape(1, d),
            w_router[l].T.astype(BF16), router_bias[l].reshape(N_EXPERTS, 1),
            ws_gate[l].astype(BF16), ws_up[l].astype(BF16), ws_down[l].astype(BF16), seq)
        nblk = n * TOP_K // FFN_BLOCK + N_EXPERTS
        start, sched = _ffn_schedule(cnt[:, 0], nblk)
        dest = _dest(start, e_idx, rank)
        xs = _sc_dispatch(h2, _chunked(dest, SC_CHUNK), nblk * FFN_BLOCK)
        y = _ffn(sched, xs, w_gate[l], w_up[l], w_down[l])
        yt = _sc_gather(y, _chunked(dest, SC_GATHER_CHUNK))
        x2 = _combine(yt, w_k.T, xb, mod3, seq)
    return x2.reshape(bsz, seq, d)
```

```python
import functools

import numpy as np
import jax
import jax.numpy as jnp
from jax import lax
from jax.experimental import pallas as pl
from jax.experimental.pallas import tpu as pltpu
from jax.experimental.pallas import tpu_sc as plsc

F32 = jnp.float32
BF16 = jnp.bfloat16
I32 = jnp.int32

D_MODEL = 1024
RET_HEADS = 4
RET_DK = 128
MOBA_HEADS = 8
MOBA_DH = 64
MOBA_BLOCK = 256
MOBA_TOPK = 3
ROPE_BASE = 10000.0
N_EXPERTS = 256
TOP_K = 8
N_GROUPS = 8
TOPK_GROUPS = 4
GROUP_SIZE = N_EXPERTS // N_GROUPS
EXPERT_FF = 256
ROUTED_SCALE = 2.5
N_MOD = 6
EPS = 1e-6
IN_COLS = 3584

LANES = 128
RET_CHUNK = 256
TM_PROJ = 512
TM_COMBINE = 256
SC_CORES = 2
SC_SUBCORES = 16
SC_CHUNK = 128
SC_GATHER_CHUNK = 64
FFN_BLOCK = 256
FFN_LOOKAHEAD = 5
FFN_WEIGHT_SLOTS = 3
FFN_OUT_SLOTS = 3
VMEM_LIMIT = 56 * 1024 * 1024

NEG_INF = float("-inf")
LOG2_E = 1.4426950408889634


def _silu(x):
    return x * jax.nn.sigmoid(x)


def _nt_dot(a, b):
    return lax.dot_general(a, b, (((1,), (1,)), ((), ())), preferred_element_type=F32)


def _tn_dot(a, b):
    return lax.dot_general(a, b, (((0,), (0,)), ((), ())), preferred_element_type=F32)


def _dot(a, b):
    return jnp.dot(a, b, preferred_element_type=F32)


HI_MASK = -65536


def _pack_halves(v):
    w = v.shape[1] // 2
    lo = lax.bitcast_convert_type(v[:, :w].astype(BF16).astype(F32), I32)
    hi = lax.bitcast_convert_type(v[:, w:].astype(BF16).astype(F32), I32)
    return lax.shift_right_logical(lo, 16) | (hi & HI_MASK)


def _unpack_halves(u):
    lo = lax.bitcast_convert_type(lax.shift_left(u, 16), F32)
    hi = lax.bitcast_convert_type(u & HI_MASK, F32)
    return lo, hi


def _adaln_kernel(c_ref, w_ref, b_ref, o_ref):
    s = _silu(c_ref[...])
    o_ref[...] = _dot(s.astype(BF16), w_ref[...].astype(BF16)) + b_ref[...]


def _adaln(c, w_ada, b_ada):
    bsz, d = c.shape
    ncol = w_ada.shape[1]
    tn = 1024
    return pl.pallas_call(
        _adaln_kernel,
        out_shape=jax.ShapeDtypeStruct((bsz, ncol), F32),
        grid=(ncol // tn,),
        in_specs=[
            pl.BlockSpec((bsz, d), lambda j: (0, 0)),
            pl.BlockSpec((d, tn), lambda j: (0, j)),
            pl.BlockSpec((1, tn), lambda j: (0, j)),
        ],
        out_specs=pl.BlockSpec((bsz, tn), lambda j: (0, j)),
        compiler_params=pltpu.CompilerParams(vmem_limit_bytes=VMEM_LIMIT),
        name="adaln",
    )(c, w_ada, b_ada.reshape(1, ncol))


def _inproj_kernel(x_ref, mod_ref, g_ref, w_ref, cos_ref, sin_ref, o_ref):
    x = x_ref[...]
    ms = jnp.mean(x * x, axis=-1, keepdims=True)
    m = mod_ref[0]
    h = (x * lax.rsqrt(ms + EPS) * g_ref[...]) * (1.0 + m[1:2]) + m[0:1]
    hb = h.astype(BF16)
    cosf = cos_ref[...]
    sinf = sin_ref[...]
    k_scale = RET_DK ** -0.5
    width = RET_HEADS * RET_DK
    for ci in range(IN_COLS // width):
        acc = _dot(hb, w_ref[:, ci * width:(ci + 1) * width])
        if ci < 2:
            for hh in range(RET_HEADS):
                xh = acc[:, hh * RET_DK:(hh + 1) * RET_DK]
                r = xh * cosf + pltpu.roll(xh, RET_DK // 2, axis=1) * sinf
                if ci == 1:
                    r = r * k_scale
                o_ref[:, ci * width + hh * RET_DK:ci * width + (hh + 1) * RET_DK] = r.astype(BF16)
        else:
            o_ref[:, ci * width:(ci + 1) * width] = acc.astype(BF16)


def _inproj(x2, mod3, g_mix, w_in_bf, cos_full, sin_signed, seq):
    n, d = x2.shape
    tm = TM_PROJ
    tiles_per_seq = seq // tm
    return pl.pallas_call(
        _inproj_kernel,
        out_shape=jax.ShapeDtypeStruct((n, IN_COLS), BF16),
        grid=(n // tm,),
        in_specs=[
            pl.BlockSpec((tm, d), lambda i: (i, 0)),
            pl.BlockSpec((1, N_MOD, d), lambda i: (i // tiles_per_seq, 0, 0)),
            pl.BlockSpec((1, d), lambda i: (0, 0)),
            pl.BlockSpec((d, IN_COLS), lambda i: (0, 0)),
            pl.BlockSpec((tm, LANES), lambda i: (i % tiles_per_seq, 0)),
            pl.BlockSpec((tm, LANES), lambda i: (i % tiles_per_seq, 0)),
        ],
        out_specs=pl.BlockSpec((tm, IN_COLS), lambda i: (i, 0)),
        compiler_params=pltpu.CompilerParams(vmem_limit_bytes=VMEM_LIMIT),
        name="inproj",
    )(x2, mod3, g_mix, w_in_bf, cos_full, sin_signed)


def _ret_kernel(lg_ref, q_ref, k_ref, v_ref, g_ref, o_ref):
    seq = q_ref.shape[1]
    c = RET_CHUNK
    lg = lg_ref[pl.program_id(1)]
    row = lax.broadcasted_iota(I32, (c, c), 0)
    col = lax.broadcasted_iota(I32, (c, c), 1)
    diff = (row - col).astype(F32)
    dmask = jnp.where(diff >= 0, jnp.exp(lg * jnp.maximum(diff, 0.0)), 0.0)
    idx = lax.broadcasted_iota(I32, (c, 1), 0).astype(F32)
    q_decay = jnp.exp(lg * (idx + 1.0))
    k_decay = jnp.exp(lg * (c - 1.0 - idx))
    chunk_decay = jnp.exp(jnp.full((1, 1), lg * c, F32))
    state = jnp.zeros((RET_DK, RET_DK), F32)
    for n in range(seq // c):
        rows = slice(n * c, (n + 1) * c)
        qn = q_ref[0, rows, :]
        kn = k_ref[0, rows, :]
        vn = v_ref[0, rows, :]
        scores = _nt_dot(qn, kn) * dmask
        inner = _dot(scores.astype(BF16), vn)
        qs = (qn.astype(F32) * q_decay).astype(BF16)
        cross = _dot(qs, state.astype(BF16))
        o = inner + cross
        o = o * lax.rsqrt(jnp.mean(o * o, axis=-1, keepdims=True) + EPS)
        gn = g_ref[0, rows, :].astype(F32)
        o_ref[0, rows, :] = (_silu(gn) * o).astype(BF16)
        ks = (kn.astype(F32) * k_decay).astype(BF16)
        state = state * chunk_decay + _tn_dot(ks, vn)


def _retention(log_g, proj3):
    bsz, seq, _ = proj3.shape
    blk = (1, seq, RET_DK)
    return pl.pallas_call(
        _ret_kernel,
        out_shape=jax.ShapeDtypeStruct((bsz, seq, RET_HEADS * RET_DK), BF16),
        grid_spec=pltpu.PrefetchScalarGridSpec(
            num_scalar_prefetch=1,
            grid=(bsz, RET_HEADS),
            in_specs=[
                pl.BlockSpec(blk, lambda b, h, lg: (b, 0, h)),
                pl.BlockSpec(blk, lambda b, h, lg: (b, 0, RET_HEADS + h)),
                pl.BlockSpec(blk, lambda b, h, lg: (b, 0, 2 * RET_HEADS + h)),
                pl.BlockSpec(blk, lambda b, h, lg: (b, 0, 3 * RET_HEADS + h)),
            ],
            out_specs=pl.BlockSpec(blk, lambda b, h, lg: (b, 0, h)),
        ),
        compiler_params=pltpu.CompilerParams(vmem_limit_bytes=VMEM_LIMIT),
        name="retention",
    )(log_g, proj3, proj3, proj3, proj3)


def _moba_kernel(q_ref, k_ref, v_ref, qg_ref, kg_ref, o_ref, qt_s, ka_s, kb_s, vta_s, vtb_s):
    seq = q_ref.shape[1]
    lb = MOBA_BLOCK
    nb = seq // lb
    lane = lax.broadcasted_iota(I32, (1, LANES), 1)
    is_a = lane < MOBA_DH
    sub = lax.broadcasted_iota(I32, (LANES, 1), 0)
    top = sub < MOBA_DH

    def head_norm(xf, gain):
        sq = xf * xf
        s_a = jnp.sum(jnp.where(is_a, sq, 0.0), axis=-1, keepdims=True)
        s_b = jnp.sum(jnp.where(is_a, 0.0, sq), axis=-1, keepdims=True)
        inv = jnp.where(is_a, lax.rsqrt(s_a / MOBA_DH + EPS), lax.rsqrt(s_b / MOBA_DH + EPS))
        return xf * inv * gain

    qg = qg_ref[...]
    kg = kg_ref[...]
    k_means = []
    for j in range(nb):
        rows = slice(j * lb, (j + 1) * lb)
        kf = head_norm(k_ref[0, rows, :].astype(F32), kg)
        ka_s[rows, :] = jnp.where(is_a, kf, 0.0).astype(BF16)
        kb_s[rows, :] = jnp.where(is_a, 0.0, kf).astype(BF16)
        k_means.append(jnp.mean(kf, axis=0, keepdims=True))
        qf = head_norm(q_ref[0, rows, :].astype(F32), qg)
        qt_s[:, rows] = (qf * (MOBA_DH ** -0.5 * LOG2_E)).T.astype(BF16)
        vt = v_ref[0, rows, :].astype(F32).T
        vta_s[:, rows] = jnp.where(top, vt, 1.0).astype(BF16)
        vtb_s[:, rows] = jnp.where(top, 1.0, vt).astype(BF16)
    k_mean = jnp.concatenate(k_means + [jnp.zeros((16 - nb, LANES), F32)], axis=0)
    k_mean_h = (jnp.where(is_a, k_mean, 0.0).astype(BF16), jnp.where(is_a, 0.0, k_mean).astype(BF16))
    k_s = (ka_s, kb_s)
    vt_s = (vta_s, vtb_s)

    r_loc = lax.broadcasted_iota(I32, (lb, lb), 0)
    c_loc = lax.broadcasted_iota(I32, (lb, lb), 1)
    causal = r_loc <= c_loc

    for i in range(nb):
        cols = slice(i * lb, (i + 1) * lb)
        qt = qt_s[:, cols]
        outs = []
        for hx in range(2):
            bias = [None] * i
            if i > MOBA_TOPK:
                gate = _dot(k_mean_h[hx], qt)
                g = [gate[j:j + 1, :] for j in range(i)]
                for j in range(i):
                    rank = jnp.zeros((1, lb), F32)
                    for j2 in range(i):
                        if j2 == j:
                            continue
                        beats = (g[j2] >= g[j]) if j2 < j else (g[j2] > g[j])
                        rank = rank + jnp.where(beats, 1.0, 0.0)
                    bias[j] = jnp.where(rank < float(MOBA_TOPK), 0.0, NEG_INF)
            pieces = []
            for j in range(i + 1):
                s = _dot(k_s[hx][j * lb:(j + 1) * lb, :], qt)
                if j == i:
                    s = jnp.where(causal, s, NEG_INF)
                elif bias[j] is not None:
                    s = s + bias[j]
                pieces.append(s)
            mx = jnp.max(pieces[0], axis=0, keepdims=True)
            for s in pieces[1:]:
                mx = jnp.maximum(mx, jnp.max(s, axis=0, keepdims=True))
            acc = jnp.zeros((LANES, lb), F32)
            for j, s in enumerate(pieces):
                p = jnp.exp2(s - mx).astype(BF16)
                acc = acc + _dot(vt_s[hx][:, j * lb:(j + 1) * lb], p)
            den = acc[MOBA_DH:MOBA_DH + 1, :] if hx == 0 else acc[0:1, :]
            outs.append(acc / den)
        o_ref[0, cols, :] = jnp.where(top, outs[0], outs[1]).T.astype(BF16)


def _moba(proj3, qg2, kg2):
    bsz, seq, _ = proj3.shape
    pairs = MOBA_HEADS // 2
    blk = (1, seq, LANES)
    base = 4 * RET_HEADS
    return pl.pallas_call(
        _moba_kernel,
        out_shape=jax.ShapeDtypeStruct((bsz, seq, MOBA_HEADS * MOBA_DH), BF16),
        grid=(bsz, pairs),
        in_specs=[
            pl.BlockSpec(blk, lambda b, p: (b, 0, base + p)),
            pl.BlockSpec(blk, lambda b, p: (b, 0, base + pairs + p)),
            pl.BlockSpec(blk, lambda b, p: (b, 0, base + 2 * pairs + p)),
            pl.BlockSpec((1, LANES), lambda b, p: (0, 0)),
            pl.BlockSpec((1, LANES), lambda b, p: (0, 0)),
        ],
        out_specs=pl.BlockSpec(blk, lambda b, p: (b, 0, p)),
        scratch_shapes=[
            pltpu.VMEM((LANES, seq), BF16),
            pltpu.VMEM((seq, LANES), BF16),
            pltpu.VMEM((seq, LANES), BF16),
            pltpu.VMEM((LANES, seq), BF16),
            pltpu.VMEM((LANES, seq), BF16),
        ],
        compiler_params=pltpu.CompilerParams(vmem_limit_bytes=VMEM_LIMIT),
        name="moba",
    )(proj3, proj3, proj3, qg2, kg2)


def _mid_kernel(ret_ref, moba_ref, x_ref, mod_ref, wo1_ref, wo2_ref, g_ref, wr_ref, rb_ref,
                wsg_ref, wsu_ref, wsd_ref,
                xb_ref, h2_ref, e_ref, w_ref, rk_ref, cnt_ref, carry_s):
    i = pl.program_id(0)
    tm = x_ref.shape[0]

    @pl.when(i == 0)
    def _():
        carry_s[...] = jnp.zeros_like(carry_s)

    m = mod_ref[0]
    mixed = _dot(ret_ref[...], wo1_ref[...]) + _dot(moba_ref[...], wo2_ref[...])
    x1 = x_ref[...] + m[2:3] * mixed
    ms = jnp.mean(x1 * x1, axis=-1, keepdims=True)
    h2 = (x1 * lax.rsqrt(ms + EPS) * g_ref[...]) * (1.0 + m[4:5]) + m[3:4]
    h2_ref[...] = _pack_halves(h2)
    h2b = h2.astype(BF16)

    hid = _silu(_dot(h2b, wsg_ref[...])) * _dot(h2b, wsu_ref[...])
    xb_ref[...] = x1 + m[5:6] * _dot(hid.astype(BF16), wsd_ref[...])

    scores = jax.nn.sigmoid(_nt_dot(wr_ref[...], h2b))
    biased = scores + rb_ref[...]
    grp = biased.reshape(N_GROUPS, GROUP_SIZE, tm)
    gi = lax.broadcasted_iota(I32, (N_GROUPS, GROUP_SIZE, tm), 1).astype(F32)
    top1 = jnp.max(grp, axis=1, keepdims=True)
    first = jnp.min(jnp.where(grp == top1, gi, float(GROUP_SIZE)), axis=1, keepdims=True)
    top2 = jnp.max(jnp.where(gi == first, NEG_INF, grp), axis=1, keepdims=True)
    gscore = (top1 + top2).reshape(N_GROUPS, tm)
    gidx = lax.broadcasted_iota(I32, (N_GROUPS, tm), 0)
    grank = jnp.zeros((N_GROUPS, tm), F32)
    for g2 in range(N_GROUPS):
        rowv = gscore[g2:g2 + 1, :]
        beats = (rowv > gscore) | ((rowv == gscore) & (g2 < gidx))
        grank = grank + jnp.where(beats, 1.0, 0.0)
    gsel = jnp.where(grank < float(TOPK_GROUPS), 1.0, 0.0)
    emask = jnp.broadcast_to(gsel.reshape(N_GROUPS, 1, tm), (N_GROUPS, GROUP_SIZE, tm)).reshape(N_EXPERTS, tm)
    choice = jnp.where(emask > 0.5, biased, NEG_INF)

    eidx = lax.broadcasted_iota(I32, (N_EXPERTS, tm), 0).astype(F32)
    remaining = choice
    e_rows = []
    for _k in range(TOP_K):
        mx = jnp.max(remaining, axis=0, keepdims=True)
        idx = jnp.min(jnp.where(remaining == mx, eidx, float(N_EXPERTS)), axis=0, keepdims=True)
        e_rows.append(idx)
        remaining = jnp.where(eidx == idx, NEG_INF, remaining)
    selmask = jnp.where(remaining != choice, 1.0, 0.0)

    tr = lax.broadcasted_iota(I32, (tm, tm), 0)
    tc = lax.broadcasted_iota(I32, (tm, tm), 1)
    upper = jnp.where(tr < tc, 1.0, 0.0).astype(BF16)
    prefix = _dot(selmask.astype(BF16), upper) + carry_s[...]
    w_rows = []
    r_rows = []
    for ek in e_rows:
        onehot = eidx == ek
        w_rows.append(jnp.sum(jnp.where(onehot, scores, 0.0), axis=0, keepdims=True))
        r_rows.append(jnp.sum(jnp.where(onehot, prefix, 0.0), axis=0, keepdims=True))
    wsum = w_rows[0]
    for wk in w_rows[1:]:
        wsum = wsum + wk
    carry_s[...] = carry_s[...] + jnp.sum(selmask, axis=1, keepdims=True)

    e_ref[...] = jnp.concatenate(e_rows, axis=0).astype(I32)
    w_ref[...] = jnp.concatenate([wk / wsum * ROUTED_SCALE for wk in w_rows], axis=0)
    rk_ref[...] = jnp.concatenate(r_rows, axis=0).astype(I32)
    cnt_ref[...] = carry_s[...].astype(I32)


def _mid(ret2, moba2, x2, mod3, wo1, wo2, g_ffn, wr_t, rbias, wsg, wsu, wsd, seq):
    n, d = x2.shape
    tm = TM_PROJ
    tiles_per_seq = seq // tm
    half = ret2.shape[1]
    ff = wsg.shape[1]
    const = lambda i: (0, 0)
    row = lambda i: (i, 0)
    colt = lambda i: (0, i)
    return pl.pallas_call(
        _mid_kernel,
        out_shape=(
            jax.ShapeDtypeStruct((n, d), F32),
            jax.ShapeDtypeStruct((n, d // 2), I32),
            jax.ShapeDtypeStruct((TOP_K, n), I32),
            jax.ShapeDtypeStruct((TOP_K, n), F32),
            jax.ShapeDtypeStruct((TOP_K, n), I32),
            jax.ShapeDtypeStruct((N_EXPERTS, 1), I32),
        ),
        grid=(n // tm,),
        in_specs=[
            pl.BlockSpec((tm, half), row),
            pl.BlockSpec((tm, half), row),
            pl.BlockSpec((tm, d), row),
            pl.BlockSpec((1, N_MOD, d), lambda i: (i // tiles_per_seq, 0, 0)),
            pl.BlockSpec((half, d), const),
            pl.BlockSpec((half, d), const),
            pl.BlockSpec((1, d), const),
            pl.BlockSpec((N_EXPERTS, d), const),
            pl.BlockSpec((N_EXPERTS, 1), const),
            pl.BlockSpec((d, ff), const),
            pl.BlockSpec((d, ff), const),
            pl.BlockSpec((ff, d), const),
        ],
        out_specs=(
            pl.BlockSpec((tm, d), row),
            pl.BlockSpec((tm, d // 2), row),
            pl.BlockSpec((TOP_K, tm), colt),
            pl.BlockSpec((TOP_K, tm), colt),
            pl.BlockSpec((TOP_K, tm), colt),
            pl.BlockSpec((N_EXPERTS, 1), const),
        ),
        scratch_shapes=[pltpu.VMEM((N_EXPERTS, 1), F32)],
        compiler_params=pltpu.CompilerParams(
            dimension_semantics=("arbitrary",), vmem_limit_bytes=VMEM_LIMIT),
        name="mid",
    )(ret2, moba2, x2, mod3, wo1, wo2, g_ffn, wr_t, rbias, wsg, wsu, wsd)


def _dest_kernel(start_ref, e_ref, rk_ref, o_ref):
    e = e_ref[...]

    def body(ex, acc):
        return acc + jnp.where(e == ex, start_ref[ex], 0)

    o_ref[...] = lax.fori_loop(0, N_EXPERTS, body, rk_ref[...], unroll=8)


def _dest(start, e_idx, rank):
    k, n = e_idx.shape
    tn = 2048
    return pl.pallas_call(
        _dest_kernel,
        out_shape=jax.ShapeDtypeStruct((k, n), I32),
        grid_spec=pltpu.PrefetchScalarGridSpec(
            num_scalar_prefetch=1,
            grid=(n // tn,),
            in_specs=[
                pl.BlockSpec((k, tn), lambda i, s: (0, i)),
                pl.BlockSpec((k, tn), lambda i, s: (0, i)),
            ],
            out_specs=pl.BlockSpec((k, tn), lambda i, s: (0, i)),
        ),
        compiler_params=pltpu.CompilerParams(vmem_limit_bytes=VMEM_LIMIT),
        name="dest",
    )(start, e_idx, rank)


def _sc_dispatch(h2p, dest3, total_rows):
    n, words = h2p.shape
    nchunks = n // SC_CHUNK
    per_worker = nchunks // (SC_CORES * SC_SUBCORES)
    mesh = plsc.VectorSubcoreMesh(core_axis_name="c", subcore_axis_name="s",
                                  num_cores=SC_CORES, num_subcores=SC_SUBCORES)

    @functools.partial(
        pl.kernel, mesh=mesh,
        out_type=jax.ShapeDtypeStruct((total_rows, words), I32),
        scratch_types=[
            pltpu.VMEM((TOP_K, SC_CHUNK), I32),
            pltpu.VMEM((SC_CHUNK, words), I32),
            pltpu.SemaphoreType.DMA,
        ],
        name="sc_dispatch",
    )
    def run(h_hbm, d_hbm, xs_hbm, idx_v, rows_v, sem):
        wid = lax.axis_index("s") * SC_CORES + lax.axis_index("c")

        @pl.loop(0, per_worker)
        def _(j):
            ch = wid * per_worker + j
            pltpu.sync_copy(d_hbm.at[ch], idx_v)
            pltpu.sync_copy(h_hbm.at[pl.ds(ch * SC_CHUNK, SC_CHUNK)], rows_v)
            copies = [pltpu.async_copy(rows_v, xs_hbm.at[idx_v.at[k]], sem) for k in range(TOP_K)]
            for cp in copies:
                cp.wait()

    return run(h2p, dest3)


def _ffn_kernel(head_ref, exp_ref, valid_ref, first_ref, ahead_ref, slot_ref,
                x_hbm, wg_hbm, wu_hbm, wd_hbm, y_hbm, x_s, y_s, wg_s, wu_s, wd_s, sem_x, sem_y, sem):
    i = pl.program_id(0)
    nreal = head_ref[0]
    rows_per = x_s.shape[1]

    def row_copy(g):
        slot = lax.rem(g, FFN_LOOKAHEAD + 1)
        return pltpu.make_async_copy(x_hbm.at[pl.ds(g * rows_per, rows_per)], x_s.at[slot], sem_x.at[slot])

    def out_copy(g):
        slot = lax.rem(g, FFN_OUT_SLOTS)
        return pltpu.make_async_copy(y_s.at[slot], y_hbm.at[pl.ds(g * rows_per, rows_per)], sem_y.at[slot])

    def weight_copies(e, s):
        return (pltpu.make_async_copy(wg_hbm.at[e], wg_s.at[s], sem.at[s, 0]),
                pltpu.make_async_copy(wu_hbm.at[e], wu_s.at[s], sem.at[s, 1]),
                pltpu.make_async_copy(wd_hbm.at[e], wd_s.at[s], sem.at[s, 2]))

    @pl.when(i == 0)
    def _():
        for j in range(FFN_WEIGHT_SLOTS - 1):
            @pl.when(head_ref[1 + j] >= 0)
            def _():
                for cp in weight_copies(head_ref[1 + j], j):
                    cp.start()
        for g in range(FFN_LOOKAHEAD):
            @pl.when(g < nreal)
            def _():
                row_copy(g).start()

    @pl.when(i < nreal)
    def _():
        s = slot_ref[i]

        @pl.when(i + FFN_LOOKAHEAD < nreal)
        def _():
            row_copy(i + FFN_LOOKAHEAD).start()

        row_copy(i).wait()

        @pl.when(first_ref[i] == 1)
        def _():
            for cp in weight_copies(exp_ref[i], s):
                cp.wait()

            @pl.when(ahead_ref[i] >= 0)
            def _():
                for cp in weight_copies(ahead_ref[i], lax.rem(s + FFN_WEIGHT_SLOTS - 1, FFN_WEIGHT_SLOTS)):
                    cp.start()

        @pl.when(i >= FFN_OUT_SLOTS)
        def _():
            out_copy(i - FFN_OUT_SLOTS).wait()

        half = x_s.shape[2]
        r = lax.broadcasted_iota(I32, (rows_per, 1), 0)
        x_blk = x_s[lax.rem(i, FFN_LOOKAHEAD + 1)]
        x_lo, x_hi = _unpack_halves(jnp.where(r < valid_ref[i], x_blk, 0))
        hg = _dot(x_lo, wg_s[s, :half, :]) + _dot(x_hi, wg_s[s, half:, :])
        hu = _dot(x_lo, wu_s[s, :half, :]) + _dot(x_hi, wu_s[s, half:, :])
        y_s[lax.rem(i, FFN_OUT_SLOTS)] = _pack_halves(_dot(_silu(hg) * hu, wd_s[s]))
        out_copy(i).start()

        @pl.when(i == nreal - 1)
        def _():
            for back in range(FFN_OUT_SLOTS):
                @pl.when(i - back >= 0)
                def _():
                    out_copy(i - back).wait()


def _ffn(sched, xs, w_gate, w_up, w_down):
    p, half = xs.shape
    d = 2 * half
    ff = w_gate.shape[2]
    steps = sched[1].shape[0]
    return pl.pallas_call(
        _ffn_kernel,
        out_shape=jax.ShapeDtypeStruct((p, half), I32),
        grid_spec=pltpu.PrefetchScalarGridSpec(
            num_scalar_prefetch=len(sched),
            grid=(steps,),
            in_specs=[
                pl.BlockSpec(memory_space=pl.ANY),
                pl.BlockSpec(memory_space=pl.ANY),
                pl.BlockSpec(memory_space=pl.ANY),
                pl.BlockSpec(memory_space=pl.ANY),
            ],
            out_specs=pl.BlockSpec(memory_space=pl.ANY),
            scratch_shapes=[
                pltpu.VMEM((FFN_LOOKAHEAD + 1, FFN_BLOCK, half), I32),
                pltpu.VMEM((FFN_OUT_SLOTS, FFN_BLOCK, half), I32),
                pltpu.VMEM((FFN_WEIGHT_SLOTS, d, ff), F32),
                pltpu.VMEM((FFN_WEIGHT_SLOTS, d, ff), F32),
                pltpu.VMEM((FFN_WEIGHT_SLOTS, ff, d), F32),
                pltpu.SemaphoreType.DMA((FFN_LOOKAHEAD + 1,)),
                pltpu.SemaphoreType.DMA((FFN_OUT_SLOTS,)),
                pltpu.SemaphoreType.DMA((FFN_WEIGHT_SLOTS, 3)),
            ],
        ),
        compiler_params=pltpu.CompilerParams(
            dimension_semantics=("arbitrary",), vmem_limit_bytes=VMEM_LIMIT, has_side_effects=True),
        name="ffn",
    )(*sched, xs, w_gate, w_up, w_down)


def _sc_gather(y, dest3):
    a, words = y.shape
    nchunks, _, chunk = dest3.shape
    n = nchunks * chunk
    per_worker = nchunks // (SC_CORES * SC_SUBCORES)
    mesh = plsc.VectorSubcoreMesh(core_axis_name="c", subcore_axis_name="s",
                                  num_cores=SC_CORES, num_subcores=SC_SUBCORES)

    @functools.partial(
        pl.kernel, mesh=mesh,
        out_type=jax.ShapeDtypeStruct((TOP_K, n, words), I32),
        scratch_types=[
            pltpu.VMEM((TOP_K, chunk), I32),
            pltpu.VMEM((chunk, words), I32),
            pltpu.VMEM((chunk, words), I32),
            pltpu.SemaphoreType.DMA,
            pltpu.SemaphoreType.DMA((2,)),
        ],
        name="sc_gather",
    )
    def run(y_hbm, d_hbm, yt_hbm, idx_v, buf0, buf1, sem_g, sem_w):
        wid = lax.axis_index("s") * SC_CORES + lax.axis_index("c")
        bufs = (buf0, buf1)

        @pl.loop(0, per_worker)
        def _(j):
            ch = wid * per_worker + j
            pltpu.sync_copy(d_hbm.at[ch], idx_v)
            rows = pl.ds(ch * chunk, chunk)
            gather = pltpu.async_copy(y_hbm.at[idx_v.at[0]], bufs[0], sem_g)
            writes = []
            for k in range(TOP_K):
                gather.wait()
                writes.append(pltpu.async_copy(bufs[k % 2], yt_hbm.at[k, rows], sem_w.at[k % 2]))
                if k + 1 < TOP_K:
                    if k >= 1:
                        writes[k - 1].wait()
                    gather = pltpu.async_copy(y_hbm.at[idx_v.at[k + 1]], bufs[(k + 1) % 2], sem_g)
            writes[TOP_K - 2].wait()
            writes[TOP_K - 1].wait()

    return run(y, dest3)


def _combine_kernel(yt_ref, wt_ref, xb_ref, mod_ref, o_ref):
    half = yt_ref.shape[2]
    wt = wt_ref[...]
    lo, hi = _unpack_halves(yt_ref[0])
    r_lo = lo * wt[:, 0:1]
    r_hi = hi * wt[:, 0:1]
    for k in range(1, TOP_K):
        lo, hi = _unpack_halves(yt_ref[k])
        r_lo = r_lo + lo * wt[:, k:k + 1]
        r_hi = r_hi + hi * wt[:, k:k + 1]
    gate = mod_ref[0][5:6]
    o_ref[:, :half] = xb_ref[:, :half] + gate[:, :half] * r_lo
    o_ref[:, half:] = xb_ref[:, half:] + gate[:, half:] * r_hi


def _combine(yt, w_t, xb, mod3, seq):
    n, d = xb.shape
    tm = TM_COMBINE
    tiles_per_seq = seq // tm
    return pl.pallas_call(
        _combine_kernel,
        out_shape=jax.ShapeDtypeStruct((n, d), F32),
        grid=(n // tm,),
        in_specs=[
            pl.BlockSpec((TOP_K, tm, d // 2), lambda i: (0, i, 0)),
            pl.BlockSpec((tm, TOP_K), lambda i: (i, 0)),
            pl.BlockSpec((tm, d), lambda i: (i, 0)),
            pl.BlockSpec((1, N_MOD, d), lambda i: (i // tiles_per_seq, 0, 0)),
        ],
        out_specs=pl.BlockSpec((tm, d), lambda i: (i, 0)),
        compiler_params=pltpu.CompilerParams(vmem_limit_bytes=VMEM_LIMIT),
        name="combine",
    )(yt, w_t, xb, mod3)


def _rotary_tables(seq):
    half = RET_DK // 2
    pos = jnp.arange(seq, dtype=F32)
    inv = ROPE_BASE ** (-jnp.arange(half, dtype=F32) / half)
    ang = pos[:, None] * inv[None, :]
    cos = jnp.cos(ang)
    sin = jnp.sin(ang)
    return jnp.concatenate([cos, cos], axis=-1), jnp.concatenate([-sin, sin], axis=-1)


def _ffn_schedule(counts, nblk):
    padded = (counts + FFN_BLOCK - 1) // FFN_BLOCK * FFN_BLOCK
    ends = jnp.cumsum(padded)
    start = ends - padded
    nreal = ends[-1:] // FFN_BLOCK
    blk = jnp.minimum(jnp.arange(nblk, dtype=I32), nreal - 1)
    row0 = blk * FFN_BLOCK
    exp = jnp.sum((ends[None, :] <= row0[:, None]).astype(I32), axis=1)
    eids = jnp.arange(N_EXPERTS, dtype=I32)
    mine = exp[:, None] == eids[None, :]

    def per_block(table):
        return jnp.sum(jnp.where(mine, table[None, :], 0), axis=1)

    valid = jnp.clip(per_block(counts + start) - row0, 0, FFN_BLOCK)
    steps = jnp.arange(nblk, dtype=I32)
    first = ((steps < nreal) & ((steps == 0) | (exp != jnp.roll(exp, 1)))).astype(I32)
    used = counts > 0
    ordinal = jnp.cumsum(used.astype(I32)) - 1

    def used_at(pos):
        hit = used[None, :] & (ordinal[None, :] == pos[:, None])
        return jnp.sum(jnp.where(hit, eids[None, :] + 1, 0), axis=1) - 1

    ahead = used_at(ordinal + (FFN_WEIGHT_SLOTS - 1))
    head = jnp.concatenate([nreal, used_at(jnp.arange(FFN_WEIGHT_SLOTS - 1, dtype=I32))])
    return start, (head, exp, valid, first, per_block(ahead), per_block(ordinal % FFN_WEIGHT_SLOTS))


def _chunked(dest, chunk):
    k, n = dest.shape
    return dest.reshape(k, n // chunk, chunk).transpose(1, 0, 2)


def kernel(x, c, w_ada, b_ada, g_mix, w_in, q_gain, k_gain, w_out, g_ffn, w_router, router_bias,
           w_gate, w_up, w_down, ws_gate, ws_up, ws_down):
    bsz, seq, d = x.shape
    n = bsz * seq
    depth = w_ada.shape[0]
    cos_full, sin_signed = _rotary_tables(seq)
    log_g = jnp.log1p(-jnp.exp2(-5.0 - jnp.arange(RET_HEADS, dtype=F32)))
    ret_w = RET_HEADS * RET_DK
    x2 = x.reshape(n, d)
    for l in range(depth):
        mod3 = _adaln(c, w_ada[l], b_ada[l]).reshape(bsz, N_MOD, d)
        proj = _inproj(x2, mod3, g_mix[l].reshape(1, d), w_in[l].astype(BF16), cos_full, sin_signed, seq)
        proj3 = proj.reshape(bsz, seq, IN_COLS)
        ret = _retention(log_g, proj3)
        qg2 = jnp.tile(q_gain[l].reshape(1, MOBA_DH), (1, 2))
        kg2 = jnp.tile(k_gain[l].reshape(1, MOBA_DH), (1, 2))
        moba = _moba(proj3, qg2, kg2)
        wo = w_out[l].astype(BF16)
        xb, h2, e_idx, w_k, rank, cnt = _mid(
            ret.reshape(n, ret_w), moba.reshape(n, MOBA_HEADS * MOBA_DH), x2, mod3,
            wo[:ret_w], wo[ret_w:], g_ffn[l].reshape(1, d),
            w_router[l].T.astype(BF16), router_bias[l].reshape(N_EXPERTS, 1),
            ws_gate[l].astype(BF16), ws_up[l].astype(BF16), ws_down[l].astype(BF16), seq)
        nblk = n * TOP_K // FFN_BLOCK + N_EXPERTS
        start, sched = _ffn_schedule(cnt[:, 0], nblk)
        dest = _dest(start, e_idx, rank)
        xs = _sc_dispatch(h2, _chunked(dest, SC_CHUNK), nblk * FFN_BLOCK)
        y = _ffn(sched, xs, w_gate[l], w_up[l], w_down[l])
        yt = _sc_gather(y, _chunked(dest, SC_GATHER_CHUNK))
        x2 = _combine(yt, w_k.T, xb, mod3, seq)
    return x2.reshape(bsz, seq, d)
```

```python
import functools

import numpy as np
import jax
import jax.numpy as jnp
from jax import lax
from jax.experimental import pallas as pl
from jax.experimental.pallas import tpu as pltpu
from jax.experimental.pallas import tpu_sc as plsc

F32 = jnp.float32
BF16 = jnp.bfloat16
I32 = jnp.int32

D_MODEL = 1024
RET_HEADS = 4
RET_DK = 128
MOBA_HEADS = 8
MOBA_DH = 64
MOBA_BLOCK = 256
MOBA_TOPK = 3
ROPE_BASE = 10000.0
N_EXPERTS = 256
TOP_K = 8
N_GROUPS = 8
TOPK_GROUPS = 4
GROUP_SIZE = N_EXPERTS // N_GROUPS
EXPERT_FF = 256
ROUTED_SCALE = 2.5
N_MOD = 6
EPS = 1e-6
IN_COLS = 3584

LANES = 128
RET_CHUNK = 256
TM_PROJ = 512
TM_COMBINE = 256
SC_CORES = 2
SC_SUBCORES = 16
SC_CHUNK = 128
SC_GATHER_CHUNK = 64
FFN_BLOCK = 512
FFN_LOOKAHEAD = 5
FFN_WEIGHT_SLOTS = 3
FFN_OUT_SLOTS = 3
VMEM_LIMIT = 56 * 1024 * 1024

NEG_INF = float("-inf")
LOG2_E = 1.4426950408889634


def _silu(x):
    return x * jax.nn.sigmoid(x)


def _nt_dot(a, b):
    return lax.dot_general(a, b, (((1,), (1,)), ((), ())), preferred_element_type=F32)


def _tn_dot(a, b):
    return lax.dot_general(a, b, (((0,), (0,)), ((), ())), preferred_element_type=F32)


def _dot(a, b):
    return jnp.dot(a, b, preferred_element_type=F32)


HI_MASK = -65536


def _pack_halves(v):
    w = v.shape[1] // 2
    lo = lax.bitcast_convert_type(v[:, :w].astype(BF16).astype(F32), I32)
    hi = lax.bitcast_convert_type(v[:, w:].astype(BF16).astype(F32), I32)
    return lax.shift_right_logical(lo, 16) | (hi & HI_MASK)


def _unpack_halves(u):
    lo = lax.bitcast_convert_type(lax.shift_left(u, 16), F32)
    hi = lax.bitcast_convert_type(u & HI_MASK, F32)
    return lo, hi


def _adaln_kernel(c_ref, w_ref, b_ref, o_ref):
    s = _silu(c_ref[...])
    o_ref[...] = _dot(s.astype(BF16), w_ref[...].astype(BF16)) + b_ref[...]


def _adaln(c, w_ada, b_ada):
    bsz, d = c.shape
    ncol = w_ada.shape[1]
    tn = 1024
    return pl.pallas_call(
        _adaln_kernel,
        out_shape=jax.ShapeDtypeStruct((bsz, ncol), F32),
        grid=(ncol // tn,),
        in_specs=[
            pl.BlockSpec((bsz, d), lambda j: (0, 0)),
            pl.BlockSpec((d, tn), lambda j: (0, j)),
            pl.BlockSpec((1, tn), lambda j: (0, j)),
        ],
        out_specs=pl.BlockSpec((bsz, tn), lambda j: (0, j)),
        compiler_params=pltpu.CompilerParams(vmem_limit_bytes=VMEM_LIMIT),
        name="adaln",
    )(c, w_ada, b_ada.reshape(1, ncol))


def _inproj_kernel(x_ref, mod_ref, g_ref, w_ref, cos_ref, sin_ref, o_ref):
    x = x_ref[...]
    ms = jnp.mean(x * x, axis=-1, keepdims=True)
    m = mod_ref[0]
    h = (x * lax.rsqrt(ms + EPS) * g_ref[...]) * (1.0 + m[1:2]) + m[0:1]
    hb = h.astype(BF16)
    cosf = cos_ref[...]
    sinf = sin_ref[...]
    k_scale = RET_DK ** -0.5
    width = RET_HEADS * RET_DK
    for ci in range(IN_COLS // width):
        acc = _dot(hb, w_ref[:, ci * width:(ci + 1) * width])
        if ci < 2:
            for hh in range(RET_HEADS):
                xh = acc[:, hh * RET_DK:(hh + 1) * RET_DK]
                r = xh * cosf + pltpu.roll(xh, RET_DK // 2, axis=1) * sinf
                if ci == 1:
                    r = r * k_scale
                o_ref[:, ci * width + hh * RET_DK:ci * width + (hh + 1) * RET_DK] = r.astype(BF16)
        else:
            o_ref[:, ci * width:(ci + 1) * width] = acc.astype(BF16)


def _inproj(x2, mod3, g_mix, w_in_bf, cos_full, sin_signed, seq):
    n, d = x2.shape
    tm = TM_PROJ
    tiles_per_seq = seq // tm
    return pl.pallas_call(
        _inproj_kernel,
        out_shape=jax.ShapeDtypeStruct((n, IN_COLS), BF16),
        grid=(n // tm,),
        in_specs=[
            pl.BlockSpec((tm, d), lambda i: (i, 0)),
            pl.BlockSpec((1, N_MOD, d), lambda i: (i // tiles_per_seq, 0, 0)),
            pl.BlockSpec((1, d), lambda i: (0, 0)),
            pl.BlockSpec((d, IN_COLS), lambda i: (0, 0)),
            pl.BlockSpec((tm, LANES), lambda i: (i % tiles_per_seq, 0)),
            pl.BlockSpec((tm, LANES), lambda i: (i % tiles_per_seq, 0)),
        ],
        out_specs=pl.BlockSpec((tm, IN_COLS), lambda i: (i, 0)),
        compiler_params=pltpu.CompilerParams(vmem_limit_bytes=VMEM_LIMIT),
        name="inproj",
    )(x2, mod3, g_mix, w_in_bf, cos_full, sin_signed)


def _ret_kernel(lg_ref, q_ref, k_ref, v_ref, g_ref, o_ref):
    seq = q_ref.shape[1]
    c = RET_CHUNK
    lg = lg_ref[pl.program_id(1)]
    row = lax.broadcasted_iota(I32, (c, c), 0)
    col = lax.broadcasted_iota(I32, (c, c), 1)
    diff = (row - col).astype(F32)
    dmask = jnp.where(diff >= 0, jnp.exp(lg * jnp.maximum(diff, 0.0)), 0.0)
    idx = lax.broadcasted_iota(I32, (c, 1), 0).astype(F32)
    q_decay = jnp.exp(lg * (idx + 1.0))
    k_decay = jnp.exp(lg * (c - 1.0 - idx))
    chunk_decay = jnp.exp(jnp.full((1, 1), lg * c, F32))
    state = jnp.zeros((RET_DK, RET_DK), F32)
    for n in range(seq // c):
        rows = slice(n * c, (n + 1) * c)
        qn = q_ref[0, rows, :]
        kn = k_ref[0, rows, :]
        vn = v_ref[0, rows, :]
        scores = _nt_dot(qn, kn) * dmask
        inner = _dot(scores.astype(BF16), vn)
        qs = (qn.astype(F32) * q_decay).astype(BF16)
        cross = _dot(qs, state.astype(BF16))
        o = inner + cross
        o = o * lax.rsqrt(jnp.mean(o * o, axis=-1, keepdims=True) + EPS)
        gn = g_ref[0, rows, :].astype(F32)
        o_ref[0, rows, :] = (_silu(gn) * o).astype(BF16)
        ks = (kn.astype(F32) * k_decay).astype(BF16)
        state = state * chunk_decay + _tn_dot(ks, vn)


def _retention(log_g, proj3):
    bsz, seq, _ = proj3.shape
    blk = (1, seq, RET_DK)
    return pl.pallas_call(
        _ret_kernel,
        out_shape=jax.ShapeDtypeStruct((bsz, seq, RET_HEADS * RET_DK), BF16),
        grid_spec=pltpu.PrefetchScalarGridSpec(
            num_scalar_prefetch=1,
            grid=(bsz, RET_HEADS),
            in_specs=[
                pl.BlockSpec(blk, lambda b, h, lg: (b, 0, h)),
                pl.BlockSpec(blk, lambda b, h, lg: (b, 0, RET_HEADS + h)),
                pl.BlockSpec(blk, lambda b, h, lg: (b, 0, 2 * RET_HEADS + h)),
                pl.BlockSpec(blk, lambda b, h, lg: (b, 0, 3 * RET_HEADS + h)),
            ],
            out_specs=pl.BlockSpec(blk, lambda b, h, lg: (b, 0, h)),
        ),
        compiler_params=pltpu.CompilerParams(vmem_limit_bytes=VMEM_LIMIT),
        name="retention",
    )(log_g, proj3, proj3, proj3, proj3)


def _moba_kernel(q_ref, k_ref, v_ref, qg_ref, kg_ref, o_ref, qt_s, ka_s, kb_s, vta_s, vtb_s):
    seq = q_ref.shape[1]
    lb = MOBA_BLOCK
    nb = seq // lb
    lane = lax.broadcasted_iota(I32, (1, LANES), 1)
    is_a = lane < MOBA_DH
    sub = lax.broadcasted_iota(I32, (LANES, 1), 0)
    top = sub < MOBA_DH

    def head_norm(xf, gain):
        sq = xf * xf
        s_a = jnp.sum(jnp.where(is_a, sq, 0.0), axis=-1, keepdims=True)
        s_b = jnp.sum(jnp.where(is_a, 0.0, sq), axis=-1, keepdims=True)
        inv = jnp.where(is_a, lax.rsqrt(s_a / MOBA_DH + EPS), lax.rsqrt(s_b / MOBA_DH + EPS))
        return xf * inv * gain

    qg = qg_ref[...]
    kg = kg_ref[...]
    k_means = []
    for j in range(nb):
        rows = slice(j * lb, (j + 1) * lb)
        kf = head_norm(k_ref[0, rows, :].astype(F32), kg)
        ka_s[rows, :] = jnp.where(is_a, kf, 0.0).astype(BF16)
        kb_s[rows, :] = jnp.where(is_a, 0.0, kf).astype(BF16)
        k_means.append(jnp.mean(kf, axis=0, keepdims=True))
        qf = head_norm(q_ref[0, rows, :].astype(F32), qg)
        qt_s[:, rows] = (qf * (MOBA_DH ** -0.5 * LOG2_E)).T.astype(BF16)
        vt = v_ref[0, rows, :].astype(F32).T
        vta_s[:, rows] = jnp.where(top, vt, 1.0).astype(BF16)
        vtb_s[:, rows] = jnp.where(top, 1.0, vt).astype(BF16)
    k_mean = jnp.concatenate(k_means + [jnp.zeros((16 - nb, LANES), F32)], axis=0)
    k_mean_h = (jnp.where(is_a, k_mean, 0.0).astype(BF16), jnp.where(is_a, 0.0, k_mean).astype(BF16))
    k_s = (ka_s, kb_s)
    vt_s = (vta_s, vtb_s)

    r_loc = lax.broadcasted_iota(I32, (lb, lb), 0)
    c_loc = lax.broadcasted_iota(I32, (lb, lb), 1)
    causal = r_loc <= c_loc

    for i in range(nb):
        cols = slice(i * lb, (i + 1) * lb)
        qt = qt_s[:, cols]
        outs = []
        for hx in range(2):
            bias = [None] * i
            if i > MOBA_TOPK:
                gate = _dot(k_mean_h[hx], qt)
                g = [gate[j:j + 1, :] for j in range(i)]
                for j in range(i):
                    rank = jnp.zeros((1, lb), F32)
                    for j2 in range(i):
                        if j2 == j:
                            continue
                        beats = (g[j2] >= g[j]) if j2 < j else (g[j2] > g[j])
                        rank = rank + jnp.where(beats, 1.0, 0.0)
                    bias[j] = jnp.where(rank < float(MOBA_TOPK), 0.0, NEG_INF)
            pieces = []
            for j in range(i + 1):
                s = _dot(k_s[hx][j * lb:(j + 1) * lb, :], qt)
                if j == i:
                    s = jnp.where(causal, s, NEG_INF)
                elif bias[j] is not None:
                    s = s + bias[j]
                pieces.append(s)
            mx = jnp.max(pieces[0], axis=0, keepdims=True)
            for s in pieces[1:]:
                mx = jnp.maximum(mx, jnp.max(s, axis=0, keepdims=True))
            acc = jnp.zeros((LANES, lb), F32)
            for j, s in enumerate(pieces):
                p = jnp.exp2(s - mx).astype(BF16)
                acc = acc + _dot(vt_s[hx][:, j * lb:(j + 1) * lb], p)
            den = acc[MOBA_DH:MOBA_DH + 1, :] if hx == 0 else acc[0:1, :]
            outs.append(acc / den)
        o_ref[0, cols, :] = jnp.where(top, outs[0], outs[1]).T.astype(BF16)


def _moba(proj3, qg2, kg2):
    bsz, seq, _ = proj3.shape
    pairs = MOBA_HEADS // 2
    blk = (1, seq, LANES)
    base = 4 * RET_HEADS
    return pl.pallas_call(
        _moba_kernel,
        out_shape=jax.ShapeDtypeStruct((bsz, seq, MOBA_HEADS * MOBA_DH), BF16),
        grid=(bsz, pairs),
        in_specs=[
            pl.BlockSpec(blk, lambda b, p: (b, 0, base + p)),
            pl.BlockSpec(blk, lambda b, p: (b, 0, base + pairs + p)),
            pl.BlockSpec(blk, lambda b, p: (b, 0, base + 2 * pairs + p)),
            pl.BlockSpec((1, LANES), lambda b, p: (0, 0)),
            pl.BlockSpec((1, LANES), lambda b, p: (0, 0)),
        ],
        out_specs=pl.BlockSpec(blk, lambda b, p: (b, 0, p)),
        scratch_shapes=[
            pltpu.VMEM((LANES, seq), BF16),
            pltpu.VMEM((seq, LANES), BF16),
            pltpu.VMEM((seq, LANES), BF16),
            pltpu.VMEM((LANES, seq), BF16),
            pltpu.VMEM((LANES, seq), BF16),
        ],
        compiler_params=pltpu.CompilerParams(vmem_limit_bytes=VMEM_LIMIT),
        name="moba",
    )(proj3, proj3, proj3, qg2, kg2)


def _mid_kernel(ret_ref, moba_ref, x_ref, mod_ref, wo1_ref, wo2_ref, g_ref, wr_ref, rb_ref,
                wsg_ref, wsu_ref, wsd_ref,
                xb_ref, h2_ref, e_ref, w_ref, rk_ref, cnt_ref, carry_s):
    i = pl.program_id(0)
    tm = x_ref.shape[0]

    @pl.when(i == 0)
    def _():
        carry_s[...] = jnp.zeros_like(carry_s)

    m = mod_ref[0]
    mixed = _dot(ret_ref[...], wo1_ref[...]) + _dot(moba_ref[...], wo2_ref[...])
    x1 = x_ref[...] + m[2:3] * mixed
    ms = jnp.mean(x1 * x1, axis=-1, keepdims=True)
    h2 = (x1 * lax.rsqrt(ms + EPS) * g_ref[...]) * (1.0 + m[4:5]) + m[3:4]
    h2_ref[...] = _pack_halves(h2)
    h2b = h2.astype(BF16)

    hid = _silu(_dot(h2b, wsg_ref[...])) * _dot(h2b, wsu_ref[...])
    xb_ref[...] = x1 + m[5:6] * _dot(hid.astype(BF16), wsd_ref[...])

    scores = jax.nn.sigmoid(_nt_dot(wr_ref[...], h2b))
    biased = scores + rb_ref[...]
    grp = biased.reshape(N_GROUPS, GROUP_SIZE, tm)
    gi = lax.broadcasted_iota(I32, (N_GROUPS, GROUP_SIZE, tm), 1).astype(F32)
    top1 = jnp.max(grp, axis=1, keepdims=True)
    first = jnp.min(jnp.where(grp == top1, gi, float(GROUP_SIZE)), axis=1, keepdims=True)
    top2 = jnp.max(jnp.where(gi == first, NEG_INF, grp), axis=1, keepdims=True)
    gscore = (top1 + top2).reshape(N_GROUPS, tm)
    gidx = lax.broadcasted_iota(I32, (N_GROUPS, tm), 0)
    grank = jnp.zeros((N_GROUPS, tm), F32)
    for g2 in range(N_GROUPS):
        rowv = gscore[g2:g2 + 1, :]
        beats = (rowv > gscore) | ((rowv == gscore) & (g2 < gidx))
        grank = grank + jnp.where(beats, 1.0, 0.0)
    gsel = jnp.where(grank < float(TOPK_GROUPS), 1.0, 0.0)
    emask = jnp.broadcast_to(gsel.reshape(N_GROUPS, 1, tm), (N_GROUPS, GROUP_SIZE, tm)).reshape(N_EXPERTS, tm)
    choice = jnp.where(emask > 0.5, biased, NEG_INF)

    eidx = lax.broadcasted_iota(I32, (N_EXPERTS, tm), 0).astype(F32)
    remaining = choice
    e_rows = []
    for _k in range(TOP_K):
        mx = jnp.max(remaining, axis=0, keepdims=True)
        idx = jnp.min(jnp.where(remaining == mx, eidx, float(N_EXPERTS)), axis=0, keepdims=True)
        e_rows.append(idx)
        remaining = jnp.where(eidx == idx, NEG_INF, remaining)
    selmask = jnp.where(remaining != choice, 1.0, 0.0)

    tr = lax.broadcasted_iota(I32, (tm, tm), 0)
    tc = lax.broadcasted_iota(I32, (tm, tm), 1)
    upper = jnp.where(tr < tc, 1.0, 0.0).astype(BF16)
    prefix = _dot(selmask.astype(BF16), upper) + carry_s[...]
    w_rows = []
    r_rows = []
    for ek in e_rows:
        onehot = eidx == ek
        w_rows.append(jnp.sum(jnp.where(onehot, scores, 0.0), axis=0, keepdims=True))
        r_rows.append(jnp.sum(jnp.where(onehot, prefix, 0.0), axis=0, keepdims=True))
    wsum = w_rows[0]
    for wk in w_rows[1:]:
        wsum = wsum + wk
    carry_s[...] = carry_s[...] + jnp.sum(selmask, axis=1, keepdims=True)

    e_ref[...] = jnp.concatenate(e_rows, axis=0).astype(I32)
    w_ref[...] = jnp.concatenate([wk / wsum * ROUTED_SCALE for wk in w_rows], axis=0)
    rk_ref[...] = jnp.concatenate(r_rows, axis=0).astype(I32)
    cnt_ref[...] = carry_s[...].astype(I32)


def _mid(ret2, moba2, x2, mod3, wo1, wo2, g_ffn, wr_t, rbias, wsg, wsu, wsd, seq):
    n, d = x2.shape
    tm = TM_PROJ
    tiles_per_seq = seq // tm
    half = ret2.shape[1]
    ff = wsg.shape[1]
    const = lambda i: (0, 0)
    row = lambda i: (i, 0)
    colt = lambda i: (0, i)
    return pl.pallas_call(
        _mid_kernel,
        out_shape=(
            jax.ShapeDtypeStruct((n, d), F32),
            jax.ShapeDtypeStruct((n, d // 2), I32),
            jax.ShapeDtypeStruct((TOP_K, n), I32),
            jax.ShapeDtypeStruct((TOP_K, n), F32),
            jax.ShapeDtypeStruct((TOP_K, n), I32),
            jax.ShapeDtypeStruct((N_EXPERTS, 1), I32),
        ),
        grid=(n // tm,),
        in_specs=[
            pl.BlockSpec((tm, half), row),
            pl.BlockSpec((tm, half), row),
            pl.BlockSpec((tm, d), row),
            pl.BlockSpec((1, N_MOD, d), lambda i: (i // tiles_per_seq, 0, 0)),
            pl.BlockSpec((half, d), const),
            pl.BlockSpec((half, d), const),
            pl.BlockSpec((1, d), const),
            pl.BlockSpec((N_EXPERTS, d), const),
            pl.BlockSpec((N_EXPERTS, 1), const),
            pl.BlockSpec((d, ff), const),
            pl.BlockSpec((d, ff), const),
            pl.BlockSpec((ff, d), const),
        ],
        out_specs=(
            pl.BlockSpec((tm, d), row),
            pl.BlockSpec((tm, d // 2), row),
            pl.BlockSpec((TOP_K, tm), colt),
            pl.BlockSpec((TOP_K, tm), colt),
            pl.BlockSpec((TOP_K, tm), colt),
            pl.BlockSpec((N_EXPERTS, 1), const),
        ),
        scratch_shapes=[pltpu.VMEM((N_EXPERTS, 1), F32)],
        compiler_params=pltpu.CompilerParams(
            dimension_semantics=("arbitrary",), vmem_limit_bytes=VMEM_LIMIT),
        name="mid",
    )(ret2, moba2, x2, mod3, wo1, wo2, g_ffn, wr_t, rbias, wsg, wsu, wsd)


def _dest_kernel(start_ref, e_ref, rk_ref, o_ref):
    e = e_ref[...]

    def body(ex, acc):
        return acc + jnp.where(e == ex, start_ref[ex], 0)

    o_ref[...] = lax.fori_loop(0, N_EXPERTS, body, rk_ref[...], unroll=8)


def _dest(start, e_idx, rank):
    k, n = e_idx.shape
    tn = 2048
    return pl.pallas_call(
        _dest_kernel,
        out_shape=jax.ShapeDtypeStruct((k, n), I32),
        grid_spec=pltpu.PrefetchScalarGridSpec(
            num_scalar_prefetch=1,
            grid=(n // tn,),
            in_specs=[
                pl.BlockSpec((k, tn), lambda i, s: (0, i)),
                pl.BlockSpec((k, tn), lambda i, s: (0, i)),
            ],
            out_specs=pl.BlockSpec((k, tn), lambda i, s: (0, i)),
        ),
        compiler_params=pltpu.CompilerParams(vmem_limit_bytes=VMEM_LIMIT),
        name="dest",
    )(start, e_idx, rank)


def _sc_dispatch(h2p, dest3, total_rows):
    n, words = h2p.shape
    nchunks = n // SC_CHUNK
    per_worker = nchunks // (SC_CORES * SC_SUBCORES)
    mesh = plsc.VectorSubcoreMesh(core_axis_name="c", subcore_axis_name="s",
                                  num_cores=SC_CORES, num_subcores=SC_SUBCORES)

    @functools.partial(
        pl.kernel, mesh=mesh,
        out_type=jax.ShapeDtypeStruct((total_rows, words), I32),
        scratch_types=[
            pltpu.VMEM((TOP_K, SC_CHUNK), I32),
            pltpu.VMEM((SC_CHUNK, words), I32),
            pltpu.SemaphoreType.DMA,
        ],
        name="sc_dispatch",
    )
    def run(h_hbm, d_hbm, xs_hbm, idx_v, rows_v, sem):
        wid = lax.axis_index("s") * SC_CORES + lax.axis_index("c")

        @pl.loop(0, per_worker)
        def _(j):
            ch = wid * per_worker + j
            pltpu.sync_copy(d_hbm.at[ch], idx_v)
            pltpu.sync_copy(h_hbm.at[pl.ds(ch * SC_CHUNK, SC_CHUNK)], rows_v)
            copies = [pltpu.async_copy(rows_v, xs_hbm.at[idx_v.at[k]], sem) for k in range(TOP_K)]
            for cp in copies:
                cp.wait()

    return run(h2p, dest3)


def _ffn_kernel(head_ref, exp_ref, valid_ref, first_ref, ahead_ref, slot_ref,
                x_hbm, wg_hbm, wu_hbm, wd_hbm, y_hbm, x_s, y_s, wg_s, wu_s, wd_s, sem_x, sem_y, sem):
    i = pl.program_id(0)
    nreal = head_ref[0]
    rows_per = x_s.shape[1]

    def row_copy(g):
        slot = lax.rem(g, FFN_LOOKAHEAD + 1)
        return pltpu.make_async_copy(x_hbm.at[pl.ds(g * rows_per, rows_per)], x_s.at[slot], sem_x.at[slot])

    def out_copy(g):
        slot = lax.rem(g, FFN_OUT_SLOTS)
        return pltpu.make_async_copy(y_s.at[slot], y_hbm.at[pl.ds(g * rows_per, rows_per)], sem_y.at[slot])

    def weight_copies(e, s):
        return (pltpu.make_async_copy(wg_hbm.at[e], wg_s.at[s], sem.at[s, 0]),
                pltpu.make_async_copy(wu_hbm.at[e], wu_s.at[s], sem.at[s, 1]),
                pltpu.make_async_copy(wd_hbm.at[e], wd_s.at[s], sem.at[s, 2]))

    @pl.when(i == 0)
    def _():
        for j in range(FFN_WEIGHT_SLOTS - 1):
            @pl.when(head_ref[1 + j] >= 0)
            def _():
                for cp in weight_copies(head_ref[1 + j], j):
                    cp.start()
        for g in range(FFN_LOOKAHEAD):
            @pl.when(g < nreal)
            def _():
                row_copy(g).start()

    @pl.when(i < nreal)
    def _():
        s = slot_ref[i]

        @pl.when(i + FFN_LOOKAHEAD < nreal)
        def _():
            row_copy(i + FFN_LOOKAHEAD).start()

        row_copy(i).wait()

        @pl.when(first_ref[i] == 1)
        def _():
            for cp in weight_copies(exp_ref[i], s):
                cp.wait()

            @pl.when(ahead_ref[i] >= 0)
            def _():
                for cp in weight_copies(ahead_ref[i], lax.rem(s + FFN_WEIGHT_SLOTS - 1, FFN_WEIGHT_SLOTS)):
                    cp.start()

        @pl.when(i >= FFN_OUT_SLOTS)
        def _():
            out_copy(i - FFN_OUT_SLOTS).wait()

        half = x_s.shape[2]
        r = lax.broadcasted_iota(I32, (rows_per, 1), 0)
        x_blk = x_s[lax.rem(i, FFN_LOOKAHEAD + 1)]
        x_lo, x_hi = _unpack_halves(jnp.where(r < valid_ref[i], x_blk, 0))
        hg = _dot(x_lo, wg_s[s, :half, :]) + _dot(x_hi, wg_s[s, half:, :])
        hu = _dot(x_lo, wu_s[s, :half, :]) + _dot(x_hi, wu_s[s, half:, :])
        y_s[lax.rem(i, FFN_OUT_SLOTS)] = _pack_halves(_dot(_silu(hg) * hu, wd_s[s]))
        out_copy(i).start()

        @pl.when(i == nreal - 1)
        def _():
            for back in range(FFN_OUT_SLOTS):
                @pl.when(i - back >= 0)
                def _():
                    out_copy(i - back).wait()


def _ffn(sched, xs, w_gate, w_up, w_down):
    p, half = xs.shape
    d = 2 * half
    ff = w_gate.shape[2]
    steps = sched[1].shape[0]
    return pl.pallas_call(
        _ffn_kernel,
        out_shape=jax.ShapeDtypeStruct((p, half), I32),
        grid_spec=pltpu.PrefetchScalarGridSpec(
            num_scalar_prefetch=len(sched),
            grid=(steps,),
            in_specs=[
                pl.BlockSpec(memory_space=pl.ANY),
                pl.BlockSpec(memory_space=pl.ANY),
                pl.BlockSpec(memory_space=pl.ANY),
                pl.BlockSpec(memory_space=pl.ANY),
            ],
            out_specs=pl.BlockSpec(memory_space=pl.ANY),
            scratch_shapes=[
                pltpu.VMEM((FFN_LOOKAHEAD + 1, FFN_BLOCK, half), I32),
                pltpu.VMEM((FFN_OUT_SLOTS, FFN_BLOCK, half), I32),
                pltpu.VMEM((FFN_WEIGHT_SLOTS, d, ff), F32),
                pltpu.VMEM((FFN_WEIGHT_SLOTS, d, ff), F32),
                pltpu.VMEM((FFN_WEIGHT_SLOTS, ff, d), F32),
                pltpu.SemaphoreType.DMA((FFN_LOOKAHEAD + 1,)),
                pltpu.SemaphoreType.DMA((FFN_OUT_SLOTS,)),
                pltpu.SemaphoreType.DMA((FFN_WEIGHT_SLOTS, 3)),
            ],
        ),
        compiler_params=pltpu.CompilerParams(
            dimension_semantics=("arbitrary",), vmem_limit_bytes=VMEM_LIMIT, has_side_effects=True),
        name="ffn",
    )(*sched, xs, w_gate, w_up, w_down)


def _sc_gather(y, dest3):
    a, words = y.shape
    nchunks, _, chunk = dest3.shape
    n = nchunks * chunk
    per_worker = nchunks // (SC_CORES * SC_SUBCORES)
    mesh = plsc.VectorSubcoreMesh(core_axis_name="c", subcore_axis_name="s",
                                  num_cores=SC_CORES, num_subcores=SC_SUBCORES)

    @functools.partial(
        pl.kernel, mesh=mesh,
        out_type=jax.ShapeDtypeStruct((TOP_K, n, words), I32),
        scratch_types=[
            pltpu.VMEM((TOP_K, chunk), I32),
            pltpu.VMEM((chunk, words), I32),
            pltpu.VMEM((chunk, words), I32),
            pltpu.SemaphoreType.DMA,
            pltpu.SemaphoreType.DMA((2,)),
        ],
        name="sc_gather",
    )
    def run(y_hbm, d_hbm, yt_hbm, idx_v, buf0, buf1, sem_g, sem_w):
        wid = lax.axis_index("s") * SC_CORES + lax.axis_index("c")
        bufs = (buf0, buf1)

        @pl.loop(0, per_worker)
        def _(j):
            ch = wid * per_worker + j
            pltpu.sync_copy(d_hbm.at[ch], idx_v)
            rows = pl.ds(ch * chunk, chunk)
            gather = pltpu.async_copy(y_hbm.at[idx_v.at[0]], bufs[0], sem_g)
            writes = []
            for k in range(TOP_K):
                gather.wait()
                writes.append(pltpu.async_copy(bufs[k % 2], yt_hbm.at[k, rows], sem_w.at[k % 2]))
                if k + 1 < TOP_K:
                    if k >= 1:
                        writes[k - 1].wait()
                    gather = pltpu.async_copy(y_hbm.at[idx_v.at[k + 1]], bufs[(k + 1) % 2], sem_g)
            writes[TOP_K - 2].wait()
            writes[TOP_K - 1].wait()

    return run(y, dest3)


def _combine_kernel(yt_ref, wt_ref, xb_ref, mod_ref, o_ref):
    half = yt_ref.shape[2]
    wt = wt_ref[...]
    lo, hi = _unpack_halves(yt_ref[0])
    r_lo = lo * wt[:, 0:1]
    r_hi = hi * wt[:, 0:1]
    for k in range(1, TOP_K):
        lo, hi = _unpack_halves(yt_ref[k])
        r_lo = r_lo + lo * wt[:, k:k + 1]
        r_hi = r_hi + hi * wt[:, k:k + 1]
    gate = mod_ref[0][5:6]
    o_ref[:, :half] = xb_ref[:, :half] + gate[:, :half] * r_lo
    o_ref[:, half:] = xb_ref[:, half:] + gate[:, half:] * r_hi


def _combine(yt, w_t, xb, mod3, seq):
    n, d = xb.shape
    tm = TM_COMBINE
    tiles_per_seq = seq // tm
    return pl.pallas_call(
        _combine_kernel,
        out_shape=jax.ShapeDtypeStruct((n, d), F32),
        grid=(n // tm,),
        in_specs=[
            pl.BlockSpec((TOP_K, tm, d // 2), lambda i: (0, i, 0)),
            pl.BlockSpec((tm, TOP_K), lambda i: (i, 0)),
            pl.BlockSpec((tm, d), lambda i: (i, 0)),
            pl.BlockSpec((1, N_MOD, d), lambda i: (i // tiles_per_seq, 0, 0)),
        ],
        out_specs=pl.BlockSpec((tm, d), lambda i: (i, 0)),
        compiler_params=pltpu.CompilerParams(vmem_limit_bytes=VMEM_LIMIT),
        name="combine",
    )(yt, w_t, xb, mod3)


def _rotary_tables(seq):
    half = RET_DK // 2
    pos = jnp.arange(seq, dtype=F32)
    inv = ROPE_BASE ** (-jnp.arange(half, dtype=F32) / half)
    ang = pos[:, None] * inv[None, :]
    cos = jnp.cos(ang)
    sin = jnp.sin(ang)
    return jnp.concatenate([cos, cos], axis=-1), jnp.concatenate([-sin, sin], axis=-1)


def _ffn_schedule(counts, nblk):
    padded = (counts + FFN_BLOCK - 1) // FFN_BLOCK * FFN_BLOCK
    ends = jnp.cumsum(padded)
    start = ends - padded
    nreal = ends[-1:] // FFN_BLOCK
    blk = jnp.minimum(jnp.arange(nblk, dtype=I32), nreal - 1)
    row0 = blk * FFN_BLOCK
    exp = jnp.sum((ends[None, :] <= row0[:, None]).astype(I32), axis=1)
    eids = jnp.arange(N_EXPERTS, dtype=I32)
    mine = exp[:, None] == eids[None, :]

    def per_block(table):
        return jnp.sum(jnp.where(mine, table[None, :], 0), axis=1)

    valid = jnp.clip(per_block(counts + start) - row0, 0, FFN_BLOCK)
    steps = jnp.arange(nblk, dtype=I32)
    first = ((steps < nreal) & ((steps == 0) | (exp != jnp.roll(exp, 1)))).astype(I32)
    used = counts > 0
    ordinal = jnp.cumsum(used.astype(I32)) - 1

    def used_at(pos):
        hit = used[None, :] & (ordinal[None, :] == pos[:, None])
        return jnp.sum(jnp.where(hit, eids[None, :] + 1, 0), axis=1) - 1

    ahead = used_at(ordinal + (FFN_WEIGHT_SLOTS - 1))
    head = jnp.concatenate([nreal, used_at(jnp.arange(FFN_WEIGHT_SLOTS - 1, dtype=I32))])
    return start, (head, exp, valid, first, per_block(ahead), per_block(ordinal % FFN_WEIGHT_SLOTS))


def _chunked(dest, chunk):
    k, n = dest.shape
    return dest.reshape(k, n // chunk, chunk).transpose(1, 0, 2)


def kernel(x, c, w_ada, b_ada, g_mix, w_in, q_gain, k_gain, w_out, g_ffn, w_router, router_bias,
           w_gate, w_up, w_down, ws_gate, ws_up, ws_down):
    bsz, seq, d = x.shape
    n = bsz * seq
    depth = w_ada.shape[0]
    cos_full, sin_signed = _rotary_tables(seq)
    log_g = jnp.log1p(-jnp.exp2(-5.0 - jnp.arange(RET_HEADS, dtype=F32)))
    ret_w = RET_HEADS * RET_DK
    x2 = x.reshape(n, d)
    for l in range(depth):
        mod3 = _adaln(c, w_ada[l], b_ada[l]).reshape(bsz, N_MOD, d)
        proj = _inproj(x2, mod3, g_mix[l].reshape(1, d), w_in[l].astype(BF16), cos_full, sin_signed, seq)
        proj3 = proj.reshape(bsz, seq, IN_COLS)
        ret = _retention(log_g, proj3)
        qg2 = jnp.tile(q_gain[l].reshape(1, MOBA_DH), (1, 2))
        kg2 = jnp.tile(k_gain[l].reshape(1, MOBA_DH), (1, 2))
        moba = _moba(proj3, qg2, kg2)
        wo = w_out[l].astype(BF16)
        xb, h2, e_idx, w_k, rank, cnt = _mid(
            ret.reshape(n, ret_w), moba.reshape(n, MOBA_HEADS * MOBA_DH), x2, mod3,
            wo[:ret_w], wo[ret_w:], g_ffn[l].reshape(1, d),
            w_router[l].T.astype(BF16), router_bias[l].reshape(N_EXPERTS, 1),
            ws_gate[l].astype(BF16), ws_up[l].astype(BF16), ws_down[l].astype(BF16), seq)
        nblk = n * TOP_K // FFN_BLOCK + N_EXPERTS
        start, sched = _ffn_schedule(cnt[:, 0], nblk)
        dest = _dest(start, e_idx, rank)
        xs = _sc_dispatch(h2, _chunked(dest, SC_CHUNK), nblk * FFN_BLOCK)
        y = _ffn(sched, xs, w_gate[l], w_up[l], w_down[l])
        yt = _sc_gather(y, _chunked(dest, SC_GATHER_CHUNK))
        x2 = _combine(yt, w_k.T, xb, mod3, seq)
    return x2.reshape(bsz, seq, d)
```

```python
import functools

import numpy as np
import jax
import jax.numpy as jnp
from jax import lax
from jax.experimental import pallas as pl
from jax.experimental.pallas import tpu as pltpu
from jax.experimental.pallas import tpu_sc as plsc

F32 = jnp.float32
BF16 = jnp.bfloat16
I32 = jnp.int32

D_MODEL = 1024
RET_HEADS = 4
RET_DK = 128
MOBA_HEADS = 8
MOBA_DH = 64
MOBA_BLOCK = 256
MOBA_TOPK = 3
ROPE_BASE = 10000.0
N_EXPERTS = 256
TOP_K = 8
N_GROUPS = 8
TOPK_GROUPS = 4
GROUP_SIZE = N_EXPERTS // N_GROUPS
EXPERT_FF = 256
ROUTED_SCALE = 2.5
N_MOD = 6
EPS = 1e-6
IN_COLS = 3584

LANES = 128
RET_CHUNK = 256
TM_PROJ = 512
TM_COMBINE = 256
SC_CORES = 2
SC_SUBCORES = 16
SC_CHUNK = 128
SC_GATHER_CHUNK = 64
FFN_BLOCK = 512
FFN_GRAIN = 128
FFN_LOOKAHEAD = 5
FFN_WEIGHT_SLOTS = 3
FFN_OUT_SLOTS = 3
VMEM_LIMIT = 56 * 1024 * 1024

NEG_INF = float("-inf")
LOG2_E = 1.4426950408889634


def _silu(x):
    return x * jax.nn.sigmoid(x)


def _nt_dot(a, b):
    return lax.dot_general(a, b, (((1,), (1,)), ((), ())), preferred_element_type=F32)


def _tn_dot(a, b):
    return lax.dot_general(a, b, (((0,), (0,)), ((), ())), preferred_element_type=F32)


def _dot(a, b):
    return jnp.dot(a, b, preferred_element_type=F32)


HI_MASK = -65536


def _pack_halves(v):
    w = v.shape[1] // 2
    lo = lax.bitcast_convert_type(v[:, :w].astype(BF16).astype(F32), I32)
    hi = lax.bitcast_convert_type(v[:, w:].astype(BF16).astype(F32), I32)
    return lax.shift_right_logical(lo, 16) | (hi & HI_MASK)


def _unpack_halves(u):
    lo = lax.bitcast_convert_type(lax.shift_left(u, 16), F32)
    hi = lax.bitcast_convert_type(u & HI_MASK, F32)
    return lo, hi


def _adaln_kernel(c_ref, w_ref, b_ref, o_ref):
    s = _silu(c_ref[...])
    o_ref[...] = _dot(s.astype(BF16), w_ref[...].astype(BF16)) + b_ref[...]


def _adaln(c, w_ada, b_ada):
    bsz, d = c.shape
    ncol = w_ada.shape[1]
    tn = 1024
    return pl.pallas_call(
        _adaln_kernel,
        out_shape=jax.ShapeDtypeStruct((bsz, ncol), F32),
        grid=(ncol // tn,),
        in_specs=[
            pl.BlockSpec((bsz, d), lambda j: (0, 0)),
            pl.BlockSpec((d, tn), lambda j: (0, j)),
            pl.BlockSpec((1, tn), lambda j: (0, j)),
        ],
        out_specs=pl.BlockSpec((bsz, tn), lambda j: (0, j)),
        compiler_params=pltpu.CompilerParams(vmem_limit_bytes=VMEM_LIMIT),
        name="adaln",
    )(c, w_ada, b_ada.reshape(1, ncol))


def _inproj_kernel(x_ref, mod_ref, g_ref, w_ref, cos_ref, sin_ref, o_ref):
    x = x_ref[...]
    ms = jnp.mean(x * x, axis=-1, keepdims=True)
    m = mod_ref[0]
    h = (x * lax.rsqrt(ms + EPS) * g_ref[...]) * (1.0 + m[1:2]) + m[0:1]
    hb = h.astype(BF16)
    cosf = cos_ref[...]
    sinf = sin_ref[...]
    k_scale = RET_DK ** -0.5
    width = RET_HEADS * RET_DK
    for ci in range(IN_COLS // width):
        acc = _dot(hb, w_ref[:, ci * width:(ci + 1) * width])
        if ci < 2:
            for hh in range(RET_HEADS):
                xh = acc[:, hh * RET_DK:(hh + 1) * RET_DK]
                r = xh * cosf + pltpu.roll(xh, RET_DK // 2, axis=1) * sinf
                if ci == 1:
                    r = r * k_scale
                o_ref[:, ci * width + hh * RET_DK:ci * width + (hh + 1) * RET_DK] = r.astype(BF16)
        else:
            o_ref[:, ci * width:(ci + 1) * width] = acc.astype(BF16)


def _inproj(x2, mod3, g_mix, w_in_bf, cos_full, sin_signed, seq):
    n, d = x2.shape
    tm = TM_PROJ
    tiles_per_seq = seq // tm
    return pl.pallas_call(
        _inproj_kernel,
        out_shape=jax.ShapeDtypeStruct((n, IN_COLS), BF16),
        grid=(n // tm,),
        in_specs=[
            pl.BlockSpec((tm, d), lambda i: (i, 0)),
            pl.BlockSpec((1, N_MOD, d), lambda i: (i // tiles_per_seq, 0, 0)),
            pl.BlockSpec((1, d), lambda i: (0, 0)),
            pl.BlockSpec((d, IN_COLS), lambda i: (0, 0)),
            pl.BlockSpec((tm, LANES), lambda i: (i % tiles_per_seq, 0)),
            pl.BlockSpec((tm, LANES), lambda i: (i % tiles_per_seq, 0)),
        ],
        out_specs=pl.BlockSpec((tm, IN_COLS), lambda i: (i, 0)),
        compiler_params=pltpu.CompilerParams(vmem_limit_bytes=VMEM_LIMIT),
        name="inproj",
    )(x2, mod3, g_mix, w_in_bf, cos_full, sin_signed)


def _ret_kernel(lg_ref, q_ref, k_ref, v_ref, g_ref, o_ref):
    seq = q_ref.shape[1]
    c = RET_CHUNK
    lg = lg_ref[pl.program_id(1)]
    row = lax.broadcasted_iota(I32, (c, c), 0)
    col = lax.broadcasted_iota(I32, (c, c), 1)
    diff = (row - col).astype(F32)
    dmask = jnp.where(diff >= 0, jnp.exp(lg * jnp.maximum(diff, 0.0)), 0.0)
    idx = lax.broadcasted_iota(I32, (c, 1), 0).astype(F32)
    q_decay = jnp.exp(lg * (idx + 1.0))
    k_decay = jnp.exp(lg * (c - 1.0 - idx))
    chunk_decay = jnp.exp(jnp.full((1, 1), lg * c, F32))
    state = jnp.zeros((RET_DK, RET_DK), F32)
    for n in range(seq // c):
        rows = slice(n * c, (n + 1) * c)
        qn = q_ref[0, rows, :]
        kn = k_ref[0, rows, :]
        vn = v_ref[0, rows, :]
        scores = _nt_dot(qn, kn) * dmask
        inner = _dot(scores.astype(BF16), vn)
        qs = (qn.astype(F32) * q_decay).astype(BF16)
        cross = _dot(qs, state.astype(BF16))
        o = inner + cross
        o = o * lax.rsqrt(jnp.mean(o * o, axis=-1, keepdims=True) + EPS)
        gn = g_ref[0, rows, :].astype(F32)
        o_ref[0, rows, :] = (_silu(gn) * o).astype(BF16)
        ks = (kn.astype(F32) * k_decay).astype(BF16)
        state = state * chunk_decay + _tn_dot(ks, vn)


def _retention(log_g, proj3):
    bsz, seq, _ = proj3.shape
    blk = (1, seq, RET_DK)
    return pl.pallas_call(
        _ret_kernel,
        out_shape=jax.ShapeDtypeStruct((bsz, seq, RET_HEADS * RET_DK), BF16),
        grid_spec=pltpu.PrefetchScalarGridSpec(
            num_scalar_prefetch=1,
            grid=(bsz, RET_HEADS),
            in_specs=[
                pl.BlockSpec(blk, lambda b, h, lg: (b, 0, h)),
                pl.BlockSpec(blk, lambda b, h, lg: (b, 0, RET_HEADS + h)),
                pl.BlockSpec(blk, lambda b, h, lg: (b, 0, 2 * RET_HEADS + h)),
                pl.BlockSpec(blk, lambda b, h, lg: (b, 0, 3 * RET_HEADS + h)),
            ],
            out_specs=pl.BlockSpec(blk, lambda b, h, lg: (b, 0, h)),
        ),
        compiler_params=pltpu.CompilerParams(vmem_limit_bytes=VMEM_LIMIT),
        name="retention",
    )(log_g, proj3, proj3, proj3, proj3)


def _moba_kernel(q_ref, k_ref, v_ref, qg_ref, kg_ref, o_ref, qt_s, ka_s, kb_s, vta_s, vtb_s):
    seq = q_ref.shape[1]
    lb = MOBA_BLOCK
    nb = seq // lb
    lane = lax.broadcasted_iota(I32, (1, LANES), 1)
    is_a = lane < MOBA_DH
    sub = lax.broadcasted_iota(I32, (LANES, 1), 0)
    top = sub < MOBA_DH

    def head_norm(xf, gain):
        sq = xf * xf
        s_a = jnp.sum(jnp.where(is_a, sq, 0.0), axis=-1, keepdims=True)
        s_b = jnp.sum(jnp.where(is_a, 0.0, sq), axis=-1, keepdims=True)
        inv = jnp.where(is_a, lax.rsqrt(s_a / MOBA_DH + EPS), lax.rsqrt(s_b / MOBA_DH + EPS))
        return xf * inv * gain

    qg = qg_ref[...]
    kg = kg_ref[...]
    k_means = []
    for j in range(nb):
        rows = slice(j * lb, (j + 1) * lb)
        kf = head_norm(k_ref[0, rows, :].astype(F32), kg)
        ka_s[rows, :] = jnp.where(is_a, kf, 0.0).astype(BF16)
        kb_s[rows, :] = jnp.where(is_a, 0.0, kf).astype(BF16)
        k_means.append(jnp.mean(kf, axis=0, keepdims=True))
        qf = head_norm(q_ref[0, rows, :].astype(F32), qg)
        qt_s[:, rows] = (qf * (MOBA_DH ** -0.5 * LOG2_E)).T.astype(BF16)
        vt = v_ref[0, rows, :].astype(F32).T
        vta_s[:, rows] = jnp.where(top, vt, 1.0).astype(BF16)
        vtb_s[:, rows] = jnp.where(top, 1.0, vt).astype(BF16)
    k_mean = jnp.concatenate(k_means + [jnp.zeros((16 - nb, LANES), F32)], axis=0)
    k_mean_h = (jnp.where(is_a, k_mean, 0.0).astype(BF16), jnp.where(is_a, 0.0, k_mean).astype(BF16))
    k_s = (ka_s, kb_s)
    vt_s = (vta_s, vtb_s)

    r_loc = lax.broadcasted_iota(I32, (lb, lb), 0)
    c_loc = lax.broadcasted_iota(I32, (lb, lb), 1)
    causal = r_loc <= c_loc

    for i in range(nb):
        cols = slice(i * lb, (i + 1) * lb)
        qt = qt_s[:, cols]
        outs = []
        for hx in range(2):
            bias = [None] * i
            if i > MOBA_TOPK:
                gate = _dot(k_mean_h[hx], qt)
                g = [gate[j:j + 1, :] for j in range(i)]
                for j in range(i):
                    rank = jnp.zeros((1, lb), F32)
                    for j2 in range(i):
                        if j2 == j:
                            continue
                        beats = (g[j2] >= g[j]) if j2 < j else (g[j2] > g[j])
                        rank = rank + jnp.where(beats, 1.0, 0.0)
                    bias[j] = jnp.where(rank < float(MOBA_TOPK), 0.0, NEG_INF)
            pieces = []
            for j in range(i + 1):
                s = _dot(k_s[hx][j * lb:(j + 1) * lb, :], qt)
                if j == i:
                    s = jnp.where(causal, s, NEG_INF)
                elif bias[j] is not None:
                    s = s + bias[j]
                pieces.append(s)
            mx = jnp.max(pieces[0], axis=0, keepdims=True)
            for s in pieces[1:]:
                mx = jnp.maximum(mx, jnp.max(s, axis=0, keepdims=True))
            acc = jnp.zeros((LANES, lb), F32)
            for j, s in enumerate(pieces):
                p = jnp.exp2(s - mx).astype(BF16)
                acc = acc + _dot(vt_s[hx][:, j * lb:(j + 1) * lb], p)
            den = acc[MOBA_DH:MOBA_DH + 1, :] if hx == 0 else acc[0:1, :]
            outs.append(acc / den)
        o_ref[0, cols, :] = jnp.where(top, outs[0], outs[1]).T.astype(BF16)


def _moba(proj3, qg2, kg2):
    bsz, seq, _ = proj3.shape
    pairs = MOBA_HEADS // 2
    blk = (1, seq, LANES)
    base = 4 * RET_HEADS
    return pl.pallas_call(
        _moba_kernel,
        out_shape=jax.ShapeDtypeStruct((bsz, seq, MOBA_HEADS * MOBA_DH), BF16),
        grid=(bsz, pairs),
        in_specs=[
            pl.BlockSpec(blk, lambda b, p: (b, 0, base + p)),
            pl.BlockSpec(blk, lambda b, p: (b, 0, base + pairs + p)),
            pl.BlockSpec(blk, lambda b, p: (b, 0, base + 2 * pairs + p)),
            pl.BlockSpec((1, LANES), lambda b, p: (0, 0)),
            pl.BlockSpec((1, LANES), lambda b, p: (0, 0)),
        ],
        out_specs=pl.BlockSpec(blk, lambda b, p: (b, 0, p)),
        scratch_shapes=[
            pltpu.VMEM((LANES, seq), BF16),
            pltpu.VMEM((seq, LANES), BF16),
            pltpu.VMEM((seq, LANES), BF16),
            pltpu.VMEM((LANES, seq), BF16),
            pltpu.VMEM((LANES, seq), BF16),
        ],
        compiler_params=pltpu.CompilerParams(vmem_limit_bytes=VMEM_LIMIT),
        name="moba",
    )(proj3, proj3, proj3, qg2, kg2)


def _mid_kernel(ret_ref, moba_ref, x_ref, mod_ref, wo1_ref, wo2_ref, g_ref, wr_ref, rb_ref,
                wsg_ref, wsu_ref, wsd_ref,
                xb_ref, h2_ref, e_ref, w_ref, rk_ref, cnt_ref, carry_s):
    i = pl.program_id(0)
    tm = x_ref.shape[0]

    @pl.when(i == 0)
    def _():
        carry_s[...] = jnp.zeros_like(carry_s)

    m = mod_ref[0]
    mixed = _dot(ret_ref[...], wo1_ref[...]) + _dot(moba_ref[...], wo2_ref[...])
    x1 = x_ref[...] + m[2:3] * mixed
    ms = jnp.mean(x1 * x1, axis=-1, keepdims=True)
    h2 = (x1 * lax.rsqrt(ms + EPS) * g_ref[...]) * (1.0 + m[4:5]) + m[3:4]
    h2_ref[...] = _pack_halves(h2)
    h2b = h2.astype(BF16)

    hid = _silu(_dot(h2b, wsg_ref[...])) * _dot(h2b, wsu_ref[...])
    xb_ref[...] = x1 + m[5:6] * _dot(hid.astype(BF16), wsd_ref[...])

    scores = jax.nn.sigmoid(_nt_dot(wr_ref[...], h2b))
    biased = scores + rb_ref[...]
    grp = biased.reshape(N_GROUPS, GROUP_SIZE, tm)
    gi = lax.broadcasted_iota(I32, (N_GROUPS, GROUP_SIZE, tm), 1).astype(F32)
    top1 = jnp.max(grp, axis=1, keepdims=True)
    first = jnp.min(jnp.where(grp == top1, gi, float(GROUP_SIZE)), axis=1, keepdims=True)
    top2 = jnp.max(jnp.where(gi == first, NEG_INF, grp), axis=1, keepdims=True)
    gscore = (top1 + top2).reshape(N_GROUPS, tm)
    gidx = lax.broadcasted_iota(I32, (N_GROUPS, tm), 0)
    grank = jnp.zeros((N_GROUPS, tm), F32)
    for g2 in range(N_GROUPS):
        rowv = gscore[g2:g2 + 1, :]
        beats = (rowv > gscore) | ((rowv == gscore) & (g2 < gidx))
        grank = grank + jnp.where(beats, 1.0, 0.0)
    gsel = jnp.where(grank < float(TOPK_GROUPS), 1.0, 0.0)
    emask = jnp.broadcast_to(gsel.reshape(N_GROUPS, 1, tm), (N_GROUPS, GROUP_SIZE, tm)).reshape(N_EXPERTS, tm)
    choice = jnp.where(emask > 0.5, biased, NEG_INF)

    eidx = lax.broadcasted_iota(I32, (N_EXPERTS, tm), 0).astype(F32)
    remaining = choice
    e_rows = []
    for _k in range(TOP_K):
        mx = jnp.max(remaining, axis=0, keepdims=True)
        idx = jnp.min(jnp.where(remaining == mx, eidx, float(N_EXPERTS)), axis=0, keepdims=True)
        e_rows.append(idx)
        remaining = jnp.where(eidx == idx, NEG_INF, remaining)
    selmask = jnp.where(remaining != choice, 1.0, 0.0)

    tr = lax.broadcasted_iota(I32, (tm, tm), 0)
    tc = lax.broadcasted_iota(I32, (tm, tm), 1)
    upper = jnp.where(tr < tc, 1.0, 0.0).astype(BF16)
    prefix = _dot(selmask.astype(BF16), upper) + carry_s[...]
    w_rows = []
    r_rows = []
    for ek in e_rows:
        onehot = eidx == ek
        w_rows.append(jnp.sum(jnp.where(onehot, scores, 0.0), axis=0, keepdims=True))
        r_rows.append(jnp.sum(jnp.where(onehot, prefix, 0.0), axis=0, keepdims=True))
    wsum = w_rows[0]
    for wk in w_rows[1:]:
        wsum = wsum + wk
    carry_s[...] = carry_s[...] + jnp.sum(selmask, axis=1, keepdims=True)

    e_ref[...] = jnp.concatenate(e_rows, axis=0).astype(I32)
    w_ref[...] = jnp.concatenate([wk / wsum * ROUTED_SCALE for wk in w_rows], axis=0)
    rk_ref[...] = jnp.concatenate(r_rows, axis=0).astype(I32)
    cnt_ref[...] = carry_s[...].astype(I32)


def _mid(ret2, moba2, x2, mod3, wo1, wo2, g_ffn, wr_t, rbias, wsg, wsu, wsd, seq):
    n, d = x2.shape
    tm = TM_PROJ
    tiles_per_seq = seq // tm
    half = ret2.shape[1]
    ff = wsg.shape[1]
    const = lambda i: (0, 0)
    row = lambda i: (i, 0)
    colt = lambda i: (0, i)
    return pl.pallas_call(
        _mid_kernel,
        out_shape=(
            jax.ShapeDtypeStruct((n, d), F32),
            jax.ShapeDtypeStruct((n, d // 2), I32),
            jax.ShapeDtypeStruct((TOP_K, n), I32),
            jax.ShapeDtypeStruct((TOP_K, n), F32),
            jax.ShapeDtypeStruct((TOP_K, n), I32),
            jax.ShapeDtypeStruct((N_EXPERTS, 1), I32),
        ),
        grid=(n // tm,),
        in_specs=[
            pl.BlockSpec((tm, half), row),
            pl.BlockSpec((tm, half), row),
            pl.BlockSpec((tm, d), row),
            pl.BlockSpec((1, N_MOD, d), lambda i: (i // tiles_per_seq, 0, 0)),
            pl.BlockSpec((half, d), const),
            pl.BlockSpec((half, d), const),
            pl.BlockSpec((1, d), const),
            pl.BlockSpec((N_EXPERTS, d), const),
            pl.BlockSpec((N_EXPERTS, 1), const),
            pl.BlockSpec((d, ff), const),
            pl.BlockSpec((d, ff), const),
            pl.BlockSpec((ff, d), const),
        ],
        out_specs=(
            pl.BlockSpec((tm, d), row),
            pl.BlockSpec((tm, d // 2), row),
            pl.BlockSpec((TOP_K, tm), colt),
            pl.BlockSpec((TOP_K, tm), colt),
            pl.BlockSpec((TOP_K, tm), colt),
            pl.BlockSpec((N_EXPERTS, 1), const),
        ),
        scratch_shapes=[pltpu.VMEM((N_EXPERTS, 1), F32)],
        compiler_params=pltpu.CompilerParams(
            dimension_semantics=("arbitrary",), vmem_limit_bytes=VMEM_LIMIT),
        name="mid",
    )(ret2, moba2, x2, mod3, wo1, wo2, g_ffn, wr_t, rbias, wsg, wsu, wsd)


def _dest_kernel(start_ref, e_ref, rk_ref, o_ref):
    e = e_ref[...]

    def body(ex, acc):
        return acc + jnp.where(e == ex, start_ref[ex], 0)

    o_ref[...] = lax.fori_loop(0, N_EXPERTS, body, rk_ref[...], unroll=8)


def _dest(start, e_idx, rank):
    k, n = e_idx.shape
    tn = 2048
    return pl.pallas_call(
        _dest_kernel,
        out_shape=jax.ShapeDtypeStruct((k, n), I32),
        grid_spec=pltpu.PrefetchScalarGridSpec(
            num_scalar_prefetch=1,
            grid=(n // tn,),
            in_specs=[
                pl.BlockSpec((k, tn), lambda i, s: (0, i)),
                pl.BlockSpec((k, tn), lambda i, s: (0, i)),
            ],
            out_specs=pl.BlockSpec((k, tn), lambda i, s: (0, i)),
        ),
        compiler_params=pltpu.CompilerParams(vmem_limit_bytes=VMEM_LIMIT),
        name="dest",
    )(start, e_idx, rank)


def _sc_dispatch(h2p, dest3, total_rows):
    n, words = h2p.shape
    nchunks = n // SC_CHUNK
    per_worker = nchunks // (SC_CORES * SC_SUBCORES)
    mesh = plsc.VectorSubcoreMesh(core_axis_name="c", subcore_axis_name="s",
                                  num_cores=SC_CORES, num_subcores=SC_SUBCORES)

    @functools.partial(
        pl.kernel, mesh=mesh,
        out_type=jax.ShapeDtypeStruct((total_rows, words), I32),
        scratch_types=[
            pltpu.VMEM((TOP_K, SC_CHUNK), I32),
            pltpu.VMEM((SC_CHUNK, words), I32),
            pltpu.SemaphoreType.DMA,
        ],
        name="sc_dispatch",
    )
    def run(h_hbm, d_hbm, xs_hbm, idx_v, rows_v, sem):
        wid = lax.axis_index("s") * SC_CORES + lax.axis_index("c")

        @pl.loop(0, per_worker)
        def _(j):
            ch = wid * per_worker + j
            pltpu.sync_copy(d_hbm.at[ch], idx_v)
            pltpu.sync_copy(h_hbm.at[pl.ds(ch * SC_CHUNK, SC_CHUNK)], rows_v)
            copies = [pltpu.async_copy(rows_v, xs_hbm.at[idx_v.at[k]], sem) for k in range(TOP_K)]
            for cp in copies:
                cp.wait()

    return run(h2p, dest3)


def _ffn_kernel(head_ref, exp_ref, valid_ref, first_ref, ahead_ref, slot_ref,
                x_hbm, wg_hbm, wu_hbm, wd_hbm, y_hbm, x_s, y_s, wg_s, wu_s, wd_s, sem_x, sem_y, sem):
    i = pl.program_id(0)
    nreal = head_ref[0]
    rows_per = x_s.shape[1]
    half = x_s.shape[2]
    sizes = tuple(range(FFN_GRAIN, rows_per + 1, FFN_GRAIN))

    def rows_needed(g):
        return (valid_ref[g] + (FFN_GRAIN - 1)) // FFN_GRAIN * FFN_GRAIN

    def by_size(nrows, fn):
        for n in sizes:
            @pl.when(nrows == n)
            def _():
                fn(n)

    def row_copy(g, n):
        slot = lax.rem(g, FFN_LOOKAHEAD + 1)
        return pltpu.make_async_copy(
            x_hbm.at[pl.ds(g * rows_per, n)], x_s.at[slot, pl.ds(0, n)], sem_x.at[slot])

    def out_copy(g, n):
        slot = lax.rem(g, FFN_OUT_SLOTS)
        return pltpu.make_async_copy(
            y_s.at[slot, pl.ds(0, n)], y_hbm.at[pl.ds(g * rows_per, n)], sem_y.at[slot])

    def weight_copies(e, s):
        return (pltpu.make_async_copy(wg_hbm.at[e], wg_s.at[s], sem.at[s, 0]),
                pltpu.make_async_copy(wu_hbm.at[e], wu_s.at[s], sem.at[s, 1]),
                pltpu.make_async_copy(wd_hbm.at[e], wd_s.at[s], sem.at[s, 2]))

    @pl.when(i == 0)
    def _():
        for j in range(FFN_WEIGHT_SLOTS - 1):
            @pl.when(head_ref[1 + j] >= 0)
            def _():
                for cp in weight_copies(head_ref[1 + j], j):
                    cp.start()
        for g in range(FFN_LOOKAHEAD):
            @pl.when(g < nreal)
            def _():
                by_size(rows_needed(g), lambda n: row_copy(g, n).start())

    @pl.when(i < nreal)
    def _():
        s = slot_ref[i]
        fetch = i + FFN_LOOKAHEAD

        @pl.when(fetch < nreal)
        def _():
            by_size(rows_needed(fetch), lambda n: row_copy(fetch, n).start())

        by_size(rows_needed(i), lambda n: row_copy(i, n).wait())

        @pl.when(first_ref[i] == 1)
        def _():
            for cp in weight_copies(exp_ref[i], s):
                cp.wait()

            @pl.when(ahead_ref[i] >= 0)
            def _():
                for cp in weight_copies(ahead_ref[i], lax.rem(s + FFN_WEIGHT_SLOTS - 1, FFN_WEIGHT_SLOTS)):
                    cp.start()

        @pl.when(i >= FFN_OUT_SLOTS)
        def _():
            done = i - FFN_OUT_SLOTS
            by_size(rows_needed(done), lambda n: out_copy(done, n).wait())

        x_slot = lax.rem(i, FFN_LOOKAHEAD + 1)
        y_slot = lax.rem(i, FFN_OUT_SLOTS)
        valid = valid_ref[i]

        def expert_rows(n):
            r = lax.broadcasted_iota(I32, (n, 1), 0)
            x_lo, x_hi = _unpack_halves(jnp.where(r < valid, x_s[x_slot, pl.ds(0, n), :], 0))
            hg = _dot(x_lo, wg_s[s, :half, :]) + _dot(x_hi, wg_s[s, half:, :])
            hu = _dot(x_lo, wu_s[s, :half, :]) + _dot(x_hi, wu_s[s, half:, :])
            y_s[y_slot, pl.ds(0, n), :] = _pack_halves(_dot(_silu(hg) * hu, wd_s[s]))
            out_copy(i, n).start()

        by_size(rows_needed(i), expert_rows)

        @pl.when(i == nreal - 1)
        def _():
            for back in range(FFN_OUT_SLOTS):
                last = i - back

                @pl.when(last >= 0)
                def _():
                    by_size(rows_needed(last), lambda n: out_copy(last, n).wait())


def _ffn(sched, xs, w_gate, w_up, w_down):
    p, half = xs.shape
    d = 2 * half
    ff = w_gate.shape[2]
    steps = sched[1].shape[0]
    return pl.pallas_call(
        _ffn_kernel,
        out_shape=jax.ShapeDtypeStruct((p, half), I32),
        grid_spec=pltpu.PrefetchScalarGridSpec(
            num_scalar_prefetch=len(sched),
            grid=(steps,),
            in_specs=[
                pl.BlockSpec(memory_space=pl.ANY),
                pl.BlockSpec(memory_space=pl.ANY),
                pl.BlockSpec(memory_space=pl.ANY),
                pl.BlockSpec(memory_space=pl.ANY),
            ],
            out_specs=pl.BlockSpec(memory_space=pl.ANY),
            scratch_shapes=[
                pltpu.VMEM((FFN_LOOKAHEAD + 1, FFN_BLOCK, half), I32),
                pltpu.VMEM((FFN_OUT_SLOTS, FFN_BLOCK, half), I32),
                pltpu.VMEM((FFN_WEIGHT_SLOTS, d, ff), F32),
                pltpu.VMEM((FFN_WEIGHT_SLOTS, d, ff), F32),
                pltpu.VMEM((FFN_WEIGHT_SLOTS, ff, d), F32),
                pltpu.SemaphoreType.DMA((FFN_LOOKAHEAD + 1,)),
                pltpu.SemaphoreType.DMA((FFN_OUT_SLOTS,)),
                pltpu.SemaphoreType.DMA((FFN_WEIGHT_SLOTS, 3)),
            ],
        ),
        compiler_params=pltpu.CompilerParams(
            dimension_semantics=("arbitrary",), vmem_limit_bytes=VMEM_LIMIT, has_side_effects=True),
        name="ffn",
    )(*sched, xs, w_gate, w_up, w_down)


def _sc_gather(y, dest3):
    a, words = y.shape
    nchunks, _, chunk = dest3.shape
    n = nchunks * chunk
    per_worker = nchunks // (SC_CORES * SC_SUBCORES)
    mesh = plsc.VectorSubcoreMesh(core_axis_name="c", subcore_axis_name="s",
                                  num_cores=SC_CORES, num_subcores=SC_SUBCORES)

    @functools.partial(
        pl.kernel, mesh=mesh,
        out_type=jax.ShapeDtypeStruct((TOP_K, n, words), I32),
        scratch_types=[
            pltpu.VMEM((TOP_K, chunk), I32),
            pltpu.VMEM((chunk, words), I32),
            pltpu.VMEM((chunk, words), I32),
            pltpu.SemaphoreType.DMA,
            pltpu.SemaphoreType.DMA((2,)),
        ],
        name="sc_gather",
    )
    def run(y_hbm, d_hbm, yt_hbm, idx_v, buf0, buf1, sem_g, sem_w):
        wid = lax.axis_index("s") * SC_CORES + lax.axis_index("c")
        bufs = (buf0, buf1)

        @pl.loop(0, per_worker)
        def _(j):
            ch = wid * per_worker + j
            pltpu.sync_copy(d_hbm.at[ch], idx_v)
            rows = pl.ds(ch * chunk, chunk)
            gather = pltpu.async_copy(y_hbm.at[idx_v.at[0]], bufs[0], sem_g)
            writes = []
            for k in range(TOP_K):
                gather.wait()
                writes.append(pltpu.async_copy(bufs[k % 2], yt_hbm.at[k, rows], sem_w.at[k % 2]))
                if k + 1 < TOP_K:
                    if k >= 1:
                        writes[k - 1].wait()
                    gather = pltpu.async_copy(y_hbm.at[idx_v.at[k + 1]], bufs[(k + 1) % 2], sem_g)
            writes[TOP_K - 2].wait()
            writes[TOP_K - 1].wait()

    return run(y, dest3)


def _combine_kernel(yt_ref, wt_ref, xb_ref, mod_ref, o_ref):
    half = yt_ref.shape[2]
    wt = wt_ref[...]
    lo, hi = _unpack_halves(yt_ref[0])
    r_lo = lo * wt[:, 0:1]
    r_hi = hi * wt[:, 0:1]
    for k in range(1, TOP_K):
        lo, hi = _unpack_halves(yt_ref[k])
        r_lo = r_lo + lo * wt[:, k:k + 1]
        r_hi = r_hi + hi * wt[:, k:k + 1]
    gate = mod_ref[0][5:6]
    o_ref[:, :half] = xb_ref[:, :half] + gate[:, :half] * r_lo
    o_ref[:, half:] = xb_ref[:, half:] + gate[:, half:] * r_hi


def _combine(yt, w_t, xb, mod3, seq):
    n, d = xb.shape
    tm = TM_COMBINE
    tiles_per_seq = seq // tm
    return pl.pallas_call(
        _combine_kernel,
        out_shape=jax.ShapeDtypeStruct((n, d), F32),
        grid=(n // tm,),
        in_specs=[
            pl.BlockSpec((TOP_K, tm, d // 2), lambda i: (0, i, 0)),
            pl.BlockSpec((tm, TOP_K), lambda i: (i, 0)),
            pl.BlockSpec((tm, d), lambda i: (i, 0)),
            pl.BlockSpec((1, N_MOD, d), lambda i: (i // tiles_per_seq, 0, 0)),
        ],
        out_specs=pl.BlockSpec((tm, d), lambda i: (i, 0)),
        compiler_params=pltpu.CompilerParams(vmem_limit_bytes=VMEM_LIMIT),
        name="combine",
    )(yt, w_t, xb, mod3)


def _rotary_tables(seq):
    half = RET_DK // 2
    pos = jnp.arange(seq, dtype=F32)
    inv = ROPE_BASE ** (-jnp.arange(half, dtype=F32) / half)
    ang = pos[:, None] * inv[None, :]
    cos = jnp.cos(ang)
    sin = jnp.sin(ang)
    return jnp.concatenate([cos, cos], axis=-1), jnp.concatenate([-sin, sin], axis=-1)


def _ffn_schedule(counts, nblk):
    padded = (counts + FFN_BLOCK - 1) // FFN_BLOCK * FFN_BLOCK
    ends = jnp.cumsum(padded)
    start = ends - padded
    nreal = ends[-1:] // FFN_BLOCK
    blk = jnp.minimum(jnp.arange(nblk, dtype=I32), nreal - 1)
    row0 = blk * FFN_BLOCK
    exp = jnp.sum((ends[None, :] <= row0[:, None]).astype(I32), axis=1)
    eids = jnp.arange(N_EXPERTS, dtype=I32)
    mine = exp[:, None] == eids[None, :]

    def per_block(table):
        return jnp.sum(jnp.where(mine, table[None, :], 0), axis=1)

    valid = jnp.clip(per_block(counts + start) - row0, 0, FFN_BLOCK)
    steps = jnp.arange(nblk, dtype=I32)
    first = ((steps < nreal) & ((steps == 0) | (exp != jnp.roll(exp, 1)))).astype(I32)
    used = counts > 0
    ordinal = jnp.cumsum(used.astype(I32)) - 1

    def used_at(pos):
        hit = used[None, :] & (ordinal[None, :] == pos[:, None])
        return jnp.sum(jnp.where(hit, eids[None, :] + 1, 0), axis=1) - 1

    ahead = used_at(ordinal + (FFN_WEIGHT_SLOTS - 1))
    head = jnp.concatenate([nreal, used_at(jnp.arange(FFN_WEIGHT_SLOTS - 1, dtype=I32))])
    return start, (head, exp, valid, first, per_block(ahead), per_block(ordinal % FFN_WEIGHT_SLOTS))


def _chunked(dest, chunk):
    k, n = dest.shape
    return dest.reshape(k, n // chunk, chunk).transpose(1, 0, 2)


def kernel(x, c, w_ada, b_ada, g_mix, w_in, q_gain, k_gain, w_out, g_ffn, w_router, router_bias,
           w_gate, w_up, w_down, ws_gate, ws_up, ws_down):
    bsz, seq, d = x.shape
    n = bsz * seq
    depth = w_ada.shape[0]
    cos_full, sin_signed = _rotary_tables(seq)
    log_g = jnp.log1p(-jnp.exp2(-5.0 - jnp.arange(RET_HEADS, dtype=F32)))
    ret_w = RET_HEADS * RET_DK
    x2 = x.reshape(n, d)
    for l in range(depth):
        mod3 = _adaln(c, w_ada[l], b_ada[l]).reshape(bsz, N_MOD, d)
        proj = _inproj(x2, mod3, g_mix[l].reshape(1, d), w_in[l].astype(BF16), cos_full, sin_signed, seq)
        proj3 = proj.reshape(bsz, seq, IN_COLS)
        ret = _retention(log_g, proj3)
        qg2 = jnp.tile(q_gain[l].reshape(1, MOBA_DH), (1, 2))
        kg2 = jnp.tile(k_gain[l].reshape(1, MOBA_DH), (1, 2))
        moba = _moba(proj3, qg2, kg2)
        wo = w_out[l].astype(BF16)
        xb, h2, e_idx, w_k, rank, cnt = _mid(
            ret.reshape(n, ret_w), moba.reshape(n, MOBA_HEADS * MOBA_DH), x2, mod3,
            wo[:ret_w], wo[ret_w:], g_ffn[l].reshape(1, d),
            w_router[l].T.astype(BF16), router_bias[l].reshape(N_EXPERTS, 1),
            ws_gate[l].astype(BF16), ws_up[l].astype(BF16), ws_down[l].astype(BF16), seq)
        nblk = n * TOP_K // FFN_BLOCK + N_EXPERTS
        start, sched = _ffn_schedule(cnt[:, 0], nblk)
        dest = _dest(start, e_idx, rank)
        xs = _sc_dispatch(h2, _chunked(dest, SC_CHUNK), nblk * FFN_BLOCK)
        y = _ffn(sched, xs, w_gate[l], w_up[l], w_down[l])
        yt = _sc_gather(y, _chunked(dest, SC_GATHER_CHUNK))
        x2 = _combine(yt, w_k.T, xb, mod3, seq)
    return x2.reshape(bsz, seq, d)
```

```python
import functools

import numpy as np
import jax
import jax.numpy as jnp
from jax import lax
from jax.experimental import pallas as pl
from jax.experimental.pallas import tpu as pltpu
from jax.experimental.pallas import tpu_sc as plsc

F32 = jnp.float32
BF16 = jnp.bfloat16
I32 = jnp.int32

D_MODEL = 1024
RET_HEADS = 4
RET_DK = 128
MOBA_HEADS = 8
MOBA_DH = 64
MOBA_BLOCK = 256
MOBA_TOPK = 3
ROPE_BASE = 10000.0
N_EXPERTS = 256
TOP_K = 8
N_GROUPS = 8
TOPK_GROUPS = 4
GROUP_SIZE = N_EXPERTS // N_GROUPS
EXPERT_FF = 256
ROUTED_SCALE = 2.5
N_MOD = 6
EPS = 1e-6
IN_COLS = 3584

LANES = 128
RET_CHUNK = 256
TM_PROJ = 512
TM_COMBINE = 512
MOBA_ONES_ROWS = 16
SC_CORES = 2
SC_SUBCORES = 16
SC_CHUNK = 128
SC_GATHER_CHUNK = 64
FFN_BLOCK = 512
FFN_GRAIN = 128
FFN_LOOKAHEAD = 5
FFN_WEIGHT_SLOTS = 3
FFN_OUT_SLOTS = 3
VMEM_LIMIT = 56 * 1024 * 1024

NEG_INF = float("-inf")
LOG2_E = 1.4426950408889634


def _silu(x):
    return x * jax.nn.sigmoid(x)


def _nt_dot(a, b):
    return lax.dot_general(a, b, (((1,), (1,)), ((), ())), preferred_element_type=F32)


def _tn_dot(a, b):
    return lax.dot_general(a, b, (((0,), (0,)), ((), ())), preferred_element_type=F32)


def _dot(a, b):
    return jnp.dot(a, b, preferred_element_type=F32)


HI_MASK = -65536


def _pack_halves(v):
    w = v.shape[1] // 2
    lo = lax.bitcast_convert_type(v[:, :w].astype(BF16).astype(F32), I32)
    hi = lax.bitcast_convert_type(v[:, w:].astype(BF16).astype(F32), I32)
    return lax.shift_right_logical(lo, 16) | (hi & HI_MASK)


def _unpack_halves(u):
    lo = lax.bitcast_convert_type(lax.shift_left(u, 16), F32)
    hi = lax.bitcast_convert_type(u & HI_MASK, F32)
    return lo, hi


def _adaln_kernel(c_ref, w_ref, b_ref, o_ref):
    s = _silu(c_ref[...])
    o_ref[...] = _dot(s.astype(BF16), w_ref[...].astype(BF16)) + b_ref[...]


def _adaln(c, w_ada, b_ada):
    bsz, d = c.shape
    ncol = w_ada.shape[1]
    tn = 1024
    return pl.pallas_call(
        _adaln_kernel,
        out_shape=jax.ShapeDtypeStruct((bsz, ncol), F32),
        grid=(ncol // tn,),
        in_specs=[
            pl.BlockSpec((bsz, d), lambda j: (0, 0)),
            pl.BlockSpec((d, tn), lambda j: (0, j)),
            pl.BlockSpec((1, tn), lambda j: (0, j)),
        ],
        out_specs=pl.BlockSpec((bsz, tn), lambda j: (0, j)),
        compiler_params=pltpu.CompilerParams(vmem_limit_bytes=VMEM_LIMIT),
        name="adaln",
    )(c, w_ada, b_ada.reshape(1, ncol))


def _inproj_kernel(x_ref, mod_ref, g_ref, w_ref, cos_ref, sin_ref, o_ref):
    x = x_ref[...]
    ms = jnp.mean(x * x, axis=-1, keepdims=True)
    m = mod_ref[0]
    h = (x * lax.rsqrt(ms + EPS) * g_ref[...]) * (1.0 + m[1:2]) + m[0:1]
    hb = h.astype(BF16)
    cosf = cos_ref[...]
    sinf = sin_ref[...]
    k_scale = RET_DK ** -0.5
    width = RET_HEADS * RET_DK
    for ci in range(IN_COLS // width):
        acc = _dot(hb, w_ref[:, ci * width:(ci + 1) * width])
        if ci < 2:
            for hh in range(RET_HEADS):
                xh = acc[:, hh * RET_DK:(hh + 1) * RET_DK]
                r = xh * cosf + pltpu.roll(xh, RET_DK // 2, axis=1) * sinf
                if ci == 1:
                    r = r * k_scale
                o_ref[:, ci * width + hh * RET_DK:ci * width + (hh + 1) * RET_DK] = r.astype(BF16)
        else:
            o_ref[:, ci * width:(ci + 1) * width] = acc.astype(BF16)


def _inproj(x2, mod3, g_mix, w_in_bf, cos_full, sin_signed, seq):
    n, d = x2.shape
    tm = TM_PROJ
    tiles_per_seq = seq // tm
    return pl.pallas_call(
        _inproj_kernel,
        out_shape=jax.ShapeDtypeStruct((n, IN_COLS), BF16),
        grid=(n // tm,),
        in_specs=[
            pl.BlockSpec((tm, d), lambda i: (i, 0)),
            pl.BlockSpec((1, N_MOD, d), lambda i: (i // tiles_per_seq, 0, 0)),
            pl.BlockSpec((1, d), lambda i: (0, 0)),
            pl.BlockSpec((d, IN_COLS), lambda i: (0, 0)),
            pl.BlockSpec((tm, LANES), lambda i: (i % tiles_per_seq, 0)),
            pl.BlockSpec((tm, LANES), lambda i: (i % tiles_per_seq, 0)),
        ],
        out_specs=pl.BlockSpec((tm, IN_COLS), lambda i: (i, 0)),
        compiler_params=pltpu.CompilerParams(vmem_limit_bytes=VMEM_LIMIT),
        name="inproj",
    )(x2, mod3, g_mix, w_in_bf, cos_full, sin_signed)


def _ret_kernel(lg_ref, q_ref, k_ref, v_ref, g_ref, o_ref):
    seq = q_ref.shape[1]
    c = RET_CHUNK
    lg = lg_ref[pl.program_id(1)]
    row = lax.broadcasted_iota(I32, (c, c), 0)
    col = lax.broadcasted_iota(I32, (c, c), 1)
    diff = (row - col).astype(F32)
    dmask = jnp.where(diff >= 0, jnp.exp(lg * jnp.maximum(diff, 0.0)), 0.0)
    idx = lax.broadcasted_iota(I32, (c, 1), 0).astype(F32)
    q_decay = jnp.exp(lg * (idx + 1.0))
    k_decay = jnp.exp(lg * (c - 1.0 - idx))
    chunk_decay = jnp.exp(jnp.full((1, 1), lg * c, F32))
    state = jnp.zeros((RET_DK, RET_DK), F32)
    for n in range(seq // c):
        rows = slice(n * c, (n + 1) * c)
        qn = q_ref[0, rows, :]
        kn = k_ref[0, rows, :]
        vn = v_ref[0, rows, :]
        scores = _nt_dot(qn, kn) * dmask
        inner = _dot(scores.astype(BF16), vn)
        qs = (qn.astype(F32) * q_decay).astype(BF16)
        cross = _dot(qs, state.astype(BF16))
        o = inner + cross
        o = o * lax.rsqrt(jnp.mean(o * o, axis=-1, keepdims=True) + EPS)
        gn = g_ref[0, rows, :].astype(F32)
        o_ref[0, rows, :] = (_silu(gn) * o).astype(BF16)
        ks = (kn.astype(F32) * k_decay).astype(BF16)
        state = state * chunk_decay + _tn_dot(ks, vn)


def _retention(log_g, proj3):
    bsz, seq, _ = proj3.shape
    blk = (1, seq, RET_DK)
    return pl.pallas_call(
        _ret_kernel,
        out_shape=jax.ShapeDtypeStruct((bsz, seq, RET_HEADS * RET_DK), BF16),
        grid_spec=pltpu.PrefetchScalarGridSpec(
            num_scalar_prefetch=1,
            grid=(bsz, RET_HEADS),
            in_specs=[
                pl.BlockSpec(blk, lambda b, h, lg: (b, 0, h)),
                pl.BlockSpec(blk, lambda b, h, lg: (b, 0, RET_HEADS + h)),
                pl.BlockSpec(blk, lambda b, h, lg: (b, 0, 2 * RET_HEADS + h)),
                pl.BlockSpec(blk, lambda b, h, lg: (b, 0, 3 * RET_HEADS + h)),
            ],
            out_specs=pl.BlockSpec(blk, lambda b, h, lg: (b, 0, h)),
        ),
        compiler_params=pltpu.CompilerParams(vmem_limit_bytes=VMEM_LIMIT),
        name="retention",
    )(log_g, proj3, proj3, proj3, proj3)


def _moba_kernel(q_ref, k_ref, v_ref, qg_ref, kg_ref, o_ref, qt_s, ka_s, kb_s, vta_s, vtb_s):
    seq = q_ref.shape[1]
    lb = MOBA_BLOCK
    nb = seq // lb
    lane = lax.broadcasted_iota(I32, (1, LANES), 1)
    is_a = lane < MOBA_DH

    def head_norm(xf, gain):
        sq = xf * xf
        s_a = jnp.sum(jnp.where(is_a, sq, 0.0), axis=-1, keepdims=True)
        s_b = jnp.sum(jnp.where(is_a, 0.0, sq), axis=-1, keepdims=True)
        inv = jnp.where(is_a, lax.rsqrt(s_a / MOBA_DH + EPS), lax.rsqrt(s_b / MOBA_DH + EPS))
        return xf * inv * gain

    qg = qg_ref[...]
    kg = kg_ref[...]
    k_means = []
    for j in range(nb):
        rows = slice(j * lb, (j + 1) * lb)
        kf = head_norm(k_ref[0, rows, :].astype(F32), kg)
        ka_s[rows, :] = jnp.where(is_a, kf, 0.0).astype(BF16)
        kb_s[rows, :] = jnp.where(is_a, 0.0, kf).astype(BF16)
        k_means.append(jnp.mean(kf, axis=0, keepdims=True))
        qf = head_norm(q_ref[0, rows, :].astype(F32), qg)
        qt_s[:, rows] = (qf * (MOBA_DH ** -0.5 * LOG2_E)).T.astype(BF16)
        vt = v_ref[0, rows, :].astype(F32).T
        ones = jnp.ones((MOBA_ONES_ROWS, lb), F32)
        vta_s[:, rows] = jnp.concatenate([vt[:MOBA_DH], ones], axis=0).astype(BF16)
        vtb_s[:, rows] = jnp.concatenate([vt[MOBA_DH:], ones], axis=0).astype(BF16)
    k_mean = jnp.concatenate(k_means + [jnp.zeros((16 - nb, LANES), F32)], axis=0)
    k_mean_h = (jnp.where(is_a, k_mean, 0.0).astype(BF16), jnp.where(is_a, 0.0, k_mean).astype(BF16))
    k_s = (ka_s, kb_s)
    vt_s = (vta_s, vtb_s)

    r_loc = lax.broadcasted_iota(I32, (lb, lb), 0)
    c_loc = lax.broadcasted_iota(I32, (lb, lb), 1)
    causal = r_loc <= c_loc

    for i in range(nb):
        cols = slice(i * lb, (i + 1) * lb)
        qt = qt_s[:, cols]
        outs = []
        for hx in range(2):
            bias = [None] * i
            if i > MOBA_TOPK:
                gate = _dot(k_mean_h[hx], qt)
                g = [gate[j:j + 1, :] for j in range(i)]
                for j in range(i):
                    rank = jnp.zeros((1, lb), F32)
                    for j2 in range(i):
                        if j2 == j:
                            continue
                        beats = (g[j2] >= g[j]) if j2 < j else (g[j2] > g[j])
                        rank = rank + jnp.where(beats, 1.0, 0.0)
                    bias[j] = jnp.where(rank < float(MOBA_TOPK), 0.0, NEG_INF)
            pieces = []
            for j in range(i + 1):
                s = _dot(k_s[hx][j * lb:(j + 1) * lb, :], qt)
                if j == i:
                    s = jnp.where(causal, s, NEG_INF)
                elif bias[j] is not None:
                    s = s + bias[j]
                pieces.append(s)
            mx = jnp.max(pieces[0], axis=0, keepdims=True)
            for s in pieces[1:]:
                mx = jnp.maximum(mx, jnp.max(s, axis=0, keepdims=True))
            acc = jnp.zeros((MOBA_DH + MOBA_ONES_ROWS, lb), F32)
            for j, s in enumerate(pieces):
                p = jnp.exp2(s - mx).astype(BF16)
                acc = acc + _dot(vt_s[hx][:, j * lb:(j + 1) * lb], p)
            outs.append(acc[:MOBA_DH] / acc[MOBA_DH:MOBA_DH + 1, :])
        o_ref[0, cols, :] = jnp.concatenate(outs, axis=0).T.astype(BF16)


def _moba(proj3, qg2, kg2):
    bsz, seq, _ = proj3.shape
    pairs = MOBA_HEADS // 2
    blk = (1, seq, LANES)
    base = 4 * RET_HEADS
    return pl.pallas_call(
        _moba_kernel,
        out_shape=jax.ShapeDtypeStruct((bsz, seq, MOBA_HEADS * MOBA_DH), BF16),
        grid=(bsz, pairs),
        in_specs=[
            pl.BlockSpec(blk, lambda b, p: (b, 0, base + p)),
            pl.BlockSpec(blk, lambda b, p: (b, 0, base + pairs + p)),
            pl.BlockSpec(blk, lambda b, p: (b, 0, base + 2 * pairs + p)),
            pl.BlockSpec((1, LANES), lambda b, p: (0, 0)),
            pl.BlockSpec((1, LANES), lambda b, p: (0, 0)),
        ],
        out_specs=pl.BlockSpec(blk, lambda b, p: (b, 0, p)),
        scratch_shapes=[
            pltpu.VMEM((LANES, seq), BF16),
            pltpu.VMEM((seq, LANES), BF16),
            pltpu.VMEM((seq, LANES), BF16),
            pltpu.VMEM((MOBA_DH + MOBA_ONES_ROWS, seq), BF16),
            pltpu.VMEM((MOBA_DH + MOBA_ONES_ROWS, seq), BF16),
        ],
        compiler_params=pltpu.CompilerParams(vmem_limit_bytes=VMEM_LIMIT),
        name="moba",
    )(proj3, proj3, proj3, qg2, kg2)


def _mid_kernel(ret_ref, moba_ref, x_ref, mod_ref, wo1_ref, wo2_ref, g_ref, wr_ref, rb_ref,
                wsg_ref, wsu_ref, wsd_ref,
                xb_ref, h2_ref, e_ref, w_ref, rk_ref, cnt_ref, carry_s):
    i = pl.program_id(0)
    tm = x_ref.shape[0]

    @pl.when(i == 0)
    def _():
        carry_s[...] = jnp.zeros_like(carry_s)

    m = mod_ref[0]
    mixed = _dot(ret_ref[...], wo1_ref[...]) + _dot(moba_ref[...], wo2_ref[...])
    x1 = x_ref[...] + m[2:3] * mixed
    ms = jnp.mean(x1 * x1, axis=-1, keepdims=True)
    h2 = (x1 * lax.rsqrt(ms + EPS) * g_ref[...]) * (1.0 + m[4:5]) + m[3:4]
    h2_ref[...] = _pack_halves(h2)
    h2b = h2.astype(BF16)

    hid = _silu(_dot(h2b, wsg_ref[...])) * _dot(h2b, wsu_ref[...])
    xb_ref[...] = x1 + m[5:6] * _dot(hid.astype(BF16), wsd_ref[...])

    scores = jax.nn.sigmoid(_nt_dot(wr_ref[...], h2b))
    biased = scores + rb_ref[...]
    grp = biased.reshape(N_GROUPS, GROUP_SIZE, tm)
    gi = lax.broadcasted_iota(I32, (N_GROUPS, GROUP_SIZE, tm), 1).astype(F32)
    top1 = jnp.max(grp, axis=1, keepdims=True)
    first = jnp.min(jnp.where(grp == top1, gi, float(GROUP_SIZE)), axis=1, keepdims=True)
    top2 = jnp.max(jnp.where(gi == first, NEG_INF, grp), axis=1, keepdims=True)
    gscore = (top1 + top2).reshape(N_GROUPS, tm)
    gidx = lax.broadcasted_iota(I32, (N_GROUPS, tm), 0)
    grank = jnp.zeros((N_GROUPS, tm), F32)
    for g2 in range(N_GROUPS):
        rowv = gscore[g2:g2 + 1, :]
        beats = (rowv > gscore) | ((rowv == gscore) & (g2 < gidx))
        grank = grank + jnp.where(beats, 1.0, 0.0)
    gsel = jnp.where(grank < float(TOPK_GROUPS), 1.0, 0.0)
    emask = jnp.broadcast_to(gsel.reshape(N_GROUPS, 1, tm), (N_GROUPS, GROUP_SIZE, tm)).reshape(N_EXPERTS, tm)
    choice = jnp.where(emask > 0.5, biased, NEG_INF)

    eidx = lax.broadcasted_iota(I32, (N_EXPERTS, tm), 0).astype(F32)
    remaining = choice
    e_rows = []
    for _k in range(TOP_K):
        mx = jnp.max(remaining, axis=0, keepdims=True)
        idx = jnp.min(jnp.where(remaining == mx, eidx, float(N_EXPERTS)), axis=0, keepdims=True)
        e_rows.append(idx)
        remaining = jnp.where(eidx == idx, NEG_INF, remaining)
    selmask = jnp.where(remaining != choice, 1.0, 0.0)

    tr = lax.broadcasted_iota(I32, (tm, tm), 0)
    tc = lax.broadcasted_iota(I32, (tm, tm), 1)
    upper = jnp.where(tr < tc, 1.0, 0.0).astype(BF16)
    prefix = _dot(selmask.astype(BF16), upper) + carry_s[...]
    w_rows = []
    r_rows = []
    for ek in e_rows:
        onehot = eidx == ek
        w_rows.append(jnp.sum(jnp.where(onehot, scores, 0.0), axis=0, keepdims=True))
        r_rows.append(jnp.sum(jnp.where(onehot, prefix, 0.0), axis=0, keepdims=True))
    wsum = w_rows[0]
    for wk in w_rows[1:]:
        wsum = wsum + wk
    carry_s[...] = carry_s[...] + jnp.sum(selmask, axis=1, keepdims=True)

    e_ref[...] = jnp.concatenate(e_rows, axis=0).astype(I32)
    w_ref[...] = jnp.concatenate([wk / wsum * ROUTED_SCALE for wk in w_rows], axis=0)
    rk_ref[...] = jnp.concatenate(r_rows, axis=0).astype(I32)
    cnt_ref[...] = carry_s[...].astype(I32)


def _mid(ret2, moba2, x2, mod3, wo1, wo2, g_ffn, wr_t, rbias, wsg, wsu, wsd, seq):
    n, d = x2.shape
    tm = TM_PROJ
    tiles_per_seq = seq // tm
    half = ret2.shape[1]
    ff = wsg.shape[1]
    const = lambda i: (0, 0)
    row = lambda i: (i, 0)
    colt = lambda i: (0, i)
    return pl.pallas_call(
        _mid_kernel,
        out_shape=(
            jax.ShapeDtypeStruct((n, d), F32),
            jax.ShapeDtypeStruct((n, d // 2), I32),
            jax.ShapeDtypeStruct((TOP_K, n), I32),
            jax.ShapeDtypeStruct((TOP_K, n), F32),
            jax.ShapeDtypeStruct((TOP_K, n), I32),
            jax.ShapeDtypeStruct((N_EXPERTS, 1), I32),
        ),
        grid=(n // tm,),
        in_specs=[
            pl.BlockSpec((tm, half), row),
            pl.BlockSpec((tm, half), row),
            pl.BlockSpec((tm, d), row),
            pl.BlockSpec((1, N_MOD, d), lambda i: (i // tiles_per_seq, 0, 0)),
            pl.BlockSpec((half, d), const),
            pl.BlockSpec((half, d), const),
            pl.BlockSpec((1, d), const),
            pl.BlockSpec((N_EXPERTS, d), const),
            pl.BlockSpec((N_EXPERTS, 1), const),
            pl.BlockSpec((d, ff), const),
            pl.BlockSpec((d, ff), const),
            pl.BlockSpec((ff, d), const),
        ],
        out_specs=(
            pl.BlockSpec((tm, d), row),
            pl.BlockSpec((tm, d // 2), row),
            pl.BlockSpec((TOP_K, tm), colt),
            pl.BlockSpec((TOP_K, tm), colt),
            pl.BlockSpec((TOP_K, tm), colt),
            pl.BlockSpec((N_EXPERTS, 1), const),
        ),
        scratch_shapes=[pltpu.VMEM((N_EXPERTS, 1), F32)],
        compiler_params=pltpu.CompilerParams(
            dimension_semantics=("arbitrary",), vmem_limit_bytes=VMEM_LIMIT),
        name="mid",
    )(ret2, moba2, x2, mod3, wo1, wo2, g_ffn, wr_t, rbias, wsg, wsu, wsd)


def _dest_kernel(start_ref, e_ref, rk_ref, o_ref):
    e = e_ref[...]

    def body(ex, acc):
        return acc + jnp.where(e == ex, start_ref[ex], 0)

    o_ref[...] = lax.fori_loop(0, N_EXPERTS, body, rk_ref[...], unroll=8)


def _dest(start, e_idx, rank):
    k, n = e_idx.shape
    tn = 2048
    return pl.pallas_call(
        _dest_kernel,
        out_shape=jax.ShapeDtypeStruct((k, n), I32),
        grid_spec=pltpu.PrefetchScalarGridSpec(
            num_scalar_prefetch=1,
            grid=(n // tn,),
            in_specs=[
                pl.BlockSpec((k, tn), lambda i, s: (0, i)),
                pl.BlockSpec((k, tn), lambda i, s: (0, i)),
            ],
            out_specs=pl.BlockSpec((k, tn), lambda i, s: (0, i)),
        ),
        compiler_params=pltpu.CompilerParams(vmem_limit_bytes=VMEM_LIMIT),
        name="dest",
    )(start, e_idx, rank)


def _sc_dispatch(h2p, dest3, total_rows):
    n, words = h2p.shape
    nchunks = n // SC_CHUNK
    per_worker = nchunks // (SC_CORES * SC_SUBCORES)
    mesh = plsc.VectorSubcoreMesh(core_axis_name="c", subcore_axis_name="s",
                                  num_cores=SC_CORES, num_subcores=SC_SUBCORES)

    @functools.partial(
        pl.kernel, mesh=mesh,
        out_type=jax.ShapeDtypeStruct((total_rows, words), I32),
        scratch_types=[
            pltpu.VMEM((TOP_K, SC_CHUNK), I32),
            pltpu.VMEM((SC_CHUNK, words), I32),
            pltpu.SemaphoreType.DMA,
        ],
        name="sc_dispatch",
    )
    def run(h_hbm, d_hbm, xs_hbm, idx_v, rows_v, sem):
        wid = lax.axis_index("s") * SC_CORES + lax.axis_index("c")

        @pl.loop(0, per_worker)
        def _(j):
            ch = wid * per_worker + j
            pltpu.sync_copy(d_hbm.at[ch], idx_v)
            pltpu.sync_copy(h_hbm.at[pl.ds(ch * SC_CHUNK, SC_CHUNK)], rows_v)
            copies = [pltpu.async_copy(rows_v, xs_hbm.at[idx_v.at[k]], sem) for k in range(TOP_K)]
            for cp in copies:
                cp.wait()

    return run(h2p, dest3)


def _ffn_kernel(head_ref, exp_ref, valid_ref, first_ref, ahead_ref, slot_ref,
                x_hbm, wg_hbm, wu_hbm, wd_hbm, y_hbm, x_s, y_s, wg_s, wu_s, wd_s, sem_x, sem_y, sem):
    i = pl.program_id(0)
    nreal = head_ref[0]
    rows_per = x_s.shape[1]
    half = x_s.shape[2]
    sizes = tuple(range(FFN_GRAIN, rows_per + 1, FFN_GRAIN))

    def rows_needed(g):
        return (valid_ref[g] + (FFN_GRAIN - 1)) // FFN_GRAIN * FFN_GRAIN

    def by_size(nrows, fn):
        for n in sizes:
            @pl.when(nrows == n)
            def _():
                fn(n)

    def row_copy(g, n):
        slot = lax.rem(g, FFN_LOOKAHEAD + 1)
        return pltpu.make_async_copy(
            x_hbm.at[pl.ds(g * rows_per, n)], x_s.at[slot, pl.ds(0, n)], sem_x.at[slot])

    def out_copy(g, n):
        slot = lax.rem(g, FFN_OUT_SLOTS)
        return pltpu.make_async_copy(
            y_s.at[slot, pl.ds(0, n)], y_hbm.at[pl.ds(g * rows_per, n)], sem_y.at[slot])

    def weight_copies(e, s):
        return (pltpu.make_async_copy(wg_hbm.at[e], wg_s.at[s], sem.at[s, 0]),
                pltpu.make_async_copy(wu_hbm.at[e], wu_s.at[s], sem.at[s, 1]),
                pltpu.make_async_copy(wd_hbm.at[e], wd_s.at[s], sem.at[s, 2]))

    @pl.when(i == 0)
    def _():
        for j in range(FFN_WEIGHT_SLOTS - 1):
            @pl.when(head_ref[1 + j] >= 0)
            def _():
                for cp in weight_copies(head_ref[1 + j], j):
                    cp.start()
        for g in range(FFN_LOOKAHEAD):
            @pl.when(g < nreal)
            def _():
                by_size(rows_needed(g), lambda n: row_copy(g, n).start())

    @pl.when(i < nreal)
    def _():
        s = slot_ref[i]
        fetch = i + FFN_LOOKAHEAD

        @pl.when(fetch < nreal)
        def _():
            by_size(rows_needed(fetch), lambda n: row_copy(fetch, n).start())

        by_size(rows_needed(i), lambda n: row_copy(i, n).wait())

        @pl.when(first_ref[i] == 1)
        def _():
            for cp in weight_copies(exp_ref[i], s):
                cp.wait()

            @pl.when(ahead_ref[i] >= 0)
            def _():
                for cp in weight_copies(ahead_ref[i], lax.rem(s + FFN_WEIGHT_SLOTS - 1, FFN_WEIGHT_SLOTS)):
                    cp.start()

        @pl.when(i >= FFN_OUT_SLOTS)
        def _():
            done = i - FFN_OUT_SLOTS
            by_size(rows_needed(done), lambda n: out_copy(done, n).wait())

        x_slot = lax.rem(i, FFN_LOOKAHEAD + 1)
        y_slot = lax.rem(i, FFN_OUT_SLOTS)
        valid = valid_ref[i]

        def expert_rows(n):
            r = lax.broadcasted_iota(I32, (n, 1), 0)
            x_lo, x_hi = _unpack_halves(jnp.where(r < valid, x_s[x_slot, pl.ds(0, n), :], 0))
            hg = _dot(x_lo, wg_s[s, :half, :]) + _dot(x_hi, wg_s[s, half:, :])
            hu = _dot(x_lo, wu_s[s, :half, :]) + _dot(x_hi, wu_s[s, half:, :])
            y_s[y_slot, pl.ds(0, n), :] = _pack_halves(_dot(_silu(hg) * hu, wd_s[s]))
            out_copy(i, n).start()

        by_size(rows_needed(i), expert_rows)

        @pl.when(i == nreal - 1)
        def _():
            for back in range(FFN_OUT_SLOTS):
                last = i - back

                @pl.when(last >= 0)
                def _():
                    by_size(rows_needed(last), lambda n: out_copy(last, n).wait())


def _ffn(sched, xs, w_gate, w_up, w_down):
    p, half = xs.shape
    d = 2 * half
    ff = w_gate.shape[2]
    steps = sched[1].shape[0]
    return pl.pallas_call(
        _ffn_kernel,
        out_shape=jax.ShapeDtypeStruct((p, half), I32),
        grid_spec=pltpu.PrefetchScalarGridSpec(
            num_scalar_prefetch=len(sched),
            grid=(steps,),
            in_specs=[
                pl.BlockSpec(memory_space=pl.ANY),
                pl.BlockSpec(memory_space=pl.ANY),
                pl.BlockSpec(memory_space=pl.ANY),
                pl.BlockSpec(memory_space=pl.ANY),
            ],
            out_specs=pl.BlockSpec(memory_space=pl.ANY),
            scratch_shapes=[
                pltpu.VMEM((FFN_LOOKAHEAD + 1, FFN_BLOCK, half), I32),
                pltpu.VMEM((FFN_OUT_SLOTS, FFN_BLOCK, half), I32),
                pltpu.VMEM((FFN_WEIGHT_SLOTS, d, ff), F32),
                pltpu.VMEM((FFN_WEIGHT_SLOTS, d, ff), F32),
                pltpu.VMEM((FFN_WEIGHT_SLOTS, ff, d), F32),
                pltpu.SemaphoreType.DMA((FFN_LOOKAHEAD + 1,)),
                pltpu.SemaphoreType.DMA((FFN_OUT_SLOTS,)),
                pltpu.SemaphoreType.DMA((FFN_WEIGHT_SLOTS, 3)),
            ],
        ),
        compiler_params=pltpu.CompilerParams(
            dimension_semantics=("arbitrary",), vmem_limit_bytes=VMEM_LIMIT, has_side_effects=True),
        name="ffn",
    )(*sched, xs, w_gate, w_up, w_down)


def _sc_gather(y, dest3):
    a, words = y.shape
    nchunks, _, chunk = dest3.shape
    n = nchunks * chunk
    per_worker = nchunks // (SC_CORES * SC_SUBCORES)
    nbuf = 3
    items = [(c, k) for c in range(per_worker) for k in range(TOP_K)]
    mesh = plsc.VectorSubcoreMesh(core_axis_name="c", subcore_axis_name="s",
                                  num_cores=SC_CORES, num_subcores=SC_SUBCORES)

    @functools.partial(
        pl.kernel, mesh=mesh,
        out_type=jax.ShapeDtypeStruct((TOP_K, n, words), I32),
        scratch_types=[
            pltpu.VMEM((per_worker, TOP_K, chunk), I32),
            [pltpu.VMEM((chunk, words), I32)] * nbuf,
            pltpu.SemaphoreType.DMA((nbuf,)),
            pltpu.SemaphoreType.DMA((nbuf,)),
        ],
        name="sc_gather",
    )
    def run(y_hbm, d_hbm, yt_hbm, idx_v, bufs, sem_g, sem_w):
        wid = lax.axis_index("s") * SC_CORES + lax.axis_index("c")
        pltpu.sync_copy(d_hbm.at[pl.ds(wid * per_worker, per_worker)], idx_v)

        def gather(m):
            c, k = items[m]
            return pltpu.async_copy(y_hbm.at[idx_v.at[c, k]], bufs[m % nbuf], sem_g.at[m % nbuf])

        def write(m):
            c, k = items[m]
            rows = pl.ds((wid * per_worker + c) * chunk, chunk)
            return pltpu.async_copy(bufs[m % nbuf], yt_hbm.at[k, rows], sem_w.at[m % nbuf])

        gathers = {m: gather(m) for m in range(min(nbuf - 1, len(items)))}
        writes = {}
        for m in range(len(items)):
            gathers.pop(m).wait()
            writes[m] = write(m)
            nxt = m + nbuf - 1
            if nxt < len(items):
                if m >= 1:
                    writes.pop(m - 1).wait()
                gathers[nxt] = gather(nxt)
        for m in sorted(writes):
            writes.pop(m).wait()

    return run(y, dest3)


def _combine_kernel(yt_ref, wt_ref, xb_ref, mod_ref, o_ref):
    half = yt_ref.shape[2]
    wt = wt_ref[...]
    lo, hi = _unpack_halves(yt_ref[0])
    r_lo = lo * wt[:, 0:1]
    r_hi = hi * wt[:, 0:1]
    for k in range(1, TOP_K):
        lo, hi = _unpack_halves(yt_ref[k])
        r_lo = r_lo + lo * wt[:, k:k + 1]
        r_hi = r_hi + hi * wt[:, k:k + 1]
    gate = mod_ref[0][5:6]
    o_ref[:, :half] = xb_ref[:, :half] + gate[:, :half] * r_lo
    o_ref[:, half:] = xb_ref[:, half:] + gate[:, half:] * r_hi


def _combine(yt, w_t, xb, mod3, seq):
    n, d = xb.shape
    tm = TM_COMBINE
    tiles_per_seq = seq // tm
    return pl.pallas_call(
        _combine_kernel,
        out_shape=jax.ShapeDtypeStruct((n, d), F32),
        grid=(n // tm,),
        in_specs=[
            pl.BlockSpec((TOP_K, tm, d // 2), lambda i: (0, i, 0)),
            pl.BlockSpec((tm, TOP_K), lambda i: (i, 0)),
            pl.BlockSpec((tm, d), lambda i: (i, 0)),
            pl.BlockSpec((1, N_MOD, d), lambda i: (i // tiles_per_seq, 0, 0)),
        ],
        out_specs=pl.BlockSpec((tm, d), lambda i: (i, 0)),
        compiler_params=pltpu.CompilerParams(vmem_limit_bytes=VMEM_LIMIT),
        name="combine",
    )(yt, w_t, xb, mod3)


def _rotary_tables(seq):
    half = RET_DK // 2
    pos = jnp.arange(seq, dtype=F32)
    inv = ROPE_BASE ** (-jnp.arange(half, dtype=F32) / half)
    ang = pos[:, None] * inv[None, :]
    cos = jnp.cos(ang)
    sin = jnp.sin(ang)
    return jnp.concatenate([cos, cos], axis=-1), jnp.concatenate([-sin, sin], axis=-1)


def _ffn_schedule(counts, nblk):
    eids = jnp.arange(N_EXPERTS, dtype=I32)
    upto = eids[None, :] <= eids[:, None]

    def running_sum(v):
        return jnp.sum(jnp.where(upto, v[None, :], 0), axis=1)

    padded = (counts + FFN_BLOCK - 1) // FFN_BLOCK * FFN_BLOCK
    ends = running_sum(padded)
    start = ends - padded
    nreal = ends[-1:] // FFN_BLOCK
    blk = jnp.minimum(jnp.arange(nblk, dtype=I32), nreal - 1)
    row0 = blk * FFN_BLOCK
    exp = jnp.sum((ends[None, :] <= row0[:, None]).astype(I32), axis=1)
    mine = exp[:, None] == eids[None, :]

    def per_block(table):
        return jnp.sum(jnp.where(mine, table[None, :], 0), axis=1)

    valid = jnp.clip(per_block(counts + start) - row0, 0, FFN_BLOCK)
    steps = jnp.arange(nblk, dtype=I32)
    first = ((steps < nreal) & ((steps == 0) | (exp != jnp.roll(exp, 1)))).astype(I32)
    used = counts > 0
    ordinal = running_sum(used.astype(I32)) - 1

    def used_at(pos):
        hit = used[None, :] & (ordinal[None, :] == pos[:, None])
        return jnp.sum(jnp.where(hit, eids[None, :] + 1, 0), axis=1) - 1

    ahead = used_at(ordinal + (FFN_WEIGHT_SLOTS - 1))
    head = jnp.concatenate([nreal, used_at(jnp.arange(FFN_WEIGHT_SLOTS - 1, dtype=I32))])
    return start, (head, exp, valid, first, per_block(ahead), per_block(ordinal % FFN_WEIGHT_SLOTS))


def _chunked(dest, chunk):
    k, n = dest.shape
    return dest.reshape(k, n // chunk, chunk).transpose(1, 0, 2)


def kernel(x, c, w_ada, b_ada, g_mix, w_in, q_gain, k_gain, w_out, g_ffn, w_router, router_bias,
           w_gate, w_up, w_down, ws_gate, ws_up, ws_down):
    bsz, seq, d = x.shape
    n = bsz * seq
    depth = w_ada.shape[0]
    cos_full, sin_signed = _rotary_tables(seq)
    log_g = jnp.log1p(-jnp.exp2(-5.0 - jnp.arange(RET_HEADS, dtype=F32)))
    ret_w = RET_HEADS * RET_DK
    x2 = x.reshape(n, d)
    for l in range(depth):
        mod3 = _adaln(c, w_ada[l], b_ada[l]).reshape(bsz, N_MOD, d)
        proj = _inproj(x2, mod3, g_mix[l].reshape(1, d), w_in[l].astype(BF16), cos_full, sin_signed, seq)
        proj3 = proj.reshape(bsz, seq, IN_COLS)
        ret = _retention(log_g, proj3)
        qg2 = jnp.tile(q_gain[l].reshape(1, MOBA_DH), (1, 2))
        kg2 = jnp.tile(k_gain[l].reshape(1, MOBA_DH), (1, 2))
        moba = _moba(proj3, qg2, kg2)
        wo = w_out[l].astype(BF16)
        xb, h2, e_idx, w_k, rank, cnt = _mid(
            ret.reshape(n, ret_w), moba.reshape(n, MOBA_HEADS * MOBA_DH), x2, mod3,
            wo[:ret_w], wo[ret_w:], g_ffn[l].reshape(1, d),
            w_router[l].T.astype(BF16), router_bias[l].reshape(N_EXPERTS, 1),
            ws_gate[l].astype(BF16), ws_up[l].astype(BF16), ws_down[l].astype(BF16), seq)
        nblk = n * TOP_K // FFN_BLOCK + N_EXPERTS
        start, sched = _ffn_schedule(cnt[:, 0], nblk)
        dest = _dest(start, e_idx, rank)
        xs = _sc_dispatch(h2, _chunked(dest, SC_CHUNK), nblk * FFN_BLOCK)
        y = _ffn(sched, xs, w_gate[l], w_up[l], w_down[l])
        yt = _sc_gather(y, _chunked(dest, SC_GATHER_CHUNK))
        x2 = _combine(yt, w_k.T, xb, mod3, seq)
    return x2.reshape(bsz, seq, d)
```

```python
import functools

import numpy as np
import jax
import jax.numpy as jnp
from jax import lax
from jax.experimental import pallas as pl
from jax.experimental.pallas import tpu as pltpu
from jax.experimental.pallas import tpu_sc as plsc

F32 = jnp.float32
BF16 = jnp.bfloat16
I32 = jnp.int32

D_MODEL = 1024
RET_HEADS = 4
RET_DK = 128
MOBA_HEADS = 8
MOBA_DH = 64
MOBA_BLOCK = 256
MOBA_TOPK = 3
ROPE_BASE = 10000.0
N_EXPERTS = 256
TOP_K = 8
N_GROUPS = 8
TOPK_GROUPS = 4
GROUP_SIZE = N_EXPERTS // N_GROUPS
EXPERT_FF = 256
ROUTED_SCALE = 2.5
N_MOD = 6
EPS = 1e-6
IN_COLS = 3584

LANES = 128
RET_CHUNK = 256
TM_PROJ = 512
TM_COMBINE = 512
MOBA_ONES_ROWS = 16
SC_CORES = 2
SC_SUBCORES = 16
SC_CHUNK = 128
SC_GATHER_ROWS = 64
FFN_BLOCK = 512
FFN_GRAIN = 128
FFN_LOOKAHEAD = 5
FFN_WEIGHT_SLOTS = 3
FFN_OUT_SLOTS = 3
VMEM_LIMIT = 56 * 1024 * 1024

NEG_INF = float("-inf")
LOG2_E = 1.4426950408889634


def _silu(x):
    return x * jax.nn.sigmoid(x)


def _nt_dot(a, b):
    return lax.dot_general(a, b, (((1,), (1,)), ((), ())), preferred_element_type=F32)


def _tn_dot(a, b):
    return lax.dot_general(a, b, (((0,), (0,)), ((), ())), preferred_element_type=F32)


def _dot(a, b):
    return jnp.dot(a, b, preferred_element_type=F32)


HI_MASK = -65536


def _pack_halves(v):
    w = v.shape[1] // 2
    lo = lax.bitcast_convert_type(v[:, :w].astype(BF16).astype(F32), I32)
    hi = lax.bitcast_convert_type(v[:, w:].astype(BF16).astype(F32), I32)
    return lax.shift_right_logical(lo, 16) | (hi & HI_MASK)


def _unpack_halves(u):
    lo = lax.bitcast_convert_type(lax.shift_left(u, 16), F32)
    hi = lax.bitcast_convert_type(u & HI_MASK, F32)
    return lo, hi


def _adaln_kernel(c_ref, w_ref, b_ref, o_ref):
    s = _silu(c_ref[...])
    o_ref[...] = _dot(s.astype(BF16), w_ref[...].astype(BF16)) + b_ref[...]


def _adaln(c, w_ada, b_ada):
    bsz, d = c.shape
    ncol = w_ada.shape[1]
    tn = 1024
    return pl.pallas_call(
        _adaln_kernel,
        out_shape=jax.ShapeDtypeStruct((bsz, ncol), F32),
        grid=(ncol // tn,),
        in_specs=[
            pl.BlockSpec((bsz, d), lambda j: (0, 0)),
            pl.BlockSpec((d, tn), lambda j: (0, j)),
            pl.BlockSpec((1, tn), lambda j: (0, j)),
        ],
        out_specs=pl.BlockSpec((bsz, tn), lambda j: (0, j)),
        compiler_params=pltpu.CompilerParams(vmem_limit_bytes=VMEM_LIMIT),
        name="adaln",
    )(c, w_ada, b_ada.reshape(1, ncol))


def _inproj_kernel(x_ref, mod_ref, g_ref, w_ref, cos_ref, sin_ref, o_ref):
    x = x_ref[...]
    ms = jnp.mean(x * x, axis=-1, keepdims=True)
    m = mod_ref[0]
    h = (x * lax.rsqrt(ms + EPS) * g_ref[...]) * (1.0 + m[1:2]) + m[0:1]
    hb = h.astype(BF16)
    cosf = cos_ref[...]
    sinf = sin_ref[...]
    k_scale = RET_DK ** -0.5
    width = RET_HEADS * RET_DK
    for ci in range(IN_COLS // width):
        acc = _dot(hb, w_ref[:, ci * width:(ci + 1) * width])
        if ci < 2:
            for hh in range(RET_HEADS):
                xh = acc[:, hh * RET_DK:(hh + 1) * RET_DK]
                r = xh * cosf + pltpu.roll(xh, RET_DK // 2, axis=1) * sinf
                if ci == 1:
                    r = r * k_scale
                o_ref[:, ci * width + hh * RET_DK:ci * width + (hh + 1) * RET_DK] = r.astype(BF16)
        else:
            o_ref[:, ci * width:(ci + 1) * width] = acc.astype(BF16)


def _inproj(x2, mod3, g_mix, w_in_bf, cos_full, sin_signed, seq):
    n, d = x2.shape
    tm = TM_PROJ
    tiles_per_seq = seq // tm
    return pl.pallas_call(
        _inproj_kernel,
        out_shape=jax.ShapeDtypeStruct((n, IN_COLS), BF16),
        grid=(n // tm,),
        in_specs=[
            pl.BlockSpec((tm, d), lambda i: (i, 0)),
            pl.BlockSpec((1, N_MOD, d), lambda i: (i // tiles_per_seq, 0, 0)),
            pl.BlockSpec((1, d), lambda i: (0, 0)),
            pl.BlockSpec((d, IN_COLS), lambda i: (0, 0)),
            pl.BlockSpec((tm, LANES), lambda i: (i % tiles_per_seq, 0)),
            pl.BlockSpec((tm, LANES), lambda i: (i % tiles_per_seq, 0)),
        ],
        out_specs=pl.BlockSpec((tm, IN_COLS), lambda i: (i, 0)),
        compiler_params=pltpu.CompilerParams(vmem_limit_bytes=VMEM_LIMIT),
        name="inproj",
    )(x2, mod3, g_mix, w_in_bf, cos_full, sin_signed)


def _ret_kernel(lg_ref, q_ref, k_ref, v_ref, g_ref, o_ref):
    seq = q_ref.shape[1]
    c = RET_CHUNK
    lg = lg_ref[pl.program_id(1)]
    row = lax.broadcasted_iota(I32, (c, c), 0)
    col = lax.broadcasted_iota(I32, (c, c), 1)
    diff = (row - col).astype(F32)
    dmask = jnp.where(diff >= 0, jnp.exp(lg * jnp.maximum(diff, 0.0)), 0.0)
    idx = lax.broadcasted_iota(I32, (c, 1), 0).astype(F32)
    q_decay = jnp.exp(lg * (idx + 1.0))
    k_decay = jnp.exp(lg * (c - 1.0 - idx))
    chunk_decay = jnp.exp(jnp.full((1, 1), lg * c, F32))
    state = jnp.zeros((RET_DK, RET_DK), F32)
    for n in range(seq // c):
        rows = slice(n * c, (n + 1) * c)
        qn = q_ref[0, rows, :]
        kn = k_ref[0, rows, :]
        vn = v_ref[0, rows, :]
        scores = _nt_dot(qn, kn) * dmask
        inner = _dot(scores.astype(BF16), vn)
        qs = (qn.astype(F32) * q_decay).astype(BF16)
        cross = _dot(qs, state.astype(BF16))
        o = inner + cross
        o = o * lax.rsqrt(jnp.mean(o * o, axis=-1, keepdims=True) + EPS)
        gn = g_ref[0, rows, :].astype(F32)
        o_ref[0, rows, :] = (_silu(gn) * o).astype(BF16)
        ks = (kn.astype(F32) * k_decay).astype(BF16)
        state = state * chunk_decay + _tn_dot(ks, vn)


def _retention(log_g, proj3):
    bsz, seq, _ = proj3.shape
    blk = (1, seq, RET_DK)
    return pl.pallas_call(
        _ret_kernel,
        out_shape=jax.ShapeDtypeStruct((bsz, seq, RET_HEADS * RET_DK), BF16),
        grid_spec=pltpu.PrefetchScalarGridSpec(
            num_scalar_prefetch=1,
            grid=(bsz, RET_HEADS),
            in_specs=[
                pl.BlockSpec(blk, lambda b, h, lg: (b, 0, h)),
                pl.BlockSpec(blk, lambda b, h, lg: (b, 0, RET_HEADS + h)),
                pl.BlockSpec(blk, lambda b, h, lg: (b, 0, 2 * RET_HEADS + h)),
                pl.BlockSpec(blk, lambda b, h, lg: (b, 0, 3 * RET_HEADS + h)),
            ],
            out_specs=pl.BlockSpec(blk, lambda b, h, lg: (b, 0, h)),
        ),
        compiler_params=pltpu.CompilerParams(vmem_limit_bytes=VMEM_LIMIT),
        name="retention",
    )(log_g, proj3, proj3, proj3, proj3)


def _moba_kernel(q_ref, k_ref, v_ref, qg_ref, kg_ref, o_ref, qt_s, ka_s, kb_s, vta_s, vtb_s):
    seq = q_ref.shape[1]
    lb = MOBA_BLOCK
    nb = seq // lb
    lane = lax.broadcasted_iota(I32, (1, LANES), 1)
    is_a = lane < MOBA_DH

    def head_norm(xf, gain):
        sq = xf * xf
        s_a = jnp.sum(jnp.where(is_a, sq, 0.0), axis=-1, keepdims=True)
        s_b = jnp.sum(jnp.where(is_a, 0.0, sq), axis=-1, keepdims=True)
        inv = jnp.where(is_a, lax.rsqrt(s_a / MOBA_DH + EPS), lax.rsqrt(s_b / MOBA_DH + EPS))
        return xf * inv * gain

    qg = qg_ref[...]
    kg = kg_ref[...]
    k_means = []
    for j in range(nb):
        rows = slice(j * lb, (j + 1) * lb)
        kf = head_norm(k_ref[0, rows, :].astype(F32), kg)
        ka_s[rows, :] = jnp.where(is_a, kf, 0.0).astype(BF16)
        kb_s[rows, :] = jnp.where(is_a, 0.0, kf).astype(BF16)
        k_means.append(jnp.mean(kf, axis=0, keepdims=True))
        qf = head_norm(q_ref[0, rows, :].astype(F32), qg)
        qt_s[:, rows] = (qf * (MOBA_DH ** -0.5 * LOG2_E)).T.astype(BF16)
        vt = v_ref[0, rows, :].astype(F32).T
        ones = jnp.ones((MOBA_ONES_ROWS, lb), F32)
        vta_s[:, rows] = jnp.concatenate([vt[:MOBA_DH], ones], axis=0).astype(BF16)
        vtb_s[:, rows] = jnp.concatenate([vt[MOBA_DH:], ones], axis=0).astype(BF16)
    k_mean = jnp.concatenate(k_means + [jnp.zeros((16 - nb, LANES), F32)], axis=0)
    k_mean_h = (jnp.where(is_a, k_mean, 0.0).astype(BF16), jnp.where(is_a, 0.0, k_mean).astype(BF16))
    k_s = (ka_s, kb_s)
    vt_s = (vta_s, vtb_s)

    r_loc = lax.broadcasted_iota(I32, (lb, lb), 0)
    c_loc = lax.broadcasted_iota(I32, (lb, lb), 1)
    causal = r_loc <= c_loc

    for i in range(nb):
        cols = slice(i * lb, (i + 1) * lb)
        qt = qt_s[:, cols]
        outs = []
        for hx in range(2):
            bias = [None] * i
            if i > MOBA_TOPK:
                gate = _dot(k_mean_h[hx], qt)
                g = [gate[j:j + 1, :] for j in range(i)]
                for j in range(i):
                    rank = jnp.zeros((1, lb), F32)
                    for j2 in range(i):
                        if j2 == j:
                            continue
                        beats = (g[j2] >= g[j]) if j2 < j else (g[j2] > g[j])
                        rank = rank + jnp.where(beats, 1.0, 0.0)
                    bias[j] = jnp.where(rank < float(MOBA_TOPK), 0.0, NEG_INF)
            pieces = []
            for j in range(i + 1):
                s = _dot(k_s[hx][j * lb:(j + 1) * lb, :], qt)
                if j == i:
                    s = jnp.where(causal, s, NEG_INF)
                elif bias[j] is not None:
                    s = s + bias[j]
                pieces.append(s)
            mx = jnp.max(pieces[0], axis=0, keepdims=True)
            for s in pieces[1:]:
                mx = jnp.maximum(mx, jnp.max(s, axis=0, keepdims=True))
            acc = jnp.zeros((MOBA_DH + MOBA_ONES_ROWS, lb), F32)
            for j, s in enumerate(pieces):
                p = jnp.exp2(s - mx).astype(BF16)
                acc = acc + _dot(vt_s[hx][:, j * lb:(j + 1) * lb], p)
            outs.append(acc[:MOBA_DH] / acc[MOBA_DH:MOBA_DH + 1, :])
        o_ref[0, cols, :] = jnp.concatenate(outs, axis=0).T.astype(BF16)


def _moba(proj3, qg2, kg2):
    bsz, seq, _ = proj3.shape
    pairs = MOBA_HEADS // 2
    blk = (1, seq, LANES)
    base = 4 * RET_HEADS
    return pl.pallas_call(
        _moba_kernel,
        out_shape=jax.ShapeDtypeStruct((bsz, seq, MOBA_HEADS * MOBA_DH), BF16),
        grid=(bsz, pairs),
        in_specs=[
            pl.BlockSpec(blk, lambda b, p: (b, 0, base + p)),
            pl.BlockSpec(blk, lambda b, p: (b, 0, base + pairs + p)),
            pl.BlockSpec(blk, lambda b, p: (b, 0, base + 2 * pairs + p)),
            pl.BlockSpec((1, LANES), lambda b, p: (0, 0)),
            pl.BlockSpec((1, LANES), lambda b, p: (0, 0)),
        ],
        out_specs=pl.BlockSpec(blk, lambda b, p: (b, 0, p)),
        scratch_shapes=[
            pltpu.VMEM((LANES, seq), BF16),
            pltpu.VMEM((seq, LANES), BF16),
            pltpu.VMEM((seq, LANES), BF16),
            pltpu.VMEM((MOBA_DH + MOBA_ONES_ROWS, seq), BF16),
            pltpu.VMEM((MOBA_DH + MOBA_ONES_ROWS, seq), BF16),
        ],
        compiler_params=pltpu.CompilerParams(vmem_limit_bytes=VMEM_LIMIT),
        name="moba",
    )(proj3, proj3, proj3, qg2, kg2)


def _mid_kernel(ret_ref, moba_ref, x_ref, mod_ref, wo1_ref, wo2_ref, g_ref, wr_ref, rb_ref,
                wsg_ref, wsu_ref, wsd_ref,
                xb_ref, h2_ref, e_ref, w_ref, rk_ref, cnt_ref, carry_s):
    i = pl.program_id(0)
    tm = x_ref.shape[0]

    @pl.when(i == 0)
    def _():
        carry_s[...] = jnp.zeros_like(carry_s)

    m = mod_ref[0]
    mixed = _dot(ret_ref[...], wo1_ref[...]) + _dot(moba_ref[...], wo2_ref[...])
    x1 = x_ref[...] + m[2:3] * mixed
    ms = jnp.mean(x1 * x1, axis=-1, keepdims=True)
    h2 = (x1 * lax.rsqrt(ms + EPS) * g_ref[...]) * (1.0 + m[4:5]) + m[3:4]
    h2_ref[...] = _pack_halves(h2)
    h2b = h2.astype(BF16)

    hid = _silu(_dot(h2b, wsg_ref[...])) * _dot(h2b, wsu_ref[...])
    xb_ref[...] = x1 + m[5:6] * _dot(hid.astype(BF16), wsd_ref[...])

    scores = jax.nn.sigmoid(_nt_dot(wr_ref[...], h2b))
    biased = scores + rb_ref[...]
    grp = biased.reshape(N_GROUPS, GROUP_SIZE, tm)
    gi = lax.broadcasted_iota(I32, (N_GROUPS, GROUP_SIZE, tm), 1).astype(F32)
    top1 = jnp.max(grp, axis=1, keepdims=True)
    first = jnp.min(jnp.where(grp == top1, gi, float(GROUP_SIZE)), axis=1, keepdims=True)
    top2 = jnp.max(jnp.where(gi == first, NEG_INF, grp), axis=1, keepdims=True)
    gscore = (top1 + top2).reshape(N_GROUPS, tm)
    gidx = lax.broadcasted_iota(I32, (N_GROUPS, tm), 0)
    grank = jnp.zeros((N_GROUPS, tm), F32)
    for g2 in range(N_GROUPS):
        rowv = gscore[g2:g2 + 1, :]
        beats = (rowv > gscore) | ((rowv == gscore) & (g2 < gidx))
        grank = grank + jnp.where(beats, 1.0, 0.0)
    gsel = jnp.where(grank < float(TOPK_GROUPS), 1.0, 0.0)
    emask = jnp.broadcast_to(gsel.reshape(N_GROUPS, 1, tm), (N_GROUPS, GROUP_SIZE, tm)).reshape(N_EXPERTS, tm)
    choice = jnp.where(emask > 0.5, biased, NEG_INF)

    eidx = lax.broadcasted_iota(I32, (N_EXPERTS, tm), 0).astype(F32)
    remaining = choice
    e_rows = []
    for _k in range(TOP_K):
        mx = jnp.max(remaining, axis=0, keepdims=True)
        idx = jnp.min(jnp.where(remaining == mx, eidx, float(N_EXPERTS)), axis=0, keepdims=True)
        e_rows.append(idx)
        remaining = jnp.where(eidx == idx, NEG_INF, remaining)
    selmask = jnp.where(remaining != choice, 1.0, 0.0)

    tr = lax.broadcasted_iota(I32, (tm, tm), 0)
    tc = lax.broadcasted_iota(I32, (tm, tm), 1)
    upper = jnp.where(tr < tc, 1.0, 0.0).astype(BF16)
    prefix = _dot(selmask.astype(BF16), upper) + carry_s[...]
    w_rows = []
    r_rows = []
    for ek in e_rows:
        onehot = eidx == ek
        w_rows.append(jnp.sum(jnp.where(onehot, scores, 0.0), axis=0, keepdims=True))
        r_rows.append(jnp.sum(jnp.where(onehot, prefix, 0.0), axis=0, keepdims=True))
    wsum = w_rows[0]
    for wk in w_rows[1:]:
        wsum = wsum + wk
    carry_s[...] = carry_s[...] + jnp.sum(selmask, axis=1, keepdims=True)

    e_ref[...] = jnp.concatenate(e_rows, axis=0).astype(I32)
    w_ref[...] = jnp.concatenate([wk / wsum * ROUTED_SCALE for wk in w_rows], axis=0)
    rk_ref[...] = jnp.concatenate(r_rows, axis=0).astype(I32)
    cnt_ref[...] = carry_s[...].astype(I32)


def _mid(ret2, moba2, x2, mod3, wo1, wo2, g_ffn, wr_t, rbias, wsg, wsu, wsd, seq):
    n, d = x2.shape
    tm = TM_PROJ
    tiles_per_seq = seq // tm
    half = ret2.shape[1]
    ff = wsg.shape[1]
    const = lambda i: (0, 0)
    row = lambda i: (i, 0)
    colt = lambda i: (0, i)
    return pl.pallas_call(
        _mid_kernel,
        out_shape=(
            jax.ShapeDtypeStruct((n, d), F32),
            jax.ShapeDtypeStruct((n, d // 2), I32),
            jax.ShapeDtypeStruct((TOP_K, n), I32),
            jax.ShapeDtypeStruct((TOP_K, n), F32),
            jax.ShapeDtypeStruct((TOP_K, n), I32),
            jax.ShapeDtypeStruct((N_EXPERTS, 1), I32),
        ),
        grid=(n // tm,),
        in_specs=[
            pl.BlockSpec((tm, half), row),
            pl.BlockSpec((tm, half), row),
            pl.BlockSpec((tm, d), row),
            pl.BlockSpec((1, N_MOD, d), lambda i: (i // tiles_per_seq, 0, 0)),
            pl.BlockSpec((half, d), const),
            pl.BlockSpec((half, d), const),
            pl.BlockSpec((1, d), const),
            pl.BlockSpec((N_EXPERTS, d), const),
            pl.BlockSpec((N_EXPERTS, 1), const),
            pl.BlockSpec((d, ff), const),
            pl.BlockSpec((d, ff), const),
            pl.BlockSpec((ff, d), const),
        ],
        out_specs=(
            pl.BlockSpec((tm, d), row),
            pl.BlockSpec((tm, d // 2), row),
            pl.BlockSpec((TOP_K, tm), colt),
            pl.BlockSpec((TOP_K, tm), colt),
            pl.BlockSpec((TOP_K, tm), colt),
            pl.BlockSpec((N_EXPERTS, 1), const),
        ),
        scratch_shapes=[pltpu.VMEM((N_EXPERTS, 1), F32)],
        compiler_params=pltpu.CompilerParams(
            dimension_semantics=("arbitrary",), vmem_limit_bytes=VMEM_LIMIT),
        name="mid",
    )(ret2, moba2, x2, mod3, wo1, wo2, g_ffn, wr_t, rbias, wsg, wsu, wsd)


def _dest_kernel(start_ref, e_ref, rk_ref, o_ref):
    e = e_ref[...]

    def body(ex, acc):
        return acc + jnp.where(e == ex, start_ref[ex], 0)

    dest = lax.fori_loop(0, N_EXPERTS, body, rk_ref[...], unroll=8)
    for ch in range(o_ref.shape[0]):
        o_ref[ch] = dest[:, ch * SC_CHUNK:(ch + 1) * SC_CHUNK]


def _dest(start, e_idx, rank):
    k, n = e_idx.shape
    tn = 2048
    return pl.pallas_call(
        _dest_kernel,
        out_shape=jax.ShapeDtypeStruct((n // SC_CHUNK, k, SC_CHUNK), I32),
        grid_spec=pltpu.PrefetchScalarGridSpec(
            num_scalar_prefetch=1,
            grid=(n // tn,),
            in_specs=[
                pl.BlockSpec((k, tn), lambda i, s: (0, i)),
                pl.BlockSpec((k, tn), lambda i, s: (0, i)),
            ],
            out_specs=pl.BlockSpec((tn // SC_CHUNK, k, SC_CHUNK), lambda i, s: (i, 0, 0)),
        ),
        compiler_params=pltpu.CompilerParams(vmem_limit_bytes=VMEM_LIMIT),
        name="dest",
    )(start, e_idx, rank)


def _sc_dispatch(h2p, dest3, total_rows):
    n, words = h2p.shape
    nchunks = n // SC_CHUNK
    per_worker = nchunks // (SC_CORES * SC_SUBCORES)
    mesh = plsc.VectorSubcoreMesh(core_axis_name="c", subcore_axis_name="s",
                                  num_cores=SC_CORES, num_subcores=SC_SUBCORES)

    @functools.partial(
        pl.kernel, mesh=mesh,
        out_type=jax.ShapeDtypeStruct((total_rows, words), I32),
        scratch_types=[
            pltpu.VMEM((TOP_K, SC_CHUNK), I32),
            pltpu.VMEM((SC_CHUNK, words), I32),
            pltpu.SemaphoreType.DMA,
        ],
        name="sc_dispatch",
    )
    def run(h_hbm, d_hbm, xs_hbm, idx_v, rows_v, sem):
        wid = lax.axis_index("s") * SC_CORES + lax.axis_index("c")

        @pl.loop(0, per_worker)
        def _(j):
            ch = wid * per_worker + j
            pltpu.sync_copy(d_hbm.at[ch], idx_v)
            pltpu.sync_copy(h_hbm.at[pl.ds(ch * SC_CHUNK, SC_CHUNK)], rows_v)
            copies = [pltpu.async_copy(rows_v, xs_hbm.at[idx_v.at[k]], sem) for k in range(TOP_K)]
            for cp in copies:
                cp.wait()

    return run(h2p, dest3)


def _ffn_kernel(head_ref, exp_ref, valid_ref, first_ref, ahead_ref, slot_ref,
                x_hbm, wg_hbm, wu_hbm, wd_hbm, y_hbm, x_s, y_s, wg_s, wu_s, wd_s, sem_x, sem_y, sem):
    i = pl.program_id(0)
    nreal = head_ref[0]
    rows_per = x_s.shape[1]
    half = x_s.shape[2]
    sizes = tuple(range(FFN_GRAIN, rows_per + 1, FFN_GRAIN))

    def rows_needed(g):
        return (valid_ref[g] + (FFN_GRAIN - 1)) // FFN_GRAIN * FFN_GRAIN

    def by_size(nrows, fn):
        for n in sizes:
            @pl.when(nrows == n)
            def _():
                fn(n)

    def row_copy(g, n):
        slot = lax.rem(g, FFN_LOOKAHEAD + 1)
        return pltpu.make_async_copy(
            x_hbm.at[pl.ds(g * rows_per, n)], x_s.at[slot, pl.ds(0, n)], sem_x.at[slot])

    def out_copy(g, n):
        slot = lax.rem(g, FFN_OUT_SLOTS)
        return pltpu.make_async_copy(
            y_s.at[slot, pl.ds(0, n)], y_hbm.at[pl.ds(g * rows_per, n)], sem_y.at[slot])

    def weight_copies(e, s):
        return (pltpu.make_async_copy(wg_hbm.at[e], wg_s.at[s], sem.at[s, 0]),
                pltpu.make_async_copy(wu_hbm.at[e], wu_s.at[s], sem.at[s, 1]),
                pltpu.make_async_copy(wd_hbm.at[e], wd_s.at[s], sem.at[s, 2]))

    @pl.when(i == 0)
    def _():
        for j in range(FFN_WEIGHT_SLOTS - 1):
            @pl.when(head_ref[1 + j] >= 0)
            def _():
                for cp in weight_copies(head_ref[1 + j], j):
                    cp.start()
        for g in range(FFN_LOOKAHEAD):
            @pl.when(g < nreal)
            def _():
                by_size(rows_needed(g), lambda n: row_copy(g, n).start())

    @pl.when(i < nreal)
    def _():
        s = slot_ref[i]
        fetch = i + FFN_LOOKAHEAD

        @pl.when(fetch < nreal)
        def _():
            by_size(rows_needed(fetch), lambda n: row_copy(fetch, n).start())

        by_size(rows_needed(i), lambda n: row_copy(i, n).wait())

        @pl.when(first_ref[i] == 1)
        def _():
            for cp in weight_copies(exp_ref[i], s):
                cp.wait()

            @pl.when(ahead_ref[i] >= 0)
            def _():
                for cp in weight_copies(ahead_ref[i], lax.rem(s + FFN_WEIGHT_SLOTS - 1, FFN_WEIGHT_SLOTS)):
                    cp.start()

        @pl.when(i >= FFN_OUT_SLOTS)
        def _():
            done = i - FFN_OUT_SLOTS
            by_size(rows_needed(done), lambda n: out_copy(done, n).wait())

        x_slot = lax.rem(i, FFN_LOOKAHEAD + 1)
        y_slot = lax.rem(i, FFN_OUT_SLOTS)
        valid = valid_ref[i]

        def expert_rows(n):
            r = lax.broadcasted_iota(I32, (n, 1), 0)
            x_lo, x_hi = _unpack_halves(jnp.where(r < valid, x_s[x_slot, pl.ds(0, n), :], 0))
            hg = _dot(x_lo, wg_s[s, :half, :]) + _dot(x_hi, wg_s[s, half:, :])
            hu = _dot(x_lo, wu_s[s, :half, :]) + _dot(x_hi, wu_s[s, half:, :])
            y_s[y_slot, pl.ds(0, n), :] = _pack_halves(_dot(_silu(hg) * hu, wd_s[s]))
            out_copy(i, n).start()

        by_size(rows_needed(i), expert_rows)

        @pl.when(i == nreal - 1)
        def _():
            for back in range(FFN_OUT_SLOTS):
                last = i - back

                @pl.when(last >= 0)
                def _():
                    by_size(rows_needed(last), lambda n: out_copy(last, n).wait())


def _ffn(sched, xs, w_gate, w_up, w_down):
    p, half = xs.shape
    d = 2 * half
    ff = w_gate.shape[2]
    steps = sched[1].shape[0]
    return pl.pallas_call(
        _ffn_kernel,
        out_shape=jax.ShapeDtypeStruct((p, half), I32),
        grid_spec=pltpu.PrefetchScalarGridSpec(
            num_scalar_prefetch=len(sched),
            grid=(steps,),
            in_specs=[
                pl.BlockSpec(memory_space=pl.ANY),
                pl.BlockSpec(memory_space=pl.ANY),
                pl.BlockSpec(memory_space=pl.ANY),
                pl.BlockSpec(memory_space=pl.ANY),
            ],
            out_specs=pl.BlockSpec(memory_space=pl.ANY),
            scratch_shapes=[
                pltpu.VMEM((FFN_LOOKAHEAD + 1, FFN_BLOCK, half), I32),
                pltpu.VMEM((FFN_OUT_SLOTS, FFN_BLOCK, half), I32),
                pltpu.VMEM((FFN_WEIGHT_SLOTS, d, ff), F32),
                pltpu.VMEM((FFN_WEIGHT_SLOTS, d, ff), F32),
                pltpu.VMEM((FFN_WEIGHT_SLOTS, ff, d), F32),
                pltpu.SemaphoreType.DMA((FFN_LOOKAHEAD + 1,)),
                pltpu.SemaphoreType.DMA((FFN_OUT_SLOTS,)),
                pltpu.SemaphoreType.DMA((FFN_WEIGHT_SLOTS, 3)),
            ],
        ),
        compiler_params=pltpu.CompilerParams(
            dimension_semantics=("arbitrary",), vmem_limit_bytes=VMEM_LIMIT, has_side_effects=True),
        name="ffn",
    )(*sched, xs, w_gate, w_up, w_down)


def _sc_gather(y, dest3):
    a, words = y.shape
    nchunks, _, chunk = dest3.shape
    n = nchunks * chunk
    per_worker = nchunks // (SC_CORES * SC_SUBCORES)
    nbuf = 3
    parts = chunk // SC_GATHER_ROWS
    items = [(c, k, h) for c in range(per_worker) for k in range(TOP_K) for h in range(parts)]
    mesh = plsc.VectorSubcoreMesh(core_axis_name="c", subcore_axis_name="s",
                                  num_cores=SC_CORES, num_subcores=SC_SUBCORES)

    @functools.partial(
        pl.kernel, mesh=mesh,
        out_type=jax.ShapeDtypeStruct((TOP_K, n, words), I32),
        scratch_types=[
            pltpu.VMEM((per_worker, TOP_K, chunk), I32),
            [pltpu.VMEM((SC_GATHER_ROWS, words), I32)] * nbuf,
            pltpu.SemaphoreType.DMA((nbuf,)),
            pltpu.SemaphoreType.DMA((nbuf,)),
        ],
        name="sc_gather",
    )
    def run(y_hbm, d_hbm, yt_hbm, idx_v, bufs, sem_g, sem_w):
        wid = lax.axis_index("s") * SC_CORES + lax.axis_index("c")
        pltpu.sync_copy(d_hbm.at[pl.ds(wid * per_worker, per_worker)], idx_v)

        def gather(m):
            c, k, h = items[m]
            idx = idx_v.at[c, k, pl.ds(h * SC_GATHER_ROWS, SC_GATHER_ROWS)]
            return pltpu.async_copy(y_hbm.at[idx], bufs[m % nbuf], sem_g.at[m % nbuf])

        def write(m):
            c, k, h = items[m]
            rows = pl.ds((wid * per_worker + c) * chunk + h * SC_GATHER_ROWS, SC_GATHER_ROWS)
            return pltpu.async_copy(bufs[m % nbuf], yt_hbm.at[k, rows], sem_w.at[m % nbuf])

        gathers = {m: gather(m) for m in range(min(nbuf - 1, len(items)))}
        writes = {}
        for m in range(len(items)):
            gathers.pop(m).wait()
            writes[m] = write(m)
            nxt = m + nbuf - 1
            if nxt < len(items):
                if m >= 1:
                    writes.pop(m - 1).wait()
                gathers[nxt] = gather(nxt)
        for m in sorted(writes):
            writes.pop(m).wait()

    return run(y, dest3)


def _combine_kernel(yt_ref, wt_ref, xb_ref, mod_ref, o_ref):
    half = yt_ref.shape[2]
    wt = wt_ref[...]
    lo, hi = _unpack_halves(yt_ref[0])
    r_lo = lo * wt[:, 0:1]
    r_hi = hi * wt[:, 0:1]
    for k in range(1, TOP_K):
        lo, hi = _unpack_halves(yt_ref[k])
        r_lo = r_lo + lo * wt[:, k:k + 1]
        r_hi = r_hi + hi * wt[:, k:k + 1]
    gate = mod_ref[0][5:6]
    o_ref[:, :half] = xb_ref[:, :half] + gate[:, :half] * r_lo
    o_ref[:, half:] = xb_ref[:, half:] + gate[:, half:] * r_hi


def _combine(yt, w_t, xb, mod3, seq):
    n, d = xb.shape
    tm = TM_COMBINE
    tiles_per_seq = seq // tm
    return pl.pallas_call(
        _combine_kernel,
        out_shape=jax.ShapeDtypeStruct((n, d), F32),
        grid=(n // tm,),
        in_specs=[
            pl.BlockSpec((TOP_K, tm, d // 2), lambda i: (0, i, 0)),
            pl.BlockSpec((tm, TOP_K), lambda i: (i, 0)),
            pl.BlockSpec((tm, d), lambda i: (i, 0)),
            pl.BlockSpec((1, N_MOD, d), lambda i: (i // tiles_per_seq, 0, 0)),
        ],
        out_specs=pl.BlockSpec((tm, d), lambda i: (i, 0)),
        compiler_params=pltpu.CompilerParams(vmem_limit_bytes=VMEM_LIMIT),
        name="combine",
    )(yt, w_t, xb, mod3)


def _rotary_tables(seq):
    half = RET_DK // 2
    inv = ROPE_BASE ** (-np.arange(half, dtype=np.float64) / half)
    ang = np.arange(seq, dtype=np.float64)[:, None] * inv[None, :]
    cos = np.cos(ang).astype(np.float32)
    sin = np.sin(ang).astype(np.float32)
    return (jnp.asarray(np.concatenate([cos, cos], axis=-1)),
            jnp.asarray(np.concatenate([-sin, sin], axis=-1)))


def _ffn_schedule(counts, nblk):
    eids = jnp.arange(N_EXPERTS, dtype=I32)
    upto = eids[None, :] <= eids[:, None]

    def running_sum(v):
        return jnp.sum(jnp.where(upto, v[None, :], 0), axis=1)

    padded = (counts + FFN_BLOCK - 1) // FFN_BLOCK * FFN_BLOCK
    ends = running_sum(padded)
    start = ends - padded
    nreal = ends[-1:] // FFN_BLOCK
    blk = jnp.minimum(jnp.arange(nblk, dtype=I32), nreal - 1)
    row0 = blk * FFN_BLOCK
    exp = jnp.sum((ends[None, :] <= row0[:, None]).astype(I32), axis=1)
    mine = exp[:, None] == eids[None, :]

    def per_block(table):
        return jnp.sum(jnp.where(mine, table[None, :], 0), axis=1)

    valid = jnp.clip(per_block(counts + start) - row0, 0, FFN_BLOCK)
    steps = jnp.arange(nblk, dtype=I32)
    first = ((steps < nreal) & ((steps == 0) | (exp != jnp.roll(exp, 1)))).astype(I32)
    used = counts > 0
    ordinal = running_sum(used.astype(I32)) - 1

    def used_at(pos):
        hit = used[None, :] & (ordinal[None, :] == pos[:, None])
        return jnp.sum(jnp.where(hit, eids[None, :] + 1, 0), axis=1) - 1

    ahead = used_at(ordinal + (FFN_WEIGHT_SLOTS - 1))
    head = jnp.concatenate([nreal, used_at(jnp.arange(FFN_WEIGHT_SLOTS - 1, dtype=I32))])
    return start, (head, exp, valid, first, per_block(ahead), per_block(ordinal % FFN_WEIGHT_SLOTS))


def kernel(x, c, w_ada, b_ada, g_mix, w_in, q_gain, k_gain, w_out, g_ffn, w_router, router_bias,
           w_gate, w_up, w_down, ws_gate, ws_up, ws_down):
    bsz, seq, d = x.shape
    n = bsz * seq
    depth = w_ada.shape[0]
    cos_full, sin_signed = _rotary_tables(seq)
    log_g = jnp.asarray(np.log1p(-np.exp2(-5.0 - np.arange(RET_HEADS, dtype=np.float64))).astype(np.float32))
    ret_w = RET_HEADS * RET_DK
    x2 = x.reshape(n, d)
    for l in range(depth):
        mod3 = _adaln(c, w_ada[l], b_ada[l]).reshape(bsz, N_MOD, d)
        proj = _inproj(x2, mod3, g_mix[l].reshape(1, d), w_in[l].astype(BF16), cos_full, sin_signed, seq)
        proj3 = proj.reshape(bsz, seq, IN_COLS)
        ret = _retention(log_g, proj3)
        qg2 = jnp.tile(q_gain[l].reshape(1, MOBA_DH), (1, 2))
        kg2 = jnp.tile(k_gain[l].reshape(1, MOBA_DH), (1, 2))
        moba = _moba(proj3, qg2, kg2)
        wo = w_out[l].astype(BF16)
        xb, h2, e_idx, w_k, rank, cnt = _mid(
            ret.reshape(n, ret_w), moba.reshape(n, MOBA_HEADS * MOBA_DH), x2, mod3,
            wo[:ret_w], wo[ret_w:], g_ffn[l].reshape(1, d),
            w_router[l].T.astype(BF16), router_bias[l].reshape(N_EXPERTS, 1),
            ws_gate[l].astype(BF16), ws_up[l].astype(BF16), ws_down[l].astype(BF16), seq)
        nblk = n * TOP_K // FFN_BLOCK + N_EXPERTS
        start, sched = _ffn_schedule(cnt[:, 0], nblk)
        dest3 = _dest(start, e_idx, rank)
        xs = _sc_dispatch(h2, dest3, nblk * FFN_BLOCK)
        y = _ffn(sched, xs, w_gate[l], w_up[l], w_down[l])
        yt = _sc_gather(y, dest3)
        x2 = _combine(yt, w_k.T, xb, mod3, seq)
    return x2.reshape(bsz, seq, d)
```

```python
import functools

import numpy as np
import jax
import jax.numpy as jnp
from jax import lax
from jax.experimental import pallas as pl
from jax.experimental.pallas import tpu as pltpu
from jax.experimental.pallas import tpu_sc as plsc

F32 = jnp.float32
BF16 = jnp.bfloat16
I32 = jnp.int32

D_MODEL = 1024
RET_HEADS = 4
RET_DK = 128
MOBA_HEADS = 8
MOBA_DH = 64
MOBA_BLOCK = 256
MOBA_TOPK = 3
ROPE_BASE = 10000.0
N_EXPERTS = 256
TOP_K = 8
N_GROUPS = 8
TOPK_GROUPS = 4
GROUP_SIZE = N_EXPERTS // N_GROUPS
EXPERT_FF = 256
ROUTED_SCALE = 2.5
N_MOD = 6
EPS = 1e-6
IN_COLS = 3584

LANES = 128
RET_CHUNK = 256
TM_PROJ = 512
TM_COMBINE = 512
MOBA_ONES_ROWS = 16
SC_CORES = 2
SC_SUBCORES = 16
SC_CHUNK = 128
SC_GATHER_ROWS = 64
FFN_BLOCK = 512
FFN_GRAIN = 128
FFN_LOOKAHEAD = 5
FFN_WEIGHT_SLOTS = 3
FFN_OUT_SLOTS = 3
VMEM_LIMIT = 56 * 1024 * 1024

NEG_INF = float("-inf")
LOG2_E = 1.4426950408889634


def _silu(x):
    return x * jax.nn.sigmoid(x)


def _nt_dot(a, b):
    return lax.dot_general(a, b, (((1,), (1,)), ((), ())), preferred_element_type=F32)


def _tn_dot(a, b):
    return lax.dot_general(a, b, (((0,), (0,)), ((), ())), preferred_element_type=F32)


def _dot(a, b):
    return jnp.dot(a, b, preferred_element_type=F32)


HI_MASK = -65536


def _pack_halves(v):
    w = v.shape[1] // 2
    lo = lax.bitcast_convert_type(v[:, :w].astype(BF16).astype(F32), I32)
    hi = lax.bitcast_convert_type(v[:, w:].astype(BF16).astype(F32), I32)
    return lax.shift_right_logical(lo, 16) | (hi & HI_MASK)


def _unpack_halves(u):
    lo = lax.bitcast_convert_type(lax.shift_left(u, 16), F32)
    hi = lax.bitcast_convert_type(u & HI_MASK, F32)
    return lo, hi


def _adaln_kernel(c_ref, w_ref, b_ref, o_ref):
    s = _silu(c_ref[...])
    o_ref[...] = _dot(s.astype(BF16), w_ref[...].astype(BF16)) + b_ref[...]


def _adaln(c, w_ada, b_ada):
    bsz, d = c.shape
    ncol = w_ada.shape[1]
    tn = 1024
    return pl.pallas_call(
        _adaln_kernel,
        out_shape=jax.ShapeDtypeStruct((bsz, ncol), F32),
        grid=(ncol // tn,),
        in_specs=[
            pl.BlockSpec((bsz, d), lambda j: (0, 0)),
            pl.BlockSpec((d, tn), lambda j: (0, j)),
            pl.BlockSpec((1, tn), lambda j: (0, j)),
        ],
        out_specs=pl.BlockSpec((bsz, tn), lambda j: (0, j)),
        compiler_params=pltpu.CompilerParams(vmem_limit_bytes=VMEM_LIMIT),
        name="adaln",
    )(c, w_ada, b_ada.reshape(1, ncol))


def _inproj_kernel(x_ref, mod_ref, g_ref, w_ref, cos_ref, sin_ref, o_ref):
    x = x_ref[...]
    ms = jnp.mean(x * x, axis=-1, keepdims=True)
    m = mod_ref[0]
    h = (x * lax.rsqrt(ms + EPS) * g_ref[...]) * (1.0 + m[1:2]) + m[0:1]
    hb = h.astype(BF16)
    cosf = cos_ref[...]
    sinf = sin_ref[...]
    k_scale = RET_DK ** -0.5
    width = RET_HEADS * RET_DK
    for ci in range(IN_COLS // width):
        acc = _dot(hb, w_ref[:, ci * width:(ci + 1) * width])
        if ci < 2:
            for hh in range(RET_HEADS):
                xh = acc[:, hh * RET_DK:(hh + 1) * RET_DK]
                r = xh * cosf + pltpu.roll(xh, RET_DK // 2, axis=1) * sinf
                if ci == 1:
                    r = r * k_scale
                o_ref[:, ci * width + hh * RET_DK:ci * width + (hh + 1) * RET_DK] = r.astype(BF16)
        else:
            o_ref[:, ci * width:(ci + 1) * width] = acc.astype(BF16)


def _inproj(x2, mod3, g_mix, w_in_bf, cos_full, sin_signed, seq):
    n, d = x2.shape
    tm = TM_PROJ
    tiles_per_seq = seq // tm
    return pl.pallas_call(
        _inproj_kernel,
        out_shape=jax.ShapeDtypeStruct((n, IN_COLS), BF16),
        grid=(n // tm,),
        in_specs=[
            pl.BlockSpec((tm, d), lambda i: (i, 0)),
            pl.BlockSpec((1, N_MOD, d), lambda i: (i // tiles_per_seq, 0, 0)),
            pl.BlockSpec((1, d), lambda i: (0, 0)),
            pl.BlockSpec((d, IN_COLS), lambda i: (0, 0)),
            pl.BlockSpec((tm, LANES), lambda i: (i % tiles_per_seq, 0)),
            pl.BlockSpec((tm, LANES), lambda i: (i % tiles_per_seq, 0)),
        ],
        out_specs=pl.BlockSpec((tm, IN_COLS), lambda i: (i, 0)),
        compiler_params=pltpu.CompilerParams(vmem_limit_bytes=VMEM_LIMIT),
        name="inproj",
    )(x2, mod3, g_mix, w_in_bf, cos_full, sin_signed)


def _ret_kernel(lg_ref, q_ref, k_ref, v_ref, g_ref, o_ref):
    seq = q_ref.shape[1]
    c = RET_CHUNK
    lg = lg_ref[pl.program_id(1)]
    row = lax.broadcasted_iota(I32, (c, c), 0)
    col = lax.broadcasted_iota(I32, (c, c), 1)
    diff = (row - col).astype(F32)
    dmask = jnp.where(diff >= 0, jnp.exp(lg * jnp.maximum(diff, 0.0)), 0.0)
    idx = lax.broadcasted_iota(I32, (c, 1), 0).astype(F32)
    q_decay = jnp.exp(lg * (idx + 1.0))
    k_decay = jnp.exp(lg * (c - 1.0 - idx))
    chunk_decay = jnp.exp(jnp.full((1, 1), lg * c, F32))
    state = jnp.zeros((RET_DK, RET_DK), F32)
    for n in range(seq // c):
        rows = slice(n * c, (n + 1) * c)
        qn = q_ref[0, rows, :]
        kn = k_ref[0, rows, :]
        vn = v_ref[0, rows, :]
        scores = _nt_dot(qn, kn) * dmask
        inner = _dot(scores.astype(BF16), vn)
        qs = (qn.astype(F32) * q_decay).astype(BF16)
        cross = _dot(qs, state.astype(BF16))
        o = inner + cross
        o = o * lax.rsqrt(jnp.mean(o * o, axis=-1, keepdims=True) + EPS)
        gn = g_ref[0, rows, :].astype(F32)
        o_ref[0, rows, :] = (_silu(gn) * o).astype(BF16)
        ks = (kn.astype(F32) * k_decay).astype(BF16)
        state = state * chunk_decay + _tn_dot(ks, vn)


def _retention(log_g, proj3):
    bsz, seq, _ = proj3.shape
    blk = (1, seq, RET_DK)
    return pl.pallas_call(
        _ret_kernel,
        out_shape=jax.ShapeDtypeStruct((bsz, seq, RET_HEADS * RET_DK), BF16),
        grid_spec=pltpu.PrefetchScalarGridSpec(
            num_scalar_prefetch=1,
            grid=(bsz, RET_HEADS),
            in_specs=[
                pl.BlockSpec(blk, lambda b, h, lg: (b, 0, h)),
                pl.BlockSpec(blk, lambda b, h, lg: (b, 0, RET_HEADS + h)),
                pl.BlockSpec(blk, lambda b, h, lg: (b, 0, 2 * RET_HEADS + h)),
                pl.BlockSpec(blk, lambda b, h, lg: (b, 0, 3 * RET_HEADS + h)),
            ],
            out_specs=pl.BlockSpec(blk, lambda b, h, lg: (b, 0, h)),
        ),
        compiler_params=pltpu.CompilerParams(vmem_limit_bytes=VMEM_LIMIT),
        name="retention",
    )(log_g, proj3, proj3, proj3, proj3)


def _moba_kernel(q_ref, k_ref, v_ref, qg_ref, kg_ref, o_ref, qt_s, ka_s, kb_s, vta_s, vtb_s):
    seq = q_ref.shape[1]
    lb = MOBA_BLOCK
    nb = seq // lb
    lane = lax.broadcasted_iota(I32, (1, LANES), 1)
    is_a = lane < MOBA_DH

    def head_norm(xf, gain):
        sq = xf * xf
        s_a = jnp.sum(jnp.where(is_a, sq, 0.0), axis=-1, keepdims=True)
        s_b = jnp.sum(jnp.where(is_a, 0.0, sq), axis=-1, keepdims=True)
        inv = jnp.where(is_a, lax.rsqrt(s_a / MOBA_DH + EPS), lax.rsqrt(s_b / MOBA_DH + EPS))
        return xf * inv * gain

    qg = qg_ref[...]
    kg = kg_ref[...]
    k_means = []
    for j in range(nb):
        rows = slice(j * lb, (j + 1) * lb)
        kf = head_norm(k_ref[0, rows, :].astype(F32), kg)
        ka_s[rows, :] = jnp.where(is_a, kf, 0.0).astype(BF16)
        kb_s[rows, :] = jnp.where(is_a, 0.0, kf).astype(BF16)
        k_means.append(jnp.mean(kf, axis=0, keepdims=True))
        qf = head_norm(q_ref[0, rows, :].astype(F32), qg)
        qt_s[:, rows] = (qf * (MOBA_DH ** -0.5 * LOG2_E)).T.astype(BF16)
        vt = v_ref[0, rows, :].astype(F32).T
        ones = jnp.ones((MOBA_ONES_ROWS, lb), F32)
        vta_s[:, rows] = jnp.concatenate([vt[:MOBA_DH], ones], axis=0).astype(BF16)
        vtb_s[:, rows] = jnp.concatenate([vt[MOBA_DH:], ones], axis=0).astype(BF16)
    k_mean = jnp.concatenate(k_means + [jnp.zeros((16 - nb, LANES), F32)], axis=0)
    k_mean_h = (jnp.where(is_a, k_mean, 0.0).astype(BF16), jnp.where(is_a, 0.0, k_mean).astype(BF16))
    k_s = (ka_s, kb_s)
    vt_s = (vta_s, vtb_s)

    r_loc = lax.broadcasted_iota(I32, (lb, lb), 0)
    c_loc = lax.broadcasted_iota(I32, (lb, lb), 1)
    causal = r_loc <= c_loc

    for i in range(nb):
        cols = slice(i * lb, (i + 1) * lb)
        qt = qt_s[:, cols]
        outs = []
        for hx in range(2):
            bias = [None] * i
            if i > MOBA_TOPK:
                gate = _dot(k_mean_h[hx], qt)
                g = [gate[j:j + 1, :] for j in range(i)]
                for j in range(i):
                    rank = jnp.zeros((1, lb), F32)
                    for j2 in range(i):
                        if j2 == j:
                            continue
                        beats = (g[j2] >= g[j]) if j2 < j else (g[j2] > g[j])
                        rank = rank + jnp.where(beats, 1.0, 0.0)
                    bias[j] = jnp.where(rank < float(MOBA_TOPK), 0.0, NEG_INF)
            pieces = []
            for j in range(i + 1):
                s = _dot(k_s[hx][j * lb:(j + 1) * lb, :], qt)
                if j == i:
                    s = jnp.where(causal, s, NEG_INF)
                elif bias[j] is not None:
                    s = s + bias[j]
                pieces.append(s)
            mx = jnp.max(pieces[0], axis=0, keepdims=True)
            for s in pieces[1:]:
                mx = jnp.maximum(mx, jnp.max(s, axis=0, keepdims=True))
            acc = jnp.zeros((MOBA_DH + MOBA_ONES_ROWS, lb), F32)
            for j, s in enumerate(pieces):
                p = jnp.exp2(s - mx).astype(BF16)
                acc = acc + _dot(vt_s[hx][:, j * lb:(j + 1) * lb], p)
            outs.append(acc[:MOBA_DH] / acc[MOBA_DH:MOBA_DH + 1, :])
        o_ref[0, cols, :] = jnp.concatenate(outs, axis=0).T.astype(BF16)


def _moba(proj3, qg2, kg2):
    bsz, seq, _ = proj3.shape
    pairs = MOBA_HEADS // 2
    blk = (1, seq, LANES)
    base = 4 * RET_HEADS
    return pl.pallas_call(
        _moba_kernel,
        out_shape=jax.ShapeDtypeStruct((bsz, seq, MOBA_HEADS * MOBA_DH), BF16),
        grid=(bsz, pairs),
        in_specs=[
            pl.BlockSpec(blk, lambda b, p: (b, 0, base + p)),
            pl.BlockSpec(blk, lambda b, p: (b, 0, base + pairs + p)),
            pl.BlockSpec(blk, lambda b, p: (b, 0, base + 2 * pairs + p)),
            pl.BlockSpec((1, LANES), lambda b, p: (0, 0)),
            pl.BlockSpec((1, LANES), lambda b, p: (0, 0)),
        ],
        out_specs=pl.BlockSpec(blk, lambda b, p: (b, 0, p)),
        scratch_shapes=[
            pltpu.VMEM((LANES, seq), BF16),
            pltpu.VMEM((seq, LANES), BF16),
            pltpu.VMEM((seq, LANES), BF16),
            pltpu.VMEM((MOBA_DH + MOBA_ONES_ROWS, seq), BF16),
            pltpu.VMEM((MOBA_DH + MOBA_ONES_ROWS, seq), BF16),
        ],
        compiler_params=pltpu.CompilerParams(vmem_limit_bytes=VMEM_LIMIT),
        name="moba",
    )(proj3, proj3, proj3, qg2, kg2)


def _mid_kernel(ret_ref, moba_ref, x_ref, mod_ref, wo1_ref, wo2_ref, g_ref, wr_ref, rb_ref,
                wsg_ref, wsu_ref, wsd_ref,
                xb_ref, h2_ref, e_ref, w_ref, rk_ref, cnt_ref, carry_s):
    i = pl.program_id(0)
    tm = x_ref.shape[0]

    @pl.when(i == 0)
    def _():
        carry_s[...] = jnp.zeros_like(carry_s)

    m = mod_ref[0]
    mixed = _dot(ret_ref[...], wo1_ref[...]) + _dot(moba_ref[...], wo2_ref[...])
    x1 = x_ref[...] + m[2:3] * mixed
    ms = jnp.mean(x1 * x1, axis=-1, keepdims=True)
    h2 = (x1 * lax.rsqrt(ms + EPS) * g_ref[...]) * (1.0 + m[4:5]) + m[3:4]
    h2_ref[...] = _pack_halves(h2)
    h2b = h2.astype(BF16)

    hid = _silu(_dot(h2b, wsg_ref[...])) * _dot(h2b, wsu_ref[...])
    xb_ref[...] = x1 + m[5:6] * _dot(hid.astype(BF16), wsd_ref[...])

    scores = jax.nn.sigmoid(_nt_dot(wr_ref[...], h2b))
    biased = scores + rb_ref[...]
    grp = biased.reshape(N_GROUPS, GROUP_SIZE, tm)
    gi = lax.broadcasted_iota(I32, (N_GROUPS, GROUP_SIZE, tm), 1).astype(F32)
    top1 = jnp.max(grp, axis=1, keepdims=True)
    first = jnp.min(jnp.where(grp == top1, gi, float(GROUP_SIZE)), axis=1, keepdims=True)
    top2 = jnp.max(jnp.where(gi == first, NEG_INF, grp), axis=1, keepdims=True)
    gscore = (top1 + top2).reshape(N_GROUPS, tm)
    gidx = lax.broadcasted_iota(I32, (N_GROUPS, tm), 0)
    grank = jnp.zeros((N_GROUPS, tm), F32)
    for g2 in range(N_GROUPS):
        rowv = gscore[g2:g2 + 1, :]
        beats = (rowv > gscore) | ((rowv == gscore) & (g2 < gidx))
        grank = grank + jnp.where(beats, 1.0, 0.0)
    gsel = jnp.where(grank < float(TOPK_GROUPS), 1.0, 0.0)
    emask = jnp.broadcast_to(gsel.reshape(N_GROUPS, 1, tm), (N_GROUPS, GROUP_SIZE, tm)).reshape(N_EXPERTS, tm)
    choice = jnp.where(emask > 0.5, biased, NEG_INF)

    eidx = lax.broadcasted_iota(I32, (N_EXPERTS, tm), 0).astype(F32)
    remaining = choice
    e_rows = []
    for _k in range(TOP_K):
        mx = jnp.max(remaining, axis=0, keepdims=True)
        idx = jnp.min(jnp.where(remaining == mx, eidx, float(N_EXPERTS)), axis=0, keepdims=True)
        e_rows.append(idx)
        remaining = jnp.where(eidx == idx, NEG_INF, remaining)
    selmask = jnp.where(remaining != choice, 1.0, 0.0)

    tr = lax.broadcasted_iota(I32, (tm, tm), 0)
    tc = lax.broadcasted_iota(I32, (tm, tm), 1)
    upper = jnp.where(tr < tc, 1.0, 0.0).astype(BF16)
    prefix = _dot(selmask.astype(BF16), upper) + carry_s[...]
    w_rows = []
    r_rows = []
    for ek in e_rows:
        onehot = eidx == ek
        w_rows.append(jnp.sum(jnp.where(onehot, scores, 0.0), axis=0, keepdims=True))
        r_rows.append(jnp.sum(jnp.where(onehot, prefix, 0.0), axis=0, keepdims=True))
    wsum = w_rows[0]
    for wk in w_rows[1:]:
        wsum = wsum + wk
    carry_s[...] = carry_s[...] + jnp.sum(selmask, axis=1, keepdims=True)

    e_ref[...] = jnp.concatenate(e_rows, axis=0).astype(I32)
    w_ref[...] = jnp.concatenate([wk / wsum * ROUTED_SCALE for wk in w_rows], axis=0)
    rk_ref[...] = jnp.concatenate(r_rows, axis=0).astype(I32)
    cnt_ref[...] = carry_s[...].astype(I32)


def _mid(ret2, moba2, x2, mod3, wo1, wo2, g_ffn, wr_t, rbias, wsg, wsu, wsd, seq):
    n, d = x2.shape
    tm = TM_PROJ
    tiles_per_seq = seq // tm
    half = ret2.shape[1]
    ff = wsg.shape[1]
    const = lambda i: (0, 0)
    row = lambda i: (i, 0)
    colt = lambda i: (0, i)
    return pl.pallas_call(
        _mid_kernel,
        out_shape=(
            jax.ShapeDtypeStruct((n, d), F32),
            jax.ShapeDtypeStruct((n, d // 2), I32),
            jax.ShapeDtypeStruct((TOP_K, n), I32),
            jax.ShapeDtypeStruct((TOP_K, n), F32),
            jax.ShapeDtypeStruct((TOP_K, n), I32),
            jax.ShapeDtypeStruct((N_EXPERTS, 1), I32),
        ),
        grid=(n // tm,),
        in_specs=[
            pl.BlockSpec((tm, half), row),
            pl.BlockSpec((tm, half), row),
            pl.BlockSpec((tm, d), row),
            pl.BlockSpec((1, N_MOD, d), lambda i: (i // tiles_per_seq, 0, 0)),
            pl.BlockSpec((half, d), const),
            pl.BlockSpec((half, d), const),
            pl.BlockSpec((1, d), const),
            pl.BlockSpec((N_EXPERTS, d), const),
            pl.BlockSpec((N_EXPERTS, 1), const),
            pl.BlockSpec((d, ff), const),
            pl.BlockSpec((d, ff), const),
            pl.BlockSpec((ff, d), const),
        ],
        out_specs=(
            pl.BlockSpec((tm, d), row),
            pl.BlockSpec((tm, d // 2), row),
            pl.BlockSpec((TOP_K, tm), colt),
            pl.BlockSpec((TOP_K, tm), colt),
            pl.BlockSpec((TOP_K, tm), colt),
            pl.BlockSpec((N_EXPERTS, 1), const),
        ),
        scratch_shapes=[pltpu.VMEM((N_EXPERTS, 1), F32)],
        compiler_params=pltpu.CompilerParams(
            dimension_semantics=("arbitrary",), vmem_limit_bytes=VMEM_LIMIT),
        name="mid",
    )(ret2, moba2, x2, mod3, wo1, wo2, g_ffn, wr_t, rbias, wsg, wsu, wsd)


ROW_EXP = 0
ROW_VALID = 1
ROW_FIRST = 2
ROW_AHEAD = 3
ROW_SLOT = 4
ROW_HEAD = 5
ROW_START = 6
SCHED_ROWS = 8


def _sched_kernel(cnt_ref, tab_ref):
    ne = N_EXPERTS
    nblk = tab_ref.shape[1]
    shift = FFN_BLOCK.bit_length() - 1
    e_sub = lax.broadcasted_iota(I32, (ne, ne), 0)
    e_lane = lax.broadcasted_iota(I32, (ne, ne), 1)
    e_col = lax.broadcasted_iota(I32, (ne, 1), 0).astype(F32)
    ids_row = lax.broadcasted_iota(I32, (1, ne), 1).astype(F32) + 1.0

    def to_row(col):
        return jnp.sum(jnp.where(e_sub == e_lane, col, 0.0), axis=0, keepdims=True)

    def running_col(row):
        return jnp.sum(jnp.where(e_lane <= e_sub, row, 0.0), axis=1, keepdims=True)

    def running_row(col):
        return jnp.sum(jnp.where(e_sub <= e_lane, col, 0.0), axis=0, keepdims=True)

    cnt_i = cnt_ref[...]
    cnt_col = cnt_i.astype(F32)
    pad_col = lax.shift_left(lax.shift_right_logical(cnt_i + (FFN_BLOCK - 1), shift), shift).astype(F32)
    pad_row = to_row(pad_col)
    ends_col = running_col(pad_row)
    start_col = ends_col - pad_col
    start_row = running_row(pad_col) - pad_row
    nreal = jnp.sum(pad_row, axis=1, keepdims=True) * (1.0 / FFN_BLOCK)

    step = lax.broadcasted_iota(I32, (1, nblk), 1)
    g = step.astype(F32)
    row0 = jnp.minimum(g, nreal - 1.0) * float(FFN_BLOCK)
    exp_g = jnp.sum(jnp.where(ends_col <= row0, 1.0, 0.0), axis=0, keepdims=True)
    mine = e_col == exp_g

    def per_block(col):
        return jnp.sum(jnp.where(mine, col, 0.0), axis=0, keepdims=True)

    valid = jnp.clip(per_block(cnt_col + start_col) - row0, 0.0, float(FFN_BLOCK))
    first = jnp.where(jnp.logical_and(g < nreal, per_block(start_col) == row0), 1.0, 0.0)

    used_col = jnp.where(cnt_col > 0.0, 1.0, 0.0)
    used_row = to_row(used_col)
    ord_col = running_col(used_row) - 1.0
    ord_row = running_row(used_col) - 1.0

    def used_at(pos):
        hit = jnp.logical_and(used_row > 0.0, ord_row == pos)
        return jnp.sum(jnp.where(hit, ids_row, 0.0), axis=1, keepdims=True) - 1.0

    ahead_col = used_at(ord_col + float(FFN_WEIGHT_SLOTS - 1))
    slot_col = ord_col - FFN_WEIGHT_SLOTS * jnp.floor((ord_col + 0.5) * (1.0 / FFN_WEIGHT_SLOTS))
    head = jnp.where(step == 0, nreal, 0.0)
    for j in range(FFN_WEIGHT_SLOTS - 1):
        head = head + jnp.where(step == 1 + j, used_at(jnp.full((1, 1), float(j), F32)), 0.0)
    rows = [exp_g, valid, first, per_block(ahead_col), per_block(slot_col), head,
            jnp.concatenate([start_row, jnp.zeros((1, nblk - ne), F32)], axis=1),
            jnp.zeros((1, nblk), F32)]
    tab_ref[...] = jnp.concatenate(rows, axis=0).astype(I32)


def _sched(cnt, nblk):
    assert FFN_BLOCK & (FFN_BLOCK - 1) == 0 and nblk >= N_EXPERTS
    return pl.pallas_call(
        _sched_kernel,
        out_shape=jax.ShapeDtypeStruct((SCHED_ROWS, nblk), I32),
        compiler_params=pltpu.CompilerParams(vmem_limit_bytes=VMEM_LIMIT),
        name="sched",
    )(cnt)


def _dest_kernel(tab_ref, e_ref, rk_ref, o_ref):
    e = e_ref[...]

    def body(ex, acc):
        return acc + jnp.where(e == ex, tab_ref[ROW_START, ex], 0)

    dest = lax.fori_loop(0, N_EXPERTS, body, rk_ref[...], unroll=8)
    for ch in range(o_ref.shape[0]):
        o_ref[ch] = dest[:, ch * SC_CHUNK:(ch + 1) * SC_CHUNK]


def _dest(tab, e_idx, rank):
    k, n = e_idx.shape
    tn = 2048
    return pl.pallas_call(
        _dest_kernel,
        out_shape=jax.ShapeDtypeStruct((n // SC_CHUNK, k, SC_CHUNK), I32),
        grid_spec=pltpu.PrefetchScalarGridSpec(
            num_scalar_prefetch=1,
            grid=(n // tn,),
            in_specs=[
                pl.BlockSpec((k, tn), lambda i, s: (0, i)),
                pl.BlockSpec((k, tn), lambda i, s: (0, i)),
            ],
            out_specs=pl.BlockSpec((tn // SC_CHUNK, k, SC_CHUNK), lambda i, s: (i, 0, 0)),
        ),
        compiler_params=pltpu.CompilerParams(vmem_limit_bytes=VMEM_LIMIT),
        name="dest",
    )(tab, e_idx, rank)


def _sc_dispatch(h2p, dest3, total_rows):
    n, words = h2p.shape
    nchunks = n // SC_CHUNK
    per_worker = nchunks // (SC_CORES * SC_SUBCORES)
    mesh = plsc.VectorSubcoreMesh(core_axis_name="c", subcore_axis_name="s",
                                  num_cores=SC_CORES, num_subcores=SC_SUBCORES)

    @functools.partial(
        pl.kernel, mesh=mesh,
        out_type=jax.ShapeDtypeStruct((total_rows, words), I32),
        scratch_types=[
            pltpu.VMEM((TOP_K, SC_CHUNK), I32),
            pltpu.VMEM((SC_CHUNK, words), I32),
            pltpu.SemaphoreType.DMA,
        ],
        name="sc_dispatch",
    )
    def run(h_hbm, d_hbm, xs_hbm, idx_v, rows_v, sem):
        wid = lax.axis_index("s") * SC_CORES + lax.axis_index("c")

        @pl.loop(0, per_worker)
        def _(j):
            ch = wid * per_worker + j
            pltpu.sync_copy(d_hbm.at[ch], idx_v)
            pltpu.sync_copy(h_hbm.at[pl.ds(ch * SC_CHUNK, SC_CHUNK)], rows_v)
            copies = [pltpu.async_copy(rows_v, xs_hbm.at[idx_v.at[k]], sem) for k in range(TOP_K)]
            for cp in copies:
                cp.wait()

    return run(h2p, dest3)


def _ffn_kernel(tab_ref, x_hbm, wg_hbm, wu_hbm, wd_hbm, y_hbm,
                x_s, y_s, wg_s, wu_s, wd_s, sem_x, sem_y, sem):
    i = pl.program_id(0)
    nreal = tab_ref[ROW_HEAD, 0]
    rows_per = x_s.shape[1]
    half = x_s.shape[2]
    sizes = tuple(range(FFN_GRAIN, rows_per + 1, FFN_GRAIN))

    def rows_needed(g):
        return (tab_ref[ROW_VALID, g] + (FFN_GRAIN - 1)) // FFN_GRAIN * FFN_GRAIN

    def by_size(nrows, fn):
        for n in sizes:
            @pl.when(nrows == n)
            def _():
                fn(n)

    def row_copy(g, n):
        slot = lax.rem(g, FFN_LOOKAHEAD + 1)
        return pltpu.make_async_copy(
            x_hbm.at[pl.ds(g * rows_per, n)], x_s.at[slot, pl.ds(0, n)], sem_x.at[slot])

    def out_copy(g, n):
        slot = lax.rem(g, FFN_OUT_SLOTS)
        return pltpu.make_async_copy(
            y_s.at[slot, pl.ds(0, n)], y_hbm.at[pl.ds(g * rows_per, n)], sem_y.at[slot])

    def weight_copies(e, s):
        return (pltpu.make_async_copy(wg_hbm.at[e], wg_s.at[s], sem.at[s, 0]),
                pltpu.make_async_copy(wu_hbm.at[e], wu_s.at[s], sem.at[s, 1]),
                pltpu.make_async_copy(wd_hbm.at[e], wd_s.at[s], sem.at[s, 2]))

    @pl.when(i == 0)
    def _():
        for j in range(FFN_WEIGHT_SLOTS - 1):
            @pl.when(tab_ref[ROW_HEAD, 1 + j] >= 0)
            def _():
                for cp in weight_copies(tab_ref[ROW_HEAD, 1 + j], j):
                    cp.start()
        for g in range(FFN_LOOKAHEAD):
            @pl.when(g < nreal)
            def _():
                by_size(rows_needed(g), lambda n: row_copy(g, n).start())

    @pl.when(i < nreal)
    def _():
        s = tab_ref[ROW_SLOT, i]
        fetch = i + FFN_LOOKAHEAD

        @pl.when(fetch < nreal)
        def _():
            by_size(rows_needed(fetch), lambda n: row_copy(fetch, n).start())

        by_size(rows_needed(i), lambda n: row_copy(i, n).wait())

        @pl.when(tab_ref[ROW_FIRST, i] == 1)
        def _():
            for cp in weight_copies(tab_ref[ROW_EXP, i], s):
                cp.wait()

            ahead = tab_ref[ROW_AHEAD, i]

            @pl.when(ahead >= 0)
            def _():
                for cp in weight_copies(ahead, lax.rem(s + FFN_WEIGHT_SLOTS - 1, FFN_WEIGHT_SLOTS)):
                    cp.start()

        @pl.when(i >= FFN_OUT_SLOTS)
        def _():
            done = i - FFN_OUT_SLOTS
            by_size(rows_needed(done), lambda n: out_copy(done, n).wait())

        x_slot = lax.rem(i, FFN_LOOKAHEAD + 1)
        y_slot = lax.rem(i, FFN_OUT_SLOTS)
        valid = tab_ref[ROW_VALID, i]

        def expert_rows(n):
            r = lax.broadcasted_iota(I32, (n, 1), 0)
            x_lo, x_hi = _unpack_halves(jnp.where(r < valid, x_s[x_slot, pl.ds(0, n), :], 0))
            hg = _dot(x_lo, wg_s[s, :half, :]) + _dot(x_hi, wg_s[s, half:, :])
            hu = _dot(x_lo, wu_s[s, :half, :]) + _dot(x_hi, wu_s[s, half:, :])
            y_s[y_slot, pl.ds(0, n), :] = _pack_halves(_dot(_silu(hg) * hu, wd_s[s]))
            out_copy(i, n).start()

        by_size(rows_needed(i), expert_rows)

        @pl.when(i == nreal - 1)
        def _():
            for back in range(FFN_OUT_SLOTS):
                last = i - back

                @pl.when(last >= 0)
                def _():
                    by_size(rows_needed(last), lambda n: out_copy(last, n).wait())


def _ffn(tab, xs, w_gate, w_up, w_down):
    p, half = xs.shape
    d = 2 * half
    ff = w_gate.shape[2]
    return pl.pallas_call(
        _ffn_kernel,
        out_shape=jax.ShapeDtypeStruct((p, half), I32),
        grid_spec=pltpu.PrefetchScalarGridSpec(
            num_scalar_prefetch=1,
            grid=(tab.shape[1],),
            in_specs=[
                pl.BlockSpec(memory_space=pl.ANY),
                pl.BlockSpec(memory_space=pl.ANY),
                pl.BlockSpec(memory_space=pl.ANY),
                pl.BlockSpec(memory_space=pl.ANY),
            ],
            out_specs=pl.BlockSpec(memory_space=pl.ANY),
            scratch_shapes=[
                pltpu.VMEM((FFN_LOOKAHEAD + 1, FFN_BLOCK, half), I32),
                pltpu.VMEM((FFN_OUT_SLOTS, FFN_BLOCK, half), I32),
                pltpu.VMEM((FFN_WEIGHT_SLOTS, d, ff), F32),
                pltpu.VMEM((FFN_WEIGHT_SLOTS, d, ff), F32),
                pltpu.VMEM((FFN_WEIGHT_SLOTS, ff, d), F32),
                pltpu.SemaphoreType.DMA((FFN_LOOKAHEAD + 1,)),
                pltpu.SemaphoreType.DMA((FFN_OUT_SLOTS,)),
                pltpu.SemaphoreType.DMA((FFN_WEIGHT_SLOTS, 3)),
            ],
        ),
        compiler_params=pltpu.CompilerParams(
            dimension_semantics=("arbitrary",), vmem_limit_bytes=VMEM_LIMIT, has_side_effects=True),
        name="ffn",
    )(tab, xs, w_gate, w_up, w_down)


def _sc_gather(y, dest3):
    a, words = y.shape
    nchunks, _, chunk = dest3.shape
    n = nchunks * chunk
    per_worker = nchunks // (SC_CORES * SC_SUBCORES)
    nbuf = 3
    parts = chunk // SC_GATHER_ROWS
    items = [(c, k, h) for c in range(per_worker) for k in range(TOP_K) for h in range(parts)]
    mesh = plsc.VectorSubcoreMesh(core_axis_name="c", subcore_axis_name="s",
                                  num_cores=SC_CORES, num_subcores=SC_SUBCORES)

    @functools.partial(
        pl.kernel, mesh=mesh,
        out_type=jax.ShapeDtypeStruct((TOP_K, n, words), I32),
        scratch_types=[
            pltpu.VMEM((per_worker, TOP_K, chunk), I32),
            [pltpu.VMEM((SC_GATHER_ROWS, words), I32)] * nbuf,
            pltpu.SemaphoreType.DMA((nbuf,)),
            pltpu.SemaphoreType.DMA((nbuf,)),
        ],
        name="sc_gather",
    )
    def run(y_hbm, d_hbm, yt_hbm, idx_v, bufs, sem_g, sem_w):
        wid = lax.axis_index("s") * SC_CORES + lax.axis_index("c")
        pltpu.sync_copy(d_hbm.at[pl.ds(wid * per_worker, per_worker)], idx_v)

        def gather(m):
            c, k, h = items[m]
            idx = idx_v.at[c, k, pl.ds(h * SC_GATHER_ROWS, SC_GATHER_ROWS)]
            return pltpu.async_copy(y_hbm.at[idx], bufs[m % nbuf], sem_g.at[m % nbuf])

        def write(m):
            c, k, h = items[m]
            rows = pl.ds((wid * per_worker + c) * chunk + h * SC_GATHER_ROWS, SC_GATHER_ROWS)
            return pltpu.async_copy(bufs[m % nbuf], yt_hbm.at[k, rows], sem_w.at[m % nbuf])

        gathers = {m: gather(m) for m in range(min(nbuf - 1, len(items)))}
        writes = {}
        for m in range(len(items)):
            gathers.pop(m).wait()
            writes[m] = write(m)
            nxt = m + nbuf - 1
            if nxt < len(items):
                if m >= 1:
                    writes.pop(m - 1).wait()
                gathers[nxt] = gather(nxt)
        for m in sorted(writes):
            writes.pop(m).wait()

    return run(y, dest3)


def _combine_kernel(yt_ref, wt_ref, xb_ref, mod_ref, o_ref):
    half = yt_ref.shape[2]
    wt = wt_ref[...].T
    lo, hi = _unpack_halves(yt_ref[0])
    r_lo = lo * wt[:, 0:1]
    r_hi = hi * wt[:, 0:1]
    for k in range(1, TOP_K):
        lo, hi = _unpack_halves(yt_ref[k])
        r_lo = r_lo + lo * wt[:, k:k + 1]
        r_hi = r_hi + hi * wt[:, k:k + 1]
    gate = mod_ref[0][5:6]
    o_ref[:, :half] = xb_ref[:, :half] + gate[:, :half] * r_lo
    o_ref[:, half:] = xb_ref[:, half:] + gate[:, half:] * r_hi


def _combine(yt, w_t, xb, mod3, seq):
    n, d = xb.shape
    tm = TM_COMBINE
    tiles_per_seq = seq // tm
    return pl.pallas_call(
        _combine_kernel,
        out_shape=jax.ShapeDtypeStruct((n, d), F32),
        grid=(n // tm,),
        in_specs=[
            pl.BlockSpec((TOP_K, tm, d // 2), lambda i: (0, i, 0)),
            pl.BlockSpec((TOP_K, tm), lambda i: (0, i)),
            pl.BlockSpec((tm, d), lambda i: (i, 0)),
            pl.BlockSpec((1, N_MOD, d), lambda i: (i // tiles_per_seq, 0, 0)),
        ],
        out_specs=pl.BlockSpec((tm, d), lambda i: (i, 0)),
        compiler_params=pltpu.CompilerParams(vmem_limit_bytes=VMEM_LIMIT),
        name="combine",
    )(yt, w_t, xb, mod3)


def _rotary_tables(seq):
    half = RET_DK // 2
    inv = ROPE_BASE ** (-np.arange(half, dtype=np.float64) / half)
    ang = np.arange(seq, dtype=np.float64)[:, None] * inv[None, :]
    cos = np.cos(ang).astype(np.float32)
    sin = np.sin(ang).astype(np.float32)
    return (jnp.asarray(np.concatenate([cos, cos], axis=-1)),
            jnp.asarray(np.concatenate([-sin, sin], axis=-1)))


def kernel(x, c, w_ada, b_ada, g_mix, w_in, q_gain, k_gain, w_out, g_ffn, w_router, router_bias,
           w_gate, w_up, w_down, ws_gate, ws_up, ws_down):
    bsz, seq, d = x.shape
    n = bsz * seq
    depth = w_ada.shape[0]
    cos_full, sin_signed = _rotary_tables(seq)
    log_g = jnp.asarray(np.log1p(-np.exp2(-5.0 - np.arange(RET_HEADS, dtype=np.float64))).astype(np.float32))
    ret_w = RET_HEADS * RET_DK
    x2 = x.reshape(n, d)
    for l in range(depth):
        mod3 = _adaln(c, w_ada[l], b_ada[l]).reshape(bsz, N_MOD, d)
        proj = _inproj(x2, mod3, g_mix[l].reshape(1, d), w_in[l].astype(BF16), cos_full, sin_signed, seq)
        proj3 = proj.reshape(bsz, seq, IN_COLS)
        ret = _retention(log_g, proj3)
        qg2 = jnp.tile(q_gain[l].reshape(1, MOBA_DH), (1, 2))
        kg2 = jnp.tile(k_gain[l].reshape(1, MOBA_DH), (1, 2))
        moba = _moba(proj3, qg2, kg2)
        wo = w_out[l].astype(BF16)
        xb, h2, e_idx, w_k, rank, cnt = _mid(
            ret.reshape(n, ret_w), moba.reshape(n, MOBA_HEADS * MOBA_DH), x2, mod3,
            wo[:ret_w], wo[ret_w:], g_ffn[l].reshape(1, d),
            w_router[l].T.astype(BF16), router_bias[l].reshape(N_EXPERTS, 1),
            ws_gate[l].astype(BF16), ws_up[l].astype(BF16), ws_down[l].astype(BF16), seq)
        nblk = n * TOP_K // FFN_BLOCK + N_EXPERTS
        tab = _sched(cnt, nblk)
        dest3 = _dest(tab, e_idx, rank)
        xs = _sc_dispatch(h2, dest3, nblk * FFN_BLOCK)
        y = _ffn(tab, xs, w_gate[l], w_up[l], w_down[l])
        yt = _sc_gather(y, dest3)
        x2 = _combine(yt, w_k, xb, mod3, seq)
    return x2.reshape(bsz, seq, d)
```

```python
import functools

import numpy as np
import jax
import jax.numpy as jnp
from jax import lax
from jax.experimental import pallas as pl
from jax.experimental.pallas import tpu as pltpu
from jax.experimental.pallas import tpu_sc as plsc

F32 = jnp.float32
BF16 = jnp.bfloat16
I32 = jnp.int32

D_MODEL = 1024
RET_HEADS = 4
RET_DK = 128
MOBA_HEADS = 8
MOBA_DH = 64
MOBA_BLOCK = 256
MOBA_TOPK = 3
ROPE_BASE = 10000.0
N_EXPERTS = 256
TOP_K = 8
N_GROUPS = 8
TOPK_GROUPS = 4
GROUP_SIZE = N_EXPERTS // N_GROUPS
EXPERT_FF = 256
ROUTED_SCALE = 2.5
N_MOD = 6
EPS = 1e-6
IN_COLS = 3584

LANES = 128
RET_CHUNK = 256
TM_PROJ = 512
TM_COMBINE = 512
MOE_PARTS = 2
MOBA_ONES_ROWS = 16
SC_CORES = 2
SC_SUBCORES = 16
SC_CHUNK = 128
SC_GATHER_ROWS = 64
FFN_BLOCK = 512
FFN_GRAIN = 128
FFN_LOOKAHEAD = 5
FFN_WEIGHT_SLOTS = 3
FFN_OUT_SLOTS = 3
VMEM_LIMIT = 56 * 1024 * 1024

NEG_INF = float("-inf")
LOG2_E = 1.4426950408889634


def _silu(x):
    return x * jax.nn.sigmoid(x)


def _nt_dot(a, b):
    return lax.dot_general(a, b, (((1,), (1,)), ((), ())), preferred_element_type=F32)


def _tn_dot(a, b):
    return lax.dot_general(a, b, (((0,), (0,)), ((), ())), preferred_element_type=F32)


def _dot(a, b):
    return jnp.dot(a, b, preferred_element_type=F32)


HI_MASK = -65536


def _pack_halves(v):
    w = v.shape[1] // 2
    lo = lax.bitcast_convert_type(v[:, :w].astype(BF16).astype(F32), I32)
    hi = lax.bitcast_convert_type(v[:, w:].astype(BF16).astype(F32), I32)
    return lax.shift_right_logical(lo, 16) | (hi & HI_MASK)


def _unpack_halves(u):
    lo = lax.bitcast_convert_type(lax.shift_left(u, 16), F32)
    hi = lax.bitcast_convert_type(u & HI_MASK, F32)
    return lo, hi


def _adaln_kernel(c_ref, w_ref, b_ref, o_ref):
    s = _silu(c_ref[...])
    o_ref[...] = _dot(s.astype(BF16), w_ref[...].astype(BF16)) + b_ref[...]


def _adaln(c, w_ada, b_ada):
    bsz, d = c.shape
    ncol = w_ada.shape[1]
    tn = 1024
    return pl.pallas_call(
        _adaln_kernel,
        out_shape=jax.ShapeDtypeStruct((bsz, ncol), F32),
        grid=(ncol // tn,),
        in_specs=[
            pl.BlockSpec((bsz, d), lambda j: (0, 0)),
            pl.BlockSpec((d, tn), lambda j: (0, j)),
            pl.BlockSpec((1, tn), lambda j: (0, j)),
        ],
        out_specs=pl.BlockSpec((bsz, tn), lambda j: (0, j)),
        compiler_params=pltpu.CompilerParams(vmem_limit_bytes=VMEM_LIMIT),
        name="adaln",
    )(c, w_ada, b_ada.reshape(1, ncol))


def _inproj_kernel(x_ref, mod_ref, g_ref, w_ref, cos_ref, sin_ref, o_ref):
    x = x_ref[...]
    ms = jnp.mean(x * x, axis=-1, keepdims=True)
    m = mod_ref[0]
    h = (x * lax.rsqrt(ms + EPS) * g_ref[...]) * (1.0 + m[1:2]) + m[0:1]
    hb = h.astype(BF16)
    cosf = cos_ref[...]
    sinf = sin_ref[...]
    k_scale = RET_DK ** -0.5
    width = RET_HEADS * RET_DK
    for ci in range(IN_COLS // width):
        acc = _dot(hb, w_ref[:, ci * width:(ci + 1) * width])
        if ci < 2:
            for hh in range(RET_HEADS):
                xh = acc[:, hh * RET_DK:(hh + 1) * RET_DK]
                r = xh * cosf + pltpu.roll(xh, RET_DK // 2, axis=1) * sinf
                if ci == 1:
                    r = r * k_scale
                o_ref[:, ci * width + hh * RET_DK:ci * width + (hh + 1) * RET_DK] = r.astype(BF16)
        else:
            o_ref[:, ci * width:(ci + 1) * width] = acc.astype(BF16)


def _inproj(x2, mod3, g_mix, w_in_bf, cos_full, sin_signed, seq):
    n, d = x2.shape
    tm = TM_PROJ
    tiles_per_seq = seq // tm
    return pl.pallas_call(
        _inproj_kernel,
        out_shape=jax.ShapeDtypeStruct((n, IN_COLS), BF16),
        grid=(n // tm,),
        in_specs=[
            pl.BlockSpec((tm, d), lambda i: (i, 0)),
            pl.BlockSpec((1, N_MOD, d), lambda i: (i // tiles_per_seq, 0, 0)),
            pl.BlockSpec((1, d), lambda i: (0, 0)),
            pl.BlockSpec((d, IN_COLS), lambda i: (0, 0), pipeline_mode=pl.Buffered(1)),
            pl.BlockSpec((tm, LANES), lambda i: (i % tiles_per_seq, 0)),
            pl.BlockSpec((tm, LANES), lambda i: (i % tiles_per_seq, 0)),
        ],
        out_specs=pl.BlockSpec((tm, IN_COLS), lambda i: (i, 0)),
        compiler_params=pltpu.CompilerParams(vmem_limit_bytes=VMEM_LIMIT),
        name="inproj",
    )(x2, mod3, g_mix, w_in_bf, cos_full, sin_signed)


def _ret_kernel(lg_ref, q_ref, k_ref, v_ref, g_ref, o_ref):
    seq = q_ref.shape[1]
    c = RET_CHUNK
    lg = lg_ref[pl.program_id(1)]
    row = lax.broadcasted_iota(I32, (c, c), 0)
    col = lax.broadcasted_iota(I32, (c, c), 1)
    diff = (row - col).astype(F32)
    dmask = jnp.where(diff >= 0, jnp.exp(lg * jnp.maximum(diff, 0.0)), 0.0)
    idx = lax.broadcasted_iota(I32, (c, 1), 0).astype(F32)
    q_decay = jnp.exp(lg * (idx + 1.0))
    k_decay = jnp.exp(lg * (c - 1.0 - idx))
    chunk_decay = jnp.exp(jnp.full((1, 1), lg * c, F32))
    state = jnp.zeros((RET_DK, RET_DK), F32)
    for n in range(seq // c):
        rows = slice(n * c, (n + 1) * c)
        qn = q_ref[0, rows, :]
        kn = k_ref[0, rows, :]
        vn = v_ref[0, rows, :]
        scores = _nt_dot(qn, kn) * dmask
        inner = _dot(scores.astype(BF16), vn)
        qs = (qn.astype(F32) * q_decay).astype(BF16)
        cross = _dot(qs, state.astype(BF16))
        o = inner + cross
        o = o * lax.rsqrt(jnp.mean(o * o, axis=-1, keepdims=True) + EPS)
        gn = g_ref[0, rows, :].astype(F32)
        o_ref[0, rows, :] = (_silu(gn) * o).astype(BF16)
        ks = (kn.astype(F32) * k_decay).astype(BF16)
        state = state * chunk_decay + _tn_dot(ks, vn)


def _retention(log_g, proj3):
    bsz, seq, _ = proj3.shape
    blk = (1, seq, RET_DK)
    return pl.pallas_call(
        _ret_kernel,
        out_shape=jax.ShapeDtypeStruct((bsz, seq, RET_HEADS * RET_DK), BF16),
        grid_spec=pltpu.PrefetchScalarGridSpec(
            num_scalar_prefetch=1,
            grid=(bsz, RET_HEADS),
            in_specs=[
                pl.BlockSpec(blk, lambda b, h, lg: (b, 0, h)),
                pl.BlockSpec(blk, lambda b, h, lg: (b, 0, RET_HEADS + h)),
                pl.BlockSpec(blk, lambda b, h, lg: (b, 0, 2 * RET_HEADS + h)),
                pl.BlockSpec(blk, lambda b, h, lg: (b, 0, 3 * RET_HEADS + h)),
            ],
            out_specs=pl.BlockSpec(blk, lambda b, h, lg: (b, 0, h)),
        ),
        compiler_params=pltpu.CompilerParams(vmem_limit_bytes=VMEM_LIMIT),
        name="retention",
    )(log_g, proj3, proj3, proj3, proj3)


def _moba_kernel(q_ref, k_ref, v_ref, qg_ref, kg_ref, o_ref, qt_s, ka_s, kb_s, vta_s, vtb_s):
    seq = q_ref.shape[1]
    lb = MOBA_BLOCK
    nb = seq // lb
    lane = lax.broadcasted_iota(I32, (1, LANES), 1)
    is_a = lane < MOBA_DH

    def head_norm(xf, gain):
        sq = xf * xf
        s_a = jnp.sum(jnp.where(is_a, sq, 0.0), axis=-1, keepdims=True)
        s_b = jnp.sum(jnp.where(is_a, 0.0, sq), axis=-1, keepdims=True)
        inv = jnp.where(is_a, lax.rsqrt(s_a / MOBA_DH + EPS), lax.rsqrt(s_b / MOBA_DH + EPS))
        return xf * inv * gain

    qg = qg_ref[...]
    kg = kg_ref[...]
    k_means = []
    for j in range(nb):
        rows = slice(j * lb, (j + 1) * lb)
        kf = head_norm(k_ref[0, rows, :].astype(F32), kg)
        ka_s[rows, :] = jnp.where(is_a, kf, 0.0).astype(BF16)
        kb_s[rows, :] = jnp.where(is_a, 0.0, kf).astype(BF16)
        k_means.append(jnp.mean(kf, axis=0, keepdims=True))
        qf = head_norm(q_ref[0, rows, :].astype(F32), qg)
        qt_s[:, rows] = (qf * (MOBA_DH ** -0.5 * LOG2_E)).T.astype(BF16)
        vt = v_ref[0, rows, :].astype(F32).T
        ones = jnp.ones((MOBA_ONES_ROWS, lb), F32)
        vta_s[:, rows] = jnp.concatenate([vt[:MOBA_DH], ones], axis=0).astype(BF16)
        vtb_s[:, rows] = jnp.concatenate([vt[MOBA_DH:], ones], axis=0).astype(BF16)
    k_mean = jnp.concatenate(k_means + [jnp.zeros((16 - nb, LANES), F32)], axis=0)
    k_mean_h = (jnp.where(is_a, k_mean, 0.0).astype(BF16), jnp.where(is_a, 0.0, k_mean).astype(BF16))
    k_s = (ka_s, kb_s)
    vt_s = (vta_s, vtb_s)

    r_loc = lax.broadcasted_iota(I32, (lb, lb), 0)
    c_loc = lax.broadcasted_iota(I32, (lb, lb), 1)
    causal = r_loc <= c_loc

    for i in range(nb):
        cols = slice(i * lb, (i + 1) * lb)
        qt = qt_s[:, cols]
        outs = []
        for hx in range(2):
            bias = [None] * i
            if i > MOBA_TOPK:
                gate = _dot(k_mean_h[hx], qt)
                g = [gate[j:j + 1, :] for j in range(i)]
                for j in range(i):
                    rank = jnp.zeros((1, lb), F32)
                    for j2 in range(i):
                        if j2 == j:
                            continue
                        beats = (g[j2] >= g[j]) if j2 < j else (g[j2] > g[j])
                        rank = rank + jnp.where(beats, 1.0, 0.0)
                    bias[j] = jnp.where(rank < float(MOBA_TOPK), 0.0, NEG_INF)
            pieces = []
            for j in range(i + 1):
                s = _dot(k_s[hx][j * lb:(j + 1) * lb, :], qt)
                if j == i:
                    s = jnp.where(causal, s, NEG_INF)
                elif bias[j] is not None:
                    s = s + bias[j]
                pieces.append(s)
            mx = jnp.max(pieces[0], axis=0, keepdims=True)
            for s in pieces[1:]:
                mx = jnp.maximum(mx, jnp.max(s, axis=0, keepdims=True))
            acc = jnp.zeros((MOBA_DH + MOBA_ONES_ROWS, lb), F32)
            for j, s in enumerate(pieces):
                p = jnp.exp2(s - mx).astype(BF16)
                acc = acc + _dot(vt_s[hx][:, j * lb:(j + 1) * lb], p)
            outs.append(acc[:MOBA_DH] / acc[MOBA_DH:MOBA_DH + 1, :])
        o_ref[0, cols, :] = jnp.concatenate(outs, axis=0).T.astype(BF16)


def _moba(proj3, qg2, kg2):
    bsz, seq, _ = proj3.shape
    pairs = MOBA_HEADS // 2
    blk = (1, seq, LANES)
    base = 4 * RET_HEADS
    return pl.pallas_call(
        _moba_kernel,
        out_shape=jax.ShapeDtypeStruct((bsz, seq, MOBA_HEADS * MOBA_DH), BF16),
        grid=(bsz, pairs),
        in_specs=[
            pl.BlockSpec(blk, lambda b, p: (b, 0, base + p)),
            pl.BlockSpec(blk, lambda b, p: (b, 0, base + pairs + p)),
            pl.BlockSpec(blk, lambda b, p: (b, 0, base + 2 * pairs + p)),
            pl.BlockSpec((1, LANES), lambda b, p: (0, 0)),
            pl.BlockSpec((1, LANES), lambda b, p: (0, 0)),
        ],
        out_specs=pl.BlockSpec(blk, lambda b, p: (b, 0, p)),
        scratch_shapes=[
            pltpu.VMEM((LANES, seq), BF16),
            pltpu.VMEM((seq, LANES), BF16),
            pltpu.VMEM((seq, LANES), BF16),
            pltpu.VMEM((MOBA_DH + MOBA_ONES_ROWS, seq), BF16),
            pltpu.VMEM((MOBA_DH + MOBA_ONES_ROWS, seq), BF16),
        ],
        compiler_params=pltpu.CompilerParams(vmem_limit_bytes=VMEM_LIMIT),
        name="moba",
    )(proj3, proj3, proj3, qg2, kg2)


def _mid_kernel(ret_ref, moba_ref, x_ref, mod_ref, wo1_ref, wo2_ref, g_ref, wr_ref, rb_ref,
                wsg_ref, wsu_ref, wsd_ref,
                xb_ref, h2_ref, e_ref, w_ref, rk_ref, cnt_ref, carry_s):
    i = pl.program_id(0)
    tm = x_ref.shape[0]

    @pl.when(i == 0)
    def _():
        carry_s[...] = jnp.zeros_like(carry_s)

    m = mod_ref[0]
    mixed = _dot(ret_ref[...], wo1_ref[...]) + _dot(moba_ref[...], wo2_ref[...])
    x1 = x_ref[...] + m[2:3] * mixed
    ms = jnp.mean(x1 * x1, axis=-1, keepdims=True)
    h2 = (x1 * lax.rsqrt(ms + EPS) * g_ref[...]) * (1.0 + m[4:5]) + m[3:4]
    h2_ref[...] = _pack_halves(h2)
    h2b = h2.astype(BF16)

    hid = _silu(_dot(h2b, wsg_ref[...])) * _dot(h2b, wsu_ref[...])
    xb_ref[...] = x1 + m[5:6] * _dot(hid.astype(BF16), wsd_ref[...])

    scores = jax.nn.sigmoid(_nt_dot(wr_ref[...], h2b))
    biased = scores + rb_ref[...]
    grp = biased.reshape(N_GROUPS, GROUP_SIZE, tm)
    gi = lax.broadcasted_iota(I32, (N_GROUPS, GROUP_SIZE, tm), 1).astype(F32)
    top1 = jnp.max(grp, axis=1, keepdims=True)
    first = jnp.min(jnp.where(grp == top1, gi, float(GROUP_SIZE)), axis=1, keepdims=True)
    top2 = jnp.max(jnp.where(gi == first, NEG_INF, grp), axis=1, keepdims=True)
    gscore = (top1 + top2).reshape(N_GROUPS, tm)
    gidx = lax.broadcasted_iota(I32, (N_GROUPS, tm), 0)
    grank = jnp.zeros((N_GROUPS, tm), F32)
    for g2 in range(N_GROUPS):
        rowv = gscore[g2:g2 + 1, :]
        beats = (rowv > gscore) | ((rowv == gscore) & (g2 < gidx))
        grank = grank + jnp.where(beats, 1.0, 0.0)
    gsel = jnp.where(grank < float(TOPK_GROUPS), 1.0, 0.0)
    emask = jnp.broadcast_to(gsel.reshape(N_GROUPS, 1, tm), (N_GROUPS, GROUP_SIZE, tm)).reshape(N_EXPERTS, tm)
    choice = jnp.where(emask > 0.5, biased, NEG_INF)

    eidx = lax.broadcasted_iota(I32, (N_EXPERTS, tm), 0).astype(F32)
    remaining = choice
    e_rows = []
    for _k in range(TOP_K):
        mx = jnp.max(remaining, axis=0, keepdims=True)
        idx = jnp.min(jnp.where(remaining == mx, eidx, float(N_EXPERTS)), axis=0, keepdims=True)
        e_rows.append(idx)
        remaining = jnp.where(eidx == idx, NEG_INF, remaining)
    selmask = jnp.where(remaining != choice, 1.0, 0.0)

    tr = lax.broadcasted_iota(I32, (tm, tm), 0)
    tc = lax.broadcasted_iota(I32, (tm, tm), 1)
    upper = jnp.where(tr < tc, 1.0, 0.0).astype(BF16)
    prefix = _dot(selmask.astype(BF16), upper) + carry_s[...]
    w_rows = []
    r_rows = []
    for ek in e_rows:
        onehot = eidx == ek
        w_rows.append(jnp.sum(jnp.where(onehot, scores, 0.0), axis=0, keepdims=True))
        r_rows.append(jnp.sum(jnp.where(onehot, prefix, 0.0), axis=0, keepdims=True))
    wsum = w_rows[0]
    for wk in w_rows[1:]:
        wsum = wsum + wk
    carry_s[...] = carry_s[...] + jnp.sum(selmask, axis=1, keepdims=True)

    e_ref[...] = jnp.concatenate(e_rows, axis=0).astype(I32)
    w_ref[...] = jnp.concatenate([wk / wsum * ROUTED_SCALE for wk in w_rows], axis=0)
    rk_ref[...] = jnp.concatenate(r_rows, axis=0).astype(I32)
    cnt_ref[...] = carry_s[...].astype(I32)


def _mid(ret2, moba2, x2, mod3, wo1, wo2, g_ffn, wr_t, rbias, wsg, wsu, wsd, seq):
    n, d = x2.shape
    tm = TM_PROJ
    tiles_per_seq = seq // tm
    half = ret2.shape[1]
    ff = wsg.shape[1]
    const = lambda i: (0, 0)
    row = lambda i: (i, 0)
    colt = lambda i: (0, i)
    return pl.pallas_call(
        _mid_kernel,
        out_shape=(
            jax.ShapeDtypeStruct((n, d), F32),
            jax.ShapeDtypeStruct((n, d // 2), I32),
            jax.ShapeDtypeStruct((TOP_K, n), I32),
            jax.ShapeDtypeStruct((TOP_K, n), F32),
            jax.ShapeDtypeStruct((TOP_K, n), I32),
            jax.ShapeDtypeStruct((N_EXPERTS, 1), I32),
        ),
        grid=(n // tm,),
        in_specs=[
            pl.BlockSpec((tm, half), row),
            pl.BlockSpec((tm, half), row),
            pl.BlockSpec((tm, d), row),
            pl.BlockSpec((1, N_MOD, d), lambda i: (i // tiles_per_seq, 0, 0)),
            pl.BlockSpec((half, d), const),
            pl.BlockSpec((half, d), const),
            pl.BlockSpec((1, d), const),
            pl.BlockSpec((N_EXPERTS, d), const),
            pl.BlockSpec((N_EXPERTS, 1), const),
            pl.BlockSpec((d, ff), const),
            pl.BlockSpec((d, ff), const),
            pl.BlockSpec((ff, d), const),
        ],
        out_specs=(
            pl.BlockSpec((tm, d), row),
            pl.BlockSpec((tm, d // 2), row),
            pl.BlockSpec((TOP_K, tm), colt),
            pl.BlockSpec((TOP_K, tm), colt),
            pl.BlockSpec((TOP_K, tm), colt),
            pl.BlockSpec((N_EXPERTS, 1), const),
        ),
        scratch_shapes=[pltpu.VMEM((N_EXPERTS, 1), F32)],
        compiler_params=pltpu.CompilerParams(
            dimension_semantics=("arbitrary",), vmem_limit_bytes=VMEM_LIMIT),
        name="mid",
    )(ret2, moba2, x2, mod3, wo1, wo2, g_ffn, wr_t, rbias, wsg, wsu, wsd)


ROW_EXP = 0
ROW_VALID = 1
ROW_FIRST = 2
ROW_AHEAD = 3
ROW_SLOT = 4
ROW_HEAD = 5
ROW_START = 6
SCHED_ROWS = 8


def _sched_kernel(cnt_ref, tab_ref):
    ne = N_EXPERTS
    nblk = tab_ref.shape[1]
    shift = FFN_BLOCK.bit_length() - 1
    e_sub = lax.broadcasted_iota(I32, (ne, ne), 0)
    e_lane = lax.broadcasted_iota(I32, (ne, ne), 1)
    e_col = lax.broadcasted_iota(I32, (ne, 1), 0).astype(F32)
    ids_row = lax.broadcasted_iota(I32, (1, ne), 1).astype(F32) + 1.0

    def to_row(col):
        return jnp.sum(jnp.where(e_sub == e_lane, col, 0.0), axis=0, keepdims=True)

    def running_col(row):
        return jnp.sum(jnp.where(e_lane <= e_sub, row, 0.0), axis=1, keepdims=True)

    def running_row(col):
        return jnp.sum(jnp.where(e_sub <= e_lane, col, 0.0), axis=0, keepdims=True)

    cnt_i = cnt_ref[...]
    cnt_col = cnt_i.astype(F32)
    pad_col = lax.shift_left(lax.shift_right_logical(cnt_i + (FFN_BLOCK - 1), shift), shift).astype(F32)
    pad_row = to_row(pad_col)
    ends_col = running_col(pad_row)
    start_col = ends_col - pad_col
    start_row = running_row(pad_col) - pad_row
    nreal = jnp.sum(pad_row, axis=1, keepdims=True) * (1.0 / FFN_BLOCK)

    step = lax.broadcasted_iota(I32, (1, nblk), 1)
    g = step.astype(F32)
    row0 = jnp.minimum(g, nreal - 1.0) * float(FFN_BLOCK)
    exp_g = jnp.sum(jnp.where(ends_col <= row0, 1.0, 0.0), axis=0, keepdims=True)
    mine = e_col == exp_g

    def per_block(col):
        return jnp.sum(jnp.where(mine, col, 0.0), axis=0, keepdims=True)

    valid = jnp.clip(per_block(cnt_col + start_col) - row0, 0.0, float(FFN_BLOCK))
    first = jnp.where(jnp.logical_and(g < nreal, per_block(start_col) == row0), 1.0, 0.0)

    used_col = jnp.where(cnt_col > 0.0, 1.0, 0.0)
    used_row = to_row(used_col)
    ord_col = running_col(used_row) - 1.0
    ord_row = running_row(used_col) - 1.0

    def used_at(pos):
        hit = jnp.logical_and(used_row > 0.0, ord_row == pos)
        return jnp.sum(jnp.where(hit, ids_row, 0.0), axis=1, keepdims=True) - 1.0

    ahead_col = used_at(ord_col + float(FFN_WEIGHT_SLOTS - 1))
    slot_col = ord_col - FFN_WEIGHT_SLOTS * jnp.floor((ord_col + 0.5) * (1.0 / FFN_WEIGHT_SLOTS))
    head = jnp.where(step == 0, nreal, 0.0)
    for j in range(FFN_WEIGHT_SLOTS - 1):
        head = head + jnp.where(step == 1 + j, used_at(jnp.full((1, 1), float(j), F32)), 0.0)
    rows = [exp_g, valid, first, per_block(ahead_col), per_block(slot_col), head,
            jnp.concatenate([start_row, jnp.zeros((1, nblk - ne), F32)], axis=1),
            jnp.zeros((1, nblk), F32)]
    tab_ref[...] = jnp.concatenate(rows, axis=0).astype(I32)


def _sched(cnt, nblk):
    assert FFN_BLOCK & (FFN_BLOCK - 1) == 0 and nblk >= N_EXPERTS
    return pl.pallas_call(
        _sched_kernel,
        out_shape=jax.ShapeDtypeStruct((SCHED_ROWS, nblk), I32),
        compiler_params=pltpu.CompilerParams(vmem_limit_bytes=VMEM_LIMIT),
        name="sched",
    )(cnt)


def _dest_kernel(tab_ref, e_ref, rk_ref, o_ref):
    e = e_ref[...]

    def body(ex, acc):
        return acc + jnp.where(e == ex, tab_ref[ROW_START, ex], 0)

    dest = lax.fori_loop(0, N_EXPERTS, body, rk_ref[...], unroll=8)
    for ch in range(o_ref.shape[0]):
        o_ref[ch] = dest[:, ch * SC_CHUNK:(ch + 1) * SC_CHUNK]


def _dest(tab, e_idx, rank):
    k, n = e_idx.shape
    tn = 2048
    return pl.pallas_call(
        _dest_kernel,
        out_shape=jax.ShapeDtypeStruct((n // SC_CHUNK, k, SC_CHUNK), I32),
        grid_spec=pltpu.PrefetchScalarGridSpec(
            num_scalar_prefetch=1,
            grid=(n // tn,),
            in_specs=[
                pl.BlockSpec((k, tn), lambda i, s: (0, i)),
                pl.BlockSpec((k, tn), lambda i, s: (0, i)),
            ],
            out_specs=pl.BlockSpec((tn // SC_CHUNK, k, SC_CHUNK), lambda i, s: (i, 0, 0)),
        ),
        compiler_params=pltpu.CompilerParams(vmem_limit_bytes=VMEM_LIMIT),
        name="dest",
    )(tab, e_idx, rank)


def _sc_dispatch(h2p, dest3, total_rows):
    n, words = h2p.shape
    nchunks = n // SC_CHUNK
    per_worker = nchunks // (SC_CORES * SC_SUBCORES)
    mesh = plsc.VectorSubcoreMesh(core_axis_name="c", subcore_axis_name="s",
                                  num_cores=SC_CORES, num_subcores=SC_SUBCORES)

    @functools.partial(
        pl.kernel, mesh=mesh,
        out_type=jax.ShapeDtypeStruct((total_rows, words), I32),
        scratch_types=[
            pltpu.VMEM((TOP_K, SC_CHUNK), I32),
            pltpu.VMEM((SC_CHUNK, words), I32),
            pltpu.SemaphoreType.DMA,
        ],
        name="sc_dispatch",
    )
    def run(h_hbm, d_hbm, xs_hbm, idx_v, rows_v, sem):
        wid = lax.axis_index("s") * SC_CORES + lax.axis_index("c")

        @pl.loop(0, per_worker)
        def _(j):
            ch = wid * per_worker + j
            pltpu.sync_copy(d_hbm.at[ch], idx_v)
            pltpu.sync_copy(h_hbm.at[pl.ds(ch * SC_CHUNK, SC_CHUNK)], rows_v)
            copies = [pltpu.async_copy(rows_v, xs_hbm.at[idx_v.at[k]], sem) for k in range(TOP_K)]
            for cp in copies:
                cp.wait()

    return run(h2p, dest3)


def _ffn_kernel(tab_ref, x_hbm, wg_hbm, wu_hbm, wd_hbm, y_hbm,
                x_s, y_s, wg_s, wu_s, wd_s, sem_x, sem_y, sem):
    i = pl.program_id(0)
    nreal = tab_ref[ROW_HEAD, 0]
    rows_per = x_s.shape[1]
    half = x_s.shape[2]
    sizes = tuple(range(FFN_GRAIN, rows_per + 1, FFN_GRAIN))

    def rows_needed(g):
        return (tab_ref[ROW_VALID, g] + (FFN_GRAIN - 1)) // FFN_GRAIN * FFN_GRAIN

    def by_size(nrows, fn):
        for n in sizes:
            @pl.when(nrows == n)
            def _():
                fn(n)

    def row_copy(g, n):
        slot = lax.rem(g, FFN_LOOKAHEAD + 1)
        return pltpu.make_async_copy(
            x_hbm.at[pl.ds(g * rows_per, n)], x_s.at[slot, pl.ds(0, n)], sem_x.at[slot])

    def out_copy(g, n):
        slot = lax.rem(g, FFN_OUT_SLOTS)
        return pltpu.make_async_copy(
            y_s.at[slot, pl.ds(0, n)], y_hbm.at[pl.ds(g * rows_per, n)], sem_y.at[slot])

    def weight_copies(e, s):
        return (pltpu.make_async_copy(wg_hbm.at[e], wg_s.at[s], sem.at[s, 0]),
                pltpu.make_async_copy(wu_hbm.at[e], wu_s.at[s], sem.at[s, 1]),
                pltpu.make_async_copy(wd_hbm.at[e], wd_s.at[s], sem.at[s, 2]))

    @pl.when(i == 0)
    def _():
        for j in range(FFN_WEIGHT_SLOTS - 1):
            @pl.when(tab_ref[ROW_HEAD, 1 + j] >= 0)
            def _():
                for cp in weight_copies(tab_ref[ROW_HEAD, 1 + j], j):
                    cp.start()
        for g in range(FFN_LOOKAHEAD):
            @pl.when(g < nreal)
            def _():
                by_size(rows_needed(g), lambda n: row_copy(g, n).start())

    @pl.when(i < nreal)
    def _():
        s = tab_ref[ROW_SLOT, i]
        fetch = i + FFN_LOOKAHEAD

        @pl.when(fetch < nreal)
        def _():
            by_size(rows_needed(fetch), lambda n: row_copy(fetch, n).start())

        by_size(rows_needed(i), lambda n: row_copy(i, n).wait())

        @pl.when(tab_ref[ROW_FIRST, i] == 1)
        def _():
            for cp in weight_copies(tab_ref[ROW_EXP, i], s):
                cp.wait()

            ahead = tab_ref[ROW_AHEAD, i]

            @pl.when(ahead >= 0)
            def _():
                for cp in weight_copies(ahead, lax.rem(s + FFN_WEIGHT_SLOTS - 1, FFN_WEIGHT_SLOTS)):
                    cp.start()

        @pl.when(i >= FFN_OUT_SLOTS)
        def _():
            done = i - FFN_OUT_SLOTS
            by_size(rows_needed(done), lambda n: out_copy(done, n).wait())

        x_slot = lax.rem(i, FFN_LOOKAHEAD + 1)
        y_slot = lax.rem(i, FFN_OUT_SLOTS)
        valid = tab_ref[ROW_VALID, i]

        def expert_rows(n):
            r = lax.broadcasted_iota(I32, (n, 1), 0)
            x_lo, x_hi = _unpack_halves(jnp.where(r < valid, x_s[x_slot, pl.ds(0, n), :], 0))
            hg = _dot(x_lo, wg_s[s, :half, :]) + _dot(x_hi, wg_s[s, half:, :])
            hu = _dot(x_lo, wu_s[s, :half, :]) + _dot(x_hi, wu_s[s, half:, :])
            y_s[y_slot, pl.ds(0, n), :] = _pack_halves(_dot(_silu(hg) * hu, wd_s[s]))
            out_copy(i, n).start()

        by_size(rows_needed(i), expert_rows)

        @pl.when(i == nreal - 1)
        def _():
            for back in range(FFN_OUT_SLOTS):
                last = i - back

                @pl.when(last >= 0)
                def _():
                    by_size(rows_needed(last), lambda n: out_copy(last, n).wait())


def _ffn(tab, xs, w_gate, w_up, w_down):
    p, half = xs.shape
    d = 2 * half
    ff = w_gate.shape[2]
    return pl.pallas_call(
        _ffn_kernel,
        out_shape=jax.ShapeDtypeStruct((p, half), I32),
        grid_spec=pltpu.PrefetchScalarGridSpec(
            num_scalar_prefetch=1,
            grid=(tab.shape[1],),
            in_specs=[
                pl.BlockSpec(memory_space=pl.ANY),
                pl.BlockSpec(memory_space=pl.ANY),
                pl.BlockSpec(memory_space=pl.ANY),
                pl.BlockSpec(memory_space=pl.ANY),
            ],
            out_specs=pl.BlockSpec(memory_space=pl.ANY),
            scratch_shapes=[
                pltpu.VMEM((FFN_LOOKAHEAD + 1, FFN_BLOCK, half), I32),
                pltpu.VMEM((FFN_OUT_SLOTS, FFN_BLOCK, half), I32),
                pltpu.VMEM((FFN_WEIGHT_SLOTS, d, ff), F32),
                pltpu.VMEM((FFN_WEIGHT_SLOTS, d, ff), F32),
                pltpu.VMEM((FFN_WEIGHT_SLOTS, ff, d), F32),
                pltpu.SemaphoreType.DMA((FFN_LOOKAHEAD + 1,)),
                pltpu.SemaphoreType.DMA((FFN_OUT_SLOTS,)),
                pltpu.SemaphoreType.DMA((FFN_WEIGHT_SLOTS, 3)),
            ],
        ),
        compiler_params=pltpu.CompilerParams(
            dimension_semantics=("arbitrary",), vmem_limit_bytes=VMEM_LIMIT, has_side_effects=True),
        name="ffn",
    )(tab, xs, w_gate, w_up, w_down)


def _sc_gather(y, dest3):
    a, words = y.shape
    nchunks, _, chunk = dest3.shape
    n = nchunks * chunk
    per_worker = nchunks // (SC_CORES * SC_SUBCORES)
    nbuf = 3
    parts = chunk // SC_GATHER_ROWS
    items = [(c, k, h) for c in range(per_worker) for k in range(TOP_K) for h in range(parts)]
    mesh = plsc.VectorSubcoreMesh(core_axis_name="c", subcore_axis_name="s",
                                  num_cores=SC_CORES, num_subcores=SC_SUBCORES)

    @functools.partial(
        pl.kernel, mesh=mesh,
        out_type=jax.ShapeDtypeStruct((TOP_K, n, words), I32),
        scratch_types=[
            pltpu.VMEM((per_worker, TOP_K, chunk), I32),
            [pltpu.VMEM((SC_GATHER_ROWS, words), I32)] * nbuf,
            pltpu.SemaphoreType.DMA((nbuf,)),
            pltpu.SemaphoreType.DMA((nbuf,)),
        ],
        name="sc_gather",
    )
    def run(y_hbm, d_hbm, yt_hbm, idx_v, bufs, sem_g, sem_w):
        wid = lax.axis_index("s") * SC_CORES + lax.axis_index("c")
        pltpu.sync_copy(d_hbm.at[pl.ds(wid * per_worker, per_worker)], idx_v)

        def gather(m):
            c, k, h = items[m]
            idx = idx_v.at[c, k, pl.ds(h * SC_GATHER_ROWS, SC_GATHER_ROWS)]
            return pltpu.async_copy(y_hbm.at[idx], bufs[m % nbuf], sem_g.at[m % nbuf])

        def write(m):
            c, k, h = items[m]
            rows = pl.ds((wid * per_worker + c) * chunk + h * SC_GATHER_ROWS, SC_GATHER_ROWS)
            return pltpu.async_copy(bufs[m % nbuf], yt_hbm.at[k, rows], sem_w.at[m % nbuf])

        gathers = {m: gather(m) for m in range(min(nbuf - 1, len(items)))}
        writes = {}
        for m in range(len(items)):
            gathers.pop(m).wait()
            writes[m] = write(m)
            nxt = m + nbuf - 1
            if nxt < len(items):
                if m >= 1:
                    writes.pop(m - 1).wait()
                gathers[nxt] = gather(nxt)
        for m in sorted(writes):
            writes.pop(m).wait()

    return run(y, dest3)


def _combine_kernel(yt_ref, wt_ref, xb_ref, mod_ref, o_ref):
    half = yt_ref.shape[2]
    wt = wt_ref[...].T
    lo, hi = _unpack_halves(yt_ref[0])
    r_lo = lo * wt[:, 0:1]
    r_hi = hi * wt[:, 0:1]
    for k in range(1, TOP_K):
        lo, hi = _unpack_halves(yt_ref[k])
        r_lo = r_lo + lo * wt[:, k:k + 1]
        r_hi = r_hi + hi * wt[:, k:k + 1]
    gate = mod_ref[0][5:6]
    o_ref[:, :half] = xb_ref[:, :half] + gate[:, :half] * r_lo
    o_ref[:, half:] = xb_ref[:, half:] + gate[:, half:] * r_hi


def _combine_part_kernel(yt_ref, wt_ref, xb_ref, mod_ref, prev_ref, o_ref):
    del prev_ref
    _combine_kernel(yt_ref, wt_ref, xb_ref, mod_ref, o_ref)


def _combine(yt_part, w_t, xb, mod3, seq, first_tile, prev):
    n, d = xb.shape
    tm = TM_COMBINE
    tiles_per_seq = seq // tm
    tiles = yt_part.shape[1] // tm
    in_specs = [
        pl.BlockSpec((TOP_K, tm, d // 2), lambda i: (0, i, 0)),
        pl.BlockSpec((TOP_K, tm), lambda i: (0, i + first_tile)),
        pl.BlockSpec((tm, d), lambda i: (i + first_tile, 0)),
        pl.BlockSpec((1, N_MOD, d), lambda i: ((i + first_tile) // tiles_per_seq, 0, 0)),
    ]
    args = [yt_part, w_t, xb, mod3]
    if prev is not None:
        in_specs.append(pl.BlockSpec(memory_space=pl.ANY))
        args.append(prev)
    return pl.pallas_call(
        _combine_kernel if prev is None else _combine_part_kernel,
        out_shape=jax.ShapeDtypeStruct((n, d), F32),
        grid=(tiles,),
        in_specs=in_specs,
        out_specs=pl.BlockSpec((tm, d), lambda i: (i + first_tile, 0)),
        input_output_aliases={} if prev is None else {len(args) - 1: 0},
        compiler_params=pltpu.CompilerParams(vmem_limit_bytes=VMEM_LIMIT),
        name="combine",
    )(*args)


def _rotary_tables(seq):
    half = RET_DK // 2
    inv = ROPE_BASE ** (-np.arange(half, dtype=np.float64) / half)
    ang = np.arange(seq, dtype=np.float64)[:, None] * inv[None, :]
    cos = np.cos(ang).astype(np.float32)
    sin = np.sin(ang).astype(np.float32)
    return (jnp.asarray(np.concatenate([cos, cos], axis=-1)),
            jnp.asarray(np.concatenate([-sin, sin], axis=-1)))


def kernel(x, c, w_ada, b_ada, g_mix, w_in, q_gain, k_gain, w_out, g_ffn, w_router, router_bias,
           w_gate, w_up, w_down, ws_gate, ws_up, ws_down):
    bsz, seq, d = x.shape
    n = bsz * seq
    depth = w_ada.shape[0]
    cos_full, sin_signed = _rotary_tables(seq)
    log_g = jnp.asarray(np.log1p(-np.exp2(-5.0 - np.arange(RET_HEADS, dtype=np.float64))).astype(np.float32))
    ret_w = RET_HEADS * RET_DK
    x2 = x.reshape(n, d)
    for l in range(depth):
        mod3 = _adaln(c, w_ada[l], b_ada[l]).reshape(bsz, N_MOD, d)
        proj = _inproj(x2, mod3, g_mix[l].reshape(1, d), w_in[l], cos_full, sin_signed, seq)
        proj3 = proj.reshape(bsz, seq, IN_COLS)
        ret = _retention(log_g, proj3)
        qg2 = jnp.tile(q_gain[l].reshape(1, MOBA_DH), (1, 2))
        kg2 = jnp.tile(k_gain[l].reshape(1, MOBA_DH), (1, 2))
        moba = _moba(proj3, qg2, kg2)
        wo = w_out[l].astype(BF16)
        xb, h2, e_idx, w_k, rank, cnt = _mid(
            ret.reshape(n, ret_w), moba.reshape(n, MOBA_HEADS * MOBA_DH), x2, mod3,
            wo[:ret_w], wo[ret_w:], g_ffn[l].reshape(1, d),
            w_router[l].T.astype(BF16), router_bias[l].reshape(N_EXPERTS, 1),
            ws_gate[l].astype(BF16), ws_up[l].astype(BF16), ws_down[l].astype(BF16), seq)
        nblk = n * TOP_K // FFN_BLOCK + N_EXPERTS
        tab = _sched(cnt, nblk)
        dest3 = _dest(tab, e_idx, rank)
        xs = _sc_dispatch(h2, dest3, nblk * FFN_BLOCK)
        y = _ffn(tab, xs, w_gate[l], w_up[l], w_down[l])
        chunks = (n // SC_CHUNK) // MOE_PARTS
        tiles = (n // TM_COMBINE) // MOE_PARTS
        x2 = None
        for part in range(MOE_PARTS):
            yt = _sc_gather(y, dest3[part * chunks:(part + 1) * chunks])
            x2 = _combine(yt, w_k, xb, mod3, seq, part * tiles, x2)
    return x2.reshape(bsz, seq, d)
```

```python
import functools

import numpy as np
import jax
import jax.numpy as jnp
from jax import lax
from jax.experimental import pallas as pl
from jax.experimental.pallas import tpu as pltpu
from jax.experimental.pallas import tpu_sc as plsc

F32 = jnp.float32
BF16 = jnp.bfloat16
I32 = jnp.int32

D_MODEL = 1024
RET_HEADS = 4
RET_DK = 128
MOBA_HEADS = 8
MOBA_DH = 64
MOBA_BLOCK = 256
MOBA_TOPK = 3
ROPE_BASE = 10000.0
N_EXPERTS = 256
TOP_K = 8
N_GROUPS = 8
TOPK_GROUPS = 4
GROUP_SIZE = N_EXPERTS // N_GROUPS
EXPERT_FF = 256
ROUTED_SCALE = 2.5
N_MOD = 6
EPS = 1e-6
IN_COLS = 3584

LANES = 128
RET_CHUNK = 256
TM_PROJ = 512
TM_COMBINE = 512
MOBA_ONES_ROWS = 16
SC_CORES = 2
SC_SUBCORES = 16
SC_CHUNK = 128
SC_GATHER_ROWS = 64
FFN_BLOCK = 512
FFN_GRAIN = 128
FFN_LOOKAHEAD = 5
FFN_WEIGHT_SLOTS = 3
FFN_OUT_SLOTS = 3
VMEM_LIMIT = 56 * 1024 * 1024

NEG_INF = float("-inf")
LOG2_E = 1.4426950408889634


def _silu(x):
    return x * jax.nn.sigmoid(x)


def _nt_dot(a, b):
    return lax.dot_general(a, b, (((1,), (1,)), ((), ())), preferred_element_type=F32)


def _tn_dot(a, b):
    return lax.dot_general(a, b, (((0,), (0,)), ((), ())), preferred_element_type=F32)


def _dot(a, b):
    return jnp.dot(a, b, preferred_element_type=F32)


HI_MASK = -65536


def _pack_halves(v):
    w = v.shape[1] // 2
    lo = lax.bitcast_convert_type(v[:, :w].astype(BF16).astype(F32), I32)
    hi = lax.bitcast_convert_type(v[:, w:].astype(BF16).astype(F32), I32)
    return lax.shift_right_logical(lo, 16) | (hi & HI_MASK)


def _unpack_halves(u):
    lo = lax.bitcast_convert_type(lax.shift_left(u, 16), F32)
    hi = lax.bitcast_convert_type(u & HI_MASK, F32)
    return lo, hi


def _adaln_kernel(c_ref, w_ref, b_ref, o_ref):
    s = _silu(c_ref[...])
    o_ref[...] = _dot(s.astype(BF16), w_ref[...].astype(BF16)) + b_ref[...]


def _adaln(c, w_ada, b_ada):
    bsz, d = c.shape
    ncol = w_ada.shape[1]
    tn = 1024
    return pl.pallas_call(
        _adaln_kernel,
        out_shape=jax.ShapeDtypeStruct((bsz, ncol), F32),
        grid=(ncol // tn,),
        in_specs=[
            pl.BlockSpec((bsz, d), lambda j: (0, 0)),
            pl.BlockSpec((d, tn), lambda j: (0, j)),
            pl.BlockSpec((1, tn), lambda j: (0, j)),
        ],
        out_specs=pl.BlockSpec((bsz, tn), lambda j: (0, j)),
        compiler_params=pltpu.CompilerParams(vmem_limit_bytes=VMEM_LIMIT),
        name="adaln",
    )(c, w_ada, b_ada.reshape(1, ncol))


def _inproj_kernel(x_ref, mod_ref, g_ref, w_ref, cos_ref, sin_ref, o_ref):
    x = x_ref[...]
    ms = jnp.mean(x * x, axis=-1, keepdims=True)
    m = mod_ref[0]
    h = (x * lax.rsqrt(ms + EPS) * g_ref[...]) * (1.0 + m[1:2]) + m[0:1]
    hb = h.astype(BF16)
    cosf = cos_ref[...]
    sinf = sin_ref[...]
    k_scale = RET_DK ** -0.5
    width = RET_HEADS * RET_DK
    for ci in range(IN_COLS // width):
        acc = _dot(hb, w_ref[:, ci * width:(ci + 1) * width])
        if ci < 2:
            for hh in range(RET_HEADS):
                xh = acc[:, hh * RET_DK:(hh + 1) * RET_DK]
                r = xh * cosf + pltpu.roll(xh, RET_DK // 2, axis=1) * sinf
                if ci == 1:
                    r = r * k_scale
                o_ref[:, ci * width + hh * RET_DK:ci * width + (hh + 1) * RET_DK] = r.astype(BF16)
        else:
            o_ref[:, ci * width:(ci + 1) * width] = acc.astype(BF16)


def _inproj(x2, mod3, g_mix, w_in_bf, cos_full, sin_signed, seq):
    n, d = x2.shape
    tm = TM_PROJ
    tiles_per_seq = seq // tm
    return pl.pallas_call(
        _inproj_kernel,
        out_shape=jax.ShapeDtypeStruct((n, IN_COLS), BF16),
        grid=(n // tm,),
        in_specs=[
            pl.BlockSpec((tm, d), lambda i: (i, 0)),
            pl.BlockSpec((1, N_MOD, d), lambda i: (i // tiles_per_seq, 0, 0)),
            pl.BlockSpec((1, d), lambda i: (0, 0)),
            pl.BlockSpec((d, IN_COLS), lambda i: (0, 0), pipeline_mode=pl.Buffered(1)),
            pl.BlockSpec((tm, LANES), lambda i: (i % tiles_per_seq, 0)),
            pl.BlockSpec((tm, LANES), lambda i: (i % tiles_per_seq, 0)),
        ],
        out_specs=pl.BlockSpec((tm, IN_COLS), lambda i: (i, 0)),
        compiler_params=pltpu.CompilerParams(vmem_limit_bytes=VMEM_LIMIT),
        name="inproj",
    )(x2, mod3, g_mix, w_in_bf, cos_full, sin_signed)


def _ret_kernel(lg_ref, q_ref, k_ref, v_ref, g_ref, o_ref):
    seq = q_ref.shape[1]
    c = RET_CHUNK
    lg = lg_ref[pl.program_id(1)]
    row = lax.broadcasted_iota(I32, (c, c), 0)
    col = lax.broadcasted_iota(I32, (c, c), 1)
    diff = (row - col).astype(F32)
    dmask = jnp.where(diff >= 0, jnp.exp(lg * jnp.maximum(diff, 0.0)), 0.0)
    idx = lax.broadcasted_iota(I32, (c, 1), 0).astype(F32)
    q_decay = jnp.exp(lg * (idx + 1.0))
    k_decay = jnp.exp(lg * (c - 1.0 - idx))
    chunk_decay = jnp.exp(jnp.full((1, 1), lg * c, F32))
    state = jnp.zeros((RET_DK, RET_DK), F32)
    for n in range(seq // c):
        rows = slice(n * c, (n + 1) * c)
        qn = q_ref[0, rows, :]
        kn = k_ref[0, rows, :]
        vn = v_ref[0, rows, :]
        scores = _nt_dot(qn, kn) * dmask
        inner = _dot(scores.astype(BF16), vn)
        qs = (qn.astype(F32) * q_decay).astype(BF16)
        cross = _dot(qs, state.astype(BF16))
        o = inner + cross
        o = o * lax.rsqrt(jnp.mean(o * o, axis=-1, keepdims=True) + EPS)
        gn = g_ref[0, rows, :].astype(F32)
        o_ref[0, rows, :] = (_silu(gn) * o).astype(BF16)
        ks = (kn.astype(F32) * k_decay).astype(BF16)
        state = state * chunk_decay + _tn_dot(ks, vn)


def _moba_kernel(q_ref, k_ref, v_ref, qg_ref, kg_ref, o_ref, qt_s, ka_s, kb_s, vta_s, vtb_s):
    seq = q_ref.shape[1]
    lb = MOBA_BLOCK
    nb = seq // lb
    lane = lax.broadcasted_iota(I32, (1, LANES), 1)
    is_a = lane < MOBA_DH

    def head_norm(xf, gain):
        sq = xf * xf
        s_a = jnp.sum(jnp.where(is_a, sq, 0.0), axis=-1, keepdims=True)
        s_b = jnp.sum(jnp.where(is_a, 0.0, sq), axis=-1, keepdims=True)
        inv = jnp.where(is_a, lax.rsqrt(s_a / MOBA_DH + EPS), lax.rsqrt(s_b / MOBA_DH + EPS))
        return xf * inv * gain

    qg = qg_ref[...]
    kg = kg_ref[...]
    k_means = []
    for j in range(nb):
        rows = slice(j * lb, (j + 1) * lb)
        kf = head_norm(k_ref[0, rows, :].astype(F32), kg)
        ka_s[rows, :] = jnp.where(is_a, kf, 0.0).astype(BF16)
        kb_s[rows, :] = jnp.where(is_a, 0.0, kf).astype(BF16)
        k_means.append(jnp.mean(kf, axis=0, keepdims=True))
        qf = head_norm(q_ref[0, rows, :].astype(F32), qg)
        qt_s[:, rows] = (qf * (MOBA_DH ** -0.5 * LOG2_E)).T.astype(BF16)
        vt = v_ref[0, rows, :].astype(F32).T
        ones = jnp.ones((MOBA_ONES_ROWS, lb), F32)
        vta_s[:, rows] = jnp.concatenate([vt[:MOBA_DH], ones], axis=0).astype(BF16)
        vtb_s[:, rows] = jnp.concatenate([vt[MOBA_DH:], ones], axis=0).astype(BF16)
    k_mean = jnp.concatenate(k_means + [jnp.zeros((16 - nb, LANES), F32)], axis=0)
    k_mean_h = (jnp.where(is_a, k_mean, 0.0).astype(BF16), jnp.where(is_a, 0.0, k_mean).astype(BF16))
    k_s = (ka_s, kb_s)
    vt_s = (vta_s, vtb_s)

    r_loc = lax.broadcasted_iota(I32, (lb, lb), 0)
    c_loc = lax.broadcasted_iota(I32, (lb, lb), 1)
    causal = r_loc <= c_loc

    for i in range(nb):
        cols = slice(i * lb, (i + 1) * lb)
        qt = qt_s[:, cols]
        outs = []
        for hx in range(2):
            bias = [None] * i
            if i > MOBA_TOPK:
                gate = _dot(k_mean_h[hx], qt)
                g = [gate[j:j + 1, :] for j in range(i)]
                for j in range(i):
                    rank = jnp.zeros((1, lb), F32)
                    for j2 in range(i):
                        if j2 == j:
                            continue
                        beats = (g[j2] >= g[j]) if j2 < j else (g[j2] > g[j])
                        rank = rank + jnp.where(beats, 1.0, 0.0)
                    bias[j] = jnp.where(rank < float(MOBA_TOPK), 0.0, NEG_INF)
            pieces = []
            for j in range(i + 1):
                s = _dot(k_s[hx][j * lb:(j + 1) * lb, :], qt)
                if j == i:
                    s = jnp.where(causal, s, NEG_INF)
                elif bias[j] is not None:
                    s = s + bias[j]
                pieces.append(s)
            mx = jnp.max(pieces[0], axis=0, keepdims=True)
            for s in pieces[1:]:
                mx = jnp.maximum(mx, jnp.max(s, axis=0, keepdims=True))
            acc = jnp.zeros((MOBA_DH + MOBA_ONES_ROWS, lb), F32)
            for j, s in enumerate(pieces):
                p = jnp.exp2(s - mx).astype(BF16)
                acc = acc + _dot(vt_s[hx][:, j * lb:(j + 1) * lb], p)
            outs.append(acc[:MOBA_DH] / acc[MOBA_DH:MOBA_DH + 1, :])
        o_ref[0, cols, :] = jnp.concatenate(outs, axis=0).T.astype(BF16)


def _mixers_kernel(lg_ref, rq_ref, rk_ref, rv_ref, rg_ref, mq_ref, mk_ref, mv_ref, qg_ref, kg_ref,
                   ret_ref, moba_ref, *moba_scratch):
    _ret_kernel(lg_ref, rq_ref, rk_ref, rv_ref, rg_ref, ret_ref)
    _moba_kernel(mq_ref, mk_ref, mv_ref, qg_ref, kg_ref, moba_ref, *moba_scratch)


def _mixers(log_g, proj3, qg2, kg2):
    bsz, seq, _ = proj3.shape
    pairs = MOBA_HEADS // 2
    assert pairs == RET_HEADS and RET_DK == LANES
    blk = (1, seq, LANES)
    base = 4 * RET_HEADS

    def col(first):
        return pl.BlockSpec(blk, lambda b, p, lg: (b, 0, first + p))

    return pl.pallas_call(
        _mixers_kernel,
        out_shape=(jax.ShapeDtypeStruct((bsz, seq, RET_HEADS * RET_DK), BF16),
                   jax.ShapeDtypeStruct((bsz, seq, MOBA_HEADS * MOBA_DH), BF16)),
        grid_spec=pltpu.PrefetchScalarGridSpec(
            num_scalar_prefetch=1,
            grid=(bsz, pairs),
            in_specs=[
                col(0), col(RET_HEADS), col(2 * RET_HEADS), col(3 * RET_HEADS),
                col(base), col(base + pairs), col(base + 2 * pairs),
                pl.BlockSpec((1, LANES), lambda b, p, lg: (0, 0)),
                pl.BlockSpec((1, LANES), lambda b, p, lg: (0, 0)),
            ],
            out_specs=(col(0), col(0)),
            scratch_shapes=[
                pltpu.VMEM((LANES, seq), BF16),
                pltpu.VMEM((seq, LANES), BF16),
                pltpu.VMEM((seq, LANES), BF16),
                pltpu.VMEM((MOBA_DH + MOBA_ONES_ROWS, seq), BF16),
                pltpu.VMEM((MOBA_DH + MOBA_ONES_ROWS, seq), BF16),
            ],
        ),
        compiler_params=pltpu.CompilerParams(vmem_limit_bytes=VMEM_LIMIT),
        name="mixers",
    )(log_g, *([proj3] * 7), qg2, kg2)


def _mid_kernel(ret_ref, moba_ref, x_ref, mod_ref, wo1_ref, wo2_ref, g_ref, wr_ref, rb_ref,
                wsg_ref, wsu_ref, wsd_ref,
                xb_ref, h2_ref, e_ref, w_ref, rk_ref, cnt_ref, carry_s):
    i = pl.program_id(0)
    tm = x_ref.shape[0]

    @pl.when(i == 0)
    def _():
        carry_s[...] = jnp.zeros_like(carry_s)

    m = mod_ref[0]
    mixed = _dot(ret_ref[...], wo1_ref[...]) + _dot(moba_ref[...], wo2_ref[...])
    x1 = x_ref[...] + m[2:3] * mixed
    ms = jnp.mean(x1 * x1, axis=-1, keepdims=True)
    h2 = (x1 * lax.rsqrt(ms + EPS) * g_ref[...]) * (1.0 + m[4:5]) + m[3:4]
    h2_ref[...] = _pack_halves(h2)
    h2b = h2.astype(BF16)

    hid = _silu(_dot(h2b, wsg_ref[...])) * _dot(h2b, wsu_ref[...])
    xb_ref[...] = x1 + m[5:6] * _dot(hid.astype(BF16), wsd_ref[...])

    scores = jax.nn.sigmoid(_nt_dot(wr_ref[...], h2b))
    biased = scores + rb_ref[...]
    grp = biased.reshape(N_GROUPS, GROUP_SIZE, tm)
    gi = lax.broadcasted_iota(I32, (N_GROUPS, GROUP_SIZE, tm), 1).astype(F32)
    top1 = jnp.max(grp, axis=1, keepdims=True)
    first = jnp.min(jnp.where(grp == top1, gi, float(GROUP_SIZE)), axis=1, keepdims=True)
    top2 = jnp.max(jnp.where(gi == first, NEG_INF, grp), axis=1, keepdims=True)
    gscore = (top1 + top2).reshape(N_GROUPS, tm)
    gidx = lax.broadcasted_iota(I32, (N_GROUPS, tm), 0)
    grank = jnp.zeros((N_GROUPS, tm), F32)
    for g2 in range(N_GROUPS):
        rowv = gscore[g2:g2 + 1, :]
        beats = (rowv > gscore) | ((rowv == gscore) & (g2 < gidx))
        grank = grank + jnp.where(beats, 1.0, 0.0)
    gsel = jnp.where(grank < float(TOPK_GROUPS), 1.0, 0.0)
    emask = jnp.broadcast_to(gsel.reshape(N_GROUPS, 1, tm), (N_GROUPS, GROUP_SIZE, tm)).reshape(N_EXPERTS, tm)
    choice = jnp.where(emask > 0.5, biased, NEG_INF)

    eidx = lax.broadcasted_iota(I32, (N_EXPERTS, tm), 0).astype(F32)
    remaining = choice
    e_rows = []
    for _k in range(TOP_K):
        mx = jnp.max(remaining, axis=0, keepdims=True)
        idx = jnp.min(jnp.where(remaining == mx, eidx, float(N_EXPERTS)), axis=0, keepdims=True)
        e_rows.append(idx)
        remaining = jnp.where(eidx == idx, NEG_INF, remaining)
    selmask = jnp.where(remaining != choice, 1.0, 0.0)

    tr = lax.broadcasted_iota(I32, (tm, tm), 0)
    tc = lax.broadcasted_iota(I32, (tm, tm), 1)
    upper = jnp.where(tr < tc, 1.0, 0.0).astype(BF16)
    prefix = _dot(selmask.astype(BF16), upper) + carry_s[...]
    w_rows = []
    r_rows = []
    for ek in e_rows:
        onehot = eidx == ek
        w_rows.append(jnp.sum(jnp.where(onehot, scores, 0.0), axis=0, keepdims=True))
        r_rows.append(jnp.sum(jnp.where(onehot, prefix, 0.0), axis=0, keepdims=True))
    wsum = w_rows[0]
    for wk in w_rows[1:]:
        wsum = wsum + wk
    carry_s[...] = carry_s[...] + jnp.sum(selmask, axis=1, keepdims=True)

    e_ref[...] = jnp.concatenate(e_rows, axis=0).astype(I32)
    w_ref[...] = jnp.concatenate([wk / wsum * ROUTED_SCALE for wk in w_rows], axis=0)
    rk_ref[...] = jnp.concatenate(r_rows, axis=0).astype(I32)
    cnt_ref[...] = carry_s[...].astype(I32)


def _mid(ret2, moba2, x2, mod3, wo1, wo2, g_ffn, wr_t, rbias, wsg, wsu, wsd, seq):
    n, d = x2.shape
    tm = TM_PROJ
    tiles_per_seq = seq // tm
    half = ret2.shape[1]
    ff = wsg.shape[1]
    const = lambda i: (0, 0)
    row = lambda i: (i, 0)
    colt = lambda i: (0, i)
    return pl.pallas_call(
        _mid_kernel,
        out_shape=(
            jax.ShapeDtypeStruct((n, d), F32),
            jax.ShapeDtypeStruct((n, d // 2), I32),
            jax.ShapeDtypeStruct((TOP_K, n), I32),
            jax.ShapeDtypeStruct((TOP_K, n), F32),
            jax.ShapeDtypeStruct((TOP_K, n), I32),
            jax.ShapeDtypeStruct((N_EXPERTS, 1), I32),
        ),
        grid=(n // tm,),
        in_specs=[
            pl.BlockSpec((tm, half), row),
            pl.BlockSpec((tm, half), row),
            pl.BlockSpec((tm, d), row),
            pl.BlockSpec((1, N_MOD, d), lambda i: (i // tiles_per_seq, 0, 0)),
            pl.BlockSpec((half, d), const),
            pl.BlockSpec((half, d), const),
            pl.BlockSpec((1, d), const),
            pl.BlockSpec((N_EXPERTS, d), const),
            pl.BlockSpec((N_EXPERTS, 1), const),
            pl.BlockSpec((d, ff), const),
            pl.BlockSpec((d, ff), const),
            pl.BlockSpec((ff, d), const),
        ],
        out_specs=(
            pl.BlockSpec((tm, d), row),
            pl.BlockSpec((tm, d // 2), row),
            pl.BlockSpec((TOP_K, tm), colt),
            pl.BlockSpec((TOP_K, tm), colt),
            pl.BlockSpec((TOP_K, tm), colt),
            pl.BlockSpec((N_EXPERTS, 1), const),
        ),
        scratch_shapes=[pltpu.VMEM((N_EXPERTS, 1), F32)],
        compiler_params=pltpu.CompilerParams(
            dimension_semantics=("arbitrary",), vmem_limit_bytes=VMEM_LIMIT),
        name="mid",
    )(ret2, moba2, x2, mod3, wo1, wo2, g_ffn, wr_t, rbias, wsg, wsu, wsd)


ROW_EXP = 0
ROW_VALID = 1
ROW_FIRST = 2
ROW_AHEAD = 3
ROW_SLOT = 4
ROW_HEAD = 5
ROW_START = 6
SCHED_ROWS = 8


def _sched_kernel(cnt_ref, tab_ref):
    ne = N_EXPERTS
    nblk = tab_ref.shape[1]
    shift = FFN_BLOCK.bit_length() - 1
    e_sub = lax.broadcasted_iota(I32, (ne, ne), 0)
    e_lane = lax.broadcasted_iota(I32, (ne, ne), 1)
    e_col = lax.broadcasted_iota(I32, (ne, 1), 0).astype(F32)
    ids_row = lax.broadcasted_iota(I32, (1, ne), 1).astype(F32) + 1.0

    def to_row(col):
        return jnp.sum(jnp.where(e_sub == e_lane, col, 0.0), axis=0, keepdims=True)

    def running_col(row):
        return jnp.sum(jnp.where(e_lane <= e_sub, row, 0.0), axis=1, keepdims=True)

    def running_row(col):
        return jnp.sum(jnp.where(e_sub <= e_lane, col, 0.0), axis=0, keepdims=True)

    cnt_i = cnt_ref[...]
    cnt_col = cnt_i.astype(F32)
    pad_col = lax.shift_left(lax.shift_right_logical(cnt_i + (FFN_BLOCK - 1), shift), shift).astype(F32)
    pad_row = to_row(pad_col)
    ends_col = running_col(pad_row)
    start_col = ends_col - pad_col
    start_row = running_row(pad_col) - pad_row
    nreal = jnp.sum(pad_row, axis=1, keepdims=True) * (1.0 / FFN_BLOCK)

    step = lax.broadcasted_iota(I32, (1, nblk), 1)
    g = step.astype(F32)
    row0 = jnp.minimum(g, nreal - 1.0) * float(FFN_BLOCK)
    exp_g = jnp.sum(jnp.where(ends_col <= row0, 1.0, 0.0), axis=0, keepdims=True)
    mine = e_col == exp_g

    def per_block(col):
        return jnp.sum(jnp.where(mine, col, 0.0), axis=0, keepdims=True)

    valid = jnp.clip(per_block(cnt_col + start_col) - row0, 0.0, float(FFN_BLOCK))
    first = jnp.where(jnp.logical_and(g < nreal, per_block(start_col) == row0), 1.0, 0.0)

    used_col = jnp.where(cnt_col > 0.0, 1.0, 0.0)
    used_row = to_row(used_col)
    ord_col = running_col(used_row) - 1.0
    ord_row = running_row(used_col) - 1.0

    def used_at(pos):
        hit = jnp.logical_and(used_row > 0.0, ord_row == pos)
        return jnp.sum(jnp.where(hit, ids_row, 0.0), axis=1, keepdims=True) - 1.0

    ahead_col = used_at(ord_col + float(FFN_WEIGHT_SLOTS - 1))
    slot_col = ord_col - FFN_WEIGHT_SLOTS * jnp.floor((ord_col + 0.5) * (1.0 / FFN_WEIGHT_SLOTS))
    head = jnp.where(step == 0, nreal, 0.0)
    for j in range(FFN_WEIGHT_SLOTS - 1):
        head = head + jnp.where(step == 1 + j, used_at(jnp.full((1, 1), float(j), F32)), 0.0)
    rows = [exp_g, valid, first, per_block(ahead_col), per_block(slot_col), head,
            jnp.concatenate([start_row, jnp.zeros((1, nblk - ne), F32)], axis=1),
            jnp.zeros((1, nblk), F32)]
    tab_ref[...] = jnp.concatenate(rows, axis=0).astype(I32)


def _sched(cnt, nblk):
    assert FFN_BLOCK & (FFN_BLOCK - 1) == 0 and nblk >= N_EXPERTS
    return pl.pallas_call(
        _sched_kernel,
        out_shape=jax.ShapeDtypeStruct((SCHED_ROWS, nblk), I32),
        compiler_params=pltpu.CompilerParams(vmem_limit_bytes=VMEM_LIMIT),
        name="sched",
    )(cnt)


def _dest_kernel(tab_ref, e_ref, rk_ref, o_ref):
    e = e_ref[...]

    def body(ex, acc):
        return acc + jnp.where(e == ex, tab_ref[ROW_START, ex], 0)

    dest = lax.fori_loop(0, N_EXPERTS, body, rk_ref[...], unroll=8)
    for ch in range(o_ref.shape[0]):
        o_ref[ch] = dest[:, ch * SC_CHUNK:(ch + 1) * SC_CHUNK]


def _dest(tab, e_idx, rank):
    k, n = e_idx.shape
    tn = 2048
    return pl.pallas_call(
        _dest_kernel,
        out_shape=jax.ShapeDtypeStruct((n // SC_CHUNK, k, SC_CHUNK), I32),
        grid_spec=pltpu.PrefetchScalarGridSpec(
            num_scalar_prefetch=1,
            grid=(n // tn,),
            in_specs=[
                pl.BlockSpec((k, tn), lambda i, s: (0, i)),
                pl.BlockSpec((k, tn), lambda i, s: (0, i)),
            ],
            out_specs=pl.BlockSpec((tn // SC_CHUNK, k, SC_CHUNK), lambda i, s: (i, 0, 0)),
        ),
        compiler_params=pltpu.CompilerParams(vmem_limit_bytes=VMEM_LIMIT),
        name="dest",
    )(tab, e_idx, rank)


def _sc_dispatch(h2p, dest3, total_rows):
    n, words = h2p.shape
    nchunks = n // SC_CHUNK
    per_worker = nchunks // (SC_CORES * SC_SUBCORES)
    mesh = plsc.VectorSubcoreMesh(core_axis_name="c", subcore_axis_name="s",
                                  num_cores=SC_CORES, num_subcores=SC_SUBCORES)

    @functools.partial(
        pl.kernel, mesh=mesh,
        out_type=jax.ShapeDtypeStruct((total_rows, words), I32),
        scratch_types=[
            pltpu.VMEM((TOP_K, SC_CHUNK), I32),
            pltpu.VMEM((SC_CHUNK, words), I32),
            pltpu.SemaphoreType.DMA,
        ],
        name="sc_dispatch",
    )
    def run(h_hbm, d_hbm, xs_hbm, idx_v, rows_v, sem):
        wid = lax.axis_index("s") * SC_CORES + lax.axis_index("c")

        @pl.loop(0, per_worker)
        def _(j):
            ch = wid * per_worker + j
            pltpu.sync_copy(d_hbm.at[ch], idx_v)
            pltpu.sync_copy(h_hbm.at[pl.ds(ch * SC_CHUNK, SC_CHUNK)], rows_v)
            copies = [pltpu.async_copy(rows_v, xs_hbm.at[idx_v.at[k]], sem) for k in range(TOP_K)]
            for cp in copies:
                cp.wait()

    return run(h2p, dest3)


def _ffn_kernel(tab_ref, x_hbm, wg_hbm, wu_hbm, wd_hbm, y_hbm,
                x_s, y_s, wg_s, wu_s, wd_s, sem_x, sem_y, sem):
    i = pl.program_id(0)
    nreal = tab_ref[ROW_HEAD, 0]
    rows_per = x_s.shape[1]
    half = x_s.shape[2]
    sizes = tuple(range(FFN_GRAIN, rows_per + 1, FFN_GRAIN))

    def rows_needed(g):
        return (tab_ref[ROW_VALID, g] + (FFN_GRAIN - 1)) // FFN_GRAIN * FFN_GRAIN

    def by_size(nrows, fn):
        for n in sizes:
            @pl.when(nrows == n)
            def _():
                fn(n)

    def row_copy(g, n):
        slot = lax.rem(g, FFN_LOOKAHEAD + 1)
        return pltpu.make_async_copy(
            x_hbm.at[pl.ds(g * rows_per, n)], x_s.at[slot, pl.ds(0, n)], sem_x.at[slot])

    def out_copy(g, n):
        slot = lax.rem(g, FFN_OUT_SLOTS)
        return pltpu.make_async_copy(
            y_s.at[slot, pl.ds(0, n)], y_hbm.at[pl.ds(g * rows_per, n)], sem_y.at[slot])

    def weight_copies(e, s):
        return (pltpu.make_async_copy(wg_hbm.at[e], wg_s.at[s], sem.at[s, 0]),
                pltpu.make_async_copy(wu_hbm.at[e], wu_s.at[s], sem.at[s, 1]),
                pltpu.make_async_copy(wd_hbm.at[e], wd_s.at[s], sem.at[s, 2]))

    @pl.when(i == 0)
    def _():
        for j in range(FFN_WEIGHT_SLOTS - 1):
            @pl.when(tab_ref[ROW_HEAD, 1 + j] >= 0)
            def _():
                for cp in weight_copies(tab_ref[ROW_HEAD, 1 + j], j):
                    cp.start()
        for g in range(FFN_LOOKAHEAD):
            @pl.when(g < nreal)
            def _():
                by_size(rows_needed(g), lambda n: row_copy(g, n).start())

    @pl.when(i < nreal)
    def _():
        s = tab_ref[ROW_SLOT, i]
        fetch = i + FFN_LOOKAHEAD

        @pl.when(fetch < nreal)
        def _():
            by_size(rows_needed(fetch), lambda n: row_copy(fetch, n).start())

        by_size(rows_needed(i), lambda n: row_copy(i, n).wait())

        @pl.when(tab_ref[ROW_FIRST, i] == 1)
        def _():
            for cp in weight_copies(tab_ref[ROW_EXP, i], s):
                cp.wait()

            ahead = tab_ref[ROW_AHEAD, i]

            @pl.when(ahead >= 0)
            def _():
                for cp in weight_copies(ahead, lax.rem(s + FFN_WEIGHT_SLOTS - 1, FFN_WEIGHT_SLOTS)):
                    cp.start()

        @pl.when(i >= FFN_OUT_SLOTS)
        def _():
            done = i - FFN_OUT_SLOTS
            by_size(rows_needed(done), lambda n: out_copy(done, n).wait())

        x_slot = lax.rem(i, FFN_LOOKAHEAD + 1)
        y_slot = lax.rem(i, FFN_OUT_SLOTS)
        valid = tab_ref[ROW_VALID, i]

        def expert_rows(n):
            r = lax.broadcasted_iota(I32, (n, 1), 0)
            x_lo, x_hi = _unpack_halves(jnp.where(r < valid, x_s[x_slot, pl.ds(0, n), :], 0))
            hg = _dot(x_lo, wg_s[s, :half, :]) + _dot(x_hi, wg_s[s, half:, :])
            hu = _dot(x_lo, wu_s[s, :half, :]) + _dot(x_hi, wu_s[s, half:, :])
            y_s[y_slot, pl.ds(0, n), :] = _pack_halves(_dot(_silu(hg) * hu, wd_s[s]))
            out_copy(i, n).start()

        by_size(rows_needed(i), expert_rows)

        @pl.when(i == nreal - 1)
        def _():
            for back in range(FFN_OUT_SLOTS):
                last = i - back

                @pl.when(last >= 0)
                def _():
                    by_size(rows_needed(last), lambda n: out_copy(last, n).wait())


def _ffn(tab, xs, w_gate, w_up, w_down):
    p, half = xs.shape
    d = 2 * half
    ff = w_gate.shape[2]
    return pl.pallas_call(
        _ffn_kernel,
        out_shape=jax.ShapeDtypeStruct((p, half), I32),
        grid_spec=pltpu.PrefetchScalarGridSpec(
            num_scalar_prefetch=1,
            grid=(tab.shape[1],),
            in_specs=[
                pl.BlockSpec(memory_space=pl.ANY),
                pl.BlockSpec(memory_space=pl.ANY),
                pl.BlockSpec(memory_space=pl.ANY),
                pl.BlockSpec(memory_space=pl.ANY),
            ],
            out_specs=pl.BlockSpec(memory_space=pl.ANY),
            scratch_shapes=[
                pltpu.VMEM((FFN_LOOKAHEAD + 1, FFN_BLOCK, half), I32),
                pltpu.VMEM((FFN_OUT_SLOTS, FFN_BLOCK, half), I32),
                pltpu.VMEM((FFN_WEIGHT_SLOTS, d, ff), F32),
                pltpu.VMEM((FFN_WEIGHT_SLOTS, d, ff), F32),
                pltpu.VMEM((FFN_WEIGHT_SLOTS, ff, d), F32),
                pltpu.SemaphoreType.DMA((FFN_LOOKAHEAD + 1,)),
                pltpu.SemaphoreType.DMA((FFN_OUT_SLOTS,)),
                pltpu.SemaphoreType.DMA((FFN_WEIGHT_SLOTS, 3)),
            ],
        ),
        compiler_params=pltpu.CompilerParams(
            dimension_semantics=("arbitrary",), vmem_limit_bytes=VMEM_LIMIT, has_side_effects=True),
        name="ffn",
    )(tab, xs, w_gate, w_up, w_down)


def _sc_gather(y, dest3):
    a, words = y.shape
    nchunks, _, chunk = dest3.shape
    n = nchunks * chunk
    per_worker = nchunks // (SC_CORES * SC_SUBCORES)
    nbuf = 3
    parts = chunk // SC_GATHER_ROWS
    items = [(c, k, h) for c in range(per_worker) for k in range(TOP_K) for h in range(parts)]
    mesh = plsc.VectorSubcoreMesh(core_axis_name="c", subcore_axis_name="s",
                                  num_cores=SC_CORES, num_subcores=SC_SUBCORES)

    @functools.partial(
        pl.kernel, mesh=mesh,
        out_type=jax.ShapeDtypeStruct((TOP_K, n, words), I32),
        scratch_types=[
            pltpu.VMEM((per_worker, TOP_K, chunk), I32),
            [pltpu.VMEM((SC_GATHER_ROWS, words), I32)] * nbuf,
            pltpu.SemaphoreType.DMA((nbuf,)),
            pltpu.SemaphoreType.DMA((nbuf,)),
        ],
        name="sc_gather",
    )
    def run(y_hbm, d_hbm, yt_hbm, idx_v, bufs, sem_g, sem_w):
        wid = lax.axis_index("s") * SC_CORES + lax.axis_index("c")
        pltpu.sync_copy(d_hbm.at[pl.ds(wid * per_worker, per_worker)], idx_v)

        def gather(m):
            c, k, h = items[m]
            idx = idx_v.at[c, k, pl.ds(h * SC_GATHER_ROWS, SC_GATHER_ROWS)]
            return pltpu.async_copy(y_hbm.at[idx], bufs[m % nbuf], sem_g.at[m % nbuf])

        def write(m):
            c, k, h = items[m]
            rows = pl.ds((wid * per_worker + c) * chunk + h * SC_GATHER_ROWS, SC_GATHER_ROWS)
            return pltpu.async_copy(bufs[m % nbuf], yt_hbm.at[k, rows], sem_w.at[m % nbuf])

        gathers = {m: gather(m) for m in range(min(nbuf - 1, len(items)))}
        writes = {}
        for m in range(len(items)):
            gathers.pop(m).wait()
            writes[m] = write(m)
            nxt = m + nbuf - 1
            if nxt < len(items):
                if m >= 1:
                    writes.pop(m - 1).wait()
                gathers[nxt] = gather(nxt)
        for m in sorted(writes):
            writes.pop(m).wait()

    return run(y, dest3)


def _combine_kernel(yt_ref, wt_ref, xb_ref, mod_ref, o_ref):
    half = yt_ref.shape[2]
    wt = wt_ref[...].T
    lo, hi = _unpack_halves(yt_ref[0])
    r_lo = lo * wt[:, 0:1]
    r_hi = hi * wt[:, 0:1]
    for k in range(1, TOP_K):
        lo, hi = _unpack_halves(yt_ref[k])
        r_lo = r_lo + lo * wt[:, k:k + 1]
        r_hi = r_hi + hi * wt[:, k:k + 1]
    gate = mod_ref[0][5:6]
    o_ref[:, :half] = xb_ref[:, :half] + gate[:, :half] * r_lo
    o_ref[:, half:] = xb_ref[:, half:] + gate[:, half:] * r_hi


def _combine(yt, w_k, xb, mod3, seq):
    n, d = xb.shape
    tm = TM_COMBINE
    tiles_per_seq = seq // tm
    return pl.pallas_call(
        _combine_kernel,
        out_shape=jax.ShapeDtypeStruct((n, d), F32),
        grid=(n // tm,),
        in_specs=[
            pl.BlockSpec((TOP_K, tm, d // 2), lambda i: (0, i, 0)),
            pl.BlockSpec((TOP_K, tm), lambda i: (0, i)),
            pl.BlockSpec((tm, d), lambda i: (i, 0)),
            pl.BlockSpec((1, N_MOD, d), lambda i: (i // tiles_per_seq, 0, 0)),
        ],
        out_specs=pl.BlockSpec((tm, d), lambda i: (i, 0)),
        compiler_params=pltpu.CompilerParams(vmem_limit_bytes=VMEM_LIMIT),
        name="combine",
    )(yt, w_k, xb, mod3)


def _rotary_tables(seq):
    half = RET_DK // 2
    inv = ROPE_BASE ** (-np.arange(half, dtype=np.float64) / half)
    ang = np.arange(seq, dtype=np.float64)[:, None] * inv[None, :]
    cos = np.cos(ang).astype(np.float32)
    sin = np.sin(ang).astype(np.float32)
    return (jnp.asarray(np.concatenate([cos, cos], axis=-1)),
            jnp.asarray(np.concatenate([-sin, sin], axis=-1)))


def kernel(x, c, w_ada, b_ada, g_mix, w_in, q_gain, k_gain, w_out, g_ffn, w_router, router_bias,
           w_gate, w_up, w_down, ws_gate, ws_up, ws_down):
    bsz, seq, d = x.shape
    n = bsz * seq
    depth = w_ada.shape[0]
    cos_full, sin_signed = _rotary_tables(seq)
    log_g = jnp.asarray(np.log1p(-np.exp2(-5.0 - np.arange(RET_HEADS, dtype=np.float64))).astype(np.float32))
    ret_w = RET_HEADS * RET_DK
    x2 = x.reshape(n, d)
    for l in range(depth):
        mod3 = _adaln(c, w_ada[l], b_ada[l]).reshape(bsz, N_MOD, d)
        proj = _inproj(x2, mod3, g_mix[l].reshape(1, d), w_in[l], cos_full, sin_signed, seq)
        proj3 = proj.reshape(bsz, seq, IN_COLS)
        qg2 = jnp.tile(q_gain[l].reshape(1, MOBA_DH), (1, 2))
        kg2 = jnp.tile(k_gain[l].reshape(1, MOBA_DH), (1, 2))
        ret, moba = _mixers(log_g, proj3, qg2, kg2)
        wo = w_out[l].astype(BF16)
        xb, h2, e_idx, w_k, rank, cnt = _mid(
            ret.reshape(n, ret_w), moba.reshape(n, MOBA_HEADS * MOBA_DH), x2, mod3,
            wo[:ret_w], wo[ret_w:], g_ffn[l].reshape(1, d),
            w_router[l].T.astype(BF16), router_bias[l].reshape(N_EXPERTS, 1),
            ws_gate[l].astype(BF16), ws_up[l].astype(BF16), ws_down[l].astype(BF16), seq)
        nblk = n * TOP_K // FFN_BLOCK + N_EXPERTS
        tab = _sched(cnt, nblk)
        dest3 = _dest(tab, e_idx, rank)
        xs = _sc_dispatch(h2, dest3, nblk * FFN_BLOCK)
        y = _ffn(tab, xs, w_gate[l], w_up[l], w_down[l])
        yt = _sc_gather(y, dest3)
        x2 = _combine(yt, w_k, xb, mod3, seq)
    return x2.reshape(bsz, seq, d)
```

```python
import functools

import numpy as np
import jax
import jax.numpy as jnp
from jax import lax
from jax.experimental import pallas as pl
from jax.experimental.pallas import tpu as pltpu
from jax.experimental.pallas import tpu_sc as plsc

F32 = jnp.float32
BF16 = jnp.bfloat16
I32 = jnp.int32

RET_HEADS = 4
RET_DK = 128
MOBA_HEADS = 8
MOBA_DH = 64
MOBA_BLOCK = 256
MOBA_TOPK = 3
ROPE_BASE = 10000.0
N_EXPERTS = 256
TOP_K = 8
N_GROUPS = 8
TOPK_GROUPS = 4
GROUP_SIZE = N_EXPERTS // N_GROUPS
ROUTED_SCALE = 2.5
N_MOD = 6
EPS = 1e-6
IN_COLS = 3584

LANES = 128
BF16_TILE_ROWS = 16
RET_CHUNK = 256
TN_ADALN = 1024
TM_PROJ = 512
TN_DEST = 2048
TM_COMBINE = 512
MOBA_ONES_ROWS = BF16_TILE_ROWS
SC_CORES = 2
SC_SUBCORES = 16
SC_CHUNK = 128
SC_GATHER_ROWS = 64
SC_GATHER_BUFS = 3
FFN_BLOCK = 1024
FFN_GRAIN = 128
FFN_LOOKAHEAD = 5
FFN_WEIGHT_SLOTS = 3
FFN_OUT_SLOTS = 3
VMEM_LIMIT = 56 * 1024 * 1024

NEG_INF = float("-inf")
LOG2_E = 1.4426950408889634


def _silu(x):
    return x * jax.nn.sigmoid(x)


def _nt_dot(a, b):
    return lax.dot_general(a, b, (((1,), (1,)), ((), ())), preferred_element_type=F32)


def _tn_dot(a, b):
    return lax.dot_general(a, b, (((0,), (0,)), ((), ())), preferred_element_type=F32)


def _dot(a, b):
    return jnp.dot(a, b, preferred_element_type=F32)


HI_MASK = -65536


def _pack_halves(v):
    w = v.shape[1] // 2
    lo = lax.bitcast_convert_type(v[:, :w].astype(BF16).astype(F32), I32)
    hi = lax.bitcast_convert_type(v[:, w:].astype(BF16).astype(F32), I32)
    return lax.shift_right_logical(lo, 16) | (hi & HI_MASK)


def _unpack_halves(u):
    lo = lax.bitcast_convert_type(lax.shift_left(u, 16), F32)
    hi = lax.bitcast_convert_type(u & HI_MASK, F32)
    return lo, hi


def _adaln_kernel(c_ref, w_ref, b_ref, o_ref):
    s = _silu(c_ref[...])
    o_ref[...] = _dot(s.astype(BF16), w_ref[...].astype(BF16)) + b_ref[...]


def _adaln(c, w_ada, b_ada):
    bsz, d = c.shape
    ncol = w_ada.shape[1]
    tn = TN_ADALN
    return pl.pallas_call(
        _adaln_kernel,
        out_shape=jax.ShapeDtypeStruct((bsz, ncol), F32),
        grid=(ncol // tn,),
        in_specs=[
            pl.BlockSpec((bsz, d), lambda j: (0, 0)),
            pl.BlockSpec((d, tn), lambda j: (0, j)),
            pl.BlockSpec((1, tn), lambda j: (0, j)),
        ],
        out_specs=pl.BlockSpec((bsz, tn), lambda j: (0, j)),
        compiler_params=pltpu.CompilerParams(vmem_limit_bytes=VMEM_LIMIT),
        name="adaln",
    )(c, w_ada, b_ada.reshape(1, ncol))


def _inproj_kernel(x_ref, mod_ref, g_ref, w_ref, cos_ref, sin_ref, o_ref):
    x = x_ref[...]
    ms = jnp.mean(x * x, axis=-1, keepdims=True)
    m = mod_ref[0]
    h = (x * lax.rsqrt(ms + EPS) * g_ref[...]) * (1.0 + m[1:2]) + m[0:1]
    hb = h.astype(BF16)
    cosf = cos_ref[...]
    sinf = sin_ref[...]
    k_scale = RET_DK ** -0.5
    width = RET_HEADS * RET_DK
    for ci in range(IN_COLS // width):
        acc = _dot(hb, w_ref[:, ci * width:(ci + 1) * width])
        if ci < 2:
            for hh in range(RET_HEADS):
                xh = acc[:, hh * RET_DK:(hh + 1) * RET_DK]
                r = xh * cosf + pltpu.roll(xh, RET_DK // 2, axis=1) * sinf
                if ci == 1:
                    r = r * k_scale
                o_ref[:, ci * width + hh * RET_DK:ci * width + (hh + 1) * RET_DK] = r.astype(BF16)
        else:
            o_ref[:, ci * width:(ci + 1) * width] = acc.astype(BF16)


def _inproj(x2, mod3, g_mix, w_in_bf, cos_full, sin_signed, seq):
    n, d = x2.shape
    tm = TM_PROJ
    tiles_per_seq = seq // tm
    return pl.pallas_call(
        _inproj_kernel,
        out_shape=jax.ShapeDtypeStruct((n, IN_COLS), BF16),
        grid=(n // tm,),
        in_specs=[
            pl.BlockSpec((tm, d), lambda i: (i, 0)),
            pl.BlockSpec((1, N_MOD, d), lambda i: (i // tiles_per_seq, 0, 0)),
            pl.BlockSpec((1, d), lambda i: (0, 0)),
            pl.BlockSpec((d, IN_COLS), lambda i: (0, 0), pipeline_mode=pl.Buffered(1)),
            pl.BlockSpec((tm, LANES), lambda i: (i % tiles_per_seq, 0)),
            pl.BlockSpec((tm, LANES), lambda i: (i % tiles_per_seq, 0)),
        ],
        out_specs=pl.BlockSpec((tm, IN_COLS), lambda i: (i, 0)),
        compiler_params=pltpu.CompilerParams(vmem_limit_bytes=VMEM_LIMIT),
        name="inproj",
    )(x2, mod3, g_mix, w_in_bf, cos_full, sin_signed)


def _ret_kernel(lg_ref, q_ref, k_ref, v_ref, g_ref, o_ref):
    seq = q_ref.shape[1]
    c = RET_CHUNK
    lg = lg_ref[pl.program_id(1)]
    row = lax.broadcasted_iota(I32, (c, c), 0)
    col = lax.broadcasted_iota(I32, (c, c), 1)
    diff = (row - col).astype(F32)
    dmask = jnp.where(diff >= 0, jnp.exp(lg * jnp.maximum(diff, 0.0)), 0.0)
    idx = lax.broadcasted_iota(I32, (c, 1), 0).astype(F32)
    q_decay = jnp.exp(lg * (idx + 1.0))
    k_decay = jnp.exp(lg * (c - 1.0 - idx))
    chunk_decay = jnp.exp(jnp.full((1, 1), lg * c, F32))
    state = jnp.zeros((RET_DK, RET_DK), F32)
    for n in range(seq // c):
        rows = slice(n * c, (n + 1) * c)
        qn = q_ref[0, rows, :]
        kn = k_ref[0, rows, :]
        vn = v_ref[0, rows, :]
        scores = _nt_dot(qn, kn) * dmask
        inner = _dot(scores.astype(BF16), vn)
        qs = (qn.astype(F32) * q_decay).astype(BF16)
        cross = _dot(qs, state.astype(BF16))
        o = inner + cross
        o = o * lax.rsqrt(jnp.mean(o * o, axis=-1, keepdims=True) + EPS)
        gn = g_ref[0, rows, :].astype(F32)
        o_ref[0, rows, :] = (_silu(gn) * o).astype(BF16)
        ks = (kn.astype(F32) * k_decay).astype(BF16)
        state = state * chunk_decay + _tn_dot(ks, vn)


def _moba_kernel(q_ref, k_ref, v_ref, qg_ref, kg_ref, o_ref, qt_s, ka_s, kb_s, vta_s, vtb_s):
    seq = q_ref.shape[1]
    lb = MOBA_BLOCK
    nb = seq // lb
    lane = lax.broadcasted_iota(I32, (1, LANES), 1)
    is_a = lane < MOBA_DH

    def head_norm(xf, gain):
        sq = xf * xf
        s_a = jnp.sum(jnp.where(is_a, sq, 0.0), axis=-1, keepdims=True)
        s_b = jnp.sum(jnp.where(is_a, 0.0, sq), axis=-1, keepdims=True)
        inv = jnp.where(is_a, lax.rsqrt(s_a / MOBA_DH + EPS), lax.rsqrt(s_b / MOBA_DH + EPS))
        return xf * inv * gain

    qg = qg_ref[...]
    kg = kg_ref[...]
    k_means = []
    for j in range(nb):
        rows = slice(j * lb, (j + 1) * lb)
        kf = head_norm(k_ref[0, rows, :].astype(F32), kg)
        ka_s[rows, :] = jnp.where(is_a, kf, 0.0).astype(BF16)
        kb_s[rows, :] = jnp.where(is_a, 0.0, kf).astype(BF16)
        k_means.append(jnp.mean(kf, axis=0, keepdims=True))
        qf = head_norm(q_ref[0, rows, :].astype(F32), qg)
        qt_s[:, rows] = (qf * (MOBA_DH ** -0.5 * LOG2_E)).T.astype(BF16)
        vt = v_ref[0, rows, :].astype(F32).T
        ones = jnp.ones((MOBA_ONES_ROWS, lb), F32)
        vta_s[:, rows] = jnp.concatenate([vt[:MOBA_DH], ones], axis=0).astype(BF16)
        vtb_s[:, rows] = jnp.concatenate([vt[MOBA_DH:], ones], axis=0).astype(BF16)
    k_mean = jnp.concatenate(k_means + [jnp.zeros((BF16_TILE_ROWS - nb, LANES), F32)], axis=0)
    k_mean_h = (jnp.where(is_a, k_mean, 0.0).astype(BF16), jnp.where(is_a, 0.0, k_mean).astype(BF16))
    k_s = (ka_s, kb_s)
    vt_s = (vta_s, vtb_s)

    r_loc = lax.broadcasted_iota(I32, (lb, lb), 0)
    c_loc = lax.broadcasted_iota(I32, (lb, lb), 1)
    causal = r_loc <= c_loc

    for i in range(nb):
        cols = slice(i * lb, (i + 1) * lb)
        qt = qt_s[:, cols]
        outs = []
        for hx in range(2):
            bias = [None] * i
            if i > MOBA_TOPK:
                gate = _dot(k_mean_h[hx], qt)
                g = [gate[j:j + 1, :] for j in range(i)]
                for j in range(i):
                    rank = jnp.zeros((1, lb), F32)
                    for j2 in range(i):
                        if j2 == j:
                            continue
                        beats = (g[j2] >= g[j]) if j2 < j else (g[j2] > g[j])
                        rank = rank + jnp.where(beats, 1.0, 0.0)
                    bias[j] = jnp.where(rank < float(MOBA_TOPK), 0.0, NEG_INF)
            pieces = []
            for j in range(i + 1):
                s = _dot(k_s[hx][j * lb:(j + 1) * lb, :], qt)
                if j == i:
                    s = jnp.where(causal, s, NEG_INF)
                elif bias[j] is not None:
                    s = s + bias[j]
                pieces.append(s)
            mx = jnp.max(pieces[0], axis=0, keepdims=True)
            for s in pieces[1:]:
                mx = jnp.maximum(mx, jnp.max(s, axis=0, keepdims=True))
            acc = jnp.zeros((MOBA_DH + MOBA_ONES_ROWS, lb), F32)
            for j, s in enumerate(pieces):
                p = jnp.exp2(s - mx).astype(BF16)
                acc = acc + _dot(vt_s[hx][:, j * lb:(j + 1) * lb], p)
            outs.append(acc[:MOBA_DH] / acc[MOBA_DH:MOBA_DH + 1, :])
        o_ref[0, cols, :] = jnp.concatenate(outs, axis=0).T.astype(BF16)


def _mixers_kernel(lg_ref, rq_ref, rk_ref, rv_ref, rg_ref, mq_ref, mk_ref, mv_ref, qg_ref, kg_ref,
                   ret_ref, moba_ref, *moba_scratch):
    _ret_kernel(lg_ref, rq_ref, rk_ref, rv_ref, rg_ref, ret_ref)
    _moba_kernel(mq_ref, mk_ref, mv_ref, qg_ref, kg_ref, moba_ref, *moba_scratch)


def _mixers(log_g, proj3, qg2, kg2):
    bsz, seq, _ = proj3.shape
    pairs = MOBA_HEADS // 2
    assert pairs == RET_HEADS and RET_DK == LANES
    blk = (1, seq, LANES)
    base = 4 * RET_HEADS

    def col(first):
        return pl.BlockSpec(blk, lambda b, p, lg: (b, 0, first + p))

    return pl.pallas_call(
        _mixers_kernel,
        out_shape=(jax.ShapeDtypeStruct((bsz, seq, RET_HEADS * RET_DK), BF16),
                   jax.ShapeDtypeStruct((bsz, seq, MOBA_HEADS * MOBA_DH), BF16)),
        grid_spec=pltpu.PrefetchScalarGridSpec(
            num_scalar_prefetch=1,
            grid=(bsz, pairs),
            in_specs=[
                col(0), col(RET_HEADS), col(2 * RET_HEADS), col(3 * RET_HEADS),
                col(base), col(base + pairs), col(base + 2 * pairs),
                pl.BlockSpec((1, LANES), lambda b, p, lg: (0, 0)),
                pl.BlockSpec((1, LANES), lambda b, p, lg: (0, 0)),
            ],
            out_specs=(col(0), col(0)),
            scratch_shapes=[
                pltpu.VMEM((LANES, seq), BF16),
                pltpu.VMEM((seq, LANES), BF16),
                pltpu.VMEM((seq, LANES), BF16),
                pltpu.VMEM((MOBA_DH + MOBA_ONES_ROWS, seq), BF16),
                pltpu.VMEM((MOBA_DH + MOBA_ONES_ROWS, seq), BF16),
            ],
        ),
        compiler_params=pltpu.CompilerParams(vmem_limit_bytes=VMEM_LIMIT),
        name="mixers",
    )(log_g, *([proj3] * 7), qg2, kg2)


def _mid_kernel(ret_ref, moba_ref, x_ref, mod_ref, wo1_ref, wo2_ref, g_ref, wr_ref, rb_ref,
                wsg_ref, wsu_ref, wsd_ref,
                xb_ref, h2_ref, e_ref, w_ref, rk_ref, cnt_ref, carry_s):
    i = pl.program_id(0)
    tm = x_ref.shape[0]

    @pl.when(i == 0)
    def _():
        carry_s[...] = jnp.zeros_like(carry_s)

    m = mod_ref[0]
    mixed = _dot(ret_ref[...], wo1_ref[...]) + _dot(moba_ref[...], wo2_ref[...])
    x1 = x_ref[...] + m[2:3] * mixed
    ms = jnp.mean(x1 * x1, axis=-1, keepdims=True)
    h2 = (x1 * lax.rsqrt(ms + EPS) * g_ref[...]) * (1.0 + m[4:5]) + m[3:4]
    h2_ref[...] = _pack_halves(h2)
    h2b = h2.astype(BF16)

    hid = _silu(_dot(h2b, wsg_ref[...])) * _dot(h2b, wsu_ref[...])
    xb_ref[...] = x1 + m[5:6] * _dot(hid.astype(BF16), wsd_ref[...])

    scores = jax.nn.sigmoid(_nt_dot(wr_ref[...], h2b))
    biased = scores + rb_ref[...]
    grp = biased.reshape(N_GROUPS, GROUP_SIZE, tm)
    gi = lax.broadcasted_iota(I32, (N_GROUPS, GROUP_SIZE, tm), 1).astype(F32)
    top1 = jnp.max(grp, axis=1, keepdims=True)
    first = jnp.min(jnp.where(grp == top1, gi, float(GROUP_SIZE)), axis=1, keepdims=True)
    top2 = jnp.max(jnp.where(gi == first, NEG_INF, grp), axis=1, keepdims=True)
    gscore = (top1 + top2).reshape(N_GROUPS, tm)
    gidx = lax.broadcasted_iota(I32, (N_GROUPS, tm), 0)
    grank = jnp.zeros((N_GROUPS, tm), F32)
    for g2 in range(N_GROUPS):
        rowv = gscore[g2:g2 + 1, :]
        beats = (rowv > gscore) | ((rowv == gscore) & (g2 < gidx))
        grank = grank + jnp.where(beats, 1.0, 0.0)
    gsel = jnp.where(grank < float(TOPK_GROUPS), 1.0, 0.0)
    emask = jnp.broadcast_to(gsel.reshape(N_GROUPS, 1, tm), (N_GROUPS, GROUP_SIZE, tm)).reshape(N_EXPERTS, tm)
    choice = jnp.where(emask > 0.5, biased, NEG_INF)

    eidx = lax.broadcasted_iota(I32, (N_EXPERTS, tm), 0).astype(F32)
    remaining = choice
    e_rows = []
    for _k in range(TOP_K):
        mx = jnp.max(remaining, axis=0, keepdims=True)
        idx = jnp.min(jnp.where(remaining == mx, eidx, float(N_EXPERTS)), axis=0, keepdims=True)
        e_rows.append(idx)
        remaining = jnp.where(eidx == idx, NEG_INF, remaining)
    selmask = jnp.where(remaining != choice, 1.0, 0.0)

    tr = lax.broadcasted_iota(I32, (tm, tm), 0)
    tc = lax.broadcasted_iota(I32, (tm, tm), 1)
    upper = jnp.where(tr < tc, 1.0, 0.0).astype(BF16)
    prefix = _dot(selmask.astype(BF16), upper) + carry_s[...]
    w_rows = []
    r_rows = []
    for ek in e_rows:
        onehot = eidx == ek
        w_rows.append(jnp.sum(jnp.where(onehot, scores, 0.0), axis=0, keepdims=True))
        r_rows.append(jnp.sum(jnp.where(onehot, prefix, 0.0), axis=0, keepdims=True))
    wsum = w_rows[0]
    for wk in w_rows[1:]:
        wsum = wsum + wk
    carry_s[...] = carry_s[...] + jnp.sum(selmask, axis=1, keepdims=True)

    e_ref[...] = jnp.concatenate(e_rows, axis=0).astype(I32)
    w_ref[...] = jnp.concatenate([wk / wsum * ROUTED_SCALE for wk in w_rows], axis=0)
    rk_ref[...] = jnp.concatenate(r_rows, axis=0).astype(I32)
    cnt_ref[...] = carry_s[...].astype(I32)


def _mid(ret2, moba2, x2, mod3, wo1, wo2, g_ffn, wr_t, rbias, wsg, wsu, wsd, seq):
    n, d = x2.shape
    tm = TM_PROJ
    tiles_per_seq = seq // tm
    half = ret2.shape[1]
    ff = wsg.shape[1]
    const = lambda i: (0, 0)
    row = lambda i: (i, 0)
    colt = lambda i: (0, i)
    return pl.pallas_call(
        _mid_kernel,
        out_shape=(
            jax.ShapeDtypeStruct((n, d), F32),
            jax.ShapeDtypeStruct((n, d // 2), I32),
            jax.ShapeDtypeStruct((TOP_K, n), I32),
            jax.ShapeDtypeStruct((TOP_K, n), F32),
            jax.ShapeDtypeStruct((TOP_K, n), I32),
            jax.ShapeDtypeStruct((N_EXPERTS, 1), I32),
        ),
        grid=(n // tm,),
        in_specs=[
            pl.BlockSpec((tm, half), row),
            pl.BlockSpec((tm, half), row),
            pl.BlockSpec((tm, d), row),
            pl.BlockSpec((1, N_MOD, d), lambda i: (i // tiles_per_seq, 0, 0)),
            pl.BlockSpec((half, d), const),
            pl.BlockSpec((half, d), const),
            pl.BlockSpec((1, d), const),
            pl.BlockSpec((N_EXPERTS, d), const),
            pl.BlockSpec((N_EXPERTS, 1), const),
            pl.BlockSpec((d, ff), const),
            pl.BlockSpec((d, ff), const),
            pl.BlockSpec((ff, d), const),
        ],
        out_specs=(
            pl.BlockSpec((tm, d), row),
            pl.BlockSpec((tm, d // 2), row),
            pl.BlockSpec((TOP_K, tm), colt),
            pl.BlockSpec((TOP_K, tm), colt),
            pl.BlockSpec((TOP_K, tm), colt),
            pl.BlockSpec((N_EXPERTS, 1), const),
        ),
        scratch_shapes=[pltpu.VMEM((N_EXPERTS, 1), F32)],
        compiler_params=pltpu.CompilerParams(
            dimension_semantics=("arbitrary",), vmem_limit_bytes=VMEM_LIMIT),
        name="mid",
    )(ret2, moba2, x2, mod3, wo1, wo2, g_ffn, wr_t, rbias, wsg, wsu, wsd)


ROW_EXP = 0
ROW_VALID = 1
ROW_FIRST = 2
ROW_AHEAD = 3
ROW_SLOT = 4
ROW_HEAD = 5
ROW_START = 6
SCHED_ROWS = 8


def _sched_kernel(cnt_ref, tab_ref):
    ne = N_EXPERTS
    nblk = tab_ref.shape[1]
    shift = FFN_BLOCK.bit_length() - 1
    e_sub = lax.broadcasted_iota(I32, (ne, ne), 0)
    e_lane = lax.broadcasted_iota(I32, (ne, ne), 1)
    e_col = lax.broadcasted_iota(I32, (ne, 1), 0).astype(F32)
    ids_row = lax.broadcasted_iota(I32, (1, ne), 1).astype(F32) + 1.0

    def to_row(col):
        return jnp.sum(jnp.where(e_sub == e_lane, col, 0.0), axis=0, keepdims=True)

    def running_col(row):
        return jnp.sum(jnp.where(e_lane <= e_sub, row, 0.0), axis=1, keepdims=True)

    def running_row(col):
        return jnp.sum(jnp.where(e_sub <= e_lane, col, 0.0), axis=0, keepdims=True)

    cnt_i = cnt_ref[...]
    cnt_col = cnt_i.astype(F32)
    pad_col = lax.shift_left(lax.shift_right_logical(cnt_i + (FFN_BLOCK - 1), shift), shift).astype(F32)
    pad_row = to_row(pad_col)
    ends_col = running_col(pad_row)
    start_col = ends_col - pad_col
    start_row = running_row(pad_col) - pad_row
    nreal = jnp.sum(pad_row, axis=1, keepdims=True) * (1.0 / FFN_BLOCK)

    step = lax.broadcasted_iota(I32, (1, nblk), 1)
    g = step.astype(F32)
    row0 = jnp.minimum(g, nreal - 1.0) * float(FFN_BLOCK)
    exp_g = jnp.sum(jnp.where(ends_col <= row0, 1.0, 0.0), axis=0, keepdims=True)
    mine = e_col == exp_g

    def per_block(col):
        return jnp.sum(jnp.where(mine, col, 0.0), axis=0, keepdims=True)

    valid = jnp.clip(per_block(cnt_col + start_col) - row0, 0.0, float(FFN_BLOCK))
    first = jnp.where(jnp.logical_and(g < nreal, per_block(start_col) == row0), 1.0, 0.0)

    used_col = jnp.where(cnt_col > 0.0, 1.0, 0.0)
    used_row = to_row(used_col)
    ord_col = running_col(used_row) - 1.0
    ord_row = running_row(used_col) - 1.0

    def used_at(pos):
        hit = jnp.logical_and(used_row > 0.0, ord_row == pos)
        return jnp.sum(jnp.where(hit, ids_row, 0.0), axis=1, keepdims=True) - 1.0

    ahead_col = used_at(ord_col + float(FFN_WEIGHT_SLOTS - 1))
    slot_col = ord_col - FFN_WEIGHT_SLOTS * jnp.floor((ord_col + 0.5) * (1.0 / FFN_WEIGHT_SLOTS))
    head = jnp.where(step == 0, nreal, 0.0)
    for j in range(FFN_WEIGHT_SLOTS - 1):
        head = head + jnp.where(step == 1 + j, used_at(jnp.full((1, 1), float(j), F32)), 0.0)
    rows = [exp_g, valid, first, per_block(ahead_col), per_block(slot_col), head,
            jnp.concatenate([start_row, jnp.zeros((1, nblk - ne), F32)], axis=1),
            jnp.zeros((1, nblk), F32)]
    tab_ref[...] = jnp.concatenate(rows, axis=0).astype(I32)


def _sched(cnt, nblk):
    assert FFN_BLOCK & (FFN_BLOCK - 1) == 0 and nblk >= N_EXPERTS
    return pl.pallas_call(
        _sched_kernel,
        out_shape=jax.ShapeDtypeStruct((SCHED_ROWS, nblk), I32),
        compiler_params=pltpu.CompilerParams(vmem_limit_bytes=VMEM_LIMIT),
        name="sched",
    )(cnt)


def _dest_kernel(tab_ref, e_ref, rk_ref, o_ref):
    e = e_ref[...]

    def body(ex, acc):
        return acc + jnp.where(e == ex, tab_ref[ROW_START, ex], 0)

    dest = lax.fori_loop(0, N_EXPERTS, body, rk_ref[...], unroll=8)
    for ch in range(o_ref.shape[0]):
        o_ref[ch] = dest[:, ch * SC_CHUNK:(ch + 1) * SC_CHUNK]


def _dest(tab, e_idx, rank):
    k, n = e_idx.shape
    tn = TN_DEST
    return pl.pallas_call(
        _dest_kernel,
        out_shape=jax.ShapeDtypeStruct((n // SC_CHUNK, k, SC_CHUNK), I32),
        grid_spec=pltpu.PrefetchScalarGridSpec(
            num_scalar_prefetch=1,
            grid=(n // tn,),
            in_specs=[
                pl.BlockSpec((k, tn), lambda i, s: (0, i)),
                pl.BlockSpec((k, tn), lambda i, s: (0, i)),
            ],
            out_specs=pl.BlockSpec((tn // SC_CHUNK, k, SC_CHUNK), lambda i, s: (i, 0, 0)),
        ),
        compiler_params=pltpu.CompilerParams(vmem_limit_bytes=VMEM_LIMIT),
        name="dest",
    )(tab, e_idx, rank)


def _sc_dispatch(h2p, dest3, total_rows):
    n, words = h2p.shape
    nchunks = n // SC_CHUNK
    per_worker = nchunks // (SC_CORES * SC_SUBCORES)
    mesh = plsc.VectorSubcoreMesh(core_axis_name="c", subcore_axis_name="s",
                                  num_cores=SC_CORES, num_subcores=SC_SUBCORES)

    @functools.partial(
        pl.kernel, mesh=mesh,
        out_type=jax.ShapeDtypeStruct((total_rows, words), I32),
        scratch_types=[
            pltpu.VMEM((TOP_K, SC_CHUNK), I32),
            pltpu.VMEM((SC_CHUNK, words), I32),
            pltpu.SemaphoreType.DMA,
        ],
        name="sc_dispatch",
    )
    def run(h_hbm, d_hbm, xs_hbm, idx_v, rows_v, sem):
        wid = lax.axis_index("s") * SC_CORES + lax.axis_index("c")

        @pl.loop(0, per_worker)
        def _(j):
            ch = wid * per_worker + j
            pltpu.sync_copy(d_hbm.at[ch], idx_v)
            pltpu.sync_copy(h_hbm.at[pl.ds(ch * SC_CHUNK, SC_CHUNK)], rows_v)
            copies = [pltpu.async_copy(rows_v, xs_hbm.at[idx_v.at[k]], sem) for k in range(TOP_K)]
            for cp in copies:
                cp.wait()

    return run(h2p, dest3)


def _ffn_kernel(tab_ref, x_hbm, wg_hbm, wu_hbm, wd_hbm, y_hbm,
                x_s, y_s, wg_s, wu_s, wd_s, sem_x, sem_y, sem):
    i = pl.program_id(0)
    nreal = tab_ref[ROW_HEAD, 0]
    rows_per = x_s.shape[1]
    half = x_s.shape[2]
    sizes = tuple(range(FFN_GRAIN, rows_per + 1, FFN_GRAIN))

    def rows_needed(g):
        return (tab_ref[ROW_VALID, g] + (FFN_GRAIN - 1)) // FFN_GRAIN * FFN_GRAIN

    def by_size(nrows, fn):
        for n in sizes:
            @pl.when(nrows == n)
            def _():
                fn(n)

    def row_copy(g, n):
        slot = lax.rem(g, FFN_LOOKAHEAD + 1)
        return pltpu.make_async_copy(
            x_hbm.at[pl.ds(g * rows_per, n)], x_s.at[slot, pl.ds(0, n)], sem_x.at[slot])

    def out_copy(g, n):
        slot = lax.rem(g, FFN_OUT_SLOTS)
        return pltpu.make_async_copy(
            y_s.at[slot, pl.ds(0, n)], y_hbm.at[pl.ds(g * rows_per, n)], sem_y.at[slot])

    def weight_copies(e, s):
        return (pltpu.make_async_copy(wg_hbm.at[e], wg_s.at[s], sem.at[s, 0]),
                pltpu.make_async_copy(wu_hbm.at[e], wu_s.at[s], sem.at[s, 1]),
                pltpu.make_async_copy(wd_hbm.at[e], wd_s.at[s], sem.at[s, 2]))

    @pl.when(i == 0)
    def _():
        for j in range(FFN_WEIGHT_SLOTS - 1):
            @pl.when(tab_ref[ROW_HEAD, 1 + j] >= 0)
            def _():
                for cp in weight_copies(tab_ref[ROW_HEAD, 1 + j], j):
                    cp.start()
        for g in range(FFN_LOOKAHEAD):
            @pl.when(g < nreal)
            def _():
                by_size(rows_needed(g), lambda n: row_copy(g, n).start())

    @pl.when(i < nreal)
    def _():
        s = tab_ref[ROW_SLOT, i]
        fetch = i + FFN_LOOKAHEAD

        @pl.when(fetch < nreal)
        def _():
            by_size(rows_needed(fetch), lambda n: row_copy(fetch, n).start())

        by_size(rows_needed(i), lambda n: row_copy(i, n).wait())

        @pl.when(tab_ref[ROW_FIRST, i] == 1)
        def _():
            for cp in weight_copies(tab_ref[ROW_EXP, i], s):
                cp.wait()

            ahead = tab_ref[ROW_AHEAD, i]

            @pl.when(ahead >= 0)
            def _():
                for cp in weight_copies(ahead, lax.rem(s + FFN_WEIGHT_SLOTS - 1, FFN_WEIGHT_SLOTS)):
                    cp.start()

        @pl.when(i >= FFN_OUT_SLOTS)
        def _():
            done = i - FFN_OUT_SLOTS
            by_size(rows_needed(done), lambda n: out_copy(done, n).wait())

        x_slot = lax.rem(i, FFN_LOOKAHEAD + 1)
        y_slot = lax.rem(i, FFN_OUT_SLOTS)
        valid = tab_ref[ROW_VALID, i]

        def expert_rows(n):
            r = lax.broadcasted_iota(I32, (n, 1), 0)
            x_lo, x_hi = _unpack_halves(jnp.where(r < valid, x_s[x_slot, pl.ds(0, n), :], 0))
            hg = _dot(x_lo, wg_s[s, :half, :]) + _dot(x_hi, wg_s[s, half:, :])
            hu = _dot(x_lo, wu_s[s, :half, :]) + _dot(x_hi, wu_s[s, half:, :])
            y_s[y_slot, pl.ds(0, n), :] = _pack_halves(_dot(_silu(hg) * hu, wd_s[s]))
            out_copy(i, n).start()

        by_size(rows_needed(i), expert_rows)

        @pl.when(i == nreal - 1)
        def _():
            for back in range(FFN_OUT_SLOTS):
                last = i - back

                @pl.when(last >= 0)
                def _():
                    by_size(rows_needed(last), lambda n: out_copy(last, n).wait())


def _ffn(tab, xs, w_gate, w_up, w_down):
    p, half = xs.shape
    d = 2 * half
    ff = w_gate.shape[2]
    return pl.pallas_call(
        _ffn_kernel,
        out_shape=jax.ShapeDtypeStruct((p, half), I32),
        grid_spec=pltpu.PrefetchScalarGridSpec(
            num_scalar_prefetch=1,
            grid=(tab.shape[1],),
            in_specs=[
                pl.BlockSpec(memory_space=pl.ANY),
                pl.BlockSpec(memory_space=pl.ANY),
                pl.BlockSpec(memory_space=pl.ANY),
                pl.BlockSpec(memory_space=pl.ANY),
            ],
            out_specs=pl.BlockSpec(memory_space=pl.ANY),
            scratch_shapes=[
                pltpu.VMEM((FFN_LOOKAHEAD + 1, FFN_BLOCK, half), I32),
                pltpu.VMEM((FFN_OUT_SLOTS, FFN_BLOCK, half), I32),
                pltpu.VMEM((FFN_WEIGHT_SLOTS, d, ff), F32),
                pltpu.VMEM((FFN_WEIGHT_SLOTS, d, ff), F32),
                pltpu.VMEM((FFN_WEIGHT_SLOTS, ff, d), F32),
                pltpu.SemaphoreType.DMA((FFN_LOOKAHEAD + 1,)),
                pltpu.SemaphoreType.DMA((FFN_OUT_SLOTS,)),
                pltpu.SemaphoreType.DMA((FFN_WEIGHT_SLOTS, 3)),
            ],
        ),
        compiler_params=pltpu.CompilerParams(
            dimension_semantics=("arbitrary",), vmem_limit_bytes=VMEM_LIMIT, has_side_effects=True),
        name="ffn",
    )(tab, xs, w_gate, w_up, w_down)


def _sc_gather(y, dest3):
    a, words = y.shape
    nchunks, _, chunk = dest3.shape
    n = nchunks * chunk
    per_worker = nchunks // (SC_CORES * SC_SUBCORES)
    nbuf = SC_GATHER_BUFS
    parts = chunk // SC_GATHER_ROWS
    items = [(c, k, h) for c in range(per_worker) for k in range(TOP_K) for h in range(parts)]
    mesh = plsc.VectorSubcoreMesh(core_axis_name="c", subcore_axis_name="s",
                                  num_cores=SC_CORES, num_subcores=SC_SUBCORES)

    @functools.partial(
        pl.kernel, mesh=mesh,
        out_type=jax.ShapeDtypeStruct((TOP_K, n, words), I32),
        scratch_types=[
            pltpu.VMEM((per_worker, TOP_K, chunk), I32),
            [pltpu.VMEM((SC_GATHER_ROWS, words), I32)] * nbuf,
            pltpu.SemaphoreType.DMA((nbuf,)),
            pltpu.SemaphoreType.DMA((nbuf,)),
        ],
        name="sc_gather",
    )
    def run(y_hbm, d_hbm, yt_hbm, idx_v, bufs, sem_g, sem_w):
        wid = lax.axis_index("s") * SC_CORES + lax.axis_index("c")
        pltpu.sync_copy(d_hbm.at[pl.ds(wid * per_worker, per_worker)], idx_v)

        def gather(m):
            c, k, h = items[m]
            idx = idx_v.at[c, k, pl.ds(h * SC_GATHER_ROWS, SC_GATHER_ROWS)]
            return pltpu.async_copy(y_hbm.at[idx], bufs[m % nbuf], sem_g.at[m % nbuf])

        def write(m):
            c, k, h = items[m]
            rows = pl.ds((wid * per_worker + c) * chunk + h * SC_GATHER_ROWS, SC_GATHER_ROWS)
            return pltpu.async_copy(bufs[m % nbuf], yt_hbm.at[k, rows], sem_w.at[m % nbuf])

        gathers = {m: gather(m) for m in range(min(nbuf - 1, len(items)))}
        writes = {}
        for m in range(len(items)):
            gathers.pop(m).wait()
            writes[m] = write(m)
            nxt = m + nbuf - 1
            if nxt < len(items):
                if m >= 1:
                    writes.pop(m - 1).wait()
                gathers[nxt] = gather(nxt)
        for m in sorted(writes):
            writes.pop(m).wait()

    return run(y, dest3)


def _combine_kernel(yt_ref, wt_ref, xb_ref, mod_ref, o_ref):
    half = yt_ref.shape[2]
    wt = wt_ref[...].T
    lo, hi = _unpack_halves(yt_ref[0])
    r_lo = lo * wt[:, 0:1]
    r_hi = hi * wt[:, 0:1]
    for k in range(1, TOP_K):
        lo, hi = _unpack_halves(yt_ref[k])
        r_lo = r_lo + lo * wt[:, k:k + 1]
        r_hi = r_hi + hi * wt[:, k:k + 1]
    gate = mod_ref[0][5:6]
    o_ref[:, :half] = xb_ref[:, :half] + gate[:, :half] * r_lo
    o_ref[:, half:] = xb_ref[:, half:] + gate[:, half:] * r_hi


def _combine(yt, w_k, xb, mod3, seq):
    n, d = xb.shape
    tm = TM_COMBINE
    tiles_per_seq = seq // tm
    return pl.pallas_call(
        _combine_kernel,
        out_shape=jax.ShapeDtypeStruct((n, d), F32),
        grid=(n // tm,),
        in_specs=[
            pl.BlockSpec((TOP_K, tm, d // 2), lambda i: (0, i, 0)),
            pl.BlockSpec((TOP_K, tm), lambda i: (0, i)),
            pl.BlockSpec((tm, d), lambda i: (i, 0)),
            pl.BlockSpec((1, N_MOD, d), lambda i: (i // tiles_per_seq, 0, 0)),
        ],
        out_specs=pl.BlockSpec((tm, d), lambda i: (i, 0)),
        compiler_params=pltpu.CompilerParams(vmem_limit_bytes=VMEM_LIMIT),
        name="combine",
    )(yt, w_k, xb, mod3)


def _rotary_tables(seq):
    half = RET_DK // 2
    inv = ROPE_BASE ** (-np.arange(half, dtype=np.float64) / half)
    ang = np.arange(seq, dtype=np.float64)[:, None] * inv[None, :]
    cos = np.cos(ang).astype(np.float32)
    sin = np.sin(ang).astype(np.float32)
    return (jnp.asarray(np.concatenate([cos, cos], axis=-1)),
            jnp.asarray(np.concatenate([-sin, sin], axis=-1)))


def kernel(x, c, w_ada, b_ada, g_mix, w_in, q_gain, k_gain, w_out, g_ffn, w_router, router_bias,
           w_gate, w_up, w_down, ws_gate, ws_up, ws_down):
    bsz, seq, d = x.shape
    n = bsz * seq
    depth = w_ada.shape[0]
    cos_full, sin_signed = _rotary_tables(seq)
    log_g = jnp.asarray(np.log1p(-np.exp2(-5.0 - np.arange(RET_HEADS, dtype=np.float64))).astype(np.float32))
    ret_w = RET_HEADS * RET_DK
    x2 = x.reshape(n, d)
    for l in range(depth):
        mod3 = _adaln(c, w_ada[l], b_ada[l]).reshape(bsz, N_MOD, d)
        proj = _inproj(x2, mod3, g_mix[l].reshape(1, d), w_in[l], cos_full, sin_signed, seq)
        proj3 = proj.reshape(bsz, seq, IN_COLS)
        qg2 = jnp.tile(q_gain[l].reshape(1, MOBA_DH), (1, 2))
        kg2 = jnp.tile(k_gain[l].reshape(1, MOBA_DH), (1, 2))
        ret, moba = _mixers(log_g, proj3, qg2, kg2)
        wo = w_out[l].astype(BF16)
        xb, h2, e_idx, w_k, rank, cnt = _mid(
            ret.reshape(n, ret_w), moba.reshape(n, MOBA_HEADS * MOBA_DH), x2, mod3,
            wo[:ret_w], wo[ret_w:], g_ffn[l].reshape(1, d),
            w_router[l].T.astype(BF16), router_bias[l].reshape(N_EXPERTS, 1),
            ws_gate[l].astype(BF16), ws_up[l].astype(BF16), ws_down[l].astype(BF16), seq)
        nblk = n * TOP_K // FFN_BLOCK + N_EXPERTS
        tab = _sched(cnt, nblk)
        dest3 = _dest(tab, e_idx, rank)
        xs = _sc_dispatch(h2, dest3, nblk * FFN_BLOCK)
        y = _ffn(tab, xs, w_gate[l], w_up[l], w_down[l])
        yt = _sc_gather(y, dest3)
        x2 = _combine(yt, w_k, xb, mod3, seq)
    return x2.reshape(bsz, seq, d)
```

```python
import functools

import numpy as np
import jax
import jax.numpy as jnp
from jax import lax
from jax.experimental import pallas as pl
from jax.experimental.pallas import tpu as pltpu
from jax.experimental.pallas import tpu_sc as plsc

F32 = jnp.float32
BF16 = jnp.bfloat16
I32 = jnp.int32

RET_HEADS = 4
RET_DK = 128
MOBA_HEADS = 8
MOBA_DH = 64
MOBA_BLOCK = 256
MOBA_TOPK = 3
ROPE_BASE = 10000.0
N_EXPERTS = 256
TOP_K = 8
N_GROUPS = 8
TOPK_GROUPS = 4
GROUP_SIZE = N_EXPERTS // N_GROUPS
ROUTED_SCALE = 2.5
N_MOD = 6
EPS = 1e-6
IN_COLS = 3584

LANES = 128
BF16_TILE_ROWS = 16
RET_CHUNK = 256
TN_ADALN = 1024
TM_INPROJ = 512
TM_MID = 512
TN_DEST = 2048
TM_COMBINE = 512
MOBA_ONES_ROWS = BF16_TILE_ROWS
SC_CORES = 2
SC_SUBCORES = 16
SC_CHUNK = 128
SC_GATHER_ROWS = 64
SC_GATHER_BUFS = 3
FFN_BLOCK = 1024
FFN_GRAIN = 128
FFN_LOOKAHEAD = 5
FFN_WEIGHT_SLOTS = 3
FFN_OUT_SLOTS = 3
VMEM_LIMIT = 56 * 1024 * 1024

NEG_INF = float("-inf")
LOG2_E = 1.4426950408889634


def _silu(x):
    return x * jax.nn.sigmoid(x)


def _nt_dot(a, b):
    return lax.dot_general(a, b, (((1,), (1,)), ((), ())), preferred_element_type=F32)


def _tn_dot(a, b):
    return lax.dot_general(a, b, (((0,), (0,)), ((), ())), preferred_element_type=F32)


def _dot(a, b):
    return jnp.dot(a, b, preferred_element_type=F32)


HI_MASK = -65536


def _pack_halves(v):
    w = v.shape[1] // 2
    lo = lax.bitcast_convert_type(v[:, :w].astype(BF16).astype(F32), I32)
    hi = lax.bitcast_convert_type(v[:, w:].astype(BF16).astype(F32), I32)
    return lax.shift_right_logical(lo, 16) | (hi & HI_MASK)


def _unpack_halves(u):
    lo = lax.bitcast_convert_type(lax.shift_left(u, 16), F32)
    hi = lax.bitcast_convert_type(u & HI_MASK, F32)
    return lo, hi


def _adaln_kernel(c_ref, w_ref, b_ref, o_ref):
    s = _silu(c_ref[...])
    o_ref[...] = _dot(s.astype(BF16), w_ref[...].astype(BF16)) + b_ref[...]


def _adaln(c, w_ada, b_ada):
    bsz, d = c.shape
    ncol = w_ada.shape[1]
    tn = TN_ADALN
    return pl.pallas_call(
        _adaln_kernel,
        out_shape=jax.ShapeDtypeStruct((bsz, ncol), F32),
        grid=(ncol // tn,),
        in_specs=[
            pl.BlockSpec((bsz, d), lambda j: (0, 0)),
            pl.BlockSpec((d, tn), lambda j: (0, j)),
            pl.BlockSpec((1, tn), lambda j: (0, j)),
        ],
        out_specs=pl.BlockSpec((bsz, tn), lambda j: (0, j)),
        compiler_params=pltpu.CompilerParams(vmem_limit_bytes=VMEM_LIMIT),
        name="adaln",
    )(c, w_ada, b_ada.reshape(1, ncol))


def _inproj_kernel(x_ref, mod_ref, g_ref, w_ref, cos_ref, sin_ref, o_ref):
    x = x_ref[...]
    ms = jnp.mean(x * x, axis=-1, keepdims=True)
    m = mod_ref[0]
    h = (x * lax.rsqrt(ms + EPS) * g_ref[...]) * (1.0 + m[1:2]) + m[0:1]
    hb = h.astype(BF16)
    cosf = cos_ref[...]
    sinf = sin_ref[...]
    k_scale = RET_DK ** -0.5
    width = RET_HEADS * RET_DK
    for ci in range(IN_COLS // width):
        acc = _dot(hb, w_ref[:, ci * width:(ci + 1) * width])
        if ci < 2:
            for hh in range(RET_HEADS):
                xh = acc[:, hh * RET_DK:(hh + 1) * RET_DK]
                r = xh * cosf + pltpu.roll(xh, RET_DK // 2, axis=1) * sinf
                if ci == 1:
                    r = r * k_scale
                o_ref[:, ci * width + hh * RET_DK:ci * width + (hh + 1) * RET_DK] = r.astype(BF16)
        else:
            o_ref[:, ci * width:(ci + 1) * width] = acc.astype(BF16)


def _inproj(x2, mod3, g_mix, w_in, cos_full, sin_signed, seq):
    n, d = x2.shape
    tm = TM_INPROJ
    tiles_per_seq = seq // tm
    return pl.pallas_call(
        _inproj_kernel,
        out_shape=jax.ShapeDtypeStruct((n, IN_COLS), BF16),
        grid=(n // tm,),
        in_specs=[
            pl.BlockSpec((tm, d), lambda i: (i, 0)),
            pl.BlockSpec((1, N_MOD, d), lambda i: (i // tiles_per_seq, 0, 0)),
            pl.BlockSpec((1, d), lambda i: (0, 0)),
            pl.BlockSpec((d, IN_COLS), lambda i: (0, 0), pipeline_mode=pl.Buffered(1)),
            pl.BlockSpec((tm, LANES), lambda i: (i % tiles_per_seq, 0)),
            pl.BlockSpec((tm, LANES), lambda i: (i % tiles_per_seq, 0)),
        ],
        out_specs=pl.BlockSpec((tm, IN_COLS), lambda i: (i, 0)),
        compiler_params=pltpu.CompilerParams(vmem_limit_bytes=VMEM_LIMIT),
        name="inproj",
    )(x2, mod3, g_mix, w_in, cos_full, sin_signed)


def _ret_kernel(lg_ref, q_ref, k_ref, v_ref, g_ref, o_ref):
    seq = q_ref.shape[1]
    c = RET_CHUNK
    lg = lg_ref[pl.program_id(1)]
    row = lax.broadcasted_iota(I32, (c, c), 0)
    col = lax.broadcasted_iota(I32, (c, c), 1)
    diff = (row - col).astype(F32)
    dmask = jnp.where(diff >= 0, jnp.exp(lg * jnp.maximum(diff, 0.0)), 0.0)
    idx = lax.broadcasted_iota(I32, (c, 1), 0).astype(F32)
    q_decay = jnp.exp(lg * (idx + 1.0))
    k_decay = jnp.exp(lg * (c - 1.0 - idx))
    chunk_decay = jnp.exp(jnp.full((1, 1), lg * c, F32))
    state = jnp.zeros((RET_DK, RET_DK), F32)
    for n in range(seq // c):
        rows = slice(n * c, (n + 1) * c)
        qn = q_ref[0, rows, :]
        kn = k_ref[0, rows, :]
        vn = v_ref[0, rows, :]
        scores = _nt_dot(qn, kn) * dmask
        inner = _dot(scores.astype(BF16), vn)
        qs = (qn.astype(F32) * q_decay).astype(BF16)
        cross = _dot(qs, state.astype(BF16))
        o = inner + cross
        o = o * lax.rsqrt(jnp.mean(o * o, axis=-1, keepdims=True) + EPS)
        gn = g_ref[0, rows, :].astype(F32)
        o_ref[0, rows, :] = (_silu(gn) * o).astype(BF16)
        ks = (kn.astype(F32) * k_decay).astype(BF16)
        state = state * chunk_decay + _tn_dot(ks, vn)


def _moba_kernel(q_ref, k_ref, v_ref, qg_ref, kg_ref, o_ref, qt_s, ka_s, kb_s, vta_s, vtb_s):
    seq = q_ref.shape[1]
    lb = MOBA_BLOCK
    nb = seq // lb
    lane = lax.broadcasted_iota(I32, (1, LANES), 1)
    is_a = lane < MOBA_DH

    def head_norm(xf, gain):
        sq = xf * xf
        s_a = jnp.sum(jnp.where(is_a, sq, 0.0), axis=-1, keepdims=True)
        s_b = jnp.sum(jnp.where(is_a, 0.0, sq), axis=-1, keepdims=True)
        inv = jnp.where(is_a, lax.rsqrt(s_a / MOBA_DH + EPS), lax.rsqrt(s_b / MOBA_DH + EPS))
        return xf * inv * gain

    qg = qg_ref[...]
    kg = kg_ref[...]
    k_means = []
    for j in range(nb):
        rows = slice(j * lb, (j + 1) * lb)
        kf = head_norm(k_ref[0, rows, :].astype(F32), kg)
        ka_s[rows, :] = jnp.where(is_a, kf, 0.0).astype(BF16)
        kb_s[rows, :] = jnp.where(is_a, 0.0, kf).astype(BF16)
        k_means.append(jnp.mean(kf, axis=0, keepdims=True))
        qf = head_norm(q_ref[0, rows, :].astype(F32), qg)
        qt_s[:, rows] = (qf * (MOBA_DH ** -0.5 * LOG2_E)).T.astype(BF16)
        vt = v_ref[0, rows, :].T
        ones = jnp.ones((MOBA_ONES_ROWS, lb), BF16)
        vta_s[:, rows] = jnp.concatenate([vt[:MOBA_DH], ones], axis=0)
        vtb_s[:, rows] = jnp.concatenate([vt[MOBA_DH:], ones], axis=0)
    k_mean = jnp.concatenate(k_means + [jnp.zeros((BF16_TILE_ROWS - nb, LANES), F32)], axis=0)
    k_mean_h = (jnp.where(is_a, k_mean, 0.0).astype(BF16), jnp.where(is_a, 0.0, k_mean).astype(BF16))
    k_s = (ka_s, kb_s)
    vt_s = (vta_s, vtb_s)

    r_loc = lax.broadcasted_iota(I32, (lb, lb), 0)
    c_loc = lax.broadcasted_iota(I32, (lb, lb), 1)
    causal = r_loc <= c_loc

    for i in range(nb):
        cols = slice(i * lb, (i + 1) * lb)
        qt = qt_s[:, cols]
        outs = []
        for hx in range(2):
            bias = [None] * i
            if i > MOBA_TOPK:
                gate = _dot(k_mean_h[hx], qt)
                g = [gate[j:j + 1, :] for j in range(i)]
                for j in range(i):
                    rank = jnp.zeros((1, lb), F32)
                    for j2 in range(i):
                        if j2 == j:
                            continue
                        beats = (g[j2] >= g[j]) if j2 < j else (g[j2] > g[j])
                        rank = rank + jnp.where(beats, 1.0, 0.0)
                    bias[j] = jnp.where(rank < float(MOBA_TOPK), 0.0, NEG_INF).astype(BF16)
            pieces = []
            for j in range(i + 1):
                s = _dot(k_s[hx][j * lb:(j + 1) * lb, :], qt).astype(BF16)
                if j == i:
                    s = jnp.where(causal, s, jnp.asarray(NEG_INF, BF16))
                elif bias[j] is not None:
                    s = s + bias[j]
                pieces.append(s)
            mx = jnp.max(pieces[0], axis=0, keepdims=True)
            for s in pieces[1:]:
                mx = jnp.maximum(mx, jnp.max(s, axis=0, keepdims=True))
            acc = jnp.zeros((MOBA_DH + MOBA_ONES_ROWS, lb), F32)
            for j, s in enumerate(pieces):
                acc = acc + _dot(vt_s[hx][:, j * lb:(j + 1) * lb], jnp.exp2(s - mx))
            outs.append(acc[:MOBA_DH] / acc[MOBA_DH:MOBA_DH + 1, :])
        o_ref[0, cols, :] = jnp.concatenate(outs, axis=0).T.astype(BF16)


def _mixers_kernel(lg_ref, rq_ref, rk_ref, rv_ref, rg_ref, mq_ref, mk_ref, mv_ref, qg_ref, kg_ref,
                   ret_ref, moba_ref, *moba_scratch):
    _ret_kernel(lg_ref, rq_ref, rk_ref, rv_ref, rg_ref, ret_ref)
    _moba_kernel(mq_ref, mk_ref, mv_ref, qg_ref, kg_ref, moba_ref, *moba_scratch)


def _mixers(log_g, proj3, qg2, kg2):
    bsz, seq, _ = proj3.shape
    pairs = MOBA_HEADS // 2
    assert pairs == RET_HEADS and RET_DK == LANES
    blk = (1, seq, LANES)
    base = 4 * RET_HEADS

    def col(first):
        return pl.BlockSpec(blk, lambda b, p, lg: (b, 0, first + p))

    return pl.pallas_call(
        _mixers_kernel,
        out_shape=(jax.ShapeDtypeStruct((bsz, seq, RET_HEADS * RET_DK), BF16),
                   jax.ShapeDtypeStruct((bsz, seq, MOBA_HEADS * MOBA_DH), BF16)),
        grid_spec=pltpu.PrefetchScalarGridSpec(
            num_scalar_prefetch=1,
            grid=(bsz, pairs),
            in_specs=[
                col(0), col(RET_HEADS), col(2 * RET_HEADS), col(3 * RET_HEADS),
                col(base), col(base + pairs), col(base + 2 * pairs),
                pl.BlockSpec((1, LANES), lambda b, p, lg: (0, 0)),
                pl.BlockSpec((1, LANES), lambda b, p, lg: (0, 0)),
            ],
            out_specs=(col(0), col(0)),
            scratch_shapes=[
                pltpu.VMEM((LANES, seq), BF16),
                pltpu.VMEM((seq, LANES), BF16),
                pltpu.VMEM((seq, LANES), BF16),
                pltpu.VMEM((MOBA_DH + MOBA_ONES_ROWS, seq), BF16),
                pltpu.VMEM((MOBA_DH + MOBA_ONES_ROWS, seq), BF16),
            ],
        ),
        compiler_params=pltpu.CompilerParams(vmem_limit_bytes=VMEM_LIMIT),
        name="mixers",
    )(log_g, *([proj3] * 7), qg2, kg2)


def _mid_kernel(ret_ref, moba_ref, x_ref, mod_ref, wo1_ref, wo2_ref, g_ref, wr_ref, rb_ref,
                wsg_ref, wsu_ref, wsd_ref,
                xb_ref, h2_ref, e_ref, w_ref, rk_ref, cnt_ref, carry_s):
    i = pl.program_id(0)
    tm = x_ref.shape[0]

    @pl.when(i == 0)
    def _():
        carry_s[...] = jnp.zeros_like(carry_s)

    m = mod_ref[0]
    mixed = _dot(ret_ref[...], wo1_ref[...]) + _dot(moba_ref[...], wo2_ref[...])
    x1 = x_ref[...] + m[2:3] * mixed
    ms = jnp.mean(x1 * x1, axis=-1, keepdims=True)
    h2 = (x1 * lax.rsqrt(ms + EPS) * g_ref[...]) * (1.0 + m[4:5]) + m[3:4]
    h2_ref[...] = _pack_halves(h2)
    h2b = h2.astype(BF16)

    hid = _silu(_dot(h2b, wsg_ref[...])) * _dot(h2b, wsu_ref[...])
    xb_ref[...] = x1 + m[5:6] * _dot(hid.astype(BF16), wsd_ref[...])

    scores = jax.nn.sigmoid(_nt_dot(wr_ref[...], h2b))
    biased = scores + rb_ref[...]
    grp = biased.reshape(N_GROUPS, GROUP_SIZE, tm)
    gi = lax.broadcasted_iota(I32, (N_GROUPS, GROUP_SIZE, tm), 1).astype(F32)
    top1 = jnp.max(grp, axis=1, keepdims=True)
    first = jnp.min(jnp.where(grp == top1, gi, float(GROUP_SIZE)), axis=1, keepdims=True)
    top2 = jnp.max(jnp.where(gi == first, NEG_INF, grp), axis=1, keepdims=True)
    gscore = (top1 + top2).reshape(N_GROUPS, tm)
    gidx = lax.broadcasted_iota(I32, (N_GROUPS, tm), 0)
    grank = jnp.zeros((N_GROUPS, tm), F32)
    for g2 in range(N_GROUPS):
        rowv = gscore[g2:g2 + 1, :]
        beats = (rowv > gscore) | ((rowv == gscore) & (g2 < gidx))
        grank = grank + jnp.where(beats, 1.0, 0.0)
    gsel = jnp.where(grank < float(TOPK_GROUPS), 1.0, 0.0)
    emask = jnp.broadcast_to(gsel.reshape(N_GROUPS, 1, tm), (N_GROUPS, GROUP_SIZE, tm)).reshape(N_EXPERTS, tm)
    choice = jnp.where(emask > 0.5, biased, NEG_INF)

    eidx = lax.broadcasted_iota(I32, (N_EXPERTS, tm), 0).astype(F32)
    remaining = choice
    e_rows = []
    for _k in range(TOP_K):
        mx = jnp.max(remaining, axis=0, keepdims=True)
        idx = jnp.min(jnp.where(remaining == mx, eidx, float(N_EXPERTS)), axis=0, keepdims=True)
        e_rows.append(idx)
        remaining = jnp.where(eidx == idx, NEG_INF, remaining)
    selmask = jnp.where(remaining != choice, 1.0, 0.0)

    tr = lax.broadcasted_iota(I32, (tm, tm), 0)
    tc = lax.broadcasted_iota(I32, (tm, tm), 1)
    upper = jnp.where(tr < tc, 1.0, 0.0).astype(BF16)
    prefix = _dot(selmask.astype(BF16), upper) + carry_s[...]
    w_rows = []
    r_rows = []
    for ek in e_rows:
        onehot = eidx == ek
        w_rows.append(jnp.sum(jnp.where(onehot, scores, 0.0), axis=0, keepdims=True))
        r_rows.append(jnp.sum(jnp.where(onehot, prefix, 0.0), axis=0, keepdims=True))
    wsum = w_rows[0]
    for wk in w_rows[1:]:
        wsum = wsum + wk
    carry_s[...] = carry_s[...] + jnp.sum(selmask, axis=1, keepdims=True)

    e_ref[...] = jnp.concatenate(e_rows, axis=0).astype(I32)
    w_ref[...] = jnp.concatenate([wk / wsum * ROUTED_SCALE for wk in w_rows], axis=0)
    rk_ref[...] = jnp.concatenate(r_rows, axis=0).astype(I32)
    cnt_ref[...] = carry_s[...].astype(I32)


def _mid(ret2, moba2, x2, mod3, wo1, wo2, g_ffn, wr_t, rbias, wsg, wsu, wsd, seq):
    n, d = x2.shape
    tm = TM_MID
    tiles_per_seq = seq // tm
    half = ret2.shape[1]
    ff = wsg.shape[1]
    const = lambda i: (0, 0)
    row = lambda i: (i, 0)
    colt = lambda i: (0, i)
    return pl.pallas_call(
        _mid_kernel,
        out_shape=(
            jax.ShapeDtypeStruct((n, d), F32),
            jax.ShapeDtypeStruct((n, d // 2), I32),
            jax.ShapeDtypeStruct((TOP_K, n), I32),
            jax.ShapeDtypeStruct((TOP_K, n), F32),
            jax.ShapeDtypeStruct((TOP_K, n), I32),
            jax.ShapeDtypeStruct((N_EXPERTS, 1), I32),
        ),
        grid=(n // tm,),
        in_specs=[
            pl.BlockSpec((tm, half), row),
            pl.BlockSpec((tm, half), row),
            pl.BlockSpec((tm, d), row),
            pl.BlockSpec((1, N_MOD, d), lambda i: (i // tiles_per_seq, 0, 0)),
            pl.BlockSpec((half, d), const),
            pl.BlockSpec((half, d), const),
            pl.BlockSpec((1, d), const),
            pl.BlockSpec((N_EXPERTS, d), const),
            pl.BlockSpec((N_EXPERTS, 1), const),
            pl.BlockSpec((d, ff), const),
            pl.BlockSpec((d, ff), const),
            pl.BlockSpec((ff, d), const),
        ],
        out_specs=(
            pl.BlockSpec((tm, d), row),
            pl.BlockSpec((tm, d // 2), row),
            pl.BlockSpec((TOP_K, tm), colt),
            pl.BlockSpec((TOP_K, tm), colt),
            pl.BlockSpec((TOP_K, tm), colt),
            pl.BlockSpec((N_EXPERTS, 1), const),
        ),
        scratch_shapes=[pltpu.VMEM((N_EXPERTS, 1), F32)],
        compiler_params=pltpu.CompilerParams(
            dimension_semantics=("arbitrary",), vmem_limit_bytes=VMEM_LIMIT),
        name="mid",
    )(ret2, moba2, x2, mod3, wo1, wo2, g_ffn, wr_t, rbias, wsg, wsu, wsd)


ROW_EXP = 0
ROW_VALID = 1
ROW_FIRST = 2
ROW_AHEAD = 3
ROW_SLOT = 4
ROW_HEAD = 5
ROW_START = 6
SCHED_ROWS = 8


def _sched_kernel(cnt_ref, tab_ref):
    ne = N_EXPERTS
    nblk = tab_ref.shape[1]
    shift = FFN_BLOCK.bit_length() - 1
    e_sub = lax.broadcasted_iota(I32, (ne, ne), 0)
    e_lane = lax.broadcasted_iota(I32, (ne, ne), 1)
    e_col = lax.broadcasted_iota(I32, (ne, 1), 0).astype(F32)
    ids_row = lax.broadcasted_iota(I32, (1, ne), 1).astype(F32) + 1.0

    def to_row(col):
        return jnp.sum(jnp.where(e_sub == e_lane, col, 0.0), axis=0, keepdims=True)

    def running_col(row):
        return jnp.sum(jnp.where(e_lane <= e_sub, row, 0.0), axis=1, keepdims=True)

    def running_row(col):
        return jnp.sum(jnp.where(e_sub <= e_lane, col, 0.0), axis=0, keepdims=True)

    cnt_i = cnt_ref[...]
    cnt_col = cnt_i.astype(F32)
    pad_col = lax.shift_left(lax.shift_right_logical(cnt_i + (FFN_BLOCK - 1), shift), shift).astype(F32)
    pad_row = to_row(pad_col)
    ends_col = running_col(pad_row)
    start_col = ends_col - pad_col
    start_row = running_row(pad_col) - pad_row
    nreal = jnp.sum(pad_row, axis=1, keepdims=True) * (1.0 / FFN_BLOCK)

    step = lax.broadcasted_iota(I32, (1, nblk), 1)
    g = step.astype(F32)
    row0 = jnp.minimum(g, nreal - 1.0) * float(FFN_BLOCK)
    exp_g = jnp.sum(jnp.where(ends_col <= row0, 1.0, 0.0), axis=0, keepdims=True)
    mine = e_col == exp_g

    def per_block(col):
        return jnp.sum(jnp.where(mine, col, 0.0), axis=0, keepdims=True)

    valid = jnp.clip(per_block(cnt_col + start_col) - row0, 0.0, float(FFN_BLOCK))
    first = jnp.where(jnp.logical_and(g < nreal, per_block(start_col) == row0), 1.0, 0.0)

    used_col = jnp.where(cnt_col > 0.0, 1.0, 0.0)
    used_row = to_row(used_col)
    ord_col = running_col(used_row) - 1.0
    ord_row = running_row(used_col) - 1.0

    def used_at(pos):
        hit = jnp.logical_and(used_row > 0.0, ord_row == pos)
        return jnp.sum(jnp.where(hit, ids_row, 0.0), axis=1, keepdims=True) - 1.0

    ahead_col = used_at(ord_col + float(FFN_WEIGHT_SLOTS - 1))
    slot_col = ord_col - FFN_WEIGHT_SLOTS * jnp.floor((ord_col + 0.5) * (1.0 / FFN_WEIGHT_SLOTS))
    head = jnp.where(step == 0, nreal, 0.0)
    for j in range(FFN_WEIGHT_SLOTS - 1):
        head = head + jnp.where(step == 1 + j, used_at(jnp.full((1, 1), float(j), F32)), 0.0)
    rows = [exp_g, valid, first, per_block(ahead_col), per_block(slot_col), head,
            jnp.concatenate([start_row, jnp.zeros((1, nblk - ne), F32)], axis=1),
            jnp.zeros((1, nblk), F32)]
    tab_ref[...] = jnp.concatenate(rows, axis=0).astype(I32)


def _sched(cnt, nblk):
    assert FFN_BLOCK & (FFN_BLOCK - 1) == 0 and nblk >= N_EXPERTS
    return pl.pallas_call(
        _sched_kernel,
        out_shape=jax.ShapeDtypeStruct((SCHED_ROWS, nblk), I32),
        compiler_params=pltpu.CompilerParams(vmem_limit_bytes=VMEM_LIMIT),
        name="sched",
    )(cnt)


def _dest_kernel(tab_ref, e_ref, rk_ref, o_ref):
    e = e_ref[...]

    def body(ex, acc):
        return acc + jnp.where(e == ex, tab_ref[ROW_START, ex], 0)

    dest = lax.fori_loop(0, N_EXPERTS, body, rk_ref[...], unroll=8)
    for ch in range(o_ref.shape[0]):
        o_ref[ch] = dest[:, ch * SC_CHUNK:(ch + 1) * SC_CHUNK]


def _dest(tab, e_idx, rank):
    k, n = e_idx.shape
    tn = TN_DEST
    return pl.pallas_call(
        _dest_kernel,
        out_shape=jax.ShapeDtypeStruct((n // SC_CHUNK, k, SC_CHUNK), I32),
        grid_spec=pltpu.PrefetchScalarGridSpec(
            num_scalar_prefetch=1,
            grid=(n // tn,),
            in_specs=[
                pl.BlockSpec((k, tn), lambda i, s: (0, i)),
                pl.BlockSpec((k, tn), lambda i, s: (0, i)),
            ],
            out_specs=pl.BlockSpec((tn // SC_CHUNK, k, SC_CHUNK), lambda i, s: (i, 0, 0)),
        ),
        compiler_params=pltpu.CompilerParams(vmem_limit_bytes=VMEM_LIMIT),
        name="dest",
    )(tab, e_idx, rank)


def _sc_dispatch(h2p, dest3, total_rows):
    n, words = h2p.shape
    nchunks = n // SC_CHUNK
    per_worker = nchunks // (SC_CORES * SC_SUBCORES)
    mesh = plsc.VectorSubcoreMesh(core_axis_name="c", subcore_axis_name="s",
                                  num_cores=SC_CORES, num_subcores=SC_SUBCORES)

    @functools.partial(
        pl.kernel, mesh=mesh,
        out_type=jax.ShapeDtypeStruct((total_rows, words), I32),
        scratch_types=[
            pltpu.VMEM((TOP_K, SC_CHUNK), I32),
            pltpu.VMEM((SC_CHUNK, words), I32),
            pltpu.SemaphoreType.DMA,
        ],
        name="sc_dispatch",
    )
    def run(h_hbm, d_hbm, xs_hbm, idx_v, rows_v, sem):
        wid = lax.axis_index("s") * SC_CORES + lax.axis_index("c")

        @pl.loop(0, per_worker)
        def _(j):
            ch = wid * per_worker + j
            pltpu.sync_copy(d_hbm.at[ch], idx_v)
            pltpu.sync_copy(h_hbm.at[pl.ds(ch * SC_CHUNK, SC_CHUNK)], rows_v)
            copies = [pltpu.async_copy(rows_v, xs_hbm.at[idx_v.at[k]], sem) for k in range(TOP_K)]
            for cp in copies:
                cp.wait()

    return run(h2p, dest3)


def _ffn_kernel(tab_ref, x_hbm, wg_hbm, wu_hbm, wd_hbm, y_hbm,
                x_s, y_s, wg_s, wu_s, wd_s, sem_x, sem_y, sem):
    i = pl.program_id(0)
    nreal = tab_ref[ROW_HEAD, 0]
    rows_per = x_s.shape[1]
    half = x_s.shape[2]
    sizes = tuple(range(FFN_GRAIN, rows_per + 1, FFN_GRAIN))

    def rows_needed(g):
        return (tab_ref[ROW_VALID, g] + (FFN_GRAIN - 1)) // FFN_GRAIN * FFN_GRAIN

    def by_size(nrows, fn):
        for n in sizes:
            @pl.when(nrows == n)
            def _():
                fn(n)

    def row_copy(g, n):
        slot = lax.rem(g, FFN_LOOKAHEAD + 1)
        return pltpu.make_async_copy(
            x_hbm.at[pl.ds(g * rows_per, n)], x_s.at[slot, pl.ds(0, n)], sem_x.at[slot])

    def out_copy(g, n):
        slot = lax.rem(g, FFN_OUT_SLOTS)
        return pltpu.make_async_copy(
            y_s.at[slot, pl.ds(0, n)], y_hbm.at[pl.ds(g * rows_per, n)], sem_y.at[slot])

    def weight_copies(e, s):
        return (pltpu.make_async_copy(wg_hbm.at[e], wg_s.at[s], sem.at[s, 0]),
                pltpu.make_async_copy(wu_hbm.at[e], wu_s.at[s], sem.at[s, 1]),
                pltpu.make_async_copy(wd_hbm.at[e], wd_s.at[s], sem.at[s, 2]))

    @pl.when(i == 0)
    def _():
        for j in range(FFN_WEIGHT_SLOTS - 1):
            @pl.when(tab_ref[ROW_HEAD, 1 + j] >= 0)
            def _():
                for cp in weight_copies(tab_ref[ROW_HEAD, 1 + j], j):
                    cp.start()
        for g in range(FFN_LOOKAHEAD):
            @pl.when(g < nreal)
            def _():
                by_size(rows_needed(g), lambda n: row_copy(g, n).start())

    @pl.when(i < nreal)
    def _():
        s = tab_ref[ROW_SLOT, i]
        fetch = i + FFN_LOOKAHEAD

        @pl.when(fetch < nreal)
        def _():
            by_size(rows_needed(fetch), lambda n: row_copy(fetch, n).start())

        by_size(rows_needed(i), lambda n: row_copy(i, n).wait())

        @pl.when(tab_ref[ROW_FIRST, i] == 1)
        def _():
            for cp in weight_copies(tab_ref[ROW_EXP, i], s):
                cp.wait()

            ahead = tab_ref[ROW_AHEAD, i]

            @pl.when(ahead >= 0)
            def _():
                for cp in weight_copies(ahead, lax.rem(s + FFN_WEIGHT_SLOTS - 1, FFN_WEIGHT_SLOTS)):
                    cp.start()

        @pl.when(i >= FFN_OUT_SLOTS)
        def _():
            done = i - FFN_OUT_SLOTS
            by_size(rows_needed(done), lambda n: out_copy(done, n).wait())

        x_slot = lax.rem(i, FFN_LOOKAHEAD + 1)
        y_slot = lax.rem(i, FFN_OUT_SLOTS)
        valid = tab_ref[ROW_VALID, i]

        def expert_rows(n):
            r = lax.broadcasted_iota(I32, (n, 1), 0)
            x_lo, x_hi = _unpack_halves(jnp.where(r < valid, x_s[x_slot, pl.ds(0, n), :], 0))
            hg = _dot(x_lo, wg_s[s, :half, :]) + _dot(x_hi, wg_s[s, half:, :])
            hu = _dot(x_lo, wu_s[s, :half, :]) + _dot(x_hi, wu_s[s, half:, :])
            y_s[y_slot, pl.ds(0, n), :] = _pack_halves(_dot(_silu(hg) * hu, wd_s[s]))
            out_copy(i, n).start()

        by_size(rows_needed(i), expert_rows)

        @pl.when(i == nreal - 1)
        def _():
            for back in range(FFN_OUT_SLOTS):
                last = i - back

                @pl.when(last >= 0)
                def _():
                    by_size(rows_needed(last), lambda n: out_copy(last, n).wait())


def _ffn(tab, xs, w_gate, w_up, w_down):
    p, half = xs.shape
    d = 2 * half
    ff = w_gate.shape[2]
    return pl.pallas_call(
        _ffn_kernel,
        out_shape=jax.ShapeDtypeStruct((p, half), I32),
        grid_spec=pltpu.PrefetchScalarGridSpec(
            num_scalar_prefetch=1,
            grid=(tab.shape[1],),
            in_specs=[
                pl.BlockSpec(memory_space=pl.ANY),
                pl.BlockSpec(memory_space=pl.ANY),
                pl.BlockSpec(memory_space=pl.ANY),
                pl.BlockSpec(memory_space=pl.ANY),
            ],
            out_specs=pl.BlockSpec(memory_space=pl.ANY),
            scratch_shapes=[
                pltpu.VMEM((FFN_LOOKAHEAD + 1, FFN_BLOCK, half), I32),
                pltpu.VMEM((FFN_OUT_SLOTS, FFN_BLOCK, half), I32),
                pltpu.VMEM((FFN_WEIGHT_SLOTS, d, ff), F32),
                pltpu.VMEM((FFN_WEIGHT_SLOTS, d, ff), F32),
                pltpu.VMEM((FFN_WEIGHT_SLOTS, ff, d), F32),
                pltpu.SemaphoreType.DMA((FFN_LOOKAHEAD + 1,)),
                pltpu.SemaphoreType.DMA((FFN_OUT_SLOTS,)),
                pltpu.SemaphoreType.DMA((FFN_WEIGHT_SLOTS, 3)),
            ],
        ),
        compiler_params=pltpu.CompilerParams(
            dimension_semantics=("arbitrary",), vmem_limit_bytes=VMEM_LIMIT, has_side_effects=True),
        name="ffn",
    )(tab, xs, w_gate, w_up, w_down)


def _sc_gather(y, dest3):
    a, words = y.shape
    nchunks, _, chunk = dest3.shape
    n = nchunks * chunk
    per_worker = nchunks // (SC_CORES * SC_SUBCORES)
    nbuf = SC_GATHER_BUFS
    parts = chunk // SC_GATHER_ROWS
    items = [(c, k, h) for c in range(per_worker) for k in range(TOP_K) for h in range(parts)]
    mesh = plsc.VectorSubcoreMesh(core_axis_name="c", subcore_axis_name="s",
                                  num_cores=SC_CORES, num_subcores=SC_SUBCORES)

    @functools.partial(
        pl.kernel, mesh=mesh,
        out_type=jax.ShapeDtypeStruct((TOP_K, n, words), I32),
        scratch_types=[
            pltpu.VMEM((per_worker, TOP_K, chunk), I32),
            [pltpu.VMEM((SC_GATHER_ROWS, words), I32)] * nbuf,
            pltpu.SemaphoreType.DMA((nbuf,)),
            pltpu.SemaphoreType.DMA((nbuf,)),
        ],
        name="sc_gather",
    )
    def run(y_hbm, d_hbm, yt_hbm, idx_v, bufs, sem_g, sem_w):
        wid = lax.axis_index("s") * SC_CORES + lax.axis_index("c")
        pltpu.sync_copy(d_hbm.at[pl.ds(wid * per_worker, per_worker)], idx_v)

        def gather(m):
            c, k, h = items[m]
            idx = idx_v.at[c, k, pl.ds(h * SC_GATHER_ROWS, SC_GATHER_ROWS)]
            return pltpu.async_copy(y_hbm.at[idx], bufs[m % nbuf], sem_g.at[m % nbuf])

        def write(m):
            c, k, h = items[m]
            rows = pl.ds((wid * per_worker + c) * chunk + h * SC_GATHER_ROWS, SC_GATHER_ROWS)
            return pltpu.async_copy(bufs[m % nbuf], yt_hbm.at[k, rows], sem_w.at[m % nbuf])

        gathers = {m: gather(m) for m in range(min(nbuf - 1, len(items)))}
        writes = {}
        for m in range(len(items)):
            gathers.pop(m).wait()
            writes[m] = write(m)
            nxt = m + nbuf - 1
            if nxt < len(items):
                if m >= 1:
                    writes.pop(m - 1).wait()
                gathers[nxt] = gather(nxt)
        for m in sorted(writes):
            writes.pop(m).wait()

    return run(y, dest3)


def _combine_kernel(yt_ref, wt_ref, xb_ref, mod_ref, o_ref):
    half = yt_ref.shape[2]
    wt = wt_ref[...].T
    lo, hi = _unpack_halves(yt_ref[0])
    r_lo = lo * wt[:, 0:1]
    r_hi = hi * wt[:, 0:1]
    for k in range(1, TOP_K):
        lo, hi = _unpack_halves(yt_ref[k])
        r_lo = r_lo + lo * wt[:, k:k + 1]
        r_hi = r_hi + hi * wt[:, k:k + 1]
    gate = mod_ref[0][5:6]
    o_ref[:, :half] = xb_ref[:, :half] + gate[:, :half] * r_lo
    o_ref[:, half:] = xb_ref[:, half:] + gate[:, half:] * r_hi


def _combine(yt, w_k, xb, mod3, seq):
    n, d = xb.shape
    tm = TM_COMBINE
    tiles_per_seq = seq // tm
    return pl.pallas_call(
        _combine_kernel,
        out_shape=jax.ShapeDtypeStruct((n, d), F32),
        grid=(n // tm,),
        in_specs=[
            pl.BlockSpec((TOP_K, tm, d // 2), lambda i: (0, i, 0)),
            pl.BlockSpec((TOP_K, tm), lambda i: (0, i)),
            pl.BlockSpec((tm, d), lambda i: (i, 0)),
            pl.BlockSpec((1, N_MOD, d), lambda i: (i // tiles_per_seq, 0, 0)),
        ],
        out_specs=pl.BlockSpec((tm, d), lambda i: (i, 0)),
        compiler_params=pltpu.CompilerParams(vmem_limit_bytes=VMEM_LIMIT),
        name="combine",
    )(yt, w_k, xb, mod3)


def _rotary_tables(seq):
    half = RET_DK // 2
    inv = ROPE_BASE ** (-np.arange(half, dtype=np.float64) / half)
    ang = np.arange(seq, dtype=np.float64)[:, None] * inv[None, :]
    cos = np.cos(ang).astype(np.float32)
    sin = np.sin(ang).astype(np.float32)
    return (jnp.asarray(np.concatenate([cos, cos], axis=-1)),
            jnp.asarray(np.concatenate([-sin, sin], axis=-1)))


def kernel(x, c, w_ada, b_ada, g_mix, w_in, q_gain, k_gain, w_out, g_ffn, w_router, router_bias,
           w_gate, w_up, w_down, ws_gate, ws_up, ws_down):
    bsz, seq, d = x.shape
    n = bsz * seq
    depth = w_ada.shape[0]
    cos_full, sin_signed = _rotary_tables(seq)
    log_g = jnp.asarray(np.log1p(-np.exp2(-5.0 - np.arange(RET_HEADS, dtype=np.float64))).astype(np.float32))
    ret_w = RET_HEADS * RET_DK
    x2 = x.reshape(n, d)
    for l in range(depth):
        mod3 = _adaln(c, w_ada[l], b_ada[l]).reshape(bsz, N_MOD, d)
        proj = _inproj(x2, mod3, g_mix[l].reshape(1, d), w_in[l], cos_full, sin_signed, seq)
        proj3 = proj.reshape(bsz, seq, IN_COLS)
        qg2 = jnp.tile(q_gain[l].reshape(1, MOBA_DH), (1, 2))
        kg2 = jnp.tile(k_gain[l].reshape(1, MOBA_DH), (1, 2))
        ret, moba = _mixers(log_g, proj3, qg2, kg2)
        wo = w_out[l].astype(BF16)
        xb, h2, e_idx, w_k, rank, cnt = _mid(
            ret.reshape(n, ret_w), moba.reshape(n, MOBA_HEADS * MOBA_DH), x2, mod3,
            wo[:ret_w], wo[ret_w:], g_ffn[l].reshape(1, d),
            w_router[l].T.astype(BF16), router_bias[l].reshape(N_EXPERTS, 1),
            ws_gate[l].astype(BF16), ws_up[l].astype(BF16), ws_down[l].astype(BF16), seq)
        nblk = n * TOP_K // FFN_BLOCK + N_EXPERTS
        tab = _sched(cnt, nblk)
        dest3 = _dest(tab, e_idx, rank)
        xs = _sc_dispatch(h2, dest3, nblk * FFN_BLOCK)
        y = _ffn(tab, xs, w_gate[l], w_up[l], w_down[l])
        yt = _sc_gather(y, dest3)
        x2 = _combine(yt, w_k, xb, mod3, seq)
    return x2.reshape(bsz, seq, d)
```

```python
import functools

import numpy as np
import jax
import jax.numpy as jnp
from jax import lax
from jax.experimental import pallas as pl
from jax.experimental.pallas import tpu as pltpu
from jax.experimental.pallas import tpu_sc as plsc

F32 = jnp.float32
BF16 = jnp.bfloat16
I32 = jnp.int32

RET_HEADS = 4
RET_DK = 128
MOBA_HEADS = 8
MOBA_DH = 64
MOBA_BLOCK = 256
MOBA_TOPK = 3
ROPE_BASE = 10000.0
N_EXPERTS = 256
TOP_K = 8
N_GROUPS = 8
TOPK_GROUPS = 4
GROUP_SIZE = N_EXPERTS // N_GROUPS
ROUTED_SCALE = 2.5
N_MOD = 6
EPS = 1e-6
IN_COLS = 3584

LANES = 128
BF16_TILE_ROWS = 16
RET_CHUNK = 256
TN_ADALN = 1024
TM_INPROJ = 512
TM_MID = 512
TN_DEST = 2048
TM_COMBINE = 512
MOBA_ONES_ROWS = BF16_TILE_ROWS
SC_CORES = 2
SC_SUBCORES = 16
SC_CHUNK = 128
SC_GATHER_ROWS = 64
SC_GATHER_BUFS = 3
FFN_BLOCK = 1024
FFN_GRAIN = 64
FFN_LOOKAHEAD = 5
FFN_WEIGHT_SLOTS = 3
FFN_OUT_SLOTS = 3
VMEM_LIMIT = 56 * 1024 * 1024

NEG_INF = float("-inf")
LOG2_E = 1.4426950408889634


def _silu(x):
    return x * jax.nn.sigmoid(x)


def _nt_dot(a, b):
    return lax.dot_general(a, b, (((1,), (1,)), ((), ())), preferred_element_type=F32)


def _tn_dot(a, b):
    return lax.dot_general(a, b, (((0,), (0,)), ((), ())), preferred_element_type=F32)


def _dot(a, b):
    return jnp.dot(a, b, preferred_element_type=F32)


HI_MASK = -65536


def _pack_halves(v):
    w = v.shape[1] // 2
    lo = lax.bitcast_convert_type(v[:, :w].astype(BF16).astype(F32), I32)
    hi = lax.bitcast_convert_type(v[:, w:].astype(BF16).astype(F32), I32)
    return lax.shift_right_logical(lo, 16) | (hi & HI_MASK)


def _unpack_halves(u):
    lo = lax.bitcast_convert_type(lax.shift_left(u, 16), F32)
    hi = lax.bitcast_convert_type(u & HI_MASK, F32)
    return lo, hi


def _adaln_kernel(c_ref, w_ref, b_ref, o_ref):
    s = _silu(c_ref[...])
    o_ref[...] = _dot(s.astype(BF16), w_ref[...].astype(BF16)) + b_ref[...]


def _adaln(c, w_ada, b_ada):
    bsz, d = c.shape
    ncol = w_ada.shape[1]
    tn = TN_ADALN
    return pl.pallas_call(
        _adaln_kernel,
        out_shape=jax.ShapeDtypeStruct((bsz, ncol), F32),
        grid=(ncol // tn,),
        in_specs=[
            pl.BlockSpec((bsz, d), lambda j: (0, 0)),
            pl.BlockSpec((d, tn), lambda j: (0, j)),
            pl.BlockSpec((1, tn), lambda j: (0, j)),
        ],
        out_specs=pl.BlockSpec((bsz, tn), lambda j: (0, j)),
        compiler_params=pltpu.CompilerParams(vmem_limit_bytes=VMEM_LIMIT),
        name="adaln",
    )(c, w_ada, b_ada.reshape(1, ncol))


def _inproj_kernel(x_ref, mod_ref, g_ref, w_ref, cos_ref, sin_ref, o_ref):
    x = x_ref[...]
    ms = jnp.mean(x * x, axis=-1, keepdims=True)
    m = mod_ref[0]
    h = (x * lax.rsqrt(ms + EPS) * g_ref[...]) * (1.0 + m[1:2]) + m[0:1]
    hb = h.astype(BF16)
    cosf = cos_ref[...]
    sinf = sin_ref[...]
    k_scale = RET_DK ** -0.5
    width = RET_HEADS * RET_DK
    for ci in range(IN_COLS // width):
        acc = _dot(hb, w_ref[:, ci * width:(ci + 1) * width])
        if ci < 2:
            for hh in range(RET_HEADS):
                xh = acc[:, hh * RET_DK:(hh + 1) * RET_DK]
                r = xh * cosf + pltpu.roll(xh, RET_DK // 2, axis=1) * sinf
                if ci == 1:
                    r = r * k_scale
                o_ref[:, ci * width + hh * RET_DK:ci * width + (hh + 1) * RET_DK] = r.astype(BF16)
        else:
            o_ref[:, ci * width:(ci + 1) * width] = acc.astype(BF16)


def _inproj(x2, mod3, g_mix, w_in, cos_full, sin_signed, seq):
    n, d = x2.shape
    tm = TM_INPROJ
    tiles_per_seq = seq // tm
    return pl.pallas_call(
        _inproj_kernel,
        out_shape=jax.ShapeDtypeStruct((n, IN_COLS), BF16),
        grid=(n // tm,),
        in_specs=[
            pl.BlockSpec((tm, d), lambda i: (i, 0)),
            pl.BlockSpec((1, N_MOD, d), lambda i: (i // tiles_per_seq, 0, 0)),
            pl.BlockSpec((1, d), lambda i: (0, 0)),
            pl.BlockSpec((d, IN_COLS), lambda i: (0, 0), pipeline_mode=pl.Buffered(1)),
            pl.BlockSpec((tm, LANES), lambda i: (i % tiles_per_seq, 0)),
            pl.BlockSpec((tm, LANES), lambda i: (i % tiles_per_seq, 0)),
        ],
        out_specs=pl.BlockSpec((tm, IN_COLS), lambda i: (i, 0)),
        compiler_params=pltpu.CompilerParams(vmem_limit_bytes=VMEM_LIMIT),
        name="inproj",
    )(x2, mod3, g_mix, w_in, cos_full, sin_signed)


def _ret_kernel(lg_ref, q_ref, k_ref, v_ref, g_ref, o_ref):
    seq = q_ref.shape[1]
    c = RET_CHUNK
    lg = lg_ref[pl.program_id(1)]
    row = lax.broadcasted_iota(I32, (c, c), 0)
    col = lax.broadcasted_iota(I32, (c, c), 1)
    diff = (row - col).astype(F32)
    dmask = jnp.where(diff >= 0, jnp.exp(lg * jnp.maximum(diff, 0.0)), 0.0)
    idx = lax.broadcasted_iota(I32, (c, 1), 0).astype(F32)
    q_decay = jnp.exp(lg * (idx + 1.0))
    k_decay = jnp.exp(lg * (c - 1.0 - idx))
    chunk_decay = jnp.exp(jnp.full((1, 1), lg * c, F32))
    state = jnp.zeros((RET_DK, RET_DK), F32)
    for n in range(seq // c):
        rows = slice(n * c, (n + 1) * c)
        qn = q_ref[0, rows, :]
        kn = k_ref[0, rows, :]
        vn = v_ref[0, rows, :]
        scores = _nt_dot(qn, kn) * dmask
        inner = _dot(scores.astype(BF16), vn)
        qs = (qn.astype(F32) * q_decay).astype(BF16)
        cross = _dot(qs, state.astype(BF16))
        o = inner + cross
        o = o * lax.rsqrt(jnp.mean(o * o, axis=-1, keepdims=True) + EPS)
        gn = g_ref[0, rows, :].astype(F32)
        o_ref[0, rows, :] = (_silu(gn) * o).astype(BF16)
        ks = (kn.astype(F32) * k_decay).astype(BF16)
        state = state * chunk_decay + _tn_dot(ks, vn)


def _moba_kernel(q_ref, k_ref, v_ref, qg_ref, kg_ref, o_ref, qt_s, ka_s, kb_s, vta_s, vtb_s):
    seq = q_ref.shape[1]
    lb = MOBA_BLOCK
    nb = seq // lb
    lane = lax.broadcasted_iota(I32, (1, LANES), 1)
    is_a = lane < MOBA_DH

    def head_norm(xf, gain):
        sq = xf * xf
        s_a = jnp.sum(jnp.where(is_a, sq, 0.0), axis=-1, keepdims=True)
        s_b = jnp.sum(jnp.where(is_a, 0.0, sq), axis=-1, keepdims=True)
        inv = jnp.where(is_a, lax.rsqrt(s_a / MOBA_DH + EPS), lax.rsqrt(s_b / MOBA_DH + EPS))
        return xf * inv * gain

    qg = qg_ref[...]
    kg = kg_ref[...]
    k_means = []
    for j in range(nb):
        rows = slice(j * lb, (j + 1) * lb)
        kf = head_norm(k_ref[0, rows, :].astype(F32), kg)
        ka_s[rows, :] = jnp.where(is_a, kf, 0.0).astype(BF16)
        kb_s[rows, :] = jnp.where(is_a, 0.0, kf).astype(BF16)
        k_means.append(jnp.mean(kf, axis=0, keepdims=True))
        qf = head_norm(q_ref[0, rows, :].astype(F32), qg)
        qt_s[:, rows] = (qf * (MOBA_DH ** -0.5 * LOG2_E)).T.astype(BF16)
        vt = v_ref[0, rows, :].T
        ones = jnp.ones((MOBA_ONES_ROWS, lb), BF16)
        vta_s[:, rows] = jnp.concatenate([vt[:MOBA_DH], ones], axis=0)
        vtb_s[:, rows] = jnp.concatenate([vt[MOBA_DH:], ones], axis=0)
    k_mean = jnp.concatenate(k_means + [jnp.zeros((BF16_TILE_ROWS - nb, LANES), F32)], axis=0)
    k_mean_h = (jnp.where(is_a, k_mean, 0.0).astype(BF16), jnp.where(is_a, 0.0, k_mean).astype(BF16))
    k_s = (ka_s, kb_s)
    vt_s = (vta_s, vtb_s)

    r_loc = lax.broadcasted_iota(I32, (lb, lb), 0)
    c_loc = lax.broadcasted_iota(I32, (lb, lb), 1)
    causal = r_loc <= c_loc

    for i in range(nb):
        cols = slice(i * lb, (i + 1) * lb)
        qt = qt_s[:, cols]
        outs = []
        for hx in range(2):
            bias = [None] * i
            if i > MOBA_TOPK:
                gate = _dot(k_mean_h[hx], qt)
                g = [gate[j:j + 1, :] for j in range(i)]
                for j in range(i):
                    rank = jnp.zeros((1, lb), F32)
                    for j2 in range(i):
                        if j2 == j:
                            continue
                        beats = (g[j2] >= g[j]) if j2 < j else (g[j2] > g[j])
                        rank = rank + jnp.where(beats, 1.0, 0.0)
                    bias[j] = jnp.where(rank < float(MOBA_TOPK), 0.0, NEG_INF).astype(BF16)
            pieces = []
            for j in range(i + 1):
                s = _dot(k_s[hx][j * lb:(j + 1) * lb, :], qt).astype(BF16)
                if j == i:
                    s = jnp.where(causal, s, jnp.asarray(NEG_INF, BF16))
                elif bias[j] is not None:
                    s = s + bias[j]
                pieces.append(s)
            mx = jnp.max(pieces[0], axis=0, keepdims=True)
            for s in pieces[1:]:
                mx = jnp.maximum(mx, jnp.max(s, axis=0, keepdims=True))
            acc = jnp.zeros((MOBA_DH + MOBA_ONES_ROWS, lb), F32)
            for j, s in enumerate(pieces):
                acc = acc + _dot(vt_s[hx][:, j * lb:(j + 1) * lb], jnp.exp2(s - mx))
            outs.append(acc[:MOBA_DH] / acc[MOBA_DH:MOBA_DH + 1, :])
        o_ref[0, cols, :] = jnp.concatenate(outs, axis=0).T.astype(BF16)


def _mixers_kernel(lg_ref, rq_ref, rk_ref, rv_ref, rg_ref, mq_ref, mk_ref, mv_ref, qg_ref, kg_ref,
                   ret_ref, moba_ref, *moba_scratch):
    _ret_kernel(lg_ref, rq_ref, rk_ref, rv_ref, rg_ref, ret_ref)
    _moba_kernel(mq_ref, mk_ref, mv_ref, qg_ref, kg_ref, moba_ref, *moba_scratch)


def _mixers(log_g, proj3, qg2, kg2):
    bsz, seq, _ = proj3.shape
    pairs = MOBA_HEADS // 2
    assert pairs == RET_HEADS and RET_DK == LANES
    blk = (1, seq, LANES)
    base = 4 * RET_HEADS

    def col(first):
        return pl.BlockSpec(blk, lambda b, p, lg: (b, 0, first + p))

    return pl.pallas_call(
        _mixers_kernel,
        out_shape=(jax.ShapeDtypeStruct((bsz, seq, RET_HEADS * RET_DK), BF16),
                   jax.ShapeDtypeStruct((bsz, seq, MOBA_HEADS * MOBA_DH), BF16)),
        grid_spec=pltpu.PrefetchScalarGridSpec(
            num_scalar_prefetch=1,
            grid=(bsz, pairs),
            in_specs=[
                col(0), col(RET_HEADS), col(2 * RET_HEADS), col(3 * RET_HEADS),
                col(base), col(base + pairs), col(base + 2 * pairs),
                pl.BlockSpec((1, LANES), lambda b, p, lg: (0, 0)),
                pl.BlockSpec((1, LANES), lambda b, p, lg: (0, 0)),
            ],
            out_specs=(col(0), col(0)),
            scratch_shapes=[
                pltpu.VMEM((LANES, seq), BF16),
                pltpu.VMEM((seq, LANES), BF16),
                pltpu.VMEM((seq, LANES), BF16),
                pltpu.VMEM((MOBA_DH + MOBA_ONES_ROWS, seq), BF16),
                pltpu.VMEM((MOBA_DH + MOBA_ONES_ROWS, seq), BF16),
            ],
        ),
        compiler_params=pltpu.CompilerParams(vmem_limit_bytes=VMEM_LIMIT),
        name="mixers",
    )(log_g, *([proj3] * 7), qg2, kg2)


def _mid_kernel(ret_ref, moba_ref, x_ref, mod_ref, wo1_ref, wo2_ref, g_ref, wr_ref, rb_ref,
                wsg_ref, wsu_ref, wsd_ref,
                xb_ref, h2_ref, e_ref, w_ref, rk_ref, cnt_ref, carry_s):
    i = pl.program_id(0)
    tm = x_ref.shape[0]

    @pl.when(i == 0)
    def _():
        carry_s[...] = jnp.zeros_like(carry_s)

    m = mod_ref[0]
    mixed = _dot(ret_ref[...], wo1_ref[...]) + _dot(moba_ref[...], wo2_ref[...])
    x1 = x_ref[...] + m[2:3] * mixed
    ms = jnp.mean(x1 * x1, axis=-1, keepdims=True)
    h2 = (x1 * lax.rsqrt(ms + EPS) * g_ref[...]) * (1.0 + m[4:5]) + m[3:4]
    h2_ref[...] = _pack_halves(h2)
    h2b = h2.astype(BF16)

    hid = _silu(_dot(h2b, wsg_ref[...])) * _dot(h2b, wsu_ref[...])
    xb_ref[...] = x1 + m[5:6] * _dot(hid.astype(BF16), wsd_ref[...])

    scores = jax.nn.sigmoid(_nt_dot(wr_ref[...], h2b))
    biased = scores + rb_ref[...]
    grp = biased.reshape(N_GROUPS, GROUP_SIZE, tm)
    gi = lax.broadcasted_iota(I32, (N_GROUPS, GROUP_SIZE, tm), 1).astype(F32)
    top1 = jnp.max(grp, axis=1, keepdims=True)
    first = jnp.min(jnp.where(grp == top1, gi, float(GROUP_SIZE)), axis=1, keepdims=True)
    top2 = jnp.max(jnp.where(gi == first, NEG_INF, grp), axis=1, keepdims=True)
    gscore = (top1 + top2).reshape(N_GROUPS, tm)
    gidx = lax.broadcasted_iota(I32, (N_GROUPS, tm), 0)
    grank = jnp.zeros((N_GROUPS, tm), F32)
    for g2 in range(N_GROUPS):
        rowv = gscore[g2:g2 + 1, :]
        beats = (rowv > gscore) | ((rowv == gscore) & (g2 < gidx))
        grank = grank + jnp.where(beats, 1.0, 0.0)
    gsel = jnp.where(grank < float(TOPK_GROUPS), 1.0, 0.0)
    emask = jnp.broadcast_to(gsel.reshape(N_GROUPS, 1, tm), (N_GROUPS, GROUP_SIZE, tm)).reshape(N_EXPERTS, tm)
    choice = jnp.where(emask > 0.5, biased, NEG_INF)

    eidx = lax.broadcasted_iota(I32, (N_EXPERTS, tm), 0).astype(F32)
    remaining = choice
    e_rows = []
    for _k in range(TOP_K):
        mx = jnp.max(remaining, axis=0, keepdims=True)
        idx = jnp.min(jnp.where(remaining == mx, eidx, float(N_EXPERTS)), axis=0, keepdims=True)
        e_rows.append(idx)
        remaining = jnp.where(eidx == idx, NEG_INF, remaining)
    selmask = jnp.where(remaining != choice, 1.0, 0.0)

    tr = lax.broadcasted_iota(I32, (tm, tm), 0)
    tc = lax.broadcasted_iota(I32, (tm, tm), 1)
    upper = jnp.where(tr < tc, 1.0, 0.0).astype(BF16)
    prefix = _dot(selmask.astype(BF16), upper) + carry_s[...]
    w_rows = []
    r_rows = []
    for ek in e_rows:
        onehot = eidx == ek
        w_rows.append(jnp.sum(jnp.where(onehot, scores, 0.0), axis=0, keepdims=True))
        r_rows.append(jnp.sum(jnp.where(onehot, prefix, 0.0), axis=0, keepdims=True))
    wsum = w_rows[0]
    for wk in w_rows[1:]:
        wsum = wsum + wk
    carry_s[...] = carry_s[...] + jnp.sum(selmask, axis=1, keepdims=True)

    e_ref[...] = jnp.concatenate(e_rows, axis=0).astype(I32)
    w_ref[...] = jnp.concatenate([wk / wsum * ROUTED_SCALE for wk in w_rows], axis=0)
    rk_ref[...] = jnp.concatenate(r_rows, axis=0).astype(I32)
    cnt_ref[...] = carry_s[...].astype(I32)


def _mid(ret2, moba2, x2, mod3, wo1, wo2, g_ffn, wr_t, rbias, wsg, wsu, wsd, seq):
    n, d = x2.shape
    tm = TM_MID
    tiles_per_seq = seq // tm
    half = ret2.shape[1]
    ff = wsg.shape[1]
    const = lambda i: (0, 0)
    row = lambda i: (i, 0)
    colt = lambda i: (0, i)
    return pl.pallas_call(
        _mid_kernel,
        out_shape=(
            jax.ShapeDtypeStruct((n, d), F32),
            jax.ShapeDtypeStruct((n, d // 2), I32),
            jax.ShapeDtypeStruct((TOP_K, n), I32),
            jax.ShapeDtypeStruct((TOP_K, n), F32),
            jax.ShapeDtypeStruct((TOP_K, n), I32),
            jax.ShapeDtypeStruct((N_EXPERTS, 1), I32),
        ),
        grid=(n // tm,),
        in_specs=[
            pl.BlockSpec((tm, half), row),
            pl.BlockSpec((tm, half), row),
            pl.BlockSpec((tm, d), row),
            pl.BlockSpec((1, N_MOD, d), lambda i: (i // tiles_per_seq, 0, 0)),
            pl.BlockSpec((half, d), const),
            pl.BlockSpec((half, d), const),
            pl.BlockSpec((1, d), const),
            pl.BlockSpec((N_EXPERTS, d), const),
            pl.BlockSpec((N_EXPERTS, 1), const),
            pl.BlockSpec((d, ff), const),
            pl.BlockSpec((d, ff), const),
            pl.BlockSpec((ff, d), const),
        ],
        out_specs=(
            pl.BlockSpec((tm, d), row),
            pl.BlockSpec((tm, d // 2), row),
            pl.BlockSpec((TOP_K, tm), colt),
            pl.BlockSpec((TOP_K, tm), colt),
            pl.BlockSpec((TOP_K, tm), colt),
            pl.BlockSpec((N_EXPERTS, 1), const),
        ),
        scratch_shapes=[pltpu.VMEM((N_EXPERTS, 1), F32)],
        compiler_params=pltpu.CompilerParams(
            dimension_semantics=("arbitrary",), vmem_limit_bytes=VMEM_LIMIT),
        name="mid",
    )(ret2, moba2, x2, mod3, wo1, wo2, g_ffn, wr_t, rbias, wsg, wsu, wsd)


ROW_EXP = 0
ROW_VALID = 1
ROW_FIRST = 2
ROW_AHEAD = 3
ROW_SLOT = 4
ROW_HEAD = 5
ROW_START = 6
SCHED_ROWS = 8


def _sched_kernel(cnt_ref, tab_ref):
    ne = N_EXPERTS
    nblk = tab_ref.shape[1]
    shift = FFN_BLOCK.bit_length() - 1
    e_sub = lax.broadcasted_iota(I32, (ne, ne), 0)
    e_lane = lax.broadcasted_iota(I32, (ne, ne), 1)
    e_col = lax.broadcasted_iota(I32, (ne, 1), 0).astype(F32)
    ids_row = lax.broadcasted_iota(I32, (1, ne), 1).astype(F32) + 1.0

    def to_row(col):
        return jnp.sum(jnp.where(e_sub == e_lane, col, 0.0), axis=0, keepdims=True)

    def running_col(row):
        return jnp.sum(jnp.where(e_lane <= e_sub, row, 0.0), axis=1, keepdims=True)

    def running_row(col):
        return jnp.sum(jnp.where(e_sub <= e_lane, col, 0.0), axis=0, keepdims=True)

    cnt_i = cnt_ref[...]
    cnt_col = cnt_i.astype(F32)
    pad_col = lax.shift_left(lax.shift_right_logical(cnt_i + (FFN_BLOCK - 1), shift), shift).astype(F32)
    pad_row = to_row(pad_col)
    ends_col = running_col(pad_row)
    start_col = ends_col - pad_col
    start_row = running_row(pad_col) - pad_row
    nreal = jnp.sum(pad_row, axis=1, keepdims=True) * (1.0 / FFN_BLOCK)

    step = lax.broadcasted_iota(I32, (1, nblk), 1)
    g = step.astype(F32)
    row0 = jnp.minimum(g, nreal - 1.0) * float(FFN_BLOCK)
    exp_g = jnp.sum(jnp.where(ends_col <= row0, 1.0, 0.0), axis=0, keepdims=True)
    mine = e_col == exp_g

    def per_block(col):
        return jnp.sum(jnp.where(mine, col, 0.0), axis=0, keepdims=True)

    valid = jnp.clip(per_block(cnt_col + start_col) - row0, 0.0, float(FFN_BLOCK))
    first = jnp.where(jnp.logical_and(g < nreal, per_block(start_col) == row0), 1.0, 0.0)

    used_col = jnp.where(cnt_col > 0.0, 1.0, 0.0)
    used_row = to_row(used_col)
    ord_col = running_col(used_row) - 1.0
    ord_row = running_row(used_col) - 1.0

    def used_at(pos):
        hit = jnp.logical_and(used_row > 0.0, ord_row == pos)
        return jnp.sum(jnp.where(hit, ids_row, 0.0), axis=1, keepdims=True) - 1.0

    ahead_col = used_at(ord_col + float(FFN_WEIGHT_SLOTS - 1))
    slot_col = ord_col - FFN_WEIGHT_SLOTS * jnp.floor((ord_col + 0.5) * (1.0 / FFN_WEIGHT_SLOTS))
    head = jnp.where(step == 0, nreal, 0.0)
    for j in range(FFN_WEIGHT_SLOTS - 1):
        head = head + jnp.where(step == 1 + j, used_at(jnp.full((1, 1), float(j), F32)), 0.0)
    rows = [exp_g, valid, first, per_block(ahead_col), per_block(slot_col), head,
            jnp.concatenate([start_row, jnp.zeros((1, nblk - ne), F32)], axis=1),
            jnp.zeros((1, nblk), F32)]
    tab_ref[...] = jnp.concatenate(rows, axis=0).astype(I32)


def _sched(cnt, nblk):
    assert FFN_BLOCK & (FFN_BLOCK - 1) == 0 and nblk >= N_EXPERTS
    return pl.pallas_call(
        _sched_kernel,
        out_shape=jax.ShapeDtypeStruct((SCHED_ROWS, nblk), I32),
        compiler_params=pltpu.CompilerParams(vmem_limit_bytes=VMEM_LIMIT),
        name="sched",
    )(cnt)


def _dest_kernel(tab_ref, e_ref, rk_ref, o_ref):
    e = e_ref[...]

    def body(ex, acc):
        return acc + jnp.where(e == ex, tab_ref[ROW_START, ex], 0)

    dest = lax.fori_loop(0, N_EXPERTS, body, rk_ref[...], unroll=8)
    for ch in range(o_ref.shape[0]):
        o_ref[ch] = dest[:, ch * SC_CHUNK:(ch + 1) * SC_CHUNK]


def _dest(tab, e_idx, rank):
    k, n = e_idx.shape
    tn = TN_DEST
    return pl.pallas_call(
        _dest_kernel,
        out_shape=jax.ShapeDtypeStruct((n // SC_CHUNK, k, SC_CHUNK), I32),
        grid_spec=pltpu.PrefetchScalarGridSpec(
            num_scalar_prefetch=1,
            grid=(n // tn,),
            in_specs=[
                pl.BlockSpec((k, tn), lambda i, s: (0, i)),
                pl.BlockSpec((k, tn), lambda i, s: (0, i)),
            ],
            out_specs=pl.BlockSpec((tn // SC_CHUNK, k, SC_CHUNK), lambda i, s: (i, 0, 0)),
        ),
        compiler_params=pltpu.CompilerParams(vmem_limit_bytes=VMEM_LIMIT),
        name="dest",
    )(tab, e_idx, rank)


def _sc_dispatch(h2p, dest3, total_rows):
    n, words = h2p.shape
    nchunks = n // SC_CHUNK
    per_worker = nchunks // (SC_CORES * SC_SUBCORES)
    mesh = plsc.VectorSubcoreMesh(core_axis_name="c", subcore_axis_name="s",
                                  num_cores=SC_CORES, num_subcores=SC_SUBCORES)

    @functools.partial(
        pl.kernel, mesh=mesh,
        out_type=jax.ShapeDtypeStruct((total_rows, words), I32),
        scratch_types=[
            pltpu.VMEM((TOP_K, SC_CHUNK), I32),
            pltpu.VMEM((SC_CHUNK, words), I32),
            pltpu.SemaphoreType.DMA,
        ],
        name="sc_dispatch",
    )
    def run(h_hbm, d_hbm, xs_hbm, idx_v, rows_v, sem):
        wid = lax.axis_index("s") * SC_CORES + lax.axis_index("c")

        @pl.loop(0, per_worker)
        def _(j):
            ch = wid * per_worker + j
            pltpu.sync_copy(d_hbm.at[ch], idx_v)
            pltpu.sync_copy(h_hbm.at[pl.ds(ch * SC_CHUNK, SC_CHUNK)], rows_v)
            copies = [pltpu.async_copy(rows_v, xs_hbm.at[idx_v.at[k]], sem) for k in range(TOP_K)]
            for cp in copies:
                cp.wait()

    return run(h2p, dest3)


def _ffn_kernel(tab_ref, x_hbm, wg_hbm, wu_hbm, wd_hbm, y_hbm,
                x_s, y_s, wg_s, wu_s, wd_s, sem_x, sem_y, sem):
    i = pl.program_id(0)
    nreal = tab_ref[ROW_HEAD, 0]
    rows_per = x_s.shape[1]
    half = x_s.shape[2]
    sizes = tuple(range(FFN_GRAIN, rows_per + 1, FFN_GRAIN))

    def rows_needed(g):
        return (tab_ref[ROW_VALID, g] + (FFN_GRAIN - 1)) // FFN_GRAIN * FFN_GRAIN

    def by_size(nrows, fn):
        for n in sizes:
            @pl.when(nrows == n)
            def _():
                fn(n)

    def row_copy(g, n):
        slot = lax.rem(g, FFN_LOOKAHEAD + 1)
        return pltpu.make_async_copy(
            x_hbm.at[pl.ds(g * rows_per, n)], x_s.at[slot, pl.ds(0, n)], sem_x.at[slot])

    def out_copy(g, n):
        slot = lax.rem(g, FFN_OUT_SLOTS)
        return pltpu.make_async_copy(
            y_s.at[slot, pl.ds(0, n)], y_hbm.at[pl.ds(g * rows_per, n)], sem_y.at[slot])

    def weight_copies(e, s):
        return (pltpu.make_async_copy(wg_hbm.at[e], wg_s.at[s], sem.at[s, 0]),
                pltpu.make_async_copy(wu_hbm.at[e], wu_s.at[s], sem.at[s, 1]),
                pltpu.make_async_copy(wd_hbm.at[e], wd_s.at[s], sem.at[s, 2]))

    @pl.when(i == 0)
    def _():
        for j in range(FFN_WEIGHT_SLOTS - 1):
            @pl.when(tab_ref[ROW_HEAD, 1 + j] >= 0)
            def _():
                for cp in weight_copies(tab_ref[ROW_HEAD, 1 + j], j):
                    cp.start()
        for g in range(FFN_LOOKAHEAD):
            @pl.when(g < nreal)
            def _():
                by_size(rows_needed(g), lambda n: row_copy(g, n).start())

    @pl.when(i < nreal)
    def _():
        s = tab_ref[ROW_SLOT, i]
        fetch = i + FFN_LOOKAHEAD

        @pl.when(fetch < nreal)
        def _():
            by_size(rows_needed(fetch), lambda n: row_copy(fetch, n).start())

        by_size(rows_needed(i), lambda n: row_copy(i, n).wait())

        @pl.when(tab_ref[ROW_FIRST, i] == 1)
        def _():
            for cp in weight_copies(tab_ref[ROW_EXP, i], s):
                cp.wait()

            ahead = tab_ref[ROW_AHEAD, i]

            @pl.when(ahead >= 0)
            def _():
                for cp in weight_copies(ahead, lax.rem(s + FFN_WEIGHT_SLOTS - 1, FFN_WEIGHT_SLOTS)):
                    cp.start()

        @pl.when(i >= FFN_OUT_SLOTS)
        def _():
            done = i - FFN_OUT_SLOTS
            by_size(rows_needed(done), lambda n: out_copy(done, n).wait())

        x_slot = lax.rem(i, FFN_LOOKAHEAD + 1)
        y_slot = lax.rem(i, FFN_OUT_SLOTS)
        valid = tab_ref[ROW_VALID, i]

        def expert_rows(n):
            r = lax.broadcasted_iota(I32, (n, 1), 0)
            x_lo, x_hi = _unpack_halves(jnp.where(r < valid, x_s[x_slot, pl.ds(0, n), :], 0))
            hg = _dot(x_lo, wg_s[s, :half, :]) + _dot(x_hi, wg_s[s, half:, :])
            hu = _dot(x_lo, wu_s[s, :half, :]) + _dot(x_hi, wu_s[s, half:, :])
            y_s[y_slot, pl.ds(0, n), :] = _pack_halves(_dot(_silu(hg) * hu, wd_s[s]))
            out_copy(i, n).start()

        by_size(rows_needed(i), expert_rows)

        @pl.when(i == nreal - 1)
        def _():
            for back in range(FFN_OUT_SLOTS):
                last = i - back

                @pl.when(last >= 0)
                def _():
                    by_size(rows_needed(last), lambda n: out_copy(last, n).wait())


def _ffn(tab, xs, w_gate, w_up, w_down):
    p, half = xs.shape
    d = 2 * half
    ff = w_gate.shape[2]
    return pl.pallas_call(
        _ffn_kernel,
        out_shape=jax.ShapeDtypeStruct((p, half), I32),
        grid_spec=pltpu.PrefetchScalarGridSpec(
            num_scalar_prefetch=1,
            grid=(tab.shape[1],),
            in_specs=[
                pl.BlockSpec(memory_space=pl.ANY),
                pl.BlockSpec(memory_space=pl.ANY),
                pl.BlockSpec(memory_space=pl.ANY),
                pl.BlockSpec(memory_space=pl.ANY),
            ],
            out_specs=pl.BlockSpec(memory_space=pl.ANY),
            scratch_shapes=[
                pltpu.VMEM((FFN_LOOKAHEAD + 1, FFN_BLOCK, half), I32),
                pltpu.VMEM((FFN_OUT_SLOTS, FFN_BLOCK, half), I32),
                pltpu.VMEM((FFN_WEIGHT_SLOTS, d, ff), F32),
                pltpu.VMEM((FFN_WEIGHT_SLOTS, d, ff), F32),
                pltpu.VMEM((FFN_WEIGHT_SLOTS, ff, d), F32),
                pltpu.SemaphoreType.DMA((FFN_LOOKAHEAD + 1,)),
                pltpu.SemaphoreType.DMA((FFN_OUT_SLOTS,)),
                pltpu.SemaphoreType.DMA((FFN_WEIGHT_SLOTS, 3)),
            ],
        ),
        compiler_params=pltpu.CompilerParams(
            dimension_semantics=("arbitrary",), vmem_limit_bytes=VMEM_LIMIT, has_side_effects=True),
        name="ffn",
    )(tab, xs, w_gate, w_up, w_down)


def _sc_gather(y, dest3):
    a, words = y.shape
    nchunks, _, chunk = dest3.shape
    n = nchunks * chunk
    per_worker = nchunks // (SC_CORES * SC_SUBCORES)
    nbuf = SC_GATHER_BUFS
    parts = chunk // SC_GATHER_ROWS
    items = [(c, k, h) for c in range(per_worker) for k in range(TOP_K) for h in range(parts)]
    mesh = plsc.VectorSubcoreMesh(core_axis_name="c", subcore_axis_name="s",
                                  num_cores=SC_CORES, num_subcores=SC_SUBCORES)

    @functools.partial(
        pl.kernel, mesh=mesh,
        out_type=jax.ShapeDtypeStruct((TOP_K, n, words), I32),
        scratch_types=[
            pltpu.VMEM((per_worker, TOP_K, chunk), I32),
            [pltpu.VMEM((SC_GATHER_ROWS, words), I32)] * nbuf,
            pltpu.SemaphoreType.DMA((nbuf,)),
            pltpu.SemaphoreType.DMA((nbuf,)),
        ],
        name="sc_gather",
    )
    def run(y_hbm, d_hbm, yt_hbm, idx_v, bufs, sem_g, sem_w):
        wid = lax.axis_index("s") * SC_CORES + lax.axis_index("c")
        pltpu.sync_copy(d_hbm.at[pl.ds(wid * per_worker, per_worker)], idx_v)

        def gather(m):
            c, k, h = items[m]
            idx = idx_v.at[c, k, pl.ds(h * SC_GATHER_ROWS, SC_GATHER_ROWS)]
            return pltpu.async_copy(y_hbm.at[idx], bufs[m % nbuf], sem_g.at[m % nbuf])

        def write(m):
            c, k, h = items[m]
            rows = pl.ds((wid * per_worker + c) * chunk + h * SC_GATHER_ROWS, SC_GATHER_ROWS)
            return pltpu.async_copy(bufs[m % nbuf], yt_hbm.at[k, rows], sem_w.at[m % nbuf])

        gathers = {m: gather(m) for m in range(min(nbuf - 1, len(items)))}
        writes = {}
        for m in range(len(items)):
            gathers.pop(m).wait()
            writes[m] = write(m)
            nxt = m + nbuf - 1
            if nxt < len(items):
                if m >= 1:
                    writes.pop(m - 1).wait()
                gathers[nxt] = gather(nxt)
        for m in sorted(writes):
            writes.pop(m).wait()

    return run(y, dest3)


def _combine_kernel(yt_ref, wt_ref, xb_ref, mod_ref, o_ref):
    half = yt_ref.shape[2]
    wt = wt_ref[...].T
    lo, hi = _unpack_halves(yt_ref[0])
    r_lo = lo * wt[:, 0:1]
    r_hi = hi * wt[:, 0:1]
    for k in range(1, TOP_K):
        lo, hi = _unpack_halves(yt_ref[k])
        r_lo = r_lo + lo * wt[:, k:k + 1]
        r_hi = r_hi + hi * wt[:, k:k + 1]
    gate = mod_ref[0][5:6]
    o_ref[:, :half] = xb_ref[:, :half] + gate[:, :half] * r_lo
    o_ref[:, half:] = xb_ref[:, half:] + gate[:, half:] * r_hi


def _combine(yt, w_k, xb, mod3, seq):
    n, d = xb.shape
    tm = TM_COMBINE
    tiles_per_seq = seq // tm
    return pl.pallas_call(
        _combine_kernel,
        out_shape=jax.ShapeDtypeStruct((n, d), F32),
        grid=(n // tm,),
        in_specs=[
            pl.BlockSpec((TOP_K, tm, d // 2), lambda i: (0, i, 0)),
            pl.BlockSpec((TOP_K, tm), lambda i: (0, i)),
            pl.BlockSpec((tm, d), lambda i: (i, 0)),
            pl.BlockSpec((1, N_MOD, d), lambda i: (i // tiles_per_seq, 0, 0)),
        ],
        out_specs=pl.BlockSpec((tm, d), lambda i: (i, 0)),
        compiler_params=pltpu.CompilerParams(vmem_limit_bytes=VMEM_LIMIT),
        name="combine",
    )(yt, w_k, xb, mod3)


def _rotary_tables(seq):
    half = RET_DK // 2
    inv = ROPE_BASE ** (-np.arange(half, dtype=np.float64) / half)
    ang = np.arange(seq, dtype=np.float64)[:, None] * inv[None, :]
    cos = np.cos(ang).astype(np.float32)
    sin = np.sin(ang).astype(np.float32)
    return (jnp.asarray(np.concatenate([cos, cos], axis=-1)),
            jnp.asarray(np.concatenate([-sin, sin], axis=-1)))


def kernel(x, c, w_ada, b_ada, g_mix, w_in, q_gain, k_gain, w_out, g_ffn, w_router, router_bias,
           w_gate, w_up, w_down, ws_gate, ws_up, ws_down):
    bsz, seq, d = x.shape
    n = bsz * seq
    depth = w_ada.shape[0]
    cos_full, sin_signed = _rotary_tables(seq)
    log_g = jnp.asarray(np.log1p(-np.exp2(-5.0 - np.arange(RET_HEADS, dtype=np.float64))).astype(np.float32))
    ret_w = RET_HEADS * RET_DK
    x2 = x.reshape(n, d)
    for l in range(depth):
        mod3 = _adaln(c, w_ada[l], b_ada[l]).reshape(bsz, N_MOD, d)
        proj = _inproj(x2, mod3, g_mix[l].reshape(1, d), w_in[l], cos_full, sin_signed, seq)
        proj3 = proj.reshape(bsz, seq, IN_COLS)
        qg2 = jnp.tile(q_gain[l].reshape(1, MOBA_DH), (1, 2))
        kg2 = jnp.tile(k_gain[l].reshape(1, MOBA_DH), (1, 2))
        ret, moba = _mixers(log_g, proj3, qg2, kg2)
        wo = w_out[l].astype(BF16)
        xb, h2, e_idx, w_k, rank, cnt = _mid(
            ret.reshape(n, ret_w), moba.reshape(n, MOBA_HEADS * MOBA_DH), x2, mod3,
            wo[:ret_w], wo[ret_w:], g_ffn[l].reshape(1, d),
            w_router[l].T.astype(BF16), router_bias[l].reshape(N_EXPERTS, 1),
            ws_gate[l].astype(BF16), ws_up[l].astype(BF16), ws_down[l].astype(BF16), seq)
        nblk = n * TOP_K // FFN_BLOCK + N_EXPERTS
        tab = _sched(cnt, nblk)
        dest3 = _dest(tab, e_idx, rank)
        xs = _sc_dispatch(h2, dest3, nblk * FFN_BLOCK)
        y = _ffn(tab, xs, w_gate[l], w_up[l], w_down[l])
        yt = _sc_gather(y, dest3)
        x2 = _combine(yt, w_k, xb, mod3, seq)
    return x2.reshape(bsz, seq, d)
```

```python
import functools

import numpy as np
import jax
import jax.numpy as jnp
from jax import lax
from jax.experimental import pallas as pl
from jax.experimental.pallas import tpu as pltpu
from jax.experimental.pallas import tpu_sc as plsc

F32 = jnp.float32
BF16 = jnp.bfloat16
I32 = jnp.int32

RET_HEADS = 4
RET_DK = 128
MOBA_HEADS = 8
MOBA_DH = 64
MOBA_BLOCK = 256
MOBA_TOPK = 3
ROPE_BASE = 10000.0
N_EXPERTS = 256
TOP_K = 8
N_GROUPS = 8
TOPK_GROUPS = 4
GROUP_SIZE = N_EXPERTS // N_GROUPS
ROUTED_SCALE = 2.5
N_MOD = 6
EPS = 1e-6
IN_COLS = 3584

LANES = 128
BF16_TILE_ROWS = 16
RET_CHUNK = 256
TN_ADALN = 1024
TM_INPROJ = 512
TM_MID = 512
TN_DEST = 2048
TM_COMBINE = 512
MOBA_ONES_ROWS = BF16_TILE_ROWS
SC_CORES = 2
SC_SUBCORES = 16
SC_CHUNK = 128
SC_GATHER_ROWS = 32
SC_GATHER_BUFS = 6
FFN_BLOCK = 1024
FFN_GRAIN = 128
FFN_LOOKAHEAD = 5
FFN_WEIGHT_SLOTS = 3
FFN_OUT_SLOTS = 3
VMEM_LIMIT = 56 * 1024 * 1024

NEG_INF = float("-inf")
LOG2_E = 1.4426950408889634


def _silu(x):
    return x * jax.nn.sigmoid(x)


def _nt_dot(a, b):
    return lax.dot_general(a, b, (((1,), (1,)), ((), ())), preferred_element_type=F32)


def _tn_dot(a, b):
    return lax.dot_general(a, b, (((0,), (0,)), ((), ())), preferred_element_type=F32)


def _dot(a, b):
    return jnp.dot(a, b, preferred_element_type=F32)


HI_MASK = -65536


def _pack_halves(v):
    w = v.shape[1] // 2
    lo = lax.bitcast_convert_type(v[:, :w].astype(BF16).astype(F32), I32)
    hi = lax.bitcast_convert_type(v[:, w:].astype(BF16).astype(F32), I32)
    return lax.shift_right_logical(lo, 16) | (hi & HI_MASK)


def _unpack_halves(u):
    lo = lax.bitcast_convert_type(lax.shift_left(u, 16), F32)
    hi = lax.bitcast_convert_type(u & HI_MASK, F32)
    return lo, hi


def _adaln_kernel(c_ref, w_ref, b_ref, o_ref):
    s = _silu(c_ref[...])
    o_ref[...] = _dot(s.astype(BF16), w_ref[...].astype(BF16)) + b_ref[...]


def _adaln(c, w_ada, b_ada):
    bsz, d = c.shape
    ncol = w_ada.shape[1]
    tn = TN_ADALN
    return pl.pallas_call(
        _adaln_kernel,
        out_shape=jax.ShapeDtypeStruct((bsz, ncol), F32),
        grid=(ncol // tn,),
        in_specs=[
            pl.BlockSpec((bsz, d), lambda j: (0, 0)),
            pl.BlockSpec((d, tn), lambda j: (0, j)),
            pl.BlockSpec((1, tn), lambda j: (0, j)),
        ],
        out_specs=pl.BlockSpec((bsz, tn), lambda j: (0, j)),
        compiler_params=pltpu.CompilerParams(vmem_limit_bytes=VMEM_LIMIT),
        name="adaln",
    )(c, w_ada, b_ada.reshape(1, ncol))


def _inproj_kernel(x_ref, mod_ref, g_ref, w_ref, cos_ref, sin_ref, o_ref):
    x = x_ref[...]
    ms = jnp.mean(x * x, axis=-1, keepdims=True)
    m = mod_ref[0]
    h = (x * lax.rsqrt(ms + EPS) * g_ref[...]) * (1.0 + m[1:2]) + m[0:1]
    hb = h.astype(BF16)
    cosf = cos_ref[...]
    sinf = sin_ref[...]
    k_scale = RET_DK ** -0.5
    width = RET_HEADS * RET_DK
    for ci in range(IN_COLS // width):
        acc = _dot(hb, w_ref[:, ci * width:(ci + 1) * width])
        if ci < 2:
            for hh in range(RET_HEADS):
                xh = acc[:, hh * RET_DK:(hh + 1) * RET_DK]
                r = xh * cosf + pltpu.roll(xh, RET_DK // 2, axis=1) * sinf
                if ci == 1:
                    r = r * k_scale
                o_ref[:, ci * width + hh * RET_DK:ci * width + (hh + 1) * RET_DK] = r.astype(BF16)
        else:
            o_ref[:, ci * width:(ci + 1) * width] = acc.astype(BF16)


def _inproj(x2, mod3, g_mix, w_in, cos_full, sin_signed, seq):
    n, d = x2.shape
    tm = TM_INPROJ
    tiles_per_seq = seq // tm
    return pl.pallas_call(
        _inproj_kernel,
        out_shape=jax.ShapeDtypeStruct((n, IN_COLS), BF16),
        grid=(n // tm,),
        in_specs=[
            pl.BlockSpec((tm, d), lambda i: (i, 0)),
            pl.BlockSpec((1, N_MOD, d), lambda i: (i // tiles_per_seq, 0, 0)),
            pl.BlockSpec((1, d), lambda i: (0, 0)),
            pl.BlockSpec((d, IN_COLS), lambda i: (0, 0), pipeline_mode=pl.Buffered(1)),
            pl.BlockSpec((tm, LANES), lambda i: (i % tiles_per_seq, 0)),
            pl.BlockSpec((tm, LANES), lambda i: (i % tiles_per_seq, 0)),
        ],
        out_specs=pl.BlockSpec((tm, IN_COLS), lambda i: (i, 0)),
        compiler_params=pltpu.CompilerParams(vmem_limit_bytes=VMEM_LIMIT),
        name="inproj",
    )(x2, mod3, g_mix, w_in, cos_full, sin_signed)


def _ret_kernel(lg_ref, q_ref, k_ref, v_ref, g_ref, o_ref):
    seq = q_ref.shape[1]
    c = RET_CHUNK
    lg = lg_ref[pl.program_id(1)]
    row = lax.broadcasted_iota(I32, (c, c), 0)
    col = lax.broadcasted_iota(I32, (c, c), 1)
    diff = (row - col).astype(F32)
    dmask = jnp.where(diff >= 0, jnp.exp(lg * jnp.maximum(diff, 0.0)), 0.0)
    idx = lax.broadcasted_iota(I32, (c, 1), 0).astype(F32)
    q_decay = jnp.exp(lg * (idx + 1.0))
    k_decay = jnp.exp(lg * (c - 1.0 - idx))
    chunk_decay = jnp.exp(jnp.full((1, 1), lg * c, F32))
    state = jnp.zeros((RET_DK, RET_DK), F32)
    for n in range(seq // c):
        rows = slice(n * c, (n + 1) * c)
        qn = q_ref[0, rows, :]
        kn = k_ref[0, rows, :]
        vn = v_ref[0, rows, :]
        scores = _nt_dot(qn, kn) * dmask
        inner = _dot(scores.astype(BF16), vn)
        qs = (qn.astype(F32) * q_decay).astype(BF16)
        cross = _dot(qs, state.astype(BF16))
        o = inner + cross
        o = o * lax.rsqrt(jnp.mean(o * o, axis=-1, keepdims=True) + EPS)
        gn = g_ref[0, rows, :].astype(F32)
        o_ref[0, rows, :] = (_silu(gn) * o).astype(BF16)
        ks = (kn.astype(F32) * k_decay).astype(BF16)
        state = state * chunk_decay + _tn_dot(ks, vn)


def _moba_kernel(q_ref, k_ref, v_ref, qg_ref, kg_ref, o_ref, qt_s, ka_s, kb_s, vta_s, vtb_s):
    seq = q_ref.shape[1]
    lb = MOBA_BLOCK
    nb = seq // lb
    lane = lax.broadcasted_iota(I32, (1, LANES), 1)
    is_a = lane < MOBA_DH

    def head_norm(xf, gain):
        sq = xf * xf
        s_a = jnp.sum(jnp.where(is_a, sq, 0.0), axis=-1, keepdims=True)
        s_b = jnp.sum(jnp.where(is_a, 0.0, sq), axis=-1, keepdims=True)
        inv = jnp.where(is_a, lax.rsqrt(s_a / MOBA_DH + EPS), lax.rsqrt(s_b / MOBA_DH + EPS))
        return xf * inv * gain

    qg = qg_ref[...]
    kg = kg_ref[...]
    k_means = []
    for j in range(nb):
        rows = slice(j * lb, (j + 1) * lb)
        kf = head_norm(k_ref[0, rows, :].astype(F32), kg)
        ka_s[rows, :] = jnp.where(is_a, kf, 0.0).astype(BF16)
        kb_s[rows, :] = jnp.where(is_a, 0.0, kf).astype(BF16)
        k_means.append(jnp.mean(kf, axis=0, keepdims=True))
        qf = head_norm(q_ref[0, rows, :].astype(F32), qg)
        qt_s[:, rows] = (qf * (MOBA_DH ** -0.5 * LOG2_E)).T.astype(BF16)
        vt = v_ref[0, rows, :].T
        ones = jnp.ones((MOBA_ONES_ROWS, lb), BF16)
        vta_s[:, rows] = jnp.concatenate([vt[:MOBA_DH], ones], axis=0)
        vtb_s[:, rows] = jnp.concatenate([vt[MOBA_DH:], ones], axis=0)
    k_mean = jnp.concatenate(k_means + [jnp.zeros((BF16_TILE_ROWS - nb, LANES), F32)], axis=0)
    k_mean_h = (jnp.where(is_a, k_mean, 0.0).astype(BF16), jnp.where(is_a, 0.0, k_mean).astype(BF16))
    k_s = (ka_s, kb_s)
    vt_s = (vta_s, vtb_s)

    r_loc = lax.broadcasted_iota(I32, (lb, lb), 0)
    c_loc = lax.broadcasted_iota(I32, (lb, lb), 1)
    causal = r_loc <= c_loc

    for i in range(nb):
        cols = slice(i * lb, (i + 1) * lb)
        qt = qt_s[:, cols]
        outs = []
        for hx in range(2):
            bias = [None] * i
            if i > MOBA_TOPK:
                gate = _dot(k_mean_h[hx], qt)
                g = [gate[j:j + 1, :] for j in range(i)]
                for j in range(i):
                    rank = jnp.zeros((1, lb), F32)
                    for j2 in range(i):
                        if j2 == j:
                            continue
                        beats = (g[j2] >= g[j]) if j2 < j else (g[j2] > g[j])
                        rank = rank + jnp.where(beats, 1.0, 0.0)
                    bias[j] = jnp.where(rank < float(MOBA_TOPK), 0.0, NEG_INF).astype(BF16)
            pieces = []
            for j in range(i + 1):
                s = _dot(k_s[hx][j * lb:(j + 1) * lb, :], qt).astype(BF16)
                if j == i:
                    s = jnp.where(causal, s, jnp.asarray(NEG_INF, BF16))
                elif bias[j] is not None:
                    s = s + bias[j]
                pieces.append(s)
            mx = jnp.max(pieces[0], axis=0, keepdims=True)
            for s in pieces[1:]:
                mx = jnp.maximum(mx, jnp.max(s, axis=0, keepdims=True))
            acc = jnp.zeros((MOBA_DH + MOBA_ONES_ROWS, lb), F32)
            for j, s in enumerate(pieces):
                acc = acc + _dot(vt_s[hx][:, j * lb:(j + 1) * lb], jnp.exp2(s - mx))
            outs.append(acc[:MOBA_DH] / acc[MOBA_DH:MOBA_DH + 1, :])
        o_ref[0, cols, :] = jnp.concatenate(outs, axis=0).T.astype(BF16)


def _mixers_kernel(lg_ref, rq_ref, rk_ref, rv_ref, rg_ref, mq_ref, mk_ref, mv_ref, qg_ref, kg_ref,
                   ret_ref, moba_ref, *moba_scratch):
    _ret_kernel(lg_ref, rq_ref, rk_ref, rv_ref, rg_ref, ret_ref)
    _moba_kernel(mq_ref, mk_ref, mv_ref, qg_ref, kg_ref, moba_ref, *moba_scratch)


def _mixers(log_g, proj3, qg2, kg2):
    bsz, seq, _ = proj3.shape
    pairs = MOBA_HEADS // 2
    assert pairs == RET_HEADS and RET_DK == LANES
    blk = (1, seq, LANES)
    base = 4 * RET_HEADS

    def col(first):
        return pl.BlockSpec(blk, lambda b, p, lg: (b, 0, first + p))

    return pl.pallas_call(
        _mixers_kernel,
        out_shape=(jax.ShapeDtypeStruct((bsz, seq, RET_HEADS * RET_DK), BF16),
                   jax.ShapeDtypeStruct((bsz, seq, MOBA_HEADS * MOBA_DH), BF16)),
        grid_spec=pltpu.PrefetchScalarGridSpec(
            num_scalar_prefetch=1,
            grid=(bsz, pairs),
            in_specs=[
                col(0), col(RET_HEADS), col(2 * RET_HEADS), col(3 * RET_HEADS),
                col(base), col(base + pairs), col(base + 2 * pairs),
                pl.BlockSpec((1, LANES), lambda b, p, lg: (0, 0)),
                pl.BlockSpec((1, LANES), lambda b, p, lg: (0, 0)),
            ],
            out_specs=(col(0), col(0)),
            scratch_shapes=[
                pltpu.VMEM((LANES, seq), BF16),
                pltpu.VMEM((seq, LANES), BF16),
                pltpu.VMEM((seq, LANES), BF16),
                pltpu.VMEM((MOBA_DH + MOBA_ONES_ROWS, seq), BF16),
                pltpu.VMEM((MOBA_DH + MOBA_ONES_ROWS, seq), BF16),
            ],
        ),
        compiler_params=pltpu.CompilerParams(vmem_limit_bytes=VMEM_LIMIT),
        name="mixers",
    )(log_g, *([proj3] * 7), qg2, kg2)


def _mid_kernel(ret_ref, moba_ref, x_ref, mod_ref, wo1_ref, wo2_ref, g_ref, wr_ref, rb_ref,
                wsg_ref, wsu_ref, wsd_ref,
                xb_ref, h2_ref, e_ref, w_ref, rk_ref, cnt_ref, carry_s):
    i = pl.program_id(0)
    tm = x_ref.shape[0]

    @pl.when(i == 0)
    def _():
        carry_s[...] = jnp.zeros_like(carry_s)

    m = mod_ref[0]
    mixed = _dot(ret_ref[...], wo1_ref[...]) + _dot(moba_ref[...], wo2_ref[...])
    x1 = x_ref[...] + m[2:3] * mixed
    ms = jnp.mean(x1 * x1, axis=-1, keepdims=True)
    h2 = (x1 * lax.rsqrt(ms + EPS) * g_ref[...]) * (1.0 + m[4:5]) + m[3:4]
    h2_ref[...] = _pack_halves(h2)
    h2b = h2.astype(BF16)

    hid = _silu(_dot(h2b, wsg_ref[...])) * _dot(h2b, wsu_ref[...])
    xb_ref[...] = x1 + m[5:6] * _dot(hid.astype(BF16), wsd_ref[...])

    scores = jax.nn.sigmoid(_nt_dot(wr_ref[...], h2b))
    biased = scores + rb_ref[...]
    grp = biased.reshape(N_GROUPS, GROUP_SIZE, tm)
    gi = lax.broadcasted_iota(I32, (N_GROUPS, GROUP_SIZE, tm), 1).astype(F32)
    top1 = jnp.max(grp, axis=1, keepdims=True)
    first = jnp.min(jnp.where(grp == top1, gi, float(GROUP_SIZE)), axis=1, keepdims=True)
    top2 = jnp.max(jnp.where(gi == first, NEG_INF, grp), axis=1, keepdims=True)
    gscore = (top1 + top2).reshape(N_GROUPS, tm)
    gidx = lax.broadcasted_iota(I32, (N_GROUPS, tm), 0)
    grank = jnp.zeros((N_GROUPS, tm), F32)
    for g2 in range(N_GROUPS):
        rowv = gscore[g2:g2 + 1, :]
        beats = (rowv > gscore) | ((rowv == gscore) & (g2 < gidx))
        grank = grank + jnp.where(beats, 1.0, 0.0)
    gsel = jnp.where(grank < float(TOPK_GROUPS), 1.0, 0.0)
    emask = jnp.broadcast_to(gsel.reshape(N_GROUPS, 1, tm), (N_GROUPS, GROUP_SIZE, tm)).reshape(N_EXPERTS, tm)
    choice = jnp.where(emask > 0.5, biased, NEG_INF)

    eidx = lax.broadcasted_iota(I32, (N_EXPERTS, tm), 0).astype(F32)
    remaining = choice
    e_rows = []
    for _k in range(TOP_K):
        mx = jnp.max(remaining, axis=0, keepdims=True)
        idx = jnp.min(jnp.where(remaining == mx, eidx, float(N_EXPERTS)), axis=0, keepdims=True)
        e_rows.append(idx)
        remaining = jnp.where(eidx == idx, NEG_INF, remaining)
    selmask = jnp.where(remaining != choice, 1.0, 0.0)

    tr = lax.broadcasted_iota(I32, (tm, tm), 0)
    tc = lax.broadcasted_iota(I32, (tm, tm), 1)
    upper = jnp.where(tr < tc, 1.0, 0.0).astype(BF16)
    prefix = _dot(selmask.astype(BF16), upper) + carry_s[...]
    w_rows = []
    r_rows = []
    for ek in e_rows:
        onehot = eidx == ek
        w_rows.append(jnp.sum(jnp.where(onehot, scores, 0.0), axis=0, keepdims=True))
        r_rows.append(jnp.sum(jnp.where(onehot, prefix, 0.0), axis=0, keepdims=True))
    wsum = w_rows[0]
    for wk in w_rows[1:]:
        wsum = wsum + wk
    carry_s[...] = carry_s[...] + jnp.sum(selmask, axis=1, keepdims=True)

    e_ref[...] = jnp.concatenate(e_rows, axis=0).astype(I32)
    w_ref[...] = jnp.concatenate([wk / wsum * ROUTED_SCALE for wk in w_rows], axis=0)
    rk_ref[...] = jnp.concatenate(r_rows, axis=0).astype(I32)
    cnt_ref[...] = carry_s[...].astype(I32)


def _mid(ret2, moba2, x2, mod3, wo1, wo2, g_ffn, wr_t, rbias, wsg, wsu, wsd, seq):
    n, d = x2.shape
    tm = TM_MID
    tiles_per_seq = seq // tm
    half = ret2.shape[1]
    ff = wsg.shape[1]
    const = lambda i: (0, 0)
    row = lambda i: (i, 0)
    colt = lambda i: (0, i)
    return pl.pallas_call(
        _mid_kernel,
        out_shape=(
            jax.ShapeDtypeStruct((n, d), F32),
            jax.ShapeDtypeStruct((n, d // 2), I32),
            jax.ShapeDtypeStruct((TOP_K, n), I32),
            jax.ShapeDtypeStruct((TOP_K, n), F32),
            jax.ShapeDtypeStruct((TOP_K, n), I32),
            jax.ShapeDtypeStruct((N_EXPERTS, 1), I32),
        ),
        grid=(n // tm,),
        in_specs=[
            pl.BlockSpec((tm, half), row),
            pl.BlockSpec((tm, half), row),
            pl.BlockSpec((tm, d), row),
            pl.BlockSpec((1, N_MOD, d), lambda i: (i // tiles_per_seq, 0, 0)),
            pl.BlockSpec((half, d), const),
            pl.BlockSpec((half, d), const),
            pl.BlockSpec((1, d), const),
            pl.BlockSpec((N_EXPERTS, d), const),
            pl.BlockSpec((N_EXPERTS, 1), const),
            pl.BlockSpec((d, ff), const),
            pl.BlockSpec((d, ff), const),
            pl.BlockSpec((ff, d), const),
        ],
        out_specs=(
            pl.BlockSpec((tm, d), row),
            pl.BlockSpec((tm, d // 2), row),
            pl.BlockSpec((TOP_K, tm), colt),
            pl.BlockSpec((TOP_K, tm), colt),
            pl.BlockSpec((TOP_K, tm), colt),
            pl.BlockSpec((N_EXPERTS, 1), const),
        ),
        scratch_shapes=[pltpu.VMEM((N_EXPERTS, 1), F32)],
        compiler_params=pltpu.CompilerParams(
            dimension_semantics=("arbitrary",), vmem_limit_bytes=VMEM_LIMIT),
        name="mid",
    )(ret2, moba2, x2, mod3, wo1, wo2, g_ffn, wr_t, rbias, wsg, wsu, wsd)


ROW_EXP = 0
ROW_VALID = 1
ROW_FIRST = 2
ROW_AHEAD = 3
ROW_SLOT = 4
ROW_HEAD = 5
ROW_START = 6
SCHED_ROWS = 8


def _sched_kernel(cnt_ref, tab_ref):
    ne = N_EXPERTS
    nblk = tab_ref.shape[1]
    shift = FFN_BLOCK.bit_length() - 1
    e_sub = lax.broadcasted_iota(I32, (ne, ne), 0)
    e_lane = lax.broadcasted_iota(I32, (ne, ne), 1)
    e_col = lax.broadcasted_iota(I32, (ne, 1), 0).astype(F32)
    ids_row = lax.broadcasted_iota(I32, (1, ne), 1).astype(F32) + 1.0

    def to_row(col):
        return jnp.sum(jnp.where(e_sub == e_lane, col, 0.0), axis=0, keepdims=True)

    def running_col(row):
        return jnp.sum(jnp.where(e_lane <= e_sub, row, 0.0), axis=1, keepdims=True)

    def running_row(col):
        return jnp.sum(jnp.where(e_sub <= e_lane, col, 0.0), axis=0, keepdims=True)

    cnt_i = cnt_ref[...]
    cnt_col = cnt_i.astype(F32)
    pad_col = lax.shift_left(lax.shift_right_logical(cnt_i + (FFN_BLOCK - 1), shift), shift).astype(F32)
    pad_row = to_row(pad_col)
    ends_col = running_col(pad_row)
    start_col = ends_col - pad_col
    start_row = running_row(pad_col) - pad_row
    nreal = jnp.sum(pad_row, axis=1, keepdims=True) * (1.0 / FFN_BLOCK)

    step = lax.broadcasted_iota(I32, (1, nblk), 1)
    g = step.astype(F32)
    row0 = jnp.minimum(g, nreal - 1.0) * float(FFN_BLOCK)
    exp_g = jnp.sum(jnp.where(ends_col <= row0, 1.0, 0.0), axis=0, keepdims=True)
    mine = e_col == exp_g

    def per_block(col):
        return jnp.sum(jnp.where(mine, col, 0.0), axis=0, keepdims=True)

    valid = jnp.clip(per_block(cnt_col + start_col) - row0, 0.0, float(FFN_BLOCK))
    first = jnp.where(jnp.logical_and(g < nreal, per_block(start_col) == row0), 1.0, 0.0)

    used_col = jnp.where(cnt_col > 0.0, 1.0, 0.0)
    used_row = to_row(used_col)
    ord_col = running_col(used_row) - 1.0
    ord_row = running_row(used_col) - 1.0

    def used_at(pos):
        hit = jnp.logical_and(used_row > 0.0, ord_row == pos)
        return jnp.sum(jnp.where(hit, ids_row, 0.0), axis=1, keepdims=True) - 1.0

    ahead_col = used_at(ord_col + float(FFN_WEIGHT_SLOTS - 1))
    slot_col = ord_col - FFN_WEIGHT_SLOTS * jnp.floor((ord_col + 0.5) * (1.0 / FFN_WEIGHT_SLOTS))
    head = jnp.where(step == 0, nreal, 0.0)
    for j in range(FFN_WEIGHT_SLOTS - 1):
        head = head + jnp.where(step == 1 + j, used_at(jnp.full((1, 1), float(j), F32)), 0.0)
    rows = [exp_g, valid, first, per_block(ahead_col), per_block(slot_col), head,
            jnp.concatenate([start_row, jnp.zeros((1, nblk - ne), F32)], axis=1),
            jnp.zeros((1, nblk), F32)]
    tab_ref[...] = jnp.concatenate(rows, axis=0).astype(I32)


def _sched(cnt, nblk):
    assert FFN_BLOCK & (FFN_BLOCK - 1) == 0 and nblk >= N_EXPERTS
    return pl.pallas_call(
        _sched_kernel,
        out_shape=jax.ShapeDtypeStruct((SCHED_ROWS, nblk), I32),
        compiler_params=pltpu.CompilerParams(vmem_limit_bytes=VMEM_LIMIT),
        name="sched",
    )(cnt)


def _dest_kernel(tab_ref, e_ref, rk_ref, o_ref):
    e = e_ref[...]

    def body(ex, acc):
        return acc + jnp.where(e == ex, tab_ref[ROW_START, ex], 0)

    dest = lax.fori_loop(0, N_EXPERTS, body, rk_ref[...], unroll=8)
    for ch in range(o_ref.shape[0]):
        o_ref[ch] = dest[:, ch * SC_CHUNK:(ch + 1) * SC_CHUNK]


def _dest(tab, e_idx, rank):
    k, n = e_idx.shape
    tn = TN_DEST
    return pl.pallas_call(
        _dest_kernel,
        out_shape=jax.ShapeDtypeStruct((n // SC_CHUNK, k, SC_CHUNK), I32),
        grid_spec=pltpu.PrefetchScalarGridSpec(
            num_scalar_prefetch=1,
            grid=(n // tn,),
            in_specs=[
                pl.BlockSpec((k, tn), lambda i, s: (0, i)),
                pl.BlockSpec((k, tn), lambda i, s: (0, i)),
            ],
            out_specs=pl.BlockSpec((tn // SC_CHUNK, k, SC_CHUNK), lambda i, s: (i, 0, 0)),
        ),
        compiler_params=pltpu.CompilerParams(vmem_limit_bytes=VMEM_LIMIT),
        name="dest",
    )(tab, e_idx, rank)


def _sc_dispatch(h2p, dest3, total_rows):
    n, words = h2p.shape
    nchunks = n // SC_CHUNK
    per_worker = nchunks // (SC_CORES * SC_SUBCORES)
    mesh = plsc.VectorSubcoreMesh(core_axis_name="c", subcore_axis_name="s",
                                  num_cores=SC_CORES, num_subcores=SC_SUBCORES)

    @functools.partial(
        pl.kernel, mesh=mesh,
        out_type=jax.ShapeDtypeStruct((total_rows, words), I32),
        scratch_types=[
            pltpu.VMEM((TOP_K, SC_CHUNK), I32),
            pltpu.VMEM((SC_CHUNK, words), I32),
            pltpu.SemaphoreType.DMA,
        ],
        name="sc_dispatch",
    )
    def run(h_hbm, d_hbm, xs_hbm, idx_v, rows_v, sem):
        wid = lax.axis_index("s") * SC_CORES + lax.axis_index("c")

        @pl.loop(0, per_worker)
        def _(j):
            ch = wid * per_worker + j
            pltpu.sync_copy(d_hbm.at[ch], idx_v)
            pltpu.sync_copy(h_hbm.at[pl.ds(ch * SC_CHUNK, SC_CHUNK)], rows_v)
            copies = [pltpu.async_copy(rows_v, xs_hbm.at[idx_v.at[k]], sem) for k in range(TOP_K)]
            for cp in copies:
                cp.wait()

    return run(h2p, dest3)


def _ffn_kernel(tab_ref, x_hbm, wg_hbm, wu_hbm, wd_hbm, y_hbm,
                x_s, y_s, wg_s, wu_s, wd_s, sem_x, sem_y, sem):
    i = pl.program_id(0)
    nreal = tab_ref[ROW_HEAD, 0]
    rows_per = x_s.shape[1]
    half = x_s.shape[2]
    sizes = tuple(range(FFN_GRAIN, rows_per + 1, FFN_GRAIN))

    def rows_needed(g):
        return (tab_ref[ROW_VALID, g] + (FFN_GRAIN - 1)) // FFN_GRAIN * FFN_GRAIN

    def by_size(nrows, fn):
        for n in sizes:
            @pl.when(nrows == n)
            def _():
                fn(n)

    def row_copy(g, n):
        slot = lax.rem(g, FFN_LOOKAHEAD + 1)
        return pltpu.make_async_copy(
            x_hbm.at[pl.ds(g * rows_per, n)], x_s.at[slot, pl.ds(0, n)], sem_x.at[slot])

    def out_copy(g, n):
        slot = lax.rem(g, FFN_OUT_SLOTS)
        return pltpu.make_async_copy(
            y_s.at[slot, pl.ds(0, n)], y_hbm.at[pl.ds(g * rows_per, n)], sem_y.at[slot])

    def weight_copies(e, s):
        return (pltpu.make_async_copy(wg_hbm.at[e], wg_s.at[s], sem.at[s, 0]),
                pltpu.make_async_copy(wu_hbm.at[e], wu_s.at[s], sem.at[s, 1]),
                pltpu.make_async_copy(wd_hbm.at[e], wd_s.at[s], sem.at[s, 2]))

    @pl.when(i == 0)
    def _():
        for j in range(FFN_WEIGHT_SLOTS - 1):
            @pl.when(tab_ref[ROW_HEAD, 1 + j] >= 0)
            def _():
                for cp in weight_copies(tab_ref[ROW_HEAD, 1 + j], j):
                    cp.start()
        for g in range(FFN_LOOKAHEAD):
            @pl.when(g < nreal)
            def _():
                by_size(rows_needed(g), lambda n: row_copy(g, n).start())

    @pl.when(i < nreal)
    def _():
        s = tab_ref[ROW_SLOT, i]
        fetch = i + FFN_LOOKAHEAD

        @pl.when(fetch < nreal)
        def _():
            by_size(rows_needed(fetch), lambda n: row_copy(fetch, n).start())

        by_size(rows_needed(i), lambda n: row_copy(i, n).wait())

        @pl.when(tab_ref[ROW_FIRST, i] == 1)
        def _():
            for cp in weight_copies(tab_ref[ROW_EXP, i], s):
                cp.wait()

            ahead = tab_ref[ROW_AHEAD, i]

            @pl.when(ahead >= 0)
            def _():
                for cp in weight_copies(ahead, lax.rem(s + FFN_WEIGHT_SLOTS - 1, FFN_WEIGHT_SLOTS)):
                    cp.start()

        @pl.when(i >= FFN_OUT_SLOTS)
        def _():
            done = i - FFN_OUT_SLOTS
            by_size(rows_needed(done), lambda n: out_copy(done, n).wait())

        x_slot = lax.rem(i, FFN_LOOKAHEAD + 1)
        y_slot = lax.rem(i, FFN_OUT_SLOTS)
        valid = tab_ref[ROW_VALID, i]

        def expert_rows(n):
            r = lax.broadcasted_iota(I32, (n, 1), 0)
            x_lo, x_hi = _unpack_halves(jnp.where(r < valid, x_s[x_slot, pl.ds(0, n), :], 0))
            hg = _dot(x_lo, wg_s[s, :half, :]) + _dot(x_hi, wg_s[s, half:, :])
            hu = _dot(x_lo, wu_s[s, :half, :]) + _dot(x_hi, wu_s[s, half:, :])
            y_s[y_slot, pl.ds(0, n), :] = _pack_halves(_dot(_silu(hg) * hu, wd_s[s]))
            out_copy(i, n).start()

        by_size(rows_needed(i), expert_rows)

        @pl.when(i == nreal - 1)
        def _():
            for back in range(FFN_OUT_SLOTS):
                last = i - back

                @pl.when(last >= 0)
                def _():
                    by_size(rows_needed(last), lambda n: out_copy(last, n).wait())


def _ffn(tab, xs, w_gate, w_up, w_down):
    p, half = xs.shape
    d = 2 * half
    ff = w_gate.shape[2]
    return pl.pallas_call(
        _ffn_kernel,
        out_shape=jax.ShapeDtypeStruct((p, half), I32),
        grid_spec=pltpu.PrefetchScalarGridSpec(
            num_scalar_prefetch=1,
            grid=(tab.shape[1],),
            in_specs=[
                pl.BlockSpec(memory_space=pl.ANY),
                pl.BlockSpec(memory_space=pl.ANY),
                pl.BlockSpec(memory_space=pl.ANY),
                pl.BlockSpec(memory_space=pl.ANY),
            ],
            out_specs=pl.BlockSpec(memory_space=pl.ANY),
            scratch_shapes=[
                pltpu.VMEM((FFN_LOOKAHEAD + 1, FFN_BLOCK, half), I32),
                pltpu.VMEM((FFN_OUT_SLOTS, FFN_BLOCK, half), I32),
                pltpu.VMEM((FFN_WEIGHT_SLOTS, d, ff), F32),
                pltpu.VMEM((FFN_WEIGHT_SLOTS, d, ff), F32),
                pltpu.VMEM((FFN_WEIGHT_SLOTS, ff, d), F32),
                pltpu.SemaphoreType.DMA((FFN_LOOKAHEAD + 1,)),
                pltpu.SemaphoreType.DMA((FFN_OUT_SLOTS,)),
                pltpu.SemaphoreType.DMA((FFN_WEIGHT_SLOTS, 3)),
            ],
        ),
        compiler_params=pltpu.CompilerParams(
            dimension_semantics=("arbitrary",), vmem_limit_bytes=VMEM_LIMIT, has_side_effects=True),
        name="ffn",
    )(tab, xs, w_gate, w_up, w_down)


def _sc_gather(y, dest3):
    a, words = y.shape
    nchunks, _, chunk = dest3.shape
    n = nchunks * chunk
    per_worker = nchunks // (SC_CORES * SC_SUBCORES)
    nbuf = SC_GATHER_BUFS
    parts = chunk // SC_GATHER_ROWS
    items = [(c, k, h) for c in range(per_worker) for k in range(TOP_K) for h in range(parts)]
    mesh = plsc.VectorSubcoreMesh(core_axis_name="c", subcore_axis_name="s",
                                  num_cores=SC_CORES, num_subcores=SC_SUBCORES)

    @functools.partial(
        pl.kernel, mesh=mesh,
        out_type=jax.ShapeDtypeStruct((TOP_K, n, words), I32),
        scratch_types=[
            pltpu.VMEM((per_worker, TOP_K, chunk), I32),
            [pltpu.VMEM((SC_GATHER_ROWS, words), I32)] * nbuf,
            pltpu.SemaphoreType.DMA((nbuf,)),
            pltpu.SemaphoreType.DMA((nbuf,)),
        ],
        name="sc_gather",
    )
    def run(y_hbm, d_hbm, yt_hbm, idx_v, bufs, sem_g, sem_w):
        wid = lax.axis_index("s") * SC_CORES + lax.axis_index("c")
        pltpu.sync_copy(d_hbm.at[pl.ds(wid * per_worker, per_worker)], idx_v)

        def gather(m):
            c, k, h = items[m]
            idx = idx_v.at[c, k, pl.ds(h * SC_GATHER_ROWS, SC_GATHER_ROWS)]
            return pltpu.async_copy(y_hbm.at[idx], bufs[m % nbuf], sem_g.at[m % nbuf])

        def write(m):
            c, k, h = items[m]
            rows = pl.ds((wid * per_worker + c) * chunk + h * SC_GATHER_ROWS, SC_GATHER_ROWS)
            return pltpu.async_copy(bufs[m % nbuf], yt_hbm.at[k, rows], sem_w.at[m % nbuf])

        gathers = {m: gather(m) for m in range(min(nbuf - 1, len(items)))}
        writes = {}
        for m in range(len(items)):
            gathers.pop(m).wait()
            writes[m] = write(m)
            nxt = m + nbuf - 1
            if nxt < len(items):
                if m >= 1:
                    writes.pop(m - 1).wait()
                gathers[nxt] = gather(nxt)
        for m in sorted(writes):
            writes.pop(m).wait()

    return run(y, dest3)


def _combine_kernel(yt_ref, wt_ref, xb_ref, mod_ref, o_ref):
    half = yt_ref.shape[2]
    wt = wt_ref[...].T
    lo, hi = _unpack_halves(yt_ref[0])
    r_lo = lo * wt[:, 0:1]
    r_hi = hi * wt[:, 0:1]
    for k in range(1, TOP_K):
        lo, hi = _unpack_halves(yt_ref[k])
        r_lo = r_lo + lo * wt[:, k:k + 1]
        r_hi = r_hi + hi * wt[:, k:k + 1]
    gate = mod_ref[0][5:6]
    o_ref[:, :half] = xb_ref[:, :half] + gate[:, :half] * r_lo
    o_ref[:, half:] = xb_ref[:, half:] + gate[:, half:] * r_hi


def _combine(yt, w_k, xb, mod3, seq):
    n, d = xb.shape
    tm = TM_COMBINE
    tiles_per_seq = seq // tm
    return pl.pallas_call(
        _combine_kernel,
        out_shape=jax.ShapeDtypeStruct((n, d), F32),
        grid=(n // tm,),
        in_specs=[
            pl.BlockSpec((TOP_K, tm, d // 2), lambda i: (0, i, 0)),
            pl.BlockSpec((TOP_K, tm), lambda i: (0, i)),
            pl.BlockSpec((tm, d), lambda i: (i, 0)),
            pl.BlockSpec((1, N_MOD, d), lambda i: (i // tiles_per_seq, 0, 0)),
        ],
        out_specs=pl.BlockSpec((tm, d), lambda i: (i, 0)),
        compiler_params=pltpu.CompilerParams(vmem_limit_bytes=VMEM_LIMIT),
        name="combine",
    )(yt, w_k, xb, mod3)


def _rotary_tables(seq):
    half = RET_DK // 2
    inv = ROPE_BASE ** (-np.arange(half, dtype=np.float64) / half)
    ang = np.arange(seq, dtype=np.float64)[:, None] * inv[None, :]
    cos = np.cos(ang).astype(np.float32)
    sin = np.sin(ang).astype(np.float32)
    return (jnp.asarray(np.concatenate([cos, cos], axis=-1)),
            jnp.asarray(np.concatenate([-sin, sin], axis=-1)))


def kernel(x, c, w_ada, b_ada, g_mix, w_in, q_gain, k_gain, w_out, g_ffn, w_router, router_bias,
           w_gate, w_up, w_down, ws_gate, ws_up, ws_down):
    bsz, seq, d = x.shape
    n = bsz * seq
    depth = w_ada.shape[0]
    cos_full, sin_signed = _rotary_tables(seq)
    log_g = jnp.asarray(np.log1p(-np.exp2(-5.0 - np.arange(RET_HEADS, dtype=np.float64))).astype(np.float32))
    ret_w = RET_HEADS * RET_DK
    x2 = x.reshape(n, d)
    for l in range(depth):
        mod3 = _adaln(c, w_ada[l], b_ada[l]).reshape(bsz, N_MOD, d)
        proj = _inproj(x2, mod3, g_mix[l].reshape(1, d), w_in[l], cos_full, sin_signed, seq)
        proj3 = proj.reshape(bsz, seq, IN_COLS)
        qg2 = jnp.tile(q_gain[l].reshape(1, MOBA_DH), (1, 2))
        kg2 = jnp.tile(k_gain[l].reshape(1, MOBA_DH), (1, 2))
        ret, moba = _mixers(log_g, proj3, qg2, kg2)
        wo = w_out[l].astype(BF16)
        xb, h2, e_idx, w_k, rank, cnt = _mid(
            ret.reshape(n, ret_w), moba.reshape(n, MOBA_HEADS * MOBA_DH), x2, mod3,
            wo[:ret_w], wo[ret_w:], g_ffn[l].reshape(1, d),
            w_router[l].T.astype(BF16), router_bias[l].reshape(N_EXPERTS, 1),
            ws_gate[l].astype(BF16), ws_up[l].astype(BF16), ws_down[l].astype(BF16), seq)
        nblk = n * TOP_K // FFN_BLOCK + N_EXPERTS
        tab = _sched(cnt, nblk)
        dest3 = _dest(tab, e_idx, rank)
        xs = _sc_dispatch(h2, dest3, nblk * FFN_BLOCK)
        y = _ffn(tab, xs, w_gate[l], w_up[l], w_down[l])
        yt = _sc_gather(y, dest3)
        x2 = _combine(yt, w_k, xb, mod3, seq)
    return x2.reshape(bsz, seq, d)
```

```python
import functools

import numpy as np
import jax
import jax.numpy as jnp
from jax import lax
from jax.experimental import pallas as pl
from jax.experimental.pallas import tpu as pltpu
from jax.experimental.pallas import tpu_sc as plsc

F32 = jnp.float32
BF16 = jnp.bfloat16
I32 = jnp.int32

RET_HEADS = 4
RET_DK = 128
MOBA_HEADS = 8
MOBA_DH = 64
MOBA_BLOCK = 256
MOBA_TOPK = 3
ROPE_BASE = 10000.0
N_EXPERTS = 256
TOP_K = 8
N_GROUPS = 8
TOPK_GROUPS = 4
GROUP_SIZE = N_EXPERTS // N_GROUPS
ROUTED_SCALE = 2.5
N_MOD = 6
EPS = 1e-6
IN_COLS = 3584

LANES = 128
BF16_TILE_ROWS = 16
RET_CHUNK = 256
TN_ADALN = 1024
TM_INPROJ = 512
TM_MID = 512
TN_DEST = 2048
TM_COMBINE = 512
MOBA_ONES_ROWS = BF16_TILE_ROWS
SC_CORES = 2
SC_SUBCORES = 16
SC_CHUNK = 128
SC_GATHER_ROWS = 64
SC_GATHER_BUFS = 3
FFN_BLOCK = 1024
FFN_GRAIN = 128
FFN_LOOKAHEAD = 5
FFN_WEIGHT_SLOTS = 3
FFN_OUT_SLOTS = 3
VMEM_LIMIT = 56 * 1024 * 1024

NEG_INF = float("-inf")
LOG2_E = 1.4426950408889634


def _silu(x):
    return x * jax.nn.sigmoid(x)


def _nt_dot(a, b):
    return lax.dot_general(a, b, (((1,), (1,)), ((), ())), preferred_element_type=F32)


def _tn_dot(a, b):
    return lax.dot_general(a, b, (((0,), (0,)), ((), ())), preferred_element_type=F32)


def _dot(a, b):
    return jnp.dot(a, b, preferred_element_type=F32)


HI_MASK = -65536


def _pack_halves(v):
    w = v.shape[1] // 2
    lo = lax.bitcast_convert_type(v[:, :w].astype(BF16).astype(F32), I32)
    hi = lax.bitcast_convert_type(v[:, w:].astype(BF16).astype(F32), I32)
    return lax.shift_right_logical(lo, 16) | (hi & HI_MASK)


def _unpack_halves(u):
    lo = lax.bitcast_convert_type(lax.shift_left(u, 16), F32)
    hi = lax.bitcast_convert_type(u & HI_MASK, F32)
    return lo, hi


def _adaln_kernel(c_ref, w_ref, b_ref, o_ref):
    s = _silu(c_ref[...])
    o_ref[...] = _dot(s.astype(BF16), w_ref[...].astype(BF16)) + b_ref[...]


def _adaln(c, w_ada, b_ada):
    bsz, d = c.shape
    ncol = w_ada.shape[1]
    tn = TN_ADALN
    return pl.pallas_call(
        _adaln_kernel,
        out_shape=jax.ShapeDtypeStruct((bsz, ncol), F32),
        grid=(ncol // tn,),
        in_specs=[
            pl.BlockSpec((bsz, d), lambda j: (0, 0)),
            pl.BlockSpec((d, tn), lambda j: (0, j)),
            pl.BlockSpec((1, tn), lambda j: (0, j)),
        ],
        out_specs=pl.BlockSpec((bsz, tn), lambda j: (0, j)),
        compiler_params=pltpu.CompilerParams(vmem_limit_bytes=VMEM_LIMIT),
        name="adaln",
    )(c, w_ada, b_ada.reshape(1, ncol))


def _inproj_kernel(x_ref, mod_ref, g_ref, w_ref, cos_ref, sin_ref, o_ref):
    x = x_ref[...]
    ms = jnp.mean(x * x, axis=-1, keepdims=True)
    m = mod_ref[0]
    h = (x * lax.rsqrt(ms + EPS) * g_ref[...]) * (1.0 + m[1:2]) + m[0:1]
    hb = h.astype(BF16)
    cosf = cos_ref[...]
    sinf = sin_ref[...]
    k_scale = RET_DK ** -0.5
    width = RET_HEADS * RET_DK
    for ci in range(IN_COLS // width):
        acc = _dot(hb, w_ref[:, ci * width:(ci + 1) * width])
        if ci < 2:
            for hh in range(RET_HEADS):
                xh = acc[:, hh * RET_DK:(hh + 1) * RET_DK]
                r = xh * cosf + pltpu.roll(xh, RET_DK // 2, axis=1) * sinf
                if ci == 1:
                    r = r * k_scale
                o_ref[:, ci * width + hh * RET_DK:ci * width + (hh + 1) * RET_DK] = r.astype(BF16)
        else:
            o_ref[:, ci * width:(ci + 1) * width] = acc.astype(BF16)


def _inproj(x2, mod3, g_mix, w_in, cos_full, sin_signed, seq):
    n, d = x2.shape
    tm = TM_INPROJ
    tiles_per_seq = seq // tm
    return pl.pallas_call(
        _inproj_kernel,
        out_shape=jax.ShapeDtypeStruct((n, IN_COLS), BF16),
        grid=(n // tm,),
        in_specs=[
            pl.BlockSpec((tm, d), lambda i: (i, 0)),
            pl.BlockSpec((1, N_MOD, d), lambda i: (i // tiles_per_seq, 0, 0)),
            pl.BlockSpec((1, d), lambda i: (0, 0)),
            pl.BlockSpec((d, IN_COLS), lambda i: (0, 0), pipeline_mode=pl.Buffered(1)),
            pl.BlockSpec((tm, LANES), lambda i: (i % tiles_per_seq, 0)),
            pl.BlockSpec((tm, LANES), lambda i: (i % tiles_per_seq, 0)),
        ],
        out_specs=pl.BlockSpec((tm, IN_COLS), lambda i: (i, 0)),
        compiler_params=pltpu.CompilerParams(vmem_limit_bytes=VMEM_LIMIT),
        name="inproj",
    )(x2, mod3, g_mix, w_in, cos_full, sin_signed)


def _ret_kernel(lg_ref, q_ref, k_ref, v_ref, g_ref, o_ref):
    seq = q_ref.shape[1]
    c = RET_CHUNK
    lg = lg_ref[pl.program_id(1)]
    row = lax.broadcasted_iota(I32, (c, c), 0)
    col = lax.broadcasted_iota(I32, (c, c), 1)
    diff = (row - col).astype(F32)
    dmask = jnp.where(diff >= 0, jnp.exp(lg * jnp.maximum(diff, 0.0)), 0.0)
    idx = lax.broadcasted_iota(I32, (c, 1), 0).astype(F32)
    q_decay = jnp.exp(lg * (idx + 1.0))
    k_decay = jnp.exp(lg * (c - 1.0 - idx))
    chunk_decay = jnp.exp(jnp.full((1, 1), lg * c, F32))
    state = jnp.zeros((RET_DK, RET_DK), F32)
    for n in range(seq // c):
        rows = slice(n * c, (n + 1) * c)
        qn = q_ref[0, rows, :]
        kn = k_ref[0, rows, :]
        vn = v_ref[0, rows, :]
        scores = _nt_dot(qn, kn) * dmask
        inner = _dot(scores.astype(BF16), vn)
        qs = (qn.astype(F32) * q_decay).astype(BF16)
        cross = _dot(qs, state.astype(BF16))
        o = inner + cross
        o = o * lax.rsqrt(jnp.mean(o * o, axis=-1, keepdims=True) + EPS)
        gn = g_ref[0, rows, :].astype(F32)
        o_ref[0, rows, :] = (_silu(gn) * o).astype(BF16)
        ks = (kn.astype(F32) * k_decay).astype(BF16)
        state = state * chunk_decay + _tn_dot(ks, vn)


def _moba_kernel(q_ref, k_ref, v_ref, qg_ref, kg_ref, o_ref, qt_s, ka_s, kb_s, vta_s, vtb_s):
    seq = q_ref.shape[1]
    lb = MOBA_BLOCK
    nb = seq // lb
    lane = lax.broadcasted_iota(I32, (1, LANES), 1)
    is_a = lane < MOBA_DH

    def head_norm(xf, gain):
        sq = xf * xf
        s_a = jnp.sum(jnp.where(is_a, sq, 0.0), axis=-1, keepdims=True)
        s_b = jnp.sum(jnp.where(is_a, 0.0, sq), axis=-1, keepdims=True)
        inv = jnp.where(is_a, lax.rsqrt(s_a / MOBA_DH + EPS), lax.rsqrt(s_b / MOBA_DH + EPS))
        return xf * inv * gain

    qg = qg_ref[...]
    kg = kg_ref[...]
    k_means = []
    for j in range(nb):
        rows = slice(j * lb, (j + 1) * lb)
        kf = head_norm(k_ref[0, rows, :].astype(F32), kg)
        ka_s[rows, :] = jnp.where(is_a, kf, 0.0).astype(BF16)
        kb_s[rows, :] = jnp.where(is_a, 0.0, kf).astype(BF16)
        k_means.append(jnp.mean(kf, axis=0, keepdims=True))
        qf = head_norm(q_ref[0, rows, :].astype(F32), qg)
        qt_s[:, rows] = (qf * (MOBA_DH ** -0.5 * LOG2_E)).T.astype(BF16)
        vt = v_ref[0, rows, :].T
        ones = jnp.ones((MOBA_ONES_ROWS, lb), BF16)
        vta_s[:, rows] = jnp.concatenate([vt[:MOBA_DH], ones], axis=0)
        vtb_s[:, rows] = jnp.concatenate([vt[MOBA_DH:], ones], axis=0)
    k_mean = jnp.concatenate(k_means + [jnp.zeros((BF16_TILE_ROWS - nb, LANES), F32)], axis=0)
    k_mean_h = (jnp.where(is_a, k_mean, 0.0).astype(BF16), jnp.where(is_a, 0.0, k_mean).astype(BF16))
    k_s = (ka_s, kb_s)
    vt_s = (vta_s, vtb_s)

    r_loc = lax.broadcasted_iota(I32, (lb, lb), 0)
    c_loc = lax.broadcasted_iota(I32, (lb, lb), 1)
    causal = r_loc <= c_loc

    for i in range(nb):
        cols = slice(i * lb, (i + 1) * lb)
        qt = qt_s[:, cols]
        outs = []
        for hx in range(2):
            bias = [None] * i
            if i > MOBA_TOPK:
                gate = _dot(k_mean_h[hx], qt)
                g = [gate[j:j + 1, :] for j in range(i)]
                for j in range(i):
                    rank = jnp.zeros((1, lb), F32)
                    for j2 in range(i):
                        if j2 == j:
                            continue
                        beats = (g[j2] >= g[j]) if j2 < j else (g[j2] > g[j])
                        rank = rank + jnp.where(beats, 1.0, 0.0)
                    bias[j] = jnp.where(rank < float(MOBA_TOPK), 0.0, NEG_INF).astype(BF16)
            pieces = []
            for j in range(i + 1):
                s = _dot(k_s[hx][j * lb:(j + 1) * lb, :], qt).astype(BF16)
                if j == i:
                    s = jnp.where(causal, s, jnp.asarray(NEG_INF, BF16))
                elif bias[j] is not None:
                    s = s + bias[j]
                pieces.append(s)
            mx = jnp.max(pieces[0], axis=0, keepdims=True)
            for s in pieces[1:]:
                mx = jnp.maximum(mx, jnp.max(s, axis=0, keepdims=True))
            acc = jnp.zeros((MOBA_DH + MOBA_ONES_ROWS, lb), F32)
            for j, s in enumerate(pieces):
                acc = acc + _dot(vt_s[hx][:, j * lb:(j + 1) * lb], jnp.exp2(s - mx))
            outs.append(acc[:MOBA_DH] / acc[MOBA_DH:MOBA_DH + 1, :])
        o_ref[0, cols, :] = jnp.concatenate(outs, axis=0).T.astype(BF16)


def _mixers_kernel(lg_ref, rq_ref, rk_ref, rv_ref, rg_ref, mq_ref, mk_ref, mv_ref, qg_ref, kg_ref,
                   ret_ref, moba_ref, *moba_scratch):
    _ret_kernel(lg_ref, rq_ref, rk_ref, rv_ref, rg_ref, ret_ref)
    _moba_kernel(mq_ref, mk_ref, mv_ref, qg_ref, kg_ref, moba_ref, *moba_scratch)


def _mixers(log_g, proj3, qg2, kg2):
    bsz, seq, _ = proj3.shape
    pairs = MOBA_HEADS // 2
    assert pairs == RET_HEADS and RET_DK == LANES
    blk = (1, seq, LANES)
    base = 4 * RET_HEADS

    def col(first):
        return pl.BlockSpec(blk, lambda b, p, lg: (b, 0, first + p))

    return pl.pallas_call(
        _mixers_kernel,
        out_shape=(jax.ShapeDtypeStruct((bsz, seq, RET_HEADS * RET_DK), BF16),
                   jax.ShapeDtypeStruct((bsz, seq, MOBA_HEADS * MOBA_DH), BF16)),
        grid_spec=pltpu.PrefetchScalarGridSpec(
            num_scalar_prefetch=1,
            grid=(bsz, pairs),
            in_specs=[
                col(0), col(RET_HEADS), col(2 * RET_HEADS), col(3 * RET_HEADS),
                col(base), col(base + pairs), col(base + 2 * pairs),
                pl.BlockSpec((1, LANES), lambda b, p, lg: (0, 0)),
                pl.BlockSpec((1, LANES), lambda b, p, lg: (0, 0)),
            ],
            out_specs=(col(0), col(0)),
            scratch_shapes=[
                pltpu.VMEM((LANES, seq), BF16),
                pltpu.VMEM((seq, LANES), BF16),
                pltpu.VMEM((seq, LANES), BF16),
                pltpu.VMEM((MOBA_DH + MOBA_ONES_ROWS, seq), BF16),
                pltpu.VMEM((MOBA_DH + MOBA_ONES_ROWS, seq), BF16),
            ],
        ),
        compiler_params=pltpu.CompilerParams(vmem_limit_bytes=VMEM_LIMIT),
        name="mixers",
    )(log_g, *([proj3] * 7), qg2, kg2)


def _mid_kernel(ret_ref, moba_ref, x_ref, mod_ref, wo_ref, g_ref, wr_ref, rb_ref,
                wsg_ref, wsu_ref, wsd_ref,
                xb_ref, h2_ref, e_ref, w_ref, rk_ref, cnt_ref, carry_s, wo_s, wsg_s, wsu_s, wsd_s):
    i = pl.program_id(0)
    tm = x_ref.shape[0]
    half = ret_ref.shape[1]

    @pl.when(i == 0)
    def _():
        carry_s[...] = jnp.zeros_like(carry_s)
        wo_s[...] = wo_ref[...].astype(BF16)
        wsg_s[...] = wsg_ref[...].astype(BF16)
        wsu_s[...] = wsu_ref[...].astype(BF16)
        wsd_s[...] = wsd_ref[...].astype(BF16)

    m = mod_ref[0]
    mixed = _dot(ret_ref[...], wo_s[:half, :]) + _dot(moba_ref[...], wo_s[half:, :])
    x1 = x_ref[...] + m[2:3] * mixed
    ms = jnp.mean(x1 * x1, axis=-1, keepdims=True)
    h2 = (x1 * lax.rsqrt(ms + EPS) * g_ref[...]) * (1.0 + m[4:5]) + m[3:4]
    h2_ref[...] = _pack_halves(h2)
    h2b = h2.astype(BF16)

    hid = _silu(_dot(h2b, wsg_s[...])) * _dot(h2b, wsu_s[...])
    xb_ref[...] = x1 + m[5:6] * _dot(hid.astype(BF16), wsd_s[...])

    scores = jax.nn.sigmoid(_nt_dot(wr_ref[...], h2b))
    biased = scores + rb_ref[...]
    grp = biased.reshape(N_GROUPS, GROUP_SIZE, tm)
    gi = lax.broadcasted_iota(I32, (N_GROUPS, GROUP_SIZE, tm), 1).astype(F32)
    top1 = jnp.max(grp, axis=1, keepdims=True)
    first = jnp.min(jnp.where(grp == top1, gi, float(GROUP_SIZE)), axis=1, keepdims=True)
    top2 = jnp.max(jnp.where(gi == first, NEG_INF, grp), axis=1, keepdims=True)
    gscore = (top1 + top2).reshape(N_GROUPS, tm)
    gidx = lax.broadcasted_iota(I32, (N_GROUPS, tm), 0)
    grank = jnp.zeros((N_GROUPS, tm), F32)
    for g2 in range(N_GROUPS):
        rowv = gscore[g2:g2 + 1, :]
        beats = (rowv > gscore) | ((rowv == gscore) & (g2 < gidx))
        grank = grank + jnp.where(beats, 1.0, 0.0)
    gsel = jnp.where(grank < float(TOPK_GROUPS), 1.0, 0.0)
    emask = jnp.broadcast_to(gsel.reshape(N_GROUPS, 1, tm), (N_GROUPS, GROUP_SIZE, tm)).reshape(N_EXPERTS, tm)
    choice = jnp.where(emask > 0.5, biased, NEG_INF)

    eidx = lax.broadcasted_iota(I32, (N_EXPERTS, tm), 0).astype(F32)
    remaining = choice
    e_rows = []
    for _k in range(TOP_K):
        mx = jnp.max(remaining, axis=0, keepdims=True)
        idx = jnp.min(jnp.where(remaining == mx, eidx, float(N_EXPERTS)), axis=0, keepdims=True)
        e_rows.append(idx)
        remaining = jnp.where(eidx == idx, NEG_INF, remaining)
    selmask = jnp.where(remaining != choice, 1.0, 0.0)

    tr = lax.broadcasted_iota(I32, (tm, tm), 0)
    tc = lax.broadcasted_iota(I32, (tm, tm), 1)
    upper = jnp.where(tr < tc, 1.0, 0.0).astype(BF16)
    prefix = _dot(selmask.astype(BF16), upper) + carry_s[...]
    w_rows = []
    r_rows = []
    for ek in e_rows:
        onehot = eidx == ek
        w_rows.append(jnp.sum(jnp.where(onehot, scores, 0.0), axis=0, keepdims=True))
        r_rows.append(jnp.sum(jnp.where(onehot, prefix, 0.0), axis=0, keepdims=True))
    wsum = w_rows[0]
    for wk in w_rows[1:]:
        wsum = wsum + wk
    carry_s[...] = carry_s[...] + jnp.sum(selmask, axis=1, keepdims=True)

    e_ref[...] = jnp.concatenate(e_rows, axis=0).astype(I32)
    w_ref[...] = jnp.concatenate([wk / wsum * ROUTED_SCALE for wk in w_rows], axis=0)
    rk_ref[...] = jnp.concatenate(r_rows, axis=0).astype(I32)
    cnt_ref[...] = carry_s[...].astype(I32)


def _mid(ret2, moba2, x2, mod3, wo, g_ffn, wr_t, rbias, wsg, wsu, wsd, seq):
    n, d = x2.shape
    tm = TM_MID
    tiles_per_seq = seq // tm
    half = ret2.shape[1]
    ff = wsg.shape[1]
    const = lambda i: (0, 0)
    row = lambda i: (i, 0)
    colt = lambda i: (0, i)

    def resident(shape):
        return pl.BlockSpec(shape, const, pipeline_mode=pl.Buffered(1))

    return pl.pallas_call(
        _mid_kernel,
        out_shape=(
            jax.ShapeDtypeStruct((n, d), F32),
            jax.ShapeDtypeStruct((n, d // 2), I32),
            jax.ShapeDtypeStruct((TOP_K, n), I32),
            jax.ShapeDtypeStruct((TOP_K, n), F32),
            jax.ShapeDtypeStruct((TOP_K, n), I32),
            jax.ShapeDtypeStruct((N_EXPERTS, 1), I32),
        ),
        grid=(n // tm,),
        in_specs=[
            pl.BlockSpec((tm, half), row),
            pl.BlockSpec((tm, half), row),
            pl.BlockSpec((tm, d), row),
            pl.BlockSpec((1, N_MOD, d), lambda i: (i // tiles_per_seq, 0, 0)),
            resident(wo.shape),
            pl.BlockSpec((1, d), const),
            pl.BlockSpec((N_EXPERTS, d), const),
            pl.BlockSpec((N_EXPERTS, 1), const),
            resident((d, ff)),
            resident((d, ff)),
            resident((ff, d)),
        ],
        out_specs=(
            pl.BlockSpec((tm, d), row),
            pl.BlockSpec((tm, d // 2), row),
            pl.BlockSpec((TOP_K, tm), colt),
            pl.BlockSpec((TOP_K, tm), colt),
            pl.BlockSpec((TOP_K, tm), colt),
            pl.BlockSpec((N_EXPERTS, 1), const),
        ),
        scratch_shapes=[
            pltpu.VMEM((N_EXPERTS, 1), F32),
            pltpu.VMEM(wo.shape, BF16),
            pltpu.VMEM((d, ff), BF16),
            pltpu.VMEM((d, ff), BF16),
            pltpu.VMEM((ff, d), BF16),
        ],
        compiler_params=pltpu.CompilerParams(
            dimension_semantics=("arbitrary",), vmem_limit_bytes=VMEM_LIMIT),
        name="mid",
    )(ret2, moba2, x2, mod3, wo, g_ffn, wr_t, rbias, wsg, wsu, wsd)


ROW_EXP = 0
ROW_VALID = 1
ROW_FIRST = 2
ROW_AHEAD = 3
ROW_SLOT = 4
ROW_HEAD = 5
ROW_START = 6
SCHED_ROWS = 8


def _sched_kernel(cnt_ref, tab_ref):
    ne = N_EXPERTS
    nblk = tab_ref.shape[1]
    shift = FFN_BLOCK.bit_length() - 1
    e_sub = lax.broadcasted_iota(I32, (ne, ne), 0)
    e_lane = lax.broadcasted_iota(I32, (ne, ne), 1)
    e_col = lax.broadcasted_iota(I32, (ne, 1), 0).astype(F32)
    ids_row = lax.broadcasted_iota(I32, (1, ne), 1).astype(F32) + 1.0

    def to_row(col):
        return jnp.sum(jnp.where(e_sub == e_lane, col, 0.0), axis=0, keepdims=True)

    def running_col(row):
        return jnp.sum(jnp.where(e_lane <= e_sub, row, 0.0), axis=1, keepdims=True)

    def running_row(col):
        return jnp.sum(jnp.where(e_sub <= e_lane, col, 0.0), axis=0, keepdims=True)

    cnt_i = cnt_ref[...]
    cnt_col = cnt_i.astype(F32)
    pad_col = lax.shift_left(lax.shift_right_logical(cnt_i + (FFN_BLOCK - 1), shift), shift).astype(F32)
    pad_row = to_row(pad_col)
    ends_col = running_col(pad_row)
    start_col = ends_col - pad_col
    start_row = running_row(pad_col) - pad_row
    nreal = jnp.sum(pad_row, axis=1, keepdims=True) * (1.0 / FFN_BLOCK)

    step = lax.broadcasted_iota(I32, (1, nblk), 1)
    g = step.astype(F32)
    row0 = jnp.minimum(g, nreal - 1.0) * float(FFN_BLOCK)
    exp_g = jnp.sum(jnp.where(ends_col <= row0, 1.0, 0.0), axis=0, keepdims=True)
    mine = e_col == exp_g

    def per_block(col):
        return jnp.sum(jnp.where(mine, col, 0.0), axis=0, keepdims=True)

    valid = jnp.clip(per_block(cnt_col + start_col) - row0, 0.0, float(FFN_BLOCK))
    first = jnp.where(jnp.logical_and(g < nreal, per_block(start_col) == row0), 1.0, 0.0)

    used_col = jnp.where(cnt_col > 0.0, 1.0, 0.0)
    used_row = to_row(used_col)
    ord_col = running_col(used_row) - 1.0
    ord_row = running_row(used_col) - 1.0

    def used_at(pos):
        hit = jnp.logical_and(used_row > 0.0, ord_row == pos)
        return jnp.sum(jnp.where(hit, ids_row, 0.0), axis=1, keepdims=True) - 1.0

    ahead_col = used_at(ord_col + float(FFN_WEIGHT_SLOTS - 1))
    slot_col = ord_col - FFN_WEIGHT_SLOTS * jnp.floor((ord_col + 0.5) * (1.0 / FFN_WEIGHT_SLOTS))
    head = jnp.where(step == 0, nreal, 0.0)
    for j in range(FFN_WEIGHT_SLOTS - 1):
        head = head + jnp.where(step == 1 + j, used_at(jnp.full((1, 1), float(j), F32)), 0.0)
    rows = [exp_g, valid, first, per_block(ahead_col), per_block(slot_col), head,
            jnp.concatenate([start_row, jnp.zeros((1, nblk - ne), F32)], axis=1),
            jnp.zeros((1, nblk), F32)]
    tab_ref[...] = jnp.concatenate(rows, axis=0).astype(I32)


def _sched(cnt, nblk):
    assert FFN_BLOCK & (FFN_BLOCK - 1) == 0 and nblk >= N_EXPERTS
    return pl.pallas_call(
        _sched_kernel,
        out_shape=jax.ShapeDtypeStruct((SCHED_ROWS, nblk), I32),
        compiler_params=pltpu.CompilerParams(vmem_limit_bytes=VMEM_LIMIT),
        name="sched",
    )(cnt)


def _dest_kernel(tab_ref, e_ref, rk_ref, o_ref):
    e = e_ref[...]

    def body(ex, acc):
        return acc + jnp.where(e == ex, tab_ref[ROW_START, ex], 0)

    dest = lax.fori_loop(0, N_EXPERTS, body, rk_ref[...], unroll=8)
    for ch in range(o_ref.shape[0]):
        o_ref[ch] = dest[:, ch * SC_CHUNK:(ch + 1) * SC_CHUNK]


def _dest(tab, e_idx, rank):
    k, n = e_idx.shape
    tn = TN_DEST
    return pl.pallas_call(
        _dest_kernel,
        out_shape=jax.ShapeDtypeStruct((n // SC_CHUNK, k, SC_CHUNK), I32),
        grid_spec=pltpu.PrefetchScalarGridSpec(
            num_scalar_prefetch=1,
            grid=(n // tn,),
            in_specs=[
                pl.BlockSpec((k, tn), lambda i, s: (0, i)),
                pl.BlockSpec((k, tn), lambda i, s: (0, i)),
            ],
            out_specs=pl.BlockSpec((tn // SC_CHUNK, k, SC_CHUNK), lambda i, s: (i, 0, 0)),
        ),
        compiler_params=pltpu.CompilerParams(vmem_limit_bytes=VMEM_LIMIT),
        name="dest",
    )(tab, e_idx, rank)


def _sc_dispatch(h2p, dest3, total_rows):
    n, words = h2p.shape
    nchunks = n // SC_CHUNK
    per_worker = nchunks // (SC_CORES * SC_SUBCORES)
    mesh = plsc.VectorSubcoreMesh(core_axis_name="c", subcore_axis_name="s",
                                  num_cores=SC_CORES, num_subcores=SC_SUBCORES)

    @functools.partial(
        pl.kernel, mesh=mesh,
        out_type=jax.ShapeDtypeStruct((total_rows, words), I32),
        scratch_types=[
            pltpu.VMEM((TOP_K, SC_CHUNK), I32),
            pltpu.VMEM((SC_CHUNK, words), I32),
            pltpu.SemaphoreType.DMA,
        ],
        name="sc_dispatch",
    )
    def run(h_hbm, d_hbm, xs_hbm, idx_v, rows_v, sem):
        wid = lax.axis_index("s") * SC_CORES + lax.axis_index("c")

        @pl.loop(0, per_worker)
        def _(j):
            ch = wid * per_worker + j
            pltpu.sync_copy(d_hbm.at[ch], idx_v)
            pltpu.sync_copy(h_hbm.at[pl.ds(ch * SC_CHUNK, SC_CHUNK)], rows_v)
            copies = [pltpu.async_copy(rows_v, xs_hbm.at[idx_v.at[k]], sem) for k in range(TOP_K)]
            for cp in copies:
                cp.wait()

    return run(h2p, dest3)


def _ffn_kernel(tab_ref, x_hbm, wg_hbm, wu_hbm, wd_hbm, y_hbm,
                x_s, y_s, wg_s, wu_s, wd_s, sem_x, sem_y, sem):
    i = pl.program_id(0)
    nreal = tab_ref[ROW_HEAD, 0]
    rows_per = x_s.shape[1]
    half = x_s.shape[2]
    sizes = tuple(range(FFN_GRAIN, rows_per + 1, FFN_GRAIN))

    def rows_needed(g):
        return (tab_ref[ROW_VALID, g] + (FFN_GRAIN - 1)) // FFN_GRAIN * FFN_GRAIN

    def by_size(nrows, fn):
        for n in sizes:
            @pl.when(nrows == n)
            def _():
                fn(n)

    def row_copy(g, n):
        slot = lax.rem(g, FFN_LOOKAHEAD + 1)
        return pltpu.make_async_copy(
            x_hbm.at[pl.ds(g * rows_per, n)], x_s.at[slot, pl.ds(0, n)], sem_x.at[slot])

    def out_copy(g, n):
        slot = lax.rem(g, FFN_OUT_SLOTS)
        return pltpu.make_async_copy(
            y_s.at[slot, pl.ds(0, n)], y_hbm.at[pl.ds(g * rows_per, n)], sem_y.at[slot])

    def weight_copies(e, s):
        return (pltpu.make_async_copy(wg_hbm.at[e], wg_s.at[s], sem.at[s, 0]),
                pltpu.make_async_copy(wu_hbm.at[e], wu_s.at[s], sem.at[s, 1]),
                pltpu.make_async_copy(wd_hbm.at[e], wd_s.at[s], sem.at[s, 2]))

    @pl.when(i == 0)
    def _():
        for j in range(FFN_WEIGHT_SLOTS - 1):
            @pl.when(tab_ref[ROW_HEAD, 1 + j] >= 0)
            def _():
                for cp in weight_copies(tab_ref[ROW_HEAD, 1 + j], j):
                    cp.start()
        for g in range(FFN_LOOKAHEAD):
            @pl.when(g < nreal)
            def _():
                by_size(rows_needed(g), lambda n: row_copy(g, n).start())

    @pl.when(i < nreal)
    def _():
        s = tab_ref[ROW_SLOT, i]
        fetch = i + FFN_LOOKAHEAD

        @pl.when(fetch < nreal)
        def _():
            by_size(rows_needed(fetch), lambda n: row_copy(fetch, n).start())

        by_size(rows_needed(i), lambda n: row_copy(i, n).wait())

        @pl.when(tab_ref[ROW_FIRST, i] == 1)
        def _():
            for cp in weight_copies(tab_ref[ROW_EXP, i], s):
                cp.wait()

            ahead = tab_ref[ROW_AHEAD, i]

            @pl.when(ahead >= 0)
            def _():
                for cp in weight_copies(ahead, lax.rem(s + FFN_WEIGHT_SLOTS - 1, FFN_WEIGHT_SLOTS)):
                    cp.start()

        @pl.when(i >= FFN_OUT_SLOTS)
        def _():
            done = i - FFN_OUT_SLOTS
            by_size(rows_needed(done), lambda n: out_copy(done, n).wait())

        x_slot = lax.rem(i, FFN_LOOKAHEAD + 1)
        y_slot = lax.rem(i, FFN_OUT_SLOTS)
        valid = tab_ref[ROW_VALID, i]

        def expert_rows(n):
            r = lax.broadcasted_iota(I32, (n, 1), 0)
            x_lo, x_hi = _unpack_halves(jnp.where(r < valid, x_s[x_slot, pl.ds(0, n), :], 0))
            hg = _dot(x_lo, wg_s[s, :half, :]) + _dot(x_hi, wg_s[s, half:, :])
            hu = _dot(x_lo, wu_s[s, :half, :]) + _dot(x_hi, wu_s[s, half:, :])
            y_s[y_slot, pl.ds(0, n), :] = _pack_halves(_dot(_silu(hg) * hu, wd_s[s]))
            out_copy(i, n).start()

        by_size(rows_needed(i), expert_rows)

        @pl.when(i == nreal - 1)
        def _():
            for back in range(FFN_OUT_SLOTS):
                last = i - back

                @pl.when(last >= 0)
                def _():
                    by_size(rows_needed(last), lambda n: out_copy(last, n).wait())


def _ffn(tab, xs, w_gate, w_up, w_down):
    p, half = xs.shape
    d = 2 * half
    ff = w_gate.shape[2]
    return pl.pallas_call(
        _ffn_kernel,
        out_shape=jax.ShapeDtypeStruct((p, half), I32),
        grid_spec=pltpu.PrefetchScalarGridSpec(
            num_scalar_prefetch=1,
            grid=(tab.shape[1],),
            in_specs=[
                pl.BlockSpec(memory_space=pl.ANY),
                pl.BlockSpec(memory_space=pl.ANY),
                pl.BlockSpec(memory_space=pl.ANY),
                pl.BlockSpec(memory_space=pl.ANY),
            ],
            out_specs=pl.BlockSpec(memory_space=pl.ANY),
            scratch_shapes=[
                pltpu.VMEM((FFN_LOOKAHEAD + 1, FFN_BLOCK, half), I32),
                pltpu.VMEM((FFN_OUT_SLOTS, FFN_BLOCK, half), I32),
                pltpu.VMEM((FFN_WEIGHT_SLOTS, d, ff), F32),
                pltpu.VMEM((FFN_WEIGHT_SLOTS, d, ff), F32),
                pltpu.VMEM((FFN_WEIGHT_SLOTS, ff, d), F32),
                pltpu.SemaphoreType.DMA((FFN_LOOKAHEAD + 1,)),
                pltpu.SemaphoreType.DMA((FFN_OUT_SLOTS,)),
                pltpu.SemaphoreType.DMA((FFN_WEIGHT_SLOTS, 3)),
            ],
        ),
        compiler_params=pltpu.CompilerParams(
            dimension_semantics=("arbitrary",), vmem_limit_bytes=VMEM_LIMIT, has_side_effects=True),
        name="ffn",
    )(tab, xs, w_gate, w_up, w_down)


def _sc_gather(y, dest3):
    a, words = y.shape
    nchunks, _, chunk = dest3.shape
    n = nchunks * chunk
    per_worker = nchunks // (SC_CORES * SC_SUBCORES)
    nbuf = SC_GATHER_BUFS
    parts = chunk // SC_GATHER_ROWS
    items = [(c, k, h) for c in range(per_worker) for k in range(TOP_K) for h in range(parts)]
    mesh = plsc.VectorSubcoreMesh(core_axis_name="c", subcore_axis_name="s",
                                  num_cores=SC_CORES, num_subcores=SC_SUBCORES)

    @functools.partial(
        pl.kernel, mesh=mesh,
        out_type=jax.ShapeDtypeStruct((TOP_K, n, words), I32),
        scratch_types=[
            pltpu.VMEM((per_worker, TOP_K, chunk), I32),
            [pltpu.VMEM((SC_GATHER_ROWS, words), I32)] * nbuf,
            pltpu.SemaphoreType.DMA((nbuf,)),
            pltpu.SemaphoreType.DMA((nbuf,)),
        ],
        name="sc_gather",
    )
    def run(y_hbm, d_hbm, yt_hbm, idx_v, bufs, sem_g, sem_w):
        wid = lax.axis_index("s") * SC_CORES + lax.axis_index("c")
        pltpu.sync_copy(d_hbm.at[pl.ds(wid * per_worker, per_worker)], idx_v)

        def gather(m):
            c, k, h = items[m]
            idx = idx_v.at[c, k, pl.ds(h * SC_GATHER_ROWS, SC_GATHER_ROWS)]
            return pltpu.async_copy(y_hbm.at[idx], bufs[m % nbuf], sem_g.at[m % nbuf])

        def write(m):
            c, k, h = items[m]
            rows = pl.ds((wid * per_worker + c) * chunk + h * SC_GATHER_ROWS, SC_GATHER_ROWS)
            return pltpu.async_copy(bufs[m % nbuf], yt_hbm.at[k, rows], sem_w.at[m % nbuf])

        gathers = {m: gather(m) for m in range(min(nbuf - 1, len(items)))}
        writes = {}
        for m in range(len(items)):
            gathers.pop(m).wait()
            writes[m] = write(m)
            nxt = m + nbuf - 1
            if nxt < len(items):
                if m >= 1:
                    writes.pop(m - 1).wait()
                gathers[nxt] = gather(nxt)
        for m in sorted(writes):
            writes.pop(m).wait()

    return run(y, dest3)


def _combine_kernel(yt_ref, wt_ref, xb_ref, mod_ref, o_ref):
    half = yt_ref.shape[2]
    wt = wt_ref[...].T
    lo, hi = _unpack_halves(yt_ref[0])
    r_lo = lo * wt[:, 0:1]
    r_hi = hi * wt[:, 0:1]
    for k in range(1, TOP_K):
        lo, hi = _unpack_halves(yt_ref[k])
        r_lo = r_lo + lo * wt[:, k:k + 1]
        r_hi = r_hi + hi * wt[:, k:k + 1]
    gate = mod_ref[0][5:6]
    o_ref[:, :half] = xb_ref[:, :half] + gate[:, :half] * r_lo
    o_ref[:, half:] = xb_ref[:, half:] + gate[:, half:] * r_hi


def _combine(yt, w_k, xb, mod3, seq):
    n, d = xb.shape
    tm = TM_COMBINE
    tiles_per_seq = seq // tm
    return pl.pallas_call(
        _combine_kernel,
        out_shape=jax.ShapeDtypeStruct((n, d), F32),
        grid=(n // tm,),
        in_specs=[
            pl.BlockSpec((TOP_K, tm, d // 2), lambda i: (0, i, 0)),
            pl.BlockSpec((TOP_K, tm), lambda i: (0, i)),
            pl.BlockSpec((tm, d), lambda i: (i, 0)),
            pl.BlockSpec((1, N_MOD, d), lambda i: (i // tiles_per_seq, 0, 0)),
        ],
        out_specs=pl.BlockSpec((tm, d), lambda i: (i, 0)),
        compiler_params=pltpu.CompilerParams(vmem_limit_bytes=VMEM_LIMIT),
        name="combine",
    )(yt, w_k, xb, mod3)


def _rotary_tables(seq):
    half = RET_DK // 2
    inv = ROPE_BASE ** (-np.arange(half, dtype=np.float64) / half)
    ang = np.arange(seq, dtype=np.float64)[:, None] * inv[None, :]
    cos = np.cos(ang).astype(np.float32)
    sin = np.sin(ang).astype(np.float32)
    return (jnp.asarray(np.concatenate([cos, cos], axis=-1)),
            jnp.asarray(np.concatenate([-sin, sin], axis=-1)))


def kernel(x, c, w_ada, b_ada, g_mix, w_in, q_gain, k_gain, w_out, g_ffn, w_router, router_bias,
           w_gate, w_up, w_down, ws_gate, ws_up, ws_down):
    bsz, seq, d = x.shape
    n = bsz * seq
    depth = w_ada.shape[0]
    cos_full, sin_signed = _rotary_tables(seq)
    log_g = jnp.asarray(np.log1p(-np.exp2(-5.0 - np.arange(RET_HEADS, dtype=np.float64))).astype(np.float32))
    ret_w = RET_HEADS * RET_DK
    x2 = x.reshape(n, d)
    for l in range(depth):
        mod3 = _adaln(c, w_ada[l], b_ada[l]).reshape(bsz, N_MOD, d)
        proj = _inproj(x2, mod3, g_mix[l].reshape(1, d), w_in[l], cos_full, sin_signed, seq)
        proj3 = proj.reshape(bsz, seq, IN_COLS)
        qg2 = jnp.tile(q_gain[l].reshape(1, MOBA_DH), (1, 2))
        kg2 = jnp.tile(k_gain[l].reshape(1, MOBA_DH), (1, 2))
        ret, moba = _mixers(log_g, proj3, qg2, kg2)
        xb, h2, e_idx, w_k, rank, cnt = _mid(
            ret.reshape(n, ret_w), moba.reshape(n, MOBA_HEADS * MOBA_DH), x2, mod3,
            w_out[l], g_ffn[l].reshape(1, d),
            w_router[l].T.astype(BF16), router_bias[l].reshape(N_EXPERTS, 1),
            ws_gate[l], ws_up[l], ws_down[l], seq)
        nblk = n * TOP_K // FFN_BLOCK + N_EXPERTS
        tab = _sched(cnt, nblk)
        dest3 = _dest(tab, e_idx, rank)
        xs = _sc_dispatch(h2, dest3, nblk * FFN_BLOCK)
        y = _ffn(tab, xs, w_gate[l], w_up[l], w_down[l])
        yt = _sc_gather(y, dest3)
        x2 = _combine(yt, w_k, xb, mod3, seq)
    return x2.reshape(bsz, seq, d)
```

```python
import functools

import numpy as np
import jax
import jax.numpy as jnp
from jax import lax
from jax.experimental import pallas as pl
from jax.experimental.pallas import tpu as pltpu
from jax.experimental.pallas import tpu_sc as plsc

F32 = jnp.float32
BF16 = jnp.bfloat16
I32 = jnp.int32

RET_HEADS = 4
RET_DK = 128
MOBA_HEADS = 8
MOBA_DH = 64
MOBA_BLOCK = 256
MOBA_TOPK = 3
ROPE_BASE = 10000.0
N_EXPERTS = 256
TOP_K = 8
N_GROUPS = 8
TOPK_GROUPS = 4
GROUP_SIZE = N_EXPERTS // N_GROUPS
ROUTED_SCALE = 2.5
N_MOD = 6
EPS = 1e-6
IN_COLS = 3584

LANES = 128
BF16_TILE_ROWS = 16
RET_CHUNK = 256
TN_ADALN = 1024
TM_INPROJ = 512
TM_MID = 512
TN_DEST = 2048
TM_COMBINE = 512
MOBA_ONES_ROWS = BF16_TILE_ROWS
SC_CORES = 2
SC_SUBCORES = 16
SC_CHUNK = 128
SC_GATHER_ROWS = 64
SC_GATHER_BUFS = 3
FFN_BLOCK = 1024
FFN_GRAIN = 128
FFN_LOOKAHEAD = 5
FFN_WEIGHT_SLOTS = 3
FFN_OUT_SLOTS = 3
VMEM_LIMIT = 56 * 1024 * 1024

NEG_INF = float("-inf")
LOG2_E = 1.4426950408889634


def _silu(x):
    return x * jax.nn.sigmoid(x)


def _nt_dot(a, b):
    return lax.dot_general(a, b, (((1,), (1,)), ((), ())), preferred_element_type=F32)


def _tn_dot(a, b):
    return lax.dot_general(a, b, (((0,), (0,)), ((), ())), preferred_element_type=F32)


def _dot(a, b):
    return jnp.dot(a, b, preferred_element_type=F32)


HI_MASK = -65536


def _pack_halves(v):
    w = v.shape[1] // 2
    lo = lax.bitcast_convert_type(v[:, :w].astype(BF16).astype(F32), I32)
    hi = lax.bitcast_convert_type(v[:, w:].astype(BF16).astype(F32), I32)
    return lax.shift_right_logical(lo, 16) | (hi & HI_MASK)


def _unpack_halves(u):
    lo = lax.bitcast_convert_type(lax.shift_left(u, 16), F32)
    hi = lax.bitcast_convert_type(u & HI_MASK, F32)
    return lo, hi


def _adaln_kernel(c_ref, w_ref, b_ref, o_ref):
    s = _silu(c_ref[...])
    o_ref[...] = _dot(s.astype(BF16), w_ref[...].astype(BF16)) + b_ref[...]


def _adaln(c, w_ada, b_ada):
    bsz, d = c.shape
    ncol = w_ada.shape[1]
    tn = TN_ADALN
    return pl.pallas_call(
        _adaln_kernel,
        out_shape=jax.ShapeDtypeStruct((bsz, ncol), F32),
        grid=(ncol // tn,),
        in_specs=[
            pl.BlockSpec((bsz, d), lambda j: (0, 0)),
            pl.BlockSpec((d, tn), lambda j: (0, j)),
            pl.BlockSpec((1, tn), lambda j: (0, j)),
        ],
        out_specs=pl.BlockSpec((bsz, tn), lambda j: (0, j)),
        compiler_params=pltpu.CompilerParams(vmem_limit_bytes=VMEM_LIMIT),
        name="adaln",
    )(c, w_ada, b_ada.reshape(1, ncol))


def _inproj_kernel(x_ref, mod_ref, g_ref, w_ref, cos_ref, sin_ref, o_ref):
    x = x_ref[...]
    ms = jnp.mean(x * x, axis=-1, keepdims=True)
    m = mod_ref[0]
    h = (x * lax.rsqrt(ms + EPS) * g_ref[...]) * (1.0 + m[1:2]) + m[0:1]
    hb = h.astype(BF16)
    cosf = cos_ref[...]
    sinf = sin_ref[...]
    k_scale = RET_DK ** -0.5
    width = RET_HEADS * RET_DK
    for ci in range(IN_COLS // width):
        acc = _dot(hb, w_ref[:, ci * width:(ci + 1) * width])
        if ci < 2:
            for hh in range(RET_HEADS):
                xh = acc[:, hh * RET_DK:(hh + 1) * RET_DK]
                r = xh * cosf + pltpu.roll(xh, RET_DK // 2, axis=1) * sinf
                if ci == 1:
                    r = r * k_scale
                o_ref[:, ci * width + hh * RET_DK:ci * width + (hh + 1) * RET_DK] = r.astype(BF16)
        else:
            o_ref[:, ci * width:(ci + 1) * width] = acc.astype(BF16)


def _inproj(x2, mod3, g_mix, w_in, cos_full, sin_signed, seq):
    n, d = x2.shape
    tm = TM_INPROJ
    tiles_per_seq = seq // tm
    return pl.pallas_call(
        _inproj_kernel,
        out_shape=jax.ShapeDtypeStruct((n, IN_COLS), BF16),
        grid=(n // tm,),
        in_specs=[
            pl.BlockSpec((tm, d), lambda i: (i, 0)),
            pl.BlockSpec((1, N_MOD, d), lambda i: (i // tiles_per_seq, 0, 0)),
            pl.BlockSpec((1, d), lambda i: (0, 0)),
            pl.BlockSpec((d, IN_COLS), lambda i: (0, 0), pipeline_mode=pl.Buffered(1)),
            pl.BlockSpec((tm, LANES), lambda i: (i % tiles_per_seq, 0)),
            pl.BlockSpec((tm, LANES), lambda i: (i % tiles_per_seq, 0)),
        ],
        out_specs=pl.BlockSpec((tm, IN_COLS), lambda i: (i, 0)),
        compiler_params=pltpu.CompilerParams(vmem_limit_bytes=VMEM_LIMIT),
        name="inproj",
    )(x2, mod3, g_mix, w_in, cos_full, sin_signed)


def _ret_kernel(lg_ref, q_ref, k_ref, v_ref, g_ref, o_ref):
    seq = q_ref.shape[1]
    c = RET_CHUNK
    lg = lg_ref[pl.program_id(1)]
    row = lax.broadcasted_iota(I32, (c, c), 0)
    col = lax.broadcasted_iota(I32, (c, c), 1)
    diff = (row - col).astype(F32)
    dmask = jnp.where(diff >= 0, jnp.exp(lg * jnp.maximum(diff, 0.0)), 0.0)
    idx = lax.broadcasted_iota(I32, (c, 1), 0).astype(F32)
    q_decay = jnp.exp(lg * (idx + 1.0))
    k_decay = jnp.exp(lg * (c - 1.0 - idx))
    chunk_decay = jnp.exp(jnp.full((1, 1), lg * c, F32))
    state = jnp.zeros((RET_DK, RET_DK), F32)
    for n in range(seq // c):
        rows = slice(n * c, (n + 1) * c)
        qn = q_ref[0, rows, :]
        kn = k_ref[0, rows, :]
        vn = v_ref[0, rows, :]
        scores = _nt_dot(qn, kn) * dmask
        inner = _dot(scores.astype(BF16), vn)
        qs = (qn.astype(F32) * q_decay).astype(BF16)
        cross = _dot(qs, state.astype(BF16))
        o = inner + cross
        o = o * lax.rsqrt(jnp.mean(o * o, axis=-1, keepdims=True) + EPS)
        gn = g_ref[0, rows, :].astype(F32)
        o_ref[0, rows, :] = (_silu(gn) * o).astype(BF16)
        ks = (kn.astype(F32) * k_decay).astype(BF16)
        state = state * chunk_decay + _tn_dot(ks, vn)


def _moba_kernel(q_ref, k_ref, v_ref, qg_ref, kg_ref, o_ref, qt_s, ka_s, kb_s, vta_s, vtb_s):
    seq = q_ref.shape[1]
    lb = MOBA_BLOCK
    nb = seq // lb
    lane = lax.broadcasted_iota(I32, (1, LANES), 1)
    is_a = lane < MOBA_DH

    def head_norm(xf, gain):
        sq = xf * xf
        s_a = jnp.sum(jnp.where(is_a, sq, 0.0), axis=-1, keepdims=True)
        s_b = jnp.sum(jnp.where(is_a, 0.0, sq), axis=-1, keepdims=True)
        inv = jnp.where(is_a, lax.rsqrt(s_a / MOBA_DH + EPS), lax.rsqrt(s_b / MOBA_DH + EPS))
        return xf * inv * gain

    qg = qg_ref[...]
    kg = kg_ref[...]
    k_means = []
    for j in range(nb):
        rows = slice(j * lb, (j + 1) * lb)
        kf = head_norm(k_ref[0, rows, :].astype(F32), kg)
        ka_s[rows, :] = jnp.where(is_a, kf, 0.0).astype(BF16)
        kb_s[rows, :] = jnp.where(is_a, 0.0, kf).astype(BF16)
        k_means.append(jnp.mean(kf, axis=0, keepdims=True))
        qf = head_norm(q_ref[0, rows, :].astype(F32), qg)
        qt_s[:, rows] = (qf * (MOBA_DH ** -0.5 * LOG2_E)).T.astype(BF16)
        vt = v_ref[0, rows, :].T
        ones = jnp.ones((MOBA_ONES_ROWS, lb), BF16)
        vta_s[:, rows] = jnp.concatenate([vt[:MOBA_DH], ones], axis=0)
        vtb_s[:, rows] = jnp.concatenate([vt[MOBA_DH:], ones], axis=0)
    k_mean = jnp.concatenate(k_means + [jnp.zeros((BF16_TILE_ROWS - nb, LANES), F32)], axis=0)
    k_mean_h = (jnp.where(is_a, k_mean, 0.0).astype(BF16), jnp.where(is_a, 0.0, k_mean).astype(BF16))
    k_s = (ka_s, kb_s)
    vt_s = (vta_s, vtb_s)

    r_loc = lax.broadcasted_iota(I32, (lb, lb), 0)
    c_loc = lax.broadcasted_iota(I32, (lb, lb), 1)
    causal = r_loc <= c_loc

    def logits(i, hx):
        qt = qt_s[:, i * lb:(i + 1) * lb]
        bias = [None] * i
        if i > MOBA_TOPK:
            gate = _dot(k_mean_h[hx], qt)
            g = [gate[j:j + 1, :] for j in range(i)]
            for j in range(i):
                rank = jnp.zeros((1, lb), F32)
                for j2 in range(i):
                    if j2 == j:
                        continue
                    beats = (g[j2] >= g[j]) if j2 < j else (g[j2] > g[j])
                    rank = rank + jnp.where(beats, 1.0, 0.0)
                bias[j] = jnp.where(rank < float(MOBA_TOPK), 0.0, NEG_INF).astype(BF16)
        pieces = []
        for j in range(i + 1):
            s = _dot(k_s[hx][j * lb:(j + 1) * lb, :], qt).astype(BF16)
            if j == i:
                s = jnp.where(causal, s, jnp.asarray(NEG_INF, BF16))
            elif bias[j] is not None:
                s = s + bias[j]
            pieces.append(s)
        mx = jnp.max(pieces[0], axis=0, keepdims=True)
        for s in pieces[1:]:
            mx = jnp.maximum(mx, jnp.max(s, axis=0, keepdims=True))
        return pieces, mx

    def attend(hx, pieces, mx):
        acc = jnp.zeros((MOBA_DH + MOBA_ONES_ROWS, lb), F32)
        for j, s in enumerate(pieces):
            acc = acc + _dot(vt_s[hx][:, j * lb:(j + 1) * lb], jnp.exp2(s - mx))
        return acc[:MOBA_DH] / acc[MOBA_DH:MOBA_DH + 1, :]

    groups = [(i, hx) for i in range(nb) for hx in range(2)]
    outs = {}
    pending = logits(*groups[0])
    for n, (i, hx) in enumerate(groups):
        ahead = logits(*groups[n + 1]) if n + 1 < len(groups) else None
        outs[(i, hx)] = attend(hx, *pending)
        pending = ahead
        if hx == 1:
            o_ref[0, i * lb:(i + 1) * lb, :] = jnp.concatenate(
                [outs.pop((i, 0)), outs.pop((i, 1))], axis=0).T.astype(BF16)


def _mixers_kernel(lg_ref, rq_ref, rk_ref, rv_ref, rg_ref, mq_ref, mk_ref, mv_ref, qg_ref, kg_ref,
                   ret_ref, moba_ref, *moba_scratch):
    _ret_kernel(lg_ref, rq_ref, rk_ref, rv_ref, rg_ref, ret_ref)
    _moba_kernel(mq_ref, mk_ref, mv_ref, qg_ref, kg_ref, moba_ref, *moba_scratch)


def _mixers(log_g, proj3, qg2, kg2):
    bsz, seq, _ = proj3.shape
    pairs = MOBA_HEADS // 2
    assert pairs == RET_HEADS and RET_DK == LANES
    blk = (1, seq, LANES)
    base = 4 * RET_HEADS

    def col(first):
        return pl.BlockSpec(blk, lambda b, p, lg: (b, 0, first + p))

    return pl.pallas_call(
        _mixers_kernel,
        out_shape=(jax.ShapeDtypeStruct((bsz, seq, RET_HEADS * RET_DK), BF16),
                   jax.ShapeDtypeStruct((bsz, seq, MOBA_HEADS * MOBA_DH), BF16)),
        grid_spec=pltpu.PrefetchScalarGridSpec(
            num_scalar_prefetch=1,
            grid=(bsz, pairs),
            in_specs=[
                col(0), col(RET_HEADS), col(2 * RET_HEADS), col(3 * RET_HEADS),
                col(base), col(base + pairs), col(base + 2 * pairs),
                pl.BlockSpec((1, LANES), lambda b, p, lg: (0, 0)),
                pl.BlockSpec((1, LANES), lambda b, p, lg: (0, 0)),
            ],
            out_specs=(col(0), col(0)),
            scratch_shapes=[
                pltpu.VMEM((LANES, seq), BF16),
                pltpu.VMEM((seq, LANES), BF16),
                pltpu.VMEM((seq, LANES), BF16),
                pltpu.VMEM((MOBA_DH + MOBA_ONES_ROWS, seq), BF16),
                pltpu.VMEM((MOBA_DH + MOBA_ONES_ROWS, seq), BF16),
            ],
        ),
        compiler_params=pltpu.CompilerParams(vmem_limit_bytes=VMEM_LIMIT),
        name="mixers",
    )(log_g, *([proj3] * 7), qg2, kg2)


def _mid_kernel(ret_ref, moba_ref, x_ref, mod_ref, wo_ref, g_ref, wr_ref, rb_ref,
                wsg_ref, wsu_ref, wsd_ref,
                xb_ref, h2_ref, e_ref, w_ref, rk_ref, cnt_ref,
                carry_s, wo_s, wr_s, wsg_s, wsu_s, wsd_s, upper_s):
    i = pl.program_id(0)
    tm = x_ref.shape[0]
    half = ret_ref.shape[1]

    @pl.when(i == 0)
    def _():
        carry_s[...] = jnp.zeros_like(carry_s)
        wo_s[...] = wo_ref[...].astype(BF16)
        wsg_s[...] = wsg_ref[...].astype(BF16)
        wsu_s[...] = wsu_ref[...].astype(BF16)
        wsd_s[...] = wsd_ref[...].astype(BF16)
        wr_s[...] = wr_ref[...].T.astype(BF16)
        tr = lax.broadcasted_iota(I32, (tm, tm), 0)
        tc = lax.broadcasted_iota(I32, (tm, tm), 1)
        upper_s[...] = jnp.where(tr < tc, 1.0, 0.0).astype(BF16)

    m = mod_ref[0]
    mixed = _dot(ret_ref[...], wo_s[:half, :]) + _dot(moba_ref[...], wo_s[half:, :])
    x1 = x_ref[...] + m[2:3] * mixed
    ms = jnp.mean(x1 * x1, axis=-1, keepdims=True)
    h2 = (x1 * lax.rsqrt(ms + EPS) * g_ref[...]) * (1.0 + m[4:5]) + m[3:4]
    h2_ref[...] = _pack_halves(h2)
    h2b = h2.astype(BF16)

    hid = _silu(_dot(h2b, wsg_s[...])) * _dot(h2b, wsu_s[...])
    xb_ref[...] = x1 + m[5:6] * _dot(hid.astype(BF16), wsd_s[...])

    scores = jax.nn.sigmoid(_nt_dot(wr_s[...], h2b))
    biased = scores + rb_ref[...]
    grp = biased.reshape(N_GROUPS, GROUP_SIZE, tm)
    gi = lax.broadcasted_iota(I32, (N_GROUPS, GROUP_SIZE, tm), 1).astype(F32)
    top1 = jnp.max(grp, axis=1, keepdims=True)
    first = jnp.min(jnp.where(grp == top1, gi, float(GROUP_SIZE)), axis=1, keepdims=True)
    top2 = jnp.max(jnp.where(gi == first, NEG_INF, grp), axis=1, keepdims=True)
    gscore = (top1 + top2).reshape(N_GROUPS, tm)
    gidx = lax.broadcasted_iota(I32, (N_GROUPS, tm), 0)
    grank = jnp.zeros((N_GROUPS, tm), F32)
    for g2 in range(N_GROUPS):
        rowv = gscore[g2:g2 + 1, :]
        beats = (rowv > gscore) | ((rowv == gscore) & (g2 < gidx))
        grank = grank + jnp.where(beats, 1.0, 0.0)
    gsel = jnp.where(grank < float(TOPK_GROUPS), 1.0, 0.0)
    emask = jnp.broadcast_to(gsel.reshape(N_GROUPS, 1, tm), (N_GROUPS, GROUP_SIZE, tm)).reshape(N_EXPERTS, tm)
    choice = jnp.where(emask > 0.5, biased, NEG_INF)

    eidx = lax.broadcasted_iota(I32, (N_EXPERTS, tm), 0).astype(F32)
    remaining = choice
    e_rows = []
    for _k in range(TOP_K):
        mx = jnp.max(remaining, axis=0, keepdims=True)
        idx = jnp.min(jnp.where(remaining == mx, eidx, float(N_EXPERTS)), axis=0, keepdims=True)
        e_rows.append(idx)
        remaining = jnp.where(eidx == idx, NEG_INF, remaining)
    selmask = jnp.where(remaining != choice, 1.0, 0.0)

    prefix = _dot(selmask.astype(BF16), upper_s[...]) + carry_s[...]
    w_rows = []
    r_rows = []
    for ek in e_rows:
        onehot = eidx == ek
        w_rows.append(jnp.sum(jnp.where(onehot, scores, 0.0), axis=0, keepdims=True))
        r_rows.append(jnp.sum(jnp.where(onehot, prefix, 0.0), axis=0, keepdims=True))
    wsum = w_rows[0]
    for wk in w_rows[1:]:
        wsum = wsum + wk
    carry_s[...] = carry_s[...] + jnp.sum(selmask, axis=1, keepdims=True)

    e_ref[...] = jnp.concatenate(e_rows, axis=0).astype(I32)
    w_ref[...] = jnp.concatenate([wk / wsum * ROUTED_SCALE for wk in w_rows], axis=0)
    rk_ref[...] = jnp.concatenate(r_rows, axis=0).astype(I32)
    cnt_ref[...] = carry_s[...].astype(I32)


def _mid(ret2, moba2, x2, mod3, wo, g_ffn, w_router, rbias, wsg, wsu, wsd, seq):
    n, d = x2.shape
    tm = TM_MID
    tiles_per_seq = seq // tm
    half = ret2.shape[1]
    ff = wsg.shape[1]
    const = lambda i: (0, 0)
    row = lambda i: (i, 0)
    colt = lambda i: (0, i)

    def resident(shape):
        return pl.BlockSpec(shape, const, pipeline_mode=pl.Buffered(1))

    return pl.pallas_call(
        _mid_kernel,
        out_shape=(
            jax.ShapeDtypeStruct((n, d), F32),
            jax.ShapeDtypeStruct((n, d // 2), I32),
            jax.ShapeDtypeStruct((TOP_K, n), I32),
            jax.ShapeDtypeStruct((TOP_K, n), F32),
            jax.ShapeDtypeStruct((TOP_K, n), I32),
            jax.ShapeDtypeStruct((N_EXPERTS, 1), I32),
        ),
        grid=(n // tm,),
        in_specs=[
            pl.BlockSpec((tm, half), row),
            pl.BlockSpec((tm, half), row),
            pl.BlockSpec((tm, d), row),
            pl.BlockSpec((1, N_MOD, d), lambda i: (i // tiles_per_seq, 0, 0)),
            resident(wo.shape),
            pl.BlockSpec((1, d), const),
            resident((d, N_EXPERTS)),
            pl.BlockSpec((N_EXPERTS, 1), const),
            resident((d, ff)),
            resident((d, ff)),
            resident((ff, d)),
        ],
        out_specs=(
            pl.BlockSpec((tm, d), row),
            pl.BlockSpec((tm, d // 2), row),
            pl.BlockSpec((TOP_K, tm), colt),
            pl.BlockSpec((TOP_K, tm), colt),
            pl.BlockSpec((TOP_K, tm), colt),
            pl.BlockSpec((N_EXPERTS, 1), const),
        ),
        scratch_shapes=[
            pltpu.VMEM((N_EXPERTS, 1), F32),
            pltpu.VMEM(wo.shape, BF16),
            pltpu.VMEM((N_EXPERTS, d), BF16),
            pltpu.VMEM((d, ff), BF16),
            pltpu.VMEM((d, ff), BF16),
            pltpu.VMEM((ff, d), BF16),
            pltpu.VMEM((tm, tm), BF16),
        ],
        compiler_params=pltpu.CompilerParams(
            dimension_semantics=("arbitrary",), vmem_limit_bytes=VMEM_LIMIT),
        name="mid",
    )(ret2, moba2, x2, mod3, wo, g_ffn, w_router, rbias, wsg, wsu, wsd)


ROW_EXP = 0
ROW_VALID = 1
ROW_FIRST = 2
ROW_AHEAD = 3
ROW_SLOT = 4
ROW_HEAD = 5
ROW_START = 6
SCHED_ROWS = 8


def _sched_kernel(cnt_ref, tab_ref):
    ne = N_EXPERTS
    nblk = tab_ref.shape[1]
    shift = FFN_BLOCK.bit_length() - 1
    e_sub = lax.broadcasted_iota(I32, (ne, ne), 0)
    e_lane = lax.broadcasted_iota(I32, (ne, ne), 1)
    e_col = lax.broadcasted_iota(I32, (ne, 1), 0).astype(F32)
    ids_row = lax.broadcasted_iota(I32, (1, ne), 1).astype(F32) + 1.0

    def to_row(col):
        return jnp.sum(jnp.where(e_sub == e_lane, col, 0.0), axis=0, keepdims=True)

    def running_col(row):
        return jnp.sum(jnp.where(e_lane <= e_sub, row, 0.0), axis=1, keepdims=True)

    def running_row(col):
        return jnp.sum(jnp.where(e_sub <= e_lane, col, 0.0), axis=0, keepdims=True)

    cnt_i = cnt_ref[...]
    cnt_col = cnt_i.astype(F32)
    pad_col = lax.shift_left(lax.shift_right_logical(cnt_i + (FFN_BLOCK - 1), shift), shift).astype(F32)
    pad_row = to_row(pad_col)
    ends_col = running_col(pad_row)
    start_col = ends_col - pad_col
    start_row = running_row(pad_col) - pad_row
    nreal = jnp.sum(pad_row, axis=1, keepdims=True) * (1.0 / FFN_BLOCK)

    step = lax.broadcasted_iota(I32, (1, nblk), 1)
    g = step.astype(F32)
    row0 = jnp.minimum(g, nreal - 1.0) * float(FFN_BLOCK)
    exp_g = jnp.sum(jnp.where(ends_col <= row0, 1.0, 0.0), axis=0, keepdims=True)
    mine = e_col == exp_g

    def per_block(col):
        return jnp.sum(jnp.where(mine, col, 0.0), axis=0, keepdims=True)

    valid = jnp.clip(per_block(cnt_col + start_col) - row0, 0.0, float(FFN_BLOCK))
    first = jnp.where(jnp.logical_and(g < nreal, per_block(start_col) == row0), 1.0, 0.0)

    used_col = jnp.where(cnt_col > 0.0, 1.0, 0.0)
    used_row = to_row(used_col)
    ord_col = running_col(used_row) - 1.0
    ord_row = running_row(used_col) - 1.0

    def used_at(pos):
        hit = jnp.logical_and(used_row > 0.0, ord_row == pos)
        return jnp.sum(jnp.where(hit, ids_row, 0.0), axis=1, keepdims=True) - 1.0

    ahead_col = used_at(ord_col + float(FFN_WEIGHT_SLOTS - 1))
    slot_col = ord_col - FFN_WEIGHT_SLOTS * jnp.floor((ord_col + 0.5) * (1.0 / FFN_WEIGHT_SLOTS))
    head = jnp.where(step == 0, nreal, 0.0)
    for j in range(FFN_WEIGHT_SLOTS - 1):
        head = head + jnp.where(step == 1 + j, used_at(jnp.full((1, 1), float(j), F32)), 0.0)
    rows = [exp_g, valid, first, per_block(ahead_col), per_block(slot_col), head,
            jnp.concatenate([start_row, jnp.zeros((1, nblk - ne), F32)], axis=1),
            jnp.zeros((1, nblk), F32)]
    tab_ref[...] = jnp.concatenate(rows, axis=0).astype(I32)


def _sched(cnt, nblk):
    assert FFN_BLOCK & (FFN_BLOCK - 1) == 0 and nblk >= N_EXPERTS
    return pl.pallas_call(
        _sched_kernel,
        out_shape=jax.ShapeDtypeStruct((SCHED_ROWS, nblk), I32),
        compiler_params=pltpu.CompilerParams(vmem_limit_bytes=VMEM_LIMIT),
        name="sched",
    )(cnt)


def _dest_kernel(tab_ref, e_ref, rk_ref, o_ref):
    k = e_ref.shape[0]
    segments = [jnp.broadcast_to(tab_ref[ROW_START:ROW_START + 1, s * LANES:(s + 1) * LANES], (k, LANES))
                for s in range(N_EXPERTS // LANES)]
    for ch in range(o_ref.shape[0]):
        cols = slice(ch * SC_CHUNK, (ch + 1) * SC_CHUNK)
        e = e_ref[:, cols]
        lane = e & (LANES - 1)
        start = jnp.take_along_axis(segments[0], lane, axis=1)
        for s in range(1, len(segments)):
            start = jnp.where(e >= s * LANES, jnp.take_along_axis(segments[s], lane, axis=1), start)
        o_ref[ch] = start + rk_ref[:, cols]


def _dest(tab, e_idx, rank):
    k, n = e_idx.shape
    tn = TN_DEST
    assert SC_CHUNK == LANES and N_EXPERTS % LANES == 0
    return pl.pallas_call(
        _dest_kernel,
        out_shape=jax.ShapeDtypeStruct((n // SC_CHUNK, k, SC_CHUNK), I32),
        grid=(n // tn,),
        in_specs=[
            pl.BlockSpec(tab.shape, lambda i: (0, 0)),
            pl.BlockSpec((k, tn), lambda i: (0, i)),
            pl.BlockSpec((k, tn), lambda i: (0, i)),
        ],
        out_specs=pl.BlockSpec((tn // SC_CHUNK, k, SC_CHUNK), lambda i: (i, 0, 0)),
        compiler_params=pltpu.CompilerParams(vmem_limit_bytes=VMEM_LIMIT),
        name="dest",
    )(tab, e_idx, rank)


def _sc_dispatch(h2p, dest3, total_rows):
    n, words = h2p.shape
    nchunks = n // SC_CHUNK
    per_worker = nchunks // (SC_CORES * SC_SUBCORES)
    mesh = plsc.VectorSubcoreMesh(core_axis_name="c", subcore_axis_name="s",
                                  num_cores=SC_CORES, num_subcores=SC_SUBCORES)

    @functools.partial(
        pl.kernel, mesh=mesh,
        out_type=jax.ShapeDtypeStruct((total_rows, words), I32),
        scratch_types=[
            pltpu.VMEM((TOP_K, SC_CHUNK), I32),
            pltpu.VMEM((SC_CHUNK, words), I32),
            pltpu.SemaphoreType.DMA,
        ],
        name="sc_dispatch",
    )
    def run(h_hbm, d_hbm, xs_hbm, idx_v, rows_v, sem):
        wid = lax.axis_index("s") * SC_CORES + lax.axis_index("c")

        @pl.loop(0, per_worker)
        def _(j):
            ch = wid * per_worker + j
            pltpu.sync_copy(d_hbm.at[ch], idx_v)
            pltpu.sync_copy(h_hbm.at[pl.ds(ch * SC_CHUNK, SC_CHUNK)], rows_v)
            copies = [pltpu.async_copy(rows_v, xs_hbm.at[idx_v.at[k]], sem) for k in range(TOP_K)]
            for cp in copies:
                cp.wait()

    return run(h2p, dest3)


def _ffn_kernel(tab_ref, x_hbm, wg_hbm, wu_hbm, wd_hbm, y_hbm,
                x_s, y_s, wg_s, wu_s, wd_s, sem_x, sem_y, sem):
    i = pl.program_id(0)
    nreal = tab_ref[ROW_HEAD, 0]
    rows_per = x_s.shape[1]
    half = x_s.shape[2]
    sizes = tuple(range(FFN_GRAIN, rows_per + 1, FFN_GRAIN))

    def rows_needed(g):
        return (tab_ref[ROW_VALID, g] + (FFN_GRAIN - 1)) // FFN_GRAIN * FFN_GRAIN

    def by_size(nrows, fn):
        for n in sizes:
            @pl.when(nrows == n)
            def _():
                fn(n)

    def row_copy(g, n):
        slot = lax.rem(g, FFN_LOOKAHEAD + 1)
        return pltpu.make_async_copy(
            x_hbm.at[pl.ds(g * rows_per, n)], x_s.at[slot, pl.ds(0, n)], sem_x.at[slot])

    def out_copy(g, n):
        slot = lax.rem(g, FFN_OUT_SLOTS)
        return pltpu.make_async_copy(
            y_s.at[slot, pl.ds(0, n)], y_hbm.at[pl.ds(g * rows_per, n)], sem_y.at[slot])

    def weight_copies(e, s):
        return (pltpu.make_async_copy(wg_hbm.at[e], wg_s.at[s], sem.at[s, 0]),
                pltpu.make_async_copy(wu_hbm.at[e], wu_s.at[s], sem.at[s, 1]),
                pltpu.make_async_copy(wd_hbm.at[e], wd_s.at[s], sem.at[s, 2]))

    @pl.when(i == 0)
    def _():
        for j in range(FFN_WEIGHT_SLOTS - 1):
            @pl.when(tab_ref[ROW_HEAD, 1 + j] >= 0)
            def _():
                for cp in weight_copies(tab_ref[ROW_HEAD, 1 + j], j):
                    cp.start()
        for g in range(FFN_LOOKAHEAD):
            @pl.when(g < nreal)
            def _():
                by_size(rows_needed(g), lambda n: row_copy(g, n).start())

    @pl.when(i < nreal)
    def _():
        s = tab_ref[ROW_SLOT, i]
        fetch = i + FFN_LOOKAHEAD

        @pl.when(fetch < nreal)
        def _():
            by_size(rows_needed(fetch), lambda n: row_copy(fetch, n).start())

        by_size(rows_needed(i), lambda n: row_copy(i, n).wait())

        @pl.when(tab_ref[ROW_FIRST, i] == 1)
        def _():
            for cp in weight_copies(tab_ref[ROW_EXP, i], s):
                cp.wait()

            ahead = tab_ref[ROW_AHEAD, i]

            @pl.when(ahead >= 0)
            def _():
                for cp in weight_copies(ahead, lax.rem(s + FFN_WEIGHT_SLOTS - 1, FFN_WEIGHT_SLOTS)):
                    cp.start()

        @pl.when(i >= FFN_OUT_SLOTS)
        def _():
            done = i - FFN_OUT_SLOTS
            by_size(rows_needed(done), lambda n: out_copy(done, n).wait())

        x_slot = lax.rem(i, FFN_LOOKAHEAD + 1)
        y_slot = lax.rem(i, FFN_OUT_SLOTS)
        valid = tab_ref[ROW_VALID, i]

        def expert_rows(n):
            r = lax.broadcasted_iota(I32, (n, 1), 0)
            x_lo, x_hi = _unpack_halves(jnp.where(r < valid, x_s[x_slot, pl.ds(0, n), :], 0))
            hg = _dot(x_lo, wg_s[s, :half, :]) + _dot(x_hi, wg_s[s, half:, :])
            hu = _dot(x_lo, wu_s[s, :half, :]) + _dot(x_hi, wu_s[s, half:, :])
            y_s[y_slot, pl.ds(0, n), :] = _pack_halves(_dot(_silu(hg) * hu, wd_s[s]))
            out_copy(i, n).start()

        by_size(rows_needed(i), expert_rows)

        @pl.when(i == nreal - 1)
        def _():
            for back in range(FFN_OUT_SLOTS):
                last = i - back

                @pl.when(last >= 0)
                def _():
                    by_size(rows_needed(last), lambda n: out_copy(last, n).wait())


def _ffn(tab, xs, w_gate, w_up, w_down):
    p, half = xs.shape
    d = 2 * half
    ff = w_gate.shape[2]
    return pl.pallas_call(
        _ffn_kernel,
        out_shape=jax.ShapeDtypeStruct((p, half), I32),
        grid_spec=pltpu.PrefetchScalarGridSpec(
            num_scalar_prefetch=1,
            grid=(tab.shape[1],),
            in_specs=[
                pl.BlockSpec(memory_space=pl.ANY),
                pl.BlockSpec(memory_space=pl.ANY),
                pl.BlockSpec(memory_space=pl.ANY),
                pl.BlockSpec(memory_space=pl.ANY),
            ],
            out_specs=pl.BlockSpec(memory_space=pl.ANY),
            scratch_shapes=[
                pltpu.VMEM((FFN_LOOKAHEAD + 1, FFN_BLOCK, half), I32),
                pltpu.VMEM((FFN_OUT_SLOTS, FFN_BLOCK, half), I32),
                pltpu.VMEM((FFN_WEIGHT_SLOTS, d, ff), F32),
                pltpu.VMEM((FFN_WEIGHT_SLOTS, d, ff), F32),
                pltpu.VMEM((FFN_WEIGHT_SLOTS, ff, d), F32),
                pltpu.SemaphoreType.DMA((FFN_LOOKAHEAD + 1,)),
                pltpu.SemaphoreType.DMA((FFN_OUT_SLOTS,)),
                pltpu.SemaphoreType.DMA((FFN_WEIGHT_SLOTS, 3)),
            ],
        ),
        compiler_params=pltpu.CompilerParams(
            dimension_semantics=("arbitrary",), vmem_limit_bytes=VMEM_LIMIT, has_side_effects=True),
        name="ffn",
    )(tab, xs, w_gate, w_up, w_down)


def _sc_gather(y, dest3):
    a, words = y.shape
    nchunks, _, chunk = dest3.shape
    n = nchunks * chunk
    per_worker = nchunks // (SC_CORES * SC_SUBCORES)
    nbuf = SC_GATHER_BUFS
    parts = chunk // SC_GATHER_ROWS
    items = [(c, k, h) for c in range(per_worker) for k in range(TOP_K) for h in range(parts)]
    mesh = plsc.VectorSubcoreMesh(core_axis_name="c", subcore_axis_name="s",
                                  num_cores=SC_CORES, num_subcores=SC_SUBCORES)

    @functools.partial(
        pl.kernel, mesh=mesh,
        out_type=jax.ShapeDtypeStruct((TOP_K, n, words), I32),
        scratch_types=[
            pltpu.VMEM((per_worker, TOP_K, chunk), I32),
            [pltpu.VMEM((SC_GATHER_ROWS, words), I32)] * nbuf,
            pltpu.SemaphoreType.DMA((nbuf,)),
            pltpu.SemaphoreType.DMA((nbuf,)),
        ],
        name="sc_gather",
    )
    def run(y_hbm, d_hbm, yt_hbm, idx_v, bufs, sem_g, sem_w):
        wid = lax.axis_index("s") * SC_CORES + lax.axis_index("c")
        pltpu.sync_copy(d_hbm.at[pl.ds(wid * per_worker, per_worker)], idx_v)

        def gather(m):
            c, k, h = items[m]
            idx = idx_v.at[c, k, pl.ds(h * SC_GATHER_ROWS, SC_GATHER_ROWS)]
            return pltpu.async_copy(y_hbm.at[idx], bufs[m % nbuf], sem_g.at[m % nbuf])

        def write(m):
            c, k, h = items[m]
            rows = pl.ds((wid * per_worker + c) * chunk + h * SC_GATHER_ROWS, SC_GATHER_ROWS)
            return pltpu.async_copy(bufs[m % nbuf], yt_hbm.at[k, rows], sem_w.at[m % nbuf])

        gathers = {m: gather(m) for m in range(min(nbuf - 1, len(items)))}
        writes = {}
        for m in range(len(items)):
            gathers.pop(m).wait()
            writes[m] = write(m)
            nxt = m + nbuf - 1
            if nxt < len(items):
                if m >= 1:
                    writes.pop(m - 1).wait()
                gathers[nxt] = gather(nxt)
        for m in sorted(writes):
            writes.pop(m).wait()

    return run(y, dest3)


def _combine_kernel(yt_ref, wt_ref, xb_ref, mod_ref, o_ref):
    half = yt_ref.shape[2]
    wt = wt_ref[...].T
    lo, hi = _unpack_halves(yt_ref[0])
    r_lo = lo * wt[:, 0:1]
    r_hi = hi * wt[:, 0:1]
    for k in range(1, TOP_K):
        lo, hi = _unpack_halves(yt_ref[k])
        r_lo = r_lo + lo * wt[:, k:k + 1]
        r_hi = r_hi + hi * wt[:, k:k + 1]
    gate = mod_ref[0][5:6]
    o_ref[:, :half] = xb_ref[:, :half] + gate[:, :half] * r_lo
    o_ref[:, half:] = xb_ref[:, half:] + gate[:, half:] * r_hi


def _combine(yt, w_k, xb, mod3, seq):
    n, d = xb.shape
    tm = TM_COMBINE
    tiles_per_seq = seq // tm
    return pl.pallas_call(
        _combine_kernel,
        out_shape=jax.ShapeDtypeStruct((n, d), F32),
        grid=(n // tm,),
        in_specs=[
            pl.BlockSpec((TOP_K, tm, d // 2), lambda i: (0, i, 0)),
            pl.BlockSpec((TOP_K, tm), lambda i: (0, i)),
            pl.BlockSpec((tm, d), lambda i: (i, 0)),
            pl.BlockSpec((1, N_MOD, d), lambda i: (i // tiles_per_seq, 0, 0)),
        ],
        out_specs=pl.BlockSpec((tm, d), lambda i: (i, 0)),
        compiler_params=pltpu.CompilerParams(vmem_limit_bytes=VMEM_LIMIT),
        name="combine",
    )(yt, w_k, xb, mod3)


def _rotary_tables(seq):
    half = RET_DK // 2
    inv = ROPE_BASE ** (-np.arange(half, dtype=np.float64) / half)
    ang = np.arange(seq, dtype=np.float64)[:, None] * inv[None, :]
    cos = np.cos(ang).astype(np.float32)
    sin = np.sin(ang).astype(np.float32)
    return (jnp.asarray(np.concatenate([cos, cos], axis=-1)),
            jnp.asarray(np.concatenate([-sin, sin], axis=-1)))


def kernel(x, c, w_ada, b_ada, g_mix, w_in, q_gain, k_gain, w_out, g_ffn, w_router, router_bias,
           w_gate, w_up, w_down, ws_gate, ws_up, ws_down):
    bsz, seq, d = x.shape
    n = bsz * seq
    depth = w_ada.shape[0]
    cos_full, sin_signed = _rotary_tables(seq)
    log_g = jnp.asarray(np.log1p(-np.exp2(-5.0 - np.arange(RET_HEADS, dtype=np.float64))).astype(np.float32))
    ret_w = RET_HEADS * RET_DK
    x2 = x.reshape(n, d)
    for l in range(depth):
        mod3 = _adaln(c, w_ada[l], b_ada[l]).reshape(bsz, N_MOD, d)
        proj = _inproj(x2, mod3, g_mix[l].reshape(1, d), w_in[l], cos_full, sin_signed, seq)
        proj3 = proj.reshape(bsz, seq, IN_COLS)
        qg2 = jnp.tile(q_gain[l].reshape(1, MOBA_DH), (1, 2))
        kg2 = jnp.tile(k_gain[l].reshape(1, MOBA_DH), (1, 2))
        ret, moba = _mixers(log_g, proj3, qg2, kg2)
        xb, h2, e_idx, w_k, rank, cnt = _mid(
            ret.reshape(n, ret_w), moba.reshape(n, MOBA_HEADS * MOBA_DH), x2, mod3,
            w_out[l], g_ffn[l].reshape(1, d),
            w_router[l], router_bias[l].reshape(N_EXPERTS, 1),
            ws_gate[l], ws_up[l], ws_down[l], seq)
        nblk = n * TOP_K // FFN_BLOCK + N_EXPERTS
        tab = _sched(cnt, nblk)
        dest3 = _dest(tab, e_idx, rank)
        xs = _sc_dispatch(h2, dest3, nblk * FFN_BLOCK)
        y = _ffn(tab, xs, w_gate[l], w_up[l], w_down[l])
        yt = _sc_gather(y, dest3)
        x2 = _combine(yt, w_k, xb, mod3, seq)
    return x2.reshape(bsz, seq, d)
```

```python
import functools

import numpy as np
import jax
import jax.numpy as jnp
from jax import lax
from jax.experimental import pallas as pl
from jax.experimental.pallas import tpu as pltpu
from jax.experimental.pallas import tpu_sc as plsc

F32 = jnp.float32
BF16 = jnp.bfloat16
I32 = jnp.int32

RET_HEADS = 4
RET_DK = 128
MOBA_HEADS = 8
MOBA_DH = 64
MOBA_BLOCK = 256
MOBA_TOPK = 3
ROPE_BASE = 10000.0
N_EXPERTS = 256
TOP_K = 8
N_GROUPS = 8
TOPK_GROUPS = 4
GROUP_SIZE = N_EXPERTS // N_GROUPS
ROUTED_SCALE = 2.5
N_MOD = 6
EPS = 1e-6
IN_COLS = 3584

LANES = 128
BF16_TILE_ROWS = 16
RET_CHUNK = 256
TN_ADALN = 1024
TM_INPROJ = 512
TM_MID = 512
TN_DEST = 2048
TM_COMBINE = 512
MOBA_ONES_ROWS = BF16_TILE_ROWS
SC_CORES = 2
SC_SUBCORES = 16
SC_CHUNK = 128
SC_GATHER_ROWS = 64
SC_GATHER_BUFS = 3
FFN_BLOCK = 1024
FFN_GRAIN = 128
FFN_LOOKAHEAD = 5
FFN_WEIGHT_SLOTS = 3
FFN_OUT_SLOTS = 3
VMEM_LIMIT = 56 * 1024 * 1024

NEG_INF = float("-inf")
LOG2_E = 1.4426950408889634


def _silu(x):
    return x * jax.nn.sigmoid(x)


def _nt_dot(a, b):
    return lax.dot_general(a, b, (((1,), (1,)), ((), ())), preferred_element_type=F32)


def _tn_dot(a, b):
    return lax.dot_general(a, b, (((0,), (0,)), ((), ())), preferred_element_type=F32)


def _dot(a, b):
    return jnp.dot(a, b, preferred_element_type=F32)


HI_MASK = -65536


def _pack_halves(v):
    w = v.shape[1] // 2
    lo = lax.bitcast_convert_type(v[:, :w].astype(BF16).astype(F32), I32)
    hi = lax.bitcast_convert_type(v[:, w:].astype(BF16).astype(F32), I32)
    return lax.shift_right_logical(lo, 16) | (hi & HI_MASK)


def _unpack_halves(u):
    lo = lax.bitcast_convert_type(lax.shift_left(u, 16), F32)
    hi = lax.bitcast_convert_type(u & HI_MASK, F32)
    return lo, hi


def _adaln_kernel(c_ref, w_ref, b_ref, o_ref):
    s = _silu(c_ref[...])
    o_ref[...] = _dot(s.astype(BF16), w_ref[...].astype(BF16)) + b_ref[...]


def _adaln(c, w_ada, b_ada):
    bsz, d = c.shape
    ncol = w_ada.shape[1]
    tn = TN_ADALN
    return pl.pallas_call(
        _adaln_kernel,
        out_shape=jax.ShapeDtypeStruct((bsz, ncol), F32),
        grid=(ncol // tn,),
        in_specs=[
            pl.BlockSpec((bsz, d), lambda j: (0, 0)),
            pl.BlockSpec((d, tn), lambda j: (0, j)),
            pl.BlockSpec((1, tn), lambda j: (0, j)),
        ],
        out_specs=pl.BlockSpec((bsz, tn), lambda j: (0, j)),
        compiler_params=pltpu.CompilerParams(vmem_limit_bytes=VMEM_LIMIT),
        name="adaln",
    )(c, w_ada, b_ada.reshape(1, ncol))


def _inproj_kernel(x_ref, mod_ref, g_ref, w_ref, cos_ref, sin_ref, o_ref):
    x = x_ref[...]
    ms = jnp.mean(x * x, axis=-1, keepdims=True)
    m = mod_ref[0]
    h = (x * lax.rsqrt(ms + EPS) * g_ref[...]) * (1.0 + m[1:2]) + m[0:1]
    hb = h.astype(BF16)
    cosf = cos_ref[...]
    sinf = sin_ref[...]
    k_scale = RET_DK ** -0.5
    width = RET_HEADS * RET_DK
    for ci in range(IN_COLS // width):
        acc = _dot(hb, w_ref[:, ci * width:(ci + 1) * width])
        if ci < 2:
            for hh in range(RET_HEADS):
                xh = acc[:, hh * RET_DK:(hh + 1) * RET_DK]
                r = xh * cosf + pltpu.roll(xh, RET_DK // 2, axis=1) * sinf
                if ci == 1:
                    r = r * k_scale
                o_ref[:, ci * width + hh * RET_DK:ci * width + (hh + 1) * RET_DK] = r.astype(BF16)
        else:
            o_ref[:, ci * width:(ci + 1) * width] = acc.astype(BF16)


def _inproj(x2, mod3, g_mix, w_in, cos_full, sin_signed, seq):
    n, d = x2.shape
    tm = TM_INPROJ
    tiles_per_seq = seq // tm
    return pl.pallas_call(
        _inproj_kernel,
        out_shape=jax.ShapeDtypeStruct((n, IN_COLS), BF16),
        grid=(n // tm,),
        in_specs=[
            pl.BlockSpec((tm, d), lambda i: (i, 0)),
            pl.BlockSpec((1, N_MOD, d), lambda i: (i // tiles_per_seq, 0, 0)),
            pl.BlockSpec((1, d), lambda i: (0, 0)),
            pl.BlockSpec((d, IN_COLS), lambda i: (0, 0), pipeline_mode=pl.Buffered(1)),
            pl.BlockSpec((tm, LANES), lambda i: (i % tiles_per_seq, 0)),
            pl.BlockSpec((tm, LANES), lambda i: (i % tiles_per_seq, 0)),
        ],
        out_specs=pl.BlockSpec((tm, IN_COLS), lambda i: (i, 0)),
        compiler_params=pltpu.CompilerParams(vmem_limit_bytes=VMEM_LIMIT),
        name="inproj",
    )(x2, mod3, g_mix, w_in, cos_full, sin_signed)


def _ret_kernel(lg_ref, q_ref, k_ref, v_ref, g_ref, o_ref):
    seq = q_ref.shape[1]
    c = RET_CHUNK
    lg = lg_ref[pl.program_id(1)]
    row = lax.broadcasted_iota(I32, (c, c), 0)
    col = lax.broadcasted_iota(I32, (c, c), 1)
    diff = (row - col).astype(F32)
    dmask = jnp.where(diff >= 0, jnp.exp(lg * jnp.maximum(diff, 0.0)), 0.0)
    idx = lax.broadcasted_iota(I32, (c, 1), 0).astype(F32)
    q_decay = jnp.exp(lg * (idx + 1.0))
    k_decay = jnp.exp(lg * (c - 1.0 - idx))
    chunk_decay = jnp.exp(jnp.full((1, 1), lg * c, F32))
    state = jnp.zeros((RET_DK, RET_DK), F32)
    for n in range(seq // c):
        rows = slice(n * c, (n + 1) * c)
        qn = q_ref[0, rows, :]
        kn = k_ref[0, rows, :]
        vn = v_ref[0, rows, :]
        scores = _nt_dot(qn, kn) * dmask
        inner = _dot(scores.astype(BF16), vn)
        qs = (qn.astype(F32) * q_decay).astype(BF16)
        cross = _dot(qs, state.astype(BF16))
        o = inner + cross
        o = o * lax.rsqrt(jnp.mean(o * o, axis=-1, keepdims=True) + EPS)
        gn = g_ref[0, rows, :].astype(F32)
        o_ref[0, rows, :] = (_silu(gn) * o).astype(BF16)
        ks = (kn.astype(F32) * k_decay).astype(BF16)
        state = state * chunk_decay + _tn_dot(ks, vn)


def _moba_kernel(q_ref, k_ref, v_ref, qg_ref, kg_ref, o_ref, qt_s, ka_s, kb_s, vta_s, vtb_s):
    seq = q_ref.shape[1]
    lb = MOBA_BLOCK
    nb = seq // lb
    lane = lax.broadcasted_iota(I32, (1, LANES), 1)
    is_a = lane < MOBA_DH

    def head_norm(xf, gain):
        sq = xf * xf
        s_a = jnp.sum(jnp.where(is_a, sq, 0.0), axis=-1, keepdims=True)
        s_b = jnp.sum(jnp.where(is_a, 0.0, sq), axis=-1, keepdims=True)
        inv = jnp.where(is_a, lax.rsqrt(s_a / MOBA_DH + EPS), lax.rsqrt(s_b / MOBA_DH + EPS))
        return xf * inv * gain

    qg = qg_ref[...]
    kg = kg_ref[...]
    k_means = []
    for j in range(nb):
        rows = slice(j * lb, (j + 1) * lb)
        kf = head_norm(k_ref[0, rows, :].astype(F32), kg)
        ka_s[rows, :] = jnp.where(is_a, kf, 0.0).astype(BF16)
        kb_s[rows, :] = jnp.where(is_a, 0.0, kf).astype(BF16)
        k_means.append(jnp.mean(kf, axis=0, keepdims=True))
        qf = head_norm(q_ref[0, rows, :].astype(F32), qg)
        qt_s[:, rows] = (qf * (MOBA_DH ** -0.5 * LOG2_E)).T.astype(BF16)
        vt = v_ref[0, rows, :].T
        ones = jnp.ones((MOBA_ONES_ROWS, lb), BF16)
        vta_s[:, rows] = jnp.concatenate([vt[:MOBA_DH], ones], axis=0)
        vtb_s[:, rows] = jnp.concatenate([vt[MOBA_DH:], ones], axis=0)
    k_mean = jnp.concatenate(k_means + [jnp.zeros((BF16_TILE_ROWS - nb, LANES), F32)], axis=0)
    k_mean_h = (jnp.where(is_a, k_mean, 0.0).astype(BF16), jnp.where(is_a, 0.0, k_mean).astype(BF16))
    k_s = (ka_s, kb_s)
    vt_s = (vta_s, vtb_s)

    r_loc = lax.broadcasted_iota(I32, (lb, lb), 0)
    c_loc = lax.broadcasted_iota(I32, (lb, lb), 1)
    causal = r_loc <= c_loc

    def logits(i, hx):
        qt = qt_s[:, i * lb:(i + 1) * lb]
        bias = [None] * i
        if i > MOBA_TOPK:
            gate = _dot(k_mean_h[hx], qt)
            g = [gate[j:j + 1, :] for j in range(i)]
            for j in range(i):
                rank = jnp.zeros((1, lb), F32)
                for j2 in range(i):
                    if j2 == j:
                        continue
                    beats = (g[j2] >= g[j]) if j2 < j else (g[j2] > g[j])
                    rank = rank + jnp.where(beats, 1.0, 0.0)
                bias[j] = jnp.where(rank < float(MOBA_TOPK), 0.0, NEG_INF).astype(BF16)
        pieces = []
        for j in range(i + 1):
            s = _dot(k_s[hx][j * lb:(j + 1) * lb, :], qt).astype(BF16)
            if j == i:
                s = jnp.where(causal, s, jnp.asarray(NEG_INF, BF16))
            elif bias[j] is not None:
                s = s + bias[j]
            pieces.append(s)
        mx = jnp.max(pieces[0], axis=0, keepdims=True)
        for s in pieces[1:]:
            mx = jnp.maximum(mx, jnp.max(s, axis=0, keepdims=True))
        return pieces, mx

    def attend(hx, pieces, mx):
        acc = jnp.zeros((MOBA_DH + MOBA_ONES_ROWS, lb), F32)
        for j, s in enumerate(pieces):
            acc = acc + _dot(vt_s[hx][:, j * lb:(j + 1) * lb], jnp.exp2(s - mx))
        return acc[:MOBA_DH] / acc[MOBA_DH:MOBA_DH + 1, :]

    groups = [(i, hx) for i in range(nb) for hx in range(2)]
    outs = {}
    pending = logits(*groups[0])
    for n, (i, hx) in enumerate(groups):
        ahead = logits(*groups[n + 1]) if n + 1 < len(groups) else None
        outs[(i, hx)] = attend(hx, *pending)
        pending = ahead
        if hx == 1:
            o_ref[0, i * lb:(i + 1) * lb, :] = jnp.concatenate(
                [outs.pop((i, 0)), outs.pop((i, 1))], axis=0).T.astype(BF16)


def _mixers_kernel(lg_ref, rq_ref, rk_ref, rv_ref, rg_ref, mq_ref, mk_ref, mv_ref, qg_ref, kg_ref,
                   ret_ref, moba_ref, *moba_scratch):
    _ret_kernel(lg_ref, rq_ref, rk_ref, rv_ref, rg_ref, ret_ref)
    _moba_kernel(mq_ref, mk_ref, mv_ref, qg_ref, kg_ref, moba_ref, *moba_scratch)


def _mixers(log_g, proj3, qg2, kg2):
    bsz, seq, _ = proj3.shape
    pairs = MOBA_HEADS // 2
    assert pairs == RET_HEADS and RET_DK == LANES
    blk = (1, seq, LANES)
    base = 4 * RET_HEADS

    def col(first):
        return pl.BlockSpec(blk, lambda b, p, lg: (b, 0, first + p))

    return pl.pallas_call(
        _mixers_kernel,
        out_shape=(jax.ShapeDtypeStruct((bsz, seq, RET_HEADS * RET_DK), BF16),
                   jax.ShapeDtypeStruct((bsz, seq, MOBA_HEADS * MOBA_DH), BF16)),
        grid_spec=pltpu.PrefetchScalarGridSpec(
            num_scalar_prefetch=1,
            grid=(bsz, pairs),
            in_specs=[
                col(0), col(RET_HEADS), col(2 * RET_HEADS), col(3 * RET_HEADS),
                col(base), col(base + pairs), col(base + 2 * pairs),
                pl.BlockSpec((1, LANES), lambda b, p, lg: (0, 0)),
                pl.BlockSpec((1, LANES), lambda b, p, lg: (0, 0)),
            ],
            out_specs=(col(0), col(0)),
            scratch_shapes=[
                pltpu.VMEM((LANES, seq), BF16),
                pltpu.VMEM((seq, LANES), BF16),
                pltpu.VMEM((seq, LANES), BF16),
                pltpu.VMEM((MOBA_DH + MOBA_ONES_ROWS, seq), BF16),
                pltpu.VMEM((MOBA_DH + MOBA_ONES_ROWS, seq), BF16),
            ],
        ),
        compiler_params=pltpu.CompilerParams(vmem_limit_bytes=VMEM_LIMIT),
        name="mixers",
    )(log_g, *([proj3] * 7), qg2, kg2)


def _mid_kernel(ret_ref, moba_ref, x_ref, mod_ref, wo_ref, g_ref, wr_ref, rb_ref,
                wsg_ref, wsu_ref, wsd_ref,
                xb_ref, h2_ref, e_ref, w_ref, rk_ref, cnt_ref,
                carry_s, wo_s, wr_s, wsg_s, wsu_s, wsd_s, upper_s):
    i = pl.program_id(0)
    tm = x_ref.shape[0]
    half = ret_ref.shape[1]

    @pl.when(i == 0)
    def _():
        carry_s[...] = jnp.zeros_like(carry_s)
        wo_s[...] = wo_ref[...].astype(BF16)
        wsg_s[...] = wsg_ref[...].astype(BF16)
        wsu_s[...] = wsu_ref[...].astype(BF16)
        wsd_s[...] = wsd_ref[...].astype(BF16)
        wr_s[...] = wr_ref[...].T.astype(BF16)
        tr = lax.broadcasted_iota(I32, (tm, tm), 0)
        tc = lax.broadcasted_iota(I32, (tm, tm), 1)
        upper_s[...] = jnp.where(tr < tc, 1.0, 0.0).astype(BF16)

    m = mod_ref[0]
    mixed = _dot(ret_ref[...], wo_s[:half, :]) + _dot(moba_ref[...], wo_s[half:, :])
    x1 = x_ref[...] + m[2:3] * mixed
    ms = jnp.mean(x1 * x1, axis=-1, keepdims=True)
    h2 = (x1 * lax.rsqrt(ms + EPS) * g_ref[...]) * (1.0 + m[4:5]) + m[3:4]
    h2_ref[...] = _pack_halves(h2)
    h2b = h2.astype(BF16)

    scores = jax.nn.sigmoid(_nt_dot(wr_s[...], h2b))

    hid = _silu(_dot(h2b, wsg_s[...])) * _dot(h2b, wsu_s[...])
    xb_ref[...] = x1 + m[5:6] * _dot(hid.astype(BF16), wsd_s[...])

    biased = scores + rb_ref[...]
    grp = biased.reshape(N_GROUPS, GROUP_SIZE, tm)
    gi = lax.broadcasted_iota(I32, (N_GROUPS, GROUP_SIZE, tm), 1).astype(F32)
    top1 = jnp.max(grp, axis=1, keepdims=True)
    first = jnp.min(jnp.where(grp == top1, gi, float(GROUP_SIZE)), axis=1, keepdims=True)
    top2 = jnp.max(jnp.where(gi == first, NEG_INF, grp), axis=1, keepdims=True)
    gscore = (top1 + top2).reshape(N_GROUPS, tm)
    gidx = lax.broadcasted_iota(I32, (N_GROUPS, tm), 0)
    grank = jnp.zeros((N_GROUPS, tm), F32)
    for g2 in range(N_GROUPS):
        rowv = gscore[g2:g2 + 1, :]
        beats = (rowv > gscore) | ((rowv == gscore) & (g2 < gidx))
        grank = grank + jnp.where(beats, 1.0, 0.0)
    gsel = jnp.where(grank < float(TOPK_GROUPS), 1.0, 0.0)
    emask = jnp.broadcast_to(gsel.reshape(N_GROUPS, 1, tm), (N_GROUPS, GROUP_SIZE, tm)).reshape(N_EXPERTS, tm)
    choice = jnp.where(emask > 0.5, biased, NEG_INF)

    eidx = lax.broadcasted_iota(I32, (N_EXPERTS, tm), 0).astype(F32)
    remaining = choice
    e_rows = []
    for _k in range(TOP_K):
        mx = jnp.max(remaining, axis=0, keepdims=True)
        idx = jnp.min(jnp.where(remaining == mx, eidx, float(N_EXPERTS)), axis=0, keepdims=True)
        e_rows.append(idx)
        remaining = jnp.where(eidx == idx, NEG_INF, remaining)
    selmask = jnp.where(remaining != choice, 1.0, 0.0)

    prefix = _dot(selmask.astype(BF16), upper_s[...]) + carry_s[...]
    w_rows = []
    r_rows = []
    for ek in e_rows:
        onehot = eidx == ek
        w_rows.append(jnp.sum(jnp.where(onehot, scores, 0.0), axis=0, keepdims=True))
        r_rows.append(jnp.sum(jnp.where(onehot, prefix, 0.0), axis=0, keepdims=True))
    wsum = w_rows[0]
    for wk in w_rows[1:]:
        wsum = wsum + wk
    carry_s[...] = carry_s[...] + jnp.sum(selmask, axis=1, keepdims=True)

    e_ref[...] = jnp.concatenate(e_rows, axis=0).astype(I32)
    w_ref[...] = jnp.concatenate([wk / wsum * ROUTED_SCALE for wk in w_rows], axis=0)
    rk_ref[...] = jnp.concatenate(r_rows, axis=0).astype(I32)
    cnt_ref[...] = carry_s[...].astype(I32)


def _mid(ret2, moba2, x2, mod3, wo, g_ffn, w_router, rbias, wsg, wsu, wsd, seq):
    n, d = x2.shape
    tm = TM_MID
    tiles_per_seq = seq // tm
    half = ret2.shape[1]
    ff = wsg.shape[1]
    const = lambda i: (0, 0)
    row = lambda i: (i, 0)
    colt = lambda i: (0, i)

    def resident(shape):
        return pl.BlockSpec(shape, const, pipeline_mode=pl.Buffered(1))

    return pl.pallas_call(
        _mid_kernel,
        out_shape=(
            jax.ShapeDtypeStruct((n, d), F32),
            jax.ShapeDtypeStruct((n, d // 2), I32),
            jax.ShapeDtypeStruct((TOP_K, n), I32),
            jax.ShapeDtypeStruct((TOP_K, n), F32),
            jax.ShapeDtypeStruct((TOP_K, n), I32),
            jax.ShapeDtypeStruct((N_EXPERTS, 1), I32),
        ),
        grid=(n // tm,),
        in_specs=[
            pl.BlockSpec((tm, half), row),
            pl.BlockSpec((tm, half), row),
            pl.BlockSpec((tm, d), row),
            pl.BlockSpec((1, N_MOD, d), lambda i: (i // tiles_per_seq, 0, 0)),
            resident(wo.shape),
            pl.BlockSpec((1, d), const),
            resident((d, N_EXPERTS)),
            pl.BlockSpec((N_EXPERTS, 1), const),
            resident((d, ff)),
            resident((d, ff)),
            resident((ff, d)),
        ],
        out_specs=(
            pl.BlockSpec((tm, d), row),
            pl.BlockSpec((tm, d // 2), row),
            pl.BlockSpec((TOP_K, tm), colt),
            pl.BlockSpec((TOP_K, tm), colt),
            pl.BlockSpec((TOP_K, tm), colt),
            pl.BlockSpec((N_EXPERTS, 1), const),
        ),
        scratch_shapes=[
            pltpu.VMEM((N_EXPERTS, 1), F32),
            pltpu.VMEM(wo.shape, BF16),
            pltpu.VMEM((N_EXPERTS, d), BF16),
            pltpu.VMEM((d, ff), BF16),
            pltpu.VMEM((d, ff), BF16),
            pltpu.VMEM((ff, d), BF16),
            pltpu.VMEM((tm, tm), BF16),
        ],
        compiler_params=pltpu.CompilerParams(
            dimension_semantics=("arbitrary",), vmem_limit_bytes=VMEM_LIMIT),
        name="mid",
    )(ret2, moba2, x2, mod3, wo, g_ffn, w_router, rbias, wsg, wsu, wsd)


ROW_EXP = 0
ROW_VALID = 1
ROW_FIRST = 2
ROW_AHEAD = 3
ROW_SLOT = 4
ROW_HEAD = 5
ROW_START = 6
SCHED_ROWS = 8


def _sched_kernel(cnt_ref, tab_ref):
    ne = N_EXPERTS
    nblk = tab_ref.shape[1]
    shift = FFN_BLOCK.bit_length() - 1
    e_sub = lax.broadcasted_iota(I32, (ne, ne), 0)
    e_lane = lax.broadcasted_iota(I32, (ne, ne), 1)
    e_col = lax.broadcasted_iota(I32, (ne, 1), 0).astype(F32)
    ids_row = lax.broadcasted_iota(I32, (1, ne), 1).astype(F32) + 1.0

    def to_row(col):
        return jnp.sum(jnp.where(e_sub == e_lane, col, 0.0), axis=0, keepdims=True)

    def running_col(row):
        return jnp.sum(jnp.where(e_lane <= e_sub, row, 0.0), axis=1, keepdims=True)

    def running_row(col):
        return jnp.sum(jnp.where(e_sub <= e_lane, col, 0.0), axis=0, keepdims=True)

    cnt_i = cnt_ref[...]
    cnt_col = cnt_i.astype(F32)
    pad_col = lax.shift_left(lax.shift_right_logical(cnt_i + (FFN_BLOCK - 1), shift), shift).astype(F32)
    pad_row = to_row(pad_col)
    ends_col = running_col(pad_row)
    start_col = ends_col - pad_col
    start_row = running_row(pad_col) - pad_row
    nreal = jnp.sum(pad_row, axis=1, keepdims=True) * (1.0 / FFN_BLOCK)

    step = lax.broadcasted_iota(I32, (1, nblk), 1)
    g = step.astype(F32)
    row0 = jnp.minimum(g, nreal - 1.0) * float(FFN_BLOCK)
    exp_g = jnp.sum(jnp.where(ends_col <= row0, 1.0, 0.0), axis=0, keepdims=True)
    mine = e_col == exp_g

    def per_block(col):
        return jnp.sum(jnp.where(mine, col, 0.0), axis=0, keepdims=True)

    valid = jnp.clip(per_block(cnt_col + start_col) - row0, 0.0, float(FFN_BLOCK))
    first = jnp.where(jnp.logical_and(g < nreal, per_block(start_col) == row0), 1.0, 0.0)

    used_col = jnp.where(cnt_col > 0.0, 1.0, 0.0)
    used_row = to_row(used_col)
    ord_col = running_col(used_row) - 1.0
    ord_row = running_row(used_col) - 1.0

    def used_at(pos):
        hit = jnp.logical_and(used_row > 0.0, ord_row == pos)
        return jnp.sum(jnp.where(hit, ids_row, 0.0), axis=1, keepdims=True) - 1.0

    ahead_col = used_at(ord_col + float(FFN_WEIGHT_SLOTS - 1))
    slot_col = ord_col - FFN_WEIGHT_SLOTS * jnp.floor((ord_col + 0.5) * (1.0 / FFN_WEIGHT_SLOTS))
    head = jnp.where(step == 0, nreal, 0.0)
    for j in range(FFN_WEIGHT_SLOTS - 1):
        head = head + jnp.where(step == 1 + j, used_at(jnp.full((1, 1), float(j), F32)), 0.0)
    rows = [exp_g, valid, first, per_block(ahead_col), per_block(slot_col), head,
            jnp.concatenate([start_row, jnp.zeros((1, nblk - ne), F32)], axis=1),
            jnp.zeros((1, nblk), F32)]
    tab_ref[...] = jnp.concatenate(rows, axis=0).astype(I32)


def _sched(cnt, nblk):
    assert FFN_BLOCK & (FFN_BLOCK - 1) == 0 and nblk >= N_EXPERTS
    return pl.pallas_call(
        _sched_kernel,
        out_shape=jax.ShapeDtypeStruct((SCHED_ROWS, nblk), I32),
        compiler_params=pltpu.CompilerParams(vmem_limit_bytes=VMEM_LIMIT),
        name="sched",
    )(cnt)


def _dest_kernel(tab_ref, e_ref, rk_ref, o_ref):
    k = e_ref.shape[0]
    segments = [jnp.broadcast_to(tab_ref[ROW_START:ROW_START + 1, s * LANES:(s + 1) * LANES], (k, LANES))
                for s in range(N_EXPERTS // LANES)]
    for ch in range(o_ref.shape[0]):
        cols = slice(ch * SC_CHUNK, (ch + 1) * SC_CHUNK)
        e = e_ref[:, cols]
        lane = e & (LANES - 1)
        start = jnp.take_along_axis(segments[0], lane, axis=1)
        for s in range(1, len(segments)):
            start = jnp.where(e >= s * LANES, jnp.take_along_axis(segments[s], lane, axis=1), start)
        o_ref[ch] = start + rk_ref[:, cols]


def _dest(tab, e_idx, rank):
    k, n = e_idx.shape
    tn = TN_DEST
    assert SC_CHUNK == LANES and N_EXPERTS % LANES == 0
    return pl.pallas_call(
        _dest_kernel,
        out_shape=jax.ShapeDtypeStruct((n // SC_CHUNK, k, SC_CHUNK), I32),
        grid=(n // tn,),
        in_specs=[
            pl.BlockSpec(tab.shape, lambda i: (0, 0)),
            pl.BlockSpec((k, tn), lambda i: (0, i)),
            pl.BlockSpec((k, tn), lambda i: (0, i)),
        ],
        out_specs=pl.BlockSpec((tn // SC_CHUNK, k, SC_CHUNK), lambda i: (i, 0, 0)),
        compiler_params=pltpu.CompilerParams(vmem_limit_bytes=VMEM_LIMIT),
        name="dest",
    )(tab, e_idx, rank)


def _sc_dispatch(h2p, dest3, total_rows):
    n, words = h2p.shape
    nchunks = n // SC_CHUNK
    per_worker = nchunks // (SC_CORES * SC_SUBCORES)
    mesh = plsc.VectorSubcoreMesh(core_axis_name="c", subcore_axis_name="s",
                                  num_cores=SC_CORES, num_subcores=SC_SUBCORES)

    @functools.partial(
        pl.kernel, mesh=mesh,
        out_type=jax.ShapeDtypeStruct((total_rows, words), I32),
        scratch_types=[
            pltpu.VMEM((TOP_K, SC_CHUNK), I32),
            pltpu.VMEM((SC_CHUNK, words), I32),
            pltpu.SemaphoreType.DMA,
        ],
        name="sc_dispatch",
    )
    def run(h_hbm, d_hbm, xs_hbm, idx_v, rows_v, sem):
        wid = lax.axis_index("s") * SC_CORES + lax.axis_index("c")

        @pl.loop(0, per_worker)
        def _(j):
            ch = wid * per_worker + j
            pltpu.sync_copy(d_hbm.at[ch], idx_v)
            pltpu.sync_copy(h_hbm.at[pl.ds(ch * SC_CHUNK, SC_CHUNK)], rows_v)
            copies = [pltpu.async_copy(rows_v, xs_hbm.at[idx_v.at[k]], sem) for k in range(TOP_K)]
            for cp in copies:
                cp.wait()

    return run(h2p, dest3)


def _ffn_kernel(tab_ref, x_hbm, wg_hbm, wu_hbm, wd_hbm, y_hbm,
                x_s, y_s, wg_s, wu_s, wd_s, sem_x, sem_y, sem):
    i = pl.program_id(0)
    nreal = tab_ref[ROW_HEAD, 0]
    rows_per = x_s.shape[1]
    half = x_s.shape[2]
    sizes = tuple(range(FFN_GRAIN, rows_per + 1, FFN_GRAIN))

    def rows_needed(g):
        return (tab_ref[ROW_VALID, g] + (FFN_GRAIN - 1)) // FFN_GRAIN * FFN_GRAIN

    def by_size(nrows, fn):
        for n in sizes:
            @pl.when(nrows == n)
            def _():
                fn(n)

    def row_copy(g, n):
        slot = lax.rem(g, FFN_LOOKAHEAD + 1)
        return pltpu.make_async_copy(
            x_hbm.at[pl.ds(g * rows_per, n)], x_s.at[slot, pl.ds(0, n)], sem_x.at[slot])

    def out_copy(g, n):
        slot = lax.rem(g, FFN_OUT_SLOTS)
        return pltpu.make_async_copy(
            y_s.at[slot, pl.ds(0, n)], y_hbm.at[pl.ds(g * rows_per, n)], sem_y.at[slot])

    def weight_copies(e, s):
        return (pltpu.make_async_copy(wg_hbm.at[e], wg_s.at[s], sem.at[s, 0]),
                pltpu.make_async_copy(wu_hbm.at[e], wu_s.at[s], sem.at[s, 1]),
                pltpu.make_async_copy(wd_hbm.at[e], wd_s.at[s], sem.at[s, 2]))

    @pl.when(i == 0)
    def _():
        for j in range(FFN_WEIGHT_SLOTS - 1):
            @pl.when(tab_ref[ROW_HEAD, 1 + j] >= 0)
            def _():
                for cp in weight_copies(tab_ref[ROW_HEAD, 1 + j], j):
                    cp.start()
        for g in range(FFN_LOOKAHEAD):
            @pl.when(g < nreal)
            def _():
                by_size(rows_needed(g), lambda n: row_copy(g, n).start())

    @pl.when(i < nreal)
    def _():
        s = tab_ref[ROW_SLOT, i]
        fetch = i + FFN_LOOKAHEAD

        @pl.when(fetch < nreal)
        def _():
            by_size(rows_needed(fetch), lambda n: row_copy(fetch, n).start())

        by_size(rows_needed(i), lambda n: row_copy(i, n).wait())

        @pl.when(tab_ref[ROW_FIRST, i] == 1)
        def _():
            for cp in weight_copies(tab_ref[ROW_EXP, i], s):
                cp.wait()

            ahead = tab_ref[ROW_AHEAD, i]

            @pl.when(ahead >= 0)
            def _():
                for cp in weight_copies(ahead, lax.rem(s + FFN_WEIGHT_SLOTS - 1, FFN_WEIGHT_SLOTS)):
                    cp.start()

        @pl.when(i >= FFN_OUT_SLOTS)
        def _():
            done = i - FFN_OUT_SLOTS
            by_size(rows_needed(done), lambda n: out_copy(done, n).wait())

        x_slot = lax.rem(i, FFN_LOOKAHEAD + 1)
        y_slot = lax.rem(i, FFN_OUT_SLOTS)
        valid = tab_ref[ROW_VALID, i]

        def expert_rows(n):
            r = lax.broadcasted_iota(I32, (n, 1), 0)
            x_lo, x_hi = _unpack_halves(jnp.where(r < valid, x_s[x_slot, pl.ds(0, n), :], 0))
            hg = _dot(x_lo, wg_s[s, :half, :]) + _dot(x_hi, wg_s[s, half:, :])
            hu = _dot(x_lo, wu_s[s, :half, :]) + _dot(x_hi, wu_s[s, half:, :])
            y_s[y_slot, pl.ds(0, n), :] = _pack_halves(_dot(_silu(hg) * hu, wd_s[s]))
            out_copy(i, n).start()

        by_size(rows_needed(i), expert_rows)

        @pl.when(i == nreal - 1)
        def _():
            for back in range(FFN_OUT_SLOTS):
                last = i - back

                @pl.when(last >= 0)
                def _():
                    by_size(rows_needed(last), lambda n: out_copy(last, n).wait())


def _ffn(tab, xs, w_gate, w_up, w_down):
    p, half = xs.shape
    d = 2 * half
    ff = w_gate.shape[2]
    return pl.pallas_call(
        _ffn_kernel,
        out_shape=jax.ShapeDtypeStruct((p, half), I32),
        grid_spec=pltpu.PrefetchScalarGridSpec(
            num_scalar_prefetch=1,
            grid=(tab.shape[1],),
            in_specs=[
                pl.BlockSpec(memory_space=pl.ANY),
                pl.BlockSpec(memory_space=pl.ANY),
                pl.BlockSpec(memory_space=pl.ANY),
                pl.BlockSpec(memory_space=pl.ANY),
            ],
            out_specs=pl.BlockSpec(memory_space=pl.ANY),
            scratch_shapes=[
                pltpu.VMEM((FFN_LOOKAHEAD + 1, FFN_BLOCK, half), I32),
                pltpu.VMEM((FFN_OUT_SLOTS, FFN_BLOCK, half), I32),
                pltpu.VMEM((FFN_WEIGHT_SLOTS, d, ff), F32),
                pltpu.VMEM((FFN_WEIGHT_SLOTS, d, ff), F32),
                pltpu.VMEM((FFN_WEIGHT_SLOTS, ff, d), F32),
                pltpu.SemaphoreType.DMA((FFN_LOOKAHEAD + 1,)),
                pltpu.SemaphoreType.DMA((FFN_OUT_SLOTS,)),
                pltpu.SemaphoreType.DMA((FFN_WEIGHT_SLOTS, 3)),
            ],
        ),
        compiler_params=pltpu.CompilerParams(
            dimension_semantics=("arbitrary",), vmem_limit_bytes=VMEM_LIMIT, has_side_effects=True),
        name="ffn",
    )(tab, xs, w_gate, w_up, w_down)


def _sc_gather(y, dest3):
    a, words = y.shape
    nchunks, _, chunk = dest3.shape
    n = nchunks * chunk
    per_worker = nchunks // (SC_CORES * SC_SUBCORES)
    nbuf = SC_GATHER_BUFS
    parts = chunk // SC_GATHER_ROWS
    items = [(c, k, h) for c in range(per_worker) for k in range(TOP_K) for h in range(parts)]
    mesh = plsc.VectorSubcoreMesh(core_axis_name="c", subcore_axis_name="s",
                                  num_cores=SC_CORES, num_subcores=SC_SUBCORES)

    @functools.partial(
        pl.kernel, mesh=mesh,
        out_type=jax.ShapeDtypeStruct((TOP_K, n, words), I32),
        scratch_types=[
            pltpu.VMEM((per_worker, TOP_K, chunk), I32),
            [pltpu.VMEM((SC_GATHER_ROWS, words), I32)] * nbuf,
            pltpu.SemaphoreType.DMA((nbuf,)),
            pltpu.SemaphoreType.DMA((nbuf,)),
        ],
        name="sc_gather",
    )
    def run(y_hbm, d_hbm, yt_hbm, idx_v, bufs, sem_g, sem_w):
        wid = lax.axis_index("s") * SC_CORES + lax.axis_index("c")
        pltpu.sync_copy(d_hbm.at[pl.ds(wid * per_worker, per_worker)], idx_v)

        def gather(m):
            c, k, h = items[m]
            idx = idx_v.at[c, k, pl.ds(h * SC_GATHER_ROWS, SC_GATHER_ROWS)]
            return pltpu.async_copy(y_hbm.at[idx], bufs[m % nbuf], sem_g.at[m % nbuf])

        def write(m):
            c, k, h = items[m]
            rows = pl.ds((wid * per_worker + c) * chunk + h * SC_GATHER_ROWS, SC_GATHER_ROWS)
            return pltpu.async_copy(bufs[m % nbuf], yt_hbm.at[k, rows], sem_w.at[m % nbuf])

        gathers = {m: gather(m) for m in range(min(nbuf - 1, len(items)))}
        writes = {}
        for m in range(len(items)):
            gathers.pop(m).wait()
            writes[m] = write(m)
            nxt = m + nbuf - 1
            if nxt < len(items):
                if m >= 1:
                    writes.pop(m - 1).wait()
                gathers[nxt] = gather(nxt)
        for m in sorted(writes):
            writes.pop(m).wait()

    return run(y, dest3)


def _combine_kernel(yt_ref, wt_ref, xb_ref, mod_ref, o_ref):
    half = yt_ref.shape[2]
    wt = wt_ref[...].T
    lo, hi = _unpack_halves(yt_ref[0])
    r_lo = lo * wt[:, 0:1]
    r_hi = hi * wt[:, 0:1]
    for k in range(1, TOP_K):
        lo, hi = _unpack_halves(yt_ref[k])
        r_lo = r_lo + lo * wt[:, k:k + 1]
        r_hi = r_hi + hi * wt[:, k:k + 1]
    gate = mod_ref[0][5:6]
    o_ref[:, :half] = xb_ref[:, :half] + gate[:, :half] * r_lo
    o_ref[:, half:] = xb_ref[:, half:] + gate[:, half:] * r_hi


def _combine(yt, w_k, xb, mod3, seq):
    n, d = xb.shape
    tm = TM_COMBINE
    tiles_per_seq = seq // tm
    return pl.pallas_call(
        _combine_kernel,
        out_shape=jax.ShapeDtypeStruct((n, d), F32),
        grid=(n // tm,),
        in_specs=[
            pl.BlockSpec((TOP_K, tm, d // 2), lambda i: (0, i, 0)),
            pl.BlockSpec((TOP_K, tm), lambda i: (0, i)),
            pl.BlockSpec((tm, d), lambda i: (i, 0)),
            pl.BlockSpec((1, N_MOD, d), lambda i: (i // tiles_per_seq, 0, 0)),
        ],
        out_specs=pl.BlockSpec((tm, d), lambda i: (i, 0)),
        compiler_params=pltpu.CompilerParams(vmem_limit_bytes=VMEM_LIMIT),
        name="combine",
    )(yt, w_k, xb, mod3)


def _rotary_tables(seq):
    half = RET_DK // 2
    inv = ROPE_BASE ** (-np.arange(half, dtype=np.float64) / half)
    ang = np.arange(seq, dtype=np.float64)[:, None] * inv[None, :]
    cos = np.cos(ang).astype(np.float32)
    sin = np.sin(ang).astype(np.float32)
    return (jnp.asarray(np.concatenate([cos, cos], axis=-1)),
            jnp.asarray(np.concatenate([-sin, sin], axis=-1)))


def kernel(x, c, w_ada, b_ada, g_mix, w_in, q_gain, k_gain, w_out, g_ffn, w_router, router_bias,
           w_gate, w_up, w_down, ws_gate, ws_up, ws_down):
    bsz, seq, d = x.shape
    n = bsz * seq
    depth = w_ada.shape[0]
    cos_full, sin_signed = _rotary_tables(seq)
    log_g = jnp.asarray(np.log1p(-np.exp2(-5.0 - np.arange(RET_HEADS, dtype=np.float64))).astype(np.float32))
    ret_w = RET_HEADS * RET_DK
    x2 = x.reshape(n, d)
    for l in range(depth):
        mod3 = _adaln(c, w_ada[l], b_ada[l]).reshape(bsz, N_MOD, d)
        proj = _inproj(x2, mod3, g_mix[l].reshape(1, d), w_in[l], cos_full, sin_signed, seq)
        proj3 = proj.reshape(bsz, seq, IN_COLS)
        qg2 = jnp.tile(q_gain[l].reshape(1, MOBA_DH), (1, 2))
        kg2 = jnp.tile(k_gain[l].reshape(1, MOBA_DH), (1, 2))
        ret, moba = _mixers(log_g, proj3, qg2, kg2)
        xb, h2, e_idx, w_k, rank, cnt = _mid(
            ret.reshape(n, ret_w), moba.reshape(n, MOBA_HEADS * MOBA_DH), x2, mod3,
            w_out[l], g_ffn[l].reshape(1, d),
            w_router[l], router_bias[l].reshape(N_EXPERTS, 1),
            ws_gate[l], ws_up[l], ws_down[l], seq)
        nblk = n * TOP_K // FFN_BLOCK + N_EXPERTS
        tab = _sched(cnt, nblk)
        dest3 = _dest(tab, e_idx, rank)
        xs = _sc_dispatch(h2, dest3, nblk * FFN_BLOCK)
        y = _ffn(tab, xs, w_gate[l], w_up[l], w_down[l])
        yt = _sc_gather(y, dest3)
        x2 = _combine(yt, w_k, xb, mod3, seq)
    return x2.reshape(bsz, seq, d)
```

```python
import functools

import numpy as np
import jax
import jax.numpy as jnp
from jax import lax
from jax.experimental import pallas as pl
from jax.experimental.pallas import tpu as pltpu
from jax.experimental.pallas import tpu_sc as plsc

F32 = jnp.float32
BF16 = jnp.bfloat16
I32 = jnp.int32

RET_HEADS = 4
RET_DK = 128
MOBA_HEADS = 8
MOBA_DH = 64
MOBA_BLOCK = 256
MOBA_TOPK = 3
ROPE_BASE = 10000.0
N_EXPERTS = 256
TOP_K = 8
N_GROUPS = 8
TOPK_GROUPS = 4
GROUP_SIZE = N_EXPERTS // N_GROUPS
ROUTED_SCALE = 2.5
N_MOD = 6
EPS = 1e-6
IN_COLS = 3584

LANES = 128
BF16_TILE_ROWS = 16
RET_CHUNK = 256
TN_ADALN = 1024
TM_INPROJ = 512
TM_MID = 512
TN_DEST = 2048
TM_COMBINE = 512
MOBA_ONES_ROWS = BF16_TILE_ROWS
SC_CORES = 2
SC_SUBCORES = 16
SC_CHUNK = 128
SC_GATHER_ROWS = 64
SC_GATHER_BUFS = 3
FFN_BLOCK = 1024
FFN_GRAIN = 128
FFN_LOOKAHEAD = 5
FFN_WEIGHT_SLOTS = 3
FFN_OUT_SLOTS = 3
FFN_WEIGHT_DMA_PRIORITY = 1
VMEM_LIMIT = 56 * 1024 * 1024

NEG_INF = float("-inf")
LOG2_E = 1.4426950408889634


def _silu(x):
    return x * jax.nn.sigmoid(x)


def _nt_dot(a, b):
    return lax.dot_general(a, b, (((1,), (1,)), ((), ())), preferred_element_type=F32)


def _tn_dot(a, b):
    return lax.dot_general(a, b, (((0,), (0,)), ((), ())), preferred_element_type=F32)


def _dot(a, b):
    return jnp.dot(a, b, preferred_element_type=F32)


HI_MASK = -65536


def _pack_halves(v):
    w = v.shape[1] // 2
    lo = lax.bitcast_convert_type(v[:, :w].astype(BF16).astype(F32), I32)
    hi = lax.bitcast_convert_type(v[:, w:].astype(BF16).astype(F32), I32)
    return lax.shift_right_logical(lo, 16) | (hi & HI_MASK)


def _unpack_halves(u):
    lo = lax.bitcast_convert_type(lax.shift_left(u, 16), F32)
    hi = lax.bitcast_convert_type(u & HI_MASK, F32)
    return lo, hi


def _adaln_kernel(c_ref, w_ref, b_ref, o_ref):
    s = _silu(c_ref[...])
    o_ref[...] = _dot(s.astype(BF16), w_ref[...].astype(BF16)) + b_ref[...]


def _adaln(c, w_ada, b_ada):
    bsz, d = c.shape
    ncol = w_ada.shape[1]
    tn = TN_ADALN
    return pl.pallas_call(
        _adaln_kernel,
        out_shape=jax.ShapeDtypeStruct((bsz, ncol), F32),
        grid=(ncol // tn,),
        in_specs=[
            pl.BlockSpec((bsz, d), lambda j: (0, 0)),
            pl.BlockSpec((d, tn), lambda j: (0, j)),
            pl.BlockSpec((1, tn), lambda j: (0, j)),
        ],
        out_specs=pl.BlockSpec((bsz, tn), lambda j: (0, j)),
        compiler_params=pltpu.CompilerParams(vmem_limit_bytes=VMEM_LIMIT),
        name="adaln",
    )(c, w_ada, b_ada.reshape(1, ncol))


def _inproj_kernel(x_ref, mod_ref, g_ref, w_ref, cos_ref, sin_ref, o_ref):
    x = x_ref[...]
    ms = jnp.mean(x * x, axis=-1, keepdims=True)
    m = mod_ref[0]
    h = (x * lax.rsqrt(ms + EPS) * g_ref[...]) * (1.0 + m[1:2]) + m[0:1]
    hb = h.astype(BF16)
    cosf = cos_ref[...]
    sinf = sin_ref[...]
    k_scale = RET_DK ** -0.5
    width = RET_HEADS * RET_DK
    for ci in range(IN_COLS // width):
        acc = _dot(hb, w_ref[:, ci * width:(ci + 1) * width])
        if ci < 2:
            for hh in range(RET_HEADS):
                xh = acc[:, hh * RET_DK:(hh + 1) * RET_DK]
                r = xh * cosf + pltpu.roll(xh, RET_DK // 2, axis=1) * sinf
                if ci == 1:
                    r = r * k_scale
                o_ref[:, ci * width + hh * RET_DK:ci * width + (hh + 1) * RET_DK] = r.astype(BF16)
        else:
            o_ref[:, ci * width:(ci + 1) * width] = acc.astype(BF16)


def _inproj(x2, mod3, g_mix, w_in, cos_full, sin_signed, seq):
    n, d = x2.shape
    tm = TM_INPROJ
    tiles_per_seq = seq // tm
    return pl.pallas_call(
        _inproj_kernel,
        out_shape=jax.ShapeDtypeStruct((n, IN_COLS), BF16),
        grid=(n // tm,),
        in_specs=[
            pl.BlockSpec((tm, d), lambda i: (i, 0)),
            pl.BlockSpec((1, N_MOD, d), lambda i: (i // tiles_per_seq, 0, 0)),
            pl.BlockSpec((1, d), lambda i: (0, 0)),
            pl.BlockSpec((d, IN_COLS), lambda i: (0, 0), pipeline_mode=pl.Buffered(1)),
            pl.BlockSpec((tm, LANES), lambda i: (i % tiles_per_seq, 0)),
            pl.BlockSpec((tm, LANES), lambda i: (i % tiles_per_seq, 0)),
        ],
        out_specs=pl.BlockSpec((tm, IN_COLS), lambda i: (i, 0)),
        compiler_params=pltpu.CompilerParams(vmem_limit_bytes=VMEM_LIMIT),
        name="inproj",
    )(x2, mod3, g_mix, w_in, cos_full, sin_signed)


def _ret_kernel(lg_ref, q_ref, k_ref, v_ref, g_ref, o_ref):
    seq = q_ref.shape[1]
    c = RET_CHUNK
    lg = lg_ref[pl.program_id(1)]
    row = lax.broadcasted_iota(I32, (c, c), 0)
    col = lax.broadcasted_iota(I32, (c, c), 1)
    diff = (row - col).astype(F32)
    dmask = jnp.where(diff >= 0, jnp.exp(lg * jnp.maximum(diff, 0.0)), 0.0)
    idx = lax.broadcasted_iota(I32, (c, 1), 0).astype(F32)
    q_decay = jnp.exp(lg * (idx + 1.0))
    k_decay = jnp.exp(lg * (c - 1.0 - idx))
    chunk_decay = jnp.exp(jnp.full((1, 1), lg * c, F32))
    state = jnp.zeros((RET_DK, RET_DK), F32)
    for n in range(seq // c):
        rows = slice(n * c, (n + 1) * c)
        qn = q_ref[0, rows, :]
        kn = k_ref[0, rows, :]
        vn = v_ref[0, rows, :]
        scores = _nt_dot(qn, kn) * dmask
        inner = _dot(scores.astype(BF16), vn)
        qs = (qn.astype(F32) * q_decay).astype(BF16)
        cross = _dot(qs, state.astype(BF16))
        o = inner + cross
        o = o * lax.rsqrt(jnp.mean(o * o, axis=-1, keepdims=True) + EPS)
        gn = g_ref[0, rows, :].astype(F32)
        o_ref[0, rows, :] = (_silu(gn) * o).astype(BF16)
        ks = (kn.astype(F32) * k_decay).astype(BF16)
        state = state * chunk_decay + _tn_dot(ks, vn)


def _moba_kernel(q_ref, k_ref, v_ref, qg_ref, kg_ref, o_ref, qt_s, ka_s, kb_s, vta_s, vtb_s):
    seq = q_ref.shape[1]
    lb = MOBA_BLOCK
    nb = seq // lb
    lane = lax.broadcasted_iota(I32, (1, LANES), 1)
    is_a = lane < MOBA_DH

    def head_norm(xf, gain):
        sq = xf * xf
        s_a = jnp.sum(jnp.where(is_a, sq, 0.0), axis=-1, keepdims=True)
        s_b = jnp.sum(jnp.where(is_a, 0.0, sq), axis=-1, keepdims=True)
        inv = jnp.where(is_a, lax.rsqrt(s_a / MOBA_DH + EPS), lax.rsqrt(s_b / MOBA_DH + EPS))
        return xf * inv * gain

    qg = qg_ref[...]
    kg = kg_ref[...]
    k_means = []
    for j in range(nb):
        rows = slice(j * lb, (j + 1) * lb)
        kf = head_norm(k_ref[0, rows, :].astype(F32), kg)
        ka_s[rows, :] = jnp.where(is_a, kf, 0.0).astype(BF16)
        kb_s[rows, :] = jnp.where(is_a, 0.0, kf).astype(BF16)
        k_means.append(jnp.mean(kf, axis=0, keepdims=True))
        qf = head_norm(q_ref[0, rows, :].astype(F32), qg)
        qt_s[:, rows] = (qf * (MOBA_DH ** -0.5 * LOG2_E)).T.astype(BF16)
        vt = v_ref[0, rows, :].T
        ones = jnp.ones((MOBA_ONES_ROWS, lb), BF16)
        vta_s[:, rows] = jnp.concatenate([vt[:MOBA_DH], ones], axis=0)
        vtb_s[:, rows] = jnp.concatenate([vt[MOBA_DH:], ones], axis=0)
    k_mean = jnp.concatenate(k_means + [jnp.zeros((BF16_TILE_ROWS - nb, LANES), F32)], axis=0)
    k_mean_h = (jnp.where(is_a, k_mean, 0.0).astype(BF16), jnp.where(is_a, 0.0, k_mean).astype(BF16))
    k_s = (ka_s, kb_s)
    vt_s = (vta_s, vtb_s)

    r_loc = lax.broadcasted_iota(I32, (lb, lb), 0)
    c_loc = lax.broadcasted_iota(I32, (lb, lb), 1)
    causal = r_loc <= c_loc

    def logits(i, hx):
        qt = qt_s[:, i * lb:(i + 1) * lb]
        bias = [None] * i
        if i > MOBA_TOPK:
            gate = _dot(k_mean_h[hx], qt)
            g = [gate[j:j + 1, :] for j in range(i)]
            for j in range(i):
                rank = jnp.zeros((1, lb), F32)
                for j2 in range(i):
                    if j2 == j:
                        continue
                    beats = (g[j2] >= g[j]) if j2 < j else (g[j2] > g[j])
                    rank = rank + jnp.where(beats, 1.0, 0.0)
                bias[j] = jnp.where(rank < float(MOBA_TOPK), 0.0, NEG_INF).astype(BF16)
        pieces = []
        for j in range(i + 1):
            s = _dot(k_s[hx][j * lb:(j + 1) * lb, :], qt).astype(BF16)
            if j == i:
                s = jnp.where(causal, s, jnp.asarray(NEG_INF, BF16))
            elif bias[j] is not None:
                s = s + bias[j]
            pieces.append(s)
        mx = jnp.max(pieces[0], axis=0, keepdims=True)
        for s in pieces[1:]:
            mx = jnp.maximum(mx, jnp.max(s, axis=0, keepdims=True))
        return pieces, mx

    def attend(hx, pieces, mx):
        acc = jnp.zeros((MOBA_DH + MOBA_ONES_ROWS, lb), F32)
        for j, s in enumerate(pieces):
            acc = acc + _dot(vt_s[hx][:, j * lb:(j + 1) * lb], jnp.exp2(s - mx))
        return acc[:MOBA_DH] / acc[MOBA_DH:MOBA_DH + 1, :]

    groups = [(i, hx) for i in range(nb) for hx in range(2)]
    outs = {}
    pending = logits(*groups[0])
    for n, (i, hx) in enumerate(groups):
        ahead = logits(*groups[n + 1]) if n + 1 < len(groups) else None
        outs[(i, hx)] = attend(hx, *pending)
        pending = ahead
        if hx == 1:
            o_ref[0, i * lb:(i + 1) * lb, :] = jnp.concatenate(
                [outs.pop((i, 0)), outs.pop((i, 1))], axis=0).T.astype(BF16)


def _mixers_kernel(lg_ref, rq_ref, rk_ref, rv_ref, rg_ref, mq_ref, mk_ref, mv_ref, qg_ref, kg_ref,
                   ret_ref, moba_ref, *moba_scratch):
    _ret_kernel(lg_ref, rq_ref, rk_ref, rv_ref, rg_ref, ret_ref)
    _moba_kernel(mq_ref, mk_ref, mv_ref, qg_ref, kg_ref, moba_ref, *moba_scratch)


def _mixers(log_g, proj3, qg2, kg2):
    bsz, seq, _ = proj3.shape
    pairs = MOBA_HEADS // 2
    assert pairs == RET_HEADS and RET_DK == LANES
    blk = (1, seq, LANES)
    base = 4 * RET_HEADS

    def col(first):
        return pl.BlockSpec(blk, lambda b, p, lg: (b, 0, first + p))

    return pl.pallas_call(
        _mixers_kernel,
        out_shape=(jax.ShapeDtypeStruct((bsz, seq, RET_HEADS * RET_DK), BF16),
                   jax.ShapeDtypeStruct((bsz, seq, MOBA_HEADS * MOBA_DH), BF16)),
        grid_spec=pltpu.PrefetchScalarGridSpec(
            num_scalar_prefetch=1,
            grid=(bsz, pairs),
            in_specs=[
                col(0), col(RET_HEADS), col(2 * RET_HEADS), col(3 * RET_HEADS),
                col(base), col(base + pairs), col(base + 2 * pairs),
                pl.BlockSpec((1, LANES), lambda b, p, lg: (0, 0)),
                pl.BlockSpec((1, LANES), lambda b, p, lg: (0, 0)),
            ],
            out_specs=(col(0), col(0)),
            scratch_shapes=[
                pltpu.VMEM((LANES, seq), BF16),
                pltpu.VMEM((seq, LANES), BF16),
                pltpu.VMEM((seq, LANES), BF16),
                pltpu.VMEM((MOBA_DH + MOBA_ONES_ROWS, seq), BF16),
                pltpu.VMEM((MOBA_DH + MOBA_ONES_ROWS, seq), BF16),
            ],
        ),
        compiler_params=pltpu.CompilerParams(vmem_limit_bytes=VMEM_LIMIT),
        name="mixers",
    )(log_g, *([proj3] * 7), qg2, kg2)


def _mid_kernel(ret_ref, moba_ref, x_ref, mod_ref, wo_ref, g_ref, wr_ref, rb_ref,
                wsg_ref, wsu_ref, wsd_ref,
                xb_ref, h2_ref, e_ref, w_ref, rk_ref, cnt_ref,
                carry_s, wo_s, wr_s, wsg_s, wsu_s, wsd_s, upper_s):
    i = pl.program_id(0)
    tm = x_ref.shape[0]
    half = ret_ref.shape[1]

    @pl.when(i == 0)
    def _():
        carry_s[...] = jnp.zeros_like(carry_s)
        wo_s[...] = wo_ref[...].astype(BF16)
        wsg_s[...] = wsg_ref[...].astype(BF16)
        wsu_s[...] = wsu_ref[...].astype(BF16)
        wsd_s[...] = wsd_ref[...].astype(BF16)
        wr_s[...] = wr_ref[...].T.astype(BF16)
        tr = lax.broadcasted_iota(I32, (tm, tm), 0)
        tc = lax.broadcasted_iota(I32, (tm, tm), 1)
        upper_s[...] = jnp.where(tr < tc, 1.0, 0.0).astype(BF16)

    m = mod_ref[0]
    mixed = _dot(ret_ref[...], wo_s[:half, :]) + _dot(moba_ref[...], wo_s[half:, :])
    x1 = x_ref[...] + m[2:3] * mixed
    ms = jnp.mean(x1 * x1, axis=-1, keepdims=True)
    h2 = (x1 * lax.rsqrt(ms + EPS) * g_ref[...]) * (1.0 + m[4:5]) + m[3:4]
    h2_ref[...] = _pack_halves(h2)
    h2b = h2.astype(BF16)

    hid = _silu(_dot(h2b, wsg_s[...])) * _dot(h2b, wsu_s[...])
    xb_ref[...] = x1 + m[5:6] * _dot(hid.astype(BF16), wsd_s[...])

    scores = jax.nn.sigmoid(_nt_dot(wr_s[...], h2b))
    biased = scores + rb_ref[...]
    grp = biased.reshape(N_GROUPS, GROUP_SIZE, tm)
    gi = lax.broadcasted_iota(I32, (N_GROUPS, GROUP_SIZE, tm), 1).astype(F32)
    top1 = jnp.max(grp, axis=1, keepdims=True)
    first = jnp.min(jnp.where(grp == top1, gi, float(GROUP_SIZE)), axis=1, keepdims=True)
    top2 = jnp.max(jnp.where(gi == first, NEG_INF, grp), axis=1, keepdims=True)
    gscore = (top1 + top2).reshape(N_GROUPS, tm)
    gidx = lax.broadcasted_iota(I32, (N_GROUPS, tm), 0)
    grank = jnp.zeros((N_GROUPS, tm), F32)
    for g2 in range(N_GROUPS):
        rowv = gscore[g2:g2 + 1, :]
        beats = (rowv > gscore) | ((rowv == gscore) & (g2 < gidx))
        grank = grank + jnp.where(beats, 1.0, 0.0)
    gsel = jnp.where(grank < float(TOPK_GROUPS), 1.0, 0.0)
    emask = jnp.broadcast_to(gsel.reshape(N_GROUPS, 1, tm), (N_GROUPS, GROUP_SIZE, tm)).reshape(N_EXPERTS, tm)
    choice = jnp.where(emask > 0.5, biased, NEG_INF)

    eidx = lax.broadcasted_iota(I32, (N_EXPERTS, tm), 0).astype(F32)
    remaining = choice
    e_rows = []
    for _k in range(TOP_K):
        mx = jnp.max(remaining, axis=0, keepdims=True)
        idx = jnp.min(jnp.where(remaining == mx, eidx, float(N_EXPERTS)), axis=0, keepdims=True)
        e_rows.append(idx)
        remaining = jnp.where(eidx == idx, NEG_INF, remaining)
    selmask = jnp.where(remaining != choice, 1.0, 0.0)

    prefix = _dot(selmask.astype(BF16), upper_s[...]) + carry_s[...]
    w_rows = []
    r_rows = []
    for ek in e_rows:
        onehot = eidx == ek
        w_rows.append(jnp.sum(jnp.where(onehot, scores, 0.0), axis=0, keepdims=True))
        r_rows.append(jnp.sum(jnp.where(onehot, prefix, 0.0), axis=0, keepdims=True))
    wsum = w_rows[0]
    for wk in w_rows[1:]:
        wsum = wsum + wk
    carry_s[...] = carry_s[...] + jnp.sum(selmask, axis=1, keepdims=True)

    e_ref[...] = jnp.concatenate(e_rows, axis=0).astype(I32)
    w_ref[...] = jnp.concatenate([wk / wsum * ROUTED_SCALE for wk in w_rows], axis=0)
    rk_ref[...] = jnp.concatenate(r_rows, axis=0).astype(I32)
    cnt_ref[...] = carry_s[...].astype(I32)


def _mid(ret2, moba2, x2, mod3, wo, g_ffn, w_router, rbias, wsg, wsu, wsd, seq):
    n, d = x2.shape
    tm = TM_MID
    tiles_per_seq = seq // tm
    half = ret2.shape[1]
    ff = wsg.shape[1]
    const = lambda i: (0, 0)
    row = lambda i: (i, 0)
    colt = lambda i: (0, i)

    def resident(shape):
        return pl.BlockSpec(shape, const, pipeline_mode=pl.Buffered(1))

    return pl.pallas_call(
        _mid_kernel,
        out_shape=(
            jax.ShapeDtypeStruct((n, d), F32),
            jax.ShapeDtypeStruct((n, d // 2), I32),
            jax.ShapeDtypeStruct((TOP_K, n), I32),
            jax.ShapeDtypeStruct((TOP_K, n), F32),
            jax.ShapeDtypeStruct((TOP_K, n), I32),
            jax.ShapeDtypeStruct((N_EXPERTS, 1), I32),
        ),
        grid=(n // tm,),
        in_specs=[
            pl.BlockSpec((tm, half), row),
            pl.BlockSpec((tm, half), row),
            pl.BlockSpec((tm, d), row),
            pl.BlockSpec((1, N_MOD, d), lambda i: (i // tiles_per_seq, 0, 0)),
            resident(wo.shape),
            pl.BlockSpec((1, d), const),
            resident((d, N_EXPERTS)),
            pl.BlockSpec((N_EXPERTS, 1), const),
            resident((d, ff)),
            resident((d, ff)),
            resident((ff, d)),
        ],
        out_specs=(
            pl.BlockSpec((tm, d), row),
            pl.BlockSpec((tm, d // 2), row),
            pl.BlockSpec((TOP_K, tm), colt),
            pl.BlockSpec((TOP_K, tm), colt),
            pl.BlockSpec((TOP_K, tm), colt),
            pl.BlockSpec((N_EXPERTS, 1), const),
        ),
        scratch_shapes=[
            pltpu.VMEM((N_EXPERTS, 1), F32),
            pltpu.VMEM(wo.shape, BF16),
            pltpu.VMEM((N_EXPERTS, d), BF16),
            pltpu.VMEM((d, ff), BF16),
            pltpu.VMEM((d, ff), BF16),
            pltpu.VMEM((ff, d), BF16),
            pltpu.VMEM((tm, tm), BF16),
        ],
        compiler_params=pltpu.CompilerParams(
            dimension_semantics=("arbitrary",), vmem_limit_bytes=VMEM_LIMIT),
        name="mid",
    )(ret2, moba2, x2, mod3, wo, g_ffn, w_router, rbias, wsg, wsu, wsd)


ROW_EXP = 0
ROW_VALID = 1
ROW_FIRST = 2
ROW_AHEAD = 3
ROW_SLOT = 4
ROW_HEAD = 5
ROW_START = 6
SCHED_ROWS = 8


def _sched_kernel(cnt_ref, tab_ref):
    ne = N_EXPERTS
    nblk = tab_ref.shape[1]
    shift = FFN_BLOCK.bit_length() - 1
    e_sub = lax.broadcasted_iota(I32, (ne, ne), 0)
    e_lane = lax.broadcasted_iota(I32, (ne, ne), 1)
    e_col = lax.broadcasted_iota(I32, (ne, 1), 0).astype(F32)
    ids_row = lax.broadcasted_iota(I32, (1, ne), 1).astype(F32) + 1.0

    def to_row(col):
        return jnp.sum(jnp.where(e_sub == e_lane, col, 0.0), axis=0, keepdims=True)

    def running_col(row):
        return jnp.sum(jnp.where(e_lane <= e_sub, row, 0.0), axis=1, keepdims=True)

    def running_row(col):
        return jnp.sum(jnp.where(e_sub <= e_lane, col, 0.0), axis=0, keepdims=True)

    cnt_i = cnt_ref[...]
    cnt_col = cnt_i.astype(F32)
    pad_col = lax.shift_left(lax.shift_right_logical(cnt_i + (FFN_BLOCK - 1), shift), shift).astype(F32)
    pad_row = to_row(pad_col)
    ends_col = running_col(pad_row)
    start_col = ends_col - pad_col
    start_row = running_row(pad_col) - pad_row
    nreal = jnp.sum(pad_row, axis=1, keepdims=True) * (1.0 / FFN_BLOCK)

    step = lax.broadcasted_iota(I32, (1, nblk), 1)
    g = step.astype(F32)
    row0 = jnp.minimum(g, nreal - 1.0) * float(FFN_BLOCK)
    exp_g = jnp.sum(jnp.where(ends_col <= row0, 1.0, 0.0), axis=0, keepdims=True)
    mine = e_col == exp_g

    def per_block(col):
        return jnp.sum(jnp.where(mine, col, 0.0), axis=0, keepdims=True)

    valid = jnp.clip(per_block(cnt_col + start_col) - row0, 0.0, float(FFN_BLOCK))
    first = jnp.where(jnp.logical_and(g < nreal, per_block(start_col) == row0), 1.0, 0.0)

    used_col = jnp.where(cnt_col > 0.0, 1.0, 0.0)
    used_row = to_row(used_col)
    ord_col = running_col(used_row) - 1.0
    ord_row = running_row(used_col) - 1.0

    def used_at(pos):
        hit = jnp.logical_and(used_row > 0.0, ord_row == pos)
        return jnp.sum(jnp.where(hit, ids_row, 0.0), axis=1, keepdims=True) - 1.0

    ahead_col = used_at(ord_col + float(FFN_WEIGHT_SLOTS - 1))
    slot_col = ord_col - FFN_WEIGHT_SLOTS * jnp.floor((ord_col + 0.5) * (1.0 / FFN_WEIGHT_SLOTS))
    head = jnp.where(step == 0, nreal, 0.0)
    for j in range(FFN_WEIGHT_SLOTS - 1):
        head = head + jnp.where(step == 1 + j, used_at(jnp.full((1, 1), float(j), F32)), 0.0)
    rows = [exp_g, valid, first, per_block(ahead_col), per_block(slot_col), head,
            jnp.concatenate([start_row, jnp.zeros((1, nblk - ne), F32)], axis=1),
            jnp.zeros((1, nblk), F32)]
    tab_ref[...] = jnp.concatenate(rows, axis=0).astype(I32)


def _sched(cnt, nblk):
    assert FFN_BLOCK & (FFN_BLOCK - 1) == 0 and nblk >= N_EXPERTS
    return pl.pallas_call(
        _sched_kernel,
        out_shape=jax.ShapeDtypeStruct((SCHED_ROWS, nblk), I32),
        compiler_params=pltpu.CompilerParams(vmem_limit_bytes=VMEM_LIMIT),
        name="sched",
    )(cnt)


def _dest_kernel(tab_ref, e_ref, rk_ref, o_ref):
    k = e_ref.shape[0]
    segments = [jnp.broadcast_to(tab_ref[ROW_START:ROW_START + 1, s * LANES:(s + 1) * LANES], (k, LANES))
                for s in range(N_EXPERTS // LANES)]
    for ch in range(o_ref.shape[0]):
        cols = slice(ch * SC_CHUNK, (ch + 1) * SC_CHUNK)
        e = e_ref[:, cols]
        lane = e & (LANES - 1)
        start = jnp.take_along_axis(segments[0], lane, axis=1)
        for s in range(1, len(segments)):
            start = jnp.where(e >= s * LANES, jnp.take_along_axis(segments[s], lane, axis=1), start)
        o_ref[ch] = start + rk_ref[:, cols]


def _dest(tab, e_idx, rank):
    k, n = e_idx.shape
    tn = TN_DEST
    assert SC_CHUNK == LANES and N_EXPERTS % LANES == 0
    return pl.pallas_call(
        _dest_kernel,
        out_shape=jax.ShapeDtypeStruct((n // SC_CHUNK, k, SC_CHUNK), I32),
        grid=(n // tn,),
        in_specs=[
            pl.BlockSpec(tab.shape, lambda i: (0, 0)),
            pl.BlockSpec((k, tn), lambda i: (0, i)),
            pl.BlockSpec((k, tn), lambda i: (0, i)),
        ],
        out_specs=pl.BlockSpec((tn // SC_CHUNK, k, SC_CHUNK), lambda i: (i, 0, 0)),
        compiler_params=pltpu.CompilerParams(vmem_limit_bytes=VMEM_LIMIT),
        name="dest",
    )(tab, e_idx, rank)


def _sc_dispatch(h2p, dest3, total_rows):
    n, words = h2p.shape
    nchunks = n // SC_CHUNK
    per_worker = nchunks // (SC_CORES * SC_SUBCORES)
    mesh = plsc.VectorSubcoreMesh(core_axis_name="c", subcore_axis_name="s",
                                  num_cores=SC_CORES, num_subcores=SC_SUBCORES)

    @functools.partial(
        pl.kernel, mesh=mesh,
        out_type=jax.ShapeDtypeStruct((total_rows, words), I32),
        scratch_types=[
            pltpu.VMEM((TOP_K, SC_CHUNK), I32),
            pltpu.VMEM((SC_CHUNK, words), I32),
            pltpu.SemaphoreType.DMA,
        ],
        name="sc_dispatch",
    )
    def run(h_hbm, d_hbm, xs_hbm, idx_v, rows_v, sem):
        wid = lax.axis_index("s") * SC_CORES + lax.axis_index("c")

        @pl.loop(0, per_worker)
        def _(j):
            ch = wid * per_worker + j
            pltpu.sync_copy(d_hbm.at[ch], idx_v)
            pltpu.sync_copy(h_hbm.at[pl.ds(ch * SC_CHUNK, SC_CHUNK)], rows_v)
            copies = [pltpu.async_copy(rows_v, xs_hbm.at[idx_v.at[k]], sem) for k in range(TOP_K)]
            for cp in copies:
                cp.wait()

    return run(h2p, dest3)


def _ffn_kernel(tab_ref, x_hbm, wg_hbm, wu_hbm, wd_hbm, y_hbm,
                x_s, y_s, wg_s, wu_s, wd_s, sem_x, sem_y, sem):
    i = pl.program_id(0)
    nreal = tab_ref[ROW_HEAD, 0]
    rows_per = x_s.shape[1]
    half = x_s.shape[2]
    sizes = tuple(range(FFN_GRAIN, rows_per + 1, FFN_GRAIN))

    def rows_needed(g):
        return (tab_ref[ROW_VALID, g] + (FFN_GRAIN - 1)) // FFN_GRAIN * FFN_GRAIN

    def by_size(nrows, fn):
        for n in sizes:
            @pl.when(nrows == n)
            def _():
                fn(n)

    def row_copy(g, n):
        slot = lax.rem(g, FFN_LOOKAHEAD + 1)
        return pltpu.make_async_copy(
            x_hbm.at[pl.ds(g * rows_per, n)], x_s.at[slot, pl.ds(0, n)], sem_x.at[slot])

    def out_copy(g, n):
        slot = lax.rem(g, FFN_OUT_SLOTS)
        return pltpu.make_async_copy(
            y_s.at[slot, pl.ds(0, n)], y_hbm.at[pl.ds(g * rows_per, n)], sem_y.at[slot])

    def weight_copies(e, s):
        return (pltpu.make_async_copy(wg_hbm.at[e], wg_s.at[s], sem.at[s, 0]),
                pltpu.make_async_copy(wu_hbm.at[e], wu_s.at[s], sem.at[s, 1]),
                pltpu.make_async_copy(wd_hbm.at[e], wd_s.at[s], sem.at[s, 2]))

    @pl.when(i == 0)
    def _():
        for j in range(FFN_WEIGHT_SLOTS - 1):
            @pl.when(tab_ref[ROW_HEAD, 1 + j] >= 0)
            def _():
                for cp in weight_copies(tab_ref[ROW_HEAD, 1 + j], j):
                    cp.start(priority=FFN_WEIGHT_DMA_PRIORITY)
        for g in range(FFN_LOOKAHEAD):
            @pl.when(g < nreal)
            def _():
                by_size(rows_needed(g), lambda n: row_copy(g, n).start())

    @pl.when(i < nreal)
    def _():
        s = tab_ref[ROW_SLOT, i]
        fetch = i + FFN_LOOKAHEAD

        @pl.when(fetch < nreal)
        def _():
            by_size(rows_needed(fetch), lambda n: row_copy(fetch, n).start())

        by_size(rows_needed(i), lambda n: row_copy(i, n).wait())

        @pl.when(tab_ref[ROW_FIRST, i] == 1)
        def _():
            for cp in weight_copies(tab_ref[ROW_EXP, i], s):
                cp.wait()

            ahead = tab_ref[ROW_AHEAD, i]

            @pl.when(ahead >= 0)
            def _():
                for cp in weight_copies(ahead, lax.rem(s + FFN_WEIGHT_SLOTS - 1, FFN_WEIGHT_SLOTS)):
                    cp.start(priority=FFN_WEIGHT_DMA_PRIORITY)

        @pl.when(i >= FFN_OUT_SLOTS)
        def _():
            done = i - FFN_OUT_SLOTS
            by_size(rows_needed(done), lambda n: out_copy(done, n).wait())

        x_slot = lax.rem(i, FFN_LOOKAHEAD + 1)
        y_slot = lax.rem(i, FFN_OUT_SLOTS)
        valid = tab_ref[ROW_VALID, i]

        def expert_rows(n):
            r = lax.broadcasted_iota(I32, (n, 1), 0)
            x_lo, x_hi = _unpack_halves(jnp.where(r < valid, x_s[x_slot, pl.ds(0, n), :], 0))
            hg = _dot(x_lo, wg_s[s, :half, :]) + _dot(x_hi, wg_s[s, half:, :])
            hu = _dot(x_lo, wu_s[s, :half, :]) + _dot(x_hi, wu_s[s, half:, :])
            y_s[y_slot, pl.ds(0, n), :] = _pack_halves(_dot(_silu(hg) * hu, wd_s[s]))
            out_copy(i, n).start()

        by_size(rows_needed(i), expert_rows)

        @pl.when(i == nreal - 1)
        def _():
            for back in range(FFN_OUT_SLOTS):
                last = i - back

                @pl.when(last >= 0)
                def _():
                    by_size(rows_needed(last), lambda n: out_copy(last, n).wait())


def _ffn(tab, xs, w_gate, w_up, w_down):
    p, half = xs.shape
    d = 2 * half
    ff = w_gate.shape[2]
    return pl.pallas_call(
        _ffn_kernel,
        out_shape=jax.ShapeDtypeStruct((p, half), I32),
        grid_spec=pltpu.PrefetchScalarGridSpec(
            num_scalar_prefetch=1,
            grid=(tab.shape[1],),
            in_specs=[
                pl.BlockSpec(memory_space=pl.ANY),
                pl.BlockSpec(memory_space=pl.ANY),
                pl.BlockSpec(memory_space=pl.ANY),
                pl.BlockSpec(memory_space=pl.ANY),
            ],
            out_specs=pl.BlockSpec(memory_space=pl.ANY),
            scratch_shapes=[
                pltpu.VMEM((FFN_LOOKAHEAD + 1, FFN_BLOCK, half), I32),
                pltpu.VMEM((FFN_OUT_SLOTS, FFN_BLOCK, half), I32),
                pltpu.VMEM((FFN_WEIGHT_SLOTS, d, ff), F32),
                pltpu.VMEM((FFN_WEIGHT_SLOTS, d, ff), F32),
                pltpu.VMEM((FFN_WEIGHT_SLOTS, ff, d), F32),
                pltpu.SemaphoreType.DMA((FFN_LOOKAHEAD + 1,)),
                pltpu.SemaphoreType.DMA((FFN_OUT_SLOTS,)),
                pltpu.SemaphoreType.DMA((FFN_WEIGHT_SLOTS, 3)),
            ],
        ),
        compiler_params=pltpu.CompilerParams(
            dimension_semantics=("arbitrary",), vmem_limit_bytes=VMEM_LIMIT, has_side_effects=True),
        name="ffn",
    )(tab, xs, w_gate, w_up, w_down)


def _sc_gather(y, dest3):
    a, words = y.shape
    nchunks, _, chunk = dest3.shape
    n = nchunks * chunk
    per_worker = nchunks // (SC_CORES * SC_SUBCORES)
    nbuf = SC_GATHER_BUFS
    parts = chunk // SC_GATHER_ROWS
    items = [(c, k, h) for c in range(per_worker) for k in range(TOP_K) for h in range(parts)]
    mesh = plsc.VectorSubcoreMesh(core_axis_name="c", subcore_axis_name="s",
                                  num_cores=SC_CORES, num_subcores=SC_SUBCORES)

    @functools.partial(
        pl.kernel, mesh=mesh,
        out_type=jax.ShapeDtypeStruct((TOP_K, n, words), I32),
        scratch_types=[
            pltpu.VMEM((per_worker, TOP_K, chunk), I32),
            [pltpu.VMEM((SC_GATHER_ROWS, words), I32)] * nbuf,
            pltpu.SemaphoreType.DMA((nbuf,)),
            pltpu.SemaphoreType.DMA((nbuf,)),
        ],
        name="sc_gather",
    )
    def run(y_hbm, d_hbm, yt_hbm, idx_v, bufs, sem_g, sem_w):
        wid = lax.axis_index("s") * SC_CORES + lax.axis_index("c")
        pltpu.sync_copy(d_hbm.at[pl.ds(wid * per_worker, per_worker)], idx_v)

        def gather(m):
            c, k, h = items[m]
            idx = idx_v.at[c, k, pl.ds(h * SC_GATHER_ROWS, SC_GATHER_ROWS)]
            return pltpu.async_copy(y_hbm.at[idx], bufs[m % nbuf], sem_g.at[m % nbuf])

        def write(m):
            c, k, h = items[m]
            rows = pl.ds((wid * per_worker + c) * chunk + h * SC_GATHER_ROWS, SC_GATHER_ROWS)
            return pltpu.async_copy(bufs[m % nbuf], yt_hbm.at[k, rows], sem_w.at[m % nbuf])

        gathers = {m: gather(m) for m in range(min(nbuf - 1, len(items)))}
        writes = {}
        for m in range(len(items)):
            gathers.pop(m).wait()
            writes[m] = write(m)
            nxt = m + nbuf - 1
            if nxt < len(items):
                if m >= 1:
                    writes.pop(m - 1).wait()
                gathers[nxt] = gather(nxt)
        for m in sorted(writes):
            writes.pop(m).wait()

    return run(y, dest3)


def _combine_kernel(yt_ref, wt_ref, xb_ref, mod_ref, o_ref):
    half = yt_ref.shape[2]
    wt = wt_ref[...].T
    lo, hi = _unpack_halves(yt_ref[0])
    r_lo = lo * wt[:, 0:1]
    r_hi = hi * wt[:, 0:1]
    for k in range(1, TOP_K):
        lo, hi = _unpack_halves(yt_ref[k])
        r_lo = r_lo + lo * wt[:, k:k + 1]
        r_hi = r_hi + hi * wt[:, k:k + 1]
    gate = mod_ref[0][5:6]
    o_ref[:, :half] = xb_ref[:, :half] + gate[:, :half] * r_lo
    o_ref[:, half:] = xb_ref[:, half:] + gate[:, half:] * r_hi


def _combine(yt, w_k, xb, mod3, seq):
    n, d = xb.shape
    tm = TM_COMBINE
    tiles_per_seq = seq // tm
    return pl.pallas_call(
        _combine_kernel,
        out_shape=jax.ShapeDtypeStruct((n, d), F32),
        grid=(n // tm,),
        in_specs=[
            pl.BlockSpec((TOP_K, tm, d // 2), lambda i: (0, i, 0)),
            pl.BlockSpec((TOP_K, tm), lambda i: (0, i)),
            pl.BlockSpec((tm, d), lambda i: (i, 0)),
            pl.BlockSpec((1, N_MOD, d), lambda i: (i // tiles_per_seq, 0, 0)),
        ],
        out_specs=pl.BlockSpec((tm, d), lambda i: (i, 0)),
        compiler_params=pltpu.CompilerParams(vmem_limit_bytes=VMEM_LIMIT),
        name="combine",
    )(yt, w_k, xb, mod3)


def _rotary_tables(seq):
    half = RET_DK // 2
    inv = ROPE_BASE ** (-np.arange(half, dtype=np.float64) / half)
    ang = np.arange(seq, dtype=np.float64)[:, None] * inv[None, :]
    cos = np.cos(ang).astype(np.float32)
    sin = np.sin(ang).astype(np.float32)
    return (jnp.asarray(np.concatenate([cos, cos], axis=-1)),
            jnp.asarray(np.concatenate([-sin, sin], axis=-1)))


def kernel(x, c, w_ada, b_ada, g_mix, w_in, q_gain, k_gain, w_out, g_ffn, w_router, router_bias,
           w_gate, w_up, w_down, ws_gate, ws_up, ws_down):
    bsz, seq, d = x.shape
    n = bsz * seq
    depth = w_ada.shape[0]
    cos_full, sin_signed = _rotary_tables(seq)
    log_g = jnp.asarray(np.log1p(-np.exp2(-5.0 - np.arange(RET_HEADS, dtype=np.float64))).astype(np.float32))
    ret_w = RET_HEADS * RET_DK
    x2 = x.reshape(n, d)
    for l in range(depth):
        mod3 = _adaln(c, w_ada[l], b_ada[l]).reshape(bsz, N_MOD, d)
        proj = _inproj(x2, mod3, g_mix[l].reshape(1, d), w_in[l], cos_full, sin_signed, seq)
        proj3 = proj.reshape(bsz, seq, IN_COLS)
        qg2 = jnp.tile(q_gain[l].reshape(1, MOBA_DH), (1, 2))
        kg2 = jnp.tile(k_gain[l].reshape(1, MOBA_DH), (1, 2))
        ret, moba = _mixers(log_g, proj3, qg2, kg2)
        xb, h2, e_idx, w_k, rank, cnt = _mid(
            ret.reshape(n, ret_w), moba.reshape(n, MOBA_HEADS * MOBA_DH), x2, mod3,
            w_out[l], g_ffn[l].reshape(1, d),
            w_router[l], router_bias[l].reshape(N_EXPERTS, 1),
            ws_gate[l], ws_up[l], ws_down[l], seq)
        nblk = n * TOP_K // FFN_BLOCK + N_EXPERTS
        tab = _sched(cnt, nblk)
        dest3 = _dest(tab, e_idx, rank)
        xs = _sc_dispatch(h2, dest3, nblk * FFN_BLOCK)
        y = _ffn(tab, xs, w_gate[l], w_up[l], w_down[l])
        yt = _sc_gather(y, dest3)
        x2 = _combine(yt, w_k, xb, mod3, seq)
    return x2.reshape(bsz, seq, d)
```

```python
import functools

import numpy as np
import jax
import jax.numpy as jnp
from jax import lax
from jax.experimental import pallas as pl
from jax.experimental.pallas import tpu as pltpu
from jax.experimental.pallas import tpu_sc as plsc

F32 = jnp.float32
BF16 = jnp.bfloat16
I32 = jnp.int32

RET_HEADS = 4
RET_DK = 128
MOBA_HEADS = 8
MOBA_DH = 64
MOBA_BLOCK = 256
MOBA_TOPK = 3
ROPE_BASE = 10000.0
N_EXPERTS = 256
TOP_K = 8
N_GROUPS = 8
TOPK_GROUPS = 4
GROUP_SIZE = N_EXPERTS // N_GROUPS
ROUTED_SCALE = 2.5
N_MOD = 6
EPS = 1e-6
IN_COLS = 3584

LANES = 128
BF16_TILE_ROWS = 16
RET_CHUNK = 256
TN_ADALN = 1024
TM_INPROJ = 1024
TM_MID = 512
TN_DEST = 2048
TM_COMBINE = 512
MOBA_ONES_ROWS = BF16_TILE_ROWS
SC_CORES = 2
SC_SUBCORES = 16
SC_CHUNK = 128
SC_GATHER_ROWS = 64
SC_GATHER_BUFS = 3
FFN_BLOCK = 1024
FFN_GRAIN = 128
FFN_LOOKAHEAD = 5
FFN_WEIGHT_SLOTS = 3
FFN_OUT_SLOTS = 3
VMEM_LIMIT = 56 * 1024 * 1024

NEG_INF = float("-inf")
LOG2_E = 1.4426950408889634


def _silu(x):
    return x * jax.nn.sigmoid(x)


def _nt_dot(a, b):
    return lax.dot_general(a, b, (((1,), (1,)), ((), ())), preferred_element_type=F32)


def _tn_dot(a, b):
    return lax.dot_general(a, b, (((0,), (0,)), ((), ())), preferred_element_type=F32)


def _dot(a, b):
    return jnp.dot(a, b, preferred_element_type=F32)


HI_MASK = -65536


def _pack_halves(v):
    w = v.shape[1] // 2
    lo = lax.bitcast_convert_type(v[:, :w].astype(BF16).astype(F32), I32)
    hi = lax.bitcast_convert_type(v[:, w:].astype(BF16).astype(F32), I32)
    return lax.shift_right_logical(lo, 16) | (hi & HI_MASK)


def _unpack_halves(u):
    lo = lax.bitcast_convert_type(lax.shift_left(u, 16), F32)
    hi = lax.bitcast_convert_type(u & HI_MASK, F32)
    return lo, hi


def _adaln_kernel(c_ref, w_ref, b_ref, o_ref):
    s = _silu(c_ref[...])
    o_ref[...] = _dot(s.astype(BF16), w_ref[...].astype(BF16)) + b_ref[...]


def _adaln(c, w_ada, b_ada):
    bsz, d = c.shape
    ncol = w_ada.shape[1]
    tn = TN_ADALN
    return pl.pallas_call(
        _adaln_kernel,
        out_shape=jax.ShapeDtypeStruct((bsz, ncol), F32),
        grid=(ncol // tn,),
        in_specs=[
            pl.BlockSpec((bsz, d), lambda j: (0, 0)),
            pl.BlockSpec((d, tn), lambda j: (0, j)),
            pl.BlockSpec((1, tn), lambda j: (0, j)),
        ],
        out_specs=pl.BlockSpec((bsz, tn), lambda j: (0, j)),
        compiler_params=pltpu.CompilerParams(vmem_limit_bytes=VMEM_LIMIT),
        name="adaln",
    )(c, w_ada, b_ada.reshape(1, ncol))


def _inproj_kernel(x_ref, mod_ref, g_ref, w_ref, cos_ref, sin_ref, o_ref):
    x = x_ref[...]
    ms = jnp.mean(x * x, axis=-1, keepdims=True)
    m = mod_ref[0]
    h = (x * lax.rsqrt(ms + EPS) * g_ref[...]) * (1.0 + m[1:2]) + m[0:1]
    hb = h.astype(BF16)
    cosf = cos_ref[...]
    sinf = sin_ref[...]
    k_scale = RET_DK ** -0.5
    width = RET_HEADS * RET_DK
    for ci in range(IN_COLS // width):
        acc = _dot(hb, w_ref[:, ci * width:(ci + 1) * width])
        if ci < 2:
            for hh in range(RET_HEADS):
                xh = acc[:, hh * RET_DK:(hh + 1) * RET_DK]
                r = xh * cosf + pltpu.roll(xh, RET_DK // 2, axis=1) * sinf
                if ci == 1:
                    r = r * k_scale
                o_ref[:, ci * width + hh * RET_DK:ci * width + (hh + 1) * RET_DK] = r.astype(BF16)
        else:
            o_ref[:, ci * width:(ci + 1) * width] = acc.astype(BF16)


def _inproj(x2, mod3, g_mix, w_in, cos_full, sin_signed, seq):
    n, d = x2.shape
    tm = TM_INPROJ
    tiles_per_seq = seq // tm
    return pl.pallas_call(
        _inproj_kernel,
        out_shape=jax.ShapeDtypeStruct((n, IN_COLS), BF16),
        grid=(n // tm,),
        in_specs=[
            pl.BlockSpec((tm, d), lambda i: (i, 0)),
            pl.BlockSpec((1, N_MOD, d), lambda i: (i // tiles_per_seq, 0, 0)),
            pl.BlockSpec((1, d), lambda i: (0, 0)),
            pl.BlockSpec((d, IN_COLS), lambda i: (0, 0), pipeline_mode=pl.Buffered(1)),
            pl.BlockSpec((tm, LANES), lambda i: (i % tiles_per_seq, 0)),
            pl.BlockSpec((tm, LANES), lambda i: (i % tiles_per_seq, 0)),
        ],
        out_specs=pl.BlockSpec((tm, IN_COLS), lambda i: (i, 0)),
        compiler_params=pltpu.CompilerParams(vmem_limit_bytes=VMEM_LIMIT),
        name="inproj",
    )(x2, mod3, g_mix, w_in, cos_full, sin_signed)


def _ret_kernel(lg_ref, q_ref, k_ref, v_ref, g_ref, o_ref):
    seq = q_ref.shape[1]
    c = RET_CHUNK
    lg = lg_ref[pl.program_id(1)]
    row = lax.broadcasted_iota(I32, (c, c), 0)
    col = lax.broadcasted_iota(I32, (c, c), 1)
    diff = (row - col).astype(F32)
    dmask = jnp.where(diff >= 0, jnp.exp(lg * jnp.maximum(diff, 0.0)), 0.0)
    idx = lax.broadcasted_iota(I32, (c, 1), 0).astype(F32)
    q_decay = jnp.exp(lg * (idx + 1.0))
    k_decay = jnp.exp(lg * (c - 1.0 - idx))
    chunk_decay = jnp.exp(jnp.full((1, 1), lg * c, F32))
    state = jnp.zeros((RET_DK, RET_DK), F32)
    for n in range(seq // c):
        rows = slice(n * c, (n + 1) * c)
        qn = q_ref[0, rows, :]
        kn = k_ref[0, rows, :]
        vn = v_ref[0, rows, :]
        scores = _nt_dot(qn, kn) * dmask
        inner = _dot(scores.astype(BF16), vn)
        qs = (qn.astype(F32) * q_decay).astype(BF16)
        cross = _dot(qs, state.astype(BF16))
        o = inner + cross
        o = o * lax.rsqrt(jnp.mean(o * o, axis=-1, keepdims=True) + EPS)
        gn = g_ref[0, rows, :].astype(F32)
        o_ref[0, rows, :] = (_silu(gn) * o).astype(BF16)
        ks = (kn.astype(F32) * k_decay).astype(BF16)
        state = state * chunk_decay + _tn_dot(ks, vn)


def _moba_kernel(q_ref, k_ref, v_ref, qg_ref, kg_ref, o_ref, qt_s, ka_s, kb_s, vta_s, vtb_s):
    seq = q_ref.shape[1]
    lb = MOBA_BLOCK
    nb = seq // lb
    lane = lax.broadcasted_iota(I32, (1, LANES), 1)
    is_a = lane < MOBA_DH

    def head_norm(xf, gain):
        sq = xf * xf
        s_a = jnp.sum(jnp.where(is_a, sq, 0.0), axis=-1, keepdims=True)
        s_b = jnp.sum(jnp.where(is_a, 0.0, sq), axis=-1, keepdims=True)
        inv = jnp.where(is_a, lax.rsqrt(s_a / MOBA_DH + EPS), lax.rsqrt(s_b / MOBA_DH + EPS))
        return xf * inv * gain

    qg = qg_ref[...]
    kg = kg_ref[...]
    k_means = []
    for j in range(nb):
        rows = slice(j * lb, (j + 1) * lb)
        kf = head_norm(k_ref[0, rows, :].astype(F32), kg)
        ka_s[rows, :] = jnp.where(is_a, kf, 0.0).astype(BF16)
        kb_s[rows, :] = jnp.where(is_a, 0.0, kf).astype(BF16)
        k_means.append(jnp.mean(kf, axis=0, keepdims=True))
        qf = head_norm(q_ref[0, rows, :].astype(F32), qg)
        qt_s[:, rows] = (qf * (MOBA_DH ** -0.5 * LOG2_E)).T.astype(BF16)
        vt = v_ref[0, rows, :].T
        ones = jnp.ones((MOBA_ONES_ROWS, lb), BF16)
        vta_s[:, rows] = jnp.concatenate([vt[:MOBA_DH], ones], axis=0)
        vtb_s[:, rows] = jnp.concatenate([vt[MOBA_DH:], ones], axis=0)
    k_mean = jnp.concatenate(k_means + [jnp.zeros((BF16_TILE_ROWS - nb, LANES), F32)], axis=0)
    k_mean_h = (jnp.where(is_a, k_mean, 0.0).astype(BF16), jnp.where(is_a, 0.0, k_mean).astype(BF16))
    k_s = (ka_s, kb_s)
    vt_s = (vta_s, vtb_s)

    r_loc = lax.broadcasted_iota(I32, (lb, lb), 0)
    c_loc = lax.broadcasted_iota(I32, (lb, lb), 1)
    causal = r_loc <= c_loc

    def logits(i, hx):
        qt = qt_s[:, i * lb:(i + 1) * lb]
        bias = [None] * i
        if i > MOBA_TOPK:
            gate = _dot(k_mean_h[hx], qt)
            g = [gate[j:j + 1, :] for j in range(i)]
            for j in range(i):
                rank = jnp.zeros((1, lb), F32)
                for j2 in range(i):
                    if j2 == j:
                        continue
                    beats = (g[j2] >= g[j]) if j2 < j else (g[j2] > g[j])
                    rank = rank + jnp.where(beats, 1.0, 0.0)
                bias[j] = jnp.where(rank < float(MOBA_TOPK), 0.0, NEG_INF).astype(BF16)
        pieces = []
        for j in range(i + 1):
            s = _dot(k_s[hx][j * lb:(j + 1) * lb, :], qt).astype(BF16)
            if j == i:
                s = jnp.where(causal, s, jnp.asarray(NEG_INF, BF16))
            elif bias[j] is not None:
                s = s + bias[j]
            pieces.append(s)
        mx = jnp.max(pieces[0], axis=0, keepdims=True)
        for s in pieces[1:]:
            mx = jnp.maximum(mx, jnp.max(s, axis=0, keepdims=True))
        return pieces, mx

    def attend(hx, pieces, mx):
        acc = jnp.zeros((MOBA_DH + MOBA_ONES_ROWS, lb), F32)
        for j, s in enumerate(pieces):
            acc = acc + _dot(vt_s[hx][:, j * lb:(j + 1) * lb], jnp.exp2(s - mx))
        return acc[:MOBA_DH] / acc[MOBA_DH:MOBA_DH + 1, :]

    groups = [(i, hx) for i in range(nb) for hx in range(2)]
    outs = {}
    pending = logits(*groups[0])
    for n, (i, hx) in enumerate(groups):
        ahead = logits(*groups[n + 1]) if n + 1 < len(groups) else None
        outs[(i, hx)] = attend(hx, *pending)
        pending = ahead
        if hx == 1:
            o_ref[0, i * lb:(i + 1) * lb, :] = jnp.concatenate(
                [outs.pop((i, 0)), outs.pop((i, 1))], axis=0).T.astype(BF16)


def _mixers_kernel(lg_ref, rq_ref, rk_ref, rv_ref, rg_ref, mq_ref, mk_ref, mv_ref, qg_ref, kg_ref,
                   ret_ref, moba_ref, *moba_scratch):
    _ret_kernel(lg_ref, rq_ref, rk_ref, rv_ref, rg_ref, ret_ref)
    _moba_kernel(mq_ref, mk_ref, mv_ref, qg_ref, kg_ref, moba_ref, *moba_scratch)


def _mixers(log_g, proj3, qg2, kg2):
    bsz, seq, _ = proj3.shape
    pairs = MOBA_HEADS // 2
    assert pairs == RET_HEADS and RET_DK == LANES
    blk = (1, seq, LANES)
    base = 4 * RET_HEADS

    def col(first):
        return pl.BlockSpec(blk, lambda b, p, lg: (b, 0, first + p))

    return pl.pallas_call(
        _mixers_kernel,
        out_shape=(jax.ShapeDtypeStruct((bsz, seq, RET_HEADS * RET_DK), BF16),
                   jax.ShapeDtypeStruct((bsz, seq, MOBA_HEADS * MOBA_DH), BF16)),
        grid_spec=pltpu.PrefetchScalarGridSpec(
            num_scalar_prefetch=1,
            grid=(bsz, pairs),
            in_specs=[
                col(0), col(RET_HEADS), col(2 * RET_HEADS), col(3 * RET_HEADS),
                col(base), col(base + pairs), col(base + 2 * pairs),
                pl.BlockSpec((1, LANES), lambda b, p, lg: (0, 0)),
                pl.BlockSpec((1, LANES), lambda b, p, lg: (0, 0)),
            ],
            out_specs=(col(0), col(0)),
            scratch_shapes=[
                pltpu.VMEM((LANES, seq), BF16),
                pltpu.VMEM((seq, LANES), BF16),
                pltpu.VMEM((seq, LANES), BF16),
                pltpu.VMEM((MOBA_DH + MOBA_ONES_ROWS, seq), BF16),
                pltpu.VMEM((MOBA_DH + MOBA_ONES_ROWS, seq), BF16),
            ],
        ),
        compiler_params=pltpu.CompilerParams(vmem_limit_bytes=VMEM_LIMIT),
        name="mixers",
    )(log_g, *([proj3] * 7), qg2, kg2)


def _mid_kernel(ret_ref, moba_ref, x_ref, mod_ref, wo_ref, g_ref, wr_ref, rb_ref,
                wsg_ref, wsu_ref, wsd_ref,
                xb_ref, h2_ref, e_ref, w_ref, rk_ref, cnt_ref,
                carry_s, wo_s, wr_s, wsg_s, wsu_s, wsd_s, upper_s):
    i = pl.program_id(0)
    tm = x_ref.shape[0]
    half = ret_ref.shape[1]

    @pl.when(i == 0)
    def _():
        carry_s[...] = jnp.zeros_like(carry_s)
        wo_s[...] = wo_ref[...].astype(BF16)
        wsg_s[...] = wsg_ref[...].astype(BF16)
        wsu_s[...] = wsu_ref[...].astype(BF16)
        wsd_s[...] = wsd_ref[...].astype(BF16)
        wr_s[...] = wr_ref[...].T.astype(BF16)
        tr = lax.broadcasted_iota(I32, (tm, tm), 0)
        tc = lax.broadcasted_iota(I32, (tm, tm), 1)
        upper_s[...] = jnp.where(tr < tc, 1.0, 0.0).astype(BF16)

    m = mod_ref[0]
    mixed = _dot(ret_ref[...], wo_s[:half, :]) + _dot(moba_ref[...], wo_s[half:, :])
    x1 = x_ref[...] + m[2:3] * mixed
    ms = jnp.mean(x1 * x1, axis=-1, keepdims=True)
    h2 = (x1 * lax.rsqrt(ms + EPS) * g_ref[...]) * (1.0 + m[4:5]) + m[3:4]
    h2_ref[...] = _pack_halves(h2)
    h2b = h2.astype(BF16)

    hid = _silu(_dot(h2b, wsg_s[...])) * _dot(h2b, wsu_s[...])
    xb_ref[...] = x1 + m[5:6] * _dot(hid.astype(BF16), wsd_s[...])

    scores = jax.nn.sigmoid(_nt_dot(wr_s[...], h2b))
    biased = scores + rb_ref[...]
    grp = biased.reshape(N_GROUPS, GROUP_SIZE, tm)
    gi = lax.broadcasted_iota(I32, (N_GROUPS, GROUP_SIZE, tm), 1).astype(F32)
    top1 = jnp.max(grp, axis=1, keepdims=True)
    first = jnp.min(jnp.where(grp == top1, gi, float(GROUP_SIZE)), axis=1, keepdims=True)
    top2 = jnp.max(jnp.where(gi == first, NEG_INF, grp), axis=1, keepdims=True)
    gscore = (top1 + top2).reshape(N_GROUPS, tm)
    gidx = lax.broadcasted_iota(I32, (N_GROUPS, tm), 0)
    grank = jnp.zeros((N_GROUPS, tm), F32)
    for g2 in range(N_GROUPS):
        rowv = gscore[g2:g2 + 1, :]
        beats = (rowv > gscore) | ((rowv == gscore) & (g2 < gidx))
        grank = grank + jnp.where(beats, 1.0, 0.0)
    gsel = jnp.where(grank < float(TOPK_GROUPS), 1.0, 0.0)
    emask = jnp.broadcast_to(gsel.reshape(N_GROUPS, 1, tm), (N_GROUPS, GROUP_SIZE, tm)).reshape(N_EXPERTS, tm)
    choice = jnp.where(emask > 0.5, biased, NEG_INF)

    eidx = lax.broadcasted_iota(I32, (N_EXPERTS, tm), 0).astype(F32)
    remaining = choice
    e_rows = []
    for _k in range(TOP_K):
        mx = jnp.max(remaining, axis=0, keepdims=True)
        idx = jnp.min(jnp.where(remaining == mx, eidx, float(N_EXPERTS)), axis=0, keepdims=True)
        e_rows.append(idx)
        remaining = jnp.where(eidx == idx, NEG_INF, remaining)
    selmask = jnp.where(remaining != choice, 1.0, 0.0)

    prefix = _dot(selmask.astype(BF16), upper_s[...]) + carry_s[...]
    w_rows = []
    r_rows = []
    for ek in e_rows:
        onehot = eidx == ek
        w_rows.append(jnp.sum(jnp.where(onehot, scores, 0.0), axis=0, keepdims=True))
        r_rows.append(jnp.sum(jnp.where(onehot, prefix, 0.0), axis=0, keepdims=True))
    wsum = w_rows[0]
    for wk in w_rows[1:]:
        wsum = wsum + wk
    carry_s[...] = carry_s[...] + jnp.sum(selmask, axis=1, keepdims=True)

    e_ref[...] = jnp.concatenate(e_rows, axis=0).astype(I32)
    w_ref[...] = jnp.concatenate([wk / wsum * ROUTED_SCALE for wk in w_rows], axis=0)
    rk_ref[...] = jnp.concatenate(r_rows, axis=0).astype(I32)
    cnt_ref[...] = carry_s[...].astype(I32)


def _mid(ret2, moba2, x2, mod3, wo, g_ffn, w_router, rbias, wsg, wsu, wsd, seq):
    n, d = x2.shape
    tm = TM_MID
    tiles_per_seq = seq // tm
    half = ret2.shape[1]
    ff = wsg.shape[1]
    const = lambda i: (0, 0)
    row = lambda i: (i, 0)
    colt = lambda i: (0, i)

    def resident(shape):
        return pl.BlockSpec(shape, const, pipeline_mode=pl.Buffered(1))

    return pl.pallas_call(
        _mid_kernel,
        out_shape=(
            jax.ShapeDtypeStruct((n, d), F32),
            jax.ShapeDtypeStruct((n, d // 2), I32),
            jax.ShapeDtypeStruct((TOP_K, n), I32),
            jax.ShapeDtypeStruct((TOP_K, n), F32),
            jax.ShapeDtypeStruct((TOP_K, n), I32),
            jax.ShapeDtypeStruct((N_EXPERTS, 1), I32),
        ),
        grid=(n // tm,),
        in_specs=[
            pl.BlockSpec((tm, half), row),
            pl.BlockSpec((tm, half), row),
            pl.BlockSpec((tm, d), row),
            pl.BlockSpec((1, N_MOD, d), lambda i: (i // tiles_per_seq, 0, 0)),
            resident(wo.shape),
            pl.BlockSpec((1, d), const),
            resident((d, N_EXPERTS)),
            pl.BlockSpec((N_EXPERTS, 1), const),
            resident((d, ff)),
            resident((d, ff)),
            resident((ff, d)),
        ],
        out_specs=(
            pl.BlockSpec((tm, d), row),
            pl.BlockSpec((tm, d // 2), row),
            pl.BlockSpec((TOP_K, tm), colt),
            pl.BlockSpec((TOP_K, tm), colt),
            pl.BlockSpec((TOP_K, tm), colt),
            pl.BlockSpec((N_EXPERTS, 1), const),
        ),
        scratch_shapes=[
            pltpu.VMEM((N_EXPERTS, 1), F32),
            pltpu.VMEM(wo.shape, BF16),
            pltpu.VMEM((N_EXPERTS, d), BF16),
            pltpu.VMEM((d, ff), BF16),
            pltpu.VMEM((d, ff), BF16),
            pltpu.VMEM((ff, d), BF16),
            pltpu.VMEM((tm, tm), BF16),
        ],
        compiler_params=pltpu.CompilerParams(
            dimension_semantics=("arbitrary",), vmem_limit_bytes=VMEM_LIMIT),
        name="mid",
    )(ret2, moba2, x2, mod3, wo, g_ffn, w_router, rbias, wsg, wsu, wsd)


ROW_EXP = 0
ROW_VALID = 1
ROW_FIRST = 2
ROW_AHEAD = 3
ROW_SLOT = 4
ROW_HEAD = 5
ROW_START = 6
SCHED_ROWS = 8


def _sched_kernel(cnt_ref, tab_ref):
    ne = N_EXPERTS
    nblk = tab_ref.shape[1]
    shift = FFN_BLOCK.bit_length() - 1
    e_sub = lax.broadcasted_iota(I32, (ne, ne), 0)
    e_lane = lax.broadcasted_iota(I32, (ne, ne), 1)
    e_col = lax.broadcasted_iota(I32, (ne, 1), 0).astype(F32)
    ids_row = lax.broadcasted_iota(I32, (1, ne), 1).astype(F32) + 1.0

    def to_row(col):
        return jnp.sum(jnp.where(e_sub == e_lane, col, 0.0), axis=0, keepdims=True)

    def running_col(row):
        return jnp.sum(jnp.where(e_lane <= e_sub, row, 0.0), axis=1, keepdims=True)

    def running_row(col):
        return jnp.sum(jnp.where(e_sub <= e_lane, col, 0.0), axis=0, keepdims=True)

    cnt_i = cnt_ref[...]
    cnt_col = cnt_i.astype(F32)
    pad_col = lax.shift_left(lax.shift_right_logical(cnt_i + (FFN_BLOCK - 1), shift), shift).astype(F32)
    pad_row = to_row(pad_col)
    ends_col = running_col(pad_row)
    start_col = ends_col - pad_col
    start_row = running_row(pad_col) - pad_row
    nreal = jnp.sum(pad_row, axis=1, keepdims=True) * (1.0 / FFN_BLOCK)

    step = lax.broadcasted_iota(I32, (1, nblk), 1)
    g = step.astype(F32)
    row0 = jnp.minimum(g, nreal - 1.0) * float(FFN_BLOCK)
    exp_g = jnp.sum(jnp.where(ends_col <= row0, 1.0, 0.0), axis=0, keepdims=True)
    mine = e_col == exp_g

    def per_block(col):
        return jnp.sum(jnp.where(mine, col, 0.0), axis=0, keepdims=True)

    valid = jnp.clip(per_block(cnt_col + start_col) - row0, 0.0, float(FFN_BLOCK))
    first = jnp.where(jnp.logical_and(g < nreal, per_block(start_col) == row0), 1.0, 0.0)

    used_col = jnp.where(cnt_col > 0.0, 1.0, 0.0)
    used_row = to_row(used_col)
    ord_col = running_col(used_row) - 1.0
    ord_row = running_row(used_col) - 1.0

    def used_at(pos):
        hit = jnp.logical_and(used_row > 0.0, ord_row == pos)
        return jnp.sum(jnp.where(hit, ids_row, 0.0), axis=1, keepdims=True) - 1.0

    ahead_col = used_at(ord_col + float(FFN_WEIGHT_SLOTS - 1))
    slot_col = ord_col - FFN_WEIGHT_SLOTS * jnp.floor((ord_col + 0.5) * (1.0 / FFN_WEIGHT_SLOTS))
    head = jnp.where(step == 0, nreal, 0.0)
    for j in range(FFN_WEIGHT_SLOTS - 1):
        head = head + jnp.where(step == 1 + j, used_at(jnp.full((1, 1), float(j), F32)), 0.0)
    rows = [exp_g, valid, first, per_block(ahead_col), per_block(slot_col), head,
            jnp.concatenate([start_row, jnp.zeros((1, nblk - ne), F32)], axis=1),
            jnp.zeros((1, nblk), F32)]
    tab_ref[...] = jnp.concatenate(rows, axis=0).astype(I32)


def _sched(cnt, nblk):
    assert FFN_BLOCK & (FFN_BLOCK - 1) == 0 and nblk >= N_EXPERTS
    return pl.pallas_call(
        _sched_kernel,
        out_shape=jax.ShapeDtypeStruct((SCHED_ROWS, nblk), I32),
        compiler_params=pltpu.CompilerParams(vmem_limit_bytes=VMEM_LIMIT),
        name="sched",
    )(cnt)


def _dest_kernel(tab_ref, e_ref, rk_ref, o_ref):
    k = e_ref.shape[0]
    segments = [jnp.broadcast_to(tab_ref[ROW_START:ROW_START + 1, s * LANES:(s + 1) * LANES], (k, LANES))
                for s in range(N_EXPERTS // LANES)]
    for ch in range(o_ref.shape[0]):
        cols = slice(ch * SC_CHUNK, (ch + 1) * SC_CHUNK)
        e = e_ref[:, cols]
        lane = e & (LANES - 1)
        start = jnp.take_along_axis(segments[0], lane, axis=1)
        for s in range(1, len(segments)):
            start = jnp.where(e >= s * LANES, jnp.take_along_axis(segments[s], lane, axis=1), start)
        o_ref[ch] = start + rk_ref[:, cols]


def _dest(tab, e_idx, rank):
    k, n = e_idx.shape
    tn = TN_DEST
    assert SC_CHUNK == LANES and N_EXPERTS % LANES == 0
    return pl.pallas_call(
        _dest_kernel,
        out_shape=jax.ShapeDtypeStruct((n // SC_CHUNK, k, SC_CHUNK), I32),
        grid=(n // tn,),
        in_specs=[
            pl.BlockSpec(tab.shape, lambda i: (0, 0)),
            pl.BlockSpec((k, tn), lambda i: (0, i)),
            pl.BlockSpec((k, tn), lambda i: (0, i)),
        ],
        out_specs=pl.BlockSpec((tn // SC_CHUNK, k, SC_CHUNK), lambda i: (i, 0, 0)),
        compiler_params=pltpu.CompilerParams(vmem_limit_bytes=VMEM_LIMIT),
        name="dest",
    )(tab, e_idx, rank)


def _sc_dispatch(h2p, dest3, total_rows):
    n, words = h2p.shape
    nchunks = n // SC_CHUNK
    per_worker = nchunks // (SC_CORES * SC_SUBCORES)
    mesh = plsc.VectorSubcoreMesh(core_axis_name="c", subcore_axis_name="s",
                                  num_cores=SC_CORES, num_subcores=SC_SUBCORES)

    @functools.partial(
        pl.kernel, mesh=mesh,
        out_type=jax.ShapeDtypeStruct((total_rows, words), I32),
        scratch_types=[
            pltpu.VMEM((TOP_K, SC_CHUNK), I32),
            pltpu.VMEM((SC_CHUNK, words), I32),
            pltpu.SemaphoreType.DMA,
        ],
        name="sc_dispatch",
    )
    def run(h_hbm, d_hbm, xs_hbm, idx_v, rows_v, sem):
        wid = lax.axis_index("s") * SC_CORES + lax.axis_index("c")

        @pl.loop(0, per_worker)
        def _(j):
            ch = wid * per_worker + j
            pltpu.sync_copy(d_hbm.at[ch], idx_v)
            pltpu.sync_copy(h_hbm.at[pl.ds(ch * SC_CHUNK, SC_CHUNK)], rows_v)
            copies = [pltpu.async_copy(rows_v, xs_hbm.at[idx_v.at[k]], sem) for k in range(TOP_K)]
            for cp in copies:
                cp.wait()

    return run(h2p, dest3)


def _ffn_kernel(tab_ref, x_hbm, wg_hbm, wu_hbm, wd_hbm, y_hbm,
                x_s, y_s, wg_s, wu_s, wd_s, sem_x, sem_y, sem):
    i = pl.program_id(0)
    nreal = tab_ref[ROW_HEAD, 0]
    rows_per = x_s.shape[1]
    half = x_s.shape[2]
    sizes = tuple(range(FFN_GRAIN, rows_per + 1, FFN_GRAIN))

    def rows_needed(g):
        return (tab_ref[ROW_VALID, g] + (FFN_GRAIN - 1)) // FFN_GRAIN * FFN_GRAIN

    def by_size(nrows, fn):
        for n in sizes:
            @pl.when(nrows == n)
            def _():
                fn(n)

    def row_copy(g, n):
        slot = lax.rem(g, FFN_LOOKAHEAD + 1)
        return pltpu.make_async_copy(
            x_hbm.at[pl.ds(g * rows_per, n)], x_s.at[slot, pl.ds(0, n)], sem_x.at[slot])

    def out_copy(g, n):
        slot = lax.rem(g, FFN_OUT_SLOTS)
        return pltpu.make_async_copy(
            y_s.at[slot, pl.ds(0, n)], y_hbm.at[pl.ds(g * rows_per, n)], sem_y.at[slot])

    def weight_copies(e, s):
        return (pltpu.make_async_copy(wg_hbm.at[e], wg_s.at[s], sem.at[s, 0]),
                pltpu.make_async_copy(wu_hbm.at[e], wu_s.at[s], sem.at[s, 1]),
                pltpu.make_async_copy(wd_hbm.at[e], wd_s.at[s], sem.at[s, 2]))

    @pl.when(i == 0)
    def _():
        for j in range(FFN_WEIGHT_SLOTS - 1):
            @pl.when(tab_ref[ROW_HEAD, 1 + j] >= 0)
            def _():
                for cp in weight_copies(tab_ref[ROW_HEAD, 1 + j], j):
                    cp.start()
        for g in range(FFN_LOOKAHEAD):
            @pl.when(g < nreal)
            def _():
                by_size(rows_needed(g), lambda n: row_copy(g, n).start())

    @pl.when(i < nreal)
    def _():
        s = tab_ref[ROW_SLOT, i]
        fetch = i + FFN_LOOKAHEAD

        @pl.when(fetch < nreal)
        def _():
            by_size(rows_needed(fetch), lambda n: row_copy(fetch, n).start())

        by_size(rows_needed(i), lambda n: row_copy(i, n).wait())

        @pl.when(tab_ref[ROW_FIRST, i] == 1)
        def _():
            for cp in weight_copies(tab_ref[ROW_EXP, i], s):
                cp.wait()

            ahead = tab_ref[ROW_AHEAD, i]

            @pl.when(ahead >= 0)
            def _():
                for cp in weight_copies(ahead, lax.rem(s + FFN_WEIGHT_SLOTS - 1, FFN_WEIGHT_SLOTS)):
                    cp.start()

        @pl.when(i >= FFN_OUT_SLOTS)
        def _():
            done = i - FFN_OUT_SLOTS
            by_size(rows_needed(done), lambda n: out_copy(done, n).wait())

        x_slot = lax.rem(i, FFN_LOOKAHEAD + 1)
        y_slot = lax.rem(i, FFN_OUT_SLOTS)
        valid = tab_ref[ROW_VALID, i]

        def expert_rows(n):
            r = lax.broadcasted_iota(I32, (n, 1), 0)
            x_lo, x_hi = _unpack_halves(jnp.where(r < valid, x_s[x_slot, pl.ds(0, n), :], 0))
            hg = _dot(x_lo, wg_s[s, :half, :]) + _dot(x_hi, wg_s[s, half:, :])
            hu = _dot(x_lo, wu_s[s, :half, :]) + _dot(x_hi, wu_s[s, half:, :])
            y_s[y_slot, pl.ds(0, n), :] = _pack_halves(_dot(_silu(hg) * hu, wd_s[s]))
            out_copy(i, n).start()

        by_size(rows_needed(i), expert_rows)

        @pl.when(i == nreal - 1)
        def _():
            for back in range(FFN_OUT_SLOTS):
                last = i - back

                @pl.when(last >= 0)
                def _():
                    by_size(rows_needed(last), lambda n: out_copy(last, n).wait())


def _ffn(tab, xs, w_gate, w_up, w_down):
    p, half = xs.shape
    d = 2 * half
    ff = w_gate.shape[2]
    return pl.pallas_call(
        _ffn_kernel,
        out_shape=jax.ShapeDtypeStruct((p, half), I32),
        grid_spec=pltpu.PrefetchScalarGridSpec(
            num_scalar_prefetch=1,
            grid=(tab.shape[1],),
            in_specs=[
                pl.BlockSpec(memory_space=pl.ANY),
                pl.BlockSpec(memory_space=pl.ANY),
                pl.BlockSpec(memory_space=pl.ANY),
                pl.BlockSpec(memory_space=pl.ANY),
            ],
            out_specs=pl.BlockSpec(memory_space=pl.ANY),
            scratch_shapes=[
                pltpu.VMEM((FFN_LOOKAHEAD + 1, FFN_BLOCK, half), I32),
                pltpu.VMEM((FFN_OUT_SLOTS, FFN_BLOCK, half), I32),
                pltpu.VMEM((FFN_WEIGHT_SLOTS, d, ff), F32),
                pltpu.VMEM((FFN_WEIGHT_SLOTS, d, ff), F32),
                pltpu.VMEM((FFN_WEIGHT_SLOTS, ff, d), F32),
                pltpu.SemaphoreType.DMA((FFN_LOOKAHEAD + 1,)),
                pltpu.SemaphoreType.DMA((FFN_OUT_SLOTS,)),
                pltpu.SemaphoreType.DMA((FFN_WEIGHT_SLOTS, 3)),
            ],
        ),
        compiler_params=pltpu.CompilerParams(
            dimension_semantics=("arbitrary",), vmem_limit_bytes=VMEM_LIMIT, has_side_effects=True),
        name="ffn",
    )(tab, xs, w_gate, w_up, w_down)


def _sc_gather(y, dest3):
    a, words = y.shape
    nchunks, _, chunk = dest3.shape
    n = nchunks * chunk
    per_worker = nchunks // (SC_CORES * SC_SUBCORES)
    nbuf = SC_GATHER_BUFS
    parts = chunk // SC_GATHER_ROWS
    items = [(c, k, h) for c in range(per_worker) for k in range(TOP_K) for h in range(parts)]
    mesh = plsc.VectorSubcoreMesh(core_axis_name="c", subcore_axis_name="s",
                                  num_cores=SC_CORES, num_subcores=SC_SUBCORES)

    @functools.partial(
        pl.kernel, mesh=mesh,
        out_type=jax.ShapeDtypeStruct((TOP_K, n, words), I32),
        scratch_types=[
            pltpu.VMEM((per_worker, TOP_K, chunk), I32),
            [pltpu.VMEM((SC_GATHER_ROWS, words), I32)] * nbuf,
            pltpu.SemaphoreType.DMA((nbuf,)),
            pltpu.SemaphoreType.DMA((nbuf,)),
        ],
        name="sc_gather",
    )
    def run(y_hbm, d_hbm, yt_hbm, idx_v, bufs, sem_g, sem_w):
        wid = lax.axis_index("s") * SC_CORES + lax.axis_index("c")
        pltpu.sync_copy(d_hbm.at[pl.ds(wid * per_worker, per_worker)], idx_v)

        def gather(m):
            c, k, h = items[m]
            idx = idx_v.at[c, k, pl.ds(h * SC_GATHER_ROWS, SC_GATHER_ROWS)]
            return pltpu.async_copy(y_hbm.at[idx], bufs[m % nbuf], sem_g.at[m % nbuf])

        def write(m):
            c, k, h = items[m]
            rows = pl.ds((wid * per_worker + c) * chunk + h * SC_GATHER_ROWS, SC_GATHER_ROWS)
            return pltpu.async_copy(bufs[m % nbuf], yt_hbm.at[k, rows], sem_w.at[m % nbuf])

        gathers = {m: gather(m) for m in range(min(nbuf - 1, len(items)))}
        writes = {}
        for m in range(len(items)):
            gathers.pop(m).wait()
            writes[m] = write(m)
            nxt = m + nbuf - 1
            if nxt < len(items):
                if m >= 1:
                    writes.pop(m - 1).wait()
                gathers[nxt] = gather(nxt)
        for m in sorted(writes):
            writes.pop(m).wait()

    return run(y, dest3)


def _combine_kernel(yt_ref, wt_ref, xb_ref, mod_ref, o_ref):
    half = yt_ref.shape[2]
    wt = wt_ref[...].T
    lo, hi = _unpack_halves(yt_ref[0])
    r_lo = lo * wt[:, 0:1]
    r_hi = hi * wt[:, 0:1]
    for k in range(1, TOP_K):
        lo, hi = _unpack_halves(yt_ref[k])
        r_lo = r_lo + lo * wt[:, k:k + 1]
        r_hi = r_hi + hi * wt[:, k:k + 1]
    gate = mod_ref[0][5:6]
    o_ref[:, :half] = xb_ref[:, :half] + gate[:, :half] * r_lo
    o_ref[:, half:] = xb_ref[:, half:] + gate[:, half:] * r_hi


def _combine(yt, w_k, xb, mod3, seq):
    n, d = xb.shape
    tm = TM_COMBINE
    tiles_per_seq = seq // tm
    return pl.pallas_call(
        _combine_kernel,
        out_shape=jax.ShapeDtypeStruct((n, d), F32),
        grid=(n // tm,),
        in_specs=[
            pl.BlockSpec((TOP_K, tm, d // 2), lambda i: (0, i, 0)),
            pl.BlockSpec((TOP_K, tm), lambda i: (0, i)),
            pl.BlockSpec((tm, d), lambda i: (i, 0)),
            pl.BlockSpec((1, N_MOD, d), lambda i: (i // tiles_per_seq, 0, 0)),
        ],
        out_specs=pl.BlockSpec((tm, d), lambda i: (i, 0)),
        compiler_params=pltpu.CompilerParams(vmem_limit_bytes=VMEM_LIMIT),
        name="combine",
    )(yt, w_k, xb, mod3)


def _rotary_tables(seq):
    half = RET_DK // 2
    inv = ROPE_BASE ** (-np.arange(half, dtype=np.float64) / half)
    ang = np.arange(seq, dtype=np.float64)[:, None] * inv[None, :]
    cos = np.cos(ang).astype(np.float32)
    sin = np.sin(ang).astype(np.float32)
    return (jnp.asarray(np.concatenate([cos, cos], axis=-1)),
            jnp.asarray(np.concatenate([-sin, sin], axis=-1)))


def kernel(x, c, w_ada, b_ada, g_mix, w_in, q_gain, k_gain, w_out, g_ffn, w_router, router_bias,
           w_gate, w_up, w_down, ws_gate, ws_up, ws_down):
    bsz, seq, d = x.shape
    n = bsz * seq
    depth = w_ada.shape[0]
    cos_full, sin_signed = _rotary_tables(seq)
    log_g = jnp.asarray(np.log1p(-np.exp2(-5.0 - np.arange(RET_HEADS, dtype=np.float64))).astype(np.float32))
    ret_w = RET_HEADS * RET_DK
    x2 = x.reshape(n, d)
    for l in range(depth):
        mod3 = _adaln(c, w_ada[l], b_ada[l]).reshape(bsz, N_MOD, d)
        proj = _inproj(x2, mod3, g_mix[l].reshape(1, d), w_in[l], cos_full, sin_signed, seq)
        proj3 = proj.reshape(bsz, seq, IN_COLS)
        qg2 = jnp.tile(q_gain[l].reshape(1, MOBA_DH), (1, 2))
        kg2 = jnp.tile(k_gain[l].reshape(1, MOBA_DH), (1, 2))
        ret, moba = _mixers(log_g, proj3, qg2, kg2)
        xb, h2, e_idx, w_k, rank, cnt = _mid(
            ret.reshape(n, ret_w), moba.reshape(n, MOBA_HEADS * MOBA_DH), x2, mod3,
            w_out[l], g_ffn[l].reshape(1, d),
            w_router[l], router_bias[l].reshape(N_EXPERTS, 1),
            ws_gate[l], ws_up[l], ws_down[l], seq)
        nblk = n * TOP_K // FFN_BLOCK + N_EXPERTS
        tab = _sched(cnt, nblk)
        dest3 = _dest(tab, e_idx, rank)
        xs = _sc_dispatch(h2, dest3, nblk * FFN_BLOCK)
        y = _ffn(tab, xs, w_gate[l], w_up[l], w_down[l])
        yt = _sc_gather(y, dest3)
        x2 = _combine(yt, w_k, xb, mod3, seq)
    return x2.reshape(bsz, seq, d)
```

```python
import functools

import numpy as np
import jax
import jax.numpy as jnp
from jax import lax
from jax.experimental import pallas as pl
from jax.experimental.pallas import tpu as pltpu
from jax.experimental.pallas import tpu_sc as plsc

F32 = jnp.float32
BF16 = jnp.bfloat16
I32 = jnp.int32

RET_HEADS = 4
RET_DK = 128
MOBA_HEADS = 8
MOBA_DH = 64
MOBA_BLOCK = 256
MOBA_TOPK = 3
ROPE_BASE = 10000.0
N_EXPERTS = 256
TOP_K = 8
N_GROUPS = 8
TOPK_GROUPS = 4
GROUP_SIZE = N_EXPERTS // N_GROUPS
ROUTED_SCALE = 2.5
N_MOD = 6
EPS = 1e-6
IN_COLS = 3584

LANES = 128
BF16_TILE_ROWS = 16
RET_CHUNK = 256
TN_ADALN = 1024
TM_INPROJ = 1024
TM_MID = 1024
TN_DEST = 2048
TM_COMBINE = 512
MOBA_ONES_ROWS = BF16_TILE_ROWS
SC_CORES = 2
SC_SUBCORES = 16
SC_CHUNK = 128
SC_GATHER_ROWS = 64
SC_GATHER_BUFS = 3
FFN_BLOCK = 1024
FFN_GRAIN = 128
FFN_LOOKAHEAD = 5
FFN_WEIGHT_SLOTS = 3
FFN_OUT_SLOTS = 3
VMEM_LIMIT = 56 * 1024 * 1024

NEG_INF = float("-inf")
LOG2_E = 1.4426950408889634


def _silu(x):
    return x * jax.nn.sigmoid(x)


def _nt_dot(a, b):
    return lax.dot_general(a, b, (((1,), (1,)), ((), ())), preferred_element_type=F32)


def _tn_dot(a, b):
    return lax.dot_general(a, b, (((0,), (0,)), ((), ())), preferred_element_type=F32)


def _dot(a, b):
    return jnp.dot(a, b, preferred_element_type=F32)


HI_MASK = -65536


def _pack_halves(v):
    w = v.shape[1] // 2
    lo = lax.bitcast_convert_type(v[:, :w].astype(BF16).astype(F32), I32)
    hi = lax.bitcast_convert_type(v[:, w:].astype(BF16).astype(F32), I32)
    return lax.shift_right_logical(lo, 16) | (hi & HI_MASK)


def _unpack_halves(u):
    lo = lax.bitcast_convert_type(lax.shift_left(u, 16), F32)
    hi = lax.bitcast_convert_type(u & HI_MASK, F32)
    return lo, hi


def _adaln_kernel(c_ref, w_ref, b_ref, o_ref):
    s = _silu(c_ref[...])
    o_ref[...] = _dot(s.astype(BF16), w_ref[...].astype(BF16)) + b_ref[...]


def _adaln(c, w_ada, b_ada):
    bsz, d = c.shape
    ncol = w_ada.shape[1]
    tn = TN_ADALN
    return pl.pallas_call(
        _adaln_kernel,
        out_shape=jax.ShapeDtypeStruct((bsz, ncol), F32),
        grid=(ncol // tn,),
        in_specs=[
            pl.BlockSpec((bsz, d), lambda j: (0, 0)),
            pl.BlockSpec((d, tn), lambda j: (0, j)),
            pl.BlockSpec((1, tn), lambda j: (0, j)),
        ],
        out_specs=pl.BlockSpec((bsz, tn), lambda j: (0, j)),
        compiler_params=pltpu.CompilerParams(vmem_limit_bytes=VMEM_LIMIT),
        name="adaln",
    )(c, w_ada, b_ada.reshape(1, ncol))


def _inproj_kernel(x_ref, mod_ref, g_ref, w_ref, cos_ref, sin_ref, o_ref):
    x = x_ref[...]
    ms = jnp.mean(x * x, axis=-1, keepdims=True)
    m = mod_ref[0]
    h = (x * lax.rsqrt(ms + EPS) * g_ref[...]) * (1.0 + m[1:2]) + m[0:1]
    hb = h.astype(BF16)
    cosf = cos_ref[...]
    sinf = sin_ref[...]
    k_scale = RET_DK ** -0.5
    width = RET_HEADS * RET_DK
    for ci in range(IN_COLS // width):
        acc = _dot(hb, w_ref[:, ci * width:(ci + 1) * width])
        if ci < 2:
            for hh in range(RET_HEADS):
                xh = acc[:, hh * RET_DK:(hh + 1) * RET_DK]
                r = xh * cosf + pltpu.roll(xh, RET_DK // 2, axis=1) * sinf
                if ci == 1:
                    r = r * k_scale
                o_ref[:, ci * width + hh * RET_DK:ci * width + (hh + 1) * RET_DK] = r.astype(BF16)
        else:
            o_ref[:, ci * width:(ci + 1) * width] = acc.astype(BF16)


def _inproj(x2, mod3, g_mix, w_in, cos_full, sin_signed, seq):
    n, d = x2.shape
    tm = TM_INPROJ
    tiles_per_seq = seq // tm
    return pl.pallas_call(
        _inproj_kernel,
        out_shape=jax.ShapeDtypeStruct((n, IN_COLS), BF16),
        grid=(n // tm,),
        in_specs=[
            pl.BlockSpec((tm, d), lambda i: (i, 0)),
            pl.BlockSpec((1, N_MOD, d), lambda i: (i // tiles_per_seq, 0, 0)),
            pl.BlockSpec((1, d), lambda i: (0, 0)),
            pl.BlockSpec((d, IN_COLS), lambda i: (0, 0), pipeline_mode=pl.Buffered(1)),
            pl.BlockSpec((tm, LANES), lambda i: (i % tiles_per_seq, 0)),
            pl.BlockSpec((tm, LANES), lambda i: (i % tiles_per_seq, 0)),
        ],
        out_specs=pl.BlockSpec((tm, IN_COLS), lambda i: (i, 0)),
        compiler_params=pltpu.CompilerParams(vmem_limit_bytes=VMEM_LIMIT),
        name="inproj",
    )(x2, mod3, g_mix, w_in, cos_full, sin_signed)


def _ret_kernel(lg_ref, q_ref, k_ref, v_ref, g_ref, o_ref):
    seq = q_ref.shape[1]
    c = RET_CHUNK
    lg = lg_ref[pl.program_id(1)]
    row = lax.broadcasted_iota(I32, (c, c), 0)
    col = lax.broadcasted_iota(I32, (c, c), 1)
    diff = (row - col).astype(F32)
    dmask = jnp.where(diff >= 0, jnp.exp(lg * jnp.maximum(diff, 0.0)), 0.0)
    idx = lax.broadcasted_iota(I32, (c, 1), 0).astype(F32)
    q_decay = jnp.exp(lg * (idx + 1.0))
    k_decay = jnp.exp(lg * (c - 1.0 - idx))
    chunk_decay = jnp.exp(jnp.full((1, 1), lg * c, F32))
    state = jnp.zeros((RET_DK, RET_DK), F32)
    for n in range(seq // c):
        rows = slice(n * c, (n + 1) * c)
        qn = q_ref[0, rows, :]
        kn = k_ref[0, rows, :]
        vn = v_ref[0, rows, :]
        scores = _nt_dot(qn, kn) * dmask
        inner = _dot(scores.astype(BF16), vn)
        qs = (qn.astype(F32) * q_decay).astype(BF16)
        cross = _dot(qs, state.astype(BF16))
        o = inner + cross
        o = o * lax.rsqrt(jnp.mean(o * o, axis=-1, keepdims=True) + EPS)
        gn = g_ref[0, rows, :].astype(F32)
        o_ref[0, rows, :] = (_silu(gn) * o).astype(BF16)
        ks = (kn.astype(F32) * k_decay).astype(BF16)
        state = state * chunk_decay + _tn_dot(ks, vn)


def _moba_kernel(q_ref, k_ref, v_ref, qg_ref, kg_ref, o_ref, qt_s, ka_s, kb_s, vta_s, vtb_s):
    seq = q_ref.shape[1]
    lb = MOBA_BLOCK
    nb = seq // lb
    lane = lax.broadcasted_iota(I32, (1, LANES), 1)
    is_a = lane < MOBA_DH

    def head_norm(xf, gain):
        sq = xf * xf
        s_a = jnp.sum(jnp.where(is_a, sq, 0.0), axis=-1, keepdims=True)
        s_b = jnp.sum(jnp.where(is_a, 0.0, sq), axis=-1, keepdims=True)
        inv = jnp.where(is_a, lax.rsqrt(s_a / MOBA_DH + EPS), lax.rsqrt(s_b / MOBA_DH + EPS))
        return xf * inv * gain

    qg = qg_ref[...]
    kg = kg_ref[...]
    k_means = []
    for j in range(nb):
        rows = slice(j * lb, (j + 1) * lb)
        kf = head_norm(k_ref[0, rows, :].astype(F32), kg)
        ka_s[rows, :] = jnp.where(is_a, kf, 0.0).astype(BF16)
        kb_s[rows, :] = jnp.where(is_a, 0.0, kf).astype(BF16)
        k_means.append(jnp.mean(kf, axis=0, keepdims=True))
        qf = head_norm(q_ref[0, rows, :].astype(F32), qg)
        qt_s[:, rows] = (qf * (MOBA_DH ** -0.5 * LOG2_E)).T.astype(BF16)
        vt = v_ref[0, rows, :].T
        ones = jnp.ones((MOBA_ONES_ROWS, lb), BF16)
        vta_s[:, rows] = jnp.concatenate([vt[:MOBA_DH], ones], axis=0)
        vtb_s[:, rows] = jnp.concatenate([vt[MOBA_DH:], ones], axis=0)
    k_mean = jnp.concatenate(k_means + [jnp.zeros((BF16_TILE_ROWS - nb, LANES), F32)], axis=0)
    k_mean_h = (jnp.where(is_a, k_mean, 0.0).astype(BF16), jnp.where(is_a, 0.0, k_mean).astype(BF16))
    k_s = (ka_s, kb_s)
    vt_s = (vta_s, vtb_s)

    r_loc = lax.broadcasted_iota(I32, (lb, lb), 0)
    c_loc = lax.broadcasted_iota(I32, (lb, lb), 1)
    causal = r_loc <= c_loc

    def logits(i, hx):
        qt = qt_s[:, i * lb:(i + 1) * lb]
        bias = [None] * i
        if i > MOBA_TOPK:
            gate = _dot(k_mean_h[hx], qt)
            g = [gate[j:j + 1, :] for j in range(i)]
            for j in range(i):
                rank = jnp.zeros((1, lb), F32)
                for j2 in range(i):
                    if j2 == j:
                        continue
                    beats = (g[j2] >= g[j]) if j2 < j else (g[j2] > g[j])
                    rank = rank + jnp.where(beats, 1.0, 0.0)
                bias[j] = jnp.where(rank < float(MOBA_TOPK), 0.0, NEG_INF).astype(BF16)
        pieces = []
        for j in range(i + 1):
            s = _dot(k_s[hx][j * lb:(j + 1) * lb, :], qt).astype(BF16)
            if j == i:
                s = jnp.where(causal, s, jnp.asarray(NEG_INF, BF16))
            elif bias[j] is not None:
                s = s + bias[j]
            pieces.append(s)
        mx = jnp.max(pieces[0], axis=0, keepdims=True)
        for s in pieces[1:]:
            mx = jnp.maximum(mx, jnp.max(s, axis=0, keepdims=True))
        return pieces, mx

    def attend(hx, pieces, mx):
        acc = jnp.zeros((MOBA_DH + MOBA_ONES_ROWS, lb), F32)
        for j, s in enumerate(pieces):
            acc = acc + _dot(vt_s[hx][:, j * lb:(j + 1) * lb], jnp.exp2(s - mx))
        return acc[:MOBA_DH] / acc[MOBA_DH:MOBA_DH + 1, :]

    groups = [(i, hx) for i in range(nb) for hx in range(2)]
    outs = {}
    pending = logits(*groups[0])
    for n, (i, hx) in enumerate(groups):
        ahead = logits(*groups[n + 1]) if n + 1 < len(groups) else None
        outs[(i, hx)] = attend(hx, *pending)
        pending = ahead
        if hx == 1:
            o_ref[0, i * lb:(i + 1) * lb, :] = jnp.concatenate(
                [outs.pop((i, 0)), outs.pop((i, 1))], axis=0).T.astype(BF16)


def _mixers_kernel(lg_ref, rq_ref, rk_ref, rv_ref, rg_ref, mq_ref, mk_ref, mv_ref, qg_ref, kg_ref,
                   ret_ref, moba_ref, *moba_scratch):
    _ret_kernel(lg_ref, rq_ref, rk_ref, rv_ref, rg_ref, ret_ref)
    _moba_kernel(mq_ref, mk_ref, mv_ref, qg_ref, kg_ref, moba_ref, *moba_scratch)


def _mixers(log_g, proj3, qg2, kg2):
    bsz, seq, _ = proj3.shape
    pairs = MOBA_HEADS // 2
    assert pairs == RET_HEADS and RET_DK == LANES
    blk = (1, seq, LANES)
    base = 4 * RET_HEADS

    def col(first):
        return pl.BlockSpec(blk, lambda b, p, lg: (b, 0, first + p))

    return pl.pallas_call(
        _mixers_kernel,
        out_shape=(jax.ShapeDtypeStruct((bsz, seq, RET_HEADS * RET_DK), BF16),
                   jax.ShapeDtypeStruct((bsz, seq, MOBA_HEADS * MOBA_DH), BF16)),
        grid_spec=pltpu.PrefetchScalarGridSpec(
            num_scalar_prefetch=1,
            grid=(bsz, pairs),
            in_specs=[
                col(0), col(RET_HEADS), col(2 * RET_HEADS), col(3 * RET_HEADS),
                col(base), col(base + pairs), col(base + 2 * pairs),
                pl.BlockSpec((1, LANES), lambda b, p, lg: (0, 0)),
                pl.BlockSpec((1, LANES), lambda b, p, lg: (0, 0)),
            ],
            out_specs=(col(0), col(0)),
            scratch_shapes=[
                pltpu.VMEM((LANES, seq), BF16),
                pltpu.VMEM((seq, LANES), BF16),
                pltpu.VMEM((seq, LANES), BF16),
                pltpu.VMEM((MOBA_DH + MOBA_ONES_ROWS, seq), BF16),
                pltpu.VMEM((MOBA_DH + MOBA_ONES_ROWS, seq), BF16),
            ],
        ),
        compiler_params=pltpu.CompilerParams(vmem_limit_bytes=VMEM_LIMIT),
        name="mixers",
    )(log_g, *([proj3] * 7), qg2, kg2)


def _mid_kernel(ret_ref, moba_ref, x_ref, mod_ref, wo_ref, g_ref, wr_ref, rb_ref,
                wsg_ref, wsu_ref, wsd_ref,
                xb_ref, h2_ref, e_ref, w_ref, rk_ref, cnt_ref,
                carry_s, wo_s, wr_s, wsg_s, wsu_s, wsd_s, upper_s):
    i = pl.program_id(0)
    tm = x_ref.shape[0]
    half = ret_ref.shape[1]

    @pl.when(i == 0)
    def _():
        carry_s[...] = jnp.zeros_like(carry_s)
        wo_s[...] = wo_ref[...].astype(BF16)
        wsg_s[...] = wsg_ref[...].astype(BF16)
        wsu_s[...] = wsu_ref[...].astype(BF16)
        wsd_s[...] = wsd_ref[...].astype(BF16)
        wr_s[...] = wr_ref[...].T.astype(BF16)
        tr = lax.broadcasted_iota(I32, (tm, tm), 0)
        tc = lax.broadcasted_iota(I32, (tm, tm), 1)
        upper_s[...] = jnp.where(tr < tc, 1.0, 0.0).astype(BF16)

    m = mod_ref[0]
    mixed = _dot(ret_ref[...], wo_s[:half, :]) + _dot(moba_ref[...], wo_s[half:, :])
    x1 = x_ref[...] + m[2:3] * mixed
    ms = jnp.mean(x1 * x1, axis=-1, keepdims=True)
    h2 = (x1 * lax.rsqrt(ms + EPS) * g_ref[...]) * (1.0 + m[4:5]) + m[3:4]
    h2_ref[...] = _pack_halves(h2)
    h2b = h2.astype(BF16)

    hid = _silu(_dot(h2b, wsg_s[...])) * _dot(h2b, wsu_s[...])
    xb_ref[...] = x1 + m[5:6] * _dot(hid.astype(BF16), wsd_s[...])

    scores = jax.nn.sigmoid(_nt_dot(wr_s[...], h2b))
    biased = scores + rb_ref[...]
    grp = biased.reshape(N_GROUPS, GROUP_SIZE, tm)
    gi = lax.broadcasted_iota(I32, (N_GROUPS, GROUP_SIZE, tm), 1).astype(F32)
    top1 = jnp.max(grp, axis=1, keepdims=True)
    first = jnp.min(jnp.where(grp == top1, gi, float(GROUP_SIZE)), axis=1, keepdims=True)
    top2 = jnp.max(jnp.where(gi == first, NEG_INF, grp), axis=1, keepdims=True)
    gscore = (top1 + top2).reshape(N_GROUPS, tm)
    gidx = lax.broadcasted_iota(I32, (N_GROUPS, tm), 0)
    grank = jnp.zeros((N_GROUPS, tm), F32)
    for g2 in range(N_GROUPS):
        rowv = gscore[g2:g2 + 1, :]
        beats = (rowv > gscore) | ((rowv == gscore) & (g2 < gidx))
        grank = grank + jnp.where(beats, 1.0, 0.0)
    gsel = jnp.where(grank < float(TOPK_GROUPS), 1.0, 0.0)
    emask = jnp.broadcast_to(gsel.reshape(N_GROUPS, 1, tm), (N_GROUPS, GROUP_SIZE, tm)).reshape(N_EXPERTS, tm)
    choice = jnp.where(emask > 0.5, biased, NEG_INF)

    eidx = lax.broadcasted_iota(I32, (N_EXPERTS, tm), 0).astype(F32)
    remaining = choice
    e_rows = []
    for _k in range(TOP_K):
        mx = jnp.max(remaining, axis=0, keepdims=True)
        idx = jnp.min(jnp.where(remaining == mx, eidx, float(N_EXPERTS)), axis=0, keepdims=True)
        e_rows.append(idx)
        remaining = jnp.where(eidx == idx, NEG_INF, remaining)
    selmask = jnp.where(remaining != choice, 1.0, 0.0)

    prefix = _dot(selmask.astype(BF16), upper_s[...]) + carry_s[...]
    w_rows = []
    r_rows = []
    for ek in e_rows:
        onehot = eidx == ek
        w_rows.append(jnp.sum(jnp.where(onehot, scores, 0.0), axis=0, keepdims=True))
        r_rows.append(jnp.sum(jnp.where(onehot, prefix, 0.0), axis=0, keepdims=True))
    wsum = w_rows[0]
    for wk in w_rows[1:]:
        wsum = wsum + wk
    carry_s[...] = carry_s[...] + jnp.sum(selmask, axis=1, keepdims=True)

    e_ref[...] = jnp.concatenate(e_rows, axis=0).astype(I32)
    w_ref[...] = jnp.concatenate([wk / wsum * ROUTED_SCALE for wk in w_rows], axis=0)
    rk_ref[...] = jnp.concatenate(r_rows, axis=0).astype(I32)
    cnt_ref[...] = carry_s[...].astype(I32)


def _mid(ret2, moba2, x2, mod3, wo, g_ffn, w_router, rbias, wsg, wsu, wsd, seq):
    n, d = x2.shape
    tm = TM_MID
    tiles_per_seq = seq // tm
    half = ret2.shape[1]
    ff = wsg.shape[1]
    const = lambda i: (0, 0)
    row = lambda i: (i, 0)
    colt = lambda i: (0, i)

    def resident(shape):
        return pl.BlockSpec(shape, const, pipeline_mode=pl.Buffered(1))

    return pl.pallas_call(
        _mid_kernel,
        out_shape=(
            jax.ShapeDtypeStruct((n, d), F32),
            jax.ShapeDtypeStruct((n, d // 2), I32),
            jax.ShapeDtypeStruct((TOP_K, n), I32),
            jax.ShapeDtypeStruct((TOP_K, n), F32),
            jax.ShapeDtypeStruct((TOP_K, n), I32),
            jax.ShapeDtypeStruct((N_EXPERTS, 1), I32),
        ),
        grid=(n // tm,),
        in_specs=[
            pl.BlockSpec((tm, half), row),
            pl.BlockSpec((tm, half), row),
            pl.BlockSpec((tm, d), row),
            pl.BlockSpec((1, N_MOD, d), lambda i: (i // tiles_per_seq, 0, 0)),
            resident(wo.shape),
            pl.BlockSpec((1, d), const),
            resident((d, N_EXPERTS)),
            pl.BlockSpec((N_EXPERTS, 1), const),
            resident((d, ff)),
            resident((d, ff)),
            resident((ff, d)),
        ],
        out_specs=(
            pl.BlockSpec((tm, d), row),
            pl.BlockSpec((tm, d // 2), row),
            pl.BlockSpec((TOP_K, tm), colt),
            pl.BlockSpec((TOP_K, tm), colt),
            pl.BlockSpec((TOP_K, tm), colt),
            pl.BlockSpec((N_EXPERTS, 1), const),
        ),
        scratch_shapes=[
            pltpu.VMEM((N_EXPERTS, 1), F32),
            pltpu.VMEM(wo.shape, BF16),
            pltpu.VMEM((N_EXPERTS, d), BF16),
            pltpu.VMEM((d, ff), BF16),
            pltpu.VMEM((d, ff), BF16),
            pltpu.VMEM((ff, d), BF16),
            pltpu.VMEM((tm, tm), BF16),
        ],
        compiler_params=pltpu.CompilerParams(
            dimension_semantics=("arbitrary",), vmem_limit_bytes=VMEM_LIMIT),
        name="mid",
    )(ret2, moba2, x2, mod3, wo, g_ffn, w_router, rbias, wsg, wsu, wsd)


ROW_EXP = 0
ROW_VALID = 1
ROW_FIRST = 2
ROW_AHEAD = 3
ROW_SLOT = 4
ROW_HEAD = 5
ROW_START = 6
SCHED_ROWS = 8


def _sched_kernel(cnt_ref, tab_ref):
    ne = N_EXPERTS
    nblk = tab_ref.shape[1]
    shift = FFN_BLOCK.bit_length() - 1
    e_sub = lax.broadcasted_iota(I32, (ne, ne), 0)
    e_lane = lax.broadcasted_iota(I32, (ne, ne), 1)
    e_col = lax.broadcasted_iota(I32, (ne, 1), 0).astype(F32)
    ids_row = lax.broadcasted_iota(I32, (1, ne), 1).astype(F32) + 1.0

    def to_row(col):
        return jnp.sum(jnp.where(e_sub == e_lane, col, 0.0), axis=0, keepdims=True)

    def running_col(row):
        return jnp.sum(jnp.where(e_lane <= e_sub, row, 0.0), axis=1, keepdims=True)

    def running_row(col):
        return jnp.sum(jnp.where(e_sub <= e_lane, col, 0.0), axis=0, keepdims=True)

    cnt_i = cnt_ref[...]
    cnt_col = cnt_i.astype(F32)
    pad_col = lax.shift_left(lax.shift_right_logical(cnt_i + (FFN_BLOCK - 1), shift), shift).astype(F32)
    pad_row = to_row(pad_col)
    ends_col = running_col(pad_row)
    start_col = ends_col - pad_col
    start_row = running_row(pad_col) - pad_row
    nreal = jnp.sum(pad_row, axis=1, keepdims=True) * (1.0 / FFN_BLOCK)

    step = lax.broadcasted_iota(I32, (1, nblk), 1)
    g = step.astype(F32)
    row0 = jnp.minimum(g, nreal - 1.0) * float(FFN_BLOCK)
    exp_g = jnp.sum(jnp.where(ends_col <= row0, 1.0, 0.0), axis=0, keepdims=True)
    mine = e_col == exp_g

    def per_block(col):
        return jnp.sum(jnp.where(mine, col, 0.0), axis=0, keepdims=True)

    valid = jnp.clip(per_block(cnt_col + start_col) - row0, 0.0, float(FFN_BLOCK))
    first = jnp.where(jnp.logical_and(g < nreal, per_block(start_col) == row0), 1.0, 0.0)

    used_col = jnp.where(cnt_col > 0.0, 1.0, 0.0)
    used_row = to_row(used_col)
    ord_col = running_col(used_row) - 1.0
    ord_row = running_row(used_col) - 1.0

    def used_at(pos):
        hit = jnp.logical_and(used_row > 0.0, ord_row == pos)
        return jnp.sum(jnp.where(hit, ids_row, 0.0), axis=1, keepdims=True) - 1.0

    ahead_col = used_at(ord_col + float(FFN_WEIGHT_SLOTS - 1))
    slot_col = ord_col - FFN_WEIGHT_SLOTS * jnp.floor((ord_col + 0.5) * (1.0 / FFN_WEIGHT_SLOTS))
    head = jnp.where(step == 0, nreal, 0.0)
    for j in range(FFN_WEIGHT_SLOTS - 1):
        head = head + jnp.where(step == 1 + j, used_at(jnp.full((1, 1), float(j), F32)), 0.0)
    rows = [exp_g, valid, first, per_block(ahead_col), per_block(slot_col), head,
            jnp.concatenate([start_row, jnp.zeros((1, nblk - ne), F32)], axis=1),
            jnp.zeros((1, nblk), F32)]
    tab_ref[...] = jnp.concatenate(rows, axis=0).astype(I32)


def _sched(cnt, nblk):
    assert FFN_BLOCK & (FFN_BLOCK - 1) == 0 and nblk >= N_EXPERTS
    return pl.pallas_call(
        _sched_kernel,
        out_shape=jax.ShapeDtypeStruct((SCHED_ROWS, nblk), I32),
        compiler_params=pltpu.CompilerParams(vmem_limit_bytes=VMEM_LIMIT),
        name="sched",
    )(cnt)


def _dest_kernel(tab_ref, e_ref, rk_ref, o_ref):
    k = e_ref.shape[0]
    segments = [jnp.broadcast_to(tab_ref[ROW_START:ROW_START + 1, s * LANES:(s + 1) * LANES], (k, LANES))
                for s in range(N_EXPERTS // LANES)]
    for ch in range(o_ref.shape[0]):
        cols = slice(ch * SC_CHUNK, (ch + 1) * SC_CHUNK)
        e = e_ref[:, cols]
        lane = e & (LANES - 1)
        start = jnp.take_along_axis(segments[0], lane, axis=1)
        for s in range(1, len(segments)):
            start = jnp.where(e >= s * LANES, jnp.take_along_axis(segments[s], lane, axis=1), start)
        o_ref[ch] = start + rk_ref[:, cols]


def _dest(tab, e_idx, rank):
    k, n = e_idx.shape
    tn = TN_DEST
    assert SC_CHUNK == LANES and N_EXPERTS % LANES == 0
    return pl.pallas_call(
        _dest_kernel,
        out_shape=jax.ShapeDtypeStruct((n // SC_CHUNK, k, SC_CHUNK), I32),
        grid=(n // tn,),
        in_specs=[
            pl.BlockSpec(tab.shape, lambda i: (0, 0)),
            pl.BlockSpec((k, tn), lambda i: (0, i)),
            pl.BlockSpec((k, tn), lambda i: (0, i)),
        ],
        out_specs=pl.BlockSpec((tn // SC_CHUNK, k, SC_CHUNK), lambda i: (i, 0, 0)),
        compiler_params=pltpu.CompilerParams(vmem_limit_bytes=VMEM_LIMIT),
        name="dest",
    )(tab, e_idx, rank)


def _sc_dispatch(h2p, dest3, total_rows):
    n, words = h2p.shape
    nchunks = n // SC_CHUNK
    per_worker = nchunks // (SC_CORES * SC_SUBCORES)
    mesh = plsc.VectorSubcoreMesh(core_axis_name="c", subcore_axis_name="s",
                                  num_cores=SC_CORES, num_subcores=SC_SUBCORES)

    @functools.partial(
        pl.kernel, mesh=mesh,
        out_type=jax.ShapeDtypeStruct((total_rows, words), I32),
        scratch_types=[
            pltpu.VMEM((TOP_K, SC_CHUNK), I32),
            pltpu.VMEM((SC_CHUNK, words), I32),
            pltpu.SemaphoreType.DMA,
        ],
        name="sc_dispatch",
    )
    def run(h_hbm, d_hbm, xs_hbm, idx_v, rows_v, sem):
        wid = lax.axis_index("s") * SC_CORES + lax.axis_index("c")

        @pl.loop(0, per_worker)
        def _(j):
            ch = wid * per_worker + j
            pltpu.sync_copy(d_hbm.at[ch], idx_v)
            pltpu.sync_copy(h_hbm.at[pl.ds(ch * SC_CHUNK, SC_CHUNK)], rows_v)
            copies = [pltpu.async_copy(rows_v, xs_hbm.at[idx_v.at[k]], sem) for k in range(TOP_K)]
            for cp in copies:
                cp.wait()

    return run(h2p, dest3)


def _ffn_kernel(tab_ref, x_hbm, wg_hbm, wu_hbm, wd_hbm, y_hbm,
                x_s, y_s, wg_s, wu_s, wd_s, sem_x, sem_y, sem):
    i = pl.program_id(0)
    nreal = tab_ref[ROW_HEAD, 0]
    rows_per = x_s.shape[1]
    half = x_s.shape[2]
    sizes = tuple(range(FFN_GRAIN, rows_per + 1, FFN_GRAIN))

    def rows_needed(g):
        return (tab_ref[ROW_VALID, g] + (FFN_GRAIN - 1)) // FFN_GRAIN * FFN_GRAIN

    def by_size(nrows, fn):
        for n in sizes:
            @pl.when(nrows == n)
            def _():
                fn(n)

    def row_copy(g, n):
        slot = lax.rem(g, FFN_LOOKAHEAD + 1)
        return pltpu.make_async_copy(
            x_hbm.at[pl.ds(g * rows_per, n)], x_s.at[slot, pl.ds(0, n)], sem_x.at[slot])

    def out_copy(g, n):
        slot = lax.rem(g, FFN_OUT_SLOTS)
        return pltpu.make_async_copy(
            y_s.at[slot, pl.ds(0, n)], y_hbm.at[pl.ds(g * rows_per, n)], sem_y.at[slot])

    def weight_copies(e, s):
        return (pltpu.make_async_copy(wg_hbm.at[e], wg_s.at[s], sem.at[s, 0]),
                pltpu.make_async_copy(wu_hbm.at[e], wu_s.at[s], sem.at[s, 1]),
                pltpu.make_async_copy(wd_hbm.at[e], wd_s.at[s], sem.at[s, 2]))

    @pl.when(i == 0)
    def _():
        for j in range(FFN_WEIGHT_SLOTS - 1):
            @pl.when(tab_ref[ROW_HEAD, 1 + j] >= 0)
            def _():
                for cp in weight_copies(tab_ref[ROW_HEAD, 1 + j], j):
                    cp.start()
        for g in range(FFN_LOOKAHEAD):
            @pl.when(g < nreal)
            def _():
                by_size(rows_needed(g), lambda n: row_copy(g, n).start())

    @pl.when(i < nreal)
    def _():
        s = tab_ref[ROW_SLOT, i]
        fetch = i + FFN_LOOKAHEAD

        @pl.when(fetch < nreal)
        def _():
            by_size(rows_needed(fetch), lambda n: row_copy(fetch, n).start())

        by_size(rows_needed(i), lambda n: row_copy(i, n).wait())

        @pl.when(tab_ref[ROW_FIRST, i] == 1)
        def _():
            for cp in weight_copies(tab_ref[ROW_EXP, i], s):
                cp.wait()

            ahead = tab_ref[ROW_AHEAD, i]

            @pl.when(ahead >= 0)
            def _():
                for cp in weight_copies(ahead, lax.rem(s + FFN_WEIGHT_SLOTS - 1, FFN_WEIGHT_SLOTS)):
                    cp.start()

        @pl.when(i >= FFN_OUT_SLOTS)
        def _():
            done = i - FFN_OUT_SLOTS
            by_size(rows_needed(done), lambda n: out_copy(done, n).wait())

        x_slot = lax.rem(i, FFN_LOOKAHEAD + 1)
        y_slot = lax.rem(i, FFN_OUT_SLOTS)
        valid = tab_ref[ROW_VALID, i]

        def expert_rows(n):
            r = lax.broadcasted_iota(I32, (n, 1), 0)
            x_lo, x_hi = _unpack_halves(jnp.where(r < valid, x_s[x_slot, pl.ds(0, n), :], 0))
            hg = _dot(x_lo, wg_s[s, :half, :]) + _dot(x_hi, wg_s[s, half:, :])
            hu = _dot(x_lo, wu_s[s, :half, :]) + _dot(x_hi, wu_s[s, half:, :])
            y_s[y_slot, pl.ds(0, n), :] = _pack_halves(_dot(_silu(hg) * hu, wd_s[s]))
            out_copy(i, n).start()

        by_size(rows_needed(i), expert_rows)

        @pl.when(i == nreal - 1)
        def _():
            for back in range(FFN_OUT_SLOTS):
                last = i - back

                @pl.when(last >= 0)
                def _():
                    by_size(rows_needed(last), lambda n: out_copy(last, n).wait())


def _ffn(tab, xs, w_gate, w_up, w_down):
    p, half = xs.shape
    d = 2 * half
    ff = w_gate.shape[2]
    return pl.pallas_call(
        _ffn_kernel,
        out_shape=jax.ShapeDtypeStruct((p, half), I32),
        grid_spec=pltpu.PrefetchScalarGridSpec(
            num_scalar_prefetch=1,
            grid=(tab.shape[1],),
            in_specs=[
                pl.BlockSpec(memory_space=pl.ANY),
                pl.BlockSpec(memory_space=pl.ANY),
                pl.BlockSpec(memory_space=pl.ANY),
                pl.BlockSpec(memory_space=pl.ANY),
            ],
            out_specs=pl.BlockSpec(memory_space=pl.ANY),
            scratch_shapes=[
                pltpu.VMEM((FFN_LOOKAHEAD + 1, FFN_BLOCK, half), I32),
                pltpu.VMEM((FFN_OUT_SLOTS, FFN_BLOCK, half), I32),
                pltpu.VMEM((FFN_WEIGHT_SLOTS, d, ff), F32),
                pltpu.VMEM((FFN_WEIGHT_SLOTS, d, ff), F32),
                pltpu.VMEM((FFN_WEIGHT_SLOTS, ff, d), F32),
                pltpu.SemaphoreType.DMA((FFN_LOOKAHEAD + 1,)),
                pltpu.SemaphoreType.DMA((FFN_OUT_SLOTS,)),
                pltpu.SemaphoreType.DMA((FFN_WEIGHT_SLOTS, 3)),
            ],
        ),
        compiler_params=pltpu.CompilerParams(
            dimension_semantics=("arbitrary",), vmem_limit_bytes=VMEM_LIMIT, has_side_effects=True),
        name="ffn",
    )(tab, xs, w_gate, w_up, w_down)


def _sc_gather(y, dest3):
    a, words = y.shape
    nchunks, _, chunk = dest3.shape
    n = nchunks * chunk
    per_worker = nchunks // (SC_CORES * SC_SUBCORES)
    nbuf = SC_GATHER_BUFS
    parts = chunk // SC_GATHER_ROWS
    items = [(c, k, h) for c in range(per_worker) for k in range(TOP_K) for h in range(parts)]
    mesh = plsc.VectorSubcoreMesh(core_axis_name="c", subcore_axis_name="s",
                                  num_cores=SC_CORES, num_subcores=SC_SUBCORES)

    @functools.partial(
        pl.kernel, mesh=mesh,
        out_type=jax.ShapeDtypeStruct((TOP_K, n, words), I32),
        scratch_types=[
            pltpu.VMEM((per_worker, TOP_K, chunk), I32),
            [pltpu.VMEM((SC_GATHER_ROWS, words), I32)] * nbuf,
            pltpu.SemaphoreType.DMA((nbuf,)),
            pltpu.SemaphoreType.DMA((nbuf,)),
        ],
        name="sc_gather",
    )
    def run(y_hbm, d_hbm, yt_hbm, idx_v, bufs, sem_g, sem_w):
        wid = lax.axis_index("s") * SC_CORES + lax.axis_index("c")
        pltpu.sync_copy(d_hbm.at[pl.ds(wid * per_worker, per_worker)], idx_v)

        def gather(m):
            c, k, h = items[m]
            idx = idx_v.at[c, k, pl.ds(h * SC_GATHER_ROWS, SC_GATHER_ROWS)]
            return pltpu.async_copy(y_hbm.at[idx], bufs[m % nbuf], sem_g.at[m % nbuf])

        def write(m):
            c, k, h = items[m]
            rows = pl.ds((wid * per_worker + c) * chunk + h * SC_GATHER_ROWS, SC_GATHER_ROWS)
            return pltpu.async_copy(bufs[m % nbuf], yt_hbm.at[k, rows], sem_w.at[m % nbuf])

        gathers = {m: gather(m) for m in range(min(nbuf - 1, len(items)))}
        writes = {}
        for m in range(len(items)):
            gathers.pop(m).wait()
            writes[m] = write(m)
            nxt = m + nbuf - 1
            if nxt < len(items):
                if m >= 1:
                    writes.pop(m - 1).wait()
                gathers[nxt] = gather(nxt)
        for m in sorted(writes):
            writes.pop(m).wait()

    return run(y, dest3)


def _combine_kernel(yt_ref, wt_ref, xb_ref, mod_ref, o_ref):
    half = yt_ref.shape[2]
    wt = wt_ref[...].T
    lo, hi = _unpack_halves(yt_ref[0])
    r_lo = lo * wt[:, 0:1]
    r_hi = hi * wt[:, 0:1]
    for k in range(1, TOP_K):
        lo, hi = _unpack_halves(yt_ref[k])
        r_lo = r_lo + lo * wt[:, k:k + 1]
        r_hi = r_hi + hi * wt[:, k:k + 1]
    gate = mod_ref[0][5:6]
    o_ref[:, :half] = xb_ref[:, :half] + gate[:, :half] * r_lo
    o_ref[:, half:] = xb_ref[:, half:] + gate[:, half:] * r_hi


def _combine(yt, w_k, xb, mod3, seq):
    n, d = xb.shape
    tm = TM_COMBINE
    tiles_per_seq = seq // tm
    return pl.pallas_call(
        _combine_kernel,
        out_shape=jax.ShapeDtypeStruct((n, d), F32),
        grid=(n // tm,),
        in_specs=[
            pl.BlockSpec((TOP_K, tm, d // 2), lambda i: (0, i, 0)),
            pl.BlockSpec((TOP_K, tm), lambda i: (0, i)),
            pl.BlockSpec((tm, d), lambda i: (i, 0)),
            pl.BlockSpec((1, N_MOD, d), lambda i: (i // tiles_per_seq, 0, 0)),
        ],
        out_specs=pl.BlockSpec((tm, d), lambda i: (i, 0)),
        compiler_params=pltpu.CompilerParams(vmem_limit_bytes=VMEM_LIMIT),
        name="combine",
    )(yt, w_k, xb, mod3)


def _rotary_tables(seq):
    half = RET_DK // 2
    inv = ROPE_BASE ** (-np.arange(half, dtype=np.float64) / half)
    ang = np.arange(seq, dtype=np.float64)[:, None] * inv[None, :]
    cos = np.cos(ang).astype(np.float32)
    sin = np.sin(ang).astype(np.float32)
    return (jnp.asarray(np.concatenate([cos, cos], axis=-1)),
            jnp.asarray(np.concatenate([-sin, sin], axis=-1)))


def kernel(x, c, w_ada, b_ada, g_mix, w_in, q_gain, k_gain, w_out, g_ffn, w_router, router_bias,
           w_gate, w_up, w_down, ws_gate, ws_up, ws_down):
    bsz, seq, d = x.shape
    n = bsz * seq
    depth = w_ada.shape[0]
    cos_full, sin_signed = _rotary_tables(seq)
    log_g = jnp.asarray(np.log1p(-np.exp2(-5.0 - np.arange(RET_HEADS, dtype=np.float64))).astype(np.float32))
    ret_w = RET_HEADS * RET_DK
    x2 = x.reshape(n, d)
    for l in range(depth):
        mod3 = _adaln(c, w_ada[l], b_ada[l]).reshape(bsz, N_MOD, d)
        proj = _inproj(x2, mod3, g_mix[l].reshape(1, d), w_in[l], cos_full, sin_signed, seq)
        proj3 = proj.reshape(bsz, seq, IN_COLS)
        qg2 = jnp.tile(q_gain[l].reshape(1, MOBA_DH), (1, 2))
        kg2 = jnp.tile(k_gain[l].reshape(1, MOBA_DH), (1, 2))
        ret, moba = _mixers(log_g, proj3, qg2, kg2)
        xb, h2, e_idx, w_k, rank, cnt = _mid(
            ret.reshape(n, ret_w), moba.reshape(n, MOBA_HEADS * MOBA_DH), x2, mod3,
            w_out[l], g_ffn[l].reshape(1, d),
            w_router[l], router_bias[l].reshape(N_EXPERTS, 1),
            ws_gate[l], ws_up[l], ws_down[l], seq)
        nblk = n * TOP_K // FFN_BLOCK + N_EXPERTS
        tab = _sched(cnt, nblk)
        dest3 = _dest(tab, e_idx, rank)
        xs = _sc_dispatch(h2, dest3, nblk * FFN_BLOCK)
        y = _ffn(tab, xs, w_gate[l], w_up[l], w_down[l])
        yt = _sc_gather(y, dest3)
        x2 = _combine(yt, w_k, xb, mod3, seq)
    return x2.reshape(bsz, seq, d)
```

```python
import functools

import numpy as np
import jax
import jax.numpy as jnp
from jax import lax
from jax.experimental import pallas as pl
from jax.experimental.pallas import tpu as pltpu
from jax.experimental.pallas import tpu_sc as plsc

F32 = jnp.float32
BF16 = jnp.bfloat16
I32 = jnp.int32

RET_HEADS = 4
RET_DK = 128
MOBA_HEADS = 8
MOBA_DH = 64
MOBA_BLOCK = 256
MOBA_TOPK = 3
ROPE_BASE = 10000.0
N_EXPERTS = 256
TOP_K = 8
N_GROUPS = 8
TOPK_GROUPS = 4
GROUP_SIZE = N_EXPERTS // N_GROUPS
ROUTED_SCALE = 2.5
N_MOD = 6
EPS = 1e-6
IN_COLS = 3584

LANES = 128
BF16_TILE_ROWS = 16
RET_CHUNK = 256
TN_ADALN = 1024
TM_INPROJ = 1024
TM_MID = 1024
TN_DEST = 2048
TM_COMBINE = 512
COMBINE_ROWS = 32
MOBA_ONES_ROWS = BF16_TILE_ROWS
SC_CORES = 2
SC_SUBCORES = 16
SC_CHUNK = 128
SC_GATHER_ROWS = 64
SC_GATHER_BUFS = 3
FFN_BLOCK = 1024
FFN_GRAIN = 128
FFN_LOOKAHEAD = 5
FFN_WEIGHT_SLOTS = 3
FFN_OUT_SLOTS = 3
VMEM_LIMIT = 56 * 1024 * 1024

NEG_INF = float("-inf")
LOG2_E = 1.4426950408889634


def _silu(x):
    return x * jax.nn.sigmoid(x)


def _nt_dot(a, b):
    return lax.dot_general(a, b, (((1,), (1,)), ((), ())), preferred_element_type=F32)


def _tn_dot(a, b):
    return lax.dot_general(a, b, (((0,), (0,)), ((), ())), preferred_element_type=F32)


def _dot(a, b):
    return jnp.dot(a, b, preferred_element_type=F32)


HI_MASK = -65536


def _pack_halves(v):
    w = v.shape[1] // 2
    lo = lax.bitcast_convert_type(v[:, :w].astype(BF16).astype(F32), I32)
    hi = lax.bitcast_convert_type(v[:, w:].astype(BF16).astype(F32), I32)
    return lax.shift_right_logical(lo, 16) | (hi & HI_MASK)


def _unpack_halves(u):
    lo = lax.bitcast_convert_type(lax.shift_left(u, 16), F32)
    hi = lax.bitcast_convert_type(u & HI_MASK, F32)
    return lo, hi


def _adaln_kernel(c_ref, w_ref, b_ref, o_ref):
    s = _silu(c_ref[...])
    o_ref[...] = _dot(s.astype(BF16), w_ref[...].astype(BF16)) + b_ref[...]


def _adaln(c, w_ada, b_ada):
    bsz, d = c.shape
    ncol = w_ada.shape[1]
    tn = TN_ADALN
    return pl.pallas_call(
        _adaln_kernel,
        out_shape=jax.ShapeDtypeStruct((bsz, ncol), F32),
        grid=(ncol // tn,),
        in_specs=[
            pl.BlockSpec((bsz, d), lambda j: (0, 0)),
            pl.BlockSpec((d, tn), lambda j: (0, j)),
            pl.BlockSpec((1, tn), lambda j: (0, j)),
        ],
        out_specs=pl.BlockSpec((bsz, tn), lambda j: (0, j)),
        compiler_params=pltpu.CompilerParams(vmem_limit_bytes=VMEM_LIMIT),
        name="adaln",
    )(c, w_ada, b_ada.reshape(1, ncol))


def _inproj_kernel(x_ref, mod_ref, g_ref, w_ref, cos_ref, sin_ref, o_ref):
    x = x_ref[...]
    ms = jnp.mean(x * x, axis=-1, keepdims=True)
    m = mod_ref[0]
    h = (x * lax.rsqrt(ms + EPS) * g_ref[...]) * (1.0 + m[1:2]) + m[0:1]
    hb = h.astype(BF16)
    cosf = cos_ref[...]
    sinf = sin_ref[...]
    k_scale = RET_DK ** -0.5
    width = RET_HEADS * RET_DK
    for ci in range(IN_COLS // width):
        acc = _dot(hb, w_ref[:, ci * width:(ci + 1) * width])
        if ci < 2:
            for hh in range(RET_HEADS):
                xh = acc[:, hh * RET_DK:(hh + 1) * RET_DK]
                r = xh * cosf + pltpu.roll(xh, RET_DK // 2, axis=1) * sinf
                if ci == 1:
                    r = r * k_scale
                o_ref[:, ci * width + hh * RET_DK:ci * width + (hh + 1) * RET_DK] = r.astype(BF16)
        else:
            o_ref[:, ci * width:(ci + 1) * width] = acc.astype(BF16)


def _inproj(x2, mod3, g_mix, w_in, cos_full, sin_signed, seq):
    n, d = x2.shape
    tm = TM_INPROJ
    tiles_per_seq = seq // tm
    return pl.pallas_call(
        _inproj_kernel,
        out_shape=jax.ShapeDtypeStruct((n, IN_COLS), BF16),
        grid=(n // tm,),
        in_specs=[
            pl.BlockSpec((tm, d), lambda i: (i, 0)),
            pl.BlockSpec((1, N_MOD, d), lambda i: (i // tiles_per_seq, 0, 0)),
            pl.BlockSpec((1, d), lambda i: (0, 0)),
            pl.BlockSpec((d, IN_COLS), lambda i: (0, 0), pipeline_mode=pl.Buffered(1)),
            pl.BlockSpec((tm, LANES), lambda i: (i % tiles_per_seq, 0)),
            pl.BlockSpec((tm, LANES), lambda i: (i % tiles_per_seq, 0)),
        ],
        out_specs=pl.BlockSpec((tm, IN_COLS), lambda i: (i, 0)),
        compiler_params=pltpu.CompilerParams(vmem_limit_bytes=VMEM_LIMIT),
        name="inproj",
    )(x2, mod3, g_mix, w_in, cos_full, sin_signed)


def _ret_kernel(lg_ref, q_ref, k_ref, v_ref, g_ref, o_ref):
    seq = q_ref.shape[1]
    c = RET_CHUNK
    lg = lg_ref[pl.program_id(1)]
    row = lax.broadcasted_iota(I32, (c, c), 0)
    col = lax.broadcasted_iota(I32, (c, c), 1)
    diff = (row - col).astype(F32)
    dmask = jnp.where(diff >= 0, jnp.exp(lg * jnp.maximum(diff, 0.0)), 0.0)
    idx = lax.broadcasted_iota(I32, (c, 1), 0).astype(F32)
    q_decay = jnp.exp(lg * (idx + 1.0))
    k_decay = jnp.exp(lg * (c - 1.0 - idx))
    chunk_decay = jnp.exp(jnp.full((1, 1), lg * c, F32))
    state = jnp.zeros((RET_DK, RET_DK), F32)
    for n in range(seq // c):
        rows = slice(n * c, (n + 1) * c)
        qn = q_ref[0, rows, :]
        kn = k_ref[0, rows, :]
        vn = v_ref[0, rows, :]
        scores = _nt_dot(qn, kn) * dmask
        inner = _dot(scores.astype(BF16), vn)
        qs = (qn.astype(F32) * q_decay).astype(BF16)
        cross = _dot(qs, state.astype(BF16))
        o = inner + cross
        o = o * lax.rsqrt(jnp.mean(o * o, axis=-1, keepdims=True) + EPS)
        gn = g_ref[0, rows, :].astype(F32)
        o_ref[0, rows, :] = (_silu(gn) * o).astype(BF16)
        ks = (kn.astype(F32) * k_decay).astype(BF16)
        state = state * chunk_decay + _tn_dot(ks, vn)


def _moba_kernel(q_ref, k_ref, v_ref, qg_ref, kg_ref, o_ref, qt_s, ka_s, kb_s, vta_s, vtb_s):
    seq = q_ref.shape[1]
    lb = MOBA_BLOCK
    nb = seq // lb
    lane = lax.broadcasted_iota(I32, (1, LANES), 1)
    is_a = lane < MOBA_DH

    def head_norm(xf, gain):
        sq = xf * xf
        s_a = jnp.sum(jnp.where(is_a, sq, 0.0), axis=-1, keepdims=True)
        s_b = jnp.sum(jnp.where(is_a, 0.0, sq), axis=-1, keepdims=True)
        inv = jnp.where(is_a, lax.rsqrt(s_a / MOBA_DH + EPS), lax.rsqrt(s_b / MOBA_DH + EPS))
        return xf * inv * gain

    qg = qg_ref[...]
    kg = kg_ref[...]
    k_means = []
    for j in range(nb):
        rows = slice(j * lb, (j + 1) * lb)
        kf = head_norm(k_ref[0, rows, :].astype(F32), kg)
        ka_s[rows, :] = jnp.where(is_a, kf, 0.0).astype(BF16)
        kb_s[rows, :] = jnp.where(is_a, 0.0, kf).astype(BF16)
        k_means.append(jnp.mean(kf, axis=0, keepdims=True))
        qf = head_norm(q_ref[0, rows, :].astype(F32), qg)
        qt_s[:, rows] = (qf * (MOBA_DH ** -0.5 * LOG2_E)).T.astype(BF16)
        vt = v_ref[0, rows, :].T
        ones = jnp.ones((MOBA_ONES_ROWS, lb), BF16)
        vta_s[:, rows] = jnp.concatenate([vt[:MOBA_DH], ones], axis=0)
        vtb_s[:, rows] = jnp.concatenate([vt[MOBA_DH:], ones], axis=0)
    k_mean = jnp.concatenate(k_means + [jnp.zeros((BF16_TILE_ROWS - nb, LANES), F32)], axis=0)
    k_mean_h = (jnp.where(is_a, k_mean, 0.0).astype(BF16), jnp.where(is_a, 0.0, k_mean).astype(BF16))
    k_s = (ka_s, kb_s)
    vt_s = (vta_s, vtb_s)

    r_loc = lax.broadcasted_iota(I32, (lb, lb), 0)
    c_loc = lax.broadcasted_iota(I32, (lb, lb), 1)
    causal = r_loc <= c_loc

    def logits(i, hx):
        qt = qt_s[:, i * lb:(i + 1) * lb]
        bias = [None] * i
        if i > MOBA_TOPK:
            gate = _dot(k_mean_h[hx], qt)
            g = [gate[j:j + 1, :] for j in range(i)]
            for j in range(i):
                rank = jnp.zeros((1, lb), F32)
                for j2 in range(i):
                    if j2 == j:
                        continue
                    beats = (g[j2] >= g[j]) if j2 < j else (g[j2] > g[j])
                    rank = rank + jnp.where(beats, 1.0, 0.0)
                bias[j] = jnp.where(rank < float(MOBA_TOPK), 0.0, NEG_INF).astype(BF16)
        pieces = []
        for j in range(i + 1):
            s = _dot(k_s[hx][j * lb:(j + 1) * lb, :], qt).astype(BF16)
            if j == i:
                s = jnp.where(causal, s, jnp.asarray(NEG_INF, BF16))
            elif bias[j] is not None:
                s = s + bias[j]
            pieces.append(s)
        mx = jnp.max(pieces[0], axis=0, keepdims=True)
        for s in pieces[1:]:
            mx = jnp.maximum(mx, jnp.max(s, axis=0, keepdims=True))
        return pieces, mx

    def attend(hx, pieces, mx):
        acc = jnp.zeros((MOBA_DH + MOBA_ONES_ROWS, lb), F32)
        for j, s in enumerate(pieces):
            acc = acc + _dot(vt_s[hx][:, j * lb:(j + 1) * lb], jnp.exp2(s - mx))
        return acc[:MOBA_DH] / acc[MOBA_DH:MOBA_DH + 1, :]

    groups = [(i, hx) for i in range(nb) for hx in range(2)]
    outs = {}
    pending = logits(*groups[0])
    for n, (i, hx) in enumerate(groups):
        ahead = logits(*groups[n + 1]) if n + 1 < len(groups) else None
        outs[(i, hx)] = attend(hx, *pending)
        pending = ahead
        if hx == 1:
            o_ref[0, i * lb:(i + 1) * lb, :] = jnp.concatenate(
                [outs.pop((i, 0)), outs.pop((i, 1))], axis=0).T.astype(BF16)


def _mixers_kernel(lg_ref, rq_ref, rk_ref, rv_ref, rg_ref, mq_ref, mk_ref, mv_ref, qg_ref, kg_ref,
                   ret_ref, moba_ref, *moba_scratch):
    _ret_kernel(lg_ref, rq_ref, rk_ref, rv_ref, rg_ref, ret_ref)
    _moba_kernel(mq_ref, mk_ref, mv_ref, qg_ref, kg_ref, moba_ref, *moba_scratch)


def _mixers(log_g, proj3, qg2, kg2):
    bsz, seq, _ = proj3.shape
    pairs = MOBA_HEADS // 2
    assert pairs == RET_HEADS and RET_DK == LANES
    blk = (1, seq, LANES)
    base = 4 * RET_HEADS

    def col(first):
        return pl.BlockSpec(blk, lambda b, p, lg: (b, 0, first + p))

    return pl.pallas_call(
        _mixers_kernel,
        out_shape=(jax.ShapeDtypeStruct((bsz, seq, RET_HEADS * RET_DK), BF16),
                   jax.ShapeDtypeStruct((bsz, seq, MOBA_HEADS * MOBA_DH), BF16)),
        grid_spec=pltpu.PrefetchScalarGridSpec(
            num_scalar_prefetch=1,
            grid=(bsz, pairs),
            in_specs=[
                col(0), col(RET_HEADS), col(2 * RET_HEADS), col(3 * RET_HEADS),
                col(base), col(base + pairs), col(base + 2 * pairs),
                pl.BlockSpec((1, LANES), lambda b, p, lg: (0, 0)),
                pl.BlockSpec((1, LANES), lambda b, p, lg: (0, 0)),
            ],
            out_specs=(col(0), col(0)),
            scratch_shapes=[
                pltpu.VMEM((LANES, seq), BF16),
                pltpu.VMEM((seq, LANES), BF16),
                pltpu.VMEM((seq, LANES), BF16),
                pltpu.VMEM((MOBA_DH + MOBA_ONES_ROWS, seq), BF16),
                pltpu.VMEM((MOBA_DH + MOBA_ONES_ROWS, seq), BF16),
            ],
        ),
        compiler_params=pltpu.CompilerParams(vmem_limit_bytes=VMEM_LIMIT),
        name="mixers",
    )(log_g, *([proj3] * 7), qg2, kg2)


def _mid_kernel(ret_ref, moba_ref, x_ref, mod_ref, wo_ref, g_ref, wr_ref, rb_ref,
                wsg_ref, wsu_ref, wsd_ref,
                xb_ref, h2_ref, e_ref, w_ref, rk_ref, cnt_ref,
                carry_s, wo_s, wr_s, wsg_s, wsu_s, wsd_s, upper_s):
    i = pl.program_id(0)
    tm = x_ref.shape[0]
    half = ret_ref.shape[1]

    @pl.when(i == 0)
    def _():
        carry_s[...] = jnp.zeros_like(carry_s)
        wo_s[...] = wo_ref[...].astype(BF16)
        wsg_s[...] = wsg_ref[...].astype(BF16)
        wsu_s[...] = wsu_ref[...].astype(BF16)
        wsd_s[...] = wsd_ref[...].astype(BF16)
        wr_s[...] = wr_ref[...].T.astype(BF16)
        tr = lax.broadcasted_iota(I32, (tm, tm), 0)
        tc = lax.broadcasted_iota(I32, (tm, tm), 1)
        upper_s[...] = jnp.where(tr < tc, 1.0, 0.0).astype(BF16)

    m = mod_ref[0]
    mixed = _dot(ret_ref[...], wo_s[:half, :]) + _dot(moba_ref[...], wo_s[half:, :])
    x1 = x_ref[...] + m[2:3] * mixed
    ms = jnp.mean(x1 * x1, axis=-1, keepdims=True)
    h2 = (x1 * lax.rsqrt(ms + EPS) * g_ref[...]) * (1.0 + m[4:5]) + m[3:4]
    h2_ref[...] = _pack_halves(h2)
    h2b = h2.astype(BF16)

    hid = _silu(_dot(h2b, wsg_s[...])) * _dot(h2b, wsu_s[...])
    xb_ref[...] = x1 + m[5:6] * _dot(hid.astype(BF16), wsd_s[...])

    scores = jax.nn.sigmoid(_nt_dot(wr_s[...], h2b))
    biased = scores + rb_ref[...]
    grp = biased.reshape(N_GROUPS, GROUP_SIZE, tm)
    gi = lax.broadcasted_iota(I32, (N_GROUPS, GROUP_SIZE, tm), 1).astype(F32)
    top1 = jnp.max(grp, axis=1, keepdims=True)
    first = jnp.min(jnp.where(grp == top1, gi, float(GROUP_SIZE)), axis=1, keepdims=True)
    top2 = jnp.max(jnp.where(gi == first, NEG_INF, grp), axis=1, keepdims=True)
    gscore = (top1 + top2).reshape(N_GROUPS, tm)
    gidx = lax.broadcasted_iota(I32, (N_GROUPS, tm), 0)
    grank = jnp.zeros((N_GROUPS, tm), F32)
    for g2 in range(N_GROUPS):
        rowv = gscore[g2:g2 + 1, :]
        beats = (rowv > gscore) | ((rowv == gscore) & (g2 < gidx))
        grank = grank + jnp.where(beats, 1.0, 0.0)
    gsel = jnp.where(grank < float(TOPK_GROUPS), 1.0, 0.0)
    emask = jnp.broadcast_to(gsel.reshape(N_GROUPS, 1, tm), (N_GROUPS, GROUP_SIZE, tm)).reshape(N_EXPERTS, tm)
    choice = jnp.where(emask > 0.5, biased, NEG_INF)

    eidx = lax.broadcasted_iota(I32, (N_EXPERTS, tm), 0).astype(F32)
    remaining = choice
    e_rows = []
    for _k in range(TOP_K):
        mx = jnp.max(remaining, axis=0, keepdims=True)
        idx = jnp.min(jnp.where(remaining == mx, eidx, float(N_EXPERTS)), axis=0, keepdims=True)
        e_rows.append(idx)
        remaining = jnp.where(eidx == idx, NEG_INF, remaining)
    selmask = jnp.where(remaining != choice, 1.0, 0.0)

    prefix = _dot(selmask.astype(BF16), upper_s[...]) + carry_s[...]
    w_rows = []
    r_rows = []
    for ek in e_rows:
        onehot = eidx == ek
        w_rows.append(jnp.sum(jnp.where(onehot, scores, 0.0), axis=0, keepdims=True))
        r_rows.append(jnp.sum(jnp.where(onehot, prefix, 0.0), axis=0, keepdims=True))
    wsum = w_rows[0]
    for wk in w_rows[1:]:
        wsum = wsum + wk
    carry_s[...] = carry_s[...] + jnp.sum(selmask, axis=1, keepdims=True)

    e_ref[...] = jnp.concatenate(e_rows, axis=0).astype(I32)
    w_ref[...] = jnp.concatenate([wk / wsum * ROUTED_SCALE for wk in w_rows], axis=0)
    rk_ref[...] = jnp.concatenate(r_rows, axis=0).astype(I32)
    cnt_ref[...] = carry_s[...].astype(I32)


def _mid(ret2, moba2, x2, mod3, wo, g_ffn, w_router, rbias, wsg, wsu, wsd, seq):
    n, d = x2.shape
    tm = TM_MID
    tiles_per_seq = seq // tm
    half = ret2.shape[1]
    ff = wsg.shape[1]
    const = lambda i: (0, 0)
    row = lambda i: (i, 0)
    colt = lambda i: (0, i)

    def resident(shape):
        return pl.BlockSpec(shape, const, pipeline_mode=pl.Buffered(1))

    return pl.pallas_call(
        _mid_kernel,
        out_shape=(
            jax.ShapeDtypeStruct((n, d), F32),
            jax.ShapeDtypeStruct((n, d // 2), I32),
            jax.ShapeDtypeStruct((TOP_K, n), I32),
            jax.ShapeDtypeStruct((TOP_K, n), F32),
            jax.ShapeDtypeStruct((TOP_K, n), I32),
            jax.ShapeDtypeStruct((N_EXPERTS, 1), I32),
        ),
        grid=(n // tm,),
        in_specs=[
            pl.BlockSpec((tm, half), row),
            pl.BlockSpec((tm, half), row),
            pl.BlockSpec((tm, d), row),
            pl.BlockSpec((1, N_MOD, d), lambda i: (i // tiles_per_seq, 0, 0)),
            resident(wo.shape),
            pl.BlockSpec((1, d), const),
            resident((d, N_EXPERTS)),
            pl.BlockSpec((N_EXPERTS, 1), const),
            resident((d, ff)),
            resident((d, ff)),
            resident((ff, d)),
        ],
        out_specs=(
            pl.BlockSpec((tm, d), row),
            pl.BlockSpec((tm, d // 2), row),
            pl.BlockSpec((TOP_K, tm), colt),
            pl.BlockSpec((TOP_K, tm), colt),
            pl.BlockSpec((TOP_K, tm), colt),
            pl.BlockSpec((N_EXPERTS, 1), const),
        ),
        scratch_shapes=[
            pltpu.VMEM((N_EXPERTS, 1), F32),
            pltpu.VMEM(wo.shape, BF16),
            pltpu.VMEM((N_EXPERTS, d), BF16),
            pltpu.VMEM((d, ff), BF16),
            pltpu.VMEM((d, ff), BF16),
            pltpu.VMEM((ff, d), BF16),
            pltpu.VMEM((tm, tm), BF16),
        ],
        compiler_params=pltpu.CompilerParams(
            dimension_semantics=("arbitrary",), vmem_limit_bytes=VMEM_LIMIT),
        name="mid",
    )(ret2, moba2, x2, mod3, wo, g_ffn, w_router, rbias, wsg, wsu, wsd)


ROW_EXP = 0
ROW_VALID = 1
ROW_FIRST = 2
ROW_AHEAD = 3
ROW_SLOT = 4
ROW_HEAD = 5
ROW_START = 6
SCHED_ROWS = 8


def _sched_kernel(cnt_ref, tab_ref):
    ne = N_EXPERTS
    nblk = tab_ref.shape[1]
    shift = FFN_BLOCK.bit_length() - 1
    e_sub = lax.broadcasted_iota(I32, (ne, ne), 0)
    e_lane = lax.broadcasted_iota(I32, (ne, ne), 1)
    e_col = lax.broadcasted_iota(I32, (ne, 1), 0).astype(F32)
    ids_row = lax.broadcasted_iota(I32, (1, ne), 1).astype(F32) + 1.0

    def to_row(col):
        return jnp.sum(jnp.where(e_sub == e_lane, col, 0.0), axis=0, keepdims=True)

    def running_col(row):
        return jnp.sum(jnp.where(e_lane <= e_sub, row, 0.0), axis=1, keepdims=True)

    def running_row(col):
        return jnp.sum(jnp.where(e_sub <= e_lane, col, 0.0), axis=0, keepdims=True)

    cnt_i = cnt_ref[...]
    cnt_col = cnt_i.astype(F32)
    pad_col = lax.shift_left(lax.shift_right_logical(cnt_i + (FFN_BLOCK - 1), shift), shift).astype(F32)
    pad_row = to_row(pad_col)
    ends_col = running_col(pad_row)
    start_col = ends_col - pad_col
    start_row = running_row(pad_col) - pad_row
    nreal = jnp.sum(pad_row, axis=1, keepdims=True) * (1.0 / FFN_BLOCK)

    step = lax.broadcasted_iota(I32, (1, nblk), 1)
    g = step.astype(F32)
    row0 = jnp.minimum(g, nreal - 1.0) * float(FFN_BLOCK)
    exp_g = jnp.sum(jnp.where(ends_col <= row0, 1.0, 0.0), axis=0, keepdims=True)
    mine = e_col == exp_g

    def per_block(col):
        return jnp.sum(jnp.where(mine, col, 0.0), axis=0, keepdims=True)

    valid = jnp.clip(per_block(cnt_col + start_col) - row0, 0.0, float(FFN_BLOCK))
    first = jnp.where(jnp.logical_and(g < nreal, per_block(start_col) == row0), 1.0, 0.0)

    used_col = jnp.where(cnt_col > 0.0, 1.0, 0.0)
    used_row = to_row(used_col)
    ord_col = running_col(used_row) - 1.0
    ord_row = running_row(used_col) - 1.0

    def used_at(pos):
        hit = jnp.logical_and(used_row > 0.0, ord_row == pos)
        return jnp.sum(jnp.where(hit, ids_row, 0.0), axis=1, keepdims=True) - 1.0

    ahead_col = used_at(ord_col + float(FFN_WEIGHT_SLOTS - 1))
    slot_col = ord_col - FFN_WEIGHT_SLOTS * jnp.floor((ord_col + 0.5) * (1.0 / FFN_WEIGHT_SLOTS))
    head = jnp.where(step == 0, nreal, 0.0)
    for j in range(FFN_WEIGHT_SLOTS - 1):
        head = head + jnp.where(step == 1 + j, used_at(jnp.full((1, 1), float(j), F32)), 0.0)
    rows = [exp_g, valid, first, per_block(ahead_col), per_block(slot_col), head,
            jnp.concatenate([start_row, jnp.zeros((1, nblk - ne), F32)], axis=1),
            jnp.zeros((1, nblk), F32)]
    tab_ref[...] = jnp.concatenate(rows, axis=0).astype(I32)


def _sched(cnt, nblk):
    assert FFN_BLOCK & (FFN_BLOCK - 1) == 0 and nblk >= N_EXPERTS
    return pl.pallas_call(
        _sched_kernel,
        out_shape=jax.ShapeDtypeStruct((SCHED_ROWS, nblk), I32),
        compiler_params=pltpu.CompilerParams(vmem_limit_bytes=VMEM_LIMIT),
        name="sched",
    )(cnt)


def _dest_kernel(tab_ref, e_ref, rk_ref, o_ref):
    k = e_ref.shape[0]
    segments = [jnp.broadcast_to(tab_ref[ROW_START:ROW_START + 1, s * LANES:(s + 1) * LANES], (k, LANES))
                for s in range(N_EXPERTS // LANES)]
    for ch in range(o_ref.shape[0]):
        cols = slice(ch * SC_CHUNK, (ch + 1) * SC_CHUNK)
        e = e_ref[:, cols]
        lane = e & (LANES - 1)
        start = jnp.take_along_axis(segments[0], lane, axis=1)
        for s in range(1, len(segments)):
            start = jnp.where(e >= s * LANES, jnp.take_along_axis(segments[s], lane, axis=1), start)
        o_ref[ch] = start + rk_ref[:, cols]


def _dest(tab, e_idx, rank):
    k, n = e_idx.shape
    tn = TN_DEST
    assert SC_CHUNK == LANES and N_EXPERTS % LANES == 0
    return pl.pallas_call(
        _dest_kernel,
        out_shape=jax.ShapeDtypeStruct((n // SC_CHUNK, k, SC_CHUNK), I32),
        grid=(n // tn,),
        in_specs=[
            pl.BlockSpec(tab.shape, lambda i: (0, 0)),
            pl.BlockSpec((k, tn), lambda i: (0, i)),
            pl.BlockSpec((k, tn), lambda i: (0, i)),
        ],
        out_specs=pl.BlockSpec((tn // SC_CHUNK, k, SC_CHUNK), lambda i: (i, 0, 0)),
        compiler_params=pltpu.CompilerParams(vmem_limit_bytes=VMEM_LIMIT),
        name="dest",
    )(tab, e_idx, rank)


def _sc_dispatch(h2p, dest3, total_rows):
    n, words = h2p.shape
    nchunks = n // SC_CHUNK
    per_worker = nchunks // (SC_CORES * SC_SUBCORES)
    mesh = plsc.VectorSubcoreMesh(core_axis_name="c", subcore_axis_name="s",
                                  num_cores=SC_CORES, num_subcores=SC_SUBCORES)

    @functools.partial(
        pl.kernel, mesh=mesh,
        out_type=jax.ShapeDtypeStruct((total_rows, words), I32),
        scratch_types=[
            pltpu.VMEM((TOP_K, SC_CHUNK), I32),
            pltpu.VMEM((SC_CHUNK, words), I32),
            pltpu.SemaphoreType.DMA,
        ],
        name="sc_dispatch",
    )
    def run(h_hbm, d_hbm, xs_hbm, idx_v, rows_v, sem):
        wid = lax.axis_index("s") * SC_CORES + lax.axis_index("c")

        @pl.loop(0, per_worker)
        def _(j):
            ch = wid * per_worker + j
            pltpu.sync_copy(d_hbm.at[ch], idx_v)
            pltpu.sync_copy(h_hbm.at[pl.ds(ch * SC_CHUNK, SC_CHUNK)], rows_v)
            copies = [pltpu.async_copy(rows_v, xs_hbm.at[idx_v.at[k]], sem) for k in range(TOP_K)]
            for cp in copies:
                cp.wait()

    return run(h2p, dest3)


def _ffn_kernel(tab_ref, x_hbm, wg_hbm, wu_hbm, wd_hbm, y_hbm,
                x_s, y_s, wg_s, wu_s, wd_s, sem_x, sem_y, sem):
    i = pl.program_id(0)
    nreal = tab_ref[ROW_HEAD, 0]
    rows_per = x_s.shape[1]
    half = x_s.shape[2]
    sizes = tuple(range(FFN_GRAIN, rows_per + 1, FFN_GRAIN))

    def rows_needed(g):
        return (tab_ref[ROW_VALID, g] + (FFN_GRAIN - 1)) // FFN_GRAIN * FFN_GRAIN

    def by_size(nrows, fn):
        for n in sizes:
            @pl.when(nrows == n)
            def _():
                fn(n)

    def row_copy(g, n):
        slot = lax.rem(g, FFN_LOOKAHEAD + 1)
        return pltpu.make_async_copy(
            x_hbm.at[pl.ds(g * rows_per, n)], x_s.at[slot, pl.ds(0, n)], sem_x.at[slot])

    def out_copy(g, n):
        slot = lax.rem(g, FFN_OUT_SLOTS)
        return pltpu.make_async_copy(
            y_s.at[slot, pl.ds(0, n)], y_hbm.at[pl.ds(g * rows_per, n)], sem_y.at[slot])

    def weight_copies(e, s):
        return (pltpu.make_async_copy(wg_hbm.at[e], wg_s.at[s], sem.at[s, 0]),
                pltpu.make_async_copy(wu_hbm.at[e], wu_s.at[s], sem.at[s, 1]),
                pltpu.make_async_copy(wd_hbm.at[e], wd_s.at[s], sem.at[s, 2]))

    @pl.when(i == 0)
    def _():
        for j in range(FFN_WEIGHT_SLOTS - 1):
            @pl.when(tab_ref[ROW_HEAD, 1 + j] >= 0)
            def _():
                for cp in weight_copies(tab_ref[ROW_HEAD, 1 + j], j):
                    cp.start()
        for g in range(FFN_LOOKAHEAD):
            @pl.when(g < nreal)
            def _():
                by_size(rows_needed(g), lambda n: row_copy(g, n).start())

    @pl.when(i < nreal)
    def _():
        s = tab_ref[ROW_SLOT, i]
        fetch = i + FFN_LOOKAHEAD

        @pl.when(fetch < nreal)
        def _():
            by_size(rows_needed(fetch), lambda n: row_copy(fetch, n).start())

        by_size(rows_needed(i), lambda n: row_copy(i, n).wait())

        @pl.when(tab_ref[ROW_FIRST, i] == 1)
        def _():
            for cp in weight_copies(tab_ref[ROW_EXP, i], s):
                cp.wait()

            ahead = tab_ref[ROW_AHEAD, i]

            @pl.when(ahead >= 0)
            def _():
                for cp in weight_copies(ahead, lax.rem(s + FFN_WEIGHT_SLOTS - 1, FFN_WEIGHT_SLOTS)):
                    cp.start()

        @pl.when(i >= FFN_OUT_SLOTS)
        def _():
            done = i - FFN_OUT_SLOTS
            by_size(rows_needed(done), lambda n: out_copy(done, n).wait())

        x_slot = lax.rem(i, FFN_LOOKAHEAD + 1)
        y_slot = lax.rem(i, FFN_OUT_SLOTS)
        valid = tab_ref[ROW_VALID, i]

        def expert_rows(n):
            r = lax.broadcasted_iota(I32, (n, 1), 0)
            x_lo, x_hi = _unpack_halves(jnp.where(r < valid, x_s[x_slot, pl.ds(0, n), :], 0))
            hg = _dot(x_lo, wg_s[s, :half, :]) + _dot(x_hi, wg_s[s, half:, :])
            hu = _dot(x_lo, wu_s[s, :half, :]) + _dot(x_hi, wu_s[s, half:, :])
            y_s[y_slot, pl.ds(0, n), :] = _pack_halves(_dot(_silu(hg) * hu, wd_s[s]))
            out_copy(i, n).start()

        by_size(rows_needed(i), expert_rows)

        @pl.when(i == nreal - 1)
        def _():
            for back in range(FFN_OUT_SLOTS):
                last = i - back

                @pl.when(last >= 0)
                def _():
                    by_size(rows_needed(last), lambda n: out_copy(last, n).wait())


def _ffn(tab, xs, w_gate, w_up, w_down):
    p, half = xs.shape
    d = 2 * half
    ff = w_gate.shape[2]
    return pl.pallas_call(
        _ffn_kernel,
        out_shape=jax.ShapeDtypeStruct((p, half), I32),
        grid_spec=pltpu.PrefetchScalarGridSpec(
            num_scalar_prefetch=1,
            grid=(tab.shape[1],),
            in_specs=[
                pl.BlockSpec(memory_space=pl.ANY),
                pl.BlockSpec(memory_space=pl.ANY),
                pl.BlockSpec(memory_space=pl.ANY),
                pl.BlockSpec(memory_space=pl.ANY),
            ],
            out_specs=pl.BlockSpec(memory_space=pl.ANY),
            scratch_shapes=[
                pltpu.VMEM((FFN_LOOKAHEAD + 1, FFN_BLOCK, half), I32),
                pltpu.VMEM((FFN_OUT_SLOTS, FFN_BLOCK, half), I32),
                pltpu.VMEM((FFN_WEIGHT_SLOTS, d, ff), F32),
                pltpu.VMEM((FFN_WEIGHT_SLOTS, d, ff), F32),
                pltpu.VMEM((FFN_WEIGHT_SLOTS, ff, d), F32),
                pltpu.SemaphoreType.DMA((FFN_LOOKAHEAD + 1,)),
                pltpu.SemaphoreType.DMA((FFN_OUT_SLOTS,)),
                pltpu.SemaphoreType.DMA((FFN_WEIGHT_SLOTS, 3)),
            ],
        ),
        compiler_params=pltpu.CompilerParams(
            dimension_semantics=("arbitrary",), vmem_limit_bytes=VMEM_LIMIT, has_side_effects=True),
        name="ffn",
    )(tab, xs, w_gate, w_up, w_down)


def _sc_gather(y, dest3):
    a, words = y.shape
    nchunks, _, chunk = dest3.shape
    n = nchunks * chunk
    per_worker = nchunks // (SC_CORES * SC_SUBCORES)
    nbuf = SC_GATHER_BUFS
    parts = chunk // SC_GATHER_ROWS
    items = [(c, k, h) for c in range(per_worker) for k in range(TOP_K) for h in range(parts)]
    mesh = plsc.VectorSubcoreMesh(core_axis_name="c", subcore_axis_name="s",
                                  num_cores=SC_CORES, num_subcores=SC_SUBCORES)

    @functools.partial(
        pl.kernel, mesh=mesh,
        out_type=jax.ShapeDtypeStruct((TOP_K, n, words), I32),
        scratch_types=[
            pltpu.VMEM((per_worker, TOP_K, chunk), I32),
            [pltpu.VMEM((SC_GATHER_ROWS, words), I32)] * nbuf,
            pltpu.SemaphoreType.DMA((nbuf,)),
            pltpu.SemaphoreType.DMA((nbuf,)),
        ],
        name="sc_gather",
    )
    def run(y_hbm, d_hbm, yt_hbm, idx_v, bufs, sem_g, sem_w):
        wid = lax.axis_index("s") * SC_CORES + lax.axis_index("c")
        pltpu.sync_copy(d_hbm.at[pl.ds(wid * per_worker, per_worker)], idx_v)

        def gather(m):
            c, k, h = items[m]
            idx = idx_v.at[c, k, pl.ds(h * SC_GATHER_ROWS, SC_GATHER_ROWS)]
            return pltpu.async_copy(y_hbm.at[idx], bufs[m % nbuf], sem_g.at[m % nbuf])

        def write(m):
            c, k, h = items[m]
            rows = pl.ds((wid * per_worker + c) * chunk + h * SC_GATHER_ROWS, SC_GATHER_ROWS)
            return pltpu.async_copy(bufs[m % nbuf], yt_hbm.at[k, rows], sem_w.at[m % nbuf])

        gathers = {m: gather(m) for m in range(min(nbuf - 1, len(items)))}
        writes = {}
        for m in range(len(items)):
            gathers.pop(m).wait()
            writes[m] = write(m)
            nxt = m + nbuf - 1
            if nxt < len(items):
                if m >= 1:
                    writes.pop(m - 1).wait()
                gathers[nxt] = gather(nxt)
        for m in sorted(writes):
            writes.pop(m).wait()

    return run(y, dest3)


def _combine_kernel(yt_ref, wt_ref, xb_ref, mod_ref, o_ref):
    half = yt_ref.shape[2]
    wt = wt_ref[...].T
    gate = mod_ref[0][5:6]
    for r0 in range(0, yt_ref.shape[1], COMBINE_ROWS):
        rows = slice(r0, r0 + COMBINE_ROWS)
        w = wt[rows]
        lo, hi = _unpack_halves(yt_ref[0, rows, :])
        r_lo = lo * w[:, 0:1]
        r_hi = hi * w[:, 0:1]
        for k in range(1, TOP_K):
            lo, hi = _unpack_halves(yt_ref[k, rows, :])
            r_lo = r_lo + lo * w[:, k:k + 1]
            r_hi = r_hi + hi * w[:, k:k + 1]
        o_ref[rows, :half] = xb_ref[rows, :half] + gate[:, :half] * r_lo
        o_ref[rows, half:] = xb_ref[rows, half:] + gate[:, half:] * r_hi


def _combine(yt, w_k, xb, mod3, seq):
    n, d = xb.shape
    tm = TM_COMBINE
    tiles_per_seq = seq // tm
    return pl.pallas_call(
        _combine_kernel,
        out_shape=jax.ShapeDtypeStruct((n, d), F32),
        grid=(n // tm,),
        in_specs=[
            pl.BlockSpec((TOP_K, tm, d // 2), lambda i: (0, i, 0)),
            pl.BlockSpec((TOP_K, tm), lambda i: (0, i)),
            pl.BlockSpec((tm, d), lambda i: (i, 0)),
            pl.BlockSpec((1, N_MOD, d), lambda i: (i // tiles_per_seq, 0, 0)),
        ],
        out_specs=pl.BlockSpec((tm, d), lambda i: (i, 0)),
        compiler_params=pltpu.CompilerParams(vmem_limit_bytes=VMEM_LIMIT),
        name="combine",
    )(yt, w_k, xb, mod3)


def _rotary_tables(seq):
    half = RET_DK // 2
    inv = ROPE_BASE ** (-np.arange(half, dtype=np.float64) / half)
    ang = np.arange(seq, dtype=np.float64)[:, None] * inv[None, :]
    cos = np.cos(ang).astype(np.float32)
    sin = np.sin(ang).astype(np.float32)
    return (jnp.asarray(np.concatenate([cos, cos], axis=-1)),
            jnp.asarray(np.concatenate([-sin, sin], axis=-1)))


def kernel(x, c, w_ada, b_ada, g_mix, w_in, q_gain, k_gain, w_out, g_ffn, w_router, router_bias,
           w_gate, w_up, w_down, ws_gate, ws_up, ws_down):
    bsz, seq, d = x.shape
    n = bsz * seq
    depth = w_ada.shape[0]
    cos_full, sin_signed = _rotary_tables(seq)
    log_g = jnp.asarray(np.log1p(-np.exp2(-5.0 - np.arange(RET_HEADS, dtype=np.float64))).astype(np.float32))
    ret_w = RET_HEADS * RET_DK
    x2 = x.reshape(n, d)
    for l in range(depth):
        mod3 = _adaln(c, w_ada[l], b_ada[l]).reshape(bsz, N_MOD, d)
        proj = _inproj(x2, mod3, g_mix[l].reshape(1, d), w_in[l], cos_full, sin_signed, seq)
        proj3 = proj.reshape(bsz, seq, IN_COLS)
        qg2 = jnp.tile(q_gain[l].reshape(1, MOBA_DH), (1, 2))
        kg2 = jnp.tile(k_gain[l].reshape(1, MOBA_DH), (1, 2))
        ret, moba = _mixers(log_g, proj3, qg2, kg2)
        xb, h2, e_idx, w_k, rank, cnt = _mid(
            ret.reshape(n, ret_w), moba.reshape(n, MOBA_HEADS * MOBA_DH), x2, mod3,
            w_out[l], g_ffn[l].reshape(1, d),
            w_router[l], router_bias[l].reshape(N_EXPERTS, 1),
            ws_gate[l], ws_up[l], ws_down[l], seq)
        nblk = n * TOP_K // FFN_BLOCK + N_EXPERTS
        tab = _sched(cnt, nblk)
        dest3 = _dest(tab, e_idx, rank)
        xs = _sc_dispatch(h2, dest3, nblk * FFN_BLOCK)
        y = _ffn(tab, xs, w_gate[l], w_up[l], w_down[l])
        yt = _sc_gather(y, dest3)
        x2 = _combine(yt, w_k, xb, mod3, seq)
    return x2.reshape(bsz, seq, d)
```
